```python
import jax, jax.numpy as jnp
from jax import lax
import numpy as np

D_MODEL = 2048
BATCH = 8
SEQ = 2048
DEPTH = 1

D_MIX = D_MODEL
D_POOL = D_MIX // 2
POOL_WINDOWS = (2, 4, 8, 16)
N_POOL_GROUPS = len(POOL_WINDOWS)
POOL_GROUP_DIM = D_POOL // N_POOL_GROUPS
D_GLA = D_MIX - D_POOL
GLA_HEADS = 4
GLA_DV = D_GLA // GLA_HEADS
GLA_DK_TOTAL = D_GLA // 2
GLA_DK = GLA_DK_TOTAL // GLA_HEADS
GLA_GATE_RANK = 16
GATE_LOGIT_NORMALIZER = 16.0
CHUNK = 64
D_IN = D_POOL + 2 * GLA_DK_TOTAL + 2 * D_GLA + GLA_GATE_RANK
D_FF = 5632
EPS = 1e-6

kernel_name = "hymba_pool_gla_macaron_block"


def rmsnorm(x, g):
    xf = x.astype(jnp.float32)
    y = xf * lax.rsqrt(jnp.mean(xf * xf, axis=-1, keepdims=True) + EPS)
    return (y * g.astype(jnp.float32)).astype(x.dtype)


def swiglu(h, w_in, w_out):
    gu = h @ w_in
    gate, up = gu[..., :D_FF], gu[..., D_FF:]
    return (jax.nn.silu(gate) * up) @ w_out


def pool_mixer(u, w_pool, pool_scale):
    B, S, _ = u.shape
    uf = u.astype(jnp.float32).reshape(B, S, N_POOL_GROUPS, POOL_GROUP_DIM)
    cs = jnp.cumsum(uf, axis=1)
    pos1 = jnp.arange(1, S + 1, dtype=jnp.int32)
    means = []
    for gi, w in enumerate(POOL_WINDOWS):
        c = cs[:, :, gi]
        shifted = jnp.pad(c, ((0, 0), (w, 0), (0, 0)))[:, :S]
        cnt = jnp.minimum(pos1, w).astype(jnp.float32)[None, :, None]
        means.append((c - shifted) / cnt)
    pooled = jnp.stack(means, axis=2) - uf
    y = jnp.einsum('bsgc,gcd->bsgd', pooled.astype(u.dtype), w_pool)
    return y.reshape(B, S, D_POOL) * pool_scale


def gla_mixer(q, k, v, g_out, gate_lr, w_alpha, b_alpha, gla_norm):
    B, S, _ = q.shape
    N = S // CHUNK
    log_alpha = jax.nn.log_sigmoid((gate_lr @ w_alpha + b_alpha).astype(jnp.float32)) / GATE_LOGIT_NORMALIZER

    def heads(t, d):
        return t.astype(jnp.float32).reshape(B, N, CHUNK, GLA_HEADS, d).transpose(0, 3, 1, 2, 4)

    qh = heads(q, GLA_DK) * (GLA_DK ** -0.5)
    kh = heads(k, GLA_DK)
    vh = heads(v, GLA_DV)
    bcum = jnp.cumsum(heads(log_alpha, GLA_DK), axis=3)
    b_last = bcum[:, :, :, -1:]
    q_dec = qh * jnp.exp(bcum)
    k_inv = kh * jnp.exp(-bcum)
    k_tail = kh * jnp.exp(b_last - bcum)

    mask = jnp.tril(jnp.ones((CHUNK, CHUNK), dtype=bool))
    scores = jnp.where(mask, jnp.einsum('bhnid,bhnjd->bhnij', q_dec, k_inv), 0.0)
    o_intra = jnp.einsum('bhnij,bhnjv->bhniv', scores, vh)

    kv_chunk = jnp.einsum('bhncd,bhncv->bhndv', k_tail, vh)
    decay_chunk = jnp.exp(b_last[:, :, :, 0])

    def step(state, inp):
        dec, kv = inp
        return state * dec[..., None] + kv, state

    init = jnp.zeros((B, GLA_HEADS, GLA_DK, GLA_DV), jnp.float32)
    _, states = lax.scan(step, init, (decay_chunk.transpose(2, 0, 1, 3), kv_chunk.transpose(2, 0, 1, 3, 4)))
    states = states.transpose(1, 2, 0, 3, 4)
    o = o_intra + jnp.einsum('bhncd,bhndv->bhncv', q_dec, states)

    o = o * lax.rsqrt(jnp.mean(o * o, axis=-1, keepdims=True) + EPS) * gla_norm.astype(jnp.float32)
    o = o.transpose(0, 2, 3, 1, 4).reshape(B, S, D_GLA)
    return (o * jax.nn.silu(g_out.astype(jnp.float32))).astype(q.dtype)


def _fwd_setup_inputs(seed: int = 0) -> dict:
    key = jax.random.key(seed)
    ks = jax.random.split(key, 16)
    f32 = jnp.float32

    def nrm(k, shape, fan_in):
        return jax.random.normal(k, shape, f32) * (fan_in ** -0.5)

    def gain(k, shape):
        return 1.0 + 0.02 * jax.random.normal(k, shape, f32)

    L = DEPTH
    return {
        "x": jax.random.normal(ks[0], (BATCH, SEQ, D_MODEL), f32),
        "ffn1_norm": gain(ks[1], (L, D_MODEL)),
        "ffn1_w_in": nrm(ks[2], (L, D_MODEL, 2 * D_FF), D_MODEL),
        "ffn1_w_out": nrm(ks[3], (L, D_FF, D_MODEL), D_FF),
        "mix_norm": gain(ks[4], (L, D_MODEL)),
        "w_in_mix": nrm(ks[5], (L, D_MODEL, D_IN), D_MODEL),
        "w_pool": nrm(ks[6], (L, N_POOL_GROUPS, POOL_GROUP_DIM, POOL_GROUP_DIM), POOL_GROUP_DIM),
        "pool_scale": 1.0 + 0.1 * jax.random.normal(ks[7], (L, D_POOL), f32),
        "w_alpha": nrm(ks[8], (L, GLA_GATE_RANK, GLA_DK_TOTAL), GLA_GATE_RANK),
        "b_alpha": 0.01 * jax.random.normal(ks[9], (L, GLA_DK_TOTAL), f32),
        "gla_norm": gain(ks[10], (L, GLA_DV)),
        "w_out_mix": nrm(ks[11], (L, D_MIX, D_MODEL), D_MIX),
        "ffn2_norm": gain(ks[12], (L, D_MODEL)),
        "ffn2_w_in": nrm(ks[13], (L, D_MODEL, 2 * D_FF), D_MODEL),
        "ffn2_w_out": nrm(ks[14], (L, D_FF, D_MODEL), D_FF),
        "final_norm": gain(ks[15], (D_MODEL,)),
    }


def _fwd_reference(x, ffn1_norm, ffn1_w_in, ffn1_w_out, mix_norm, w_in_mix, w_pool, pool_scale,
              w_alpha, b_alpha, gla_norm, w_out_mix, ffn2_norm, ffn2_w_in, ffn2_w_out, final_norm):
    h = x
    o_q = D_POOL
    o_k = o_q + GLA_DK_TOTAL
    o_v = o_k + GLA_DK_TOTAL
    o_g = o_v + D_GLA
    o_r = o_g + D_GLA
    for l in range(DEPTH):
        h = h + 0.5 * swiglu(rmsnorm(h, ffn1_norm[l]), ffn1_w_in[l], ffn1_w_out[l])
        u = rmsnorm(h, mix_norm[l]) @ w_in_mix[l]
        y_pool = pool_mixer(u[..., :o_q], w_pool[l], pool_scale[l])
        y_gla = gla_mixer(u[..., o_q:o_k], u[..., o_k:o_v], u[..., o_v:o_g], u[..., o_g:o_r],
                          u[..., o_r:], w_alpha[l], b_alpha[l], gla_norm[l])
        h = h + jnp.concatenate([y_pool.astype(h.dtype), y_gla.astype(h.dtype)], axis=-1) @ w_out_mix[l]
        h = h + 0.5 * swiglu(rmsnorm(h, ffn2_norm[l]), ffn2_w_in[l], ffn2_w_out[l])
    return rmsnorm(h, final_norm)


import jax as _jax
import jax.numpy as _jnp

TWIN_FORMAT = 'train_step'
FWD_PARAMS = ['x', 'ffn1_norm', 'ffn1_w_in', 'ffn1_w_out', 'mix_norm', 'w_in_mix', 'w_pool', 'pool_scale', 'w_alpha', 'b_alpha', 'gla_norm', 'w_out_mix', 'ffn2_norm', 'ffn2_w_in', 'ffn2_w_out', 'final_norm']
TWIN_WEIGHTS = ['ffn1_norm', 'ffn1_w_in', 'ffn1_w_out', 'mix_norm', 'w_in_mix', 'w_pool', 'pool_scale', 'w_alpha', 'b_alpha', 'gla_norm', 'w_out_mix', 'ffn2_norm', 'ffn2_w_in', 'ffn2_w_out', 'final_norm']
TWIN_DIFF_INPUT = 'x'
TWIN_INPUTS = ['x', 'ffn1_norm', 'ffn1_w_in', 'ffn1_w_out', 'mix_norm', 'w_in_mix', 'w_pool', 'pool_scale', 'w_alpha', 'b_alpha', 'gla_norm', 'w_out_mix', 'ffn2_norm', 'ffn2_w_in', 'ffn2_w_out', 'final_norm', 'loss_target', 'm_ffn1_norm', 'm_ffn1_w_in', 'm_ffn1_w_out', 'm_mix_norm', 'm_w_in_mix', 'm_w_pool', 'm_pool_scale', 'm_w_alpha', 'm_b_alpha', 'm_gla_norm', 'm_w_out_mix', 'm_ffn2_norm', 'm_ffn2_w_in', 'm_ffn2_w_out', 'm_final_norm', 'v_ffn1_norm', 'v_ffn1_w_in', 'v_ffn1_w_out', 'v_mix_norm', 'v_w_in_mix', 'v_w_pool', 'v_pool_scale', 'v_w_alpha', 'v_b_alpha', 'v_gla_norm', 'v_w_out_mix', 'v_ffn2_norm', 'v_ffn2_w_in', 'v_ffn2_w_out', 'v_final_norm']
TWIN_OUTPUTS = ['loss', 'grad_x', 'grad_ffn1_norm', 'grad_ffn1_w_in', 'grad_ffn1_w_out', 'grad_mix_norm', 'grad_w_in_mix', 'grad_w_pool', 'grad_pool_scale', 'grad_w_alpha', 'grad_b_alpha', 'grad_gla_norm', 'grad_w_out_mix', 'grad_ffn2_norm', 'grad_ffn2_w_in', 'grad_ffn2_w_out', 'grad_final_norm', 'delta_ffn1_norm', 'delta_ffn1_w_in', 'delta_ffn1_w_out', 'delta_mix_norm', 'delta_w_in_mix', 'delta_w_pool', 'delta_pool_scale', 'delta_w_alpha', 'delta_b_alpha', 'delta_gla_norm', 'delta_w_out_mix', 'delta_ffn2_norm', 'delta_ffn2_w_in', 'delta_ffn2_w_out', 'delta_final_norm', 'new_m_ffn1_norm', 'new_m_ffn1_w_in', 'new_m_ffn1_w_out', 'new_m_mix_norm', 'new_m_w_in_mix', 'new_m_w_pool', 'new_m_pool_scale', 'new_m_w_alpha', 'new_m_b_alpha', 'new_m_gla_norm', 'new_m_w_out_mix', 'new_m_ffn2_norm', 'new_m_ffn2_w_in', 'new_m_ffn2_w_out', 'new_m_final_norm', 'new_v_ffn1_norm', 'new_v_ffn1_w_in', 'new_v_ffn1_w_out', 'new_v_mix_norm', 'new_v_w_in_mix', 'new_v_w_pool', 'new_v_pool_scale', 'new_v_w_alpha', 'new_v_b_alpha', 'new_v_gla_norm', 'new_v_w_out_mix', 'new_v_ffn2_norm', 'new_v_ffn2_w_in', 'new_v_ffn2_w_out', 'new_v_final_norm']
TWIN_LEAF_KINDS = {'loss': 'loss', 'grad_x': 'grad_x', 'grad_ffn1_norm': 'grad_w', 'grad_ffn1_w_in': 'grad_w', 'grad_ffn1_w_out': 'grad_w', 'grad_mix_norm': 'grad_w', 'grad_w_in_mix': 'grad_w', 'grad_w_pool': 'grad_w', 'grad_pool_scale': 'grad_w', 'grad_w_alpha': 'grad_w', 'grad_b_alpha': 'grad_w', 'grad_gla_norm': 'grad_w', 'grad_w_out_mix': 'grad_w', 'grad_ffn2_norm': 'grad_w', 'grad_ffn2_w_in': 'grad_w', 'grad_ffn2_w_out': 'grad_w', 'grad_final_norm': 'grad_w', 'delta_ffn1_norm': 'delta_w', 'delta_ffn1_w_in': 'delta_w', 'delta_ffn1_w_out': 'delta_w', 'delta_mix_norm': 'delta_w', 'delta_w_in_mix': 'delta_w', 'delta_w_pool': 'delta_w', 'delta_pool_scale': 'delta_w', 'delta_w_alpha': 'delta_w', 'delta_b_alpha': 'delta_w', 'delta_gla_norm': 'delta_w', 'delta_w_out_mix': 'delta_w', 'delta_ffn2_norm': 'delta_w', 'delta_ffn2_w_in': 'delta_w', 'delta_ffn2_w_out': 'delta_w', 'delta_final_norm': 'delta_w', 'new_m_ffn1_norm': 'new_m', 'new_m_ffn1_w_in': 'new_m', 'new_m_ffn1_w_out': 'new_m', 'new_m_mix_norm': 'new_m', 'new_m_w_in_mix': 'new_m', 'new_m_w_pool': 'new_m', 'new_m_pool_scale': 'new_m', 'new_m_w_alpha': 'new_m', 'new_m_b_alpha': 'new_m', 'new_m_gla_norm': 'new_m', 'new_m_w_out_mix': 'new_m', 'new_m_ffn2_norm': 'new_m', 'new_m_ffn2_w_in': 'new_m', 'new_m_ffn2_w_out': 'new_m', 'new_m_final_norm': 'new_m', 'new_v_ffn1_norm': 'new_v', 'new_v_ffn1_w_in': 'new_v', 'new_v_ffn1_w_out': 'new_v', 'new_v_mix_norm': 'new_v', 'new_v_w_in_mix': 'new_v', 'new_v_w_pool': 'new_v', 'new_v_pool_scale': 'new_v', 'new_v_w_alpha': 'new_v', 'new_v_b_alpha': 'new_v', 'new_v_gla_norm': 'new_v', 'new_v_w_out_mix': 'new_v', 'new_v_ffn2_norm': 'new_v', 'new_v_ffn2_w_in': 'new_v', 'new_v_ffn2_w_out': 'new_v', 'new_v_final_norm': 'new_v'}


def _forward(args):
    return _fwd_reference(*[args[k] for k in FWD_PARAMS])


def _output_shape():
    out = _jax.eval_shape(lambda: _forward(_fwd_setup_inputs(0)))
    return out.shape, out.dtype

N_MICROBATCH = 1
ADAM_LR = 0.001
ADAM_B1 = 0.9
ADAM_B2 = 0.999
ADAM_EPS = 1e-08
ADAM_WD = 0.01
ADAM_STEP = 10
PER_EXAMPLE_BATCH_AXIS = {'x': 0, 'loss_target': 0}
SHARED_INPUTS = []
_WEIGHT_DTYPES = {'ffn1_norm': _jnp.float32, 'ffn1_w_in': _jnp.float32, 'ffn1_w_out': _jnp.float32, 'mix_norm': _jnp.float32, 'w_in_mix': _jnp.float32, 'w_pool': _jnp.float32, 'pool_scale': _jnp.float32, 'w_alpha': _jnp.float32, 'b_alpha': _jnp.float32, 'gla_norm': _jnp.float32, 'w_out_mix': _jnp.float32, 'ffn2_norm': _jnp.float32, 'ffn2_w_in': _jnp.float32, 'ffn2_w_out': _jnp.float32, 'final_norm': _jnp.float32}
MOMENT_SCALE = {'ffn1_norm': 3.163016e-02, 'ffn1_w_in': 1.304951e-02, 'ffn1_w_out': 2.129510e-02, 'mix_norm': 5.545368e-02, 'w_in_mix': 3.755730e-02, 'w_pool': 4.407726e-02, 'pool_scale': 4.567408e-02, 'w_alpha': 4.744899e-03, 'b_alpha': 1.886061e-02, 'gla_norm': 6.540423e-02, 'w_out_mix': 3.767983e-02, 'ffn2_norm': 2.096802e-02, 'ffn2_w_in': 8.700742e-03, 'ffn2_w_out': 1.419933e-02, 'final_norm': 7.993658e+00}


def _to_microbatches(a, axis):
    t = _jnp.moveaxis(a, axis, 0)
    t = t.reshape((N_MICROBATCH, t.shape[0] // N_MICROBATCH) + t.shape[1:])
    return _jnp.moveaxis(t, 1, axis + 1)


def setup_inputs(seed: int = 0) -> dict:
    inp = _fwd_setup_inputs(seed)
    key = _jax.random.fold_in(_jax.random.key(seed), 7919)
    shape, _ = _output_shape()
    out = dict(inp)
    out["loss_target"] = _jax.random.normal(_jax.random.fold_in(key, 0), shape, _jnp.float32)
    for i, name in enumerate(TWIN_WEIGHTS):
        w = inp[name].astype(_jnp.float32)
        if MOMENT_SCALE is None:
            s = _jnp.sqrt(_jnp.mean(_jnp.square(w)) + 1e-30)
        else:
            s = MOMENT_SCALE[name]
        km, kv = _jax.random.split(_jax.random.fold_in(key, i + 1))
        out[name] = w
        out["m_" + name] = s * _jax.random.normal(km, w.shape, _jnp.float32)
        out["v_" + name] = (s * s) * _jax.random.uniform(kv, w.shape, _jnp.float32, 0.5, 1.5)
    if N_MICROBATCH > 1:
        for name, axis in PER_EXAMPLE_BATCH_AXIS.items():
            out[name] = _to_microbatches(out[name], axis)
    return {'x': out['x'], 'ffn1_norm': out['ffn1_norm'], 'ffn1_w_in': out['ffn1_w_in'], 'ffn1_w_out': out['ffn1_w_out'], 'mix_norm': out['mix_norm'], 'w_in_mix': out['w_in_mix'], 'w_pool': out['w_pool'], 'pool_scale': out['pool_scale'], 'w_alpha': out['w_alpha'], 'b_alpha': out['b_alpha'], 'gla_norm': out['gla_norm'], 'w_out_mix': out['w_out_mix'], 'ffn2_norm': out['ffn2_norm'], 'ffn2_w_in': out['ffn2_w_in'], 'ffn2_w_out': out['ffn2_w_out'], 'final_norm': out['final_norm'], 'loss_target': out['loss_target'], 'm_ffn1_norm': out['m_ffn1_norm'], 'm_ffn1_w_in': out['m_ffn1_w_in'], 'm_ffn1_w_out': out['m_ffn1_w_out'], 'm_mix_norm': out['m_mix_norm'], 'm_w_in_mix': out['m_w_in_mix'], 'm_w_pool': out['m_w_pool'], 'm_pool_scale': out['m_pool_scale'], 'm_w_alpha': out['m_w_alpha'], 'm_b_alpha': out['m_b_alpha'], 'm_gla_norm': out['m_gla_norm'], 'm_w_out_mix': out['m_w_out_mix'], 'm_ffn2_norm': out['m_ffn2_norm'], 'm_ffn2_w_in': out['m_ffn2_w_in'], 'm_ffn2_w_out': out['m_ffn2_w_out'], 'm_final_norm': out['m_final_norm'], 'v_ffn1_norm': out['v_ffn1_norm'], 'v_ffn1_w_in': out['v_ffn1_w_in'], 'v_ffn1_w_out': out['v_ffn1_w_out'], 'v_mix_norm': out['v_mix_norm'], 'v_w_in_mix': out['v_w_in_mix'], 'v_w_pool': out['v_w_pool'], 'v_pool_scale': out['v_pool_scale'], 'v_w_alpha': out['v_w_alpha'], 'v_b_alpha': out['v_b_alpha'], 'v_gla_norm': out['v_gla_norm'], 'v_w_out_mix': out['v_w_out_mix'], 'v_ffn2_norm': out['v_ffn2_norm'], 'v_ffn2_w_in': out['v_ffn2_w_in'], 'v_ffn2_w_out': out['v_ffn2_w_out'], 'v_final_norm': out['v_final_norm']}


def _loss(weights, diff, rest, loss_target):
    with _jax.named_scope("forward"):
        args = {**rest, TWIN_DIFF_INPUT: diff, **{k: w.astype(_WEIGHT_DTYPES[k]) for k, w in weights.items()}}
        y = _forward(args)
    with _jax.named_scope("loss_head"):
        err = _jnp.square(y.astype(_jnp.float32) - loss_target)
        return 0.5 * _jnp.sum(_jnp.mean(err, axis=-1)) if err.ndim else 0.5 * err


def _adamw(w, g, m, v):
    m = ADAM_B1 * m + (1.0 - ADAM_B1) * g
    v = ADAM_B2 * v + (1.0 - ADAM_B2) * _jnp.square(g)
    m_hat = m / (1.0 - ADAM_B1 ** ADAM_STEP)
    v_hat = v / (1.0 - ADAM_B2 ** ADAM_STEP)
    delta = -ADAM_LR * (m_hat / (_jnp.sqrt(v_hat) + ADAM_EPS) + ADAM_WD * w)
    return delta, m, v


def reference(x, ffn1_norm, ffn1_w_in, ffn1_w_out, mix_norm, w_in_mix, w_pool, pool_scale, w_alpha, b_alpha, gla_norm, w_out_mix, ffn2_norm, ffn2_w_in, ffn2_w_out, final_norm, loss_target, m_ffn1_norm, m_ffn1_w_in, m_ffn1_w_out, m_mix_norm, m_w_in_mix, m_w_pool, m_pool_scale, m_w_alpha, m_b_alpha, m_gla_norm, m_w_out_mix, m_ffn2_norm, m_ffn2_w_in, m_ffn2_w_out, m_final_norm, v_ffn1_norm, v_ffn1_w_in, v_ffn1_w_out, v_mix_norm, v_w_in_mix, v_w_pool, v_pool_scale, v_w_alpha, v_b_alpha, v_gla_norm, v_w_out_mix, v_ffn2_norm, v_ffn2_w_in, v_ffn2_w_out, v_final_norm):
    given = dict(x=x, ffn1_norm=ffn1_norm, ffn1_w_in=ffn1_w_in, ffn1_w_out=ffn1_w_out, mix_norm=mix_norm, w_in_mix=w_in_mix, w_pool=w_pool, pool_scale=pool_scale, w_alpha=w_alpha, b_alpha=b_alpha, gla_norm=gla_norm, w_out_mix=w_out_mix, ffn2_norm=ffn2_norm, ffn2_w_in=ffn2_w_in, ffn2_w_out=ffn2_w_out, final_norm=final_norm, loss_target=loss_target, m_ffn1_norm=m_ffn1_norm, m_ffn1_w_in=m_ffn1_w_in, m_ffn1_w_out=m_ffn1_w_out, m_mix_norm=m_mix_norm, m_w_in_mix=m_w_in_mix, m_w_pool=m_w_pool, m_pool_scale=m_pool_scale, m_w_alpha=m_w_alpha, m_b_alpha=m_b_alpha, m_gla_norm=m_gla_norm, m_w_out_mix=m_w_out_mix, m_ffn2_norm=m_ffn2_norm, m_ffn2_w_in=m_ffn2_w_in, m_ffn2_w_out=m_ffn2_w_out, m_final_norm=m_final_norm, v_ffn1_norm=v_ffn1_norm, v_ffn1_w_in=v_ffn1_w_in, v_ffn1_w_out=v_ffn1_w_out, v_mix_norm=v_mix_norm, v_w_in_mix=v_w_in_mix, v_w_pool=v_w_pool, v_pool_scale=v_pool_scale, v_w_alpha=v_w_alpha, v_b_alpha=v_b_alpha, v_gla_norm=v_gla_norm, v_w_out_mix=v_w_out_mix, v_ffn2_norm=v_ffn2_norm, v_ffn2_w_in=v_ffn2_w_in, v_ffn2_w_out=v_ffn2_w_out, v_final_norm=v_final_norm)
    weights = {n: given[n] for n in TWIN_WEIGHTS}
    shared = {n: given[n] for n in SHARED_INPUTS}
    per_example = {n: given[n] for n in ['x']}
    grad_fn = _jax.value_and_grad(_loss, argnums=(0, 1))

    def one_microbatch(ex, loss_target):
        ex = dict(ex)
        diff = ex.pop(TWIN_DIFF_INPUT)
        return grad_fn(weights, diff, {**shared, **ex}, loss_target)

    if N_MICROBATCH == 1:
        loss, (grad_w, grad_x) = one_microbatch(per_example, given["loss_target"])
    else:
        def body(carry, xs):
            loss_sum, grad_sum = carry
            l_k, (gw_k, gx_k) = one_microbatch(xs[0], xs[1])
            with _jax.named_scope("update"):
                return (loss_sum + l_k, _jax.tree.map(_jnp.add, grad_sum, gw_k)), gx_k

        init = (_jnp.zeros((), _jnp.float32), _jax.tree.map(_jnp.zeros_like, weights))
        (loss, grad_w), grad_x = _jax.lax.scan(body, init, (per_example, given["loss_target"]))
    with _jax.named_scope("update"):
        delta_w, new_m, new_v = {}, {}, {}
        for n in TWIN_WEIGHTS:
            delta_w[n], new_m[n], new_v[n] = _adamw(weights[n], grad_w[n], given["m_" + n], given["v_" + n])
    return (loss, grad_x, *[grad_w[n] for n in TWIN_WEIGHTS], *[delta_w[n] for n in TWIN_WEIGHTS],
            *[new_m[n] for n in TWIN_WEIGHTS], *[new_v[n] for n in TWIN_WEIGHTS])
```

```python
import functools

import jax
import jax.numpy as jnp
from jax import lax
from jax.experimental import pallas as pl
from jax.experimental.pallas import tpu as pltpu

F32, BF16 = jnp.float32, jnp.bfloat16
MESH = pl.DeviceIdType.MESH
ANY = pl.BlockSpec(memory_space=pl.ANY)

NDEV = 8
S = 2048
D = 2048
FF = 5632
WIN_SHARD = 2 * FF // NDEV
WOUT_SHARD = FF // NDEV
D_POOL = 1024
PG = 256
POOL_WINDOWS = (2, 4, 8, 16)
H = 4
DK = 128
DV = 256
CH = 64
NCH = S // CH
RANK = 16
RANK_PAD = 128
D_IN = 4112
D_IN_PAD = 4224
MIX_SHARD = D_IN // NDEV
O_Q, O_K, O_V, O_G, O_R = 1024, 1536, 2048, 3072, 4096
GATE_NORM = 16.0
QK_SCALE = DK ** -0.5
EPS = 1e-6
ADAM_LR, ADAM_B1, ADAM_B2, ADAM_EPS, ADAM_WD, ADAM_STEP = 0.001, 0.9, 0.999, 1e-08, 0.01, 10
V7X_VMEM_BYTES = 64 << 20

SMALL = (("ffn1_norm", 2048), ("mix_norm", 2048), ("ffn2_norm", 2048), ("final_norm", 2048),
         ("pool_scale", 1024), ("b_alpha", 512), ("gla_norm", 256))
SMALL_ROWS = 80
LOSS_AT = sum(n for _, n in SMALL)


def _params(vmem_mb, sem=None):
    return pltpu.CompilerParams(dimension_semantics=sem, vmem_limit_bytes=min(vmem_mb << 20, V7X_VMEM_BYTES - (4 << 20)))


def _dot(a, b):
    return jnp.dot(a, b, preferred_element_type=F32)


def _dot_nt(a, b):
    return lax.dot_general(a, b, (((1,), (1,)), ((), ())), preferred_element_type=F32)


def _dot_tn(a, b):
    return lax.dot_general(a, b, (((0,), (0,)), ((), ())), preferred_element_type=F32)


def _sigmoid(x):
    return 1.0 / (1.0 + jnp.exp(-x))


def _log_sigmoid(x):
    return jnp.minimum(x, 0.0) - jnp.log(1.0 + jnp.exp(-jnp.abs(x)))


def _rms(x, g):
    r = lax.rsqrt(jnp.mean(x * x, axis=-1, keepdims=True) + EPS)
    return x * r * g


def _rms_bwd(dn, x, g):
    r = lax.rsqrt(jnp.mean(x * x, axis=-1, keepdims=True) + EPS)
    xh = x * r
    dxh = dn * g
    dx = r * (dxh - xh * jnp.mean(dxh * xh, axis=-1, keepdims=True))
    return dx, jnp.sum(dn * xh, axis=0, keepdims=True)


ROWS = 64


def _row_loop(total, fn, init=0):
    def step(t, carry):
        return fn(pl.ds(pl.multiple_of(t * ROWS, ROWS), ROWS), carry)
    return lax.fori_loop(0, total // ROWS, step, init)


def _split3(x):
    hi = x.astype(BF16)
    r1 = x - hi.astype(F32)
    mid = r1.astype(BF16)
    lo = (r1 - mid.astype(F32)).astype(BF16)
    return hi, mid, lo


def _tri_dot(tri_b, x):
    hi, mid, lo = _split3(x)
    return (_dot(tri_b, lo) + _dot(tri_b, mid)) + _dot(tri_b, hi)


FFN_TS, FFN_TF = 512, 512


def _ffn_specs():
    wg = pl.BlockSpec((D, FFN_TF), lambda i, j: (0, j))
    wu = pl.BlockSpec((D, FFN_TF), lambda i, j: (0, FF // FFN_TF + j))
    wo = pl.BlockSpec((FFN_TF, D), lambda i, j: (j, 0))
    row = pl.BlockSpec((FFN_TS, D), lambda i, j: (i, 0))
    vec = pl.BlockSpec((1, D), lambda i, j: (0, 0))
    gu = pl.BlockSpec((2, FFN_TS, FFN_TF), lambda i, j: (0, i, j))
    return wg, wu, wo, row, vec, gu


def _ffn_fwd(h, g, w_in8, w_out, name):
    nj = FF // FFN_TF
    wg, wu, wo, row, vec, gu = _ffn_specs()

    def body(h_ref, g_ref, wg_ref, wu_ref, wo_ref, ho_ref, n_ref, gu_ref, acc_ref):
        j = pl.program_id(1)

        @pl.when(j == 0)
        def _():
            def norm(rows, c):
                n_ref[rows, :] = _rms(h_ref[rows, :], g_ref[...]).astype(BF16)
                return c
            _row_loop(FFN_TS, norm)
            acc_ref[...] = jnp.zeros_like(acc_ref)

        n = n_ref[...]
        gate = _dot(n, wg_ref[...])
        up = _dot(n, wu_ref[...])
        gu_ref[0] = gate.astype(BF16)
        gu_ref[1] = up.astype(BF16)
        a = (gate * _sigmoid(gate)) * up
        acc_ref[...] += _dot(a.astype(BF16), wo_ref[...])

        @pl.when(j == nj - 1)
        def _():
            def residual(rows, c):
                ho_ref[rows, :] = h_ref[rows, :] + 0.5 * acc_ref[rows, :]
                return c
            _row_loop(FFN_TS, residual)

    return pl.pallas_call(
        body, grid=(S // FFN_TS, nj), name=name,
        in_specs=[row, vec, wg, wu, wo], out_specs=[row, row, gu],
        out_shape=[jax.ShapeDtypeStruct((S, D), F32), jax.ShapeDtypeStruct((S, D), BF16),
                   jax.ShapeDtypeStruct((2, S, FF), BF16)],
        scratch_shapes=[pltpu.VMEM((FFN_TS, D), F32)],
        compiler_params=_params(56, ("arbitrary", "arbitrary")),
    )(h, g, w_in8, w_in8, w_out)


def _ffn_bwd_x(dhp, h, g, gu_arr, w_in8, w_out, name):
    ni, nj = S // FFN_TS, FF // FFN_TF
    wg, wu, wo, row, vec, gu = _ffn_specs()
    act = pl.BlockSpec((FFN_TS, FFN_TF), lambda i, j: (i, j))

    def body(dhp_ref, h_ref, g_ref, gu_ref, wg_ref, wu_ref, wo_ref,
             dgu_ref, a_ref, df_ref, dh_ref, dhb_ref, dg_ref, acc_ref):
        i, j = pl.program_id(0), pl.program_id(1)

        @pl.when(j == 0)
        def _():
            def half(rows, c):
                df_ref[rows, :] = (0.5 * dhp_ref[rows, :]).astype(BF16)
                return c
            _row_loop(FFN_TS, half)
            acc_ref[...] = jnp.zeros_like(acc_ref)

        gate = gu_ref[0].astype(F32)
        up = gu_ref[1].astype(F32)
        da = _dot_nt(df_ref[...], wo_ref[...])
        sg = _sigmoid(gate)
        silu = gate * sg
        dgate = (da * up * (sg * (1.0 + gate * (1.0 - sg)))).astype(BF16)
        dup = (da * silu).astype(BF16)
        a_ref[...] = (silu * up).astype(BF16)
        dgu_ref[0] = dgate
        dgu_ref[1] = dup
        acc_ref[...] += _dot_nt(dgate, wg_ref[...]) + _dot_nt(dup, wu_ref[...])

        @pl.when(j == nj - 1)
        def _():
            def norm_bwd(rows, dg):
                dx, dg_rows = _rms_bwd(acc_ref[rows, :], h_ref[rows, :], g_ref[...])
                dh = dhp_ref[rows, :] + dx
                dh_ref[rows, :] = dh
                dhb_ref[rows, :] = dh.astype(BF16)
                return dg + dg_rows
            dg = _row_loop(FFN_TS, norm_bwd, jnp.zeros((1, D), F32))

            @pl.when(i == 0)
            def _():
                dg_ref[...] = dg

            @pl.when(i > 0)
            def _():
                dg_ref[...] += dg

    return pl.pallas_call(
        body, grid=(ni, nj), name=name,
        in_specs=[row, row, vec, gu, wg, wu, wo],
        out_specs=[gu, act, row, row, row, vec],
        out_shape=[jax.ShapeDtypeStruct((2, S, FF), BF16), jax.ShapeDtypeStruct((S, FF), BF16),
                   jax.ShapeDtypeStruct((S, D), BF16), jax.ShapeDtypeStruct((S, D), F32),
                   jax.ShapeDtypeStruct((S, D), BF16), jax.ShapeDtypeStruct((1, D), F32)],
        scratch_shapes=[pltpu.VMEM((FFN_TS, D), F32)],
        compiler_params=_params(58, ("arbitrary", "arbitrary")),
    )(dhp, h, g, gu_arr, w_in8, w_in8, w_out)


def _tn_matmul(a, b, a_spec, b_spec, out_shape, out_spec, grid, name, vmem_mb):
    def body(a_ref, b_ref, o_ref):
        o_ref[...] = _dot_tn(a_ref[...], b_ref[...]).astype(o_ref.dtype)

    return pl.pallas_call(body, grid=grid, name=name, in_specs=[a_spec, b_spec], out_specs=out_spec,
                          out_shape=out_shape, compiler_params=_params(vmem_mb))(a, b)


def _norm_matmul(h, g, w, name, ts=512, tn=1408):
    n_out = w.shape[1]

    def body(h_ref, g_ref, w_ref, u_ref, n_ref):
        @pl.when(pl.program_id(1) == 0)
        def _():
            def norm(rows, c):
                n_ref[rows, :] = _rms(h_ref[rows, :], g_ref[...]).astype(BF16)
                return c
            _row_loop(ts, norm)

        u_ref[...] = _dot(n_ref[...], w_ref[...])

    return pl.pallas_call(
        body, grid=(S // ts, n_out // tn), name=name,
        in_specs=[pl.BlockSpec((ts, D), lambda i, j: (i, 0)), pl.BlockSpec((1, D), lambda i, j: (0, 0)),
                  pl.BlockSpec((D, tn), lambda i, j: (0, j))],
        out_specs=[pl.BlockSpec((ts, tn), lambda i, j: (i, j)), pl.BlockSpec((ts, D), lambda i, j: (i, 0))],
        out_shape=[jax.ShapeDtypeStruct((S, n_out), F32), jax.ShapeDtypeStruct((S, D), BF16)],
        compiler_params=_params(48, ("arbitrary", "arbitrary")),
    )(h, g, w)


def _matmul_residual(a, w, res, name, ts=512, tn=1024):
    k, n_out = w.shape

    def body(a_ref, w_ref, r_ref, o_ref):
        o_ref[...] = r_ref[...] + _dot(a_ref[...], w_ref[...])

    return pl.pallas_call(
        body, grid=(S // ts, n_out // tn), name=name,
        in_specs=[pl.BlockSpec((ts, k), lambda i, j: (i, 0)), pl.BlockSpec((k, tn), lambda i, j: (0, j)),
                  pl.BlockSpec((ts, tn), lambda i, j: (i, j))],
        out_specs=pl.BlockSpec((ts, tn), lambda i, j: (i, j)),
        out_shape=jax.ShapeDtypeStruct((S, n_out), F32),
        compiler_params=_params(40),
    )(a, w, res)


def _nt_matmul(a, w, name, ts=512, tn=1024):
    n_out, k = w.shape

    def body(a_ref, w_ref, o_ref):
        o_ref[...] = _dot_nt(a_ref[...], w_ref[...])

    return pl.pallas_call(
        body, grid=(S // ts, n_out // tn), name=name,
        in_specs=[pl.BlockSpec((ts, k), lambda i, j: (i, 0)), pl.BlockSpec((tn, k), lambda i, j: (j, 0))],
        out_specs=pl.BlockSpec((ts, tn), lambda i, j: (i, j)),
        out_shape=jax.ShapeDtypeStruct((S, n_out), F32),
        compiler_params=_params(40),
    )(a, w)


def _nt_matmul_normbwd(du, w, h, g, dres, name, ts=512, tn=1408):
    n_in = w.shape[1]
    nj = n_in // tn
    row = pl.BlockSpec((ts, D), lambda i, j: (i, 0))
    vec = pl.BlockSpec((1, D), lambda i, j: (0, 0))

    def body(du_ref, w_ref, h_ref, g_ref, dres_ref, dh_ref, dg_ref, acc_ref):
        i, j = pl.program_id(0), pl.program_id(1)

        @pl.when(j == 0)
        def _():
            acc_ref[...] = jnp.zeros_like(acc_ref)

        acc_ref[...] += _dot_nt(du_ref[...], w_ref[...])

        @pl.when(j == nj - 1)
        def _():
            def norm_bwd(rows, dg):
                dx, dg_rows = _rms_bwd(acc_ref[rows, :], h_ref[rows, :], g_ref[...])
                dh_ref[rows, :] = dres_ref[rows, :] + dx
                return dg + dg_rows
            dg = _row_loop(ts, norm_bwd, jnp.zeros((1, D), F32))

            @pl.when(i == 0)
            def _():
                dg_ref[...] = dg

            @pl.when(i > 0)
            def _():
                dg_ref[...] += dg

    return pl.pallas_call(
        body, grid=(S // ts, nj), name=name,
        in_specs=[pl.BlockSpec((ts, tn), lambda i, j: (i, j)), pl.BlockSpec((D, tn), lambda i, j: (0, j)), row, vec, row],
        out_specs=[row, vec],
        out_shape=[jax.ShapeDtypeStruct((S, D), F32), jax.ShapeDtypeStruct((1, D), F32)],
        scratch_shapes=[pltpu.VMEM((ts, D), F32)],
        compiler_params=_params(52, ("arbitrary", "arbitrary")),
    )(du, w, h, g, dres)


def _loss_head(h, g, tgt, ts=256):
    row = pl.BlockSpec((ts, D), lambda i: (i, 0))
    vec = pl.BlockSpec((1, D), lambda i: (0, 0))

    def body(h_ref, g_ref, t_ref, dh_ref, dg_ref, loss_ref):
        i = pl.program_id(0)

        def rows_fn(rows, carry):
            dg, part = carry
            x = h_ref[rows, :]
            gv = g_ref[...]
            err = _rms(x, gv) - t_ref[rows, :]
            part = part + 0.5 * jnp.sum(jnp.mean(err * err, axis=-1, keepdims=True), axis=0, keepdims=True)
            dx, dg_rows = _rms_bwd(err * (1.0 / D), x, gv)
            dh_ref[rows, :] = dx
            return dg + dg_rows, part
        dg, part = _row_loop(ts, rows_fn, (jnp.zeros((1, D), F32), jnp.zeros((1, 1), F32)))

        @pl.when(i == 0)
        def _():
            dg_ref[...] = dg
            loss_ref[...] = jnp.broadcast_to(part, loss_ref.shape)

        @pl.when(i > 0)
        def _():
            dg_ref[...] += dg
            loss_ref[...] += jnp.broadcast_to(part, loss_ref.shape)

    return pl.pallas_call(
        body, grid=(S // ts,), name="loss_head",
        in_specs=[row, vec, row], out_specs=[row, vec, pl.BlockSpec((1, 128), lambda i: (0, 0))],
        out_shape=[jax.ShapeDtypeStruct((S, D), F32), jax.ShapeDtypeStruct((1, D), F32),
                   jax.ShapeDtypeStruct((1, 128), F32)],
        compiler_params=_params(40, ("arbitrary",)),
    )(h, g, tgt)


def _pool_specs():
    blk = pl.BlockSpec((S, PG), lambda gi: (0, gi))
    wp = pl.BlockSpec((None, PG, PG), lambda gi: (gi, 0, 0))
    sc = pl.BlockSpec((1, PG), lambda gi: (0, gi))
    return blk, wp, sc


def _pool_fwd(u, wp_b, scale):
    blk, wp, sc = _pool_specs()

    def body(u_ref, wp_ref, sc_ref, y_ref, pooled_ref):
        win = 2 << pl.program_id(0)
        row = lax.broadcasted_iota(jnp.int32, (S, PG), 0)
        x = u_ref[...]
        s = x
        for k in (1, 2, 4, 8):
            s = s + jnp.where((row >= k) & (k < win), pltpu.roll(s, k, 0), 0.0)
        cnt = jnp.minimum(row + 1, win).astype(F32)
        pooled = (s / cnt - x).astype(BF16)
        pooled_ref[...] = pooled
        y_ref[...] = (_dot(pooled, wp_ref[...]) * sc_ref[...]).astype(BF16)

    return pl.pallas_call(
        body, grid=(len(POOL_WINDOWS),), name="pool_fwd", in_specs=[blk, wp, sc], out_specs=[blk, blk],
        out_shape=[jax.ShapeDtypeStruct((S, D_POOL), BF16), jax.ShapeDtypeStruct((S, D_POOL), BF16)],
        compiler_params=_params(40),
    )(u, wp_b, scale)


def _pool_bwd(dy, pooled, wp_b, scale):
    blk, wp, sc = _pool_specs()

    def body(dy_ref, p_ref, wp_ref, sc_ref, du_ref, dwp_ref, dsc_ref):
        win = 2 << pl.program_id(0)
        row = lax.broadcasted_iota(jnp.int32, (S, PG), 0)
        dyv = dy_ref[...]
        pooled = p_ref[...]
        w = wp_ref[...]
        dsc_ref[...] = jnp.sum(dyv * _dot(pooled, w), axis=0, keepdims=True)
        dz = (dyv * sc_ref[...]).astype(BF16)
        dwp_ref[...] = _dot_tn(pooled, dz)
        dpooled = _dot_nt(dz, w)
        cnt = jnp.minimum(row + 1, win).astype(F32)
        fs = dpooled / cnt
        for k in (1, 2, 4, 8):
            fs = fs + jnp.where((row < S - k) & (k < win), pltpu.roll(fs, S - k, 0), 0.0)
        du_ref[...] = (fs - dpooled).astype(BF16)

    return pl.pallas_call(
        body, grid=(len(POOL_WINDOWS),), name="pool_bwd", in_specs=[blk, blk, wp, sc], out_specs=[blk, wp, sc],
        out_shape=[jax.ShapeDtypeStruct((S, D_POOL), BF16), jax.ShapeDtypeStruct((len(POOL_WINDOWS), PG, PG), F32),
                   jax.ShapeDtypeStruct((1, D_POOL), F32)],
        compiler_params=_params(40),
    )(dy, pooled, wp_b, scale)


def _gla_in_specs(chunk_of):
    def at(width, col):
        return pl.BlockSpec((CH, width), lambda n: (chunk_of(n), col))
    return [at(H * DK, O_Q // (H * DK)), at(H * DK, O_K // (H * DK)), at(H * DV, O_V // (H * DV)),
            at(H * DV, O_G // (H * DV)), at(RANK_PAD, O_R // RANK_PAD)]


def _gla_decay_terms(lr_ref, wa_ref, ba_ref, q_ref, k_ref):
    row = lax.broadcasted_iota(jnp.int32, (CH, CH), 0)
    col = lax.broadcasted_iota(jnp.int32, (CH, CH), 1)
    tril = row >= col
    z = _dot(lr_ref[...].astype(BF16), wa_ref[...]) + ba_ref[...]
    la = _log_sigmoid(z) / GATE_NORM
    b = _tri_dot(jnp.where(tril, 1.0, 0.0).astype(BF16), la)
    bl = jnp.sum(la, axis=0, keepdims=True)
    e_b, e_nb, e_tb = jnp.exp(b), jnp.exp(-b), jnp.exp(bl - b)
    kk = k_ref[...]
    q_dec = (q_ref[...] * QK_SCALE) * e_b
    return tril, z, e_b, e_nb, e_tb, jnp.exp(bl), q_dec, kk * e_nb, kk * e_tb


def _gla_fwd(u, wa_b, ba, gn):
    wide = pl.BlockSpec((CH, H * DV), lambda n: (n, 0))

    def body(q_ref, k_ref, v_ref, g_ref, lr_ref, wa_ref, ba_ref, gn_ref, y_ref, o_ref, st_ref, state):
        @pl.when(pl.program_id(0) == 0)
        def _():
            state[...] = jnp.zeros_like(state)

        tril, _, _, _, _, dec, q_dec, k_inv, k_tail = _gla_decay_terms(lr_ref, wa_ref, ba_ref, q_ref, k_ref)
        for hd in range(H):
            ks, vs = slice(hd * DK, (hd + 1) * DK), slice(hd * DV, (hd + 1) * DV)
            qb, kib, ktb = q_dec[:, ks].astype(BF16), k_inv[:, ks].astype(BF16), k_tail[:, ks].astype(BF16)
            vb = v_ref[:, vs].astype(BF16)
            p = jnp.where(tril, _dot_nt(qb, kib), 0.0)
            st = state[hd]
            o = _dot(p.astype(BF16), vb) + _dot_nt(qb, st.astype(BF16))
            st_ref[hd] = st
            state[hd] = st * dec[:, ks] + _dot_tn(vb, ktb)
            o_ref[:, vs] = o
            on = _rms(o, gn_ref[...])
            gg = g_ref[:, vs]
            y_ref[:, vs] = (on * (gg * _sigmoid(gg))).astype(BF16)

    return pl.pallas_call(
        body, grid=(NCH,), name="gla_fwd",
        in_specs=_gla_in_specs(lambda n: n) + [pl.BlockSpec((RANK_PAD, H * DK), lambda n: (0, 0)),
                                               pl.BlockSpec((1, H * DK), lambda n: (0, 0)),
                                               pl.BlockSpec((1, DV), lambda n: (0, 0))],
        out_specs=[wide, wide, pl.BlockSpec((None, H, DV, DK), lambda n: (n, 0, 0, 0))],
        out_shape=[jax.ShapeDtypeStruct((S, H * DV), BF16), jax.ShapeDtypeStruct((S, H * DV), F32),
                   jax.ShapeDtypeStruct((NCH, H, DV, DK), F32)],
        scratch_shapes=[pltpu.VMEM((H, DV, DK), F32)],
        compiler_params=_params(32, ("arbitrary",)),
    )(u, u, u, u, u, wa_b, ba, gn)


GLA_DU = 2 * H * DK + 2 * H * DV + RANK_PAD


def _gla_bwd(u, o_arr, states, dy, wa_b, ba, gn):
    rev = lambda n: NCH - 1 - n
    wide = pl.BlockSpec((CH, H * DV), lambda n: (rev(n), 0))

    def body(q_ref, k_ref, v_ref, g_ref, lr_ref, o_ref, st_ref, dy_ref, wa_ref, ba_ref, gn_ref,
             du_ref, dwa_ref, dba_ref, dgn_ref, gstate, db_scr, dbl_scr):
        @pl.when(pl.program_id(0) == 0)
        def _():
            gstate[...] = jnp.zeros_like(gstate)
            dwa_ref[...] = jnp.zeros_like(dwa_ref)
            dba_ref[...] = jnp.zeros_like(dba_ref)
            dgn_ref[...] = jnp.zeros_like(dgn_ref)

        tril, z, e_b, e_nb, e_tb, dec, q_dec, k_inv, k_tail = _gla_decay_terms(lr_ref, wa_ref, ba_ref, q_ref, k_ref)
        gnv = gn_ref[...]
        dgn = jnp.zeros((1, DV), F32)
        for hd in range(H):
            ks, vs = slice(hd * DK, (hd + 1) * DK), slice(hd * DV, (hd + 1) * DV)
            qh, kih, kth = q_dec[:, ks], k_inv[:, ks], k_tail[:, ks]
            qb, kib, ktb = qh.astype(BF16), kih.astype(BF16), kth.astype(BF16)
            vb = v_ref[:, vs].astype(BF16)
            o = o_ref[:, vs]
            gg = g_ref[:, vs]
            dyh = dy_ref[:, vs]
            r = lax.rsqrt(jnp.mean(o * o, axis=-1, keepdims=True) + EPS)
            xh = o * r
            sg = _sigmoid(gg)
            dgate = dyh * (xh * gnv) * (sg * (1.0 + gg * (1.0 - sg)))
            don = dyh * (gg * sg)
            dgn = dgn + jnp.sum(don * xh, axis=0, keepdims=True)
            dxh = don * gnv
            d_o = (r * (dxh - xh * jnp.mean(dxh * xh, axis=-1, keepdims=True))).astype(BF16)
            pb = jnp.where(tril, _dot_nt(qb, kib), 0.0).astype(BF16)
            dpb = jnp.where(tril, _dot_nt(d_o, vb), 0.0).astype(BF16)
            gt = gstate[hd]
            gtb = gt.astype(BF16)
            st = st_ref[hd]
            dv = _dot_tn(pb, d_o) + _dot_nt(ktb, gtb)
            dq_dec = _dot(dpb, kib) + _dot(d_o, st.astype(BF16))
            dk_inv = _dot_tn(dpb, qb)
            dk_tail = _dot(vb, gtb)
            ddec = jnp.sum(gt * st, axis=0, keepdims=True)
            gstate[hd] = _dot_tn(d_o, qb) + dec[:, ks] * gt
            du_ref[:, ks] = (dq_dec * QK_SCALE * e_b[:, ks]).astype(BF16)
            du_ref[:, H * DK + hd * DK:H * DK + (hd + 1) * DK] = (dk_inv * e_nb[:, ks] + dk_tail * e_tb[:, ks]).astype(BF16)
            du_ref[:, 2 * H * DK + hd * DV:2 * H * DK + (hd + 1) * DV] = dv.astype(BF16)
            du_ref[:, 2 * H * DK + H * DV + hd * DV:2 * H * DK + H * DV + (hd + 1) * DV] = dgate.astype(BF16)
            db_scr[:, ks] = dq_dec * qh - dk_inv * kih - dk_tail * kth
            dbl_scr[:, ks] = jnp.sum(dk_tail * kth, axis=0, keepdims=True) + ddec * dec[:, ks]
        dgn_ref[...] += dgn
        row = lax.broadcasted_iota(jnp.int32, (CH, CH), 0)
        col = lax.broadcasted_iota(jnp.int32, (CH, CH), 1)
        dla = _tri_dot(jnp.where(row <= col, 1.0, 0.0).astype(BF16), db_scr[...]) + dbl_scr[...]
        dz = dla * (1.0 / GATE_NORM) * _sigmoid(-z)
        dzb = dz.astype(BF16)
        du_ref[:, GLA_DU - RANK_PAD:] = _dot_nt(dzb, wa_ref[...]).astype(BF16)
        dwa_ref[...] += _dot_tn(lr_ref[...].astype(BF16), dzb)
        dba_ref[...] += jnp.sum(dz, axis=0, keepdims=True)

    full = lambda shape: pl.BlockSpec(shape, lambda n: (0,) * len(shape))
    return pl.pallas_call(
        body, grid=(NCH,), name="gla_bwd",
        in_specs=_gla_in_specs(rev) + [wide, pl.BlockSpec((None, H, DV, DK), lambda n: (rev(n), 0, 0, 0)),
                                       pl.BlockSpec((CH, H * DV), lambda n: (rev(n), 1)),
                                       full((RANK_PAD, H * DK)), full((1, H * DK)), full((1, DV))],
        out_specs=[pl.BlockSpec((CH, GLA_DU), lambda n: (rev(n), 0)), full((RANK_PAD, H * DK)), full((1, H * DK)),
                   full((1, DV))],
        out_shape=[jax.ShapeDtypeStruct((S, GLA_DU), BF16), jax.ShapeDtypeStruct((RANK_PAD, H * DK), F32),
                   jax.ShapeDtypeStruct((1, H * DK), F32), jax.ShapeDtypeStruct((1, DV), F32)],
        scratch_shapes=[pltpu.VMEM((H, DV, DK), F32), pltpu.VMEM((CH, H * DK), F32), pltpu.VMEM((1, H * DK), F32)],
        compiler_params=_params(32, ("arbitrary",)),
    )(u, u, u, u, u, o_arr, states, dy, wa_b, ba, gn)


def _local_step(x, tgt, w):
    h1, n1, gu1 = _ffn_fwd(x, w["ffn1_norm"], w["ffn1_w_in"], w["ffn1_w_out"], "ffn1_fwd")
    u, n2 = _norm_matmul(h1, w["mix_norm"], w["w_in_mix"], "mix_in")
    y_pool, pooled = _pool_fwd(u, w["w_pool"], w["pool_scale"])
    y_gla, o_gla, states = _gla_fwd(u, w["w_alpha"], w["b_alpha"], w["gla_norm"])
    y = jnp.concatenate([y_pool, y_gla], axis=1)
    h2 = _matmul_residual(y, w["w_out_mix"], h1, "mix_out")
    h3, n3, gu3 = _ffn_fwd(h2, w["ffn2_norm"], w["ffn2_w_in"], w["ffn2_w_out"], "ffn2_fwd")
    dh3, d_final, loss = _loss_head(h3, w["final_norm"], tgt)

    slot_rows = pl.BlockSpec((S, 512), lambda s, m: (0, m))
    win_b = pl.BlockSpec((None, S, WIN_SHARD), lambda s, m: (s // (NDEV // 2), 0, s % (NDEV // 2)))
    win_o = pl.BlockSpec((512, WIN_SHARD), lambda s, m: (m, s))

    def ffn_weight_grads(n, dgu, act, df, tag):
        d_in = _tn_matmul(n, dgu, slot_rows, win_b, jax.ShapeDtypeStruct((D, 2 * FF), BF16), win_o,
                          (NDEV, D // 512), tag + "_dw_in", 32)
        d_out = _tn_matmul(act, df, pl.BlockSpec((S, 512), lambda m: (0, m)), pl.BlockSpec((S, D), lambda m: (0, 0)),
                           jax.ShapeDtypeStruct((FF, D), BF16), pl.BlockSpec((512, D), lambda m: (m, 0)),
                           (FF // 512,), tag + "_dw_out", 40)
        return d_in, d_out

    dgu3, act3, df3, dh2, dh2b, d_ffn2 = _ffn_bwd_x(dh3, h2, w["ffn2_norm"], gu3, w["ffn2_w_in"], w["ffn2_w_out"], "ffn2_bwd")
    d_w2in, d_w2out = ffn_weight_grads(n3, dgu3, act3, df3, "ffn2")
    dy = _nt_matmul(dh2b, w["w_out_mix"], "mix_out_bwd")
    d_wo = _tn_matmul(y, dh2b, pl.BlockSpec((S, 512), lambda m: (0, m)), pl.BlockSpec((S, D), lambda m: (0, 0)),
                      jax.ShapeDtypeStruct((D, D), BF16), pl.BlockSpec((512, D), lambda m: (m, 0)), (D // 512,),
                      "mix_out_dw", 40)
    du_pool, d_wpool, d_scale = _pool_bwd(dy, pooled, w["w_pool"], w["pool_scale"])
    du_gla, d_walpha, d_balpha, d_gnorm = _gla_bwd(u, o_gla, states, dy, w["w_alpha"], w["b_alpha"], w["gla_norm"])
    du = jnp.concatenate([du_pool, du_gla], axis=1)
    dh1, d_mix = _nt_matmul_normbwd(du, w["w_in_mix"], h1, w["mix_norm"], dh2, "mix_in_bwd")
    d_wmix = _tn_matmul(n2, du, pl.BlockSpec((S, 512), lambda j, m: (0, m)), pl.BlockSpec((S, 1408), lambda j, m: (0, j)),
                        jax.ShapeDtypeStruct((D, D_IN_PAD), BF16), pl.BlockSpec((512, 1408), lambda j, m: (m, j)),
                        (D_IN_PAD // 1408, D // 512), "mix_in_dw", 32)
    dgu1, act1, df1, dx, _, d_ffn1 = _ffn_bwd_x(dh1, x, w["ffn1_norm"], gu1, w["ffn1_w_in"], w["ffn1_w_out"], "ffn1_bwd")
    d_w1in, d_w1out = ffn_weight_grads(n1, dgu1, act1, df1, "ffn1")

    grads = dict(ffn1_norm=d_ffn1, ffn1_w_in=d_w1in, ffn1_w_out=d_w1out, mix_norm=d_mix, w_in_mix=d_wmix,
                 w_pool=d_wpool, pool_scale=d_scale, w_alpha=d_walpha, b_alpha=d_balpha, gla_norm=d_gnorm,
                 w_out_mix=d_wo, ffn2_norm=d_ffn2, ffn2_w_in=d_w2in, ffn2_w_out=d_w2out, final_norm=d_final)
    return loss, dx, grads


def _mesh_index():
    return 4 * lax.axis_index("x") + 2 * lax.axis_index("y") + lax.axis_index("c")


def _coords(p):
    return (p // 4, (p // 2) % 2, p % 2)


def _exchange(items, name):
    arrays = [a for a, _ in items]
    kinds = [k for _, k in items]
    na = len(arrays)
    out_shape = []
    for a, kind in items:
        shape = {"bcast": (NDEV,) + a.shape, "scatter": a.shape,
                 "bcast_cols": (a.shape[0], NDEV * a.shape[-1]),
                 "scatter_cols": (NDEV, a.shape[0], a.shape[-1] // NDEV)}[kind]
        out_shape.append(jax.ShapeDtypeStruct(shape, a.dtype))

    def body(*refs):
        ins, outs = refs[:na], refs[na:2 * na]
        send_sems, recv_sems, local_sems = refs[2 * na:]
        me = _mesh_index()

        def cols(ref, p, width):
            return ref.at[:, pl.ds(pl.multiple_of(p * width, 128), width)]

        def src(a, p):
            if kinds[a] == "scatter":
                return ins[a].at[p]
            if kinds[a] == "scatter_cols":
                return cols(ins[a], p, ins[a].shape[1] // NDEV)
            return ins[a]

        def dst(a, s):
            if kinds[a] == "bcast_cols":
                return cols(outs[a], s, ins[a].shape[1])
            return outs[a].at[s]

        local = [pltpu.make_async_copy(src(a, me), dst(a, me), local_sems.at[a]) for a in range(na)]
        for cp in local:
            cp.start()
        for k in range(1, NDEV):
            to = (me + k) % NDEV
            for a in range(na):
                pltpu.make_async_remote_copy(src(a, to), dst(a, me), send_sems.at[a, k], recv_sems.at[a, k],
                                             device_id=_coords(to), device_id_type=MESH).start()
        for k in range(1, NDEV):
            to = (me + k) % NDEV
            frm = (me + NDEV - k) % NDEV
            for a in range(na):
                cp = pltpu.make_async_remote_copy(src(a, to), dst(a, frm), send_sems.at[a, k], recv_sems.at[a, k],
                                                  device_id=_coords(to), device_id_type=MESH)
                cp.wait_send()
                cp.wait_recv()
        for cp in local:
            cp.wait()

    return pl.pallas_call(
        body, name=name, in_specs=[ANY] * na, out_specs=[ANY] * na, out_shape=out_shape,
        scratch_shapes=[pltpu.SemaphoreType.DMA((na, NDEV)), pltpu.SemaphoreType.DMA((na, NDEV)),
                        pltpu.SemaphoreType.DMA((na,))],
    )(*arrays)


def _adamw(parts, w, m, v, tr, name):
    rows, cols = w.shape
    blk = pl.BlockSpec((tr, cols), lambda i: (i, 0))

    def body(p_ref, w_ref, m_ref, v_ref, g_ref, d_ref, nm_ref, nv_ref):
        g = p_ref[0].astype(F32)
        for s in range(1, NDEV):
            g = g + p_ref[s].astype(F32)
        nm = ADAM_B1 * m_ref[...] + (1.0 - ADAM_B1) * g
        nv = ADAM_B2 * v_ref[...] + (1.0 - ADAM_B2) * (g * g)
        m_hat = nm / (1.0 - ADAM_B1 ** ADAM_STEP)
        v_hat = nv / (1.0 - ADAM_B2 ** ADAM_STEP)
        g_ref[...] = g
        d_ref[...] = -ADAM_LR * (m_hat / (jnp.sqrt(v_hat) + ADAM_EPS) + ADAM_WD * w_ref[...])
        nm_ref[...] = nm
        nv_ref[...] = nv

    return pl.pallas_call(
        body, grid=(rows // tr,), name=name,
        in_specs=[pl.BlockSpec((NDEV, tr, cols), lambda i: (0, i, 0)), blk, blk, blk], out_specs=[blk] * 4,
        out_shape=[jax.ShapeDtypeStruct((rows, cols), F32)] * 4,
        compiler_params=_params(40),
    )(parts, w, m, v)


def _pack_small(vals, extra=None):
    flat = [vals[n].reshape(-1).astype(F32) for n, _ in SMALL]
    tail = jnp.zeros((SMALL_ROWS * 128 - LOSS_AT,), F32)
    if extra is not None:
        tail = tail.at[0].set(extra)
    return jnp.concatenate(flat + [tail]).reshape(SMALL_ROWS, 128)


def _unpack_small(packed, like):
    flat, out, at = packed.reshape(-1), {}, 0
    for n, size in SMALL:
        out[n] = flat[at:at + size].reshape(like[n].shape)
        at += size
    return out, flat[LOSS_AT]


def kernel(x, ffn1_norm, ffn1_w_in, ffn1_w_out, mix_norm, w_in_mix, w_pool, pool_scale, w_alpha, b_alpha, gla_norm, w_out_mix, ffn2_norm, ffn2_w_in, ffn2_w_out, final_norm, loss_target, m_ffn1_norm, m_ffn1_w_in, m_ffn1_w_out, m_mix_norm, m_w_in_mix, m_w_pool, m_pool_scale, m_w_alpha, m_b_alpha, m_gla_norm, m_w_out_mix, m_ffn2_norm, m_ffn2_w_in, m_ffn2_w_out, m_final_norm, v_ffn1_norm, v_ffn1_w_in, v_ffn1_w_out, v_mix_norm, v_w_in_mix, v_w_pool, v_pool_scale, v_w_alpha, v_b_alpha, v_gla_norm, v_w_out_mix, v_ffn2_norm, v_ffn2_w_in, v_ffn2_w_out, v_final_norm):
    names = ["ffn1_norm", "ffn1_w_in", "ffn1_w_out", "mix_norm", "w_in_mix", "w_pool", "pool_scale", "w_alpha", "b_alpha",
             "gla_norm", "w_out_mix", "ffn2_norm", "ffn2_w_in", "ffn2_w_out", "final_norm"]
    p = dict(zip(names, [ffn1_norm, ffn1_w_in, ffn1_w_out, mix_norm, w_in_mix, w_pool, pool_scale, w_alpha, b_alpha,
                         gla_norm, w_out_mix, ffn2_norm, ffn2_w_in, ffn2_w_out, final_norm]))
    m = dict(zip(names, [m_ffn1_norm, m_ffn1_w_in, m_ffn1_w_out, m_mix_norm, m_w_in_mix, m_w_pool, m_pool_scale, m_w_alpha,
                         m_b_alpha, m_gla_norm, m_w_out_mix, m_ffn2_norm, m_ffn2_w_in, m_ffn2_w_out, m_final_norm]))
    v = dict(zip(names, [v_ffn1_norm, v_ffn1_w_in, v_ffn1_w_out, v_mix_norm, v_w_in_mix, v_w_pool, v_pool_scale, v_w_alpha,
                         v_b_alpha, v_gla_norm, v_w_out_mix, v_ffn2_norm, v_ffn2_w_in, v_ffn2_w_out, v_final_norm]))

    big = ["ffn1_w_in", "ffn1_w_out", "w_in_mix", "w_out_mix", "ffn2_w_in", "ffn2_w_out"]
    shards = [(p[n][0].astype(BF16), "bcast_cols" if n in ("ffn1_w_in", "ffn2_w_in") else "bcast") for n in big]
    shards += [(p["w_pool"][0].reshape(H * 32, PG), "bcast"), (p["w_alpha"][0], "bcast")]
    g_w1in, g_w1out, g_wmix, g_wo, g_w2in, g_w2out, g_wpool, g_walpha = _exchange(shards, "gather_weights")

    wmix = jnp.transpose(g_wmix, (1, 0, 2)).reshape(D, D_IN)
    walpha = jnp.transpose(g_walpha, (1, 0, 2)).reshape(RANK, H * DK)
    full = {
        "ffn1_w_in": g_w1in, "ffn1_w_out": g_w1out.reshape(FF, D),
        "ffn2_w_in": g_w2in, "ffn2_w_out": g_w2out.reshape(FF, D),
        "w_in_mix": jnp.pad(wmix, ((0, 0), (0, D_IN_PAD - D_IN))),
        "w_out_mix": g_wo.reshape(D, D),
        "w_pool": jnp.transpose(g_wpool.reshape(NDEV, H, 32, PG), (1, 0, 2, 3)).reshape(H, PG, PG).astype(BF16),
        "w_alpha": jnp.pad(walpha, ((0, RANK_PAD - RANK), (0, 0))).astype(BF16),
        "final_norm": final_norm.reshape(1, D),
    }
    for n in ("ffn1_norm", "mix_norm", "ffn2_norm", "pool_scale", "b_alpha", "gla_norm"):
        full[n] = p[n]

    loss_part, dx, g = _local_step(x[0], loss_target[0], full)

    d_wmix8 = jnp.transpose(g["w_in_mix"][:, :D_IN].reshape(D, NDEV, MIX_SHARD), (1, 0, 2))
    d_wpool8 = jnp.transpose(g["w_pool"].reshape(H, NDEV, 32, PG), (1, 0, 2, 3)).reshape(NDEV, H * 32, PG)
    d_walpha8 = jnp.transpose(g["w_alpha"][:RANK].reshape(RANK, NDEV, H * DK // NDEV), (1, 0, 2))
    parts = [(g["ffn1_w_in"], "scatter_cols"), (g["ffn1_w_out"].reshape(NDEV, WOUT_SHARD, D), "scatter"), (d_wmix8, "scatter"),
             (g["w_out_mix"].reshape(NDEV, D // NDEV, D), "scatter"), (g["ffn2_w_in"], "scatter_cols"),
             (g["ffn2_w_out"].reshape(NDEV, WOUT_SHARD, D), "scatter"), (d_wpool8, "scatter"), (d_walpha8, "scatter"),
             (_pack_small(g, loss_part[0, 0]), "bcast")]
    r_w1in, r_w1out, r_wmix, r_wo, r_w2in, r_w2out, r_wpool, r_walpha, r_small = _exchange(parts, "reduce_grads")

    def upd(parts, n, shape2d, tr):
        res = _adamw(parts, p[n].reshape(shape2d), m[n].reshape(shape2d), v[n].reshape(shape2d), tr, "adamw_" + n)
        return [r.reshape(p[n].shape) for r in res]

    out = {
        "ffn1_w_in": upd(r_w1in, "ffn1_w_in", (D, WIN_SHARD), 128),
        "ffn1_w_out": upd(r_w1out, "ffn1_w_out", (WOUT_SHARD, D), 64),
        "w_in_mix": upd(r_wmix, "w_in_mix", (D, MIX_SHARD), 128),
        "w_out_mix": upd(r_wo, "w_out_mix", (D // NDEV, D), 64),
        "ffn2_w_in": upd(r_w2in, "ffn2_w_in", (D, WIN_SHARD), 128),
        "ffn2_w_out": upd(r_w2out, "ffn2_w_out", (WOUT_SHARD, D), 64),
        "w_pool": upd(r_wpool, "w_pool", (H * 32, PG), H * 32),
        "w_alpha": upd(r_walpha, "w_alpha", (RANK, H * DK // NDEV), RANK),
    }
    small_res = _adamw(r_small, _pack_small(p), _pack_small(m), _pack_small(v), SMALL_ROWS, "adamw_small")
    unpacked = [_unpack_small(r, p) for r in small_res]
    loss = unpacked[0][1]
    for n, _ in SMALL:
        out[n] = [u[0][n] for u in unpacked]

    return (loss, dx.reshape(1, S, D), *[out[n][0] for n in names], *[out[n][1] for n in names],
            *[out[n][2] for n in names], *[out[n][3] for n in names])
```

```python
import functools

import jax
import jax.numpy as jnp
from jax import lax
from jax.experimental import pallas as pl
from jax.experimental.pallas import tpu as pltpu

F32, BF16 = jnp.float32, jnp.bfloat16
MESH = pl.DeviceIdType.MESH
ANY = pl.BlockSpec(memory_space=pl.ANY)

NDEV = 8
S = 2048
D = 2048
FF = 5632
WIN_SHARD = 2 * FF // NDEV
WOUT_SHARD = FF // NDEV
D_POOL = 1024
PG = 256
POOL_WINDOWS = (2, 4, 8, 16)
H = 4
DK = 128
DV = 256
CH = 64
NCH = S // CH
RANK = 16
RANK_PAD = 128
D_IN = 4112
D_IN_PAD = 4224
MIX_SHARD = D_IN // NDEV
O_Q, O_K, O_V, O_G, O_R = 1024, 1536, 2048, 3072, 4096
GATE_NORM = 16.0
QK_SCALE = DK ** -0.5
EPS = 1e-6
ADAM_LR, ADAM_B1, ADAM_B2, ADAM_EPS, ADAM_WD, ADAM_STEP = 0.001, 0.9, 0.999, 1e-08, 0.01, 10
V7X_VMEM_BYTES = 64 << 20

SMALL = (("ffn1_norm", 2048), ("mix_norm", 2048), ("ffn2_norm", 2048), ("final_norm", 2048),
         ("pool_scale", 1024), ("b_alpha", 512), ("gla_norm", 256))
SMALL_ROWS = 80
LOSS_AT = sum(n for _, n in SMALL)


def _params(vmem_mb, sem=None):
    return pltpu.CompilerParams(dimension_semantics=sem, vmem_limit_bytes=min(vmem_mb << 20, V7X_VMEM_BYTES - (4 << 20)))


def _dot(a, b):
    return jnp.dot(a, b, preferred_element_type=F32)


def _dot_nt(a, b):
    return lax.dot_general(a, b, (((1,), (1,)), ((), ())), preferred_element_type=F32)


def _dot_tn(a, b):
    return lax.dot_general(a, b, (((0,), (0,)), ((), ())), preferred_element_type=F32)


def _sigmoid(x):
    return 1.0 / (1.0 + jnp.exp(-x))


def _log_sigmoid(x):
    return jnp.minimum(x, 0.0) - jnp.log(1.0 + jnp.exp(-jnp.abs(x)))


def _rms(x, g):
    r = lax.rsqrt(jnp.mean(x * x, axis=-1, keepdims=True) + EPS)
    return x * r * g


def _rms_bwd(dn, x, g):
    r = lax.rsqrt(jnp.mean(x * x, axis=-1, keepdims=True) + EPS)
    xh = x * r
    dxh = dn * g
    dx = r * (dxh - xh * jnp.mean(dxh * xh, axis=-1, keepdims=True))
    return dx, jnp.sum(dn * xh, axis=0, keepdims=True)


ROWS = 64


def _row_loop(total, fn, init=0):
    def step(t, carry):
        return fn(pl.ds(pl.multiple_of(t * ROWS, ROWS), ROWS), carry)
    return lax.fori_loop(0, total // ROWS, step, init)


def _split3(x):
    hi = x.astype(BF16)
    r1 = x - hi.astype(F32)
    mid = r1.astype(BF16)
    lo = (r1 - mid.astype(F32)).astype(BF16)
    return hi, mid, lo


def _tri_dot(tri_b, x):
    hi, mid, lo = _split3(x)
    return (_dot(tri_b, lo) + _dot(tri_b, mid)) + _dot(tri_b, hi)


FFN_TS, FFN_TF = 512, 512


def _ffn_specs():
    wg = pl.BlockSpec((D, FFN_TF), lambda i, j: (0, j))
    wu = pl.BlockSpec((D, FFN_TF), lambda i, j: (0, FF // FFN_TF + j))
    wo = pl.BlockSpec((FFN_TF, D), lambda i, j: (j, 0))
    row = pl.BlockSpec((FFN_TS, D), lambda i, j: (i, 0))
    vec = pl.BlockSpec((1, D), lambda i, j: (0, 0))
    gu = pl.BlockSpec((2, FFN_TS, FFN_TF), lambda i, j: (0, i, j))
    return wg, wu, wo, row, vec, gu


def _ffn_fwd(h, g, w_in8, w_out, name):
    nj = FF // FFN_TF
    wg, wu, wo, row, vec, gu = _ffn_specs()

    def body(h_ref, g_ref, wg_ref, wu_ref, wo_ref, ho_ref, n_ref, gu_ref, acc_ref):
        j = pl.program_id(1)

        @pl.when(j == 0)
        def _():
            def norm(rows, c):
                n_ref[rows, :] = _rms(h_ref[rows, :], g_ref[...]).astype(BF16)
                return c
            _row_loop(FFN_TS, norm)
            acc_ref[...] = jnp.zeros_like(acc_ref)

        n = n_ref[...]
        gate = _dot(n, wg_ref[...])
        up = _dot(n, wu_ref[...])
        gu_ref[0] = gate.astype(BF16)
        gu_ref[1] = up.astype(BF16)
        a = (gate * _sigmoid(gate)) * up
        acc_ref[...] += _dot(a.astype(BF16), wo_ref[...])

        @pl.when(j == nj - 1)
        def _():
            def residual(rows, c):
                ho_ref[rows, :] = h_ref[rows, :] + 0.5 * acc_ref[rows, :]
                return c
            _row_loop(FFN_TS, residual)

    return pl.pallas_call(
        body, grid=(S // FFN_TS, nj), name=name,
        in_specs=[row, vec, wg, wu, wo], out_specs=[row, row, gu],
        out_shape=[jax.ShapeDtypeStruct((S, D), F32), jax.ShapeDtypeStruct((S, D), BF16),
                   jax.ShapeDtypeStruct((2, S, FF), BF16)],
        scratch_shapes=[pltpu.VMEM((FFN_TS, D), F32)],
        compiler_params=_params(56, ("arbitrary", "arbitrary")),
    )(h, g, w_in8, w_in8, w_out)


def _ffn_bwd_x(dhp, h, g, gu_arr, w_in8, w_out, name):
    ni, nj = S // FFN_TS, FF // FFN_TF
    wg, wu, wo, row, vec, gu = _ffn_specs()
    act = pl.BlockSpec((FFN_TS, FFN_TF), lambda i, j: (i, j))

    def body(dhp_ref, h_ref, g_ref, gu_ref, wg_ref, wu_ref, wo_ref,
             dgu_ref, a_ref, df_ref, dh_ref, dhb_ref, dg_ref, acc_ref):
        i, j = pl.program_id(0), pl.program_id(1)

        @pl.when(j == 0)
        def _():
            def half(rows, c):
                df_ref[rows, :] = (0.5 * dhp_ref[rows, :]).astype(BF16)
                return c
            _row_loop(FFN_TS, half)
            acc_ref[...] = jnp.zeros_like(acc_ref)

        gate = gu_ref[0].astype(F32)
        up = gu_ref[1].astype(F32)
        da = _dot_nt(df_ref[...], wo_ref[...])
        sg = _sigmoid(gate)
        silu = gate * sg
        dgate = (da * up * (sg * (1.0 + gate * (1.0 - sg)))).astype(BF16)
        dup = (da * silu).astype(BF16)
        a_ref[...] = (silu * up).astype(BF16)
        dgu_ref[0] = dgate
        dgu_ref[1] = dup
        acc_ref[...] += _dot_nt(dgate, wg_ref[...]) + _dot_nt(dup, wu_ref[...])

        @pl.when(j == nj - 1)
        def _():
            def norm_bwd(rows, dg):
                dx, dg_rows = _rms_bwd(acc_ref[rows, :], h_ref[rows, :], g_ref[...])
                dh = dhp_ref[rows, :] + dx
                dh_ref[rows, :] = dh
                dhb_ref[rows, :] = dh.astype(BF16)
                return dg + dg_rows
            dg = _row_loop(FFN_TS, norm_bwd, jnp.zeros((1, D), F32))

            @pl.when(i == 0)
            def _():
                dg_ref[...] = dg

            @pl.when(i > 0)
            def _():
                dg_ref[...] += dg

    return pl.pallas_call(
        body, grid=(ni, nj), name=name,
        in_specs=[row, row, vec, gu, wg, wu, wo],
        out_specs=[gu, act, row, row, row, vec],
        out_shape=[jax.ShapeDtypeStruct((2, S, FF), BF16), jax.ShapeDtypeStruct((S, FF), BF16),
                   jax.ShapeDtypeStruct((S, D), BF16), jax.ShapeDtypeStruct((S, D), F32),
                   jax.ShapeDtypeStruct((S, D), BF16), jax.ShapeDtypeStruct((1, D), F32)],
        scratch_shapes=[pltpu.VMEM((FFN_TS, D), F32)],
        compiler_params=_params(58, ("arbitrary", "arbitrary")),
    )(dhp, h, g, gu_arr, w_in8, w_in8, w_out)


def _tn_matmul(a, b, a_spec, b_spec, out_shape, out_spec, grid, name, vmem_mb):
    def body(a_ref, b_ref, o_ref):
        o_ref[...] = _dot_tn(a_ref[...], b_ref[...]).astype(o_ref.dtype)

    return pl.pallas_call(body, grid=grid, name=name, in_specs=[a_spec, b_spec], out_specs=out_spec,
                          out_shape=out_shape, compiler_params=_params(vmem_mb))(a, b)


def _norm_matmul(h, g, w, name, ts=512, tn=1408):
    n_out = w.shape[1]

    def body(h_ref, g_ref, w_ref, u_ref, n_ref):
        @pl.when(pl.program_id(1) == 0)
        def _():
            def norm(rows, c):
                n_ref[rows, :] = _rms(h_ref[rows, :], g_ref[...]).astype(BF16)
                return c
            _row_loop(ts, norm)

        u_ref[...] = _dot(n_ref[...], w_ref[...])

    return pl.pallas_call(
        body, grid=(S // ts, n_out // tn), name=name,
        in_specs=[pl.BlockSpec((ts, D), lambda i, j: (i, 0)), pl.BlockSpec((1, D), lambda i, j: (0, 0)),
                  pl.BlockSpec((D, tn), lambda i, j: (0, j))],
        out_specs=[pl.BlockSpec((ts, tn), lambda i, j: (i, j)), pl.BlockSpec((ts, D), lambda i, j: (i, 0))],
        out_shape=[jax.ShapeDtypeStruct((S, n_out), F32), jax.ShapeDtypeStruct((S, D), BF16)],
        compiler_params=_params(48, ("arbitrary", "arbitrary")),
    )(h, g, w)


def _matmul_residual(a, w, res, name, ts=512, tn=1024):
    k, n_out = w.shape

    def body(a_ref, w_ref, r_ref, o_ref):
        o_ref[...] = r_ref[...] + _dot(a_ref[...], w_ref[...])

    return pl.pallas_call(
        body, grid=(S // ts, n_out // tn), name=name,
        in_specs=[pl.BlockSpec((ts, k), lambda i, j: (i, 0)), pl.BlockSpec((k, tn), lambda i, j: (0, j)),
                  pl.BlockSpec((ts, tn), lambda i, j: (i, j))],
        out_specs=pl.BlockSpec((ts, tn), lambda i, j: (i, j)),
        out_shape=jax.ShapeDtypeStruct((S, n_out), F32),
        compiler_params=_params(40),
    )(a, w, res)


def _nt_matmul(a, w, name, ts=512, tn=1024):
    n_out, k = w.shape

    def body(a_ref, w_ref, o_ref):
        o_ref[...] = _dot_nt(a_ref[...], w_ref[...])

    return pl.pallas_call(
        body, grid=(S // ts, n_out // tn), name=name,
        in_specs=[pl.BlockSpec((ts, k), lambda i, j: (i, 0)), pl.BlockSpec((tn, k), lambda i, j: (j, 0))],
        out_specs=pl.BlockSpec((ts, tn), lambda i, j: (i, j)),
        out_shape=jax.ShapeDtypeStruct((S, n_out), F32),
        compiler_params=_params(40),
    )(a, w)


def _nt_matmul_normbwd(du, w, h, g, dres, name, ts=512, tn=1408):
    n_in = w.shape[1]
    nj = n_in // tn
    row = pl.BlockSpec((ts, D), lambda i, j: (i, 0))
    vec = pl.BlockSpec((1, D), lambda i, j: (0, 0))

    def body(du_ref, w_ref, h_ref, g_ref, dres_ref, dh_ref, dg_ref, acc_ref):
        i, j = pl.program_id(0), pl.program_id(1)

        @pl.when(j == 0)
        def _():
            acc_ref[...] = jnp.zeros_like(acc_ref)

        acc_ref[...] += _dot_nt(du_ref[...], w_ref[...])

        @pl.when(j == nj - 1)
        def _():
            def norm_bwd(rows, dg):
                dx, dg_rows = _rms_bwd(acc_ref[rows, :], h_ref[rows, :], g_ref[...])
                dh_ref[rows, :] = dres_ref[rows, :] + dx
                return dg + dg_rows
            dg = _row_loop(ts, norm_bwd, jnp.zeros((1, D), F32))

            @pl.when(i == 0)
            def _():
                dg_ref[...] = dg

            @pl.when(i > 0)
            def _():
                dg_ref[...] += dg

    return pl.pallas_call(
        body, grid=(S // ts, nj), name=name,
        in_specs=[pl.BlockSpec((ts, tn), lambda i, j: (i, j)), pl.BlockSpec((D, tn), lambda i, j: (0, j)), row, vec, row],
        out_specs=[row, vec],
        out_shape=[jax.ShapeDtypeStruct((S, D), F32), jax.ShapeDtypeStruct((1, D), F32)],
        scratch_shapes=[pltpu.VMEM((ts, D), F32)],
        compiler_params=_params(52, ("arbitrary", "arbitrary")),
    )(du, w, h, g, dres)


def _loss_head(h, g, tgt, ts=256):
    row = pl.BlockSpec((ts, D), lambda i: (i, 0))
    vec = pl.BlockSpec((1, D), lambda i: (0, 0))

    def body(h_ref, g_ref, t_ref, dh_ref, dg_ref, loss_ref):
        i = pl.program_id(0)

        def rows_fn(rows, carry):
            dg, part = carry
            x = h_ref[rows, :]
            gv = g_ref[...]
            err = _rms(x, gv) - t_ref[rows, :]
            part = part + 0.5 * jnp.sum(jnp.mean(err * err, axis=-1, keepdims=True), axis=0, keepdims=True)
            dx, dg_rows = _rms_bwd(err * (1.0 / D), x, gv)
            dh_ref[rows, :] = dx
            return dg + dg_rows, part
        dg, part = _row_loop(ts, rows_fn, (jnp.zeros((1, D), F32), jnp.zeros((1, 1), F32)))

        @pl.when(i == 0)
        def _():
            dg_ref[...] = dg
            loss_ref[...] = jnp.broadcast_to(part, loss_ref.shape)

        @pl.when(i > 0)
        def _():
            dg_ref[...] += dg
            loss_ref[...] += jnp.broadcast_to(part, loss_ref.shape)

    return pl.pallas_call(
        body, grid=(S // ts,), name="loss_head",
        in_specs=[row, vec, row], out_specs=[row, vec, pl.BlockSpec((1, 128), lambda i: (0, 0))],
        out_shape=[jax.ShapeDtypeStruct((S, D), F32), jax.ShapeDtypeStruct((1, D), F32),
                   jax.ShapeDtypeStruct((1, 128), F32)],
        compiler_params=_params(40, ("arbitrary",)),
    )(h, g, tgt)


def _pool_specs():
    blk = pl.BlockSpec((S, PG), lambda gi: (0, gi))
    wp = pl.BlockSpec((None, PG, PG), lambda gi: (gi, 0, 0))
    sc = pl.BlockSpec((1, PG), lambda gi: (0, gi))
    return blk, wp, sc


def _pool_fwd(u, wp_b, scale):
    blk, wp, sc = _pool_specs()

    def body(u_ref, wp_ref, sc_ref, y_ref, pooled_ref):
        win = 2 << pl.program_id(0)
        row = lax.broadcasted_iota(jnp.int32, (S, PG), 0)
        x = u_ref[...]
        s = x
        for k in (1, 2, 4, 8):
            s = s + jnp.where((row >= k) & (k < win), pltpu.roll(s, k, 0), 0.0)
        cnt = jnp.minimum(row + 1, win).astype(F32)
        pooled = (s / cnt - x).astype(BF16)
        pooled_ref[...] = pooled
        y_ref[...] = (_dot(pooled, wp_ref[...]) * sc_ref[...]).astype(BF16)

    return pl.pallas_call(
        body, grid=(len(POOL_WINDOWS),), name="pool_fwd", in_specs=[blk, wp, sc], out_specs=[blk, blk],
        out_shape=[jax.ShapeDtypeStruct((S, D_POOL), BF16), jax.ShapeDtypeStruct((S, D_POOL), BF16)],
        compiler_params=_params(40),
    )(u, wp_b, scale)


def _pool_bwd(dy, pooled, wp_b, scale):
    blk, wp, sc = _pool_specs()

    def body(dy_ref, p_ref, wp_ref, sc_ref, du_ref, dwp_ref, dsc_ref):
        win = 2 << pl.program_id(0)
        row = lax.broadcasted_iota(jnp.int32, (S, PG), 0)
        dyv = dy_ref[...]
        pooled = p_ref[...]
        w = wp_ref[...]
        dsc_ref[...] = jnp.sum(dyv * _dot(pooled, w), axis=0, keepdims=True)
        dz = (dyv * sc_ref[...]).astype(BF16)
        dwp_ref[...] = _dot_tn(pooled, dz)
        dpooled = _dot_nt(dz, w)
        cnt = jnp.minimum(row + 1, win).astype(F32)
        fs = dpooled / cnt
        for k in (1, 2, 4, 8):
            fs = fs + jnp.where((row < S - k) & (k < win), pltpu.roll(fs, S - k, 0), 0.0)
        du_ref[...] = (fs - dpooled).astype(BF16)

    return pl.pallas_call(
        body, grid=(len(POOL_WINDOWS),), name="pool_bwd", in_specs=[blk, blk, wp, sc], out_specs=[blk, wp, sc],
        out_shape=[jax.ShapeDtypeStruct((S, D_POOL), BF16), jax.ShapeDtypeStruct((len(POOL_WINDOWS), PG, PG), F32),
                   jax.ShapeDtypeStruct((1, D_POOL), F32)],
        compiler_params=_params(40),
    )(dy, pooled, wp_b, scale)


def _gla_in_specs(chunk_of):
    def at(width, col):
        return pl.BlockSpec((CH, width), lambda n: (chunk_of(n), col))
    return [at(H * DK, O_Q // (H * DK)), at(H * DK, O_K // (H * DK)), at(H * DV, O_V // (H * DV)),
            at(H * DV, O_G // (H * DV)), at(RANK_PAD, O_R // RANK_PAD)]


def _gla_decay_terms(lr_ref, wa_ref, ba_ref, q_ref, k_ref):
    row = lax.broadcasted_iota(jnp.int32, (CH, CH), 0)
    col = lax.broadcasted_iota(jnp.int32, (CH, CH), 1)
    tril = row >= col
    z = _dot(lr_ref[...].astype(BF16), wa_ref[...]) + ba_ref[...]
    la = _log_sigmoid(z) / GATE_NORM
    b = _tri_dot(jnp.where(tril, 1.0, 0.0).astype(BF16), la)
    bl = jnp.sum(la, axis=0, keepdims=True)
    e_b, e_nb, e_tb = jnp.exp(b), jnp.exp(-b), jnp.exp(bl - b)
    kk = k_ref[...]
    q_dec = (q_ref[...] * QK_SCALE) * e_b
    return tril, z, e_b, e_nb, e_tb, jnp.exp(bl), q_dec, kk * e_nb, kk * e_tb


def _gla_fwd(u, wa_b, ba, gn):
    wide = pl.BlockSpec((CH, H * DV), lambda n: (n, 0))

    def body(q_ref, k_ref, v_ref, g_ref, lr_ref, wa_ref, ba_ref, gn_ref, y_ref, o_ref, st_ref, state):
        @pl.when(pl.program_id(0) == 0)
        def _():
            state[...] = jnp.zeros_like(state)

        tril, _, _, _, _, dec, q_dec, k_inv, k_tail = _gla_decay_terms(lr_ref, wa_ref, ba_ref, q_ref, k_ref)
        for hd in range(H):
            ks, vs = slice(hd * DK, (hd + 1) * DK), slice(hd * DV, (hd + 1) * DV)
            qb, kib, ktb = q_dec[:, ks].astype(BF16), k_inv[:, ks].astype(BF16), k_tail[:, ks].astype(BF16)
            vb = v_ref[:, vs].astype(BF16)
            p = jnp.where(tril, _dot_nt(qb, kib), 0.0)
            st = state[hd]
            o = _dot(p.astype(BF16), vb) + _dot_nt(qb, st.astype(BF16))
            st_ref[hd] = st
            state[hd] = st * dec[:, ks] + _dot_tn(vb, ktb)
            o_ref[:, vs] = o
            on = _rms(o, gn_ref[...])
            gg = g_ref[:, vs]
            y_ref[:, vs] = (on * (gg * _sigmoid(gg))).astype(BF16)

    return pl.pallas_call(
        body, grid=(NCH,), name="gla_fwd",
        in_specs=_gla_in_specs(lambda n: n) + [pl.BlockSpec((RANK_PAD, H * DK), lambda n: (0, 0)),
                                               pl.BlockSpec((1, H * DK), lambda n: (0, 0)),
                                               pl.BlockSpec((1, DV), lambda n: (0, 0))],
        out_specs=[wide, wide, pl.BlockSpec((None, H, DV, DK), lambda n: (n, 0, 0, 0))],
        out_shape=[jax.ShapeDtypeStruct((S, H * DV), BF16), jax.ShapeDtypeStruct((S, H * DV), F32),
                   jax.ShapeDtypeStruct((NCH, H, DV, DK), F32)],
        scratch_shapes=[pltpu.VMEM((H, DV, DK), F32)],
        compiler_params=_params(32, ("arbitrary",)),
    )(u, u, u, u, u, wa_b, ba, gn)


GLA_DU = 2 * H * DK + 2 * H * DV + RANK_PAD


def _gla_bwd(u, o_arr, states, dy, wa_b, ba, gn):
    rev = lambda n: NCH - 1 - n
    wide = pl.BlockSpec((CH, H * DV), lambda n: (rev(n), 0))

    def body(q_ref, k_ref, v_ref, g_ref, lr_ref, o_ref, st_ref, dy_ref, wa_ref, ba_ref, gn_ref,
             du_ref, dwa_ref, dba_ref, dgn_ref, gstate, db_scr, dbl_scr):
        @pl.when(pl.program_id(0) == 0)
        def _():
            gstate[...] = jnp.zeros_like(gstate)
            dwa_ref[...] = jnp.zeros_like(dwa_ref)
            dba_ref[...] = jnp.zeros_like(dba_ref)
            dgn_ref[...] = jnp.zeros_like(dgn_ref)

        tril, z, e_b, e_nb, e_tb, dec, q_dec, k_inv, k_tail = _gla_decay_terms(lr_ref, wa_ref, ba_ref, q_ref, k_ref)
        gnv = gn_ref[...]
        dgn = jnp.zeros((1, DV), F32)
        for hd in range(H):
            ks, vs = slice(hd * DK, (hd + 1) * DK), slice(hd * DV, (hd + 1) * DV)
            qh, kih, kth = q_dec[:, ks], k_inv[:, ks], k_tail[:, ks]
            qb, kib, ktb = qh.astype(BF16), kih.astype(BF16), kth.astype(BF16)
            vb = v_ref[:, vs].astype(BF16)
            o = o_ref[:, vs]
            gg = g_ref[:, vs]
            dyh = dy_ref[:, vs]
            r = lax.rsqrt(jnp.mean(o * o, axis=-1, keepdims=True) + EPS)
            xh = o * r
            sg = _sigmoid(gg)
            dgate = dyh * (xh * gnv) * (sg * (1.0 + gg * (1.0 - sg)))
            don = dyh * (gg * sg)
            dgn = dgn + jnp.sum(don * xh, axis=0, keepdims=True)
            dxh = don * gnv
            d_o = (r * (dxh - xh * jnp.mean(dxh * xh, axis=-1, keepdims=True))).astype(BF16)
            pb = jnp.where(tril, _dot_nt(qb, kib), 0.0).astype(BF16)
            dpb = jnp.where(tril, _dot_nt(d_o, vb), 0.0).astype(BF16)
            gt = gstate[hd]
            gtb = gt.astype(BF16)
            st = st_ref[hd]
            dv = _dot_tn(pb, d_o) + _dot_nt(ktb, gtb)
            dq_dec = _dot(dpb, kib) + _dot(d_o, st.astype(BF16))
            dk_inv = _dot_tn(dpb, qb)
            dk_tail = _dot(vb, gtb)
            ddec = jnp.sum(gt * st, axis=0, keepdims=True)
            gstate[hd] = _dot_tn(d_o, qb) + dec[:, ks] * gt
            du_ref[:, ks] = (dq_dec * QK_SCALE * e_b[:, ks]).astype(BF16)
            du_ref[:, H * DK + hd * DK:H * DK + (hd + 1) * DK] = (dk_inv * e_nb[:, ks] + dk_tail * e_tb[:, ks]).astype(BF16)
            du_ref[:, 2 * H * DK + hd * DV:2 * H * DK + (hd + 1) * DV] = dv.astype(BF16)
            du_ref[:, 2 * H * DK + H * DV + hd * DV:2 * H * DK + H * DV + (hd + 1) * DV] = dgate.astype(BF16)
            db_scr[:, ks] = dq_dec * qh - dk_inv * kih - dk_tail * kth
            dbl_scr[:, ks] = jnp.sum(dk_tail * kth, axis=0, keepdims=True) + ddec * dec[:, ks]
        dgn_ref[...] += dgn
        row = lax.broadcasted_iota(jnp.int32, (CH, CH), 0)
        col = lax.broadcasted_iota(jnp.int32, (CH, CH), 1)
        dla = _tri_dot(jnp.where(row <= col, 1.0, 0.0).astype(BF16), db_scr[...]) + dbl_scr[...]
        dz = dla * (1.0 / GATE_NORM) * _sigmoid(-z)
        dzb = dz.astype(BF16)
        du_ref[:, GLA_DU - RANK_PAD:] = _dot_nt(dzb, wa_ref[...]).astype(BF16)
        dwa_ref[...] += _dot_tn(lr_ref[...].astype(BF16), dzb)
        dba_ref[...] += jnp.sum(dz, axis=0, keepdims=True)

    full = lambda shape: pl.BlockSpec(shape, lambda n: (0,) * len(shape))
    return pl.pallas_call(
        body, grid=(NCH,), name="gla_bwd",
        in_specs=_gla_in_specs(rev) + [wide, pl.BlockSpec((None, H, DV, DK), lambda n: (rev(n), 0, 0, 0)),
                                       pl.BlockSpec((CH, H * DV), lambda n: (rev(n), 1)),
                                       full((RANK_PAD, H * DK)), full((1, H * DK)), full((1, DV))],
        out_specs=[pl.BlockSpec((CH, GLA_DU), lambda n: (rev(n), 0)), full((RANK_PAD, H * DK)), full((1, H * DK)),
                   full((1, DV))],
        out_shape=[jax.ShapeDtypeStruct((S, GLA_DU), BF16), jax.ShapeDtypeStruct((RANK_PAD, H * DK), F32),
                   jax.ShapeDtypeStruct((1, H * DK), F32), jax.ShapeDtypeStruct((1, DV), F32)],
        scratch_shapes=[pltpu.VMEM((H, DV, DK), F32), pltpu.VMEM((CH, H * DK), F32), pltpu.VMEM((1, H * DK), F32)],
        compiler_params=_params(32, ("arbitrary",)),
    )(u, u, u, u, u, o_arr, states, dy, wa_b, ba, gn)


def _local_step(x, tgt, w):
    h1, n1, gu1 = _ffn_fwd(x, w["ffn1_norm"], w["ffn1_w_in"], w["ffn1_w_out"], "ffn1_fwd")
    u, n2 = _norm_matmul(h1, w["mix_norm"], w["w_in_mix"], "mix_in")
    y_pool, pooled = _pool_fwd(u, w["w_pool"], w["pool_scale"])
    y_gla, o_gla, states = _gla_fwd(u, w["w_alpha"], w["b_alpha"], w["gla_norm"])
    y = jnp.concatenate([y_pool, y_gla], axis=1)
    h2 = _matmul_residual(y, w["w_out_mix"], h1, "mix_out")
    h3, n3, gu3 = _ffn_fwd(h2, w["ffn2_norm"], w["ffn2_w_in"], w["ffn2_w_out"], "ffn2_fwd")
    dh3, d_final, loss = _loss_head(h3, w["final_norm"], tgt)

    slot_rows = pl.BlockSpec((S, 512), lambda s, m: (0, m))
    win_b = pl.BlockSpec((None, S, WIN_SHARD), lambda s, m: (s // (NDEV // 2), 0, s % (NDEV // 2)))
    win_o = pl.BlockSpec((512, WIN_SHARD), lambda s, m: (m, s))

    def ffn_weight_grads(n, dgu, act, df, tag):
        d_in = _tn_matmul(n, dgu, slot_rows, win_b, jax.ShapeDtypeStruct((D, 2 * FF), BF16), win_o,
                          (NDEV, D // 512), tag + "_dw_in", 32)
        d_out = _tn_matmul(act, df, pl.BlockSpec((S, 512), lambda m: (0, m)), pl.BlockSpec((S, D), lambda m: (0, 0)),
                           jax.ShapeDtypeStruct((FF, D), BF16), pl.BlockSpec((512, D), lambda m: (m, 0)),
                           (FF // 512,), tag + "_dw_out", 40)
        return d_in, d_out

    dgu3, act3, df3, dh2, dh2b, d_ffn2 = _ffn_bwd_x(dh3, h2, w["ffn2_norm"], gu3, w["ffn2_w_in"], w["ffn2_w_out"], "ffn2_bwd")
    d_w2in, d_w2out = ffn_weight_grads(n3, dgu3, act3, df3, "ffn2")
    dy = _nt_matmul(dh2b, w["w_out_mix"], "mix_out_bwd")
    d_wo = _tn_matmul(y, dh2b, pl.BlockSpec((S, 512), lambda m: (0, m)), pl.BlockSpec((S, D), lambda m: (0, 0)),
                      jax.ShapeDtypeStruct((D, D), BF16), pl.BlockSpec((512, D), lambda m: (m, 0)), (D // 512,),
                      "mix_out_dw", 40)
    du_pool, d_wpool, d_scale = _pool_bwd(dy, pooled, w["w_pool"], w["pool_scale"])
    du_gla, d_walpha, d_balpha, d_gnorm = _gla_bwd(u, o_gla, states, dy, w["w_alpha"], w["b_alpha"], w["gla_norm"])
    du = jnp.concatenate([du_pool, du_gla], axis=1)
    dh1, d_mix = _nt_matmul_normbwd(du, w["w_in_mix"], h1, w["mix_norm"], dh2, "mix_in_bwd")
    d_wmix = _tn_matmul(n2, du, pl.BlockSpec((S, 512), lambda j, m: (0, m)), pl.BlockSpec((S, 1408), lambda j, m: (0, j)),
                        jax.ShapeDtypeStruct((D, D_IN_PAD), BF16), pl.BlockSpec((512, 1408), lambda j, m: (m, j)),
                        (D_IN_PAD // 1408, D // 512), "mix_in_dw", 32)
    dgu1, act1, df1, dx, _, d_ffn1 = _ffn_bwd_x(dh1, x, w["ffn1_norm"], gu1, w["ffn1_w_in"], w["ffn1_w_out"], "ffn1_bwd")
    d_w1in, d_w1out = ffn_weight_grads(n1, dgu1, act1, df1, "ffn1")

    grads = dict(ffn1_norm=d_ffn1, ffn1_w_in=d_w1in, ffn1_w_out=d_w1out, mix_norm=d_mix, w_in_mix=d_wmix,
                 w_pool=d_wpool, pool_scale=d_scale, w_alpha=d_walpha, b_alpha=d_balpha, gla_norm=d_gnorm,
                 w_out_mix=d_wo, ffn2_norm=d_ffn2, ffn2_w_in=d_w2in, ffn2_w_out=d_w2out, final_norm=d_final)
    return loss, dx, grads


def _mesh_index():
    return 4 * lax.axis_index("x") + 2 * lax.axis_index("y") + lax.axis_index("c")


def _coords(p):
    return (p // 4, (p // 2) % 2, p % 2)


NCHIP = 4


def _place():
    return lax.axis_index("x"), lax.axis_index("y"), lax.axis_index("c")


def _rel_chip(x, y, rel):
    return ((1 - x) if rel & 1 else x, (1 - y) if rel & 2 else y)


def _dev_index(x, y, c):
    return 4 * x + 2 * y + c


def _cols(ref, p, width):
    return ref.at[:, pl.ds(pl.multiple_of(p * width, 128), width)]


def _sems(na, n):
    return [pltpu.SemaphoreType.DMA((na, n)), pltpu.SemaphoreType.DMA((na, n)), pltpu.SemaphoreType.DMA((na,))]


def _gather(items, name):
    arrays = [a for a, _ in items]
    kinds = [k for _, k in items]
    na = len(arrays)
    out_shape = [jax.ShapeDtypeStruct((NDEV,) + a.shape if k == "bcast" else (a.shape[0], NDEV * a.shape[1]), a.dtype)
                 for a, k in items]

    def body(*refs):
        ins, outs = refs[:na], refs[na:2 * na]
        send_sems, recv_sems, local_sems = refs[2 * na:]
        x, y, c = _place()
        sibling = (x, y, 1 - c)

        def slab(a, s):
            return _cols(outs[a], s, ins[a].shape[1]) if kinds[a] == "bcast_cols" else outs[a].at[s]

        def copy(a, k, block, to, own=False):
            return pltpu.make_async_remote_copy(ins[a] if own else slab(a, block), slab(a, block), send_sems.at[a, k],
                                                recv_sems.at[a, k], device_id=to, device_id_type=MESH)

        me = _dev_index(x, y, c)
        local = [pltpu.make_async_copy(ins[a], slab(a, me), local_sems.at[a]) for a in range(na)]
        sent = []
        for cp in local:
            cp.start()
        for rel in range(1, NCHIP):
            for a in range(na):
                sent.append(copy(a, rel, me, (*_rel_chip(x, y, rel), c), own=True))
                sent[-1].start()
        for a in range(na):
            sent.append(copy(a, 0, me, sibling, own=True))
            sent[-1].start()
        for rel in range(1, NCHIP):
            src_dev = _dev_index(*_rel_chip(x, y, rel), c)
            for a in range(na):
                copy(a, rel, src_dev, sibling).wait_recv()
                sent.append(copy(a, NCHIP - 1 + rel, src_dev, sibling))
                sent[-1].start()
        for a in range(na):
            copy(a, 0, _dev_index(x, y, 1 - c), sibling).wait_recv()
        for rel in range(1, NCHIP):
            for a in range(na):
                copy(a, NCHIP - 1 + rel, _dev_index(*_rel_chip(x, y, rel), 1 - c), sibling).wait_recv()
        for cp in sent:
            cp.wait_send()
        for cp in local:
            cp.wait()

    return pl.pallas_call(body, name=name, in_specs=[ANY] * na, out_specs=[ANY] * na, out_shape=out_shape,
                          scratch_shapes=_sems(na, 2 * NCHIP - 1))(*arrays)


def _pair_exchange(items, name):
    arrays = [a for a, _ in items]
    kinds = [k for _, k in items]
    na = len(arrays)
    out_shape = [jax.ShapeDtypeStruct((NCHIP,) + (a.shape[1:] if k == "scatter" else (a.shape[0], a.shape[1] // NDEV)), a.dtype)
                 for a, k in items]

    def body(*refs):
        ins, outs = refs[:na], refs[na:2 * na]
        send_sems, recv_sems = refs[2 * na:]
        x, y, c = _place()
        copies = []
        for rel in range(NCHIP):
            p = _dev_index(*_rel_chip(x, y, rel), 1 - c)
            for a in range(na):
                src = ins[a].at[p] if kinds[a] == "scatter" else _cols(ins[a], p, ins[a].shape[1] // NDEV)
                copies.append(pltpu.make_async_remote_copy(src, outs[a].at[rel], send_sems.at[a, rel], recv_sems.at[a, rel],
                                                           device_id=(x, y, 1 - c), device_id_type=MESH))
                copies[-1].start()
        for cp in copies:
            cp.wait_send()
            cp.wait_recv()

    return pl.pallas_call(body, name=name, in_specs=[ANY] * na, out_specs=[ANY] * na, out_shape=out_shape,
                          scratch_shapes=_sems(na, NCHIP)[:2])(*arrays)


def _pair_add(own, kind, got, table, tr, name):
    _, rows, cols = got.shape
    if kind == "scatter":
        own_spec = pl.BlockSpec((None, tr, cols), lambda rel, i, t: (t[rel], i, 0))
    else:
        own_spec = pl.BlockSpec((tr, cols), lambda rel, i, t: (i, t[rel]))
    blk = pl.BlockSpec((None, tr, cols), lambda rel, i, t: (rel, i, 0))

    def body(t_ref, a_ref, b_ref, o_ref):
        o_ref[...] = (a_ref[...].astype(F32) + b_ref[...].astype(F32)).astype(o_ref.dtype)

    return pl.pallas_call(
        body, name=name, out_shape=jax.ShapeDtypeStruct(got.shape, got.dtype),
        grid_spec=pltpu.PrefetchScalarGridSpec(num_scalar_prefetch=1, grid=(NCHIP, rows // tr), in_specs=[own_spec, blk],
                                               out_specs=blk),
        compiler_params=_params(32),
    )(table, own, got)


def _exchange(items, name):
    arrays = [a for a, _ in items]
    kinds = [k for _, k in items]
    na = len(arrays)
    out_shape = [jax.ShapeDtypeStruct((NDEV,) + a.shape if k == "bcast" else a.shape, a.dtype) for a, k in items]

    def body(*refs):
        ins, outs = refs[:na], refs[na:2 * na]
        send_sems, recv_sems, local_sems = refs[2 * na:]
        x, y, c = _place()
        me = _dev_index(x, y, c)

        def mine(a):
            return 0 if kinds[a] == "chip" else me

        def src(a, p):
            return ins[a] if kinds[a] == "bcast" else ins[a].at[p]

        local = [pltpu.make_async_copy(src(a, mine(a)), outs[a].at[mine(a)], local_sems.at[a]) for a in range(na)]
        for cp in local:
            cp.start()
        waits = []
        for k in range(1, NDEV):
            to = (me + k) % NDEV
            frm = (me + NDEV - k) % NDEV
            for a in range(na):
                if kinds[a] == "chip":
                    if k >= NCHIP:
                        continue
                    send = pltpu.make_async_remote_copy(ins[a].at[k], outs[a].at[k], send_sems.at[a, k], recv_sems.at[a, k],
                                                        device_id=(*_rel_chip(x, y, k), c), device_id_type=MESH)
                    waits.append(send)
                else:
                    send = pltpu.make_async_remote_copy(src(a, to), outs[a].at[me], send_sems.at[a, k], recv_sems.at[a, k],
                                                        device_id=_coords(to), device_id_type=MESH)
                    waits.append(pltpu.make_async_remote_copy(src(a, to), outs[a].at[frm], send_sems.at[a, k],
                                                              recv_sems.at[a, k], device_id=_coords(to), device_id_type=MESH))
                send.start()
        for cp in waits:
            cp.wait_send()
            cp.wait_recv()
        for cp in local:
            cp.wait()

    return pl.pallas_call(body, name=name, in_specs=[ANY] * na, out_specs=[ANY] * na, out_shape=out_shape,
                          scratch_shapes=_sems(na, NDEV))(*arrays)


def _adamw(parts, w, m, v, tr, name):
    rows, cols = w.shape
    nparts = parts.shape[0]
    blk = pl.BlockSpec((tr, cols), lambda i: (i, 0))

    def body(p_ref, w_ref, m_ref, v_ref, g_ref, d_ref, nm_ref, nv_ref):
        g = p_ref[0].astype(F32)
        for s in range(1, nparts):
            g = g + p_ref[s].astype(F32)
        nm = ADAM_B1 * m_ref[...] + (1.0 - ADAM_B1) * g
        nv = ADAM_B2 * v_ref[...] + (1.0 - ADAM_B2) * (g * g)
        m_hat = nm / (1.0 - ADAM_B1 ** ADAM_STEP)
        v_hat = nv / (1.0 - ADAM_B2 ** ADAM_STEP)
        g_ref[...] = g
        d_ref[...] = -ADAM_LR * (m_hat / (jnp.sqrt(v_hat) + ADAM_EPS) + ADAM_WD * w_ref[...])
        nm_ref[...] = nm
        nv_ref[...] = nv

    return pl.pallas_call(
        body, grid=(rows // tr,), name=name,
        in_specs=[pl.BlockSpec((nparts, tr, cols), lambda i: (0, i, 0)), blk, blk, blk], out_specs=[blk] * 4,
        out_shape=[jax.ShapeDtypeStruct((rows, cols), F32)] * 4,
        compiler_params=_params(40),
    )(parts, w, m, v)


def _pack_small(vals, extra=None):
    flat = [vals[n].reshape(-1).astype(F32) for n, _ in SMALL]
    tail = jnp.zeros((SMALL_ROWS * 128 - LOSS_AT,), F32)
    if extra is not None:
        tail = tail.at[0].set(extra)
    return jnp.concatenate(flat + [tail]).reshape(SMALL_ROWS, 128)


def _unpack_small(packed, like):
    flat, out, at = packed.reshape(-1), {}, 0
    for n, size in SMALL:
        out[n] = flat[at:at + size].reshape(like[n].shape)
        at += size
    return out, flat[LOSS_AT]


def kernel(x, ffn1_norm, ffn1_w_in, ffn1_w_out, mix_norm, w_in_mix, w_pool, pool_scale, w_alpha, b_alpha, gla_norm, w_out_mix, ffn2_norm, ffn2_w_in, ffn2_w_out, final_norm, loss_target, m_ffn1_norm, m_ffn1_w_in, m_ffn1_w_out, m_mix_norm, m_w_in_mix, m_w_pool, m_pool_scale, m_w_alpha, m_b_alpha, m_gla_norm, m_w_out_mix, m_ffn2_norm, m_ffn2_w_in, m_ffn2_w_out, m_final_norm, v_ffn1_norm, v_ffn1_w_in, v_ffn1_w_out, v_mix_norm, v_w_in_mix, v_w_pool, v_pool_scale, v_w_alpha, v_b_alpha, v_gla_norm, v_w_out_mix, v_ffn2_norm, v_ffn2_w_in, v_ffn2_w_out, v_final_norm):
    names = ["ffn1_norm", "ffn1_w_in", "ffn1_w_out", "mix_norm", "w_in_mix", "w_pool", "pool_scale", "w_alpha", "b_alpha",
             "gla_norm", "w_out_mix", "ffn2_norm", "ffn2_w_in", "ffn2_w_out", "final_norm"]
    p = dict(zip(names, [ffn1_norm, ffn1_w_in, ffn1_w_out, mix_norm, w_in_mix, w_pool, pool_scale, w_alpha, b_alpha,
                         gla_norm, w_out_mix, ffn2_norm, ffn2_w_in, ffn2_w_out, final_norm]))
    m = dict(zip(names, [m_ffn1_norm, m_ffn1_w_in, m_ffn1_w_out, m_mix_norm, m_w_in_mix, m_w_pool, m_pool_scale, m_w_alpha,
                         m_b_alpha, m_gla_norm, m_w_out_mix, m_ffn2_norm, m_ffn2_w_in, m_ffn2_w_out, m_final_norm]))
    v = dict(zip(names, [v_ffn1_norm, v_ffn1_w_in, v_ffn1_w_out, v_mix_norm, v_w_in_mix, v_w_pool, v_pool_scale, v_w_alpha,
                         v_b_alpha, v_gla_norm, v_w_out_mix, v_ffn2_norm, v_ffn2_w_in, v_ffn2_w_out, v_final_norm]))

    big = ["ffn1_w_in", "ffn1_w_out", "w_in_mix", "w_out_mix", "ffn2_w_in", "ffn2_w_out"]
    shards = [(p[n][0].astype(BF16), "bcast_cols" if n in ("ffn1_w_in", "ffn2_w_in") else "bcast") for n in big]
    shards += [(p["w_pool"][0].reshape(H * 32, PG), "bcast"), (p["w_alpha"][0], "bcast")]
    g_w1in, g_w1out, g_wmix, g_wo, g_w2in, g_w2out, g_wpool, g_walpha = _gather(shards, "gather_weights")

    wmix = jnp.transpose(g_wmix, (1, 0, 2)).reshape(D, D_IN)
    walpha = jnp.transpose(g_walpha, (1, 0, 2)).reshape(RANK, H * DK)
    full = {
        "ffn1_w_in": g_w1in, "ffn1_w_out": g_w1out.reshape(FF, D),
        "ffn2_w_in": g_w2in, "ffn2_w_out": g_w2out.reshape(FF, D),
        "w_in_mix": jnp.pad(wmix, ((0, 0), (0, D_IN_PAD - D_IN))),
        "w_out_mix": g_wo.reshape(D, D),
        "w_pool": jnp.transpose(g_wpool.reshape(NDEV, H, 32, PG), (1, 0, 2, 3)).reshape(H, PG, PG).astype(BF16),
        "w_alpha": jnp.pad(walpha, ((0, RANK_PAD - RANK), (0, 0))).astype(BF16),
        "final_norm": final_norm.reshape(1, D),
    }
    for n in ("ffn1_norm", "mix_norm", "ffn2_norm", "pool_scale", "b_alpha", "gla_norm"):
        full[n] = p[n]

    loss_part, dx, g = _local_step(x[0], loss_target[0], full)

    d_wmix8 = jnp.transpose(g["w_in_mix"][:, :D_IN].reshape(D, NDEV, MIX_SHARD), (1, 0, 2))
    d_wpool8 = jnp.transpose(g["w_pool"].reshape(H, NDEV, 32, PG), (1, 0, 2, 3)).reshape(NDEV, H * 32, PG)
    d_walpha8 = jnp.transpose(g["w_alpha"][:RANK].reshape(RANK, NDEV, H * DK // NDEV), (1, 0, 2))
    own = [(g["ffn1_w_in"], "scatter_cols", 256), (g["ffn1_w_out"].reshape(NDEV, WOUT_SHARD, D), "scatter", WOUT_SHARD // 2),
           (d_wmix8, "scatter", 256), (g["w_out_mix"].reshape(NDEV, D // NDEV, D), "scatter", D // NDEV),
           (g["ffn2_w_in"], "scatter_cols", 256), (g["ffn2_w_out"].reshape(NDEV, WOUT_SHARD, D), "scatter", WOUT_SHARD // 2),
           (d_wpool8, "scatter", H * 32), (d_walpha8, "scatter", RANK)]
    got = _pair_exchange([(a, kind) for a, kind, _ in own], "pair_grads")
    mx, my, mc = _place()
    table = jnp.stack([_dev_index(*_rel_chip(mx, my, rel), mc) for rel in range(NCHIP)]).astype(jnp.int32)
    pre = [_pair_add(a, kind, got_a, table, tr, "pair_add_%d" % i) for i, ((a, kind, tr), got_a) in enumerate(zip(own, got))]
    parts = [(a, "chip") for a in pre] + [(_pack_small(g, loss_part[0, 0]), "bcast")]
    r_w1in, r_w1out, r_wmix, r_wo, r_w2in, r_w2out, r_wpool, r_walpha, r_small = _exchange(parts, "reduce_grads")

    def upd(parts, n, shape2d, tr):
        res = _adamw(parts, p[n].reshape(shape2d), m[n].reshape(shape2d), v[n].reshape(shape2d), tr, "adamw_" + n)
        return [r.reshape(p[n].shape) for r in res]

    out = {
        "ffn1_w_in": upd(r_w1in, "ffn1_w_in", (D, WIN_SHARD), 128),
        "ffn1_w_out": upd(r_w1out, "ffn1_w_out", (WOUT_SHARD, D), 64),
        "w_in_mix": upd(r_wmix, "w_in_mix", (D, MIX_SHARD), 128),
        "w_out_mix": upd(r_wo, "w_out_mix", (D // NDEV, D), 64),
        "ffn2_w_in": upd(r_w2in, "ffn2_w_in", (D, WIN_SHARD), 128),
        "ffn2_w_out": upd(r_w2out, "ffn2_w_out", (WOUT_SHARD, D), 64),
        "w_pool": upd(r_wpool, "w_pool", (H * 32, PG), H * 32),
        "w_alpha": upd(r_walpha, "w_alpha", (RANK, H * DK // NDEV), RANK),
    }
    small_res = _adamw(r_small, _pack_small(p), _pack_small(m), _pack_small(v), SMALL_ROWS, "adamw_small")
    unpacked = [_unpack_small(r, p) for r in small_res]
    loss = unpacked[0][1]
    for n, _ in SMALL:
        out[n] = [u[0][n] for u in unpacked]

    return (loss, dx.reshape(1, S, D), *[out[n][0] for n in names], *[out[n][1] for n in names],
            *[out[n][2] for n in names], *[out[n][3] for n in names])
```

```python
import functools

import jax
import jax.numpy as jnp
from jax import lax
from jax.experimental import pallas as pl
from jax.experimental.pallas import tpu as pltpu

F32, BF16 = jnp.float32, jnp.bfloat16
MESH = pl.DeviceIdType.MESH
ANY = pl.BlockSpec(memory_space=pl.ANY)

NDEV = 8
S = 2048
D = 2048
FF = 5632
WIN_SHARD = 2 * FF // NDEV
WOUT_SHARD = FF // NDEV
D_POOL = 1024
PG = 256
POOL_WINDOWS = (2, 4, 8, 16)
H = 4
DK = 128
DV = 256
CH = 64
NCH = S // CH
RANK = 16
RANK_PAD = 128
D_IN = 4112
D_IN_PAD = 4224
MIX_SHARD = D_IN // NDEV
O_Q, O_K, O_V, O_G, O_R = 1024, 1536, 2048, 3072, 4096
GATE_NORM = 16.0
QK_SCALE = DK ** -0.5
EPS = 1e-6
ADAM_LR, ADAM_B1, ADAM_B2, ADAM_EPS, ADAM_WD, ADAM_STEP = 0.001, 0.9, 0.999, 1e-08, 0.01, 10
V7X_VMEM_BYTES = 64 << 20

SMALL = (("ffn1_norm", 2048), ("mix_norm", 2048), ("ffn2_norm", 2048), ("final_norm", 2048),
         ("pool_scale", 1024), ("b_alpha", 512), ("gla_norm", 256))
SMALL_ROWS = 80
LOSS_AT = sum(n for _, n in SMALL)


def _params(vmem_mb, sem=None):
    return pltpu.CompilerParams(dimension_semantics=sem, vmem_limit_bytes=min(vmem_mb << 20, V7X_VMEM_BYTES - (4 << 20)))


def _dot(a, b):
    return jnp.dot(a, b, preferred_element_type=F32)


def _dot_nt(a, b):
    return lax.dot_general(a, b, (((1,), (1,)), ((), ())), preferred_element_type=F32)


def _dot_tn(a, b):
    return lax.dot_general(a, b, (((0,), (0,)), ((), ())), preferred_element_type=F32)


def _sigmoid(x):
    return 1.0 / (1.0 + jnp.exp(-x))


def _log_sigmoid(x):
    return jnp.minimum(x, 0.0) - jnp.log(1.0 + jnp.exp(-jnp.abs(x)))


def _rms(x, g):
    r = lax.rsqrt(jnp.mean(x * x, axis=-1, keepdims=True) + EPS)
    return x * r * g


def _rms_bwd(dn, x, g):
    r = lax.rsqrt(jnp.mean(x * x, axis=-1, keepdims=True) + EPS)
    xh = x * r
    dxh = dn * g
    dx = r * (dxh - xh * jnp.mean(dxh * xh, axis=-1, keepdims=True))
    return dx, jnp.sum(dn * xh, axis=0, keepdims=True)


ROWS = 64


def _row_loop(total, fn, init=0):
    def step(t, carry):
        return fn(pl.ds(pl.multiple_of(t * ROWS, ROWS), ROWS), carry)
    return lax.fori_loop(0, total // ROWS, step, init)


def _split3(x):
    hi = x.astype(BF16)
    r1 = x - hi.astype(F32)
    mid = r1.astype(BF16)
    lo = (r1 - mid.astype(F32)).astype(BF16)
    return hi, mid, lo


def _tri_dot(tri_b, x):
    hi, mid, lo = _split3(x)
    return (_dot(tri_b, lo) + _dot(tri_b, mid)) + _dot(tri_b, hi)


FFN_TS, FFN_TF = 512, 512


def _ffn_specs():
    wg = pl.BlockSpec((D, FFN_TF), lambda i, j: (0, j))
    wu = pl.BlockSpec((D, FFN_TF), lambda i, j: (0, FF // FFN_TF + j))
    wo = pl.BlockSpec((FFN_TF, D), lambda i, j: (j, 0))
    row = pl.BlockSpec((FFN_TS, D), lambda i, j: (i, 0))
    vec = pl.BlockSpec((1, D), lambda i, j: (0, 0))
    gu = pl.BlockSpec((2, FFN_TS, FFN_TF), lambda i, j: (0, i, j))
    return wg, wu, wo, row, vec, gu


def _ordered_after(body, n_in, after):
    def wrapped(*refs):
        return body(*refs[:n_in], *refs[n_in + len(after):])
    return wrapped, [ANY] * len(after)


def _ffn_fwd(h, g, w_in8, w_out, name, after=()):
    nj = FF // FFN_TF
    wg, wu, wo, row, vec, gu = _ffn_specs()

    def body(h_ref, g_ref, wg_ref, wu_ref, wo_ref, ho_ref, n_ref, gu_ref, acc_ref):
        j = pl.program_id(1)

        @pl.when(j == 0)
        def _():
            def norm(rows, c):
                n_ref[rows, :] = _rms(h_ref[rows, :], g_ref[...]).astype(BF16)
                return c
            _row_loop(FFN_TS, norm)
            acc_ref[...] = jnp.zeros_like(acc_ref)

        n = n_ref[...]
        gate = _dot(n, wg_ref[...])
        up = _dot(n, wu_ref[...])
        gu_ref[0] = gate.astype(BF16)
        gu_ref[1] = up.astype(BF16)
        a = (gate * _sigmoid(gate)) * up
        acc_ref[...] += _dot(a.astype(BF16), wo_ref[...])

        @pl.when(j == nj - 1)
        def _():
            def residual(rows, c):
                ho_ref[rows, :] = h_ref[rows, :] + 0.5 * acc_ref[rows, :]
                return c
            _row_loop(FFN_TS, residual)

    body, extra = _ordered_after(body, 5, after)
    return pl.pallas_call(
        body, grid=(S // FFN_TS, nj), name=name,
        in_specs=[row, vec, wg, wu, wo] + extra, out_specs=[row, row, gu],
        out_shape=[jax.ShapeDtypeStruct((S, D), F32), jax.ShapeDtypeStruct((S, D), BF16),
                   jax.ShapeDtypeStruct((2, S, FF), BF16)],
        scratch_shapes=[pltpu.VMEM((FFN_TS, D), F32)],
        compiler_params=_params(56, ("arbitrary", "arbitrary")),
    )(h, g, w_in8, w_in8, w_out, *after)


def _ffn_bwd_x(dhp, h, g, gu_arr, w_in8, w_out, name, after=()):
    ni, nj = S // FFN_TS, FF // FFN_TF
    wg, wu, wo, row, vec, gu = _ffn_specs()
    act = pl.BlockSpec((FFN_TS, FFN_TF), lambda i, j: (i, j))

    def body(dhp_ref, h_ref, g_ref, gu_ref, wg_ref, wu_ref, wo_ref,
             dgu_ref, a_ref, df_ref, dh_ref, dhb_ref, dg_ref, acc_ref):
        i, j = pl.program_id(0), pl.program_id(1)

        @pl.when(j == 0)
        def _():
            def half(rows, c):
                df_ref[rows, :] = (0.5 * dhp_ref[rows, :]).astype(BF16)
                return c
            _row_loop(FFN_TS, half)
            acc_ref[...] = jnp.zeros_like(acc_ref)

        gate = gu_ref[0].astype(F32)
        up = gu_ref[1].astype(F32)
        da = _dot_nt(df_ref[...], wo_ref[...])
        sg = _sigmoid(gate)
        silu = gate * sg
        dgate = (da * up * (sg * (1.0 + gate * (1.0 - sg)))).astype(BF16)
        dup = (da * silu).astype(BF16)
        a_ref[...] = (silu * up).astype(BF16)
        dgu_ref[0] = dgate
        dgu_ref[1] = dup
        acc_ref[...] += _dot_nt(dgate, wg_ref[...]) + _dot_nt(dup, wu_ref[...])

        @pl.when(j == nj - 1)
        def _():
            def norm_bwd(rows, dg):
                dx, dg_rows = _rms_bwd(acc_ref[rows, :], h_ref[rows, :], g_ref[...])
                dh = dhp_ref[rows, :] + dx
                dh_ref[rows, :] = dh
                dhb_ref[rows, :] = dh.astype(BF16)
                return dg + dg_rows
            dg = _row_loop(FFN_TS, norm_bwd, jnp.zeros((1, D), F32))

            @pl.when(i == 0)
            def _():
                dg_ref[...] = dg

            @pl.when(i > 0)
            def _():
                dg_ref[...] += dg

    body, extra = _ordered_after(body, 7, after)
    return pl.pallas_call(
        body, grid=(ni, nj), name=name,
        in_specs=[row, row, vec, gu, wg, wu, wo] + extra,
        out_specs=[gu, act, row, row, row, vec],
        out_shape=[jax.ShapeDtypeStruct((2, S, FF), BF16), jax.ShapeDtypeStruct((S, FF), BF16),
                   jax.ShapeDtypeStruct((S, D), BF16), jax.ShapeDtypeStruct((S, D), F32),
                   jax.ShapeDtypeStruct((S, D), BF16), jax.ShapeDtypeStruct((1, D), F32)],
        scratch_shapes=[pltpu.VMEM((FFN_TS, D), F32)],
        compiler_params=_params(58, ("arbitrary", "arbitrary")),
    )(dhp, h, g, gu_arr, w_in8, w_in8, w_out, *after)


def _tn_matmul(a, b, a_spec, b_spec, out_shape, out_spec, grid, name, vmem_mb):
    def body(a_ref, b_ref, o_ref):
        o_ref[...] = _dot_tn(a_ref[...], b_ref[...]).astype(o_ref.dtype)

    return pl.pallas_call(body, grid=grid, name=name, in_specs=[a_spec, b_spec], out_specs=out_spec,
                          out_shape=out_shape, compiler_params=_params(vmem_mb))(a, b)


def _norm_matmul(h, g, w, name, after=(), ts=512, tn=1408):
    n_out = w.shape[1]

    def body(h_ref, g_ref, w_ref, u_ref, n_ref):
        @pl.when(pl.program_id(1) == 0)
        def _():
            def norm(rows, c):
                n_ref[rows, :] = _rms(h_ref[rows, :], g_ref[...]).astype(BF16)
                return c
            _row_loop(ts, norm)

        u_ref[...] = _dot(n_ref[...], w_ref[...])

    body, extra = _ordered_after(body, 3, after)
    return pl.pallas_call(
        body, grid=(S // ts, n_out // tn), name=name,
        in_specs=[pl.BlockSpec((ts, D), lambda i, j: (i, 0)), pl.BlockSpec((1, D), lambda i, j: (0, 0)),
                  pl.BlockSpec((D, tn), lambda i, j: (0, j))] + extra,
        out_specs=[pl.BlockSpec((ts, tn), lambda i, j: (i, j)), pl.BlockSpec((ts, D), lambda i, j: (i, 0))],
        out_shape=[jax.ShapeDtypeStruct((S, n_out), F32), jax.ShapeDtypeStruct((S, D), BF16)],
        compiler_params=_params(48, ("arbitrary", "arbitrary")),
    )(h, g, w, *after)


def _matmul_residual(a, w, res, name, ts=512, tn=1024):
    k, n_out = w.shape

    def body(a_ref, w_ref, r_ref, o_ref):
        o_ref[...] = r_ref[...] + _dot(a_ref[...], w_ref[...])

    return pl.pallas_call(
        body, grid=(S // ts, n_out // tn), name=name,
        in_specs=[pl.BlockSpec((ts, k), lambda i, j: (i, 0)), pl.BlockSpec((k, tn), lambda i, j: (0, j)),
                  pl.BlockSpec((ts, tn), lambda i, j: (i, j))],
        out_specs=pl.BlockSpec((ts, tn), lambda i, j: (i, j)),
        out_shape=jax.ShapeDtypeStruct((S, n_out), F32),
        compiler_params=_params(40),
    )(a, w, res)


def _nt_matmul(a, w, name, after=(), ts=512, tn=1024):
    n_out, k = w.shape

    def body(a_ref, w_ref, o_ref):
        o_ref[...] = _dot_nt(a_ref[...], w_ref[...])

    body, extra = _ordered_after(body, 2, after)
    return pl.pallas_call(
        body, grid=(S // ts, n_out // tn), name=name,
        in_specs=[pl.BlockSpec((ts, k), lambda i, j: (i, 0)), pl.BlockSpec((tn, k), lambda i, j: (j, 0))] + extra,
        out_specs=pl.BlockSpec((ts, tn), lambda i, j: (i, j)),
        out_shape=jax.ShapeDtypeStruct((S, n_out), F32),
        compiler_params=_params(40),
    )(a, w, *after)


def _nt_matmul_normbwd(du, w, h, g, dres, name, ts=512, tn=1408):
    n_in = w.shape[1]
    nj = n_in // tn
    row = pl.BlockSpec((ts, D), lambda i, j: (i, 0))
    vec = pl.BlockSpec((1, D), lambda i, j: (0, 0))

    def body(du_ref, w_ref, h_ref, g_ref, dres_ref, dh_ref, dg_ref, acc_ref):
        i, j = pl.program_id(0), pl.program_id(1)

        @pl.when(j == 0)
        def _():
            acc_ref[...] = jnp.zeros_like(acc_ref)

        acc_ref[...] += _dot_nt(du_ref[...], w_ref[...])

        @pl.when(j == nj - 1)
        def _():
            def norm_bwd(rows, dg):
                dx, dg_rows = _rms_bwd(acc_ref[rows, :], h_ref[rows, :], g_ref[...])
                dh_ref[rows, :] = dres_ref[rows, :] + dx
                return dg + dg_rows
            dg = _row_loop(ts, norm_bwd, jnp.zeros((1, D), F32))

            @pl.when(i == 0)
            def _():
                dg_ref[...] = dg

            @pl.when(i > 0)
            def _():
                dg_ref[...] += dg

    return pl.pallas_call(
        body, grid=(S // ts, nj), name=name,
        in_specs=[pl.BlockSpec((ts, tn), lambda i, j: (i, j)), pl.BlockSpec((D, tn), lambda i, j: (0, j)), row, vec, row],
        out_specs=[row, vec],
        out_shape=[jax.ShapeDtypeStruct((S, D), F32), jax.ShapeDtypeStruct((1, D), F32)],
        scratch_shapes=[pltpu.VMEM((ts, D), F32)],
        compiler_params=_params(52, ("arbitrary", "arbitrary")),
    )(du, w, h, g, dres)


def _loss_head(h, g, tgt, ts=256):
    row = pl.BlockSpec((ts, D), lambda i: (i, 0))
    vec = pl.BlockSpec((1, D), lambda i: (0, 0))

    def body(h_ref, g_ref, t_ref, dh_ref, dg_ref, loss_ref):
        i = pl.program_id(0)

        def rows_fn(rows, carry):
            dg, part = carry
            x = h_ref[rows, :]
            gv = g_ref[...]
            err = _rms(x, gv) - t_ref[rows, :]
            part = part + 0.5 * jnp.sum(jnp.mean(err * err, axis=-1, keepdims=True), axis=0, keepdims=True)
            dx, dg_rows = _rms_bwd(err * (1.0 / D), x, gv)
            dh_ref[rows, :] = dx
            return dg + dg_rows, part
        dg, part = _row_loop(ts, rows_fn, (jnp.zeros((1, D), F32), jnp.zeros((1, 1), F32)))

        @pl.when(i == 0)
        def _():
            dg_ref[...] = dg
            loss_ref[...] = jnp.broadcast_to(part, loss_ref.shape)

        @pl.when(i > 0)
        def _():
            dg_ref[...] += dg
            loss_ref[...] += jnp.broadcast_to(part, loss_ref.shape)

    return pl.pallas_call(
        body, grid=(S // ts,), name="loss_head",
        in_specs=[row, vec, row], out_specs=[row, vec, pl.BlockSpec((1, 128), lambda i: (0, 0))],
        out_shape=[jax.ShapeDtypeStruct((S, D), F32), jax.ShapeDtypeStruct((1, D), F32),
                   jax.ShapeDtypeStruct((1, 128), F32)],
        compiler_params=_params(40, ("arbitrary",)),
    )(h, g, tgt)


def _pool_specs():
    blk = pl.BlockSpec((S, PG), lambda gi: (0, gi))
    wp = pl.BlockSpec((None, PG, PG), lambda gi: (gi, 0, 0))
    sc = pl.BlockSpec((1, PG), lambda gi: (0, gi))
    return blk, wp, sc


def _pool_fwd(u, wp_b, scale):
    blk, wp, sc = _pool_specs()

    def body(u_ref, wp_ref, sc_ref, y_ref, pooled_ref):
        win = 2 << pl.program_id(0)
        row = lax.broadcasted_iota(jnp.int32, (S, PG), 0)
        x = u_ref[...]
        s = x
        for k in (1, 2, 4, 8):
            s = s + jnp.where((row >= k) & (k < win), pltpu.roll(s, k, 0), 0.0)
        cnt = jnp.minimum(row + 1, win).astype(F32)
        pooled = (s / cnt - x).astype(BF16)
        pooled_ref[...] = pooled
        y_ref[...] = (_dot(pooled, wp_ref[...]) * sc_ref[...]).astype(BF16)

    return pl.pallas_call(
        body, grid=(len(POOL_WINDOWS),), name="pool_fwd", in_specs=[blk, wp, sc], out_specs=[blk, blk],
        out_shape=[jax.ShapeDtypeStruct((S, D_POOL), BF16), jax.ShapeDtypeStruct((S, D_POOL), BF16)],
        compiler_params=_params(40),
    )(u, wp_b, scale)


def _pool_bwd(dy, pooled, wp_b, scale):
    blk, wp, sc = _pool_specs()

    def body(dy_ref, p_ref, wp_ref, sc_ref, du_ref, dwp_ref, dsc_ref):
        win = 2 << pl.program_id(0)
        row = lax.broadcasted_iota(jnp.int32, (S, PG), 0)
        dyv = dy_ref[...]
        pooled = p_ref[...]
        w = wp_ref[...]
        dsc_ref[...] = jnp.sum(dyv * _dot(pooled, w), axis=0, keepdims=True)
        dz = (dyv * sc_ref[...]).astype(BF16)
        dwp_ref[...] = _dot_tn(pooled, dz)
        dpooled = _dot_nt(dz, w)
        cnt = jnp.minimum(row + 1, win).astype(F32)
        fs = dpooled / cnt
        for k in (1, 2, 4, 8):
            fs = fs + jnp.where((row < S - k) & (k < win), pltpu.roll(fs, S - k, 0), 0.0)
        du_ref[...] = (fs - dpooled).astype(BF16)

    return pl.pallas_call(
        body, grid=(len(POOL_WINDOWS),), name="pool_bwd", in_specs=[blk, blk, wp, sc], out_specs=[blk, wp, sc],
        out_shape=[jax.ShapeDtypeStruct((S, D_POOL), BF16), jax.ShapeDtypeStruct((len(POOL_WINDOWS), PG, PG), F32),
                   jax.ShapeDtypeStruct((1, D_POOL), F32)],
        compiler_params=_params(40),
    )(dy, pooled, wp_b, scale)


def _gla_in_specs(chunk_of):
    def at(width, col):
        return pl.BlockSpec((CH, width), lambda n: (chunk_of(n), col))
    return [at(H * DK, O_Q // (H * DK)), at(H * DK, O_K // (H * DK)), at(H * DV, O_V // (H * DV)),
            at(H * DV, O_G // (H * DV)), at(RANK_PAD, O_R // RANK_PAD)]


def _gla_decay_terms(lr_ref, wa_ref, ba_ref, q_ref, k_ref):
    row = lax.broadcasted_iota(jnp.int32, (CH, CH), 0)
    col = lax.broadcasted_iota(jnp.int32, (CH, CH), 1)
    tril = row >= col
    z = _dot(lr_ref[...].astype(BF16), wa_ref[...]) + ba_ref[...]
    la = _log_sigmoid(z) / GATE_NORM
    b = _tri_dot(jnp.where(tril, 1.0, 0.0).astype(BF16), la)
    bl = jnp.sum(la, axis=0, keepdims=True)
    e_b, e_nb, e_tb = jnp.exp(b), jnp.exp(-b), jnp.exp(bl - b)
    kk = k_ref[...]
    q_dec = (q_ref[...] * QK_SCALE) * e_b
    return tril, z, e_b, e_nb, e_tb, jnp.exp(bl), q_dec, kk * e_nb, kk * e_tb


def _gla_fwd(u, wa_b, ba, gn):
    wide = pl.BlockSpec((CH, H * DV), lambda n: (n, 0))

    def body(q_ref, k_ref, v_ref, g_ref, lr_ref, wa_ref, ba_ref, gn_ref, y_ref, o_ref, st_ref, state):
        @pl.when(pl.program_id(0) == 0)
        def _():
            state[...] = jnp.zeros_like(state)

        tril, _, _, _, _, dec, q_dec, k_inv, k_tail = _gla_decay_terms(lr_ref, wa_ref, ba_ref, q_ref, k_ref)
        for hd in range(H):
            ks, vs = slice(hd * DK, (hd + 1) * DK), slice(hd * DV, (hd + 1) * DV)
            qb, kib, ktb = q_dec[:, ks].astype(BF16), k_inv[:, ks].astype(BF16), k_tail[:, ks].astype(BF16)
            vb = v_ref[:, vs].astype(BF16)
            p = jnp.where(tril, _dot_nt(qb, kib), 0.0)
            st = state[hd]
            o = _dot(p.astype(BF16), vb) + _dot_nt(qb, st.astype(BF16))
            st_ref[hd] = st
            state[hd] = st * dec[:, ks] + _dot_tn(vb, ktb)
            o_ref[:, vs] = o
            on = _rms(o, gn_ref[...])
            gg = g_ref[:, vs]
            y_ref[:, vs] = (on * (gg * _sigmoid(gg))).astype(BF16)

    return pl.pallas_call(
        body, grid=(NCH,), name="gla_fwd",
        in_specs=_gla_in_specs(lambda n: n) + [pl.BlockSpec((RANK_PAD, H * DK), lambda n: (0, 0)),
                                               pl.BlockSpec((1, H * DK), lambda n: (0, 0)),
                                               pl.BlockSpec((1, DV), lambda n: (0, 0))],
        out_specs=[wide, wide, pl.BlockSpec((None, H, DV, DK), lambda n: (n, 0, 0, 0))],
        out_shape=[jax.ShapeDtypeStruct((S, H * DV), BF16), jax.ShapeDtypeStruct((S, H * DV), F32),
                   jax.ShapeDtypeStruct((NCH, H, DV, DK), F32)],
        scratch_shapes=[pltpu.VMEM((H, DV, DK), F32)],
        compiler_params=_params(32, ("arbitrary",)),
    )(u, u, u, u, u, wa_b, ba, gn)


GLA_DU = 2 * H * DK + 2 * H * DV + RANK_PAD


def _gla_bwd(u, o_arr, states, dy, wa_b, ba, gn):
    rev = lambda n: NCH - 1 - n
    wide = pl.BlockSpec((CH, H * DV), lambda n: (rev(n), 0))

    def body(q_ref, k_ref, v_ref, g_ref, lr_ref, o_ref, st_ref, dy_ref, wa_ref, ba_ref, gn_ref,
             du_ref, dwa_ref, dba_ref, dgn_ref, gstate, db_scr, dbl_scr):
        @pl.when(pl.program_id(0) == 0)
        def _():
            gstate[...] = jnp.zeros_like(gstate)
            dwa_ref[...] = jnp.zeros_like(dwa_ref)
            dba_ref[...] = jnp.zeros_like(dba_ref)
            dgn_ref[...] = jnp.zeros_like(dgn_ref)

        tril, z, e_b, e_nb, e_tb, dec, q_dec, k_inv, k_tail = _gla_decay_terms(lr_ref, wa_ref, ba_ref, q_ref, k_ref)
        gnv = gn_ref[...]
        dgn = jnp.zeros((1, DV), F32)
        for hd in range(H):
            ks, vs = slice(hd * DK, (hd + 1) * DK), slice(hd * DV, (hd + 1) * DV)
            qh, kih, kth = q_dec[:, ks], k_inv[:, ks], k_tail[:, ks]
            qb, kib, ktb = qh.astype(BF16), kih.astype(BF16), kth.astype(BF16)
            vb = v_ref[:, vs].astype(BF16)
            o = o_ref[:, vs]
            gg = g_ref[:, vs]
            dyh = dy_ref[:, vs]
            r = lax.rsqrt(jnp.mean(o * o, axis=-1, keepdims=True) + EPS)
            xh = o * r
            sg = _sigmoid(gg)
            dgate = dyh * (xh * gnv) * (sg * (1.0 + gg * (1.0 - sg)))
            don = dyh * (gg * sg)
            dgn = dgn + jnp.sum(don * xh, axis=0, keepdims=True)
            dxh = don * gnv
            d_o = (r * (dxh - xh * jnp.mean(dxh * xh, axis=-1, keepdims=True))).astype(BF16)
            pb = jnp.where(tril, _dot_nt(qb, kib), 0.0).astype(BF16)
            dpb = jnp.where(tril, _dot_nt(d_o, vb), 0.0).astype(BF16)
            gt = gstate[hd]
            gtb = gt.astype(BF16)
            st = st_ref[hd]
            dv = _dot_tn(pb, d_o) + _dot_nt(ktb, gtb)
            dq_dec = _dot(dpb, kib) + _dot(d_o, st.astype(BF16))
            dk_inv = _dot_tn(dpb, qb)
            dk_tail = _dot(vb, gtb)
            ddec = jnp.sum(gt * st, axis=0, keepdims=True)
            gstate[hd] = _dot_tn(d_o, qb) + dec[:, ks] * gt
            du_ref[:, ks] = (dq_dec * QK_SCALE * e_b[:, ks]).astype(BF16)
            du_ref[:, H * DK + hd * DK:H * DK + (hd + 1) * DK] = (dk_inv * e_nb[:, ks] + dk_tail * e_tb[:, ks]).astype(BF16)
            du_ref[:, 2 * H * DK + hd * DV:2 * H * DK + (hd + 1) * DV] = dv.astype(BF16)
            du_ref[:, 2 * H * DK + H * DV + hd * DV:2 * H * DK + H * DV + (hd + 1) * DV] = dgate.astype(BF16)
            db_scr[:, ks] = dq_dec * qh - dk_inv * kih - dk_tail * kth
            dbl_scr[:, ks] = jnp.sum(dk_tail * kth, axis=0, keepdims=True) + ddec * dec[:, ks]
        dgn_ref[...] += dgn
        row = lax.broadcasted_iota(jnp.int32, (CH, CH), 0)
        col = lax.broadcasted_iota(jnp.int32, (CH, CH), 1)
        dla = _tri_dot(jnp.where(row <= col, 1.0, 0.0).astype(BF16), db_scr[...]) + dbl_scr[...]
        dz = dla * (1.0 / GATE_NORM) * _sigmoid(-z)
        dzb = dz.astype(BF16)
        du_ref[:, GLA_DU - RANK_PAD:] = _dot_nt(dzb, wa_ref[...]).astype(BF16)
        dwa_ref[...] += _dot_tn(lr_ref[...].astype(BF16), dzb)
        dba_ref[...] += jnp.sum(dz, axis=0, keepdims=True)

    full = lambda shape: pl.BlockSpec(shape, lambda n: (0,) * len(shape))
    return pl.pallas_call(
        body, grid=(NCH,), name="gla_bwd",
        in_specs=_gla_in_specs(rev) + [wide, pl.BlockSpec((None, H, DV, DK), lambda n: (rev(n), 0, 0, 0)),
                                       pl.BlockSpec((CH, H * DV), lambda n: (rev(n), 1)),
                                       full((RANK_PAD, H * DK)), full((1, H * DK)), full((1, DV))],
        out_specs=[pl.BlockSpec((CH, GLA_DU), lambda n: (rev(n), 0)), full((RANK_PAD, H * DK)), full((1, H * DK)),
                   full((1, DV))],
        out_shape=[jax.ShapeDtypeStruct((S, GLA_DU), BF16), jax.ShapeDtypeStruct((RANK_PAD, H * DK), F32),
                   jax.ShapeDtypeStruct((1, H * DK), F32), jax.ShapeDtypeStruct((1, DV), F32)],
        scratch_shapes=[pltpu.VMEM((H, DV, DK), F32), pltpu.VMEM((CH, H * DK), F32), pltpu.VMEM((1, H * DK), F32)],
        compiler_params=_params(32, ("arbitrary",)),
    )(u, u, u, u, u, o_arr, states, dy, wa_b, ba, gn)


def _ffn_weight_grads(n, dgu, act, df, tag):
    d_in = _tn_matmul(n, dgu, pl.BlockSpec((S, 512), lambda s, m: (0, m)),
                      pl.BlockSpec((None, S, WIN_SHARD), lambda s, m: (s // (NDEV // 2), 0, s % (NDEV // 2))),
                      jax.ShapeDtypeStruct((D, 2 * FF), BF16), pl.BlockSpec((512, WIN_SHARD), lambda s, m: (m, s)),
                      (NDEV, D // 512), tag + "_dw_in", 32)
    d_out = _tn_matmul(act, df, pl.BlockSpec((S, 512), lambda m: (0, m)), pl.BlockSpec((S, D), lambda m: (0, 0)),
                       jax.ShapeDtypeStruct((FF, D), BF16), pl.BlockSpec((512, D), lambda m: (m, 0)),
                       (FF // 512,), tag + "_dw_out", 40)
    return d_in, d_out


def _fwd_ffn1(x, w, after=()):
    return _ffn_fwd(x, w["ffn1_norm"], w["ffn1_w_in"], w["ffn1_w_out"], "ffn1_fwd", after)


def _fwd_mixer(h1, w, after=()):
    u, n2 = _norm_matmul(h1, w["mix_norm"], w["w_in_mix"], "mix_in", after)
    y_pool, pooled = _pool_fwd(u, w["w_pool"], w["pool_scale"])
    y_gla, o_gla, states = _gla_fwd(u, w["w_alpha"], w["b_alpha"], w["gla_norm"])
    y = jnp.concatenate([y_pool, y_gla], axis=1)
    h2 = _matmul_residual(y, w["w_out_mix"], h1, "mix_out")
    return h2, dict(u=u, n2=n2, pooled=pooled, o_gla=o_gla, states=states, y=y)


def _fwd_ffn2_loss(h2, tgt, w, after=()):
    h3, n3, gu3 = _ffn_fwd(h2, w["ffn2_norm"], w["ffn2_w_in"], w["ffn2_w_out"], "ffn2_fwd", after)
    dh3, d_final, loss = _loss_head(h3, w["final_norm"], tgt)
    return dh3, d_final, loss, n3, gu3


def _bwd_ffn2(dh3, h2, n3, gu3, w):
    dgu3, act3, df3, dh2, dh2b, d_norm = _ffn_bwd_x(dh3, h2, w["ffn2_norm"], gu3, w["ffn2_w_in"], w["ffn2_w_out"], "ffn2_bwd")
    d_in, d_out = _ffn_weight_grads(n3, dgu3, act3, df3, "ffn2")
    return dh2, dh2b, dict(ffn2_norm=d_norm, ffn2_w_in=d_in, ffn2_w_out=d_out)


def _bwd_mixer(dh2, dh2b, h1, sv, w, after=()):
    dy = _nt_matmul(dh2b, w["w_out_mix"], "mix_out_bwd", after)
    d_wo = _tn_matmul(sv["y"], dh2b, pl.BlockSpec((S, 512), lambda m: (0, m)), pl.BlockSpec((S, D), lambda m: (0, 0)),
                      jax.ShapeDtypeStruct((D, D), BF16), pl.BlockSpec((512, D), lambda m: (m, 0)), (D // 512,),
                      "mix_out_dw", 40)
    du_pool, d_wpool, d_scale = _pool_bwd(dy, sv["pooled"], w["w_pool"], w["pool_scale"])
    du_gla, d_walpha, d_balpha, d_gnorm = _gla_bwd(sv["u"], sv["o_gla"], sv["states"], dy, w["w_alpha"], w["b_alpha"],
                                                   w["gla_norm"])
    du = jnp.concatenate([du_pool, du_gla], axis=1)
    dh1, d_mix = _nt_matmul_normbwd(du, w["w_in_mix"], h1, w["mix_norm"], dh2, "mix_in_bwd")
    d_wmix = _tn_matmul(sv["n2"], du, pl.BlockSpec((S, 512), lambda j, m: (0, m)), pl.BlockSpec((S, 1408), lambda j, m: (0, j)),
                        jax.ShapeDtypeStruct((D, D_IN_PAD), BF16), pl.BlockSpec((512, 1408), lambda j, m: (m, j)),
                        (D_IN_PAD // 1408, D // 512), "mix_in_dw", 32)
    return dh1, dict(mix_norm=d_mix, w_in_mix=d_wmix, w_pool=d_wpool, pool_scale=d_scale, w_alpha=d_walpha,
                     b_alpha=d_balpha, gla_norm=d_gnorm, w_out_mix=d_wo)


def _bwd_ffn1(dh1, x, n1, gu1, w, after=()):
    dgu1, act1, df1, dx, _, d_norm = _ffn_bwd_x(dh1, x, w["ffn1_norm"], gu1, w["ffn1_w_in"], w["ffn1_w_out"], "ffn1_bwd", after)
    d_in, d_out = _ffn_weight_grads(n1, dgu1, act1, df1, "ffn1")
    return dx, dict(ffn1_norm=d_norm, ffn1_w_in=d_in, ffn1_w_out=d_out)


def _local_step(x, tgt, w):
    h1, n1, gu1 = _fwd_ffn1(x, w)
    h2, sv = _fwd_mixer(h1, w)
    dh3, d_final, loss, n3, gu3 = _fwd_ffn2_loss(h2, tgt, w)
    dh2, dh2b, g2 = _bwd_ffn2(dh3, h2, n3, gu3, w)
    dh1, gm = _bwd_mixer(dh2, dh2b, h1, sv, w)
    dx, g1 = _bwd_ffn1(dh1, x, n1, gu1, w)
    return loss, dx, dict(final_norm=d_final, **g1, **gm, **g2)


def _mesh_index():
    return 4 * lax.axis_index("x") + 2 * lax.axis_index("y") + lax.axis_index("c")


def _coords(p):
    return (p // 4, (p // 2) % 2, p % 2)


NCHIP = 4


def _place():
    return lax.axis_index("x"), lax.axis_index("y"), lax.axis_index("c")


def _rel_chip(x, y, rel):
    return ((1 - x) if rel & 1 else x, (1 - y) if rel & 2 else y)


def _dev_index(x, y, c):
    return 4 * x + 2 * y + c


def _cols(ref, p, width):
    return ref.at[:, pl.ds(pl.multiple_of(p * width, 128), width)]


def _sems(na, n):
    return [pltpu.SemaphoreType.DMA((na, n)), pltpu.SemaphoreType.DMA((na, n)), pltpu.SemaphoreType.DMA((na,))]


def _gather(items, name):
    arrays = [a for a, _ in items]
    kinds = [k for _, k in items]
    na = len(arrays)
    out_shape = [jax.ShapeDtypeStruct((NDEV,) + a.shape if k == "bcast" else (a.shape[0], NDEV * a.shape[1]), a.dtype)
                 for a, k in items]

    def body(*refs):
        ins, outs = refs[:na], refs[na:2 * na]
        send_sems, recv_sems, local_sems = refs[2 * na:]
        x, y, c = _place()
        sibling = (x, y, 1 - c)

        def slab(a, s):
            return _cols(outs[a], s, ins[a].shape[1]) if kinds[a] == "bcast_cols" else outs[a].at[s]

        def copy(a, k, block, to, own=False):
            return pltpu.make_async_remote_copy(ins[a] if own else slab(a, block), slab(a, block), send_sems.at[a, k],
                                                recv_sems.at[a, k], device_id=to, device_id_type=MESH)

        me = _dev_index(x, y, c)
        local = [pltpu.make_async_copy(ins[a], slab(a, me), local_sems.at[a]) for a in range(na)]
        sent = []
        for cp in local:
            cp.start()
        for rel in range(1, NCHIP):
            for a in range(na):
                sent.append(copy(a, rel, me, (*_rel_chip(x, y, rel), c), own=True))
                sent[-1].start()
        for a in range(na):
            sent.append(copy(a, 0, me, sibling, own=True))
            sent[-1].start()
        for rel in range(1, NCHIP):
            src_dev = _dev_index(*_rel_chip(x, y, rel), c)
            for a in range(na):
                copy(a, rel, src_dev, sibling).wait_recv()
                sent.append(copy(a, NCHIP - 1 + rel, src_dev, sibling))
                sent[-1].start()
        for a in range(na):
            copy(a, 0, _dev_index(x, y, 1 - c), sibling).wait_recv()
        for rel in range(1, NCHIP):
            for a in range(na):
                copy(a, NCHIP - 1 + rel, _dev_index(*_rel_chip(x, y, rel), 1 - c), sibling).wait_recv()
        for cp in sent:
            cp.wait_send()
        for cp in local:
            cp.wait()

    return pl.pallas_call(body, name=name, in_specs=[ANY] * na, out_specs=[ANY] * na, out_shape=out_shape,
                          scratch_shapes=_sems(na, 2 * NCHIP - 1))(*arrays)


def _pair_exchange(items, name):
    arrays = [a for a, _ in items]
    kinds = [k for _, k in items]
    na = len(arrays)
    out_shape = [jax.ShapeDtypeStruct((NCHIP,) + (a.shape[1:] if k == "scatter" else (a.shape[0], a.shape[1] // NDEV)), a.dtype)
                 for a, k in items]

    def body(*refs):
        ins, outs = refs[:na], refs[na:2 * na]
        send_sems, recv_sems = refs[2 * na:]
        x, y, c = _place()
        copies = []
        for rel in range(NCHIP):
            p = _dev_index(*_rel_chip(x, y, rel), 1 - c)
            for a in range(na):
                src = ins[a].at[p] if kinds[a] == "scatter" else _cols(ins[a], p, ins[a].shape[1] // NDEV)
                copies.append(pltpu.make_async_remote_copy(src, outs[a].at[rel], send_sems.at[a, rel], recv_sems.at[a, rel],
                                                           device_id=(x, y, 1 - c), device_id_type=MESH))
                copies[-1].start()
        for cp in copies:
            cp.wait_send()
            cp.wait_recv()

    return pl.pallas_call(body, name=name, in_specs=[ANY] * na, out_specs=[ANY] * na, out_shape=out_shape,
                          scratch_shapes=_sems(na, NCHIP)[:2])(*arrays)


def _pair_add(own, kind, got, table, tr, name):
    _, rows, cols = got.shape
    if kind == "scatter":
        own_spec = pl.BlockSpec((None, tr, cols), lambda rel, i, t: (t[rel], i, 0))
    else:
        own_spec = pl.BlockSpec((tr, cols), lambda rel, i, t: (i, t[rel]))
    blk = pl.BlockSpec((None, tr, cols), lambda rel, i, t: (rel, i, 0))

    def body(t_ref, a_ref, b_ref, o_ref):
        o_ref[...] = (a_ref[...].astype(F32) + b_ref[...].astype(F32)).astype(o_ref.dtype)

    return pl.pallas_call(
        body, name=name, out_shape=jax.ShapeDtypeStruct(got.shape, got.dtype),
        grid_spec=pltpu.PrefetchScalarGridSpec(num_scalar_prefetch=1, grid=(NCHIP, rows // tr), in_specs=[own_spec, blk],
                                               out_specs=blk),
        compiler_params=_params(32),
    )(table, own, got)


def _exchange(items, name):
    arrays = [a for a, _ in items]
    kinds = [k for _, k in items]
    na = len(arrays)
    out_shape = [jax.ShapeDtypeStruct((NDEV,) + a.shape if k == "bcast" else a.shape, a.dtype) for a, k in items]

    def body(*refs):
        ins, outs = refs[:na], refs[na:2 * na]
        send_sems, recv_sems, local_sems = refs[2 * na:]
        x, y, c = _place()
        me = _dev_index(x, y, c)

        def mine(a):
            return 0 if kinds[a] == "chip" else me

        def src(a, p):
            return ins[a] if kinds[a] == "bcast" else ins[a].at[p]

        local = [pltpu.make_async_copy(src(a, mine(a)), outs[a].at[mine(a)], local_sems.at[a]) for a in range(na)]
        for cp in local:
            cp.start()
        waits = []
        for k in range(1, NDEV):
            to = (me + k) % NDEV
            frm = (me + NDEV - k) % NDEV
            for a in range(na):
                if kinds[a] == "chip":
                    if k >= NCHIP:
                        continue
                    send = pltpu.make_async_remote_copy(ins[a].at[k], outs[a].at[k], send_sems.at[a, k], recv_sems.at[a, k],
                                                        device_id=(*_rel_chip(x, y, k), c), device_id_type=MESH)
                    waits.append(send)
                else:
                    send = pltpu.make_async_remote_copy(src(a, to), outs[a].at[me], send_sems.at[a, k], recv_sems.at[a, k],
                                                        device_id=_coords(to), device_id_type=MESH)
                    waits.append(pltpu.make_async_remote_copy(src(a, to), outs[a].at[frm], send_sems.at[a, k],
                                                              recv_sems.at[a, k], device_id=_coords(to), device_id_type=MESH))
                send.start()
        for cp in waits:
            cp.wait_send()
            cp.wait_recv()
        for cp in local:
            cp.wait()

    return pl.pallas_call(body, name=name, in_specs=[ANY] * na, out_specs=[ANY] * na, out_shape=out_shape,
                          scratch_shapes=_sems(na, NDEV))(*arrays)


HBM = pl.BlockSpec(memory_space=pltpu.HBM)
SEM = pl.BlockSpec(memory_space=pltpu.SEMAPHORE)
DATAFLOW = pltpu.SideEffectType.DATAFLOW_SIDE_EFFECTING


def _pair_copies(kinds):
    def describe(srcs, lands, send_sems, recv_sems):
        x, y, c = _place()
        na = len(srcs)
        for rel in range(NCHIP):
            p = _dev_index(*_rel_chip(x, y, rel), 1 - c)
            for a in range(na):
                src = srcs[a].at[p] if kinds[a] == "scatter" else _cols(srcs[a], p, srcs[a].shape[1] // NDEV)
                cp = pltpu.make_async_remote_copy(src, lands[a].at[rel], send_sems.at[rel * na + a], recv_sems.at[rel * na + a],
                                                  device_id=(x, y, 1 - c), device_id_type=MESH)
                yield cp, cp
    return describe


def _chip_copies(srcs, lands, send_sems, recv_sems):
    x, y, c = _place()
    na = len(srcs)
    for rel in range(1, NCHIP):
        for a in range(na):
            i = (rel - 1) * na + a
            cp = pltpu.make_async_remote_copy(srcs[a].at[rel], lands[a].at[rel], send_sems.at[i], recv_sems.at[i],
                                              device_id=(*_rel_chip(x, y, rel), c), device_id_type=MESH)
            yield cp, cp


def _start_copies(name, srcs, lands, describe, ncopies):
    arrays = list(srcs) + list(lands)
    ns, n = len(srcs), len(arrays)

    def body(*refs):
        for send, _ in describe(refs[:ns], refs[ns:n], refs[n], refs[n + 1]):
            send.start()
        refs[-1][...] = jnp.zeros_like(refs[-1])

    out = pl.pallas_call(
        body, name=name,
        out_shape=(pltpu.SemaphoreType.DMA((ncopies,)), pltpu.SemaphoreType.DMA((ncopies,)),
                   *[pltpu.HBM(a.shape, a.dtype) for a in arrays], jax.ShapeDtypeStruct((8, 128), F32)),
        in_specs=[HBM] * n, out_specs=(SEM, SEM, *[HBM] * n, pl.BlockSpec(memory_space=pltpu.VMEM)),
        input_output_aliases={i: 2 + i for i in range(n)},
        compiler_params=pltpu.CompilerParams(has_side_effects=DATAFLOW),
    )(*[pltpu.with_memory_space_constraint(a, pltpu.HBM) for a in arrays])
    return out[0], out[1], list(out[2:2 + n]), out[-1]


def _wait_copies(name, send_sems, recv_sems, thru, ns, describe, after):
    n = len(thru)

    def body(*refs):
        for send, arrival in describe(refs[:ns], refs[ns:n], refs[n], refs[n + 1]):
            send.wait_send()
            arrival.wait_recv()

    out = pl.pallas_call(
        body, name=name, out_shape=tuple(pltpu.HBM(a.shape, a.dtype) for a in thru),
        in_specs=[HBM] * n + [SEM, SEM] + [ANY] * len(after), out_specs=tuple([HBM] * n),
        input_output_aliases={i: i for i in range(n)},
        compiler_params=pltpu.CompilerParams(has_side_effects=DATAFLOW),
    )(*thru, send_sems, recv_sems, *after)
    return list(out[:ns]), list(out[ns:])


def _adamw(parts, w, m, v, tr, name):
    rows, cols = w.shape
    nparts = len(parts)
    blk = pl.BlockSpec((tr, cols), lambda i: (i, 0))

    def slab_spec(s):
        return pl.BlockSpec((None, tr, cols), lambda i: (s, i, 0))

    def body(*refs):
        p_refs = refs[:nparts]
        w_ref, m_ref, v_ref, g_ref, d_ref, nm_ref, nv_ref = refs[nparts:]
        g = p_refs[0][...].astype(F32)
        for p_ref in p_refs[1:]:
            g = g + p_ref[...].astype(F32)
        nm = ADAM_B1 * m_ref[...] + (1.0 - ADAM_B1) * g
        nv = ADAM_B2 * v_ref[...] + (1.0 - ADAM_B2) * (g * g)
        m_hat = nm / (1.0 - ADAM_B1 ** ADAM_STEP)
        v_hat = nv / (1.0 - ADAM_B2 ** ADAM_STEP)
        g_ref[...] = g
        d_ref[...] = -ADAM_LR * (m_hat / (jnp.sqrt(v_hat) + ADAM_EPS) + ADAM_WD * w_ref[...])
        nm_ref[...] = nm
        nv_ref[...] = nv

    return pl.pallas_call(
        body, grid=(rows // tr,), name=name,
        in_specs=[slab_spec(s) for _, s in parts] + [blk, blk, blk], out_specs=[blk] * 4,
        out_shape=[jax.ShapeDtypeStruct((rows, cols), F32)] * 4,
        compiler_params=_params(40),
    )(*[a for a, _ in parts], w, m, v)


def _pack_small(vals, extra=None):
    flat = [vals[n].reshape(-1).astype(F32) for n, _ in SMALL]
    tail = jnp.zeros((SMALL_ROWS * 128 - LOSS_AT,), F32)
    if extra is not None:
        tail = tail.at[0].set(extra)
    return jnp.concatenate(flat + [tail]).reshape(SMALL_ROWS, 128)


def _unpack_small(packed, like):
    flat, out, at = packed.reshape(-1), {}, 0
    for n, size in SMALL:
        out[n] = flat[at:at + size].reshape(like[n].shape)
        at += size
    return out, flat[LOSS_AT]


def kernel(x, ffn1_norm, ffn1_w_in, ffn1_w_out, mix_norm, w_in_mix, w_pool, pool_scale, w_alpha, b_alpha, gla_norm, w_out_mix, ffn2_norm, ffn2_w_in, ffn2_w_out, final_norm, loss_target, m_ffn1_norm, m_ffn1_w_in, m_ffn1_w_out, m_mix_norm, m_w_in_mix, m_w_pool, m_pool_scale, m_w_alpha, m_b_alpha, m_gla_norm, m_w_out_mix, m_ffn2_norm, m_ffn2_w_in, m_ffn2_w_out, m_final_norm, v_ffn1_norm, v_ffn1_w_in, v_ffn1_w_out, v_mix_norm, v_w_in_mix, v_w_pool, v_pool_scale, v_w_alpha, v_b_alpha, v_gla_norm, v_w_out_mix, v_ffn2_norm, v_ffn2_w_in, v_ffn2_w_out, v_final_norm):
    names = ["ffn1_norm", "ffn1_w_in", "ffn1_w_out", "mix_norm", "w_in_mix", "w_pool", "pool_scale", "w_alpha", "b_alpha",
             "gla_norm", "w_out_mix", "ffn2_norm", "ffn2_w_in", "ffn2_w_out", "final_norm"]
    p = dict(zip(names, [ffn1_norm, ffn1_w_in, ffn1_w_out, mix_norm, w_in_mix, w_pool, pool_scale, w_alpha, b_alpha,
                         gla_norm, w_out_mix, ffn2_norm, ffn2_w_in, ffn2_w_out, final_norm]))
    m = dict(zip(names, [m_ffn1_norm, m_ffn1_w_in, m_ffn1_w_out, m_mix_norm, m_w_in_mix, m_w_pool, m_pool_scale, m_w_alpha,
                         m_b_alpha, m_gla_norm, m_w_out_mix, m_ffn2_norm, m_ffn2_w_in, m_ffn2_w_out, m_final_norm]))
    v = dict(zip(names, [v_ffn1_norm, v_ffn1_w_in, v_ffn1_w_out, v_mix_norm, v_w_in_mix, v_w_pool, v_pool_scale, v_w_alpha,
                         v_b_alpha, v_gla_norm, v_w_out_mix, v_ffn2_norm, v_ffn2_w_in, v_ffn2_w_out, v_final_norm]))

    big = ["ffn1_w_in", "ffn1_w_out", "w_in_mix", "w_out_mix", "ffn2_w_in", "ffn2_w_out"]
    shards = [(p[n][0].astype(BF16), "bcast_cols" if n in ("ffn1_w_in", "ffn2_w_in") else "bcast") for n in big]
    shards += [(p["w_pool"][0].reshape(H * 32, PG), "bcast"), (p["w_alpha"][0], "bcast")]
    g_w1in, g_w1out, g_wmix, g_wo, g_w2in, g_w2out, g_wpool, g_walpha = _gather(shards, "gather_weights")

    wmix = jnp.transpose(g_wmix, (1, 0, 2)).reshape(D, D_IN)
    walpha = jnp.transpose(g_walpha, (1, 0, 2)).reshape(RANK, H * DK)
    full = {
        "ffn1_w_in": g_w1in, "ffn1_w_out": g_w1out.reshape(FF, D),
        "ffn2_w_in": g_w2in, "ffn2_w_out": g_w2out.reshape(FF, D),
        "w_in_mix": jnp.pad(wmix, ((0, 0), (0, D_IN_PAD - D_IN))),
        "w_out_mix": g_wo.reshape(D, D),
        "w_pool": jnp.transpose(g_wpool.reshape(NDEV, H, 32, PG), (1, 0, 2, 3)).reshape(H, PG, PG).astype(BF16),
        "w_alpha": jnp.pad(walpha, ((0, RANK_PAD - RANK), (0, 0))).astype(BF16),
        "final_norm": final_norm.reshape(1, D),
    }
    for n in ("ffn1_norm", "mix_norm", "ffn2_norm", "pool_scale", "b_alpha", "gla_norm"):
        full[n] = p[n]

    xs, tgt = x[0], loss_target[0]
    h1, n1, gu1 = _fwd_ffn1(xs, full)
    h2, sv = _fwd_mixer(h1, full)
    dh3, d_final, loss_part, n3, gu3 = _fwd_ffn2_loss(h2, tgt, full)

    mx, my, mc = _place()
    table = jnp.stack([_dev_index(*_rel_chip(mx, my, rel), mc) for rel in range(NCHIP)]).astype(jnp.int32)

    def slab_shape(a, kind):
        return (NCHIP,) + (a.shape[1:] if kind == "scatter" else (a.shape[0], a.shape[1] // NDEV))

    def pair_add_all(own, got, tag):
        return [_pair_add(a, kind, got_a, table, tr, "%s_pair_add_%d" % (tag, i))
                for i, ((a, kind, tr), got_a) in enumerate(zip(own, got))]

    dh2, dh2b, g2 = _bwd_ffn2(dh3, h2, n3, gu3, full)
    own2 = [(g2["ffn2_w_in"], "scatter_cols", 256), (g2["ffn2_w_out"].reshape(NDEV, WOUT_SHARD, D), "scatter", WOUT_SHARD // 2)]
    pair2 = _pair_copies([kind for _, kind, _ in own2])
    s2, r2, thru2, tok2 = _start_copies("ffn2_pair_start", [a for a, _, _ in own2],
                                        [lax.empty(slab_shape(a, kind), a.dtype) for a, kind, _ in own2], pair2, NCHIP * len(own2))
    dh1, gm = _bwd_mixer(dh2, dh2b, h1, sv, full, after=(tok2,))
    sent2, got2 = _wait_copies("ffn2_pair_wait", s2, r2, thru2, len(own2), pair2, after=(dh1,))
    pre2 = pair_add_all([(a, kind, tr) for a, (_, kind, tr) in zip(sent2, own2)], got2, "ffn2")
    s2c, r2c, thru2c, tok2c = _start_copies("ffn2_chip_start", pre2, [lax.empty(a.shape, a.dtype) for a in pre2], _chip_copies,
                                            (NCHIP - 1) * len(pre2))
    dx, g1 = _bwd_ffn1(dh1, xs, n1, gu1, full, after=(tok2c,))
    pre2, land2 = _wait_copies("ffn2_chip_wait", s2c, r2c, thru2c, len(pre2), _chip_copies, after=(g1["ffn1_w_out"],))

    g = dict(final_norm=d_final, **g1, **gm, **g2)
    d_wmix8 = jnp.transpose(g["w_in_mix"][:, :D_IN].reshape(D, NDEV, MIX_SHARD), (1, 0, 2))
    d_wpool8 = jnp.transpose(g["w_pool"].reshape(H, NDEV, 32, PG), (1, 0, 2, 3)).reshape(NDEV, H * 32, PG)
    d_walpha8 = jnp.transpose(g["w_alpha"][:RANK].reshape(RANK, NDEV, H * DK // NDEV), (1, 0, 2))
    own = [(g["ffn1_w_in"], "scatter_cols", 256), (g["ffn1_w_out"].reshape(NDEV, WOUT_SHARD, D), "scatter", WOUT_SHARD // 2),
           (d_wmix8, "scatter", 256), (g["w_out_mix"].reshape(NDEV, D // NDEV, D), "scatter", D // NDEV),
           (d_wpool8, "scatter", H * 32), (d_walpha8, "scatter", RANK)]
    got = _pair_exchange([(a, kind) for a, kind, _ in own], "pair_grads")
    pre = pair_add_all(own, got, "rest")
    parts = [(a, "chip") for a in pre] + [(_pack_small(g, loss_part[0, 0]), "bcast")]
    r_w1in, r_w1out, r_wmix, r_wo, r_wpool, r_walpha, r_small = _exchange(parts, "reduce_grads")

    def upd(parts, n, shape2d, tr):
        res = _adamw(parts, p[n].reshape(shape2d), m[n].reshape(shape2d), v[n].reshape(shape2d), tr, "adamw_" + n)
        return [r.reshape(p[n].shape) for r in res]

    def chips(r):
        return [(r, rel) for rel in range(NCHIP)]

    out = {
        "ffn1_w_in": upd(chips(r_w1in), "ffn1_w_in", (D, WIN_SHARD), 128),
        "ffn1_w_out": upd(chips(r_w1out), "ffn1_w_out", (WOUT_SHARD, D), 64),
        "w_in_mix": upd(chips(r_wmix), "w_in_mix", (D, MIX_SHARD), 128),
        "w_out_mix": upd(chips(r_wo), "w_out_mix", (D // NDEV, D), 64),
        "ffn2_w_in": upd([(pre2[0], 0)] + chips(land2[0])[1:], "ffn2_w_in", (D, WIN_SHARD), 128),
        "ffn2_w_out": upd([(pre2[1], 0)] + chips(land2[1])[1:], "ffn2_w_out", (WOUT_SHARD, D), 64),
        "w_pool": upd(chips(r_wpool), "w_pool", (H * 32, PG), H * 32),
        "w_alpha": upd(chips(r_walpha), "w_alpha", (RANK, H * DK // NDEV), RANK),
    }
    small_res = _adamw([(r_small, s) for s in range(NDEV)], _pack_small(p), _pack_small(m), _pack_small(v), SMALL_ROWS,
                       "adamw_small")
    unpacked = [_unpack_small(r, p) for r in small_res]
    loss = unpacked[0][1]
    for n, _ in SMALL:
        out[n] = [u[0][n] for u in unpacked]

    return (loss, dx.reshape(1, S, D), *[out[n][0] for n in names], *[out[n][1] for n in names],
            *[out[n][2] for n in names], *[out[n][3] for n in names])
```

```python
import functools

import jax
import jax.numpy as jnp
from jax import lax
from jax.experimental import pallas as pl
from jax.experimental.pallas import tpu as pltpu

F32, BF16 = jnp.float32, jnp.bfloat16
MESH = pl.DeviceIdType.MESH
ANY = pl.BlockSpec(memory_space=pl.ANY)

NDEV = 8
S = 2048
D = 2048
FF = 5632
WIN_SHARD = 2 * FF // NDEV
WOUT_SHARD = FF // NDEV
D_POOL = 1024
PG = 256
POOL_WINDOWS = (2, 4, 8, 16)
H = 4
DK = 128
DV = 256
CH = 64
NCH = S // CH
RANK = 16
RANK_PAD = 128
D_IN = 4112
D_IN_PAD = 4224
MIX_SHARD = D_IN // NDEV
O_Q, O_K, O_V, O_G, O_R = 1024, 1536, 2048, 3072, 4096
GATE_NORM = 16.0
QK_SCALE = DK ** -0.5
EPS = 1e-6
ADAM_LR, ADAM_B1, ADAM_B2, ADAM_EPS, ADAM_WD, ADAM_STEP = 0.001, 0.9, 0.999, 1e-08, 0.01, 10
V7X_VMEM_BYTES = 64 << 20

SMALL = (("ffn1_norm", 2048), ("mix_norm", 2048), ("ffn2_norm", 2048), ("final_norm", 2048),
         ("pool_scale", 1024), ("b_alpha", 512), ("gla_norm", 256))
SMALL_ROWS = 80
LOSS_AT = sum(n for _, n in SMALL)


def _params(vmem_mb, sem=None):
    return pltpu.CompilerParams(dimension_semantics=sem, vmem_limit_bytes=min(vmem_mb << 20, V7X_VMEM_BYTES - (4 << 20)))


def _dot(a, b):
    return jnp.dot(a, b, preferred_element_type=F32)


def _dot_nt(a, b):
    return lax.dot_general(a, b, (((1,), (1,)), ((), ())), preferred_element_type=F32)


def _dot_tn(a, b):
    return lax.dot_general(a, b, (((0,), (0,)), ((), ())), preferred_element_type=F32)


def _sigmoid(x):
    return 1.0 / (1.0 + jnp.exp(-x))


def _log_sigmoid(x):
    return jnp.minimum(x, 0.0) - jnp.log(1.0 + jnp.exp(-jnp.abs(x)))


def _rms(x, g):
    r = lax.rsqrt(jnp.mean(x * x, axis=-1, keepdims=True) + EPS)
    return x * r * g


def _rms_bwd(dn, x, g):
    r = lax.rsqrt(jnp.mean(x * x, axis=-1, keepdims=True) + EPS)
    xh = x * r
    dxh = dn * g
    dx = r * (dxh - xh * jnp.mean(dxh * xh, axis=-1, keepdims=True))
    return dx, jnp.sum(dn * xh, axis=0, keepdims=True)


ROWS = 64


def _row_loop(total, fn, init=0):
    def step(t, carry):
        return fn(pl.ds(pl.multiple_of(t * ROWS, ROWS), ROWS), carry)
    return lax.fori_loop(0, total // ROWS, step, init)


def _split3(x):
    hi = x.astype(BF16)
    r1 = x - hi.astype(F32)
    mid = r1.astype(BF16)
    lo = (r1 - mid.astype(F32)).astype(BF16)
    return hi, mid, lo


def _tri_dot(tri_b, x):
    hi, mid, lo = _split3(x)
    return (_dot(tri_b, lo) + _dot(tri_b, mid)) + _dot(tri_b, hi)


FFN_TS, FFN_TF = 512, 512


def _ffn_specs():
    wg = pl.BlockSpec((D, FFN_TF), lambda i, j: (0, j))
    wu = pl.BlockSpec((D, FFN_TF), lambda i, j: (0, FF // FFN_TF + j))
    wo = pl.BlockSpec((FFN_TF, D), lambda i, j: (j, 0))
    row = pl.BlockSpec((FFN_TS, D), lambda i, j: (i, 0))
    vec = pl.BlockSpec((1, D), lambda i, j: (0, 0))
    gu = pl.BlockSpec((2, FFN_TS, FFN_TF), lambda i, j: (0, i, j))
    return wg, wu, wo, row, vec, gu


def _ordered_after(body, n_in, after):
    def wrapped(*refs):
        return body(*refs[:n_in], *refs[n_in + len(after):])
    return wrapped, [ANY] * len(after)


def _ffn_fwd(h, g, w_in8, w_out, name, after=()):
    nj = FF // FFN_TF
    wg, wu, wo, row, vec, gu = _ffn_specs()

    def body(h_ref, g_ref, wg_ref, wu_ref, wo_ref, ho_ref, n_ref, gu_ref, acc_ref):
        j = pl.program_id(1)

        @pl.when(j == 0)
        def _():
            def norm(rows, c):
                n_ref[rows, :] = _rms(h_ref[rows, :], g_ref[...]).astype(BF16)
                return c
            _row_loop(FFN_TS, norm)
            acc_ref[...] = jnp.zeros_like(acc_ref)

        n = n_ref[...]
        gate = _dot(n, wg_ref[...])
        up = _dot(n, wu_ref[...])
        gu_ref[0] = gate.astype(BF16)
        gu_ref[1] = up.astype(BF16)
        a = (gate * _sigmoid(gate)) * up
        acc_ref[...] += _dot(a.astype(BF16), wo_ref[...])

        @pl.when(j == nj - 1)
        def _():
            def residual(rows, c):
                ho_ref[rows, :] = h_ref[rows, :] + 0.5 * acc_ref[rows, :]
                return c
            _row_loop(FFN_TS, residual)

    body, extra = _ordered_after(body, 5, after)
    return pl.pallas_call(
        body, grid=(S // FFN_TS, nj), name=name,
        in_specs=[row, vec, wg, wu, wo] + extra, out_specs=[row, row, gu],
        out_shape=[jax.ShapeDtypeStruct((S, D), F32), jax.ShapeDtypeStruct((S, D), BF16),
                   jax.ShapeDtypeStruct((2, S, FF), BF16)],
        scratch_shapes=[pltpu.VMEM((FFN_TS, D), F32)],
        compiler_params=_params(56, ("arbitrary", "arbitrary")),
    )(h, g, w_in8, w_in8, w_out, *after)


def _ffn_bwd_x(dhp, h, g, gu_arr, w_in8, w_out, name, after=()):
    ni, nj = S // FFN_TS, FF // FFN_TF
    wg, wu, wo, row, vec, gu = _ffn_specs()
    act = pl.BlockSpec((FFN_TS, FFN_TF), lambda i, j: (i, j))

    def body(dhp_ref, h_ref, g_ref, gu_ref, wg_ref, wu_ref, wo_ref,
             dgu_ref, a_ref, df_ref, dh_ref, dhb_ref, dg_ref, acc_ref):
        i, j = pl.program_id(0), pl.program_id(1)

        @pl.when(j == 0)
        def _():
            def half(rows, c):
                df_ref[rows, :] = (0.5 * dhp_ref[rows, :]).astype(BF16)
                return c
            _row_loop(FFN_TS, half)
            acc_ref[...] = jnp.zeros_like(acc_ref)

        gate = gu_ref[0].astype(F32)
        up = gu_ref[1].astype(F32)
        da = _dot_nt(df_ref[...], wo_ref[...])
        sg = _sigmoid(gate)
        silu = gate * sg
        dgate = (da * up * (sg * (1.0 + gate * (1.0 - sg)))).astype(BF16)
        dup = (da * silu).astype(BF16)
        a_ref[...] = (silu * up).astype(BF16)
        dgu_ref[0] = dgate
        dgu_ref[1] = dup
        acc_ref[...] += _dot_nt(dgate, wg_ref[...]) + _dot_nt(dup, wu_ref[...])

        @pl.when(j == nj - 1)
        def _():
            def norm_bwd(rows, dg):
                dx, dg_rows = _rms_bwd(acc_ref[rows, :], h_ref[rows, :], g_ref[...])
                dh = dhp_ref[rows, :] + dx
                dh_ref[rows, :] = dh
                dhb_ref[rows, :] = dh.astype(BF16)
                return dg + dg_rows
            dg = _row_loop(FFN_TS, norm_bwd, jnp.zeros((1, D), F32))

            @pl.when(i == 0)
            def _():
                dg_ref[...] = dg

            @pl.when(i > 0)
            def _():
                dg_ref[...] += dg

    body, extra = _ordered_after(body, 7, after)
    return pl.pallas_call(
        body, grid=(ni, nj), name=name,
        in_specs=[row, row, vec, gu, wg, wu, wo] + extra,
        out_specs=[gu, act, row, row, row, vec],
        out_shape=[jax.ShapeDtypeStruct((2, S, FF), BF16), jax.ShapeDtypeStruct((S, FF), BF16),
                   jax.ShapeDtypeStruct((S, D), BF16), jax.ShapeDtypeStruct((S, D), F32),
                   jax.ShapeDtypeStruct((S, D), BF16), jax.ShapeDtypeStruct((1, D), F32)],
        scratch_shapes=[pltpu.VMEM((FFN_TS, D), F32)],
        compiler_params=_params(58, ("arbitrary", "arbitrary")),
    )(dhp, h, g, gu_arr, w_in8, w_in8, w_out, *after)


def _tn_matmul(a, b, a_spec, b_spec, out_shape, out_spec, grid, name, vmem_mb, after=()):
    def body(a_ref, b_ref, o_ref):
        o_ref[...] = _dot_tn(a_ref[...], b_ref[...]).astype(o_ref.dtype)

    body, extra = _ordered_after(body, 2, after)
    return pl.pallas_call(body, grid=grid, name=name, in_specs=[a_spec, b_spec] + extra, out_specs=out_spec,
                          out_shape=out_shape, compiler_params=_params(vmem_mb))(a, b, *after)


def _norm_matmul(h, g, w, name, after=(), ts=512, tn=1408):
    n_out = w.shape[0]

    def body(h_ref, g_ref, w_ref, u_ref, n_ref):
        @pl.when(pl.program_id(1) == 0)
        def _():
            def norm(rows, c):
                n_ref[rows, :] = _rms(h_ref[rows, :], g_ref[...]).astype(BF16)
                return c
            _row_loop(ts, norm)

        u_ref[...] = _dot_nt(n_ref[...], w_ref[...])

    body, extra = _ordered_after(body, 3, after)
    return pl.pallas_call(
        body, grid=(S // ts, n_out // tn), name=name,
        in_specs=[pl.BlockSpec((ts, D), lambda i, j: (i, 0)), pl.BlockSpec((1, D), lambda i, j: (0, 0)),
                  pl.BlockSpec((tn, D), lambda i, j: (j, 0))] + extra,
        out_specs=[pl.BlockSpec((ts, tn), lambda i, j: (i, j)), pl.BlockSpec((ts, D), lambda i, j: (i, 0))],
        out_shape=[jax.ShapeDtypeStruct((S, n_out), F32), jax.ShapeDtypeStruct((S, D), BF16)],
        compiler_params=_params(48, ("arbitrary", "arbitrary")),
    )(h, g, w, *after)


def _matmul_residual(a, w, res, name, ts=512, tn=1024):
    k, n_out = w.shape

    def body(a_ref, w_ref, r_ref, o_ref):
        o_ref[...] = r_ref[...] + _dot(a_ref[...], w_ref[...])

    return pl.pallas_call(
        body, grid=(S // ts, n_out // tn), name=name,
        in_specs=[pl.BlockSpec((ts, k), lambda i, j: (i, 0)), pl.BlockSpec((k, tn), lambda i, j: (0, j)),
                  pl.BlockSpec((ts, tn), lambda i, j: (i, j))],
        out_specs=pl.BlockSpec((ts, tn), lambda i, j: (i, j)),
        out_shape=jax.ShapeDtypeStruct((S, n_out), F32),
        compiler_params=_params(40),
    )(a, w, res)


def _nt_matmul(a, w, name, after=(), ts=512, tn=1024):
    n_out, k = w.shape

    def body(a_ref, w_ref, o_ref):
        o_ref[...] = _dot_nt(a_ref[...], w_ref[...])

    body, extra = _ordered_after(body, 2, after)
    return pl.pallas_call(
        body, grid=(S // ts, n_out // tn), name=name,
        in_specs=[pl.BlockSpec((ts, k), lambda i, j: (i, 0)), pl.BlockSpec((tn, k), lambda i, j: (j, 0))] + extra,
        out_specs=pl.BlockSpec((ts, tn), lambda i, j: (i, j)),
        out_shape=jax.ShapeDtypeStruct((S, n_out), F32),
        compiler_params=_params(40),
    )(a, w, *after)


def _matmul_normbwd(du, w, h, g, dres, name, ts=512, tn=1408):
    n_in = w.shape[0]
    nj = n_in // tn
    row = pl.BlockSpec((ts, D), lambda i, j: (i, 0))
    vec = pl.BlockSpec((1, D), lambda i, j: (0, 0))

    def body(du_ref, w_ref, h_ref, g_ref, dres_ref, dh_ref, dg_ref, acc_ref):
        i, j = pl.program_id(0), pl.program_id(1)

        @pl.when(j == 0)
        def _():
            acc_ref[...] = jnp.zeros_like(acc_ref)

        acc_ref[...] += _dot(du_ref[...], w_ref[...])

        @pl.when(j == nj - 1)
        def _():
            def norm_bwd(rows, dg):
                dx, dg_rows = _rms_bwd(acc_ref[rows, :], h_ref[rows, :], g_ref[...])
                dh_ref[rows, :] = dres_ref[rows, :] + dx
                return dg + dg_rows
            dg = _row_loop(ts, norm_bwd, jnp.zeros((1, D), F32))

            @pl.when(i == 0)
            def _():
                dg_ref[...] = dg

            @pl.when(i > 0)
            def _():
                dg_ref[...] += dg

    return pl.pallas_call(
        body, grid=(S // ts, nj), name=name,
        in_specs=[pl.BlockSpec((ts, tn), lambda i, j: (i, j)), pl.BlockSpec((tn, D), lambda i, j: (j, 0)), row, vec, row],
        out_specs=[row, vec],
        out_shape=[jax.ShapeDtypeStruct((S, D), F32), jax.ShapeDtypeStruct((1, D), F32)],
        scratch_shapes=[pltpu.VMEM((ts, D), F32)],
        compiler_params=_params(52, ("arbitrary", "arbitrary")),
    )(du, w, h, g, dres)


def _loss_head(h, g, tgt, ts=256):
    row = pl.BlockSpec((ts, D), lambda i: (i, 0))
    vec = pl.BlockSpec((1, D), lambda i: (0, 0))

    def body(h_ref, g_ref, t_ref, dh_ref, dg_ref, loss_ref):
        i = pl.program_id(0)

        def rows_fn(rows, carry):
            dg, part = carry
            x = h_ref[rows, :]
            gv = g_ref[...]
            err = _rms(x, gv) - t_ref[rows, :]
            part = part + 0.5 * jnp.sum(jnp.mean(err * err, axis=-1, keepdims=True), axis=0, keepdims=True)
            dx, dg_rows = _rms_bwd(err * (1.0 / D), x, gv)
            dh_ref[rows, :] = dx
            return dg + dg_rows, part
        dg, part = _row_loop(ts, rows_fn, (jnp.zeros((1, D), F32), jnp.zeros((1, 1), F32)))

        @pl.when(i == 0)
        def _():
            dg_ref[...] = dg
            loss_ref[...] = jnp.broadcast_to(part, loss_ref.shape)

        @pl.when(i > 0)
        def _():
            dg_ref[...] += dg
            loss_ref[...] += jnp.broadcast_to(part, loss_ref.shape)

    return pl.pallas_call(
        body, grid=(S // ts,), name="loss_head",
        in_specs=[row, vec, row], out_specs=[row, vec, pl.BlockSpec((1, 128), lambda i: (0, 0))],
        out_shape=[jax.ShapeDtypeStruct((S, D), F32), jax.ShapeDtypeStruct((1, D), F32),
                   jax.ShapeDtypeStruct((1, 128), F32)],
        compiler_params=_params(40, ("arbitrary",)),
    )(h, g, tgt)


def _pool_specs():
    blk = pl.BlockSpec((S, PG), lambda gi: (0, gi))
    wp = pl.BlockSpec((None, PG, PG), lambda gi: (gi, 0, 0))
    sc = pl.BlockSpec((1, PG), lambda gi: (0, gi))
    return blk, wp, sc


def _pool_fwd(u, wp_b, scale):
    blk, wp, sc = _pool_specs()

    def body(u_ref, wp_ref, sc_ref, y_ref, pooled_ref):
        win = 2 << pl.program_id(0)
        row = lax.broadcasted_iota(jnp.int32, (S, PG), 0)
        x = u_ref[...]
        s = x
        for k in (1, 2, 4, 8):
            s = s + jnp.where((row >= k) & (k < win), pltpu.roll(s, k, 0), 0.0)
        cnt = jnp.minimum(row + 1, win).astype(F32)
        pooled = (s / cnt - x).astype(BF16)
        pooled_ref[...] = pooled
        y_ref[...] = (_dot(pooled, wp_ref[...]) * sc_ref[...]).astype(BF16)

    return pl.pallas_call(
        body, grid=(len(POOL_WINDOWS),), name="pool_fwd", in_specs=[blk, wp, sc], out_specs=[blk, blk],
        out_shape=[jax.ShapeDtypeStruct((S, D_POOL), BF16), jax.ShapeDtypeStruct((S, D_POOL), BF16)],
        compiler_params=_params(40),
    )(u, wp_b, scale)


def _pool_bwd(dy, pooled, wp_b, scale):
    blk, wp, sc = _pool_specs()

    def body(dy_ref, p_ref, wp_ref, sc_ref, du_ref, dwp_ref, dsc_ref):
        win = 2 << pl.program_id(0)
        row = lax.broadcasted_iota(jnp.int32, (S, PG), 0)
        dyv = dy_ref[...]
        pooled = p_ref[...]
        w = wp_ref[...]
        dsc_ref[...] = jnp.sum(dyv * _dot(pooled, w), axis=0, keepdims=True)
        dz = (dyv * sc_ref[...]).astype(BF16)
        dwp_ref[...] = _dot_tn(pooled, dz)
        dpooled = _dot_nt(dz, w)
        cnt = jnp.minimum(row + 1, win).astype(F32)
        fs = dpooled / cnt
        for k in (1, 2, 4, 8):
            fs = fs + jnp.where((row < S - k) & (k < win), pltpu.roll(fs, S - k, 0), 0.0)
        du_ref[...] = (fs - dpooled).astype(BF16)

    return pl.pallas_call(
        body, grid=(len(POOL_WINDOWS),), name="pool_bwd", in_specs=[blk, blk, wp, sc], out_specs=[blk, wp, sc],
        out_shape=[jax.ShapeDtypeStruct((S, D_POOL), BF16), jax.ShapeDtypeStruct((len(POOL_WINDOWS), PG, PG), F32),
                   jax.ShapeDtypeStruct((1, D_POOL), F32)],
        compiler_params=_params(40),
    )(dy, pooled, wp_b, scale)


def _gla_in_specs(chunk_of):
    def at(width, col):
        return pl.BlockSpec((CH, width), lambda n: (chunk_of(n), col))
    return [at(H * DK, O_Q // (H * DK)), at(H * DK, O_K // (H * DK)), at(H * DV, O_V // (H * DV)),
            at(H * DV, O_G // (H * DV)), at(RANK_PAD, O_R // RANK_PAD)]


def _gla_decay_terms(lr_ref, wa_ref, ba_ref, q_ref, k_ref):
    row = lax.broadcasted_iota(jnp.int32, (CH, CH), 0)
    col = lax.broadcasted_iota(jnp.int32, (CH, CH), 1)
    tril = row >= col
    z = _dot(lr_ref[...].astype(BF16), wa_ref[...]) + ba_ref[...]
    la = _log_sigmoid(z) / GATE_NORM
    b = _tri_dot(jnp.where(tril, 1.0, 0.0).astype(BF16), la)
    bl = jnp.sum(la, axis=0, keepdims=True)
    e_b, e_nb, e_tb = jnp.exp(b), jnp.exp(-b), jnp.exp(bl - b)
    kk = k_ref[...]
    q_dec = (q_ref[...] * QK_SCALE) * e_b
    return tril, z, e_b, e_nb, e_tb, jnp.exp(bl), q_dec, kk * e_nb, kk * e_tb


def _gla_fwd(u, wa_b, ba, gn):
    wide = pl.BlockSpec((CH, H * DV), lambda n: (n, 0))

    def body(q_ref, k_ref, v_ref, g_ref, lr_ref, wa_ref, ba_ref, gn_ref, y_ref, o_ref, st_ref, state):
        @pl.when(pl.program_id(0) == 0)
        def _():
            state[...] = jnp.zeros_like(state)

        tril, _, _, _, _, dec, q_dec, k_inv, k_tail = _gla_decay_terms(lr_ref, wa_ref, ba_ref, q_ref, k_ref)
        for hd in range(H):
            ks, vs = slice(hd * DK, (hd + 1) * DK), slice(hd * DV, (hd + 1) * DV)
            qb, kib, ktb = q_dec[:, ks].astype(BF16), k_inv[:, ks].astype(BF16), k_tail[:, ks].astype(BF16)
            vb = v_ref[:, vs].astype(BF16)
            p = jnp.where(tril, _dot_nt(qb, kib), 0.0)
            st = state[hd]
            o = _dot(p.astype(BF16), vb) + _dot_nt(qb, st.astype(BF16))
            st_ref[hd] = st
            state[hd] = st * dec[:, ks] + _dot_tn(vb, ktb)
            o_ref[:, vs] = o
            on = _rms(o, gn_ref[...])
            gg = g_ref[:, vs]
            y_ref[:, vs] = (on * (gg * _sigmoid(gg))).astype(BF16)

    return pl.pallas_call(
        body, grid=(NCH,), name="gla_fwd",
        in_specs=_gla_in_specs(lambda n: n) + [pl.BlockSpec((RANK_PAD, H * DK), lambda n: (0, 0)),
                                               pl.BlockSpec((1, H * DK), lambda n: (0, 0)),
                                               pl.BlockSpec((1, DV), lambda n: (0, 0))],
        out_specs=[wide, wide, pl.BlockSpec((None, H, DV, DK), lambda n: (n, 0, 0, 0))],
        out_shape=[jax.ShapeDtypeStruct((S, H * DV), BF16), jax.ShapeDtypeStruct((S, H * DV), F32),
                   jax.ShapeDtypeStruct((NCH, H, DV, DK), F32)],
        scratch_shapes=[pltpu.VMEM((H, DV, DK), F32)],
        compiler_params=_params(32, ("arbitrary",)),
    )(u, u, u, u, u, wa_b, ba, gn)


GLA_DU = 2 * H * DK + 2 * H * DV + RANK_PAD


def _gla_bwd(u, o_arr, states, dy, wa_b, ba, gn):
    rev = lambda n: NCH - 1 - n
    wide = pl.BlockSpec((CH, H * DV), lambda n: (rev(n), 0))

    def body(q_ref, k_ref, v_ref, g_ref, lr_ref, o_ref, st_ref, dy_ref, wa_ref, ba_ref, gn_ref,
             du_ref, dwa_ref, dba_ref, dgn_ref, gstate, db_scr, dbl_scr):
        @pl.when(pl.program_id(0) == 0)
        def _():
            gstate[...] = jnp.zeros_like(gstate)
            dwa_ref[...] = jnp.zeros_like(dwa_ref)
            dba_ref[...] = jnp.zeros_like(dba_ref)
            dgn_ref[...] = jnp.zeros_like(dgn_ref)

        tril, z, e_b, e_nb, e_tb, dec, q_dec, k_inv, k_tail = _gla_decay_terms(lr_ref, wa_ref, ba_ref, q_ref, k_ref)
        gnv = gn_ref[...]
        dgn = jnp.zeros((1, DV), F32)
        for hd in range(H):
            ks, vs = slice(hd * DK, (hd + 1) * DK), slice(hd * DV, (hd + 1) * DV)
            qh, kih, kth = q_dec[:, ks], k_inv[:, ks], k_tail[:, ks]
            qb, kib, ktb = qh.astype(BF16), kih.astype(BF16), kth.astype(BF16)
            vb = v_ref[:, vs].astype(BF16)
            o = o_ref[:, vs]
            gg = g_ref[:, vs]
            dyh = dy_ref[:, vs]
            r = lax.rsqrt(jnp.mean(o * o, axis=-1, keepdims=True) + EPS)
            xh = o * r
            sg = _sigmoid(gg)
            dgate = dyh * (xh * gnv) * (sg * (1.0 + gg * (1.0 - sg)))
            don = dyh * (gg * sg)
            dgn = dgn + jnp.sum(don * xh, axis=0, keepdims=True)
            dxh = don * gnv
            d_o = (r * (dxh - xh * jnp.mean(dxh * xh, axis=-1, keepdims=True))).astype(BF16)
            pb = jnp.where(tril, _dot_nt(qb, kib), 0.0).astype(BF16)
            dpb = jnp.where(tril, _dot_nt(d_o, vb), 0.0).astype(BF16)
            gt = gstate[hd]
            gtb = gt.astype(BF16)
            st = st_ref[hd]
            dv = _dot_tn(pb, d_o) + _dot_nt(ktb, gtb)
            dq_dec = _dot(dpb, kib) + _dot(d_o, st.astype(BF16))
            dk_inv = _dot_tn(dpb, qb)
            dk_tail = _dot(vb, gtb)
            ddec = jnp.sum(gt * st, axis=0, keepdims=True)
            gstate[hd] = _dot_tn(d_o, qb) + dec[:, ks] * gt
            du_ref[:, ks] = (dq_dec * QK_SCALE * e_b[:, ks]).astype(BF16)
            du_ref[:, H * DK + hd * DK:H * DK + (hd + 1) * DK] = (dk_inv * e_nb[:, ks] + dk_tail * e_tb[:, ks]).astype(BF16)
            du_ref[:, 2 * H * DK + hd * DV:2 * H * DK + (hd + 1) * DV] = dv.astype(BF16)
            du_ref[:, 2 * H * DK + H * DV + hd * DV:2 * H * DK + H * DV + (hd + 1) * DV] = dgate.astype(BF16)
            db_scr[:, ks] = dq_dec * qh - dk_inv * kih - dk_tail * kth
            dbl_scr[:, ks] = jnp.sum(dk_tail * kth, axis=0, keepdims=True) + ddec * dec[:, ks]
        dgn_ref[...] += dgn
        row = lax.broadcasted_iota(jnp.int32, (CH, CH), 0)
        col = lax.broadcasted_iota(jnp.int32, (CH, CH), 1)
        dla = _tri_dot(jnp.where(row <= col, 1.0, 0.0).astype(BF16), db_scr[...]) + dbl_scr[...]
        dz = dla * (1.0 / GATE_NORM) * _sigmoid(-z)
        dzb = dz.astype(BF16)
        du_ref[:, GLA_DU - RANK_PAD:] = _dot_nt(dzb, wa_ref[...]).astype(BF16)
        dwa_ref[...] += _dot_tn(lr_ref[...].astype(BF16), dzb)
        dba_ref[...] += jnp.sum(dz, axis=0, keepdims=True)

    full = lambda shape: pl.BlockSpec(shape, lambda n: (0,) * len(shape))
    return pl.pallas_call(
        body, grid=(NCH,), name="gla_bwd",
        in_specs=_gla_in_specs(rev) + [wide, pl.BlockSpec((None, H, DV, DK), lambda n: (rev(n), 0, 0, 0)),
                                       pl.BlockSpec((CH, H * DV), lambda n: (rev(n), 1)),
                                       full((RANK_PAD, H * DK)), full((1, H * DK)), full((1, DV))],
        out_specs=[pl.BlockSpec((CH, GLA_DU), lambda n: (rev(n), 0)), full((RANK_PAD, H * DK)), full((1, H * DK)),
                   full((1, DV))],
        out_shape=[jax.ShapeDtypeStruct((S, GLA_DU), BF16), jax.ShapeDtypeStruct((RANK_PAD, H * DK), F32),
                   jax.ShapeDtypeStruct((1, H * DK), F32), jax.ShapeDtypeStruct((1, DV), F32)],
        scratch_shapes=[pltpu.VMEM((H, DV, DK), F32), pltpu.VMEM((CH, H * DK), F32), pltpu.VMEM((1, H * DK), F32)],
        compiler_params=_params(32, ("arbitrary",)),
    )(u, u, u, u, u, o_arr, states, dy, wa_b, ba, gn)


def _ffn_weight_grads(n, dgu, act, df, tag, after=()):
    d_in = _tn_matmul(n, dgu, pl.BlockSpec((S, 512), lambda s, m: (0, m)),
                      pl.BlockSpec((None, S, WIN_SHARD), lambda s, m: (s // (NDEV // 2), 0, s % (NDEV // 2))),
                      jax.ShapeDtypeStruct((D, 2 * FF), BF16), pl.BlockSpec((512, WIN_SHARD), lambda s, m: (m, s)),
                      (NDEV, D // 512), tag + "_dw_in", 32, after)
    d_out = _tn_matmul(act, df, pl.BlockSpec((S, 512), lambda m: (0, m)), pl.BlockSpec((S, D), lambda m: (0, 0)),
                       jax.ShapeDtypeStruct((FF, D), BF16), pl.BlockSpec((512, D), lambda m: (m, 0)),
                       (FF // 512,), tag + "_dw_out", 40)
    return d_in, d_out


def _fwd_ffn1(x, w, after=()):
    return _ffn_fwd(x, w["ffn1_norm"], w["ffn1_w_in"], w["ffn1_w_out"], "ffn1_fwd", after)


def _fwd_mixer(h1, w, after=()):
    u, n2 = _norm_matmul(h1, w["mix_norm"], w["w_in_mix"], "mix_in", after)
    y_pool, pooled = _pool_fwd(u, w["w_pool"], w["pool_scale"])
    y_gla, o_gla, states = _gla_fwd(u, w["w_alpha"], w["b_alpha"], w["gla_norm"])
    y = jnp.concatenate([y_pool, y_gla], axis=1)
    h2 = _matmul_residual(y, w["w_out_mix"], h1, "mix_out")
    return h2, dict(u=u, n2=n2, pooled=pooled, o_gla=o_gla, states=states, y=y)


def _fwd_ffn2_loss(h2, tgt, w, after=()):
    h3, n3, gu3 = _ffn_fwd(h2, w["ffn2_norm"], w["ffn2_w_in"], w["ffn2_w_out"], "ffn2_fwd", after)
    dh3, d_final, loss = _loss_head(h3, w["final_norm"], tgt)
    return dh3, d_final, loss, n3, gu3


def _bwd_ffn2(dh3, h2, n3, gu3, w):
    dgu3, act3, df3, dh2, dh2b, d_norm = _ffn_bwd_x(dh3, h2, w["ffn2_norm"], gu3, w["ffn2_w_in"], w["ffn2_w_out"], "ffn2_bwd")
    d_in, d_out = _ffn_weight_grads(n3, dgu3, act3, df3, "ffn2")
    return dh2, dh2b, dict(ffn2_norm=d_norm, ffn2_w_in=d_in, ffn2_w_out=d_out)


def _bwd_mixer(dh2, dh2b, h1, sv, w, after=()):
    dy = _nt_matmul(dh2b, w["w_out_mix"], "mix_out_bwd", after)
    d_wo = _tn_matmul(sv["y"], dh2b, pl.BlockSpec((S, 512), lambda m: (0, m)), pl.BlockSpec((S, D), lambda m: (0, 0)),
                      jax.ShapeDtypeStruct((D, D), BF16), pl.BlockSpec((512, D), lambda m: (m, 0)), (D // 512,),
                      "mix_out_dw", 40)
    du_pool, d_wpool, d_scale = _pool_bwd(dy, sv["pooled"], w["w_pool"], w["pool_scale"])
    du_gla, d_walpha, d_balpha, d_gnorm = _gla_bwd(sv["u"], sv["o_gla"], sv["states"], dy, w["w_alpha"], w["b_alpha"],
                                                   w["gla_norm"])
    du = jnp.concatenate([du_pool, du_gla], axis=1)
    dh1, d_mix = _matmul_normbwd(du, w["w_in_mix"], h1, w["mix_norm"], dh2, "mix_in_bwd")
    d_wmix = _tn_matmul(du, sv["n2"], pl.BlockSpec((S, 1408), lambda j, m: (0, j)), pl.BlockSpec((S, 512), lambda j, m: (0, m)),
                        jax.ShapeDtypeStruct((D_IN_PAD, D), BF16), pl.BlockSpec((1408, 512), lambda j, m: (j, m)),
                        (D_IN_PAD // 1408, D // 512), "mix_in_dw", 32)
    return dh1, dict(mix_norm=d_mix, w_in_mix=d_wmix, w_pool=d_wpool, pool_scale=d_scale, w_alpha=d_walpha,
                     b_alpha=d_balpha, gla_norm=d_gnorm, w_out_mix=d_wo)


def _bwd_ffn1(dh1, x, n1, gu1, w, after=(), between=None):
    dgu1, act1, df1, dx, _, d_norm = _ffn_bwd_x(dh1, x, w["ffn1_norm"], gu1, w["ffn1_w_in"], w["ffn1_w_out"], "ffn1_bwd", after)
    d_in, d_out = _ffn_weight_grads(n1, dgu1, act1, df1, "ffn1", () if between is None else between(dx))
    return dx, dict(ffn1_norm=d_norm, ffn1_w_in=d_in, ffn1_w_out=d_out)


def _local_step(x, tgt, w):
    h1, n1, gu1 = _fwd_ffn1(x, w)
    h2, sv = _fwd_mixer(h1, w)
    dh3, d_final, loss, n3, gu3 = _fwd_ffn2_loss(h2, tgt, w)
    dh2, dh2b, g2 = _bwd_ffn2(dh3, h2, n3, gu3, w)
    dh1, gm = _bwd_mixer(dh2, dh2b, h1, sv, w)
    dx, g1 = _bwd_ffn1(dh1, x, n1, gu1, w)
    return loss, dx, dict(final_norm=d_final, **g1, **gm, **g2)


def _mesh_index():
    return 4 * lax.axis_index("x") + 2 * lax.axis_index("y") + lax.axis_index("c")


def _coords(p):
    return (p // 4, (p // 2) % 2, p % 2)


NCHIP = 4


def _place():
    return lax.axis_index("x"), lax.axis_index("y"), lax.axis_index("c")


def _rel_chip(x, y, rel):
    return ((1 - x) if rel & 1 else x, (1 - y) if rel & 2 else y)


def _dev_index(x, y, c):
    return 4 * x + 2 * y + c


def _cols(ref, p, width):
    return ref.at[:, pl.ds(pl.multiple_of(p * width, 128), width)]


def _sems(na, n):
    return [pltpu.SemaphoreType.DMA((na, n)), pltpu.SemaphoreType.DMA((na, n)), pltpu.SemaphoreType.DMA((na,))]


def _gather(items, name):
    arrays = [a for a, _ in items]
    kinds = [k for _, k in items]
    na = len(arrays)
    out_shape = [jax.ShapeDtypeStruct((NDEV,) + a.shape if k == "bcast" else (a.shape[0], NDEV * a.shape[1]), a.dtype)
                 for a, k in items]

    def body(*refs):
        ins, outs = refs[:na], refs[na:2 * na]
        send_sems, recv_sems, local_sems = refs[2 * na:]
        x, y, c = _place()
        sibling = (x, y, 1 - c)

        def slab(a, s):
            return _cols(outs[a], s, ins[a].shape[1]) if kinds[a] == "bcast_cols" else outs[a].at[s]

        def copy(a, k, block, to, own=False):
            return pltpu.make_async_remote_copy(ins[a] if own else slab(a, block), slab(a, block), send_sems.at[a, k],
                                                recv_sems.at[a, k], device_id=to, device_id_type=MESH)

        me = _dev_index(x, y, c)
        local = [pltpu.make_async_copy(ins[a], slab(a, me), local_sems.at[a]) for a in range(na)]
        sent = []
        for cp in local:
            cp.start()
        for rel in range(1, NCHIP):
            for a in range(na):
                sent.append(copy(a, rel, me, (*_rel_chip(x, y, rel), c), own=True))
                sent[-1].start()
        for a in range(na):
            sent.append(copy(a, 0, me, sibling, own=True))
            sent[-1].start()
        for rel in range(1, NCHIP):
            src_dev = _dev_index(*_rel_chip(x, y, rel), c)
            for a in range(na):
                copy(a, rel, src_dev, sibling).wait_recv()
                sent.append(copy(a, NCHIP - 1 + rel, src_dev, sibling))
                sent[-1].start()
        for a in range(na):
            copy(a, 0, _dev_index(x, y, 1 - c), sibling).wait_recv()
        for rel in range(1, NCHIP):
            for a in range(na):
                copy(a, NCHIP - 1 + rel, _dev_index(*_rel_chip(x, y, rel), 1 - c), sibling).wait_recv()
        for cp in sent:
            cp.wait_send()
        for cp in local:
            cp.wait()

    return pl.pallas_call(body, name=name, in_specs=[ANY] * na, out_specs=[ANY] * na, out_shape=out_shape,
                          scratch_shapes=_sems(na, 2 * NCHIP - 1))(*arrays)


def _pair_exchange(items, name):
    arrays = [a for a, _ in items]
    kinds = [k for _, k in items]
    na = len(arrays)
    out_shape = [jax.ShapeDtypeStruct((NCHIP,) + (a.shape[1:] if k == "scatter" else (a.shape[0], a.shape[1] // NDEV)), a.dtype)
                 for a, k in items]

    def body(*refs):
        ins, outs = refs[:na], refs[na:2 * na]
        send_sems, recv_sems = refs[2 * na:]
        x, y, c = _place()
        copies = []
        for rel in range(NCHIP):
            p = _dev_index(*_rel_chip(x, y, rel), 1 - c)
            for a in range(na):
                src = ins[a].at[p] if kinds[a] == "scatter" else _cols(ins[a], p, ins[a].shape[1] // NDEV)
                copies.append(pltpu.make_async_remote_copy(src, outs[a].at[rel], send_sems.at[a, rel], recv_sems.at[a, rel],
                                                           device_id=(x, y, 1 - c), device_id_type=MESH))
                copies[-1].start()
        for cp in copies:
            cp.wait_send()
            cp.wait_recv()

    return pl.pallas_call(body, name=name, in_specs=[ANY] * na, out_specs=[ANY] * na, out_shape=out_shape,
                          scratch_shapes=_sems(na, NCHIP)[:2])(*arrays)


def _pair_add(own, kind, got, table, tr, name):
    _, rows, cols = got.shape
    if kind == "scatter":
        own_spec = pl.BlockSpec((None, tr, cols), lambda rel, i, t: (t[rel], i, 0))
    else:
        own_spec = pl.BlockSpec((tr, cols), lambda rel, i, t: (i, t[rel]))
    blk = pl.BlockSpec((None, tr, cols), lambda rel, i, t: (rel, i, 0))

    def body(t_ref, a_ref, b_ref, o_ref):
        o_ref[...] = (a_ref[...].astype(F32) + b_ref[...].astype(F32)).astype(o_ref.dtype)

    return pl.pallas_call(
        body, name=name, out_shape=jax.ShapeDtypeStruct(got.shape, got.dtype),
        grid_spec=pltpu.PrefetchScalarGridSpec(num_scalar_prefetch=1, grid=(NCHIP, rows // tr), in_specs=[own_spec, blk],
                                               out_specs=blk),
        compiler_params=_params(32),
    )(table, own, got)


def _exchange(items, name):
    arrays = [a for a, _ in items]
    kinds = [k for _, k in items]
    na = len(arrays)
    out_shape = [jax.ShapeDtypeStruct((NDEV,) + a.shape if k == "bcast" else a.shape, a.dtype) for a, k in items]

    def body(*refs):
        ins, outs = refs[:na], refs[na:2 * na]
        send_sems, recv_sems, local_sems = refs[2 * na:]
        x, y, c = _place()
        me = _dev_index(x, y, c)

        def mine(a):
            return 0 if kinds[a] == "chip" else me

        def src(a, p):
            return ins[a] if kinds[a] == "bcast" else ins[a].at[p]

        local = [pltpu.make_async_copy(src(a, mine(a)), outs[a].at[mine(a)], local_sems.at[a]) for a in range(na)]
        for cp in local:
            cp.start()
        waits = []
        for k in range(1, NDEV):
            to = (me + k) % NDEV
            frm = (me + NDEV - k) % NDEV
            for a in range(na):
                if kinds[a] == "chip":
                    if k >= NCHIP:
                        continue
                    send = pltpu.make_async_remote_copy(ins[a].at[k], outs[a].at[k], send_sems.at[a, k], recv_sems.at[a, k],
                                                        device_id=(*_rel_chip(x, y, k), c), device_id_type=MESH)
                    waits.append(send)
                else:
                    send = pltpu.make_async_remote_copy(src(a, to), outs[a].at[me], send_sems.at[a, k], recv_sems.at[a, k],
                                                        device_id=_coords(to), device_id_type=MESH)
                    waits.append(pltpu.make_async_remote_copy(src(a, to), outs[a].at[frm], send_sems.at[a, k],
                                                              recv_sems.at[a, k], device_id=_coords(to), device_id_type=MESH))
                send.start()
        for cp in waits:
            cp.wait_send()
            cp.wait_recv()
        for cp in local:
            cp.wait()

    return pl.pallas_call(body, name=name, in_specs=[ANY] * na, out_specs=[ANY] * na, out_shape=out_shape,
                          scratch_shapes=_sems(na, NDEV))(*arrays)


HBM = pl.BlockSpec(memory_space=pltpu.HBM)
SEM = pl.BlockSpec(memory_space=pltpu.SEMAPHORE)
DATAFLOW = pltpu.SideEffectType.DATAFLOW_SIDE_EFFECTING


def _pair_copies(kinds):
    def describe(srcs, lands, send_sems, recv_sems):
        x, y, c = _place()
        na = len(srcs)
        for rel in range(NCHIP):
            p = _dev_index(*_rel_chip(x, y, rel), 1 - c)
            for a in range(na):
                src = srcs[a].at[p] if kinds[a] == "scatter" else _cols(srcs[a], p, srcs[a].shape[1] // NDEV)
                cp = pltpu.make_async_remote_copy(src, lands[a].at[rel], send_sems.at[rel * na + a], recv_sems.at[rel * na + a],
                                                  device_id=(x, y, 1 - c), device_id_type=MESH)
                yield cp, cp
    return describe


def _chip_copies(srcs, lands, send_sems, recv_sems):
    x, y, c = _place()
    na = len(srcs)
    for rel in range(1, NCHIP):
        for a in range(na):
            i = (rel - 1) * na + a
            cp = pltpu.make_async_remote_copy(srcs[a].at[rel], lands[a].at[rel], send_sems.at[i], recv_sems.at[i],
                                              device_id=(*_rel_chip(x, y, rel), c), device_id_type=MESH)
            yield cp, cp


def _slab(ref, kind, s, width):
    return _cols(ref, s, width) if kind == "bcast_cols" else ref.at[s]


def _gather_copies(kinds):
    def describe(srcs, lands, send_sems, recv_sems):
        x, y, c = _place()
        na = len(srcs)
        me = _dev_index(x, y, c)
        for rel in range(NCHIP):
            to = (x, y, 1 - c) if rel == 0 else (*_rel_chip(x, y, rel), c)
            for a in range(na):
                width = srcs[a].shape[-1]
                i = rel * na + a
                send = pltpu.make_async_remote_copy(srcs[a], _slab(lands[a], kinds[a], me, width), send_sems.at[i], recv_sems.at[i],
                                                    device_id=to, device_id_type=MESH)
                arrival = pltpu.make_async_remote_copy(srcs[a], _slab(lands[a], kinds[a], _dev_index(*to), width), send_sems.at[i],
                                                       recv_sems.at[i], device_id=to, device_id_type=MESH)
                yield send, arrival
    return describe


def _pass_to_sibling(arrays, kinds, widths, name):
    na = len(arrays)

    def body(*refs):
        bufs = refs[na:2 * na]
        send_sems, recv_sems = refs[2 * na:]
        x, y, c = _place()
        copies = []
        for rel in range(1, NCHIP):
            for a in range(na):
                mine = _slab(bufs[a], kinds[a], _dev_index(*_rel_chip(x, y, rel), c), widths[a])
                theirs = _slab(bufs[a], kinds[a], _dev_index(*_rel_chip(x, y, rel), 1 - c), widths[a])
                send = pltpu.make_async_remote_copy(mine, mine, send_sems.at[a, rel], recv_sems.at[a, rel],
                                                    device_id=(x, y, 1 - c), device_id_type=MESH)
                send.start()
                copies.append((send, pltpu.make_async_remote_copy(theirs, theirs, send_sems.at[a, rel], recv_sems.at[a, rel],
                                                                  device_id=(x, y, 1 - c), device_id_type=MESH)))
        for send, arrival in copies:
            send.wait_send()
            arrival.wait_recv()

    return pl.pallas_call(body, name=name, in_specs=[ANY] * na, out_specs=[ANY] * na,
                          out_shape=[jax.ShapeDtypeStruct(a.shape, a.dtype) for a in arrays],
                          input_output_aliases={i: i for i in range(na)}, scratch_shapes=_sems(na, NCHIP)[:2])(*arrays)


def _start_copies(name, srcs, lands, describe, ncopies, after=()):
    arrays = list(srcs) + list(lands)
    ns, n, nin = len(srcs), len(arrays), len(arrays) + len(after)

    def body(*refs):
        for send, _ in describe(refs[:ns], refs[ns:n], refs[nin], refs[nin + 1]):
            send.start()
        refs[-1][...] = jnp.zeros_like(refs[-1])

    out = pl.pallas_call(
        body, name=name,
        out_shape=(pltpu.SemaphoreType.DMA((ncopies,)), pltpu.SemaphoreType.DMA((ncopies,)),
                   *[pltpu.HBM(a.shape, a.dtype) for a in arrays], jax.ShapeDtypeStruct((8, 128), F32)),
        in_specs=[HBM] * n + [ANY] * len(after), out_specs=(SEM, SEM, *[HBM] * n, pl.BlockSpec(memory_space=pltpu.VMEM)),
        input_output_aliases={i: 2 + i for i in range(n)},
        compiler_params=pltpu.CompilerParams(has_side_effects=DATAFLOW),
    )(*[pltpu.with_memory_space_constraint(a, pltpu.HBM) for a in arrays], *after)
    return out[0], out[1], list(out[2:2 + n]), out[-1]


def _wait_copies(name, send_sems, recv_sems, thru, ns, describe, after):
    n = len(thru)

    def body(*refs):
        for send, arrival in describe(refs[:ns], refs[ns:n], refs[n], refs[n + 1]):
            send.wait_send()
            arrival.wait_recv()

    out = pl.pallas_call(
        body, name=name, out_shape=tuple(pltpu.HBM(a.shape, a.dtype) for a in thru),
        in_specs=[HBM] * n + [SEM, SEM] + [ANY] * len(after), out_specs=tuple([HBM] * n),
        input_output_aliases={i: i for i in range(n)},
        compiler_params=pltpu.CompilerParams(has_side_effects=DATAFLOW),
    )(*thru, send_sems, recv_sems, *after)
    return list(out[:ns]), list(out[ns:])


def _adamw(parts, w, m, v, tr, name, tc=None):
    rows, cols = w.shape
    tc = cols if tc is None else tc
    nparts = len(parts)
    blk = pl.BlockSpec((tr, tc), lambda i, j: (i, j))

    def slab_spec(s):
        return pl.BlockSpec((None, tr, tc), lambda i, j: (s, i, j))

    def body(*refs):
        p_refs = refs[:nparts]
        w_ref, m_ref, v_ref, g_ref, d_ref, nm_ref, nv_ref = refs[nparts:]
        g = p_refs[0][...].astype(F32)
        for p_ref in p_refs[1:]:
            g = g + p_ref[...].astype(F32)
        nm = ADAM_B1 * m_ref[...] + (1.0 - ADAM_B1) * g
        nv = ADAM_B2 * v_ref[...] + (1.0 - ADAM_B2) * (g * g)
        m_hat = nm / (1.0 - ADAM_B1 ** ADAM_STEP)
        v_hat = nv / (1.0 - ADAM_B2 ** ADAM_STEP)
        g_ref[...] = g
        d_ref[...] = -ADAM_LR * (m_hat / (jnp.sqrt(v_hat) + ADAM_EPS) + ADAM_WD * w_ref[...])
        nm_ref[...] = nm
        nv_ref[...] = nv

    return pl.pallas_call(
        body, grid=(rows // tr, cols // tc), name=name,
        in_specs=[slab_spec(s) for _, s in parts] + [blk, blk, blk], out_specs=[blk] * 4,
        out_shape=[jax.ShapeDtypeStruct((rows, cols), F32)] * 4,
        compiler_params=_params(40),
    )(*[a for a, _ in parts], w, m, v)


def _pack_small(vals, extra=None):
    flat = [vals[n].reshape(-1).astype(F32) for n, _ in SMALL]
    tail = jnp.zeros((SMALL_ROWS * 128 - LOSS_AT,), F32)
    if extra is not None:
        tail = tail.at[0].set(extra)
    return jnp.concatenate(flat + [tail]).reshape(SMALL_ROWS, 128)


def _unpack_small(packed, like):
    flat, out, at = packed.reshape(-1), {}, 0
    for n, size in SMALL:
        out[n] = flat[at:at + size].reshape(like[n].shape)
        at += size
    return out, flat[LOSS_AT]


def kernel(x, ffn1_norm, ffn1_w_in, ffn1_w_out, mix_norm, w_in_mix, w_pool, pool_scale, w_alpha, b_alpha, gla_norm, w_out_mix, ffn2_norm, ffn2_w_in, ffn2_w_out, final_norm, loss_target, m_ffn1_norm, m_ffn1_w_in, m_ffn1_w_out, m_mix_norm, m_w_in_mix, m_w_pool, m_pool_scale, m_w_alpha, m_b_alpha, m_gla_norm, m_w_out_mix, m_ffn2_norm, m_ffn2_w_in, m_ffn2_w_out, m_final_norm, v_ffn1_norm, v_ffn1_w_in, v_ffn1_w_out, v_mix_norm, v_w_in_mix, v_w_pool, v_pool_scale, v_w_alpha, v_b_alpha, v_gla_norm, v_w_out_mix, v_ffn2_norm, v_ffn2_w_in, v_ffn2_w_out, v_final_norm):
    names = ["ffn1_norm", "ffn1_w_in", "ffn1_w_out", "mix_norm", "w_in_mix", "w_pool", "pool_scale", "w_alpha", "b_alpha",
             "gla_norm", "w_out_mix", "ffn2_norm", "ffn2_w_in", "ffn2_w_out", "final_norm"]
    p = dict(zip(names, [ffn1_norm, ffn1_w_in, ffn1_w_out, mix_norm, w_in_mix, w_pool, pool_scale, w_alpha, b_alpha,
                         gla_norm, w_out_mix, ffn2_norm, ffn2_w_in, ffn2_w_out, final_norm]))
    m = dict(zip(names, [m_ffn1_norm, m_ffn1_w_in, m_ffn1_w_out, m_mix_norm, m_w_in_mix, m_w_pool, m_pool_scale, m_w_alpha,
                         m_b_alpha, m_gla_norm, m_w_out_mix, m_ffn2_norm, m_ffn2_w_in, m_ffn2_w_out, m_final_norm]))
    v = dict(zip(names, [v_ffn1_norm, v_ffn1_w_in, v_ffn1_w_out, v_mix_norm, v_w_in_mix, v_w_pool, v_pool_scale, v_w_alpha,
                         v_b_alpha, v_gla_norm, v_w_out_mix, v_ffn2_norm, v_ffn2_w_in, v_ffn2_w_out, v_final_norm]))

    mx, my, mc = _place()
    me = _dev_index(mx, my, mc)
    table = jnp.stack([_dev_index(*_rel_chip(mx, my, rel), mc) for rel in range(NCHIP)]).astype(jnp.int32)

    def landing(shard, kind):
        if kind == "bcast_cols":
            buf = lax.empty((shard.shape[0], NDEV * shard.shape[1]), shard.dtype)
            return lax.dynamic_update_slice(buf, shard, (0, me * shard.shape[1]))
        return lax.dynamic_update_slice(lax.empty((NDEV,) + shard.shape, shard.dtype), shard[None], (me,) + (0,) * shard.ndim)

    def gather_begin(items, tag, after):
        kinds = [kind for _, kind in items]
        copies = _gather_copies(kinds)
        s, r, thru, tok = _start_copies(tag + "_start", [a for a, _ in items], [landing(a, kind) for a, kind in items], copies,
                                        NCHIP * len(items), after)
        return (s, r, thru, copies, kinds, [a.shape[-1] for a, _ in items], tag), tok

    def gather_end(state, after):
        s, r, thru, copies, kinds, widths, tag = state
        _, lands = _wait_copies(tag + "_wait", s, r, thru, len(kinds), copies, after)
        return _pass_to_sibling(lands, kinds, widths, tag + "_pass")

    def shard16(n):
        return p[n][0].astype(BF16)

    g_w1in, g_w1out = _gather([(shard16("ffn1_w_in"), "bcast_cols"), (shard16("ffn1_w_out"), "bcast")], "gather_ffn1")
    mix_state, tok_m = gather_begin([(jnp.transpose(p["w_in_mix"][0]).astype(BF16), "bcast"), (shard16("w_out_mix"), "bcast"),
                                     (p["w_pool"][0].reshape(H * 32, PG), "bcast"), (p["w_alpha"][0], "bcast")], "gather_mix",
                                    (g_w1out,))
    ffn2_state, tok_f = gather_begin([(shard16("ffn2_w_in"), "bcast_cols"), (shard16("ffn2_w_out"), "bcast")], "gather_ffn2",
                                     (tok_m,))
    full = {"ffn1_w_in": g_w1in, "ffn1_w_out": g_w1out.reshape(FF, D), "final_norm": final_norm.reshape(1, D)}
    for n in ("ffn1_norm", "mix_norm", "ffn2_norm", "pool_scale", "b_alpha", "gla_norm"):
        full[n] = p[n]

    xs, tgt = x[0], loss_target[0]
    h1, n1, gu1 = _fwd_ffn1(xs, full, after=(tok_m, tok_f))
    g_wmix, g_wo, g_wpool, g_walpha = gather_end(mix_state, (h1,))
    walpha = jnp.transpose(g_walpha, (1, 0, 2)).reshape(RANK, H * DK)
    full.update({
        "w_in_mix": jnp.pad(g_wmix.reshape(D_IN, D), ((0, D_IN_PAD - D_IN), (0, 0))),
        "w_out_mix": g_wo.reshape(D, D),
        "w_pool": jnp.transpose(g_wpool.reshape(NDEV, H, 32, PG), (1, 0, 2, 3)).reshape(H, PG, PG).astype(BF16),
        "w_alpha": jnp.pad(walpha, ((0, RANK_PAD - RANK), (0, 0))).astype(BF16),
    })
    h2, sv = _fwd_mixer(h1, full)
    g_w2in, g_w2out = gather_end(ffn2_state, (h2,))
    full.update({"ffn2_w_in": g_w2in, "ffn2_w_out": g_w2out.reshape(FF, D)})
    dh3, d_final, loss_part, n3, gu3 = _fwd_ffn2_loss(h2, tgt, full)


    def slab_shape(a, kind):
        return (NCHIP,) + (a.shape[1:] if kind == "scatter" else (a.shape[0], a.shape[1] // NDEV))

    def pair_add_all(own, got, tag):
        return [_pair_add(a, kind, got_a, table, tr, "%s_pair_add_%d" % (tag, i))
                for i, ((a, kind, tr), got_a) in enumerate(zip(own, got))]

    def reduce_begin(own, tag, after=()):
        kinds = [kind for _, kind, _ in own]
        copies = _pair_copies(kinds)
        s, r, thru, tok = _start_copies(tag + "_pair_start", [a for a, _, _ in own],
                                        [lax.empty(slab_shape(a, kind), a.dtype) for a, kind, _ in own], copies,
                                        NCHIP * len(own), after)
        return dict(own=own, copies=copies, s=s, r=r, thru=thru, tag=tag), tok

    def reduce_middle(st, after):
        own, tag = st["own"], st["tag"]
        sent, got = _wait_copies(tag + "_pair_wait", st["s"], st["r"], st["thru"], len(own), st["copies"], after)
        pre = pair_add_all([(a, kind, tr) for a, (_, kind, tr) in zip(sent, own)], got, tag)
        st["s"], st["r"], st["thru"], tok = _start_copies(tag + "_chip_start", pre, [lax.empty(a.shape, a.dtype) for a in pre],
                                                          _chip_copies, (NCHIP - 1) * len(pre))
        return tok

    def reduce_end(st, after):
        n = len(st["own"])
        pre, land = _wait_copies(st["tag"] + "_chip_wait", st["s"], st["r"], st["thru"], n, _chip_copies, after)
        return [[(a, 0)] + [(b, rel) for rel in range(1, NCHIP)] for a, b in zip(pre, land)]

    dh2, dh2b, g2 = _bwd_ffn2(dh3, h2, n3, gu3, full)
    red2, tok = reduce_begin([(g2["ffn2_w_in"], "scatter_cols", 256),
                              (g2["ffn2_w_out"].reshape(NDEV, WOUT_SHARD, D), "scatter", WOUT_SHARD // 2)], "ffn2")
    dh1, gm = _bwd_mixer(dh2, dh2b, h1, sv, full, after=(tok,))
    tok2 = reduce_middle(red2, (dh1,))
    d_wmix8 = gm["w_in_mix"][:D_IN].reshape(NDEV, MIX_SHARD, D)
    d_wpool8 = jnp.transpose(gm["w_pool"].reshape(H, NDEV, 32, PG), (1, 0, 2, 3)).reshape(NDEV, H * 32, PG)
    d_walpha8 = jnp.transpose(gm["w_alpha"][:RANK].reshape(RANK, NDEV, H * DK // NDEV), (1, 0, 2))
    redm, tokm = reduce_begin([(d_wmix8, "scatter", MIX_SHARD), (gm["w_out_mix"].reshape(NDEV, D // NDEV, D), "scatter", D // NDEV),
                               (d_wpool8, "scatter", H * 32), (d_walpha8, "scatter", RANK)], "mix", (tok2,))
    dx, g1 = _bwd_ffn1(dh1, xs, n1, gu1, full, after=(tok2, tokm), between=lambda dx_: (reduce_middle(redm, (dx_,)),))
    red1, tok1 = reduce_begin([(g1["ffn1_w_in"], "scatter_cols", 256),
                               (g1["ffn1_w_out"].reshape(NDEV, WOUT_SHARD, D), "scatter", WOUT_SHARD // 2)], "ffn1")
    g = dict(final_norm=d_final, **g1, **gm, **g2)
    (r_small,) = _exchange([(_pack_small(g, loss_part[0, 0]), "bcast")], "gather_small")

    def upd(parts, n, shape2d, tr):
        res = _adamw(parts, p[n].reshape(shape2d), m[n].reshape(shape2d), v[n].reshape(shape2d), tr, "adamw_" + n)
        return [r.reshape(p[n].shape) for r in res]

    def transposed(a):
        return jnp.transpose(a[0])

    tok1 = reduce_middle(red1, (r_small,))
    p_w2in, p_w2out = reduce_end(red2, (tok1,))
    p_wmix, p_wo, p_wpool, p_walpha = reduce_end(redm, (tok1,))
    out = {
        "ffn2_w_in": upd(p_w2in, "ffn2_w_in", (D, WIN_SHARD), 128),
        "ffn2_w_out": upd(p_w2out, "ffn2_w_out", (WOUT_SHARD, D), 64),
        "w_out_mix": upd(p_wo, "w_out_mix", (D // NDEV, D), 64),
        "w_pool": upd(p_wpool, "w_pool", (H * 32, PG), H * 32),
        "w_alpha": upd(p_walpha, "w_alpha", (RANK, H * DK // NDEV), RANK),
    }
    out["w_in_mix"] = [jnp.transpose(r)[None] for r in
                       _adamw(p_wmix, transposed(p["w_in_mix"]), transposed(m["w_in_mix"]), transposed(v["w_in_mix"]),
                              MIX_SHARD, "adamw_w_in_mix", tc=512)]
    small_res = _adamw([(r_small, s) for s in range(NDEV)], _pack_small(p), _pack_small(m), _pack_small(v), SMALL_ROWS,
                       "adamw_small")
    p_w1in, p_w1out = reduce_end(red1, (small_res[0], out["w_in_mix"][3], out["ffn2_w_in"][3]))
    out["ffn1_w_in"] = upd(p_w1in, "ffn1_w_in", (D, WIN_SHARD), 128)
    out["ffn1_w_out"] = upd(p_w1out, "ffn1_w_out", (WOUT_SHARD, D), 64)
    unpacked = [_unpack_small(r, p) for r in small_res]
    loss = unpacked[0][1]
    for n, _ in SMALL:
        out[n] = [u[0][n] for u in unpacked]

    return (loss, dx.reshape(1, S, D), *[out[n][0] for n in names], *[out[n][1] for n in names],
            *[out[n][2] for n in names], *[out[n][3] for n in names])
```

```python
import functools

import jax
import jax.numpy as jnp
from jax import lax
from jax.experimental import pallas as pl
from jax.experimental.pallas import tpu as pltpu

F32, BF16 = jnp.float32, jnp.bfloat16
MESH = pl.DeviceIdType.MESH
ANY = pl.BlockSpec(memory_space=pl.ANY)

NDEV = 8
S = 2048
D = 2048
FF = 5632
WIN_SHARD = 2 * FF // NDEV
WOUT_SHARD = FF // NDEV
D_POOL = 1024
PG = 256
POOL_WINDOWS = (2, 4, 8, 16)
H = 4
DK = 128
DV = 256
CH = 64
NCH = S // CH
RANK = 16
RANK_PAD = 128
D_IN = 4112
D_IN_PAD = 4224
MIX_SHARD = D_IN // NDEV
O_Q, O_K, O_V, O_G, O_R = 1024, 1536, 2048, 3072, 4096
GATE_NORM = 16.0
QK_SCALE = DK ** -0.5
EPS = 1e-6
ADAM_LR, ADAM_B1, ADAM_B2, ADAM_EPS, ADAM_WD, ADAM_STEP = 0.001, 0.9, 0.999, 1e-08, 0.01, 10
V7X_VMEM_BYTES = 64 << 20

SMALL = (("ffn1_norm", 2048), ("mix_norm", 2048), ("ffn2_norm", 2048), ("final_norm", 2048),
         ("pool_scale", 1024), ("b_alpha", 512), ("gla_norm", 256))
SMALL_ROWS = 80
LOSS_AT = sum(n for _, n in SMALL)


def _params(vmem_mb, sem=None):
    return pltpu.CompilerParams(dimension_semantics=sem, vmem_limit_bytes=min(vmem_mb << 20, V7X_VMEM_BYTES - (4 << 20)))


def _dot(a, b):
    return jnp.dot(a, b, preferred_element_type=F32)


def _dot_nt(a, b):
    return lax.dot_general(a, b, (((1,), (1,)), ((), ())), preferred_element_type=F32)


def _dot_tn(a, b):
    return lax.dot_general(a, b, (((0,), (0,)), ((), ())), preferred_element_type=F32)


def _sigmoid(x):
    return 1.0 / (1.0 + jnp.exp(-x))


def _log_sigmoid(x):
    return jnp.minimum(x, 0.0) - jnp.log(1.0 + jnp.exp(-jnp.abs(x)))


def _rms(x, g):
    r = lax.rsqrt(jnp.mean(x * x, axis=-1, keepdims=True) + EPS)
    return x * r * g


def _rms_bwd(dn, x, g):
    r = lax.rsqrt(jnp.mean(x * x, axis=-1, keepdims=True) + EPS)
    xh = x * r
    dxh = dn * g
    dx = r * (dxh - xh * jnp.mean(dxh * xh, axis=-1, keepdims=True))
    return dx, jnp.sum(dn * xh, axis=0, keepdims=True)


ROWS = 64


def _row_loop(total, fn, init=0):
    def step(t, carry):
        return fn(pl.ds(pl.multiple_of(t * ROWS, ROWS), ROWS), carry)
    return lax.fori_loop(0, total // ROWS, step, init)


def _split3(x):
    hi = x.astype(BF16)
    r1 = x - hi.astype(F32)
    mid = r1.astype(BF16)
    lo = (r1 - mid.astype(F32)).astype(BF16)
    return hi, mid, lo


def _tri_dot(tri_b, x):
    hi, mid, lo = _split3(x)
    return (_dot(tri_b, lo) + _dot(tri_b, mid)) + _dot(tri_b, hi)


FFN_TS, FFN_TF = 512, 512


def _ffn_specs():
    wg = pl.BlockSpec((D, FFN_TF), lambda i, j: (0, j))
    wu = pl.BlockSpec((D, FFN_TF), lambda i, j: (0, FF // FFN_TF + j))
    wo = pl.BlockSpec((FFN_TF, D), lambda i, j: (j, 0))
    row = pl.BlockSpec((FFN_TS, D), lambda i, j: (i, 0))
    vec = pl.BlockSpec((1, D), lambda i, j: (0, 0))
    gu = pl.BlockSpec((2, FFN_TS, FFN_TF), lambda i, j: (0, i, j))
    return wg, wu, wo, row, vec, gu


def _ordered_after(body, n_in, after):
    def wrapped(*refs):
        return body(*refs[:n_in], *refs[n_in + len(after):])
    return wrapped, [ANY] * len(after)


def _ffn_fwd(h, g, w_in8, w_out, name, after=()):
    nj = FF // FFN_TF
    wg, wu, wo, row, vec, gu = _ffn_specs()

    def body(h_ref, g_ref, wg_ref, wu_ref, wo_ref, ho_ref, n_ref, gu_ref, acc_ref):
        j = pl.program_id(1)

        @pl.when(j == 0)
        def _():
            def norm(rows, c):
                n_ref[rows, :] = _rms(h_ref[rows, :], g_ref[...]).astype(BF16)
                return c
            _row_loop(FFN_TS, norm)
            acc_ref[...] = jnp.zeros_like(acc_ref)

        n = n_ref[...]
        gate = _dot(n, wg_ref[...])
        up = _dot(n, wu_ref[...])
        gu_ref[0] = gate.astype(BF16)
        gu_ref[1] = up.astype(BF16)
        a = (gate * _sigmoid(gate)) * up
        acc_ref[...] += _dot(a.astype(BF16), wo_ref[...])

        @pl.when(j == nj - 1)
        def _():
            def residual(rows, c):
                ho_ref[rows, :] = h_ref[rows, :] + 0.5 * acc_ref[rows, :]
                return c
            _row_loop(FFN_TS, residual)

    body, extra = _ordered_after(body, 5, after)
    return pl.pallas_call(
        body, grid=(S // FFN_TS, nj), name=name,
        in_specs=[row, vec, wg, wu, wo] + extra, out_specs=[row, row, gu],
        out_shape=[jax.ShapeDtypeStruct((S, D), F32), jax.ShapeDtypeStruct((S, D), BF16),
                   jax.ShapeDtypeStruct((2, S, FF), BF16)],
        scratch_shapes=[pltpu.VMEM((FFN_TS, D), F32)],
        compiler_params=_params(56, ("arbitrary", "arbitrary")),
    )(h, g, w_in8, w_in8, w_out, *after)


def _ffn_bwd_x(dhp, h, g, gu_arr, w_in8, w_out, name, after=()):
    ni, nj = S // FFN_TS, FF // FFN_TF
    wg, wu, wo, row, vec, gu = _ffn_specs()
    act = pl.BlockSpec((FFN_TS, FFN_TF), lambda i, j: (i, j))

    def body(dhp_ref, h_ref, g_ref, gu_ref, wg_ref, wu_ref, wo_ref,
             dgu_ref, a_ref, df_ref, dh_ref, dhb_ref, dg_ref, acc_ref):
        i, j = pl.program_id(0), pl.program_id(1)

        @pl.when(j == 0)
        def _():
            def half(rows, c):
                df_ref[rows, :] = (0.5 * dhp_ref[rows, :]).astype(BF16)
                return c
            _row_loop(FFN_TS, half)
            acc_ref[...] = jnp.zeros_like(acc_ref)

        gate = gu_ref[0].astype(F32)
        up = gu_ref[1].astype(F32)
        da = _dot_nt(df_ref[...], wo_ref[...])
        sg = _sigmoid(gate)
        silu = gate * sg
        dgate = (da * up * (sg * (1.0 + gate * (1.0 - sg)))).astype(BF16)
        dup = (da * silu).astype(BF16)
        a_ref[...] = (silu * up).astype(BF16)
        dgu_ref[0] = dgate
        dgu_ref[1] = dup
        acc_ref[...] += _dot_nt(dgate, wg_ref[...]) + _dot_nt(dup, wu_ref[...])

        @pl.when(j == nj - 1)
        def _():
            def norm_bwd(rows, dg):
                dx, dg_rows = _rms_bwd(acc_ref[rows, :], h_ref[rows, :], g_ref[...])
                dh = dhp_ref[rows, :] + dx
                dh_ref[rows, :] = dh
                dhb_ref[rows, :] = dh.astype(BF16)
                return dg + dg_rows
            dg = _row_loop(FFN_TS, norm_bwd, jnp.zeros((1, D), F32))

            @pl.when(i == 0)
            def _():
                dg_ref[...] = dg

            @pl.when(i > 0)
            def _():
                dg_ref[...] += dg

    body, extra = _ordered_after(body, 7, after)
    return pl.pallas_call(
        body, grid=(ni, nj), name=name,
        in_specs=[row, row, vec, gu, wg, wu, wo] + extra,
        out_specs=[gu, act, row, row, row, vec],
        out_shape=[jax.ShapeDtypeStruct((2, S, FF), BF16), jax.ShapeDtypeStruct((S, FF), BF16),
                   jax.ShapeDtypeStruct((S, D), BF16), jax.ShapeDtypeStruct((S, D), F32),
                   jax.ShapeDtypeStruct((S, D), BF16), jax.ShapeDtypeStruct((1, D), F32)],
        scratch_shapes=[pltpu.VMEM((FFN_TS, D), F32)],
        compiler_params=_params(58, ("arbitrary", "arbitrary")),
    )(dhp, h, g, gu_arr, w_in8, w_in8, w_out, *after)


def _tn_matmul(a, b, a_spec, b_spec, out_shape, out_spec, grid, name, vmem_mb, after=()):
    def body(a_ref, b_ref, o_ref):
        o_ref[...] = _dot_tn(a_ref[...], b_ref[...]).astype(o_ref.dtype)

    body, extra = _ordered_after(body, 2, after)
    return pl.pallas_call(body, grid=grid, name=name, in_specs=[a_spec, b_spec] + extra, out_specs=out_spec,
                          out_shape=out_shape, compiler_params=_params(vmem_mb))(a, b, *after)


def _norm_matmul(h, g, w, name, after=(), ts=512, tn=1408):
    n_out = w.shape[0]

    def body(h_ref, g_ref, w_ref, u_ref, n_ref):
        @pl.when(pl.program_id(1) == 0)
        def _():
            def norm(rows, c):
                n_ref[rows, :] = _rms(h_ref[rows, :], g_ref[...]).astype(BF16)
                return c
            _row_loop(ts, norm)

        u_ref[...] = _dot_nt(n_ref[...], w_ref[...])

    body, extra = _ordered_after(body, 3, after)
    return pl.pallas_call(
        body, grid=(S // ts, n_out // tn), name=name,
        in_specs=[pl.BlockSpec((ts, D), lambda i, j: (i, 0)), pl.BlockSpec((1, D), lambda i, j: (0, 0)),
                  pl.BlockSpec((tn, D), lambda i, j: (j, 0))] + extra,
        out_specs=[pl.BlockSpec((ts, tn), lambda i, j: (i, j)), pl.BlockSpec((ts, D), lambda i, j: (i, 0))],
        out_shape=[jax.ShapeDtypeStruct((S, n_out), F32), jax.ShapeDtypeStruct((S, D), BF16)],
        compiler_params=_params(48, ("arbitrary", "arbitrary")),
    )(h, g, w, *after)


def _matmul_residual(a, w, res, name, ts=512, tn=1024):
    k, n_out = w.shape

    def body(a_ref, w_ref, r_ref, o_ref):
        o_ref[...] = r_ref[...] + _dot(a_ref[...], w_ref[...])

    return pl.pallas_call(
        body, grid=(S // ts, n_out // tn), name=name,
        in_specs=[pl.BlockSpec((ts, k), lambda i, j: (i, 0)), pl.BlockSpec((k, tn), lambda i, j: (0, j)),
                  pl.BlockSpec((ts, tn), lambda i, j: (i, j))],
        out_specs=pl.BlockSpec((ts, tn), lambda i, j: (i, j)),
        out_shape=jax.ShapeDtypeStruct((S, n_out), F32),
        compiler_params=_params(40),
    )(a, w, res)


def _nt_matmul(a, w, name, after=(), ts=512, tn=1024):
    n_out, k = w.shape

    def body(a_ref, w_ref, o_ref):
        o_ref[...] = _dot_nt(a_ref[...], w_ref[...])

    body, extra = _ordered_after(body, 2, after)
    return pl.pallas_call(
        body, grid=(S // ts, n_out // tn), name=name,
        in_specs=[pl.BlockSpec((ts, k), lambda i, j: (i, 0)), pl.BlockSpec((tn, k), lambda i, j: (j, 0))] + extra,
        out_specs=pl.BlockSpec((ts, tn), lambda i, j: (i, j)),
        out_shape=jax.ShapeDtypeStruct((S, n_out), F32),
        compiler_params=_params(40),
    )(a, w, *after)


def _matmul_normbwd(du, w, h, g, dres, name, ts=512, tn=1408):
    n_in = w.shape[0]
    nj = n_in // tn
    row = pl.BlockSpec((ts, D), lambda i, j: (i, 0))
    vec = pl.BlockSpec((1, D), lambda i, j: (0, 0))

    def body(du_ref, w_ref, h_ref, g_ref, dres_ref, dh_ref, dg_ref, acc_ref):
        i, j = pl.program_id(0), pl.program_id(1)

        @pl.when(j == 0)
        def _():
            acc_ref[...] = jnp.zeros_like(acc_ref)

        acc_ref[...] += _dot(du_ref[...], w_ref[...])

        @pl.when(j == nj - 1)
        def _():
            def norm_bwd(rows, dg):
                dx, dg_rows = _rms_bwd(acc_ref[rows, :], h_ref[rows, :], g_ref[...])
                dh_ref[rows, :] = dres_ref[rows, :] + dx
                return dg + dg_rows
            dg = _row_loop(ts, norm_bwd, jnp.zeros((1, D), F32))

            @pl.when(i == 0)
            def _():
                dg_ref[...] = dg

            @pl.when(i > 0)
            def _():
                dg_ref[...] += dg

    return pl.pallas_call(
        body, grid=(S // ts, nj), name=name,
        in_specs=[pl.BlockSpec((ts, tn), lambda i, j: (i, j)), pl.BlockSpec((tn, D), lambda i, j: (j, 0)), row, vec, row],
        out_specs=[row, vec],
        out_shape=[jax.ShapeDtypeStruct((S, D), F32), jax.ShapeDtypeStruct((1, D), F32)],
        scratch_shapes=[pltpu.VMEM((ts, D), F32)],
        compiler_params=_params(52, ("arbitrary", "arbitrary")),
    )(du, w, h, g, dres)


def _loss_head(h, g, tgt, ts=256):
    row = pl.BlockSpec((ts, D), lambda i: (i, 0))
    vec = pl.BlockSpec((1, D), lambda i: (0, 0))

    def body(h_ref, g_ref, t_ref, dh_ref, dg_ref, loss_ref):
        i = pl.program_id(0)

        def rows_fn(rows, carry):
            dg, part = carry
            x = h_ref[rows, :]
            gv = g_ref[...]
            err = _rms(x, gv) - t_ref[rows, :]
            part = part + 0.5 * jnp.sum(jnp.mean(err * err, axis=-1, keepdims=True), axis=0, keepdims=True)
            dx, dg_rows = _rms_bwd(err * (1.0 / D), x, gv)
            dh_ref[rows, :] = dx
            return dg + dg_rows, part
        dg, part = _row_loop(ts, rows_fn, (jnp.zeros((1, D), F32), jnp.zeros((1, 1), F32)))

        @pl.when(i == 0)
        def _():
            dg_ref[...] = dg
            loss_ref[...] = jnp.broadcast_to(part, loss_ref.shape)

        @pl.when(i > 0)
        def _():
            dg_ref[...] += dg
            loss_ref[...] += jnp.broadcast_to(part, loss_ref.shape)

    return pl.pallas_call(
        body, grid=(S // ts,), name="loss_head",
        in_specs=[row, vec, row], out_specs=[row, vec, pl.BlockSpec((1, 128), lambda i: (0, 0))],
        out_shape=[jax.ShapeDtypeStruct((S, D), F32), jax.ShapeDtypeStruct((1, D), F32),
                   jax.ShapeDtypeStruct((1, 128), F32)],
        compiler_params=_params(40, ("arbitrary",)),
    )(h, g, tgt)


def _pool_specs():
    blk = pl.BlockSpec((S, PG), lambda gi: (0, gi))
    wp = pl.BlockSpec((None, PG, PG), lambda gi: (gi, 0, 0))
    sc = pl.BlockSpec((1, PG), lambda gi: (0, gi))
    return blk, wp, sc


def _pool_fwd(u, wp_b, scale):
    blk, wp, sc = _pool_specs()

    def body(u_ref, wp_ref, sc_ref, y_ref, pooled_ref):
        win = 2 << pl.program_id(0)
        row = lax.broadcasted_iota(jnp.int32, (S, PG), 0)
        x = u_ref[...]
        s = x
        for k in (1, 2, 4, 8):
            s = s + jnp.where((row >= k) & (k < win), pltpu.roll(s, k, 0), 0.0)
        cnt = jnp.minimum(row + 1, win).astype(F32)
        pooled = (s / cnt - x).astype(BF16)
        pooled_ref[...] = pooled
        y_ref[...] = (_dot(pooled, wp_ref[...]) * sc_ref[...]).astype(BF16)

    return pl.pallas_call(
        body, grid=(len(POOL_WINDOWS),), name="pool_fwd", in_specs=[blk, wp, sc], out_specs=[blk, blk],
        out_shape=[jax.ShapeDtypeStruct((S, D_POOL), BF16), jax.ShapeDtypeStruct((S, D_POOL), BF16)],
        compiler_params=_params(40),
    )(u, wp_b, scale)


def _pool_bwd(dy, pooled, wp_b, scale):
    blk, wp, sc = _pool_specs()

    def body(dy_ref, p_ref, wp_ref, sc_ref, du_ref, dwp_ref, dsc_ref):
        win = 2 << pl.program_id(0)
        row = lax.broadcasted_iota(jnp.int32, (S, PG), 0)
        dyv = dy_ref[...]
        pooled = p_ref[...]
        w = wp_ref[...]
        dsc_ref[...] = jnp.sum(dyv * _dot(pooled, w), axis=0, keepdims=True)
        dz = (dyv * sc_ref[...]).astype(BF16)
        dwp_ref[...] = _dot_tn(pooled, dz)
        dpooled = _dot_nt(dz, w)
        cnt = jnp.minimum(row + 1, win).astype(F32)
        fs = dpooled / cnt
        for k in (1, 2, 4, 8):
            fs = fs + jnp.where((row < S - k) & (k < win), pltpu.roll(fs, S - k, 0), 0.0)
        du_ref[...] = (fs - dpooled).astype(BF16)

    return pl.pallas_call(
        body, grid=(len(POOL_WINDOWS),), name="pool_bwd", in_specs=[blk, blk, wp, sc], out_specs=[blk, wp, sc],
        out_shape=[jax.ShapeDtypeStruct((S, D_POOL), BF16), jax.ShapeDtypeStruct((len(POOL_WINDOWS), PG, PG), F32),
                   jax.ShapeDtypeStruct((1, D_POOL), F32)],
        compiler_params=_params(40),
    )(dy, pooled, wp_b, scale)


def _gla_in_specs(chunk_of):
    def at(width, col):
        return pl.BlockSpec((CH, width), lambda n: (chunk_of(n), col))
    return [at(H * DK, O_Q // (H * DK)), at(H * DK, O_K // (H * DK)), at(H * DV, O_V // (H * DV)),
            at(H * DV, O_G // (H * DV)), at(RANK_PAD, O_R // RANK_PAD)]


def _gla_decay_terms(lr_ref, wa_ref, ba_ref, q_ref, k_ref):
    row = lax.broadcasted_iota(jnp.int32, (CH, CH), 0)
    col = lax.broadcasted_iota(jnp.int32, (CH, CH), 1)
    tril = row >= col
    z = _dot(lr_ref[...].astype(BF16), wa_ref[...]) + ba_ref[...]
    la = _log_sigmoid(z) / GATE_NORM
    b = _tri_dot(jnp.where(tril, 1.0, 0.0).astype(BF16), la)
    bl = jnp.sum(la, axis=0, keepdims=True)
    e_b, e_nb, e_tb = jnp.exp(b), jnp.exp(-b), jnp.exp(bl - b)
    kk = k_ref[...]
    q_dec = (q_ref[...] * QK_SCALE) * e_b
    return tril, z, e_b, e_nb, e_tb, jnp.exp(bl), q_dec, kk * e_nb, kk * e_tb


def _gla_fwd(u, wa_b, ba, gn):
    wide = pl.BlockSpec((CH, H * DV), lambda n: (n, 0))

    def body(q_ref, k_ref, v_ref, g_ref, lr_ref, wa_ref, ba_ref, gn_ref, y_ref, o_ref, st_ref, state):
        @pl.when(pl.program_id(0) == 0)
        def _():
            state[...] = jnp.zeros_like(state)

        tril, _, _, _, _, dec, q_dec, k_inv, k_tail = _gla_decay_terms(lr_ref, wa_ref, ba_ref, q_ref, k_ref)
        for hd in range(H):
            ks, vs = slice(hd * DK, (hd + 1) * DK), slice(hd * DV, (hd + 1) * DV)
            qb, kib, ktb = q_dec[:, ks].astype(BF16), k_inv[:, ks].astype(BF16), k_tail[:, ks].astype(BF16)
            vb = v_ref[:, vs].astype(BF16)
            p = jnp.where(tril, _dot_nt(qb, kib), 0.0)
            st = state[hd]
            o = _dot(p.astype(BF16), vb) + _dot_nt(qb, st.astype(BF16))
            st_ref[hd] = st
            state[hd] = st * dec[:, ks] + _dot_tn(vb, ktb)
            o_ref[:, vs] = o
            on = _rms(o, gn_ref[...])
            gg = g_ref[:, vs]
            y_ref[:, vs] = (on * (gg * _sigmoid(gg))).astype(BF16)

    return pl.pallas_call(
        body, grid=(NCH,), name="gla_fwd",
        in_specs=_gla_in_specs(lambda n: n) + [pl.BlockSpec((RANK_PAD, H * DK), lambda n: (0, 0)),
                                               pl.BlockSpec((1, H * DK), lambda n: (0, 0)),
                                               pl.BlockSpec((1, DV), lambda n: (0, 0))],
        out_specs=[wide, wide, pl.BlockSpec((None, H, DV, DK), lambda n: (n, 0, 0, 0))],
        out_shape=[jax.ShapeDtypeStruct((S, H * DV), BF16), jax.ShapeDtypeStruct((S, H * DV), F32),
                   jax.ShapeDtypeStruct((NCH, H, DV, DK), F32)],
        scratch_shapes=[pltpu.VMEM((H, DV, DK), F32)],
        compiler_params=_params(32, ("arbitrary",)),
    )(u, u, u, u, u, wa_b, ba, gn)


GLA_DU = 2 * H * DK + 2 * H * DV + RANK_PAD


def _gla_bwd(u, o_arr, states, dy, wa_b, ba, gn):
    rev = lambda n: NCH - 1 - n
    wide = pl.BlockSpec((CH, H * DV), lambda n: (rev(n), 0))

    def body(q_ref, k_ref, v_ref, g_ref, lr_ref, o_ref, st_ref, dy_ref, wa_ref, ba_ref, gn_ref,
             du_ref, dwa_ref, dba_ref, dgn_ref, gstate, db_scr, dbl_scr):
        @pl.when(pl.program_id(0) == 0)
        def _():
            gstate[...] = jnp.zeros_like(gstate)
            dwa_ref[...] = jnp.zeros_like(dwa_ref)
            dba_ref[...] = jnp.zeros_like(dba_ref)
            dgn_ref[...] = jnp.zeros_like(dgn_ref)

        tril, z, e_b, e_nb, e_tb, dec, q_dec, k_inv, k_tail = _gla_decay_terms(lr_ref, wa_ref, ba_ref, q_ref, k_ref)
        gnv = gn_ref[...]
        dgn = jnp.zeros((1, DV), F32)
        for hd in range(H):
            ks, vs = slice(hd * DK, (hd + 1) * DK), slice(hd * DV, (hd + 1) * DV)
            qh, kih, kth = q_dec[:, ks], k_inv[:, ks], k_tail[:, ks]
            qb, kib, ktb = qh.astype(BF16), kih.astype(BF16), kth.astype(BF16)
            vb = v_ref[:, vs].astype(BF16)
            o = o_ref[:, vs]
            gg = g_ref[:, vs]
            dyh = dy_ref[:, vs]
            r = lax.rsqrt(jnp.mean(o * o, axis=-1, keepdims=True) + EPS)
            xh = o * r
            sg = _sigmoid(gg)
            dgate = dyh * (xh * gnv) * (sg * (1.0 + gg * (1.0 - sg)))
            don = dyh * (gg * sg)
            dgn = dgn + jnp.sum(don * xh, axis=0, keepdims=True)
            dxh = don * gnv
            d_o = (r * (dxh - xh * jnp.mean(dxh * xh, axis=-1, keepdims=True))).astype(BF16)
            pb = jnp.where(tril, _dot_nt(qb, kib), 0.0).astype(BF16)
            dpb = jnp.where(tril, _dot_nt(d_o, vb), 0.0).astype(BF16)
            gt = gstate[hd]
            gtb = gt.astype(BF16)
            st = st_ref[hd]
            dv = _dot_tn(pb, d_o) + _dot_nt(ktb, gtb)
            dq_dec = _dot(dpb, kib) + _dot(d_o, st.astype(BF16))
            dk_inv = _dot_tn(dpb, qb)
            dk_tail = _dot(vb, gtb)
            ddec = jnp.sum(gt * st, axis=0, keepdims=True)
            gstate[hd] = _dot_tn(d_o, qb) + dec[:, ks] * gt
            du_ref[:, ks] = (dq_dec * QK_SCALE * e_b[:, ks]).astype(BF16)
            du_ref[:, H * DK + hd * DK:H * DK + (hd + 1) * DK] = (dk_inv * e_nb[:, ks] + dk_tail * e_tb[:, ks]).astype(BF16)
            du_ref[:, 2 * H * DK + hd * DV:2 * H * DK + (hd + 1) * DV] = dv.astype(BF16)
            du_ref[:, 2 * H * DK + H * DV + hd * DV:2 * H * DK + H * DV + (hd + 1) * DV] = dgate.astype(BF16)
            db_scr[:, ks] = dq_dec * qh - dk_inv * kih - dk_tail * kth
            dbl_scr[:, ks] = jnp.sum(dk_tail * kth, axis=0, keepdims=True) + ddec * dec[:, ks]
        dgn_ref[...] += dgn
        row = lax.broadcasted_iota(jnp.int32, (CH, CH), 0)
        col = lax.broadcasted_iota(jnp.int32, (CH, CH), 1)
        dla = _tri_dot(jnp.where(row <= col, 1.0, 0.0).astype(BF16), db_scr[...]) + dbl_scr[...]
        dz = dla * (1.0 / GATE_NORM) * _sigmoid(-z)
        dzb = dz.astype(BF16)
        du_ref[:, GLA_DU - RANK_PAD:] = _dot_nt(dzb, wa_ref[...]).astype(BF16)
        dwa_ref[...] += _dot_tn(lr_ref[...].astype(BF16), dzb)
        dba_ref[...] += jnp.sum(dz, axis=0, keepdims=True)

    full = lambda shape: pl.BlockSpec(shape, lambda n: (0,) * len(shape))
    return pl.pallas_call(
        body, grid=(NCH,), name="gla_bwd",
        in_specs=_gla_in_specs(rev) + [wide, pl.BlockSpec((None, H, DV, DK), lambda n: (rev(n), 0, 0, 0)),
                                       pl.BlockSpec((CH, H * DV), lambda n: (rev(n), 1)),
                                       full((RANK_PAD, H * DK)), full((1, H * DK)), full((1, DV))],
        out_specs=[pl.BlockSpec((CH, GLA_DU), lambda n: (rev(n), 0)), full((RANK_PAD, H * DK)), full((1, H * DK)),
                   full((1, DV))],
        out_shape=[jax.ShapeDtypeStruct((S, GLA_DU), BF16), jax.ShapeDtypeStruct((RANK_PAD, H * DK), F32),
                   jax.ShapeDtypeStruct((1, H * DK), F32), jax.ShapeDtypeStruct((1, DV), F32)],
        scratch_shapes=[pltpu.VMEM((H, DV, DK), F32), pltpu.VMEM((CH, H * DK), F32), pltpu.VMEM((1, H * DK), F32)],
        compiler_params=_params(32, ("arbitrary",)),
    )(u, u, u, u, u, o_arr, states, dy, wa_b, ba, gn)


def _ffn_weight_grads(n, dgu, act, df, tag, after=()):
    d_in = _tn_matmul(n, dgu, pl.BlockSpec((S, 512), lambda s, m: (0, m)),
                      pl.BlockSpec((None, S, WIN_SHARD), lambda s, m: (s // (NDEV // 2), 0, s % (NDEV // 2))),
                      jax.ShapeDtypeStruct((D, 2 * FF), BF16), pl.BlockSpec((512, WIN_SHARD), lambda s, m: (m, s)),
                      (NDEV, D // 512), tag + "_dw_in", 32, after)
    d_out = _tn_matmul(act, df, pl.BlockSpec((S, 512), lambda m: (0, m)), pl.BlockSpec((S, D), lambda m: (0, 0)),
                       jax.ShapeDtypeStruct((FF, D), BF16), pl.BlockSpec((512, D), lambda m: (m, 0)),
                       (FF // 512,), tag + "_dw_out", 40)
    return d_in, d_out


def _fwd_ffn1(x, w, after=()):
    return _ffn_fwd(x, w["ffn1_norm"], w["ffn1_w_in"], w["ffn1_w_out"], "ffn1_fwd", after)


def _fwd_mixer(h1, w, after=()):
    u, n2 = _norm_matmul(h1, w["mix_norm"], w["w_in_mix"], "mix_in", after)
    y_pool, pooled = _pool_fwd(u, w["w_pool"], w["pool_scale"])
    y_gla, o_gla, states = _gla_fwd(u, w["w_alpha"], w["b_alpha"], w["gla_norm"])
    y = jnp.concatenate([y_pool, y_gla], axis=1)
    h2 = _matmul_residual(y, w["w_out_mix"], h1, "mix_out")
    return h2, dict(u=u, n2=n2, pooled=pooled, o_gla=o_gla, states=states, y=y)


def _fwd_ffn2_loss(h2, tgt, w, after=()):
    h3, n3, gu3 = _ffn_fwd(h2, w["ffn2_norm"], w["ffn2_w_in"], w["ffn2_w_out"], "ffn2_fwd", after)
    dh3, d_final, loss = _loss_head(h3, w["final_norm"], tgt)
    return dh3, d_final, loss, n3, gu3


def _bwd_ffn2(dh3, h2, n3, gu3, w):
    dgu3, act3, df3, dh2, dh2b, d_norm = _ffn_bwd_x(dh3, h2, w["ffn2_norm"], gu3, w["ffn2_w_in"], w["ffn2_w_out"], "ffn2_bwd")
    d_in, d_out = _ffn_weight_grads(n3, dgu3, act3, df3, "ffn2")
    return dh2, dh2b, dict(ffn2_norm=d_norm, ffn2_w_in=d_in, ffn2_w_out=d_out)


def _bwd_mixer(dh2, dh2b, h1, sv, w, after=()):
    dy = _nt_matmul(dh2b, w["w_out_mix"], "mix_out_bwd", after)
    d_wo = _tn_matmul(sv["y"], dh2b, pl.BlockSpec((S, 512), lambda m: (0, m)), pl.BlockSpec((S, D), lambda m: (0, 0)),
                      jax.ShapeDtypeStruct((D, D), BF16), pl.BlockSpec((512, D), lambda m: (m, 0)), (D // 512,),
                      "mix_out_dw", 40)
    du_pool, d_wpool, d_scale = _pool_bwd(dy, sv["pooled"], w["w_pool"], w["pool_scale"])
    du_gla, d_walpha, d_balpha, d_gnorm = _gla_bwd(sv["u"], sv["o_gla"], sv["states"], dy, w["w_alpha"], w["b_alpha"],
                                                   w["gla_norm"])
    du = jnp.concatenate([du_pool, du_gla], axis=1)
    dh1, d_mix = _matmul_normbwd(du, w["w_in_mix"], h1, w["mix_norm"], dh2, "mix_in_bwd")
    d_wmix = _tn_matmul(du, sv["n2"], pl.BlockSpec((S, 1408), lambda j, m: (0, j)), pl.BlockSpec((S, 512), lambda j, m: (0, m)),
                        jax.ShapeDtypeStruct((D_IN_PAD, D), BF16), pl.BlockSpec((1408, 512), lambda j, m: (j, m)),
                        (D_IN_PAD // 1408, D // 512), "mix_in_dw", 32)
    return dh1, dict(mix_norm=d_mix, w_in_mix=d_wmix, w_pool=d_wpool, pool_scale=d_scale, w_alpha=d_walpha,
                     b_alpha=d_balpha, gla_norm=d_gnorm, w_out_mix=d_wo)


def _bwd_ffn1(dh1, x, n1, gu1, w, after=(), between=None):
    dgu1, act1, df1, dx, _, d_norm = _ffn_bwd_x(dh1, x, w["ffn1_norm"], gu1, w["ffn1_w_in"], w["ffn1_w_out"], "ffn1_bwd", after)
    d_in, d_out = _ffn_weight_grads(n1, dgu1, act1, df1, "ffn1", () if between is None else between(dx, d_norm))
    return dx, dict(ffn1_norm=d_norm, ffn1_w_in=d_in, ffn1_w_out=d_out)


def _local_step(x, tgt, w):
    h1, n1, gu1 = _fwd_ffn1(x, w)
    h2, sv = _fwd_mixer(h1, w)
    dh3, d_final, loss, n3, gu3 = _fwd_ffn2_loss(h2, tgt, w)
    dh2, dh2b, g2 = _bwd_ffn2(dh3, h2, n3, gu3, w)
    dh1, gm = _bwd_mixer(dh2, dh2b, h1, sv, w)
    dx, g1 = _bwd_ffn1(dh1, x, n1, gu1, w)
    return loss, dx, dict(final_norm=d_final, **g1, **gm, **g2)


def _mesh_index():
    return 4 * lax.axis_index("x") + 2 * lax.axis_index("y") + lax.axis_index("c")


def _coords(p):
    return (p // 4, (p // 2) % 2, p % 2)


NCHIP = 4


def _place():
    return lax.axis_index("x"), lax.axis_index("y"), lax.axis_index("c")


def _rel_chip(x, y, rel):
    return ((1 - x) if rel & 1 else x, (1 - y) if rel & 2 else y)


def _dev_index(x, y, c):
    return 4 * x + 2 * y + c


def _cols(ref, p, width):
    return ref.at[:, pl.ds(pl.multiple_of(p * width, 128), width)]


def _sems(na, n):
    return [pltpu.SemaphoreType.DMA((na, n)), pltpu.SemaphoreType.DMA((na, n)), pltpu.SemaphoreType.DMA((na,))]


def _gather(items, name):
    arrays = [a for a, _ in items]
    kinds = [k for _, k in items]
    na = len(arrays)
    out_shape = [jax.ShapeDtypeStruct((NDEV,) + a.shape if k == "bcast" else (a.shape[0], NDEV * a.shape[1]), a.dtype)
                 for a, k in items]

    def body(*refs):
        ins, outs = refs[:na], refs[na:2 * na]
        send_sems, recv_sems, local_sems = refs[2 * na:]
        x, y, c = _place()
        sibling = (x, y, 1 - c)
        here, over_x, over_y, across = (x, y), (1 - x, y), (x, 1 - y), (1 - x, 1 - y)

        def half(ref, h):
            rows = ref.shape[0] // 2
            return ref.at[pl.ds(h * rows, rows), :]

        def slab(a, chip, core, h=None):
            ref = _slab(outs[a], kinds[a], _dev_index(*chip, core), ins[a].shape[1])
            return ref if h is None else half(ref, h)

        def copy(a, k, src, dst, to):
            return pltpu.make_async_remote_copy(src, dst, send_sems.at[a, k], recv_sems.at[a, k], device_id=to, device_id_type=MESH)

        sent = []

        def send(a, k, src, dst, to):
            sent.append(copy(a, k, src, dst, to))
            sent[-1].start()

        def arrived(a, k, chip, core, h=None):
            ref = slab(a, chip, core, h)
            copy(a, k, ref, ref, sibling).wait_recv()
            return ref

        local = [pltpu.make_async_copy(ins[a], slab(a, here, c), local_sems.at[a]) for a in range(na)]
        for cp in local:
            cp.start()
        for k, h, chip in ((1, 0, over_x), (4, 1, over_y), (2, 1, over_x), (5, 0, over_y)):
            for a in range(na):
                send(a, k, half(ins[a], h), slab(a, here, c, h), (*chip, c))
        for a in range(na):
            send(a, 0, ins[a], slab(a, here, c), sibling)
        for k, chip, h, onward, to, down in ((1, over_x, 0, 6, over_y, 7), (4, over_y, 1, 3, over_x, 10),
                                             (2, over_x, 1, None, None, 8), (5, over_y, 0, None, None, 9),
                                             (3, across, 1, None, None, 12), (6, across, 0, None, None, 11)):
            for a in range(na):
                ref = arrived(a, k, chip, c, h)
                if onward is not None:
                    send(a, onward, ref, ref, (*to, c))
                send(a, down, ref, ref, sibling)
        for a in range(na):
            arrived(a, 0, here, 1 - c)
        for k, chip, h in ((7, over_x, 0), (10, over_y, 1), (8, over_x, 1), (9, over_y, 0), (12, across, 1), (11, across, 0)):
            for a in range(na):
                arrived(a, k, chip, 1 - c, h)
        for cp in sent:
            cp.wait_send()
        for cp in local:
            cp.wait()

    return pl.pallas_call(body, name=name, in_specs=[ANY] * na, out_specs=[ANY] * na, out_shape=out_shape,
                          scratch_shapes=_sems(na, 13))(*arrays)


def _pair_exchange(items, name):
    arrays = [a for a, _ in items]
    kinds = [k for _, k in items]
    na = len(arrays)
    out_shape = [jax.ShapeDtypeStruct((NCHIP,) + (a.shape[1:] if k == "scatter" else (a.shape[0], a.shape[1] // NDEV)), a.dtype)
                 for a, k in items]

    def body(*refs):
        ins, outs = refs[:na], refs[na:2 * na]
        send_sems, recv_sems = refs[2 * na:]
        x, y, c = _place()
        copies = []
        for rel in range(NCHIP):
            p = _dev_index(*_rel_chip(x, y, rel), 1 - c)
            for a in range(na):
                src = ins[a].at[p] if kinds[a] == "scatter" else _cols(ins[a], p, ins[a].shape[1] // NDEV)
                copies.append(pltpu.make_async_remote_copy(src, outs[a].at[rel], send_sems.at[a, rel], recv_sems.at[a, rel],
                                                           device_id=(x, y, 1 - c), device_id_type=MESH))
                copies[-1].start()
        for cp in copies:
            cp.wait_send()
            cp.wait_recv()

    return pl.pallas_call(body, name=name, in_specs=[ANY] * na, out_specs=[ANY] * na, out_shape=out_shape,
                          scratch_shapes=_sems(na, NCHIP)[:2])(*arrays)


def _pair_add(own, kind, got, table, tr, name):
    _, rows, cols = got.shape
    if kind == "scatter":
        own_spec = pl.BlockSpec((None, tr, cols), lambda rel, i, t: (t[rel], i, 0))
    else:
        own_spec = pl.BlockSpec((tr, cols), lambda rel, i, t: (i, t[rel]))
    blk = pl.BlockSpec((None, tr, cols), lambda rel, i, t: (rel, i, 0))

    def body(t_ref, a_ref, b_ref, o_ref):
        o_ref[...] = (a_ref[...].astype(F32) + b_ref[...].astype(F32)).astype(o_ref.dtype)

    return pl.pallas_call(
        body, name=name, out_shape=jax.ShapeDtypeStruct(got.shape, got.dtype),
        grid_spec=pltpu.PrefetchScalarGridSpec(num_scalar_prefetch=1, grid=(NCHIP, rows // tr), in_specs=[own_spec, blk],
                                               out_specs=blk),
        compiler_params=_params(32),
    )(table, own, got)


def _exchange(items, name):
    arrays = [a for a, _ in items]
    kinds = [k for _, k in items]
    na = len(arrays)
    out_shape = [jax.ShapeDtypeStruct((NDEV,) + a.shape if k == "bcast" else a.shape, a.dtype) for a, k in items]

    def body(*refs):
        ins, outs = refs[:na], refs[na:2 * na]
        send_sems, recv_sems, local_sems = refs[2 * na:]
        x, y, c = _place()
        me = _dev_index(x, y, c)

        def mine(a):
            return 0 if kinds[a] == "chip" else me

        def src(a, p):
            return ins[a] if kinds[a] == "bcast" else ins[a].at[p]

        local = [pltpu.make_async_copy(src(a, mine(a)), outs[a].at[mine(a)], local_sems.at[a]) for a in range(na)]
        for cp in local:
            cp.start()
        waits = []
        for k in range(1, NDEV):
            to = (me + k) % NDEV
            frm = (me + NDEV - k) % NDEV
            for a in range(na):
                if kinds[a] == "chip":
                    if k >= NCHIP:
                        continue
                    send = pltpu.make_async_remote_copy(ins[a].at[k], outs[a].at[k], send_sems.at[a, k], recv_sems.at[a, k],
                                                        device_id=(*_rel_chip(x, y, k), c), device_id_type=MESH)
                    waits.append(send)
                else:
                    send = pltpu.make_async_remote_copy(src(a, to), outs[a].at[me], send_sems.at[a, k], recv_sems.at[a, k],
                                                        device_id=_coords(to), device_id_type=MESH)
                    waits.append(pltpu.make_async_remote_copy(src(a, to), outs[a].at[frm], send_sems.at[a, k],
                                                              recv_sems.at[a, k], device_id=_coords(to), device_id_type=MESH))
                send.start()
        for cp in waits:
            cp.wait_send()
            cp.wait_recv()
        for cp in local:
            cp.wait()

    return pl.pallas_call(body, name=name, in_specs=[ANY] * na, out_specs=[ANY] * na, out_shape=out_shape,
                          scratch_shapes=_sems(na, NDEV))(*arrays)


HBM = pl.BlockSpec(memory_space=pltpu.HBM)
SEM = pl.BlockSpec(memory_space=pltpu.SEMAPHORE)
DATAFLOW = pltpu.SideEffectType.DATAFLOW_SIDE_EFFECTING


def _pair_copies(kinds):
    def describe(srcs, lands, send_sems, recv_sems):
        x, y, c = _place()
        na = len(srcs)
        for rel in range(NCHIP):
            p = _dev_index(*_rel_chip(x, y, rel), 1 - c)
            for a in range(na):
                src = srcs[a].at[p] if kinds[a] == "scatter" else _cols(srcs[a], p, srcs[a].shape[1] // NDEV)
                cp = pltpu.make_async_remote_copy(src, lands[a].at[rel], send_sems.at[rel * na + a], recv_sems.at[rel * na + a],
                                                  device_id=(x, y, 1 - c), device_id_type=MESH)
                yield cp, cp
    return describe


def _chip_copies(srcs, lands, send_sems, recv_sems):
    x, y, c = _place()
    na = len(srcs)
    for rel in range(1, NCHIP):
        for a in range(na):
            i = (rel - 1) * na + a
            cp = pltpu.make_async_remote_copy(srcs[a].at[rel], lands[a].at[rel], send_sems.at[i], recv_sems.at[i],
                                              device_id=(*_rel_chip(x, y, rel), c), device_id_type=MESH)
            yield cp, cp


def _slab(ref, kind, s, width):
    return _cols(ref, s, width) if kind == "bcast_cols" else ref.at[s]


def _all_copies(srcs, lands, send_sems, recv_sems):
    x, y, c = _place()
    na = len(srcs)
    me = _dev_index(x, y, c)
    for a in range(na):
        yield pltpu.make_async_copy(srcs[a], lands[a].at[me], send_sems.at[a]), None
    for k in range(1, NDEV):
        to, frm = (me + k) % NDEV, (me + NDEV - k) % NDEV
        for a in range(na):
            i = k * na + a
            send = pltpu.make_async_remote_copy(srcs[a], lands[a].at[me], send_sems.at[i], recv_sems.at[i],
                                                device_id=_coords(to), device_id_type=MESH)
            arrival = pltpu.make_async_remote_copy(srcs[a], lands[a].at[frm], send_sems.at[i], recv_sems.at[i],
                                                   device_id=_coords(to), device_id_type=MESH)
            yield send, arrival


def _gather_copies(kinds):
    def describe(srcs, lands, send_sems, recv_sems):
        x, y, c = _place()
        na = len(srcs)
        me = _dev_index(x, y, c)
        for a in range(na):
            yield pltpu.make_async_copy(srcs[a], _slab(lands[a], kinds[a], me, srcs[a].shape[-1]),
                                        send_sems.at[NCHIP * na + a]), None
        for rel in range(NCHIP):
            to = (x, y, 1 - c) if rel == 0 else (*_rel_chip(x, y, rel), c)
            for a in range(na):
                width = srcs[a].shape[-1]
                i = rel * na + a
                send = pltpu.make_async_remote_copy(srcs[a], _slab(lands[a], kinds[a], me, width), send_sems.at[i], recv_sems.at[i],
                                                    device_id=to, device_id_type=MESH)
                arrival = pltpu.make_async_remote_copy(srcs[a], _slab(lands[a], kinds[a], _dev_index(*to), width), send_sems.at[i],
                                                       recv_sems.at[i], device_id=to, device_id_type=MESH)
                yield send, arrival
    return describe


def _pass_to_sibling(arrays, kinds, widths, name):
    na = len(arrays)

    def body(*refs):
        bufs = refs[na:2 * na]
        send_sems, recv_sems = refs[2 * na:]
        x, y, c = _place()
        copies = []
        for rel in range(1, NCHIP):
            for a in range(na):
                mine = _slab(bufs[a], kinds[a], _dev_index(*_rel_chip(x, y, rel), c), widths[a])
                theirs = _slab(bufs[a], kinds[a], _dev_index(*_rel_chip(x, y, rel), 1 - c), widths[a])
                send = pltpu.make_async_remote_copy(mine, mine, send_sems.at[a, rel], recv_sems.at[a, rel],
                                                    device_id=(x, y, 1 - c), device_id_type=MESH)
                send.start()
                copies.append((send, pltpu.make_async_remote_copy(theirs, theirs, send_sems.at[a, rel], recv_sems.at[a, rel],
                                                                  device_id=(x, y, 1 - c), device_id_type=MESH)))
        for send, arrival in copies:
            send.wait_send()
            arrival.wait_recv()

    return pl.pallas_call(body, name=name, in_specs=[ANY] * na, out_specs=[ANY] * na,
                          out_shape=[jax.ShapeDtypeStruct(a.shape, a.dtype) for a in arrays],
                          input_output_aliases={i: i for i in range(na)}, scratch_shapes=_sems(na, NCHIP)[:2])(*arrays)


def _start_copies(name, srcs, lands, describe, ncopies, after=()):
    arrays = list(srcs) + list(lands)
    ns, n, nin = len(srcs), len(arrays), len(arrays) + len(after)

    def body(*refs):
        for send, _ in describe(refs[:ns], refs[ns:n], refs[nin], refs[nin + 1]):
            send.start()
        refs[-1][...] = jnp.zeros_like(refs[-1])

    out = pl.pallas_call(
        body, name=name,
        out_shape=(pltpu.SemaphoreType.DMA((ncopies,)), pltpu.SemaphoreType.DMA((ncopies,)),
                   *[pltpu.HBM(a.shape, a.dtype) for a in arrays], jax.ShapeDtypeStruct((8, 128), F32)),
        in_specs=[HBM] * n + [ANY] * len(after), out_specs=(SEM, SEM, *[HBM] * n, pl.BlockSpec(memory_space=pltpu.VMEM)),
        input_output_aliases={i: 2 + i for i in range(n)},
        compiler_params=pltpu.CompilerParams(has_side_effects=DATAFLOW),
    )(*[pltpu.with_memory_space_constraint(a, pltpu.HBM) for a in arrays], *after)
    return out[0], out[1], list(out[2:2 + n]), out[-1]


def _wait_copies(name, send_sems, recv_sems, thru, ns, describe, after):
    n = len(thru)

    def body(*refs):
        for send, arrival in describe(refs[:ns], refs[ns:n], refs[n], refs[n + 1]):
            if arrival is None:
                send.wait()
            else:
                send.wait_send()
                arrival.wait_recv()

    out = pl.pallas_call(
        body, name=name, out_shape=tuple(pltpu.HBM(a.shape, a.dtype) for a in thru),
        in_specs=[HBM] * n + [SEM, SEM] + [ANY] * len(after), out_specs=tuple([HBM] * n),
        input_output_aliases={i: i for i in range(n)},
        compiler_params=pltpu.CompilerParams(has_side_effects=DATAFLOW),
    )(*thru, send_sems, recv_sems, *after)
    return list(out[:ns]), list(out[ns:])


def _adamw(parts, w, m, v, tr, name, tc=None):
    rows, cols = w.shape
    tc = cols if tc is None else tc
    nparts = len(parts)
    blk = pl.BlockSpec((tr, tc), lambda i, j: (i, j))

    def slab_spec(s):
        return pl.BlockSpec((None, tr, tc), lambda i, j: (s, i, j))

    def body(*refs):
        p_refs = refs[:nparts]
        w_ref, m_ref, v_ref, g_ref, d_ref, nm_ref, nv_ref = refs[nparts:]
        g = p_refs[0][...].astype(F32)
        for p_ref in p_refs[1:]:
            g = g + p_ref[...].astype(F32)
        nm = ADAM_B1 * m_ref[...] + (1.0 - ADAM_B1) * g
        nv = ADAM_B2 * v_ref[...] + (1.0 - ADAM_B2) * (g * g)
        m_hat = nm / (1.0 - ADAM_B1 ** ADAM_STEP)
        v_hat = nv / (1.0 - ADAM_B2 ** ADAM_STEP)
        g_ref[...] = g
        d_ref[...] = -ADAM_LR * (m_hat / (jnp.sqrt(v_hat) + ADAM_EPS) + ADAM_WD * w_ref[...])
        nm_ref[...] = nm
        nv_ref[...] = nv

    return pl.pallas_call(
        body, grid=(rows // tr, cols // tc), name=name,
        in_specs=[slab_spec(s) for _, s in parts] + [blk, blk, blk], out_specs=[blk] * 4,
        out_shape=[jax.ShapeDtypeStruct((rows, cols), F32)] * 4,
        compiler_params=_params(40),
    )(*[a for a, _ in parts], w, m, v)


def _pack_small(vals, extra=None):
    flat = [vals[n].reshape(-1).astype(F32) for n, _ in SMALL]
    tail = jnp.zeros((SMALL_ROWS * 128 - LOSS_AT,), F32)
    if extra is not None:
        tail = tail.at[0].set(extra)
    return jnp.concatenate(flat + [tail]).reshape(SMALL_ROWS, 128)


def _unpack_small(packed, like):
    flat, out, at = packed.reshape(-1), {}, 0
    for n, size in SMALL:
        out[n] = flat[at:at + size].reshape(like[n].shape)
        at += size
    return out, flat[LOSS_AT]


def kernel(x, ffn1_norm, ffn1_w_in, ffn1_w_out, mix_norm, w_in_mix, w_pool, pool_scale, w_alpha, b_alpha, gla_norm, w_out_mix, ffn2_norm, ffn2_w_in, ffn2_w_out, final_norm, loss_target, m_ffn1_norm, m_ffn1_w_in, m_ffn1_w_out, m_mix_norm, m_w_in_mix, m_w_pool, m_pool_scale, m_w_alpha, m_b_alpha, m_gla_norm, m_w_out_mix, m_ffn2_norm, m_ffn2_w_in, m_ffn2_w_out, m_final_norm, v_ffn1_norm, v_ffn1_w_in, v_ffn1_w_out, v_mix_norm, v_w_in_mix, v_w_pool, v_pool_scale, v_w_alpha, v_b_alpha, v_gla_norm, v_w_out_mix, v_ffn2_norm, v_ffn2_w_in, v_ffn2_w_out, v_final_norm):
    names = ["ffn1_norm", "ffn1_w_in", "ffn1_w_out", "mix_norm", "w_in_mix", "w_pool", "pool_scale", "w_alpha", "b_alpha",
             "gla_norm", "w_out_mix", "ffn2_norm", "ffn2_w_in", "ffn2_w_out", "final_norm"]
    p = dict(zip(names, [ffn1_norm, ffn1_w_in, ffn1_w_out, mix_norm, w_in_mix, w_pool, pool_scale, w_alpha, b_alpha,
                         gla_norm, w_out_mix, ffn2_norm, ffn2_w_in, ffn2_w_out, final_norm]))
    m = dict(zip(names, [m_ffn1_norm, m_ffn1_w_in, m_ffn1_w_out, m_mix_norm, m_w_in_mix, m_w_pool, m_pool_scale, m_w_alpha,
                         m_b_alpha, m_gla_norm, m_w_out_mix, m_ffn2_norm, m_ffn2_w_in, m_ffn2_w_out, m_final_norm]))
    v = dict(zip(names, [v_ffn1_norm, v_ffn1_w_in, v_ffn1_w_out, v_mix_norm, v_w_in_mix, v_w_pool, v_pool_scale, v_w_alpha,
                         v_b_alpha, v_gla_norm, v_w_out_mix, v_ffn2_norm, v_ffn2_w_in, v_ffn2_w_out, v_final_norm]))

    mx, my, mc = _place()
    me = _dev_index(mx, my, mc)
    table = jnp.stack([_dev_index(*_rel_chip(mx, my, rel), mc) for rel in range(NCHIP)]).astype(jnp.int32)

    def landing(shard, kind):
        shape = (shard.shape[0], NDEV * shard.shape[1]) if kind == "bcast_cols" else (NDEV,) + shard.shape
        return lax.empty(shape, shard.dtype)

    def gather_begin(items, tag, after):
        kinds = [kind for _, kind in items]
        copies = _gather_copies(kinds)
        s, r, thru, tok = _start_copies(tag + "_start", [a for a, _ in items], [landing(a, kind) for a, kind in items], copies,
                                        (NCHIP + 1) * len(items), after)
        return (s, r, thru, copies, kinds, [a.shape[-1] for a, _ in items], tag), tok

    def gather_end(state, after):
        s, r, thru, copies, kinds, widths, tag = state
        _, lands = _wait_copies(tag + "_wait", s, r, thru, len(kinds), copies, after)
        return _pass_to_sibling(lands, kinds, widths, tag + "_pass")

    def shard16(n):
        return p[n][0].astype(BF16)

    g_w1in, g_w1out = _gather([(shard16("ffn1_w_in"), "bcast_cols"), (shard16("ffn1_w_out"), "bcast")], "gather_ffn1")
    mix_state, tok_m = gather_begin([(jnp.transpose(p["w_in_mix"][0]).astype(BF16), "bcast"), (shard16("w_out_mix"), "bcast"),
                                     (p["w_pool"][0].reshape(H * 32, PG), "bcast"), (p["w_alpha"][0], "bcast")], "gather_mix",
                                    (g_w1out,))
    ffn2_state, tok_f = gather_begin([(shard16("ffn2_w_in"), "bcast_cols"), (shard16("ffn2_w_out"), "bcast")], "gather_ffn2",
                                     (tok_m,))
    full = {"ffn1_w_in": g_w1in, "ffn1_w_out": g_w1out.reshape(FF, D), "final_norm": final_norm.reshape(1, D)}
    for n in ("ffn1_norm", "mix_norm", "ffn2_norm", "pool_scale", "b_alpha", "gla_norm"):
        full[n] = p[n]

    xs, tgt = x[0], loss_target[0]
    h1, n1, gu1 = _fwd_ffn1(xs, full, after=(tok_m, tok_f))
    g_wmix, g_wo, g_wpool, g_walpha = gather_end(mix_state, (h1,))
    walpha = jnp.transpose(g_walpha, (1, 0, 2)).reshape(RANK, H * DK)
    full.update({
        "w_in_mix": jnp.pad(g_wmix.reshape(D_IN, D), ((0, D_IN_PAD - D_IN), (0, 0))),
        "w_out_mix": g_wo.reshape(D, D),
        "w_pool": jnp.transpose(g_wpool.reshape(NDEV, H, 32, PG), (1, 0, 2, 3)).reshape(H, PG, PG).astype(BF16),
        "w_alpha": jnp.pad(walpha, ((0, RANK_PAD - RANK), (0, 0))).astype(BF16),
    })
    h2, sv = _fwd_mixer(h1, full)
    g_w2in, g_w2out = gather_end(ffn2_state, (h2,))
    full.update({"ffn2_w_in": g_w2in, "ffn2_w_out": g_w2out.reshape(FF, D)})
    dh3, d_final, loss_part, n3, gu3 = _fwd_ffn2_loss(h2, tgt, full)


    def slab_shape(a, kind):
        return (NCHIP,) + (a.shape[1:] if kind == "scatter" else (a.shape[0], a.shape[1] // NDEV))

    def pair_add_all(own, got, tag):
        return [_pair_add(a, kind, got_a, table, tr, "%s_pair_add_%d" % (tag, i))
                for i, ((a, kind, tr), got_a) in enumerate(zip(own, got))]

    def reduce_begin(own, tag, after=()):
        kinds = [kind for _, kind, _ in own]
        copies = _pair_copies(kinds)
        s, r, thru, tok = _start_copies(tag + "_pair_start", [a for a, _, _ in own],
                                        [lax.empty(slab_shape(a, kind), a.dtype) for a, kind, _ in own], copies,
                                        NCHIP * len(own), after)
        return dict(own=own, copies=copies, s=s, r=r, thru=thru, tag=tag), tok

    def reduce_middle(st, after):
        own, tag = st["own"], st["tag"]
        sent, got = _wait_copies(tag + "_pair_wait", st["s"], st["r"], st["thru"], len(own), st["copies"], after)
        pre = pair_add_all([(a, kind, tr) for a, (_, kind, tr) in zip(sent, own)], got, tag)
        st["s"], st["r"], st["thru"], tok = _start_copies(tag + "_chip_start", pre, [lax.empty(a.shape, a.dtype) for a in pre],
                                                          _chip_copies, (NCHIP - 1) * len(pre))
        return tok

    def reduce_end(st, after):
        n = len(st["own"])
        pre, land = _wait_copies(st["tag"] + "_chip_wait", st["s"], st["r"], st["thru"], n, _chip_copies, after)
        return [[(a, 0)] + [(b, rel) for rel in range(1, NCHIP)] for a, b in zip(pre, land)]

    dh2, dh2b, g2 = _bwd_ffn2(dh3, h2, n3, gu3, full)
    red2, tok = reduce_begin([(g2["ffn2_w_in"], "scatter_cols", 256),
                              (g2["ffn2_w_out"].reshape(NDEV, WOUT_SHARD, D), "scatter", WOUT_SHARD // 2)], "ffn2")
    dh1, gm = _bwd_mixer(dh2, dh2b, h1, sv, full, after=(tok,))
    tok2 = reduce_middle(red2, (dh1,))
    d_wmix8 = gm["w_in_mix"][:D_IN].reshape(NDEV, MIX_SHARD, D)
    d_wpool8 = jnp.transpose(gm["w_pool"].reshape(H, NDEV, 32, PG), (1, 0, 2, 3)).reshape(NDEV, H * 32, PG)
    d_walpha8 = jnp.transpose(gm["w_alpha"][:RANK].reshape(RANK, NDEV, H * DK // NDEV), (1, 0, 2))
    redm, tokm = reduce_begin([(d_wmix8, "scatter", MIX_SHARD), (gm["w_out_mix"].reshape(NDEV, D // NDEV, D), "scatter", D // NDEV),
                               (d_wpool8, "scatter", H * 32), (d_walpha8, "scatter", RANK)], "mix", (tok2,))
    small = {}

    def between(dx_, d_norm1):
        tok_mix = reduce_middle(redm, (dx_,))
        packed = _pack_small(dict(final_norm=d_final, ffn1_norm=d_norm1, **gm, **g2), loss_part[0, 0])
        small["s"], small["r"], small["thru"], tok_small = _start_copies(
            "gather_small_start", [packed], [lax.empty((NDEV,) + packed.shape, F32)], _all_copies, NDEV, (tok_mix,))
        return (tok_small,)

    dx, g1 = _bwd_ffn1(dh1, xs, n1, gu1, full, after=(tok2, tokm), between=between)
    red1, tok1 = reduce_begin([(g1["ffn1_w_in"], "scatter_cols", 256),
                               (g1["ffn1_w_out"].reshape(NDEV, WOUT_SHARD, D), "scatter", WOUT_SHARD // 2)], "ffn1")
    _, (r_small,) = _wait_copies("gather_small_wait", small["s"], small["r"], small["thru"], 1, _all_copies, (tok1,))

    def upd(parts, n, shape2d, tr):
        res = _adamw(parts, p[n].reshape(shape2d), m[n].reshape(shape2d), v[n].reshape(shape2d), tr, "adamw_" + n)
        return [r.reshape(p[n].shape) for r in res]

    def transposed(a):
        return jnp.transpose(a[0])

    tok1 = reduce_middle(red1, (r_small,))
    p_w2in, p_w2out = reduce_end(red2, (tok1,))
    p_wmix, p_wo, p_wpool, p_walpha = reduce_end(redm, (tok1,))
    out = {
        "ffn2_w_in": upd(p_w2in, "ffn2_w_in", (D, WIN_SHARD), 128),
        "ffn2_w_out": upd(p_w2out, "ffn2_w_out", (WOUT_SHARD, D), 64),
        "w_out_mix": upd(p_wo, "w_out_mix", (D // NDEV, D), 64),
        "w_pool": upd(p_wpool, "w_pool", (H * 32, PG), H * 32),
        "w_alpha": upd(p_walpha, "w_alpha", (RANK, H * DK // NDEV), RANK),
    }
    out["w_in_mix"] = [jnp.transpose(r)[None] for r in
                       _adamw(p_wmix, transposed(p["w_in_mix"]), transposed(m["w_in_mix"]), transposed(v["w_in_mix"]),
                              MIX_SHARD, "adamw_w_in_mix", tc=512)]
    small_res = _adamw([(r_small, s) for s in range(NDEV)], _pack_small(p), _pack_small(m), _pack_small(v), SMALL_ROWS,
                       "adamw_small")
    p_w1in, p_w1out = reduce_end(red1, (small_res[0], out["w_in_mix"][3], out["ffn2_w_in"][3]))
    out["ffn1_w_in"] = upd(p_w1in, "ffn1_w_in", (D, WIN_SHARD), 128)
    out["ffn1_w_out"] = upd(p_w1out, "ffn1_w_out", (WOUT_SHARD, D), 64)
    unpacked = [_unpack_small(r, p) for r in small_res]
    loss = unpacked[0][1]
    for n, _ in SMALL:
        out[n] = [u[0][n] for u in unpacked]

    return (loss, dx.reshape(1, S, D), *[out[n][0] for n in names], *[out[n][1] for n in names],
            *[out[n][2] for n in names], *[out[n][3] for n in names])
```

```python
import functools

import jax
import jax.numpy as jnp
from jax import lax
from jax.experimental import pallas as pl
from jax.experimental.pallas import tpu as pltpu

F32, BF16 = jnp.float32, jnp.bfloat16
MESH = pl.DeviceIdType.MESH
ANY = pl.BlockSpec(memory_space=pl.ANY)

NDEV = 8
S = 2048
D = 2048
FF = 5632
WIN_SHARD = 2 * FF // NDEV
WOUT_SHARD = FF // NDEV
D_POOL = 1024
PG = 256
POOL_WINDOWS = (2, 4, 8, 16)
H = 4
DK = 128
DV = 256
CH = 64
NCH = S // CH
RANK = 16
RANK_PAD = 128
D_IN = 4112
D_IN_PAD = 4224
MIX_SHARD = D_IN // NDEV
O_Q, O_K, O_V, O_G, O_R = 1024, 1536, 2048, 3072, 4096
GATE_NORM = 16.0
QK_SCALE = DK ** -0.5
EPS = 1e-6
ADAM_LR, ADAM_B1, ADAM_B2, ADAM_EPS, ADAM_WD, ADAM_STEP = 0.001, 0.9, 0.999, 1e-08, 0.01, 10
V7X_VMEM_BYTES = 64 << 20

SMALL = (("ffn1_norm", 2048), ("mix_norm", 2048), ("ffn2_norm", 2048), ("final_norm", 2048),
         ("pool_scale", 1024), ("b_alpha", 512), ("gla_norm", 256))
SMALL_ROWS = 80
LOSS_AT = sum(n for _, n in SMALL)


def _params(vmem_mb, sem=None):
    return pltpu.CompilerParams(dimension_semantics=sem, vmem_limit_bytes=min(vmem_mb << 20, V7X_VMEM_BYTES - (4 << 20)))


def _dot(a, b):
    return jnp.dot(a, b, preferred_element_type=F32)


def _dot_nt(a, b):
    return lax.dot_general(a, b, (((1,), (1,)), ((), ())), preferred_element_type=F32)


def _dot_tn(a, b):
    return lax.dot_general(a, b, (((0,), (0,)), ((), ())), preferred_element_type=F32)


def _sigmoid(x):
    return 1.0 / (1.0 + jnp.exp(-x))


def _log_sigmoid(x):
    return jnp.minimum(x, 0.0) - jnp.log(1.0 + jnp.exp(-jnp.abs(x)))


def _rms(x, g):
    r = lax.rsqrt(jnp.mean(x * x, axis=-1, keepdims=True) + EPS)
    return x * r * g


def _rms_bwd(dn, x, g):
    r = lax.rsqrt(jnp.mean(x * x, axis=-1, keepdims=True) + EPS)
    xh = x * r
    dxh = dn * g
    dx = r * (dxh - xh * jnp.mean(dxh * xh, axis=-1, keepdims=True))
    return dx, jnp.sum(dn * xh, axis=0, keepdims=True)


ROWS = 64


def _row_loop(total, fn, init=0):
    def step(t, carry):
        return fn(pl.ds(pl.multiple_of(t * ROWS, ROWS), ROWS), carry)
    return lax.fori_loop(0, total // ROWS, step, init)


def _split3(x):
    hi = x.astype(BF16)
    r1 = x - hi.astype(F32)
    mid = r1.astype(BF16)
    lo = (r1 - mid.astype(F32)).astype(BF16)
    return hi, mid, lo


def _tri_dot(tri_b, x):
    hi, mid, lo = _split3(x)
    return (_dot(tri_b, lo) + _dot(tri_b, mid)) + _dot(tri_b, hi)


FFN_TS, FFN_TF = 512, 512


def _ffn_specs():
    wg = pl.BlockSpec((D, FFN_TF), lambda i, j: (0, j))
    wu = pl.BlockSpec((D, FFN_TF), lambda i, j: (0, FF // FFN_TF + j))
    wo = pl.BlockSpec((FFN_TF, D), lambda i, j: (j, 0))
    row = pl.BlockSpec((FFN_TS, D), lambda i, j: (i, 0))
    vec = pl.BlockSpec((1, D), lambda i, j: (0, 0))
    gu = pl.BlockSpec((2, FFN_TS, FFN_TF), lambda i, j: (0, i, j))
    return wg, wu, wo, row, vec, gu


def _ordered_after(body, n_in, after):
    def wrapped(*refs):
        return body(*refs[:n_in], *refs[n_in + len(after):])
    return wrapped, [ANY] * len(after)


def _ffn_fwd(h, g, w_in8, w_out, name, after=()):
    nj = FF // FFN_TF
    wg, wu, wo, row, vec, gu = _ffn_specs()

    def body(h_ref, g_ref, wg_ref, wu_ref, wo_ref, ho_ref, n_ref, gu_ref, acc_ref):
        j = pl.program_id(1)

        @pl.when(j == 0)
        def _():
            def norm(rows, c):
                n_ref[rows, :] = _rms(h_ref[rows, :], g_ref[...]).astype(BF16)
                return c
            _row_loop(FFN_TS, norm)
            acc_ref[...] = jnp.zeros_like(acc_ref)

        n = n_ref[...]
        gate = _dot(n, wg_ref[...])
        up = _dot(n, wu_ref[...])
        gu_ref[0] = gate.astype(BF16)
        gu_ref[1] = up.astype(BF16)
        a = (gate * _sigmoid(gate)) * up
        acc_ref[...] += _dot(a.astype(BF16), wo_ref[...])

        @pl.when(j == nj - 1)
        def _():
            def residual(rows, c):
                ho_ref[rows, :] = h_ref[rows, :] + 0.5 * acc_ref[rows, :]
                return c
            _row_loop(FFN_TS, residual)

    body, extra = _ordered_after(body, 5, after)
    return pl.pallas_call(
        body, grid=(S // FFN_TS, nj), name=name,
        in_specs=[row, vec, wg, wu, wo] + extra, out_specs=[row, row, gu],
        out_shape=[jax.ShapeDtypeStruct((S, D), F32), jax.ShapeDtypeStruct((S, D), BF16),
                   jax.ShapeDtypeStruct((2, S, FF), BF16)],
        scratch_shapes=[pltpu.VMEM((FFN_TS, D), F32)],
        compiler_params=_params(56, ("arbitrary", "arbitrary")),
    )(h, g, w_in8, w_in8, w_out, *after)


def _ffn_bwd_x(dhp, h, g, gu_arr, w_in8, w_out, name, after=()):
    ni, nj = S // FFN_TS, FF // FFN_TF
    wg, wu, wo, row, vec, gu = _ffn_specs()
    act = pl.BlockSpec((FFN_TS, FFN_TF), lambda i, j: (i, j))

    def body(dhp_ref, h_ref, g_ref, gu_ref, wg_ref, wu_ref, wo_ref,
             dgu_ref, a_ref, df_ref, dh_ref, dhb_ref, dg_ref, acc_ref):
        i, j = pl.program_id(0), pl.program_id(1)

        @pl.when(j == 0)
        def _():
            def half(rows, c):
                df_ref[rows, :] = (0.5 * dhp_ref[rows, :]).astype(BF16)
                return c
            _row_loop(FFN_TS, half)
            acc_ref[...] = jnp.zeros_like(acc_ref)

        gate = gu_ref[0].astype(F32)
        up = gu_ref[1].astype(F32)
        da = _dot_nt(df_ref[...], wo_ref[...])
        sg = _sigmoid(gate)
        silu = gate * sg
        dgate = (da * up * (sg * (1.0 + gate * (1.0 - sg)))).astype(BF16)
        dup = (da * silu).astype(BF16)
        a_ref[...] = (silu * up).astype(BF16)
        dgu_ref[0] = dgate
        dgu_ref[1] = dup
        acc_ref[...] += _dot_nt(dgate, wg_ref[...]) + _dot_nt(dup, wu_ref[...])

        @pl.when(j == nj - 1)
        def _():
            def norm_bwd(rows, dg):
                dx, dg_rows = _rms_bwd(acc_ref[rows, :], h_ref[rows, :], g_ref[...])
                dh = dhp_ref[rows, :] + dx
                dh_ref[rows, :] = dh
                dhb_ref[rows, :] = dh.astype(BF16)
                return dg + dg_rows
            dg = _row_loop(FFN_TS, norm_bwd, jnp.zeros((1, D), F32))

            @pl.when(i == 0)
            def _():
                dg_ref[...] = dg

            @pl.when(i > 0)
            def _():
                dg_ref[...] += dg

    body, extra = _ordered_after(body, 7, after)
    return pl.pallas_call(
        body, grid=(ni, nj), name=name,
        in_specs=[row, row, vec, gu, wg, wu, wo] + extra,
        out_specs=[gu, act, row, row, row, vec],
        out_shape=[jax.ShapeDtypeStruct((2, S, FF), BF16), jax.ShapeDtypeStruct((S, FF), BF16),
                   jax.ShapeDtypeStruct((S, D), BF16), jax.ShapeDtypeStruct((S, D), F32),
                   jax.ShapeDtypeStruct((S, D), BF16), jax.ShapeDtypeStruct((1, D), F32)],
        scratch_shapes=[pltpu.VMEM((FFN_TS, D), F32)],
        compiler_params=_params(58, ("arbitrary", "arbitrary")),
    )(dhp, h, g, gu_arr, w_in8, w_in8, w_out, *after)


def _tn_matmul(a, b, a_spec, b_spec, out_shape, out_spec, grid, name, vmem_mb, after=()):
    def body(a_ref, b_ref, o_ref):
        o_ref[...] = _dot_tn(a_ref[...], b_ref[...]).astype(o_ref.dtype)

    body, extra = _ordered_after(body, 2, after)
    return pl.pallas_call(body, grid=grid, name=name, in_specs=[a_spec, b_spec] + extra, out_specs=out_spec,
                          out_shape=out_shape, compiler_params=_params(vmem_mb))(a, b, *after)


def _norm_matmul(h, g, w, name, after=(), ts=512, tn=1408):
    n_out = w.shape[0]

    def body(h_ref, g_ref, w_ref, u_ref, n_ref):
        @pl.when(pl.program_id(1) == 0)
        def _():
            def norm(rows, c):
                n_ref[rows, :] = _rms(h_ref[rows, :], g_ref[...]).astype(BF16)
                return c
            _row_loop(ts, norm)

        u_ref[...] = _dot_nt(n_ref[...], w_ref[...])

    body, extra = _ordered_after(body, 3, after)
    return pl.pallas_call(
        body, grid=(S // ts, n_out // tn), name=name,
        in_specs=[pl.BlockSpec((ts, D), lambda i, j: (i, 0)), pl.BlockSpec((1, D), lambda i, j: (0, 0)),
                  pl.BlockSpec((tn, D), lambda i, j: (j, 0))] + extra,
        out_specs=[pl.BlockSpec((ts, tn), lambda i, j: (i, j)), pl.BlockSpec((ts, D), lambda i, j: (i, 0))],
        out_shape=[jax.ShapeDtypeStruct((S, n_out), F32), jax.ShapeDtypeStruct((S, D), BF16)],
        compiler_params=_params(48, ("arbitrary", "arbitrary")),
    )(h, g, w, *after)


def _matmul_residual(a, w, res, name, ts=512, tn=1024):
    k, n_out = w.shape

    def body(a_ref, w_ref, r_ref, o_ref):
        o_ref[...] = r_ref[...] + _dot(a_ref[...], w_ref[...])

    return pl.pallas_call(
        body, grid=(S // ts, n_out // tn), name=name,
        in_specs=[pl.BlockSpec((ts, k), lambda i, j: (i, 0)), pl.BlockSpec((k, tn), lambda i, j: (0, j)),
                  pl.BlockSpec((ts, tn), lambda i, j: (i, j))],
        out_specs=pl.BlockSpec((ts, tn), lambda i, j: (i, j)),
        out_shape=jax.ShapeDtypeStruct((S, n_out), F32),
        compiler_params=_params(40),
    )(a, w, res)


def _nt_matmul(a, w, name, after=(), ts=512, tn=1024):
    n_out, k = w.shape

    def body(a_ref, w_ref, o_ref):
        o_ref[...] = _dot_nt(a_ref[...], w_ref[...])

    body, extra = _ordered_after(body, 2, after)
    return pl.pallas_call(
        body, grid=(S // ts, n_out // tn), name=name,
        in_specs=[pl.BlockSpec((ts, k), lambda i, j: (i, 0)), pl.BlockSpec((tn, k), lambda i, j: (j, 0))] + extra,
        out_specs=pl.BlockSpec((ts, tn), lambda i, j: (i, j)),
        out_shape=jax.ShapeDtypeStruct((S, n_out), F32),
        compiler_params=_params(40),
    )(a, w, *after)


def _matmul_normbwd(du, w, h, g, dres, name, after=(), ts=512, tn=1408):
    n_in = w.shape[0]
    nj = n_in // tn
    row = pl.BlockSpec((ts, D), lambda i, j: (i, 0))
    vec = pl.BlockSpec((1, D), lambda i, j: (0, 0))

    def body(du_ref, w_ref, h_ref, g_ref, dres_ref, dh_ref, dg_ref, acc_ref):
        i, j = pl.program_id(0), pl.program_id(1)

        @pl.when(j == 0)
        def _():
            acc_ref[...] = jnp.zeros_like(acc_ref)

        acc_ref[...] += _dot(du_ref[...], w_ref[...])

        @pl.when(j == nj - 1)
        def _():
            def norm_bwd(rows, dg):
                dx, dg_rows = _rms_bwd(acc_ref[rows, :], h_ref[rows, :], g_ref[...])
                dh_ref[rows, :] = dres_ref[rows, :] + dx
                return dg + dg_rows
            dg = _row_loop(ts, norm_bwd, jnp.zeros((1, D), F32))

            @pl.when(i == 0)
            def _():
                dg_ref[...] = dg

            @pl.when(i > 0)
            def _():
                dg_ref[...] += dg

    body, extra = _ordered_after(body, 5, after)
    return pl.pallas_call(
        body, grid=(S // ts, nj), name=name,
        in_specs=[pl.BlockSpec((ts, tn), lambda i, j: (i, j)), pl.BlockSpec((tn, D), lambda i, j: (j, 0)), row, vec, row] + extra,
        out_specs=[row, vec],
        out_shape=[jax.ShapeDtypeStruct((S, D), F32), jax.ShapeDtypeStruct((1, D), F32)],
        scratch_shapes=[pltpu.VMEM((ts, D), F32)],
        compiler_params=_params(52, ("arbitrary", "arbitrary")),
    )(du, w, h, g, dres, *after)


def _loss_head(h, g, tgt, ts=256):
    row = pl.BlockSpec((ts, D), lambda i: (i, 0))
    vec = pl.BlockSpec((1, D), lambda i: (0, 0))

    def body(h_ref, g_ref, t_ref, dh_ref, dg_ref, loss_ref):
        i = pl.program_id(0)

        def rows_fn(rows, carry):
            dg, part = carry
            x = h_ref[rows, :]
            gv = g_ref[...]
            err = _rms(x, gv) - t_ref[rows, :]
            part = part + 0.5 * jnp.sum(jnp.mean(err * err, axis=-1, keepdims=True), axis=0, keepdims=True)
            dx, dg_rows = _rms_bwd(err * (1.0 / D), x, gv)
            dh_ref[rows, :] = dx
            return dg + dg_rows, part
        dg, part = _row_loop(ts, rows_fn, (jnp.zeros((1, D), F32), jnp.zeros((1, 1), F32)))

        @pl.when(i == 0)
        def _():
            dg_ref[...] = dg
            loss_ref[...] = jnp.broadcast_to(part, loss_ref.shape)

        @pl.when(i > 0)
        def _():
            dg_ref[...] += dg
            loss_ref[...] += jnp.broadcast_to(part, loss_ref.shape)

    return pl.pallas_call(
        body, grid=(S // ts,), name="loss_head",
        in_specs=[row, vec, row], out_specs=[row, vec, pl.BlockSpec((1, 128), lambda i: (0, 0))],
        out_shape=[jax.ShapeDtypeStruct((S, D), F32), jax.ShapeDtypeStruct((1, D), F32),
                   jax.ShapeDtypeStruct((1, 128), F32)],
        compiler_params=_params(40, ("arbitrary",)),
    )(h, g, tgt)


def _pool_specs():
    blk = pl.BlockSpec((S, PG), lambda gi: (0, gi))
    wp = pl.BlockSpec((None, PG, PG), lambda gi: (gi, 0, 0))
    sc = pl.BlockSpec((1, PG), lambda gi: (0, gi))
    return blk, wp, sc


def _pool_fwd(u, wp_b, scale):
    blk, wp, sc = _pool_specs()

    def body(u_ref, wp_ref, sc_ref, y_ref, pooled_ref):
        win = 2 << pl.program_id(0)
        row = lax.broadcasted_iota(jnp.int32, (S, PG), 0)
        x = u_ref[...]
        s = x
        for k in (1, 2, 4, 8):
            s = s + jnp.where((row >= k) & (k < win), pltpu.roll(s, k, 0), 0.0)
        cnt = jnp.minimum(row + 1, win).astype(F32)
        pooled = (s / cnt - x).astype(BF16)
        pooled_ref[...] = pooled
        y_ref[...] = (_dot(pooled, wp_ref[...]) * sc_ref[...]).astype(BF16)

    return pl.pallas_call(
        body, grid=(len(POOL_WINDOWS),), name="pool_fwd", in_specs=[blk, wp, sc], out_specs=[blk, blk],
        out_shape=[jax.ShapeDtypeStruct((S, D_POOL), BF16), jax.ShapeDtypeStruct((S, D_POOL), BF16)],
        compiler_params=_params(40),
    )(u, wp_b, scale)


def _pool_bwd(dy, pooled, wp_b, scale):
    blk, wp, sc = _pool_specs()

    def body(dy_ref, p_ref, wp_ref, sc_ref, du_ref, dwp_ref, dsc_ref):
        win = 2 << pl.program_id(0)
        row = lax.broadcasted_iota(jnp.int32, (S, PG), 0)
        dyv = dy_ref[...]
        pooled = p_ref[...]
        w = wp_ref[...]
        dsc_ref[...] = jnp.sum(dyv * _dot(pooled, w), axis=0, keepdims=True)
        dz = (dyv * sc_ref[...]).astype(BF16)
        dwp_ref[...] = _dot_tn(pooled, dz)
        dpooled = _dot_nt(dz, w)
        cnt = jnp.minimum(row + 1, win).astype(F32)
        fs = dpooled / cnt
        for k in (1, 2, 4, 8):
            fs = fs + jnp.where((row < S - k) & (k < win), pltpu.roll(fs, S - k, 0), 0.0)
        du_ref[...] = (fs - dpooled).astype(BF16)

    return pl.pallas_call(
        body, grid=(len(POOL_WINDOWS),), name="pool_bwd", in_specs=[blk, blk, wp, sc], out_specs=[blk, wp, sc],
        out_shape=[jax.ShapeDtypeStruct((S, D_POOL), BF16), jax.ShapeDtypeStruct((len(POOL_WINDOWS), PG, PG), F32),
                   jax.ShapeDtypeStruct((1, D_POOL), F32)],
        compiler_params=_params(40),
    )(dy, pooled, wp_b, scale)


def _gla_in_specs(chunk_of):
    def at(width, col):
        return pl.BlockSpec((CH, width), lambda n: (chunk_of(n), col))
    return [at(H * DK, O_Q // (H * DK)), at(H * DK, O_K // (H * DK)), at(H * DV, O_V // (H * DV)),
            at(H * DV, O_G // (H * DV)), at(RANK_PAD, O_R // RANK_PAD)]


def _gla_decay_terms(lr_ref, wa_ref, ba_ref, q_ref, k_ref):
    row = lax.broadcasted_iota(jnp.int32, (CH, CH), 0)
    col = lax.broadcasted_iota(jnp.int32, (CH, CH), 1)
    tril = row >= col
    z = _dot(lr_ref[...].astype(BF16), wa_ref[...]) + ba_ref[...]
    la = _log_sigmoid(z) / GATE_NORM
    b = _tri_dot(jnp.where(tril, 1.0, 0.0).astype(BF16), la)
    bl = jnp.sum(la, axis=0, keepdims=True)
    e_b, e_nb, e_tb = jnp.exp(b), jnp.exp(-b), jnp.exp(bl - b)
    kk = k_ref[...]
    q_dec = (q_ref[...] * QK_SCALE) * e_b
    return tril, z, e_b, e_nb, e_tb, jnp.exp(bl), q_dec, kk * e_nb, kk * e_tb


def _gla_fwd(u, wa_b, ba, gn):
    wide = pl.BlockSpec((CH, H * DV), lambda n: (n, 0))

    def body(q_ref, k_ref, v_ref, g_ref, lr_ref, wa_ref, ba_ref, gn_ref, y_ref, o_ref, st_ref, state):
        @pl.when(pl.program_id(0) == 0)
        def _():
            state[...] = jnp.zeros_like(state)

        tril, _, _, _, _, dec, q_dec, k_inv, k_tail = _gla_decay_terms(lr_ref, wa_ref, ba_ref, q_ref, k_ref)
        for hd in range(H):
            ks, vs = slice(hd * DK, (hd + 1) * DK), slice(hd * DV, (hd + 1) * DV)
            qb, kib, ktb = q_dec[:, ks].astype(BF16), k_inv[:, ks].astype(BF16), k_tail[:, ks].astype(BF16)
            vb = v_ref[:, vs].astype(BF16)
            p = jnp.where(tril, _dot_nt(qb, kib), 0.0)
            st = state[hd]
            o = _dot(p.astype(BF16), vb) + _dot_nt(qb, st.astype(BF16))
            st_ref[hd] = st
            state[hd] = st * dec[:, ks] + _dot_tn(vb, ktb)
            o_ref[:, vs] = o
            on = _rms(o, gn_ref[...])
            gg = g_ref[:, vs]
            y_ref[:, vs] = (on * (gg * _sigmoid(gg))).astype(BF16)

    return pl.pallas_call(
        body, grid=(NCH,), name="gla_fwd",
        in_specs=_gla_in_specs(lambda n: n) + [pl.BlockSpec((RANK_PAD, H * DK), lambda n: (0, 0)),
                                               pl.BlockSpec((1, H * DK), lambda n: (0, 0)),
                                               pl.BlockSpec((1, DV), lambda n: (0, 0))],
        out_specs=[wide, wide, pl.BlockSpec((None, H, DV, DK), lambda n: (n, 0, 0, 0))],
        out_shape=[jax.ShapeDtypeStruct((S, H * DV), BF16), jax.ShapeDtypeStruct((S, H * DV), F32),
                   jax.ShapeDtypeStruct((NCH, H, DV, DK), F32)],
        scratch_shapes=[pltpu.VMEM((H, DV, DK), F32)],
        compiler_params=_params(32, ("arbitrary",)),
    )(u, u, u, u, u, wa_b, ba, gn)


GLA_DU = 2 * H * DK + 2 * H * DV + RANK_PAD


def _gla_bwd(u, o_arr, states, dy, wa_b, ba, gn, after=()):
    rev = lambda n: NCH - 1 - n
    wide = pl.BlockSpec((CH, H * DV), lambda n: (rev(n), 0))

    def body(q_ref, k_ref, v_ref, g_ref, lr_ref, o_ref, st_ref, dy_ref, wa_ref, ba_ref, gn_ref,
             du_ref, dwa_ref, dba_ref, dgn_ref, gstate, db_scr, dbl_scr):
        @pl.when(pl.program_id(0) == 0)
        def _():
            gstate[...] = jnp.zeros_like(gstate)
            dwa_ref[...] = jnp.zeros_like(dwa_ref)
            dba_ref[...] = jnp.zeros_like(dba_ref)
            dgn_ref[...] = jnp.zeros_like(dgn_ref)

        tril, z, e_b, e_nb, e_tb, dec, q_dec, k_inv, k_tail = _gla_decay_terms(lr_ref, wa_ref, ba_ref, q_ref, k_ref)
        gnv = gn_ref[...]
        dgn = jnp.zeros((1, DV), F32)
        for hd in range(H):
            ks, vs = slice(hd * DK, (hd + 1) * DK), slice(hd * DV, (hd + 1) * DV)
            qh, kih, kth = q_dec[:, ks], k_inv[:, ks], k_tail[:, ks]
            qb, kib, ktb = qh.astype(BF16), kih.astype(BF16), kth.astype(BF16)
            vb = v_ref[:, vs].astype(BF16)
            o = o_ref[:, vs]
            gg = g_ref[:, vs]
            dyh = dy_ref[:, vs]
            r = lax.rsqrt(jnp.mean(o * o, axis=-1, keepdims=True) + EPS)
            xh = o * r
            sg = _sigmoid(gg)
            dgate = dyh * (xh * gnv) * (sg * (1.0 + gg * (1.0 - sg)))
            don = dyh * (gg * sg)
            dgn = dgn + jnp.sum(don * xh, axis=0, keepdims=True)
            dxh = don * gnv
            d_o = (r * (dxh - xh * jnp.mean(dxh * xh, axis=-1, keepdims=True))).astype(BF16)
            pb = jnp.where(tril, _dot_nt(qb, kib), 0.0).astype(BF16)
            dpb = jnp.where(tril, _dot_nt(d_o, vb), 0.0).astype(BF16)
            gt = gstate[hd]
            gtb = gt.astype(BF16)
            st = st_ref[hd]
            dv = _dot_tn(pb, d_o) + _dot_nt(ktb, gtb)
            dq_dec = _dot(dpb, kib) + _dot(d_o, st.astype(BF16))
            dk_inv = _dot_tn(dpb, qb)
            dk_tail = _dot(vb, gtb)
            ddec = jnp.sum(gt * st, axis=0, keepdims=True)
            gstate[hd] = _dot_tn(d_o, qb) + dec[:, ks] * gt
            du_ref[:, ks] = (dq_dec * QK_SCALE * e_b[:, ks]).astype(BF16)
            du_ref[:, H * DK + hd * DK:H * DK + (hd + 1) * DK] = (dk_inv * e_nb[:, ks] + dk_tail * e_tb[:, ks]).astype(BF16)
            du_ref[:, 2 * H * DK + hd * DV:2 * H * DK + (hd + 1) * DV] = dv.astype(BF16)
            du_ref[:, 2 * H * DK + H * DV + hd * DV:2 * H * DK + H * DV + (hd + 1) * DV] = dgate.astype(BF16)
            db_scr[:, ks] = dq_dec * qh - dk_inv * kih - dk_tail * kth
            dbl_scr[:, ks] = jnp.sum(dk_tail * kth, axis=0, keepdims=True) + ddec * dec[:, ks]
        dgn_ref[...] += dgn
        row = lax.broadcasted_iota(jnp.int32, (CH, CH), 0)
        col = lax.broadcasted_iota(jnp.int32, (CH, CH), 1)
        dla = _tri_dot(jnp.where(row <= col, 1.0, 0.0).astype(BF16), db_scr[...]) + dbl_scr[...]
        dz = dla * (1.0 / GATE_NORM) * _sigmoid(-z)
        dzb = dz.astype(BF16)
        du_ref[:, GLA_DU - RANK_PAD:] = _dot_nt(dzb, wa_ref[...]).astype(BF16)
        dwa_ref[...] += _dot_tn(lr_ref[...].astype(BF16), dzb)
        dba_ref[...] += jnp.sum(dz, axis=0, keepdims=True)

    full = lambda shape: pl.BlockSpec(shape, lambda n: (0,) * len(shape))
    body, extra = _ordered_after(body, 11, after)
    return pl.pallas_call(
        body, grid=(NCH,), name="gla_bwd",
        in_specs=_gla_in_specs(rev) + [wide, pl.BlockSpec((None, H, DV, DK), lambda n: (rev(n), 0, 0, 0)),
                                       pl.BlockSpec((CH, H * DV), lambda n: (rev(n), 1)),
                                       full((RANK_PAD, H * DK)), full((1, H * DK)), full((1, DV))] + extra,
        out_specs=[pl.BlockSpec((CH, GLA_DU), lambda n: (rev(n), 0)), full((RANK_PAD, H * DK)), full((1, H * DK)),
                   full((1, DV))],
        out_shape=[jax.ShapeDtypeStruct((S, GLA_DU), BF16), jax.ShapeDtypeStruct((RANK_PAD, H * DK), F32),
                   jax.ShapeDtypeStruct((1, H * DK), F32), jax.ShapeDtypeStruct((1, DV), F32)],
        scratch_shapes=[pltpu.VMEM((H, DV, DK), F32), pltpu.VMEM((CH, H * DK), F32), pltpu.VMEM((1, H * DK), F32)],
        compiler_params=_params(32, ("arbitrary",)),
    )(u, u, u, u, u, o_arr, states, dy, wa_b, ba, gn, *after)


def _ffn_dw_in(n, dgu, tag, after=()):
    return _tn_matmul(n, dgu, pl.BlockSpec((S, 512), lambda s, m: (0, m)),
                      pl.BlockSpec((None, S, WIN_SHARD), lambda s, m: (s // (NDEV // 2), 0, s % (NDEV // 2))),
                      jax.ShapeDtypeStruct((D, 2 * FF), BF16), pl.BlockSpec((512, WIN_SHARD), lambda s, m: (m, s)),
                      (NDEV, D // 512), tag + "_dw_in", 32, after)


def _ffn_dw_out(act, df, tag, after=()):
    return _tn_matmul(act, df, pl.BlockSpec((S, 512), lambda m: (0, m)), pl.BlockSpec((S, D), lambda m: (0, 0)),
                      jax.ShapeDtypeStruct((FF, D), BF16), pl.BlockSpec((512, D), lambda m: (m, 0)),
                      (FF // 512,), tag + "_dw_out", 40, after)


def _fwd_ffn1(x, w, after=()):
    return _ffn_fwd(x, w["ffn1_norm"], w["ffn1_w_in"], w["ffn1_w_out"], "ffn1_fwd", after)


def _fwd_mixer(h1, w, after=()):
    u, n2 = _norm_matmul(h1, w["mix_norm"], w["w_in_mix"], "mix_in", after)
    y_pool, pooled = _pool_fwd(u, w["w_pool"], w["pool_scale"])
    y_gla, o_gla, states = _gla_fwd(u, w["w_alpha"], w["b_alpha"], w["gla_norm"])
    y = jnp.concatenate([y_pool, y_gla], axis=1)
    h2 = _matmul_residual(y, w["w_out_mix"], h1, "mix_out")
    return h2, dict(u=u, n2=n2, pooled=pooled, o_gla=o_gla, states=states, y=y)


def _fwd_ffn2_loss(h2, tgt, w, after=()):
    h3, n3, gu3 = _ffn_fwd(h2, w["ffn2_norm"], w["ffn2_w_in"], w["ffn2_w_out"], "ffn2_fwd", after)
    dh3, d_final, loss = _loss_head(h3, w["final_norm"], tgt)
    return dh3, d_final, loss, n3, gu3


def _backward(x, h1, h2, n1, gu1, sv, n3, gu3, dh3, w, at=lambda point, value: ()):
    g = {}
    dgu3, act3, df3, dh2, dh2b, g["ffn2_norm"] = _ffn_bwd_x(dh3, h2, w["ffn2_norm"], gu3, w["ffn2_w_in"], w["ffn2_w_out"], "ffn2_bwd")
    g["ffn2_w_out"] = _ffn_dw_out(act3, df3, "ffn2")
    g["ffn2_w_in"] = _ffn_dw_in(n3, dgu3, "ffn2", at("ffn2_w_out", g["ffn2_w_out"]))
    dy = _nt_matmul(dh2b, w["w_out_mix"], "mix_out_bwd", at("ffn2_w_in", g["ffn2_w_in"]))
    g["w_out_mix"] = _tn_matmul(sv["y"], dh2b, pl.BlockSpec((S, 512), lambda m: (0, m)), pl.BlockSpec((S, D), lambda m: (0, 0)),
                                jax.ShapeDtypeStruct((D, D), BF16), pl.BlockSpec((512, D), lambda m: (m, 0)), (D // 512,),
                                "mix_out_dw", 40)
    du_pool, g["w_pool"], g["pool_scale"] = _pool_bwd(dy, sv["pooled"], w["w_pool"], w["pool_scale"])
    du_gla, g["w_alpha"], g["b_alpha"], g["gla_norm"] = _gla_bwd(sv["u"], sv["o_gla"], sv["states"], dy, w["w_alpha"], w["b_alpha"],
                                                                 w["gla_norm"], at("pool_bwd", du_pool))
    du = jnp.concatenate([du_pool, du_gla], axis=1)
    g["w_in_mix"] = _tn_matmul(du, sv["n2"], pl.BlockSpec((S, 1408), lambda j, m: (0, j)), pl.BlockSpec((S, 512), lambda j, m: (0, m)),
                               jax.ShapeDtypeStruct((D_IN_PAD, D), BF16), pl.BlockSpec((1408, 512), lambda j, m: (j, m)),
                               (D_IN_PAD // 1408, D // 512), "mix_in_dw", 32)
    dh1, g["mix_norm"] = _matmul_normbwd(du, w["w_in_mix"], h1, w["mix_norm"], dh2, "mix_in_bwd", at("mixer_weights", g))
    dgu1, act1, df1, dx, _, g["ffn1_norm"] = _ffn_bwd_x(dh1, x, w["ffn1_norm"], gu1, w["ffn1_w_in"], w["ffn1_w_out"], "ffn1_bwd",
                                                      at("dh1", dh1))
    g["ffn1_w_out"] = _ffn_dw_out(act1, df1, "ffn1", at("small_grads", g))
    g["ffn1_w_in"] = _ffn_dw_in(n1, dgu1, "ffn1", at("ffn1_w_out", g["ffn1_w_out"]))
    return dx, g


def _local_step(x, tgt, w):
    h1, n1, gu1 = _fwd_ffn1(x, w)
    h2, sv = _fwd_mixer(h1, w)
    dh3, d_final, loss, n3, gu3 = _fwd_ffn2_loss(h2, tgt, w)
    dx, g = _backward(x, h1, h2, n1, gu1, sv, n3, gu3, dh3, w)
    return loss, dx, dict(final_norm=d_final, **g)


def _mesh_index():
    return 4 * lax.axis_index("x") + 2 * lax.axis_index("y") + lax.axis_index("c")


def _coords(p):
    return (p // 4, (p // 2) % 2, p % 2)


NCHIP = 4


def _place():
    return lax.axis_index("x"), lax.axis_index("y"), lax.axis_index("c")


def _rel_chip(x, y, rel):
    return ((1 - x) if rel & 1 else x, (1 - y) if rel & 2 else y)


def _dev_index(x, y, c):
    return 4 * x + 2 * y + c


def _cols(ref, p, width):
    return ref.at[:, pl.ds(pl.multiple_of(p * width, 128), width)]


def _sems(na, n):
    return [pltpu.SemaphoreType.DMA((na, n)), pltpu.SemaphoreType.DMA((na, n)), pltpu.SemaphoreType.DMA((na,))]


def _gather(items, name):
    arrays = [a for a, _ in items]
    kinds = [k for _, k in items]
    na = len(arrays)
    out_shape = [jax.ShapeDtypeStruct((NDEV,) + a.shape if k == "bcast" else (a.shape[0], NDEV * a.shape[1]), a.dtype)
                 for a, k in items]

    def body(*refs):
        ins, outs = refs[:na], refs[na:2 * na]
        send_sems, recv_sems, local_sems = refs[2 * na:]
        x, y, c = _place()
        sibling = (x, y, 1 - c)
        here, over_x, over_y, across = (x, y), (1 - x, y), (x, 1 - y), (1 - x, 1 - y)

        def half(ref, h):
            rows = ref.shape[0] // 2
            return ref.at[pl.ds(h * rows, rows), :]

        def slab(a, chip, core, h=None):
            ref = _slab(outs[a], kinds[a], _dev_index(*chip, core), ins[a].shape[1])
            return ref if h is None else half(ref, h)

        def copy(a, k, src, dst, to):
            return pltpu.make_async_remote_copy(src, dst, send_sems.at[a, k], recv_sems.at[a, k], device_id=to, device_id_type=MESH)

        sent = []

        def send(a, k, src, dst, to):
            sent.append(copy(a, k, src, dst, to))
            sent[-1].start()

        def arrived(a, k, chip, core, h=None):
            ref = slab(a, chip, core, h)
            copy(a, k, ref, ref, sibling).wait_recv()
            return ref

        local = [pltpu.make_async_copy(ins[a], slab(a, here, c), local_sems.at[a]) for a in range(na)]
        for cp in local:
            cp.start()
        for k, h, chip in ((1, 0, over_x), (4, 1, over_y), (2, 1, over_x), (5, 0, over_y)):
            for a in range(na):
                send(a, k, half(ins[a], h), slab(a, here, c, h), (*chip, c))
        for a in range(na):
            send(a, 0, ins[a], slab(a, here, c), sibling)
        for k, chip, h, onward, to, down in ((1, over_x, 0, 6, over_y, 7), (4, over_y, 1, 3, over_x, 10),
                                             (2, over_x, 1, None, None, 8), (5, over_y, 0, None, None, 9),
                                             (3, across, 1, None, None, 12), (6, across, 0, None, None, 11)):
            for a in range(na):
                ref = arrived(a, k, chip, c, h)
                if onward is not None:
                    send(a, onward, ref, ref, (*to, c))
                send(a, down, ref, ref, sibling)
        for a in range(na):
            arrived(a, 0, here, 1 - c)
        for k, chip, h in ((7, over_x, 0), (10, over_y, 1), (8, over_x, 1), (9, over_y, 0), (12, across, 1), (11, across, 0)):
            for a in range(na):
                arrived(a, k, chip, 1 - c, h)
        for cp in sent:
            cp.wait_send()
        for cp in local:
            cp.wait()

    return pl.pallas_call(body, name=name, in_specs=[ANY] * na, out_specs=[ANY] * na, out_shape=out_shape,
                          scratch_shapes=_sems(na, 13))(*arrays)


def _pair_exchange(items, name):
    arrays = [a for a, _ in items]
    kinds = [k for _, k in items]
    na = len(arrays)
    out_shape = [jax.ShapeDtypeStruct((NCHIP,) + (a.shape[1:] if k == "scatter" else (a.shape[0], a.shape[1] // NDEV)), a.dtype)
                 for a, k in items]

    def body(*refs):
        ins, outs = refs[:na], refs[na:2 * na]
        send_sems, recv_sems = refs[2 * na:]
        x, y, c = _place()
        copies = []
        for rel in range(NCHIP):
            p = _dev_index(*_rel_chip(x, y, rel), 1 - c)
            for a in range(na):
                src = ins[a].at[p] if kinds[a] == "scatter" else _cols(ins[a], p, ins[a].shape[1] // NDEV)
                copies.append(pltpu.make_async_remote_copy(src, outs[a].at[rel], send_sems.at[a, rel], recv_sems.at[a, rel],
                                                           device_id=(x, y, 1 - c), device_id_type=MESH))
                copies[-1].start()
        for cp in copies:
            cp.wait_send()
            cp.wait_recv()

    return pl.pallas_call(body, name=name, in_specs=[ANY] * na, out_specs=[ANY] * na, out_shape=out_shape,
                          scratch_shapes=_sems(na, NCHIP)[:2])(*arrays)


def _pair_add(own, kind, got, table, tr, name):
    _, rows, cols = got.shape
    if kind == "scatter":
        own_spec = pl.BlockSpec((None, tr, cols), lambda rel, i, t: (t[rel], i, 0))
    else:
        own_spec = pl.BlockSpec((tr, cols), lambda rel, i, t: (i, t[rel]))
    blk = pl.BlockSpec((None, tr, cols), lambda rel, i, t: (rel, i, 0))

    def body(t_ref, a_ref, b_ref, o_ref):
        o_ref[...] = (a_ref[...].astype(F32) + b_ref[...].astype(F32)).astype(o_ref.dtype)

    return pl.pallas_call(
        body, name=name, out_shape=jax.ShapeDtypeStruct(got.shape, got.dtype),
        grid_spec=pltpu.PrefetchScalarGridSpec(num_scalar_prefetch=1, grid=(NCHIP, rows // tr), in_specs=[own_spec, blk],
                                               out_specs=blk),
        compiler_params=_params(32),
    )(table, own, got)


def _exchange(items, name):
    arrays = [a for a, _ in items]
    kinds = [k for _, k in items]
    na = len(arrays)
    out_shape = [jax.ShapeDtypeStruct((NDEV,) + a.shape if k == "bcast" else a.shape, a.dtype) for a, k in items]

    def body(*refs):
        ins, outs = refs[:na], refs[na:2 * na]
        send_sems, recv_sems, local_sems = refs[2 * na:]
        x, y, c = _place()
        me = _dev_index(x, y, c)

        def mine(a):
            return 0 if kinds[a] == "chip" else me

        def src(a, p):
            return ins[a] if kinds[a] == "bcast" else ins[a].at[p]

        local = [pltpu.make_async_copy(src(a, mine(a)), outs[a].at[mine(a)], local_sems.at[a]) for a in range(na)]
        for cp in local:
            cp.start()
        waits = []
        for k in range(1, NDEV):
            to = (me + k) % NDEV
            frm = (me + NDEV - k) % NDEV
            for a in range(na):
                if kinds[a] == "chip":
                    if k >= NCHIP:
                        continue
                    send = pltpu.make_async_remote_copy(ins[a].at[k], outs[a].at[k], send_sems.at[a, k], recv_sems.at[a, k],
                                                        device_id=(*_rel_chip(x, y, k), c), device_id_type=MESH)
                    waits.append(send)
                else:
                    send = pltpu.make_async_remote_copy(src(a, to), outs[a].at[me], send_sems.at[a, k], recv_sems.at[a, k],
                                                        device_id=_coords(to), device_id_type=MESH)
                    waits.append(pltpu.make_async_remote_copy(src(a, to), outs[a].at[frm], send_sems.at[a, k],
                                                              recv_sems.at[a, k], device_id=_coords(to), device_id_type=MESH))
                send.start()
        for cp in waits:
            cp.wait_send()
            cp.wait_recv()
        for cp in local:
            cp.wait()

    return pl.pallas_call(body, name=name, in_specs=[ANY] * na, out_specs=[ANY] * na, out_shape=out_shape,
                          scratch_shapes=_sems(na, NDEV))(*arrays)


HBM = pl.BlockSpec(memory_space=pltpu.HBM)
SEM = pl.BlockSpec(memory_space=pltpu.SEMAPHORE)
DATAFLOW = pltpu.SideEffectType.DATAFLOW_SIDE_EFFECTING


def _pair_copies(kinds):
    def describe(srcs, lands, send_sems, recv_sems):
        x, y, c = _place()
        na = len(srcs)
        for rel in range(NCHIP):
            p = _dev_index(*_rel_chip(x, y, rel), 1 - c)
            for a in range(na):
                src = srcs[a].at[p] if kinds[a] == "scatter" else _cols(srcs[a], p, srcs[a].shape[1] // NDEV)
                cp = pltpu.make_async_remote_copy(src, lands[a].at[rel], send_sems.at[rel * na + a], recv_sems.at[rel * na + a],
                                                  device_id=(x, y, 1 - c), device_id_type=MESH)
                yield cp, cp
    return describe


def _chip_copies(srcs, lands, send_sems, recv_sems):
    x, y, c = _place()
    na = len(srcs)
    for rel in range(1, NCHIP):
        for a in range(na):
            i = (rel - 1) * na + a
            cp = pltpu.make_async_remote_copy(srcs[a].at[rel], lands[a].at[rel], send_sems.at[i], recv_sems.at[i],
                                              device_id=(*_rel_chip(x, y, rel), c), device_id_type=MESH)
            yield cp, cp


def _slab(ref, kind, s, width):
    return _cols(ref, s, width) if kind == "bcast_cols" else ref.at[s]


def _all_copies(srcs, lands, send_sems, recv_sems):
    x, y, c = _place()
    na = len(srcs)
    me = _dev_index(x, y, c)
    for a in range(na):
        yield pltpu.make_async_copy(srcs[a], lands[a].at[me], send_sems.at[a]), None
    for k in range(1, NDEV):
        to, frm = (me + k) % NDEV, (me + NDEV - k) % NDEV
        for a in range(na):
            i = k * na + a
            send = pltpu.make_async_remote_copy(srcs[a], lands[a].at[me], send_sems.at[i], recv_sems.at[i],
                                                device_id=_coords(to), device_id_type=MESH)
            arrival = pltpu.make_async_remote_copy(srcs[a], lands[a].at[frm], send_sems.at[i], recv_sems.at[i],
                                                   device_id=_coords(to), device_id_type=MESH)
            yield send, arrival


def _gather_copies(kinds):
    def describe(srcs, lands, send_sems, recv_sems):
        x, y, c = _place()
        na = len(srcs)
        me = _dev_index(x, y, c)
        for a in range(na):
            yield pltpu.make_async_copy(srcs[a], _slab(lands[a], kinds[a], me, srcs[a].shape[-1]),
                                        send_sems.at[NCHIP * na + a]), None
        for rel in range(NCHIP):
            to = (x, y, 1 - c) if rel == 0 else (*_rel_chip(x, y, rel), c)
            for a in range(na):
                width = srcs[a].shape[-1]
                i = rel * na + a
                send = pltpu.make_async_remote_copy(srcs[a], _slab(lands[a], kinds[a], me, width), send_sems.at[i], recv_sems.at[i],
                                                    device_id=to, device_id_type=MESH)
                arrival = pltpu.make_async_remote_copy(srcs[a], _slab(lands[a], kinds[a], _dev_index(*to), width), send_sems.at[i],
                                                       recv_sems.at[i], device_id=to, device_id_type=MESH)
                yield send, arrival
    return describe


def _pass_to_sibling(arrays, kinds, widths, name):
    na = len(arrays)

    def body(*refs):
        bufs = refs[na:2 * na]
        send_sems, recv_sems = refs[2 * na:]
        x, y, c = _place()
        copies = []
        for rel in range(1, NCHIP):
            for a in range(na):
                mine = _slab(bufs[a], kinds[a], _dev_index(*_rel_chip(x, y, rel), c), widths[a])
                theirs = _slab(bufs[a], kinds[a], _dev_index(*_rel_chip(x, y, rel), 1 - c), widths[a])
                send = pltpu.make_async_remote_copy(mine, mine, send_sems.at[a, rel], recv_sems.at[a, rel],
                                                    device_id=(x, y, 1 - c), device_id_type=MESH)
                send.start()
                copies.append((send, pltpu.make_async_remote_copy(theirs, theirs, send_sems.at[a, rel], recv_sems.at[a, rel],
                                                                  device_id=(x, y, 1 - c), device_id_type=MESH)))
        for send, arrival in copies:
            send.wait_send()
            arrival.wait_recv()

    return pl.pallas_call(body, name=name, in_specs=[ANY] * na, out_specs=[ANY] * na,
                          out_shape=[jax.ShapeDtypeStruct(a.shape, a.dtype) for a in arrays],
                          input_output_aliases={i: i for i in range(na)}, scratch_shapes=_sems(na, NCHIP)[:2])(*arrays)


def _start_copies(name, srcs, lands, describe, ncopies, after=()):
    arrays = list(srcs) + list(lands)
    ns, n, nin = len(srcs), len(arrays), len(arrays) + len(after)

    def body(*refs):
        for send, _ in describe(refs[:ns], refs[ns:n], refs[nin], refs[nin + 1]):
            send.start()
        refs[-1][...] = jnp.zeros_like(refs[-1])

    out = pl.pallas_call(
        body, name=name,
        out_shape=(pltpu.SemaphoreType.DMA((ncopies,)), pltpu.SemaphoreType.DMA((ncopies,)),
                   *[pltpu.HBM(a.shape, a.dtype) for a in arrays], jax.ShapeDtypeStruct((8, 128), F32)),
        in_specs=[HBM] * n + [ANY] * len(after), out_specs=(SEM, SEM, *[HBM] * n, pl.BlockSpec(memory_space=pltpu.VMEM)),
        input_output_aliases={i: 2 + i for i in range(n)},
        compiler_params=pltpu.CompilerParams(has_side_effects=DATAFLOW),
    )(*[pltpu.with_memory_space_constraint(a, pltpu.HBM) for a in arrays], *after)
    return out[0], out[1], list(out[2:2 + n]), out[-1]


def _wait_copies(name, send_sems, recv_sems, thru, ns, describe, after):
    n = len(thru)

    def body(*refs):
        for send, arrival in describe(refs[:ns], refs[ns:n], refs[n], refs[n + 1]):
            if arrival is None:
                send.wait()
            else:
                send.wait_send()
                arrival.wait_recv()

    out = pl.pallas_call(
        body, name=name, out_shape=tuple(pltpu.HBM(a.shape, a.dtype) for a in thru),
        in_specs=[HBM] * n + [SEM, SEM] + [ANY] * len(after), out_specs=tuple([HBM] * n),
        input_output_aliases={i: i for i in range(n)},
        compiler_params=pltpu.CompilerParams(has_side_effects=DATAFLOW),
    )(*thru, send_sems, recv_sems, *after)
    return list(out[:ns]), list(out[ns:])


def _adamw(parts, w, m, v, tr, name, tc=None):
    rows, cols = w.shape
    tc = cols if tc is None else tc
    nparts = len(parts)
    blk = pl.BlockSpec((tr, tc), lambda i, j: (i, j))

    def slab_spec(s):
        return pl.BlockSpec((None, tr, tc), lambda i, j: (s, i, j))

    def body(*refs):
        p_refs = refs[:nparts]
        w_ref, m_ref, v_ref, g_ref, d_ref, nm_ref, nv_ref = refs[nparts:]
        g = p_refs[0][...].astype(F32)
        for p_ref in p_refs[1:]:
            g = g + p_ref[...].astype(F32)
        nm = ADAM_B1 * m_ref[...] + (1.0 - ADAM_B1) * g
        nv = ADAM_B2 * v_ref[...] + (1.0 - ADAM_B2) * (g * g)
        m_hat = nm / (1.0 - ADAM_B1 ** ADAM_STEP)
        v_hat = nv / (1.0 - ADAM_B2 ** ADAM_STEP)
        g_ref[...] = g
        d_ref[...] = -ADAM_LR * (m_hat / (jnp.sqrt(v_hat) + ADAM_EPS) + ADAM_WD * w_ref[...])
        nm_ref[...] = nm
        nv_ref[...] = nv

    return pl.pallas_call(
        body, grid=(rows // tr, cols // tc), name=name,
        in_specs=[slab_spec(s) for _, s in parts] + [blk, blk, blk], out_specs=[blk] * 4,
        out_shape=[jax.ShapeDtypeStruct((rows, cols), F32)] * 4,
        compiler_params=_params(40),
    )(*[a for a, _ in parts], w, m, v)


def _pack_small(vals, extra=None):
    flat = [vals[n].reshape(-1).astype(F32) for n, _ in SMALL]
    tail = jnp.zeros((SMALL_ROWS * 128 - LOSS_AT,), F32)
    if extra is not None:
        tail = tail.at[0].set(extra)
    return jnp.concatenate(flat + [tail]).reshape(SMALL_ROWS, 128)


def _unpack_small(packed, like):
    flat, out, at = packed.reshape(-1), {}, 0
    for n, size in SMALL:
        out[n] = flat[at:at + size].reshape(like[n].shape)
        at += size
    return out, flat[LOSS_AT]


def kernel(x, ffn1_norm, ffn1_w_in, ffn1_w_out, mix_norm, w_in_mix, w_pool, pool_scale, w_alpha, b_alpha, gla_norm, w_out_mix, ffn2_norm, ffn2_w_in, ffn2_w_out, final_norm, loss_target, m_ffn1_norm, m_ffn1_w_in, m_ffn1_w_out, m_mix_norm, m_w_in_mix, m_w_pool, m_pool_scale, m_w_alpha, m_b_alpha, m_gla_norm, m_w_out_mix, m_ffn2_norm, m_ffn2_w_in, m_ffn2_w_out, m_final_norm, v_ffn1_norm, v_ffn1_w_in, v_ffn1_w_out, v_mix_norm, v_w_in_mix, v_w_pool, v_pool_scale, v_w_alpha, v_b_alpha, v_gla_norm, v_w_out_mix, v_ffn2_norm, v_ffn2_w_in, v_ffn2_w_out, v_final_norm):
    names = ["ffn1_norm", "ffn1_w_in", "ffn1_w_out", "mix_norm", "w_in_mix", "w_pool", "pool_scale", "w_alpha", "b_alpha",
             "gla_norm", "w_out_mix", "ffn2_norm", "ffn2_w_in", "ffn2_w_out", "final_norm"]
    p = dict(zip(names, [ffn1_norm, ffn1_w_in, ffn1_w_out, mix_norm, w_in_mix, w_pool, pool_scale, w_alpha, b_alpha,
                         gla_norm, w_out_mix, ffn2_norm, ffn2_w_in, ffn2_w_out, final_norm]))
    m = dict(zip(names, [m_ffn1_norm, m_ffn1_w_in, m_ffn1_w_out, m_mix_norm, m_w_in_mix, m_w_pool, m_pool_scale, m_w_alpha,
                         m_b_alpha, m_gla_norm, m_w_out_mix, m_ffn2_norm, m_ffn2_w_in, m_ffn2_w_out, m_final_norm]))
    v = dict(zip(names, [v_ffn1_norm, v_ffn1_w_in, v_ffn1_w_out, v_mix_norm, v_w_in_mix, v_w_pool, v_pool_scale, v_w_alpha,
                         v_b_alpha, v_gla_norm, v_w_out_mix, v_ffn2_norm, v_ffn2_w_in, v_ffn2_w_out, v_final_norm]))

    mx, my, mc = _place()
    me = _dev_index(mx, my, mc)
    table = jnp.stack([_dev_index(*_rel_chip(mx, my, rel), mc) for rel in range(NCHIP)]).astype(jnp.int32)

    def landing(shard, kind):
        shape = (shard.shape[0], NDEV * shard.shape[1]) if kind == "bcast_cols" else (NDEV,) + shard.shape
        return lax.empty(shape, shard.dtype)

    def gather_begin(items, tag, after):
        kinds = [kind for _, kind in items]
        copies = _gather_copies(kinds)
        s, r, thru, tok = _start_copies(tag + "_start", [a for a, _ in items], [landing(a, kind) for a, kind in items], copies,
                                        (NCHIP + 1) * len(items), after)
        return (s, r, thru, copies, kinds, [a.shape[-1] for a, _ in items], tag), tok

    def gather_end(state, after):
        s, r, thru, copies, kinds, widths, tag = state
        _, lands = _wait_copies(tag + "_wait", s, r, thru, len(kinds), copies, after)
        return _pass_to_sibling(lands, kinds, widths, tag + "_pass")

    def shard16(n):
        return p[n][0].astype(BF16)

    g_w1in, g_w1out = _gather([(shard16("ffn1_w_in"), "bcast_cols"), (shard16("ffn1_w_out"), "bcast")], "gather_ffn1")
    mix_state, tok_m = gather_begin([(jnp.transpose(p["w_in_mix"][0]).astype(BF16), "bcast"), (shard16("w_out_mix"), "bcast"),
                                     (p["w_pool"][0].reshape(H * 32, PG), "bcast"), (p["w_alpha"][0], "bcast")], "gather_mix",
                                    (g_w1out,))
    ffn2_state, tok_f = gather_begin([(shard16("ffn2_w_in"), "bcast_cols"), (shard16("ffn2_w_out"), "bcast")], "gather_ffn2",
                                     (tok_m,))
    full = {"ffn1_w_in": g_w1in, "ffn1_w_out": g_w1out.reshape(FF, D), "final_norm": final_norm.reshape(1, D)}
    for n in ("ffn1_norm", "mix_norm", "ffn2_norm", "pool_scale", "b_alpha", "gla_norm"):
        full[n] = p[n]

    xs, tgt = x[0], loss_target[0]
    h1, n1, gu1 = _fwd_ffn1(xs, full, after=(tok_m, tok_f))
    g_wmix, g_wo, g_wpool, g_walpha = gather_end(mix_state, (h1,))
    walpha = jnp.transpose(g_walpha, (1, 0, 2)).reshape(RANK, H * DK)
    full.update({
        "w_in_mix": jnp.pad(g_wmix.reshape(D_IN, D), ((0, D_IN_PAD - D_IN), (0, 0))),
        "w_out_mix": g_wo.reshape(D, D),
        "w_pool": jnp.transpose(g_wpool.reshape(NDEV, H, 32, PG), (1, 0, 2, 3)).reshape(H, PG, PG).astype(BF16),
        "w_alpha": jnp.pad(walpha, ((0, RANK_PAD - RANK), (0, 0))).astype(BF16),
    })
    h2, sv = _fwd_mixer(h1, full)
    g_w2in, g_w2out = gather_end(ffn2_state, (h2,))
    full.update({"ffn2_w_in": g_w2in, "ffn2_w_out": g_w2out.reshape(FF, D)})
    dh3, d_final, loss_part, n3, gu3 = _fwd_ffn2_loss(h2, tgt, full)


    def slab_shape(a, kind):
        return (NCHIP,) + (a.shape[1:] if kind == "scatter" else (a.shape[0], a.shape[1] // NDEV))

    def pair_add_all(own, got, tag):
        return [_pair_add(a, kind, got_a, table, tr, "%s_pair_add_%d" % (tag, i))
                for i, ((a, kind, tr), got_a) in enumerate(zip(own, got))]

    def reduce_begin(own, tag, after=()):
        kinds = [kind for _, kind, _ in own]
        copies = _pair_copies(kinds)
        s, r, thru, tok = _start_copies(tag + "_pair_start", [a for a, _, _ in own],
                                        [lax.empty(slab_shape(a, kind), a.dtype) for a, kind, _ in own], copies,
                                        NCHIP * len(own), after)
        return dict(own=own, copies=copies, s=s, r=r, thru=thru, tag=tag), tok

    def reduce_middle(st, after):
        own, tag = st["own"], st["tag"]
        sent, got = _wait_copies(tag + "_pair_wait", st["s"], st["r"], st["thru"], len(own), st["copies"], after)
        pre = pair_add_all([(a, kind, tr) for a, (_, kind, tr) in zip(sent, own)], got, tag)
        st["s"], st["r"], st["thru"], tok = _start_copies(tag + "_chip_start", pre, [lax.empty(a.shape, a.dtype) for a in pre],
                                                          _chip_copies, (NCHIP - 1) * len(pre))
        return tok

    def reduce_end(st, after):
        n = len(st["own"])
        pre, land = _wait_copies(st["tag"] + "_chip_wait", st["s"], st["r"], st["thru"], n, _chip_copies, after)
        return [[(a, 0)] + [(b, rel) for rel in range(1, NCHIP)] for a, b in zip(pre, land)]

    def w_in_item(a):
        return (a, "scatter_cols", 256)

    def w_out_item(a):
        return (a.reshape(NDEV, WOUT_SHARD, D), "scatter", WOUT_SHARD // 2)

    red, small = {}, {}

    def at(point, value):
        if point == "ffn2_w_out":
            red["w2out"], tok = reduce_begin([w_out_item(value)], "ffn2_w_out")
        elif point == "ffn2_w_in":
            tok_a = reduce_middle(red["w2out"], (value,))
            red["w2in"], tok = reduce_begin([w_in_item(value)], "ffn2_w_in", (tok_a,))
        elif point == "pool_bwd":
            tok = reduce_middle(red["w2in"], (value,))
        elif point == "mixer_weights":
            d_wmix8 = value["w_in_mix"][:D_IN].reshape(NDEV, MIX_SHARD, D)
            d_wpool8 = jnp.transpose(value["w_pool"].reshape(H, NDEV, 32, PG), (1, 0, 2, 3)).reshape(NDEV, H * 32, PG)
            d_walpha8 = jnp.transpose(value["w_alpha"][:RANK].reshape(RANK, NDEV, H * DK // NDEV), (1, 0, 2))
            red["mix"], tok = reduce_begin([(d_wmix8, "scatter", MIX_SHARD),
                                            (value["w_out_mix"].reshape(NDEV, D // NDEV, D), "scatter", D // NDEV),
                                            (d_wpool8, "scatter", H * 32), (d_walpha8, "scatter", RANK)], "mix")
        elif point == "dh1":
            tok = reduce_middle(red["mix"], (value,))
        elif point == "small_grads":
            packed = _pack_small(dict(final_norm=d_final, **value), loss_part[0, 0])
            small["s"], small["r"], small["thru"], tok = _start_copies(
                "gather_small_start", [packed], [lax.empty((NDEV,) + packed.shape, F32)], _all_copies, NDEV)
        elif point == "ffn1_w_out":
            red["w1out"], tok = reduce_begin([w_out_item(value)], "ffn1_w_out")
        return (tok,)

    dx, g = _backward(xs, h1, h2, n1, gu1, sv, n3, gu3, dh3, full, at)
    tok_a = reduce_middle(red["w1out"], (g["ffn1_w_in"],))
    red["w1in"], tok_b = reduce_begin([w_in_item(g["ffn1_w_in"])], "ffn1_w_in", (tok_a,))
    _, (r_small,) = _wait_copies("gather_small_wait", small["s"], small["r"], small["thru"], 1, _all_copies, (tok_b,))

    def upd(parts, n, shape2d, tr):
        res = _adamw(parts, p[n].reshape(shape2d), m[n].reshape(shape2d), v[n].reshape(shape2d), tr, "adamw_" + n)
        return [r.reshape(p[n].shape) for r in res]

    def transposed(a):
        return jnp.transpose(a[0])

    (p_w2out,) = reduce_end(red["w2out"], (r_small,))
    (p_w2in,) = reduce_end(red["w2in"], (r_small,))
    p_wmix, p_wo, p_wpool, p_walpha = reduce_end(red["mix"], (r_small,))
    out = {"ffn2_w_in": upd(p_w2in, "ffn2_w_in", (D, WIN_SHARD), 128)}
    tok_c = reduce_middle(red["w1in"], (out["ffn2_w_in"][3],))
    out.update({
        "ffn2_w_out": upd(p_w2out, "ffn2_w_out", (WOUT_SHARD, D), 64),
        "w_out_mix": upd(p_wo, "w_out_mix", (D // NDEV, D), 64),
        "w_pool": upd(p_wpool, "w_pool", (H * 32, PG), H * 32),
        "w_alpha": upd(p_walpha, "w_alpha", (RANK, H * DK // NDEV), RANK),
    })
    out["w_in_mix"] = [jnp.transpose(r)[None] for r in
                       _adamw(p_wmix, transposed(p["w_in_mix"]), transposed(m["w_in_mix"]), transposed(v["w_in_mix"]),
                              MIX_SHARD, "adamw_w_in_mix", tc=512)]
    small_res = _adamw([(r_small, s) for s in range(NDEV)], _pack_small(p), _pack_small(m), _pack_small(v), SMALL_ROWS,
                       "adamw_small")
    (p_w1out,) = reduce_end(red["w1out"], (small_res[0], out["w_in_mix"][3], out["ffn2_w_out"][3], tok_c))
    out["ffn1_w_out"] = upd(p_w1out, "ffn1_w_out", (WOUT_SHARD, D), 64)
    (p_w1in,) = reduce_end(red["w1in"], (out["ffn1_w_out"][3],))
    out["ffn1_w_in"] = upd(p_w1in, "ffn1_w_in", (D, WIN_SHARD), 128)
    unpacked = [_unpack_small(r, p) for r in small_res]
    loss = unpacked[0][1]
    for n, _ in SMALL:
        out[n] = [u[0][n] for u in unpacked]

    return (loss, dx.reshape(1, S, D), *[out[n][0] for n in names], *[out[n][1] for n in names],
            *[out[n][2] for n in names], *[out[n][3] for n in names])
```

```python
import functools

import jax
import jax.numpy as jnp
from jax import lax
from jax.experimental import pallas as pl
from jax.experimental.pallas import tpu as pltpu

F32, BF16 = jnp.float32, jnp.bfloat16
MESH = pl.DeviceIdType.MESH
ANY = pl.BlockSpec(memory_space=pl.ANY)

NDEV = 8
S = 2048
D = 2048
FF = 5632
WIN_SHARD = 2 * FF // NDEV
WOUT_SHARD = FF // NDEV
D_POOL = 1024
PG = 256
POOL_WINDOWS = (2, 4, 8, 16)
H = 4
DK = 128
DV = 256
CH = 64
NCH = S // CH
RANK = 16
RANK_PAD = 128
D_IN = 4112
D_IN_PAD = 4224
MIX_SHARD = D_IN // NDEV
O_Q, O_K, O_V, O_G, O_R = 1024, 1536, 2048, 3072, 4096
GATE_NORM = 16.0
QK_SCALE = DK ** -0.5
EPS = 1e-6
ADAM_LR, ADAM_B1, ADAM_B2, ADAM_EPS, ADAM_WD, ADAM_STEP = 0.001, 0.9, 0.999, 1e-08, 0.01, 10
V7X_VMEM_BYTES = 64 << 20

SMALL = (("ffn1_norm", 2048), ("mix_norm", 2048), ("ffn2_norm", 2048), ("final_norm", 2048),
         ("pool_scale", 1024), ("b_alpha", 512), ("gla_norm", 256))
SMALL_ROWS = 80
LOSS_AT = sum(n for _, n in SMALL)


def _params(vmem_mb, sem=None):
    return pltpu.CompilerParams(dimension_semantics=sem, vmem_limit_bytes=min(vmem_mb << 20, V7X_VMEM_BYTES - (4 << 20)))


def _dot(a, b):
    return jnp.dot(a, b, preferred_element_type=F32)


def _dot_nt(a, b):
    return lax.dot_general(a, b, (((1,), (1,)), ((), ())), preferred_element_type=F32)


def _dot_tn(a, b):
    return lax.dot_general(a, b, (((0,), (0,)), ((), ())), preferred_element_type=F32)


def _sigmoid(x):
    return 1.0 / (1.0 + jnp.exp(-x))


def _log_sigmoid(x):
    return jnp.minimum(x, 0.0) - jnp.log(1.0 + jnp.exp(-jnp.abs(x)))


def _rms(x, g):
    r = lax.rsqrt(jnp.mean(x * x, axis=-1, keepdims=True) + EPS)
    return x * r * g


def _rms_bwd(dn, x, g):
    r = lax.rsqrt(jnp.mean(x * x, axis=-1, keepdims=True) + EPS)
    xh = x * r
    dxh = dn * g
    dx = r * (dxh - xh * jnp.mean(dxh * xh, axis=-1, keepdims=True))
    return dx, jnp.sum(dn * xh, axis=0, keepdims=True)


ROWS = 64


def _row_loop(total, fn, init=0):
    def step(t, carry):
        return fn(pl.ds(pl.multiple_of(t * ROWS, ROWS), ROWS), carry)
    return lax.fori_loop(0, total // ROWS, step, init)


def _split3(x):
    hi = x.astype(BF16)
    r1 = x - hi.astype(F32)
    mid = r1.astype(BF16)
    lo = (r1 - mid.astype(F32)).astype(BF16)
    return hi, mid, lo


def _tri_dot(tri_b, x):
    hi, mid, lo = _split3(x)
    return (_dot(tri_b, lo) + _dot(tri_b, mid)) + _dot(tri_b, hi)


FFN_TS, FFN_TF = 512, 512


def _ffn_specs():
    wg = pl.BlockSpec((D, FFN_TF), lambda i, j: (0, j))
    wu = pl.BlockSpec((D, FFN_TF), lambda i, j: (0, FF // FFN_TF + j))
    wo = pl.BlockSpec((FFN_TF, D), lambda i, j: (j, 0))
    row = pl.BlockSpec((FFN_TS, D), lambda i, j: (i, 0))
    vec = pl.BlockSpec((1, D), lambda i, j: (0, 0))
    gu = pl.BlockSpec((2, FFN_TS, FFN_TF), lambda i, j: (0, i, j))
    return wg, wu, wo, row, vec, gu


def _ordered_after(body, n_in, after):
    def wrapped(*refs):
        return body(*refs[:n_in], *refs[n_in + len(after):])
    return wrapped, [ANY] * len(after)


def _ffn_fwd(h, g, w_in8, w_out, name, after=()):
    nj = FF // FFN_TF
    wg, wu, wo, row, vec, gu = _ffn_specs()

    def body(h_ref, g_ref, wg_ref, wu_ref, wo_ref, ho_ref, n_ref, gu_ref, acc_ref):
        j = pl.program_id(1)

        @pl.when(j == 0)
        def _():
            def norm(rows, c):
                n_ref[rows, :] = _rms(h_ref[rows, :], g_ref[...]).astype(BF16)
                return c
            _row_loop(FFN_TS, norm)
            acc_ref[...] = jnp.zeros_like(acc_ref)

        n = n_ref[...]
        gate = _dot(n, wg_ref[...])
        up = _dot(n, wu_ref[...])
        gu_ref[0] = gate.astype(BF16)
        gu_ref[1] = up.astype(BF16)
        a = (gate * _sigmoid(gate)) * up
        acc_ref[...] += _dot(a.astype(BF16), wo_ref[...])

        @pl.when(j == nj - 1)
        def _():
            def residual(rows, c):
                ho_ref[rows, :] = h_ref[rows, :] + 0.5 * acc_ref[rows, :]
                return c
            _row_loop(FFN_TS, residual)

    body, extra = _ordered_after(body, 5, after)
    return pl.pallas_call(
        body, grid=(S // FFN_TS, nj), name=name,
        in_specs=[row, vec, wg, wu, wo] + extra, out_specs=[row, row, gu],
        out_shape=[jax.ShapeDtypeStruct((S, D), F32), jax.ShapeDtypeStruct((S, D), BF16),
                   jax.ShapeDtypeStruct((2, S, FF), BF16)],
        scratch_shapes=[pltpu.VMEM((FFN_TS, D), F32)],
        compiler_params=_params(56, ("arbitrary", "arbitrary")),
    )(h, g, w_in8, w_in8, w_out, *after)


def _ffn_bwd_x(dhp, h, g, gu_arr, w_in8, w_out, name, after=()):
    ni, nj = S // FFN_TS, FF // FFN_TF
    wg, wu, wo, row, vec, gu = _ffn_specs()
    act = pl.BlockSpec((FFN_TS, FFN_TF), lambda i, j: (i, j))

    def body(dhp_ref, h_ref, g_ref, gu_ref, wg_ref, wu_ref, wo_ref,
             dgu_ref, a_ref, df_ref, dh_ref, dhb_ref, dg_ref, acc_ref):
        i, j = pl.program_id(0), pl.program_id(1)

        @pl.when(j == 0)
        def _():
            def half(rows, c):
                df_ref[rows, :] = (0.5 * dhp_ref[rows, :]).astype(BF16)
                return c
            _row_loop(FFN_TS, half)
            acc_ref[...] = jnp.zeros_like(acc_ref)

        gate = gu_ref[0].astype(F32)
        up = gu_ref[1].astype(F32)
        da = _dot_nt(df_ref[...], wo_ref[...])
        sg = _sigmoid(gate)
        silu = gate * sg
        dgate = (da * up * (sg * (1.0 + gate * (1.0 - sg)))).astype(BF16)
        dup = (da * silu).astype(BF16)
        a_ref[...] = (silu * up).astype(BF16)
        dgu_ref[0] = dgate
        dgu_ref[1] = dup
        acc_ref[...] += _dot_nt(dgate, wg_ref[...]) + _dot_nt(dup, wu_ref[...])

        @pl.when(j == nj - 1)
        def _():
            def norm_bwd(rows, dg):
                dx, dg_rows = _rms_bwd(acc_ref[rows, :], h_ref[rows, :], g_ref[...])
                dh = dhp_ref[rows, :] + dx
                dh_ref[rows, :] = dh
                dhb_ref[rows, :] = dh.astype(BF16)
                return dg + dg_rows
            dg = _row_loop(FFN_TS, norm_bwd, jnp.zeros((1, D), F32))

            @pl.when(i == 0)
            def _():
                dg_ref[...] = dg

            @pl.when(i > 0)
            def _():
                dg_ref[...] += dg

    body, extra = _ordered_after(body, 7, after)
    return pl.pallas_call(
        body, grid=(ni, nj), name=name,
        in_specs=[row, row, vec, gu, wg, wu, wo] + extra,
        out_specs=[gu, act, row, row, row, vec],
        out_shape=[jax.ShapeDtypeStruct((2, S, FF), BF16), jax.ShapeDtypeStruct((S, FF), BF16),
                   jax.ShapeDtypeStruct((S, D), BF16), jax.ShapeDtypeStruct((S, D), F32),
                   jax.ShapeDtypeStruct((S, D), BF16), jax.ShapeDtypeStruct((1, D), F32)],
        scratch_shapes=[pltpu.VMEM((FFN_TS, D), F32)],
        compiler_params=_params(58, ("arbitrary", "arbitrary")),
    )(dhp, h, g, gu_arr, w_in8, w_in8, w_out, *after)


def _ffn_bwd_gu(dhp, gu_arr, w_out, name, after=()):
    _, _, wo, row, _, gu = _ffn_specs()
    act = pl.BlockSpec((FFN_TS, FFN_TF), lambda i, j: (i, j))

    def body(dhp_ref, gu_ref, wo_ref, dgu_ref, a_ref, df_ref):
        @pl.when(pl.program_id(1) == 0)
        def _():
            def half(rows, c):
                df_ref[rows, :] = (0.5 * dhp_ref[rows, :]).astype(BF16)
                return c
            _row_loop(FFN_TS, half)

        gate = gu_ref[0].astype(F32)
        up = gu_ref[1].astype(F32)
        da = _dot_nt(df_ref[...], wo_ref[...])
        sg = _sigmoid(gate)
        silu = gate * sg
        dgu_ref[0] = (da * up * (sg * (1.0 + gate * (1.0 - sg)))).astype(BF16)
        dgu_ref[1] = (da * silu).astype(BF16)
        a_ref[...] = (silu * up).astype(BF16)

    body, extra = _ordered_after(body, 3, after)
    return pl.pallas_call(
        body, grid=(S // FFN_TS, FF // FFN_TF), name=name,
        in_specs=[row, gu, wo] + extra, out_specs=[gu, act, row],
        out_shape=[jax.ShapeDtypeStruct((2, S, FF), BF16), jax.ShapeDtypeStruct((S, FF), BF16),
                   jax.ShapeDtypeStruct((S, D), BF16)],
        compiler_params=_params(40, ("arbitrary", "arbitrary")),
    )(dhp, gu_arr, w_out, *after)


def _ffn_bwd_dx(dgu, dhp, h, g, w_in, first_tile, ntiles, name, after=(), into=None):
    nj = FF // FFN_TF
    wg, wu, _, _, vec, _ = _ffn_specs()
    row_in = pl.BlockSpec((FFN_TS, D), lambda i, j: (first_tile + i, 0))
    dgu_spec = pl.BlockSpec((2, FFN_TS, FFN_TF), lambda i, j: (0, first_tile + i, j))
    after = tuple(after) + (() if into is None else (into,))

    def body(dgu_ref, dhp_ref, h_ref, g_ref, wg_ref, wu_ref, dh_ref, dg_ref, acc_ref):
        i, j = pl.program_id(0), pl.program_id(1)

        @pl.when(j == 0)
        def _():
            acc_ref[...] = jnp.zeros_like(acc_ref)

        acc_ref[...] += _dot_nt(dgu_ref[0], wg_ref[...]) + _dot_nt(dgu_ref[1], wu_ref[...])

        @pl.when(j == nj - 1)
        def _():
            def norm_bwd(rows, dg):
                dx, dg_rows = _rms_bwd(acc_ref[rows, :], h_ref[rows, :], g_ref[...])
                dh_ref[rows, :] = dhp_ref[rows, :] + dx
                return dg + dg_rows
            dg = _row_loop(FFN_TS, norm_bwd, jnp.zeros((1, D), F32))

            @pl.when(i == 0)
            def _():
                dg_ref[...] = dg

            @pl.when(i > 0)
            def _():
                dg_ref[...] += dg

    body, extra = _ordered_after(body, 6, after)
    return pl.pallas_call(
        body, grid=(ntiles, nj), name=name,
        in_specs=[dgu_spec, row_in, row_in, vec, wg, wu] + extra, out_specs=[row_in, vec],
        out_shape=[jax.ShapeDtypeStruct((S, D), F32), jax.ShapeDtypeStruct((1, D), F32)],
        input_output_aliases={} if into is None else {6 + len(after) - 1: 0},
        scratch_shapes=[pltpu.VMEM((FFN_TS, D), F32)],
        compiler_params=_params(48, ("arbitrary", "arbitrary")),
    )(dgu, dhp, h, g, w_in, w_in, *after)


def _tn_matmul(a, b, a_spec, b_spec, out_shape, out_spec, grid, name, vmem_mb, after=()):
    def body(a_ref, b_ref, o_ref):
        o_ref[...] = _dot_tn(a_ref[...], b_ref[...]).astype(o_ref.dtype)

    body, extra = _ordered_after(body, 2, after)
    return pl.pallas_call(body, grid=grid, name=name, in_specs=[a_spec, b_spec] + extra, out_specs=out_spec,
                          out_shape=out_shape, compiler_params=_params(vmem_mb))(a, b, *after)


def _norm_matmul(h, g, w, name, after=(), ts=512, tn=1408):
    n_out = w.shape[0]

    def body(h_ref, g_ref, w_ref, u_ref, n_ref):
        @pl.when(pl.program_id(1) == 0)
        def _():
            def norm(rows, c):
                n_ref[rows, :] = _rms(h_ref[rows, :], g_ref[...]).astype(BF16)
                return c
            _row_loop(ts, norm)

        u_ref[...] = _dot_nt(n_ref[...], w_ref[...])

    body, extra = _ordered_after(body, 3, after)
    return pl.pallas_call(
        body, grid=(S // ts, n_out // tn), name=name,
        in_specs=[pl.BlockSpec((ts, D), lambda i, j: (i, 0)), pl.BlockSpec((1, D), lambda i, j: (0, 0)),
                  pl.BlockSpec((tn, D), lambda i, j: (j, 0))] + extra,
        out_specs=[pl.BlockSpec((ts, tn), lambda i, j: (i, j)), pl.BlockSpec((ts, D), lambda i, j: (i, 0))],
        out_shape=[jax.ShapeDtypeStruct((S, n_out), F32), jax.ShapeDtypeStruct((S, D), BF16)],
        compiler_params=_params(48, ("arbitrary", "arbitrary")),
    )(h, g, w, *after)


def _matmul_residual(a, w, res, name, ts=512, tn=1024):
    k, n_out = w.shape

    def body(a_ref, w_ref, r_ref, o_ref):
        o_ref[...] = r_ref[...] + _dot(a_ref[...], w_ref[...])

    return pl.pallas_call(
        body, grid=(S // ts, n_out // tn), name=name,
        in_specs=[pl.BlockSpec((ts, k), lambda i, j: (i, 0)), pl.BlockSpec((k, tn), lambda i, j: (0, j)),
                  pl.BlockSpec((ts, tn), lambda i, j: (i, j))],
        out_specs=pl.BlockSpec((ts, tn), lambda i, j: (i, j)),
        out_shape=jax.ShapeDtypeStruct((S, n_out), F32),
        compiler_params=_params(40),
    )(a, w, res)


def _nt_matmul(a, w, name, after=(), ts=512, tn=1024):
    n_out, k = w.shape

    def body(a_ref, w_ref, o_ref):
        o_ref[...] = _dot_nt(a_ref[...], w_ref[...])

    body, extra = _ordered_after(body, 2, after)
    return pl.pallas_call(
        body, grid=(S // ts, n_out // tn), name=name,
        in_specs=[pl.BlockSpec((ts, k), lambda i, j: (i, 0)), pl.BlockSpec((tn, k), lambda i, j: (j, 0))] + extra,
        out_specs=pl.BlockSpec((ts, tn), lambda i, j: (i, j)),
        out_shape=jax.ShapeDtypeStruct((S, n_out), F32),
        compiler_params=_params(40),
    )(a, w, *after)


def _matmul_normbwd(du, w, h, g, dres, name, after=(), ts=512, tn=1408):
    n_in = w.shape[0]
    nj = n_in // tn
    row = pl.BlockSpec((ts, D), lambda i, j: (i, 0))
    vec = pl.BlockSpec((1, D), lambda i, j: (0, 0))

    def body(du_ref, w_ref, h_ref, g_ref, dres_ref, dh_ref, dg_ref, acc_ref):
        i, j = pl.program_id(0), pl.program_id(1)

        @pl.when(j == 0)
        def _():
            acc_ref[...] = jnp.zeros_like(acc_ref)

        acc_ref[...] += _dot(du_ref[...], w_ref[...])

        @pl.when(j == nj - 1)
        def _():
            def norm_bwd(rows, dg):
                dx, dg_rows = _rms_bwd(acc_ref[rows, :], h_ref[rows, :], g_ref[...])
                dh_ref[rows, :] = dres_ref[rows, :] + dx
                return dg + dg_rows
            dg = _row_loop(ts, norm_bwd, jnp.zeros((1, D), F32))

            @pl.when(i == 0)
            def _():
                dg_ref[...] = dg

            @pl.when(i > 0)
            def _():
                dg_ref[...] += dg

    body, extra = _ordered_after(body, 5, after)
    return pl.pallas_call(
        body, grid=(S // ts, nj), name=name,
        in_specs=[pl.BlockSpec((ts, tn), lambda i, j: (i, j)), pl.BlockSpec((tn, D), lambda i, j: (j, 0)), row, vec, row] + extra,
        out_specs=[row, vec],
        out_shape=[jax.ShapeDtypeStruct((S, D), F32), jax.ShapeDtypeStruct((1, D), F32)],
        scratch_shapes=[pltpu.VMEM((ts, D), F32)],
        compiler_params=_params(52, ("arbitrary", "arbitrary")),
    )(du, w, h, g, dres, *after)


def _loss_head(h, g, tgt, ts=256):
    row = pl.BlockSpec((ts, D), lambda i: (i, 0))
    vec = pl.BlockSpec((1, D), lambda i: (0, 0))

    def body(h_ref, g_ref, t_ref, dh_ref, dg_ref, loss_ref):
        i = pl.program_id(0)

        def rows_fn(rows, carry):
            dg, part = carry
            x = h_ref[rows, :]
            gv = g_ref[...]
            err = _rms(x, gv) - t_ref[rows, :]
            part = part + 0.5 * jnp.sum(jnp.mean(err * err, axis=-1, keepdims=True), axis=0, keepdims=True)
            dx, dg_rows = _rms_bwd(err * (1.0 / D), x, gv)
            dh_ref[rows, :] = dx
            return dg + dg_rows, part
        dg, part = _row_loop(ts, rows_fn, (jnp.zeros((1, D), F32), jnp.zeros((1, 1), F32)))

        @pl.when(i == 0)
        def _():
            dg_ref[...] = dg
            loss_ref[...] = jnp.broadcast_to(part, loss_ref.shape)

        @pl.when(i > 0)
        def _():
            dg_ref[...] += dg
            loss_ref[...] += jnp.broadcast_to(part, loss_ref.shape)

    return pl.pallas_call(
        body, grid=(S // ts,), name="loss_head",
        in_specs=[row, vec, row], out_specs=[row, vec, pl.BlockSpec((1, 128), lambda i: (0, 0))],
        out_shape=[jax.ShapeDtypeStruct((S, D), F32), jax.ShapeDtypeStruct((1, D), F32),
                   jax.ShapeDtypeStruct((1, 128), F32)],
        compiler_params=_params(40, ("arbitrary",)),
    )(h, g, tgt)


def _pool_specs():
    blk = pl.BlockSpec((S, PG), lambda gi: (0, gi))
    wp = pl.BlockSpec((None, PG, PG), lambda gi: (gi, 0, 0))
    sc = pl.BlockSpec((1, PG), lambda gi: (0, gi))
    return blk, wp, sc


def _pool_fwd(u, wp_b, scale):
    blk, wp, sc = _pool_specs()

    def body(u_ref, wp_ref, sc_ref, y_ref, pooled_ref):
        win = 2 << pl.program_id(0)
        row = lax.broadcasted_iota(jnp.int32, (S, PG), 0)
        x = u_ref[...]
        s = x
        for k in (1, 2, 4, 8):
            s = s + jnp.where((row >= k) & (k < win), pltpu.roll(s, k, 0), 0.0)
        cnt = jnp.minimum(row + 1, win).astype(F32)
        pooled = (s / cnt - x).astype(BF16)
        pooled_ref[...] = pooled
        y_ref[...] = (_dot(pooled, wp_ref[...]) * sc_ref[...]).astype(BF16)

    return pl.pallas_call(
        body, grid=(len(POOL_WINDOWS),), name="pool_fwd", in_specs=[blk, wp, sc], out_specs=[blk, blk],
        out_shape=[jax.ShapeDtypeStruct((S, D_POOL), BF16), jax.ShapeDtypeStruct((S, D_POOL), BF16)],
        compiler_params=_params(40),
    )(u, wp_b, scale)


def _pool_bwd(dy, pooled, wp_b, scale):
    blk, wp, sc = _pool_specs()

    def body(dy_ref, p_ref, wp_ref, sc_ref, du_ref, dwp_ref, dsc_ref):
        win = 2 << pl.program_id(0)
        row = lax.broadcasted_iota(jnp.int32, (S, PG), 0)
        dyv = dy_ref[...]
        pooled = p_ref[...]
        w = wp_ref[...]
        dsc_ref[...] = jnp.sum(dyv * _dot(pooled, w), axis=0, keepdims=True)
        dz = (dyv * sc_ref[...]).astype(BF16)
        dwp_ref[...] = _dot_tn(pooled, dz)
        dpooled = _dot_nt(dz, w)
        cnt = jnp.minimum(row + 1, win).astype(F32)
        fs = dpooled / cnt
        for k in (1, 2, 4, 8):
            fs = fs + jnp.where((row < S - k) & (k < win), pltpu.roll(fs, S - k, 0), 0.0)
        du_ref[...] = (fs - dpooled).astype(BF16)

    return pl.pallas_call(
        body, grid=(len(POOL_WINDOWS),), name="pool_bwd", in_specs=[blk, blk, wp, sc], out_specs=[blk, wp, sc],
        out_shape=[jax.ShapeDtypeStruct((S, D_POOL), BF16), jax.ShapeDtypeStruct((len(POOL_WINDOWS), PG, PG), F32),
                   jax.ShapeDtypeStruct((1, D_POOL), F32)],
        compiler_params=_params(40),
    )(dy, pooled, wp_b, scale)


def _gla_in_specs(chunk_of):
    def at(width, col):
        return pl.BlockSpec((CH, width), lambda n: (chunk_of(n), col))
    return [at(H * DK, O_Q // (H * DK)), at(H * DK, O_K // (H * DK)), at(H * DV, O_V // (H * DV)),
            at(H * DV, O_G // (H * DV)), at(RANK_PAD, O_R // RANK_PAD)]


def _gla_decay_terms(lr_ref, wa_ref, ba_ref, q_ref, k_ref):
    row = lax.broadcasted_iota(jnp.int32, (CH, CH), 0)
    col = lax.broadcasted_iota(jnp.int32, (CH, CH), 1)
    tril = row >= col
    z = _dot(lr_ref[...].astype(BF16), wa_ref[...]) + ba_ref[...]
    la = _log_sigmoid(z) / GATE_NORM
    b = _tri_dot(jnp.where(tril, 1.0, 0.0).astype(BF16), la)
    bl = jnp.sum(la, axis=0, keepdims=True)
    e_b, e_nb, e_tb = jnp.exp(b), jnp.exp(-b), jnp.exp(bl - b)
    kk = k_ref[...]
    q_dec = (q_ref[...] * QK_SCALE) * e_b
    return tril, z, e_b, e_nb, e_tb, jnp.exp(bl), q_dec, kk * e_nb, kk * e_tb


def _gla_fwd(u, wa_b, ba, gn):
    wide = pl.BlockSpec((CH, H * DV), lambda n: (n, 0))

    def body(q_ref, k_ref, v_ref, g_ref, lr_ref, wa_ref, ba_ref, gn_ref, y_ref, o_ref, st_ref, state):
        @pl.when(pl.program_id(0) == 0)
        def _():
            state[...] = jnp.zeros_like(state)

        tril, _, _, _, _, dec, q_dec, k_inv, k_tail = _gla_decay_terms(lr_ref, wa_ref, ba_ref, q_ref, k_ref)
        for hd in range(H):
            ks, vs = slice(hd * DK, (hd + 1) * DK), slice(hd * DV, (hd + 1) * DV)
            qb, kib, ktb = q_dec[:, ks].astype(BF16), k_inv[:, ks].astype(BF16), k_tail[:, ks].astype(BF16)
            vb = v_ref[:, vs].astype(BF16)
            p = jnp.where(tril, _dot_nt(qb, kib), 0.0)
            st = state[hd]
            o = _dot(p.astype(BF16), vb) + _dot_nt(qb, st.astype(BF16))
            st_ref[hd] = st
            state[hd] = st * dec[:, ks] + _dot_tn(vb, ktb)
            o_ref[:, vs] = o
            on = _rms(o, gn_ref[...])
            gg = g_ref[:, vs]
            y_ref[:, vs] = (on * (gg * _sigmoid(gg))).astype(BF16)

    return pl.pallas_call(
        body, grid=(NCH,), name="gla_fwd",
        in_specs=_gla_in_specs(lambda n: n) + [pl.BlockSpec((RANK_PAD, H * DK), lambda n: (0, 0)),
                                               pl.BlockSpec((1, H * DK), lambda n: (0, 0)),
                                               pl.BlockSpec((1, DV), lambda n: (0, 0))],
        out_specs=[wide, wide, pl.BlockSpec((None, H, DV, DK), lambda n: (n, 0, 0, 0))],
        out_shape=[jax.ShapeDtypeStruct((S, H * DV), BF16), jax.ShapeDtypeStruct((S, H * DV), F32),
                   jax.ShapeDtypeStruct((NCH, H, DV, DK), F32)],
        scratch_shapes=[pltpu.VMEM((H, DV, DK), F32)],
        compiler_params=_params(32, ("arbitrary",)),
    )(u, u, u, u, u, wa_b, ba, gn)


GLA_DU = 2 * H * DK + 2 * H * DV + RANK_PAD


def _gla_bwd(u, o_arr, states, dy, wa_b, ba, gn, after=()):
    rev = lambda n: NCH - 1 - n
    wide = pl.BlockSpec((CH, H * DV), lambda n: (rev(n), 0))

    def body(q_ref, k_ref, v_ref, g_ref, lr_ref, o_ref, st_ref, dy_ref, wa_ref, ba_ref, gn_ref,
             du_ref, dwa_ref, dba_ref, dgn_ref, gstate, db_scr, dbl_scr):
        @pl.when(pl.program_id(0) == 0)
        def _():
            gstate[...] = jnp.zeros_like(gstate)
            dwa_ref[...] = jnp.zeros_like(dwa_ref)
            dba_ref[...] = jnp.zeros_like(dba_ref)
            dgn_ref[...] = jnp.zeros_like(dgn_ref)

        tril, z, e_b, e_nb, e_tb, dec, q_dec, k_inv, k_tail = _gla_decay_terms(lr_ref, wa_ref, ba_ref, q_ref, k_ref)
        gnv = gn_ref[...]
        dgn = jnp.zeros((1, DV), F32)
        for hd in range(H):
            ks, vs = slice(hd * DK, (hd + 1) * DK), slice(hd * DV, (hd + 1) * DV)
            qh, kih, kth = q_dec[:, ks], k_inv[:, ks], k_tail[:, ks]
            qb, kib, ktb = qh.astype(BF16), kih.astype(BF16), kth.astype(BF16)
            vb = v_ref[:, vs].astype(BF16)
            o = o_ref[:, vs]
            gg = g_ref[:, vs]
            dyh = dy_ref[:, vs]
            r = lax.rsqrt(jnp.mean(o * o, axis=-1, keepdims=True) + EPS)
            xh = o * r
            sg = _sigmoid(gg)
            dgate = dyh * (xh * gnv) * (sg * (1.0 + gg * (1.0 - sg)))
            don = dyh * (gg * sg)
            dgn = dgn + jnp.sum(don * xh, axis=0, keepdims=True)
            dxh = don * gnv
            d_o = (r * (dxh - xh * jnp.mean(dxh * xh, axis=-1, keepdims=True))).astype(BF16)
            pb = jnp.where(tril, _dot_nt(qb, kib), 0.0).astype(BF16)
            dpb = jnp.where(tril, _dot_nt(d_o, vb), 0.0).astype(BF16)
            gt = gstate[hd]
            gtb = gt.astype(BF16)
            st = st_ref[hd]
            dv = _dot_tn(pb, d_o) + _dot_nt(ktb, gtb)
            dq_dec = _dot(dpb, kib) + _dot(d_o, st.astype(BF16))
            dk_inv = _dot_tn(dpb, qb)
            dk_tail = _dot(vb, gtb)
            ddec = jnp.sum(gt * st, axis=0, keepdims=True)
            gstate[hd] = _dot_tn(d_o, qb) + dec[:, ks] * gt
            du_ref[:, ks] = (dq_dec * QK_SCALE * e_b[:, ks]).astype(BF16)
            du_ref[:, H * DK + hd * DK:H * DK + (hd + 1) * DK] = (dk_inv * e_nb[:, ks] + dk_tail * e_tb[:, ks]).astype(BF16)
            du_ref[:, 2 * H * DK + hd * DV:2 * H * DK + (hd + 1) * DV] = dv.astype(BF16)
            du_ref[:, 2 * H * DK + H * DV + hd * DV:2 * H * DK + H * DV + (hd + 1) * DV] = dgate.astype(BF16)
            db_scr[:, ks] = dq_dec * qh - dk_inv * kih - dk_tail * kth
            dbl_scr[:, ks] = jnp.sum(dk_tail * kth, axis=0, keepdims=True) + ddec * dec[:, ks]
        dgn_ref[...] += dgn
        row = lax.broadcasted_iota(jnp.int32, (CH, CH), 0)
        col = lax.broadcasted_iota(jnp.int32, (CH, CH), 1)
        dla = _tri_dot(jnp.where(row <= col, 1.0, 0.0).astype(BF16), db_scr[...]) + dbl_scr[...]
        dz = dla * (1.0 / GATE_NORM) * _sigmoid(-z)
        dzb = dz.astype(BF16)
        du_ref[:, GLA_DU - RANK_PAD:] = _dot_nt(dzb, wa_ref[...]).astype(BF16)
        dwa_ref[...] += _dot_tn(lr_ref[...].astype(BF16), dzb)
        dba_ref[...] += jnp.sum(dz, axis=0, keepdims=True)

    full = lambda shape: pl.BlockSpec(shape, lambda n: (0,) * len(shape))
    body, extra = _ordered_after(body, 11, after)
    return pl.pallas_call(
        body, grid=(NCH,), name="gla_bwd",
        in_specs=_gla_in_specs(rev) + [wide, pl.BlockSpec((None, H, DV, DK), lambda n: (rev(n), 0, 0, 0)),
                                       pl.BlockSpec((CH, H * DV), lambda n: (rev(n), 1)),
                                       full((RANK_PAD, H * DK)), full((1, H * DK)), full((1, DV))] + extra,
        out_specs=[pl.BlockSpec((CH, GLA_DU), lambda n: (rev(n), 0)), full((RANK_PAD, H * DK)), full((1, H * DK)),
                   full((1, DV))],
        out_shape=[jax.ShapeDtypeStruct((S, GLA_DU), BF16), jax.ShapeDtypeStruct((RANK_PAD, H * DK), F32),
                   jax.ShapeDtypeStruct((1, H * DK), F32), jax.ShapeDtypeStruct((1, DV), F32)],
        scratch_shapes=[pltpu.VMEM((H, DV, DK), F32), pltpu.VMEM((CH, H * DK), F32), pltpu.VMEM((1, H * DK), F32)],
        compiler_params=_params(32, ("arbitrary",)),
    )(u, u, u, u, u, o_arr, states, dy, wa_b, ba, gn, *after)


def _ffn_dw_in(n, dgu, tag, after=()):
    return _tn_matmul(n, dgu, pl.BlockSpec((S, 512), lambda s, m: (0, m)),
                      pl.BlockSpec((None, S, WIN_SHARD), lambda s, m: (s // (NDEV // 2), 0, s % (NDEV // 2))),
                      jax.ShapeDtypeStruct((D, 2 * FF), BF16), pl.BlockSpec((512, WIN_SHARD), lambda s, m: (m, s)),
                      (NDEV, D // 512), tag + "_dw_in", 32, after)


def _ffn_dw_out(act, df, tag, after=()):
    return _tn_matmul(act, df, pl.BlockSpec((S, 512), lambda m: (0, m)), pl.BlockSpec((S, D), lambda m: (0, 0)),
                      jax.ShapeDtypeStruct((FF, D), BF16), pl.BlockSpec((512, D), lambda m: (m, 0)),
                      (FF // 512,), tag + "_dw_out", 40, after)


def _fwd_ffn1(x, w, after=()):
    return _ffn_fwd(x, w["ffn1_norm"], w["ffn1_w_in"], w["ffn1_w_out"], "ffn1_fwd", after)


def _fwd_mixer(h1, w, after=()):
    u, n2 = _norm_matmul(h1, w["mix_norm"], w["w_in_mix"], "mix_in", after)
    y_pool, pooled = _pool_fwd(u, w["w_pool"], w["pool_scale"])
    y_gla, o_gla, states = _gla_fwd(u, w["w_alpha"], w["b_alpha"], w["gla_norm"])
    y = jnp.concatenate([y_pool, y_gla], axis=1)
    h2 = _matmul_residual(y, w["w_out_mix"], h1, "mix_out")
    return h2, dict(u=u, n2=n2, pooled=pooled, o_gla=o_gla, states=states, y=y)


def _fwd_ffn2_loss(h2, tgt, w, after=()):
    h3, n3, gu3 = _ffn_fwd(h2, w["ffn2_norm"], w["ffn2_w_in"], w["ffn2_w_out"], "ffn2_fwd", after)
    dh3, d_final, loss = _loss_head(h3, w["final_norm"], tgt)
    return dh3, d_final, loss, n3, gu3


def _backward(x, h1, h2, n1, gu1, sv, n3, gu3, dh3, w, at=lambda point, value: ()):
    g = {}
    dgu3, act3, df3, dh2, dh2b, g["ffn2_norm"] = _ffn_bwd_x(dh3, h2, w["ffn2_norm"], gu3, w["ffn2_w_in"], w["ffn2_w_out"], "ffn2_bwd")
    g["ffn2_w_out"] = _ffn_dw_out(act3, df3, "ffn2")
    g["ffn2_w_in"] = _ffn_dw_in(n3, dgu3, "ffn2", at("ffn2_w_out", g["ffn2_w_out"]))
    dy = _nt_matmul(dh2b, w["w_out_mix"], "mix_out_bwd", at("ffn2_w_in", g["ffn2_w_in"]))
    g["w_out_mix"] = _tn_matmul(sv["y"], dh2b, pl.BlockSpec((S, 512), lambda m: (0, m)), pl.BlockSpec((S, D), lambda m: (0, 0)),
                                jax.ShapeDtypeStruct((D, D), BF16), pl.BlockSpec((512, D), lambda m: (m, 0)), (D // 512,),
                                "mix_out_dw", 40)
    du_pool, g["w_pool"], g["pool_scale"] = _pool_bwd(dy, sv["pooled"], w["w_pool"], w["pool_scale"])
    du_gla, g["w_alpha"], g["b_alpha"], g["gla_norm"] = _gla_bwd(sv["u"], sv["o_gla"], sv["states"], dy, w["w_alpha"], w["b_alpha"],
                                                                 w["gla_norm"], at("pool_bwd", du_pool))
    du = jnp.concatenate([du_pool, du_gla], axis=1)
    g["w_in_mix"] = _tn_matmul(du, sv["n2"], pl.BlockSpec((S, 1408), lambda j, m: (0, j)), pl.BlockSpec((S, 512), lambda j, m: (0, m)),
                               jax.ShapeDtypeStruct((D_IN_PAD, D), BF16), pl.BlockSpec((1408, 512), lambda j, m: (j, m)),
                               (D_IN_PAD // 1408, D // 512), "mix_in_dw", 32)
    dh1, g["mix_norm"] = _matmul_normbwd(du, w["w_in_mix"], h1, w["mix_norm"], dh2, "mix_in_bwd", at("mixer_weights", g))
    dgu1, act1, df1 = _ffn_bwd_gu(dh1, gu1, w["ffn1_w_out"], "ffn1_bwd_gu", at("dh1", dh1))
    g["ffn1_w_out"] = _ffn_dw_out(act1, df1, "ffn1")
    half = S // FFN_TS // 2
    dx, dn_a = _ffn_bwd_dx(dgu1, dh1, x, w["ffn1_norm"], w["ffn1_w_in"], 0, half, "ffn1_bwd_dx_a",
                           at("ffn1_w_out", g["ffn1_w_out"]))
    g["ffn1_w_in"] = _ffn_dw_in(n1, dgu1, "ffn1", at("ffn1_dx_a", dx))
    dx, dn_b = _ffn_bwd_dx(dgu1, dh1, x, w["ffn1_norm"], w["ffn1_w_in"], half, half, "ffn1_bwd_dx_b",
                           at("ffn1_w_in", g["ffn1_w_in"]), into=dx)
    g["ffn1_norm"] = dn_a + dn_b
    return dx, g


def _local_step(x, tgt, w):
    h1, n1, gu1 = _fwd_ffn1(x, w)
    h2, sv = _fwd_mixer(h1, w)
    dh3, d_final, loss, n3, gu3 = _fwd_ffn2_loss(h2, tgt, w)
    dx, g = _backward(x, h1, h2, n1, gu1, sv, n3, gu3, dh3, w)
    return loss, dx, dict(final_norm=d_final, **g)


def _mesh_index():
    return 4 * lax.axis_index("x") + 2 * lax.axis_index("y") + lax.axis_index("c")


def _coords(p):
    return (p // 4, (p // 2) % 2, p % 2)


NCHIP = 4


def _place():
    return lax.axis_index("x"), lax.axis_index("y"), lax.axis_index("c")


def _rel_chip(x, y, rel):
    return ((1 - x) if rel & 1 else x, (1 - y) if rel & 2 else y)


def _dev_index(x, y, c):
    return 4 * x + 2 * y + c


def _cols(ref, p, width):
    return ref.at[:, pl.ds(pl.multiple_of(p * width, 128), width)]


def _sems(na, n):
    return [pltpu.SemaphoreType.DMA((na, n)), pltpu.SemaphoreType.DMA((na, n)), pltpu.SemaphoreType.DMA((na,))]


def _gather(items, name):
    arrays = [a for a, _ in items]
    kinds = [k for _, k in items]
    na = len(arrays)
    out_shape = [jax.ShapeDtypeStruct((NDEV,) + a.shape if k == "bcast" else (a.shape[0], NDEV * a.shape[1]), a.dtype)
                 for a, k in items]

    def body(*refs):
        ins, outs = refs[:na], refs[na:2 * na]
        send_sems, recv_sems, local_sems = refs[2 * na:]
        x, y, c = _place()
        sibling = (x, y, 1 - c)
        here, over_x, over_y, across = (x, y), (1 - x, y), (x, 1 - y), (1 - x, 1 - y)

        def half(ref, h):
            rows = ref.shape[0] // 2
            return ref.at[pl.ds(h * rows, rows), :]

        def slab(a, chip, core, h=None):
            ref = _slab(outs[a], kinds[a], _dev_index(*chip, core), ins[a].shape[1])
            return ref if h is None else half(ref, h)

        def copy(a, k, src, dst, to):
            return pltpu.make_async_remote_copy(src, dst, send_sems.at[a, k], recv_sems.at[a, k], device_id=to, device_id_type=MESH)

        sent = []

        def send(a, k, src, dst, to):
            sent.append(copy(a, k, src, dst, to))
            sent[-1].start()

        def arrived(a, k, chip, core, h=None):
            ref = slab(a, chip, core, h)
            copy(a, k, ref, ref, sibling).wait_recv()
            return ref

        local = [pltpu.make_async_copy(ins[a], slab(a, here, c), local_sems.at[a]) for a in range(na)]
        for cp in local:
            cp.start()
        for k, h, chip in ((1, 0, over_x), (4, 1, over_y), (2, 1, over_x), (5, 0, over_y)):
            for a in range(na):
                send(a, k, half(ins[a], h), slab(a, here, c, h), (*chip, c))
        for a in range(na):
            send(a, 0, ins[a], slab(a, here, c), sibling)
        for k, chip, h, onward, to, down in ((1, over_x, 0, 6, over_y, 7), (4, over_y, 1, 3, over_x, 10),
                                             (2, over_x, 1, None, None, 8), (5, over_y, 0, None, None, 9),
                                             (3, across, 1, None, None, 12), (6, across, 0, None, None, 11)):
            for a in range(na):
                ref = arrived(a, k, chip, c, h)
                if onward is not None:
                    send(a, onward, ref, ref, (*to, c))
                send(a, down, ref, ref, sibling)
        for a in range(na):
            arrived(a, 0, here, 1 - c)
        for k, chip, h in ((7, over_x, 0), (10, over_y, 1), (8, over_x, 1), (9, over_y, 0), (12, across, 1), (11, across, 0)):
            for a in range(na):
                arrived(a, k, chip, 1 - c, h)
        for cp in sent:
            cp.wait_send()
        for cp in local:
            cp.wait()

    return pl.pallas_call(body, name=name, in_specs=[ANY] * na, out_specs=[ANY] * na, out_shape=out_shape,
                          scratch_shapes=_sems(na, 13))(*arrays)


def _pair_exchange(items, name):
    arrays = [a for a, _ in items]
    kinds = [k for _, k in items]
    na = len(arrays)
    out_shape = [jax.ShapeDtypeStruct((NCHIP,) + (a.shape[1:] if k == "scatter" else (a.shape[0], a.shape[1] // NDEV)), a.dtype)
                 for a, k in items]

    def body(*refs):
        ins, outs = refs[:na], refs[na:2 * na]
        send_sems, recv_sems = refs[2 * na:]
        x, y, c = _place()
        copies = []
        for rel in range(NCHIP):
            p = _dev_index(*_rel_chip(x, y, rel), 1 - c)
            for a in range(na):
                src = ins[a].at[p] if kinds[a] == "scatter" else _cols(ins[a], p, ins[a].shape[1] // NDEV)
                copies.append(pltpu.make_async_remote_copy(src, outs[a].at[rel], send_sems.at[a, rel], recv_sems.at[a, rel],
                                                           device_id=(x, y, 1 - c), device_id_type=MESH))
                copies[-1].start()
        for cp in copies:
            cp.wait_send()
            cp.wait_recv()

    return pl.pallas_call(body, name=name, in_specs=[ANY] * na, out_specs=[ANY] * na, out_shape=out_shape,
                          scratch_shapes=_sems(na, NCHIP)[:2])(*arrays)


def _pair_add(own, kind, got, table, tr, name):
    _, rows, cols = got.shape
    if kind == "scatter":
        own_spec = pl.BlockSpec((None, tr, cols), lambda rel, i, t: (t[rel], i, 0))
    else:
        own_spec = pl.BlockSpec((tr, cols), lambda rel, i, t: (i, t[rel]))
    blk = pl.BlockSpec((None, tr, cols), lambda rel, i, t: (rel, i, 0))

    def body(t_ref, a_ref, b_ref, o_ref):
        o_ref[...] = (a_ref[...].astype(F32) + b_ref[...].astype(F32)).astype(o_ref.dtype)

    return pl.pallas_call(
        body, name=name, out_shape=jax.ShapeDtypeStruct(got.shape, got.dtype),
        grid_spec=pltpu.PrefetchScalarGridSpec(num_scalar_prefetch=1, grid=(NCHIP, rows // tr), in_specs=[own_spec, blk],
                                               out_specs=blk),
        compiler_params=_params(32),
    )(table, own, got)


def _exchange(items, name):
    arrays = [a for a, _ in items]
    kinds = [k for _, k in items]
    na = len(arrays)
    out_shape = [jax.ShapeDtypeStruct((NDEV,) + a.shape if k == "bcast" else a.shape, a.dtype) for a, k in items]

    def body(*refs):
        ins, outs = refs[:na], refs[na:2 * na]
        send_sems, recv_sems, local_sems = refs[2 * na:]
        x, y, c = _place()
        me = _dev_index(x, y, c)

        def mine(a):
            return 0 if kinds[a] == "chip" else me

        def src(a, p):
            return ins[a] if kinds[a] == "bcast" else ins[a].at[p]

        local = [pltpu.make_async_copy(src(a, mine(a)), outs[a].at[mine(a)], local_sems.at[a]) for a in range(na)]
        for cp in local:
            cp.start()
        waits = []
        for k in range(1, NDEV):
            to = (me + k) % NDEV
            frm = (me + NDEV - k) % NDEV
            for a in range(na):
                if kinds[a] == "chip":
                    if k >= NCHIP:
                        continue
                    send = pltpu.make_async_remote_copy(ins[a].at[k], outs[a].at[k], send_sems.at[a, k], recv_sems.at[a, k],
                                                        device_id=(*_rel_chip(x, y, k), c), device_id_type=MESH)
                    waits.append(send)
                else:
                    send = pltpu.make_async_remote_copy(src(a, to), outs[a].at[me], send_sems.at[a, k], recv_sems.at[a, k],
                                                        device_id=_coords(to), device_id_type=MESH)
                    waits.append(pltpu.make_async_remote_copy(src(a, to), outs[a].at[frm], send_sems.at[a, k],
                                                              recv_sems.at[a, k], device_id=_coords(to), device_id_type=MESH))
                send.start()
        for cp in waits:
            cp.wait_send()
            cp.wait_recv()
        for cp in local:
            cp.wait()

    return pl.pallas_call(body, name=name, in_specs=[ANY] * na, out_specs=[ANY] * na, out_shape=out_shape,
                          scratch_shapes=_sems(na, NDEV))(*arrays)


HBM = pl.BlockSpec(memory_space=pltpu.HBM)
SEM = pl.BlockSpec(memory_space=pltpu.SEMAPHORE)
DATAFLOW = pltpu.SideEffectType.DATAFLOW_SIDE_EFFECTING


def _pair_copies(kinds):
    def describe(srcs, lands, send_sems, recv_sems):
        x, y, c = _place()
        na = len(srcs)
        for rel in range(NCHIP):
            p = _dev_index(*_rel_chip(x, y, rel), 1 - c)
            for a in range(na):
                src = srcs[a].at[p] if kinds[a] == "scatter" else _cols(srcs[a], p, srcs[a].shape[1] // NDEV)
                cp = pltpu.make_async_remote_copy(src, lands[a].at[rel], send_sems.at[rel * na + a], recv_sems.at[rel * na + a],
                                                  device_id=(x, y, 1 - c), device_id_type=MESH)
                yield cp, cp
    return describe


def _chip_copies(srcs, lands, send_sems, recv_sems):
    x, y, c = _place()
    na = len(srcs)
    for rel in range(1, NCHIP):
        for a in range(na):
            i = (rel - 1) * na + a
            cp = pltpu.make_async_remote_copy(srcs[a].at[rel], lands[a].at[rel], send_sems.at[i], recv_sems.at[i],
                                              device_id=(*_rel_chip(x, y, rel), c), device_id_type=MESH)
            yield cp, cp


def _slab(ref, kind, s, width):
    return _cols(ref, s, width) if kind == "bcast_cols" else ref.at[s]


def _all_copies(srcs, lands, send_sems, recv_sems):
    x, y, c = _place()
    na = len(srcs)
    me = _dev_index(x, y, c)
    for a in range(na):
        yield pltpu.make_async_copy(srcs[a], lands[a].at[me], send_sems.at[a]), None
    for k in range(1, NDEV):
        to, frm = (me + k) % NDEV, (me + NDEV - k) % NDEV
        for a in range(na):
            i = k * na + a
            send = pltpu.make_async_remote_copy(srcs[a], lands[a].at[me], send_sems.at[i], recv_sems.at[i],
                                                device_id=_coords(to), device_id_type=MESH)
            arrival = pltpu.make_async_remote_copy(srcs[a], lands[a].at[frm], send_sems.at[i], recv_sems.at[i],
                                                   device_id=_coords(to), device_id_type=MESH)
            yield send, arrival


def _gather_copies(kinds):
    def describe(srcs, lands, send_sems, recv_sems):
        x, y, c = _place()
        na = len(srcs)
        me = _dev_index(x, y, c)
        for a in range(na):
            yield pltpu.make_async_copy(srcs[a], _slab(lands[a], kinds[a], me, srcs[a].shape[-1]),
                                        send_sems.at[NCHIP * na + a]), None
        for rel in range(NCHIP):
            to = (x, y, 1 - c) if rel == 0 else (*_rel_chip(x, y, rel), c)
            for a in range(na):
                width = srcs[a].shape[-1]
                i = rel * na + a
                send = pltpu.make_async_remote_copy(srcs[a], _slab(lands[a], kinds[a], me, width), send_sems.at[i], recv_sems.at[i],
                                                    device_id=to, device_id_type=MESH)
                arrival = pltpu.make_async_remote_copy(srcs[a], _slab(lands[a], kinds[a], _dev_index(*to), width), send_sems.at[i],
                                                       recv_sems.at[i], device_id=to, device_id_type=MESH)
                yield send, arrival
    return describe


def _pass_to_sibling(arrays, kinds, widths, name):
    na = len(arrays)

    def body(*refs):
        bufs = refs[na:2 * na]
        send_sems, recv_sems = refs[2 * na:]
        x, y, c = _place()
        copies = []
        for rel in range(1, NCHIP):
            for a in range(na):
                mine = _slab(bufs[a], kinds[a], _dev_index(*_rel_chip(x, y, rel), c), widths[a])
                theirs = _slab(bufs[a], kinds[a], _dev_index(*_rel_chip(x, y, rel), 1 - c), widths[a])
                send = pltpu.make_async_remote_copy(mine, mine, send_sems.at[a, rel], recv_sems.at[a, rel],
                                                    device_id=(x, y, 1 - c), device_id_type=MESH)
                send.start()
                copies.append((send, pltpu.make_async_remote_copy(theirs, theirs, send_sems.at[a, rel], recv_sems.at[a, rel],
                                                                  device_id=(x, y, 1 - c), device_id_type=MESH)))
        for send, arrival in copies:
            send.wait_send()
            arrival.wait_recv()

    return pl.pallas_call(body, name=name, in_specs=[ANY] * na, out_specs=[ANY] * na,
                          out_shape=[jax.ShapeDtypeStruct(a.shape, a.dtype) for a in arrays],
                          input_output_aliases={i: i for i in range(na)}, scratch_shapes=_sems(na, NCHIP)[:2])(*arrays)


def _start_copies(name, srcs, lands, describe, ncopies, after=()):
    arrays = list(srcs) + list(lands)
    ns, n, nin = len(srcs), len(arrays), len(arrays) + len(after)

    def body(*refs):
        for send, _ in describe(refs[:ns], refs[ns:n], refs[nin], refs[nin + 1]):
            send.start()
        refs[-1][...] = jnp.zeros_like(refs[-1])

    out = pl.pallas_call(
        body, name=name,
        out_shape=(pltpu.SemaphoreType.DMA((ncopies,)), pltpu.SemaphoreType.DMA((ncopies,)),
                   *[pltpu.HBM(a.shape, a.dtype) for a in arrays], jax.ShapeDtypeStruct((8, 128), F32)),
        in_specs=[HBM] * n + [ANY] * len(after), out_specs=(SEM, SEM, *[HBM] * n, pl.BlockSpec(memory_space=pltpu.VMEM)),
        input_output_aliases={i: 2 + i for i in range(n)},
        compiler_params=pltpu.CompilerParams(has_side_effects=DATAFLOW),
    )(*[pltpu.with_memory_space_constraint(a, pltpu.HBM) for a in arrays], *after)
    return out[0], out[1], list(out[2:2 + n]), out[-1]


def _wait_copies(name, send_sems, recv_sems, thru, ns, describe, after):
    n = len(thru)

    def body(*refs):
        for send, arrival in describe(refs[:ns], refs[ns:n], refs[n], refs[n + 1]):
            if arrival is None:
                send.wait()
            else:
                send.wait_send()
                arrival.wait_recv()

    out = pl.pallas_call(
        body, name=name, out_shape=tuple(pltpu.HBM(a.shape, a.dtype) for a in thru),
        in_specs=[HBM] * n + [SEM, SEM] + [ANY] * len(after), out_specs=tuple([HBM] * n),
        input_output_aliases={i: i for i in range(n)},
        compiler_params=pltpu.CompilerParams(has_side_effects=DATAFLOW),
    )(*thru, send_sems, recv_sems, *after)
    return list(out[:ns]), list(out[ns:])


def _adamw(parts, w, m, v, tr, name, tc=None):
    rows, cols = w.shape
    tc = cols if tc is None else tc
    nparts = len(parts)
    blk = pl.BlockSpec((tr, tc), lambda i, j: (i, j))

    def slab_spec(s):
        return pl.BlockSpec((None, tr, tc), lambda i, j: (s, i, j))

    def body(*refs):
        p_refs = refs[:nparts]
        w_ref, m_ref, v_ref, g_ref, d_ref, nm_ref, nv_ref = refs[nparts:]
        g = p_refs[0][...].astype(F32)
        for p_ref in p_refs[1:]:
            g = g + p_ref[...].astype(F32)
        nm = ADAM_B1 * m_ref[...] + (1.0 - ADAM_B1) * g
        nv = ADAM_B2 * v_ref[...] + (1.0 - ADAM_B2) * (g * g)
        m_hat = nm / (1.0 - ADAM_B1 ** ADAM_STEP)
        v_hat = nv / (1.0 - ADAM_B2 ** ADAM_STEP)
        g_ref[...] = g
        d_ref[...] = -ADAM_LR * (m_hat / (jnp.sqrt(v_hat) + ADAM_EPS) + ADAM_WD * w_ref[...])
        nm_ref[...] = nm
        nv_ref[...] = nv

    return pl.pallas_call(
        body, grid=(rows // tr, cols // tc), name=name,
        in_specs=[slab_spec(s) for _, s in parts] + [blk, blk, blk], out_specs=[blk] * 4,
        out_shape=[jax.ShapeDtypeStruct((rows, cols), F32)] * 4,
        compiler_params=_params(40),
    )(*[a for a, _ in parts], w, m, v)


def _pack_small(vals, extra=None):
    flat = [vals[n].reshape(-1).astype(F32) for n, _ in SMALL]
    tail = jnp.zeros((SMALL_ROWS * 128 - LOSS_AT,), F32)
    if extra is not None:
        tail = tail.at[0].set(extra)
    return jnp.concatenate(flat + [tail]).reshape(SMALL_ROWS, 128)


def _unpack_small(packed, like):
    flat, out, at = packed.reshape(-1), {}, 0
    for n, size in SMALL:
        out[n] = flat[at:at + size].reshape(like[n].shape)
        at += size
    return out, flat[LOSS_AT]


def kernel(x, ffn1_norm, ffn1_w_in, ffn1_w_out, mix_norm, w_in_mix, w_pool, pool_scale, w_alpha, b_alpha, gla_norm, w_out_mix, ffn2_norm, ffn2_w_in, ffn2_w_out, final_norm, loss_target, m_ffn1_norm, m_ffn1_w_in, m_ffn1_w_out, m_mix_norm, m_w_in_mix, m_w_pool, m_pool_scale, m_w_alpha, m_b_alpha, m_gla_norm, m_w_out_mix, m_ffn2_norm, m_ffn2_w_in, m_ffn2_w_out, m_final_norm, v_ffn1_norm, v_ffn1_w_in, v_ffn1_w_out, v_mix_norm, v_w_in_mix, v_w_pool, v_pool_scale, v_w_alpha, v_b_alpha, v_gla_norm, v_w_out_mix, v_ffn2_norm, v_ffn2_w_in, v_ffn2_w_out, v_final_norm):
    names = ["ffn1_norm", "ffn1_w_in", "ffn1_w_out", "mix_norm", "w_in_mix", "w_pool", "pool_scale", "w_alpha", "b_alpha",
             "gla_norm", "w_out_mix", "ffn2_norm", "ffn2_w_in", "ffn2_w_out", "final_norm"]
    p = dict(zip(names, [ffn1_norm, ffn1_w_in, ffn1_w_out, mix_norm, w_in_mix, w_pool, pool_scale, w_alpha, b_alpha,
                         gla_norm, w_out_mix, ffn2_norm, ffn2_w_in, ffn2_w_out, final_norm]))
    m = dict(zip(names, [m_ffn1_norm, m_ffn1_w_in, m_ffn1_w_out, m_mix_norm, m_w_in_mix, m_w_pool, m_pool_scale, m_w_alpha,
                         m_b_alpha, m_gla_norm, m_w_out_mix, m_ffn2_norm, m_ffn2_w_in, m_ffn2_w_out, m_final_norm]))
    v = dict(zip(names, [v_ffn1_norm, v_ffn1_w_in, v_ffn1_w_out, v_mix_norm, v_w_in_mix, v_w_pool, v_pool_scale, v_w_alpha,
                         v_b_alpha, v_gla_norm, v_w_out_mix, v_ffn2_norm, v_ffn2_w_in, v_ffn2_w_out, v_final_norm]))

    mx, my, mc = _place()
    me = _dev_index(mx, my, mc)
    table = jnp.stack([_dev_index(*_rel_chip(mx, my, rel), mc) for rel in range(NCHIP)]).astype(jnp.int32)

    def landing(shard, kind):
        shape = (shard.shape[0], NDEV * shard.shape[1]) if kind == "bcast_cols" else (NDEV,) + shard.shape
        return lax.empty(shape, shard.dtype)

    def gather_begin(items, tag, after):
        kinds = [kind for _, kind in items]
        copies = _gather_copies(kinds)
        s, r, thru, tok = _start_copies(tag + "_start", [a for a, _ in items], [landing(a, kind) for a, kind in items], copies,
                                        (NCHIP + 1) * len(items), after)
        return (s, r, thru, copies, kinds, [a.shape[-1] for a, _ in items], tag), tok

    def gather_end(state, after):
        s, r, thru, copies, kinds, widths, tag = state
        _, lands = _wait_copies(tag + "_wait", s, r, thru, len(kinds), copies, after)
        return _pass_to_sibling(lands, kinds, widths, tag + "_pass")

    def shard16(n):
        return p[n][0].astype(BF16)

    g_w1in, g_w1out = _gather([(shard16("ffn1_w_in"), "bcast_cols"), (shard16("ffn1_w_out"), "bcast")], "gather_ffn1")
    mix_state, tok_m = gather_begin([(jnp.transpose(p["w_in_mix"][0]).astype(BF16), "bcast"), (shard16("w_out_mix"), "bcast"),
                                     (p["w_pool"][0].reshape(H * 32, PG), "bcast"), (p["w_alpha"][0], "bcast")], "gather_mix",
                                    (g_w1out,))
    ffn2_state, tok_f = gather_begin([(shard16("ffn2_w_in"), "bcast_cols"), (shard16("ffn2_w_out"), "bcast")], "gather_ffn2",
                                     (tok_m,))
    full = {"ffn1_w_in": g_w1in, "ffn1_w_out": g_w1out.reshape(FF, D), "final_norm": final_norm.reshape(1, D)}
    for n in ("ffn1_norm", "mix_norm", "ffn2_norm", "pool_scale", "b_alpha", "gla_norm"):
        full[n] = p[n]

    xs, tgt = x[0], loss_target[0]
    h1, n1, gu1 = _fwd_ffn1(xs, full, after=(tok_m, tok_f))
    g_wmix, g_wo, g_wpool, g_walpha = gather_end(mix_state, (h1,))
    walpha = jnp.transpose(g_walpha, (1, 0, 2)).reshape(RANK, H * DK)
    full.update({
        "w_in_mix": jnp.pad(g_wmix.reshape(D_IN, D), ((0, D_IN_PAD - D_IN), (0, 0))),
        "w_out_mix": g_wo.reshape(D, D),
        "w_pool": jnp.transpose(g_wpool.reshape(NDEV, H, 32, PG), (1, 0, 2, 3)).reshape(H, PG, PG).astype(BF16),
        "w_alpha": jnp.pad(walpha, ((0, RANK_PAD - RANK), (0, 0))).astype(BF16),
    })
    h2, sv = _fwd_mixer(h1, full)
    g_w2in, g_w2out = gather_end(ffn2_state, (h2,))
    full.update({"ffn2_w_in": g_w2in, "ffn2_w_out": g_w2out.reshape(FF, D)})
    dh3, d_final, loss_part, n3, gu3 = _fwd_ffn2_loss(h2, tgt, full)


    def slab_shape(a, kind):
        return (NCHIP,) + (a.shape[1:] if kind == "scatter" else (a.shape[0], a.shape[1] // NDEV))

    def pair_add_all(own, got, tag):
        return [_pair_add(a, kind, got_a, table, tr, "%s_pair_add_%d" % (tag, i))
                for i, ((a, kind, tr), got_a) in enumerate(zip(own, got))]

    def reduce_begin(own, tag, after=()):
        kinds = [kind for _, kind, _ in own]
        copies = _pair_copies(kinds)
        s, r, thru, tok = _start_copies(tag + "_pair_start", [a for a, _, _ in own],
                                        [lax.empty(slab_shape(a, kind), a.dtype) for a, kind, _ in own], copies,
                                        NCHIP * len(own), after)
        return dict(own=own, copies=copies, s=s, r=r, thru=thru, tag=tag), tok

    def reduce_middle(st, after):
        own, tag = st["own"], st["tag"]
        sent, got = _wait_copies(tag + "_pair_wait", st["s"], st["r"], st["thru"], len(own), st["copies"], after)
        pre = pair_add_all([(a, kind, tr) for a, (_, kind, tr) in zip(sent, own)], got, tag)
        st["s"], st["r"], st["thru"], tok = _start_copies(tag + "_chip_start", pre, [lax.empty(a.shape, a.dtype) for a in pre],
                                                          _chip_copies, (NCHIP - 1) * len(pre))
        return tok

    def reduce_end(st, after):
        n = len(st["own"])
        pre, land = _wait_copies(st["tag"] + "_chip_wait", st["s"], st["r"], st["thru"], n, _chip_copies, after)
        return [[(a, 0)] + [(b, rel) for rel in range(1, NCHIP)] for a, b in zip(pre, land)]

    def w_in_item(a):
        return (a, "scatter_cols", 256)

    def w_out_item(a):
        return (a.reshape(NDEV, WOUT_SHARD, D), "scatter", WOUT_SHARD // 2)

    red, small = {}, {}

    def at(point, value):
        if point == "ffn2_w_out":
            red["w2out"], tok = reduce_begin([w_out_item(value)], "ffn2_w_out")
        elif point == "ffn2_w_in":
            tok_a = reduce_middle(red["w2out"], (value,))
            red["w2in"], tok = reduce_begin([w_in_item(value)], "ffn2_w_in", (tok_a,))
        elif point == "pool_bwd":
            tok = reduce_middle(red["w2in"], (value,))
        elif point == "mixer_weights":
            d_wmix8 = value["w_in_mix"][:D_IN].reshape(NDEV, MIX_SHARD, D)
            d_wpool8 = jnp.transpose(value["w_pool"].reshape(H, NDEV, 32, PG), (1, 0, 2, 3)).reshape(NDEV, H * 32, PG)
            d_walpha8 = jnp.transpose(value["w_alpha"][:RANK].reshape(RANK, NDEV, H * DK // NDEV), (1, 0, 2))
            red["mix"], tok = reduce_begin([(d_wmix8, "scatter", MIX_SHARD),
                                            (value["w_out_mix"].reshape(NDEV, D // NDEV, D), "scatter", D // NDEV),
                                            (d_wpool8, "scatter", H * 32), (d_walpha8, "scatter", RANK)], "mix")
        elif point == "dh1":
            tok = reduce_middle(red["mix"], (value,))
        elif point == "ffn1_w_out":
            red["w1out"], tok = reduce_begin([w_out_item(value)], "ffn1_w_out")
        elif point == "ffn1_dx_a":
            tok = reduce_middle(red["w1out"], (value,))
        elif point == "ffn1_w_in":
            red["w1in"], tok = reduce_begin([w_in_item(value)], "ffn1_w_in")
        return (tok,)

    dx, g = _backward(xs, h1, h2, n1, gu1, sv, n3, gu3, dh3, full, at)
    packed = _pack_small(dict(final_norm=d_final, **g), loss_part[0, 0])
    small_s, small_r, small_thru, tok_s = _start_copies("gather_small_start", [packed], [lax.empty((NDEV,) + packed.shape, F32)],
                                                        _all_copies, NDEV, (dx,))
    tok_c = reduce_middle(red["w1in"], (tok_s,))

    def upd(parts, n, shape2d, tr):
        res = _adamw(parts, p[n].reshape(shape2d), m[n].reshape(shape2d), v[n].reshape(shape2d), tr, "adamw_" + n)
        return [r.reshape(p[n].shape) for r in res]

    def transposed(a):
        return jnp.transpose(a[0])

    (p_w2out,) = reduce_end(red["w2out"], (tok_c,))
    (p_w2in,) = reduce_end(red["w2in"], (tok_c,))
    p_wmix, p_wo, p_wpool, p_walpha = reduce_end(red["mix"], (tok_c,))
    out = {
        "ffn2_w_in": upd(p_w2in, "ffn2_w_in", (D, WIN_SHARD), 128),
        "ffn2_w_out": upd(p_w2out, "ffn2_w_out", (WOUT_SHARD, D), 64),
        "w_out_mix": upd(p_wo, "w_out_mix", (D // NDEV, D), 64),
        "w_pool": upd(p_wpool, "w_pool", (H * 32, PG), H * 32),
        "w_alpha": upd(p_walpha, "w_alpha", (RANK, H * DK // NDEV), RANK),
    }
    out["w_in_mix"] = [jnp.transpose(r)[None] for r in
                       _adamw(p_wmix, transposed(p["w_in_mix"]), transposed(m["w_in_mix"]), transposed(v["w_in_mix"]),
                              MIX_SHARD, "adamw_w_in_mix", tc=512)]
    _, (r_small,) = _wait_copies("gather_small_wait", small_s, small_r, small_thru, 1, _all_copies,
                                 (out["w_in_mix"][3], out["ffn2_w_in"][3], out["ffn2_w_out"][3], out["w_out_mix"][3]))
    small_res = _adamw([(r_small, s) for s in range(NDEV)], _pack_small(p), _pack_small(m), _pack_small(v), SMALL_ROWS,
                       "adamw_small")
    (p_w1out,) = reduce_end(red["w1out"], (small_res[0],))
    out["ffn1_w_out"] = upd(p_w1out, "ffn1_w_out", (WOUT_SHARD, D), 64)
    (p_w1in,) = reduce_end(red["w1in"], (out["ffn1_w_out"][3],))
    out["ffn1_w_in"] = upd(p_w1in, "ffn1_w_in", (D, WIN_SHARD), 128)
    unpacked = [_unpack_small(r, p) for r in small_res]
    loss = unpacked[0][1]
    for n, _ in SMALL:
        out[n] = [u[0][n] for u in unpacked]

    return (loss, dx.reshape(1, S, D), *[out[n][0] for n in names], *[out[n][1] for n in names],
            *[out[n][2] for n in names], *[out[n][3] for n in names])
```

```python
import jax
import jax.numpy as jnp
from jax import lax
from jax.experimental import pallas as pl
from jax.experimental.pallas import tpu as pltpu

F32, BF16 = jnp.float32, jnp.bfloat16
MESH = pl.DeviceIdType.MESH
ANY = pl.BlockSpec(memory_space=pl.ANY)

NDEV = 8
S = 2048
D = 2048
FF = 5632
WIN_SHARD = 2 * FF // NDEV
WOUT_SHARD = FF // NDEV
D_POOL = 1024
PG = 256
POOL_WINDOWS = (2, 4, 8, 16)
H = 4
DK = 128
DV = 256
CH = 64
NCH = S // CH
RANK = 16
RANK_PAD = 128
D_IN = 4112
D_IN_PAD = 4224
MIX_SHARD = D_IN // NDEV
O_Q, O_K, O_V, O_G, O_R = 1024, 1536, 2048, 3072, 4096
GATE_NORM = 16.0
QK_SCALE = DK ** -0.5
EPS = 1e-6
ADAM_LR, ADAM_B1, ADAM_B2, ADAM_EPS, ADAM_WD, ADAM_STEP = 0.001, 0.9, 0.999, 1e-08, 0.01, 10
V7X_VMEM_BYTES = 64 << 20

SMALL = (("ffn1_norm", 2048), ("mix_norm", 2048), ("ffn2_norm", 2048), ("final_norm", 2048),
         ("pool_scale", 1024), ("b_alpha", 512), ("gla_norm", 256))
SMALL_ROWS = 80
LOSS_AT = sum(n for _, n in SMALL)


def _params(vmem_mb, sem=None):
    return pltpu.CompilerParams(dimension_semantics=sem, vmem_limit_bytes=min(vmem_mb << 20, V7X_VMEM_BYTES - (4 << 20)))


def _dot(a, b):
    return jnp.dot(a, b, preferred_element_type=F32)


def _dot_nt(a, b):
    return lax.dot_general(a, b, (((1,), (1,)), ((), ())), preferred_element_type=F32)


def _dot_tn(a, b):
    return lax.dot_general(a, b, (((0,), (0,)), ((), ())), preferred_element_type=F32)


def _sigmoid(x):
    return 0.5 * jnp.tanh(0.5 * x) + 0.5


def _log_sigmoid(x):
    return jnp.minimum(x, 0.0) - jnp.log(1.0 + jnp.exp(-jnp.abs(x)))


def _rms(x, g):
    r = lax.rsqrt(jnp.mean(x * x, axis=-1, keepdims=True) + EPS)
    return x * r * g


def _rms_bwd(dn, x, g):
    r = lax.rsqrt(jnp.mean(x * x, axis=-1, keepdims=True) + EPS)
    xh = x * r
    dxh = dn * g
    dx = r * (dxh - xh * jnp.mean(dxh * xh, axis=-1, keepdims=True))
    return dx, jnp.sum(dn * xh, axis=0, keepdims=True)


ROWS = 64


def _row_loop(total, fn, init=0):
    def step(t, carry):
        return fn(pl.ds(pl.multiple_of(t * ROWS, ROWS), ROWS), carry)
    return lax.fori_loop(0, total // ROWS, step, init)


def _split3(x):
    hi = x.astype(BF16)
    r1 = x - hi.astype(F32)
    mid = r1.astype(BF16)
    lo = (r1 - mid.astype(F32)).astype(BF16)
    return hi, mid, lo


def _tri_dot(tri_b, x):
    hi, mid, lo = _split3(x)
    return (_dot(tri_b, lo) + _dot(tri_b, mid)) + _dot(tri_b, hi)


FFN_TS, FFN_TF = 512, 512


def _ffn_specs():
    wg = pl.BlockSpec((D, FFN_TF), lambda i, j: (0, j))
    wu = pl.BlockSpec((D, FFN_TF), lambda i, j: (0, FF // FFN_TF + j))
    wo = pl.BlockSpec((FFN_TF, D), lambda i, j: (j, 0))
    row = pl.BlockSpec((FFN_TS, D), lambda i, j: (i, 0))
    vec = pl.BlockSpec((1, D), lambda i, j: (0, 0))
    gu = pl.BlockSpec((2, FFN_TS, FFN_TF), lambda i, j: (0, i, j))
    return wg, wu, wo, row, vec, gu


def _ordered_after(body, n_in, after):
    def wrapped(*refs):
        return body(*refs[:n_in], *refs[n_in + len(after):])
    return wrapped, [ANY] * len(after)


def _ffn_fwd(h, g, w_in8, w_out, name, after=()):
    nj = FF // FFN_TF
    wg, wu, wo, row, vec, gu = _ffn_specs()

    def body(h_ref, g_ref, wg_ref, wu_ref, wo_ref, ho_ref, n_ref, gu_ref, acc_ref):
        j = pl.program_id(1)

        @pl.when(j == 0)
        def _():
            def norm(rows, c):
                n_ref[rows, :] = _rms(h_ref[rows, :], g_ref[...]).astype(BF16)
                return c
            _row_loop(FFN_TS, norm)
            acc_ref[...] = jnp.zeros_like(acc_ref)

        n = n_ref[...]
        gate = _dot(n, wg_ref[...])
        up = _dot(n, wu_ref[...])
        gu_ref[0] = gate.astype(BF16)
        gu_ref[1] = up.astype(BF16)
        a = (gate * _sigmoid(gate)) * up
        acc_ref[...] += _dot(a.astype(BF16), wo_ref[...])

        @pl.when(j == nj - 1)
        def _():
            def residual(rows, c):
                ho_ref[rows, :] = h_ref[rows, :] + 0.5 * acc_ref[rows, :]
                return c
            _row_loop(FFN_TS, residual)

    body, extra = _ordered_after(body, 5, after)
    return pl.pallas_call(
        body, grid=(S // FFN_TS, nj), name=name,
        in_specs=[row, vec, wg, wu, wo] + extra, out_specs=[row, row, gu],
        out_shape=[jax.ShapeDtypeStruct((S, D), F32), jax.ShapeDtypeStruct((S, D), BF16),
                   jax.ShapeDtypeStruct((2, S, FF), BF16)],
        scratch_shapes=[pltpu.VMEM((FFN_TS, D), F32)],
        compiler_params=_params(56, ("arbitrary", "arbitrary")),
    )(h, g, w_in8, w_in8, w_out, *after)


def _ffn_bwd_x(dhp, h, g, gu_arr, w_in8, w_out, name, after=()):
    ni, nj = S // FFN_TS, FF // FFN_TF
    wg, wu, wo, row, vec, gu = _ffn_specs()
    act = pl.BlockSpec((FFN_TS, FFN_TF), lambda i, j: (i, j))

    def body(dhp_ref, h_ref, g_ref, gu_ref, wg_ref, wu_ref, wo_ref,
             dgu_ref, a_ref, df_ref, dh_ref, dhb_ref, dg_ref, acc_ref):
        i, j = pl.program_id(0), pl.program_id(1)

        @pl.when(j == 0)
        def _():
            def half(rows, c):
                df_ref[rows, :] = (0.5 * dhp_ref[rows, :]).astype(BF16)
                return c
            _row_loop(FFN_TS, half)
            acc_ref[...] = jnp.zeros_like(acc_ref)

        gate = gu_ref[0].astype(F32)
        up = gu_ref[1].astype(F32)
        da = _dot_nt(df_ref[...], wo_ref[...])
        sg = _sigmoid(gate)
        silu = gate * sg
        dgate = (da * up * (sg * (1.0 + gate * (1.0 - sg)))).astype(BF16)
        dup = (da * silu).astype(BF16)
        a_ref[...] = (silu * up).astype(BF16)
        dgu_ref[0] = dgate
        dgu_ref[1] = dup
        acc_ref[...] += _dot_nt(dgate, wg_ref[...]) + _dot_nt(dup, wu_ref[...])

        @pl.when(j == nj - 1)
        def _():
            def norm_bwd(rows, dg):
                dx, dg_rows = _rms_bwd(acc_ref[rows, :], h_ref[rows, :], g_ref[...])
                dh = dhp_ref[rows, :] + dx
                dh_ref[rows, :] = dh
                dhb_ref[rows, :] = dh.astype(BF16)
                return dg + dg_rows
            dg = _row_loop(FFN_TS, norm_bwd, jnp.zeros((1, D), F32))

            @pl.when(i == 0)
            def _():
                dg_ref[...] = dg

            @pl.when(i > 0)
            def _():
                dg_ref[...] += dg

    body, extra = _ordered_after(body, 7, after)
    return pl.pallas_call(
        body, grid=(ni, nj), name=name,
        in_specs=[row, row, vec, gu, wg, wu, wo] + extra,
        out_specs=[gu, act, row, row, row, vec],
        out_shape=[jax.ShapeDtypeStruct((2, S, FF), BF16), jax.ShapeDtypeStruct((S, FF), BF16),
                   jax.ShapeDtypeStruct((S, D), BF16), jax.ShapeDtypeStruct((S, D), F32),
                   jax.ShapeDtypeStruct((S, D), BF16), jax.ShapeDtypeStruct((1, D), F32)],
        scratch_shapes=[pltpu.VMEM((FFN_TS, D), F32)],
        compiler_params=_params(58, ("arbitrary", "arbitrary")),
    )(dhp, h, g, gu_arr, w_in8, w_in8, w_out, *after)


def _ffn_bwd_gu(dhp, gu_arr, w_out, name, after=()):
    _, _, wo, row, _, gu = _ffn_specs()
    act = pl.BlockSpec((FFN_TS, FFN_TF), lambda i, j: (i, j))

    def body(dhp_ref, gu_ref, wo_ref, dgu_ref, a_ref, df_ref):
        @pl.when(pl.program_id(1) == 0)
        def _():
            def half(rows, c):
                df_ref[rows, :] = (0.5 * dhp_ref[rows, :]).astype(BF16)
                return c
            _row_loop(FFN_TS, half)

        gate = gu_ref[0].astype(F32)
        up = gu_ref[1].astype(F32)
        da = _dot_nt(df_ref[...], wo_ref[...])
        sg = _sigmoid(gate)
        silu = gate * sg
        dgu_ref[0] = (da * up * (sg * (1.0 + gate * (1.0 - sg)))).astype(BF16)
        dgu_ref[1] = (da * silu).astype(BF16)
        a_ref[...] = (silu * up).astype(BF16)

    body, extra = _ordered_after(body, 3, after)
    return pl.pallas_call(
        body, grid=(S // FFN_TS, FF // FFN_TF), name=name,
        in_specs=[row, gu, wo] + extra, out_specs=[gu, act, row],
        out_shape=[jax.ShapeDtypeStruct((2, S, FF), BF16), jax.ShapeDtypeStruct((S, FF), BF16),
                   jax.ShapeDtypeStruct((S, D), BF16)],
        compiler_params=_params(40, ("arbitrary", "arbitrary")),
    )(dhp, gu_arr, w_out, *after)


def _ffn_bwd_dx(dgu, dhp, h, g, w_in, first_tile, ntiles, name, after=(), into=None):
    nj = FF // FFN_TF
    wg, wu, _, _, vec, _ = _ffn_specs()
    row_in = pl.BlockSpec((FFN_TS, D), lambda i, j: (first_tile + i, 0))
    dgu_spec = pl.BlockSpec((2, FFN_TS, FFN_TF), lambda i, j: (0, first_tile + i, j))
    after = tuple(after) + (() if into is None else (into,))

    def body(dgu_ref, dhp_ref, h_ref, g_ref, wg_ref, wu_ref, dh_ref, dg_ref, acc_ref):
        i, j = pl.program_id(0), pl.program_id(1)

        @pl.when(j == 0)
        def _():
            acc_ref[...] = jnp.zeros_like(acc_ref)

        acc_ref[...] += _dot_nt(dgu_ref[0], wg_ref[...]) + _dot_nt(dgu_ref[1], wu_ref[...])

        @pl.when(j == nj - 1)
        def _():
            def norm_bwd(rows, dg):
                dx, dg_rows = _rms_bwd(acc_ref[rows, :], h_ref[rows, :], g_ref[...])
                dh_ref[rows, :] = dhp_ref[rows, :] + dx
                return dg + dg_rows
            dg = _row_loop(FFN_TS, norm_bwd, jnp.zeros((1, D), F32))

            @pl.when(i == 0)
            def _():
                dg_ref[...] = dg

            @pl.when(i > 0)
            def _():
                dg_ref[...] += dg

    body, extra = _ordered_after(body, 6, after)
    return pl.pallas_call(
        body, grid=(ntiles, nj), name=name,
        in_specs=[dgu_spec, row_in, row_in, vec, wg, wu] + extra, out_specs=[row_in, vec],
        out_shape=[jax.ShapeDtypeStruct((S, D), F32), jax.ShapeDtypeStruct((1, D), F32)],
        input_output_aliases={} if into is None else {6 + len(after) - 1: 0},
        scratch_shapes=[pltpu.VMEM((FFN_TS, D), F32)],
        compiler_params=_params(48, ("arbitrary", "arbitrary")),
    )(dgu, dhp, h, g, w_in, w_in, *after)


def _tn_matmul(a, b, a_spec, b_spec, out_shape, out_spec, grid, name, vmem_mb, after=()):
    def body(a_ref, b_ref, o_ref):
        o_ref[...] = _dot_tn(a_ref[...], b_ref[...]).astype(o_ref.dtype)

    body, extra = _ordered_after(body, 2, after)
    return pl.pallas_call(body, grid=grid, name=name, in_specs=[a_spec, b_spec] + extra, out_specs=out_spec,
                          out_shape=out_shape, compiler_params=_params(vmem_mb))(a, b, *after)


def _resident(shape):
    return pl.BlockSpec(shape, lambda i: (0,) * len(shape), pipeline_mode=pl.Buffered(1))


def _norm_matmul(h, g, w, name, after=(), ts=256):
    n_out = w.shape[0]

    def body(h_ref, g_ref, w_ref, u_ref, n_ref):
        def norm(rows, c):
            n_ref[rows, :] = _rms(h_ref[rows, :], g_ref[...]).astype(BF16)
            return c
        _row_loop(ts, norm)
        u_ref[...] = _dot_nt(n_ref[...], w_ref[...])

    body, extra = _ordered_after(body, 3, after)
    return pl.pallas_call(
        body, grid=(S // ts,), name=name,
        in_specs=[pl.BlockSpec((ts, D), lambda i: (i, 0)), pl.BlockSpec((1, D), lambda i: (0, 0)), _resident(w.shape)] + extra,
        out_specs=[pl.BlockSpec((ts, n_out), lambda i: (i, 0)), pl.BlockSpec((ts, D), lambda i: (i, 0))],
        out_shape=[jax.ShapeDtypeStruct((S, n_out), F32), jax.ShapeDtypeStruct((S, D), BF16)],
        compiler_params=_params(48, ("arbitrary",)),
    )(h, g, w, *after)


def _matmul_residual(a, w, res, name, ts=512):
    k, n_out = w.shape

    def body(a_ref, w_ref, r_ref, o_ref):
        o_ref[...] = r_ref[...] + _dot(a_ref[...], w_ref[...])

    return pl.pallas_call(
        body, grid=(S // ts,), name=name,
        in_specs=[pl.BlockSpec((ts, k), lambda i: (i, 0)), _resident(w.shape), pl.BlockSpec((ts, n_out), lambda i: (i, 0))],
        out_specs=pl.BlockSpec((ts, n_out), lambda i: (i, 0)),
        out_shape=jax.ShapeDtypeStruct((S, n_out), F32),
        compiler_params=_params(40),
    )(a, w, res)


def _nt_matmul(a, w, name, after=(), ts=512):
    n_out, k = w.shape

    def body(a_ref, w_ref, o_ref):
        o_ref[...] = _dot_nt(a_ref[...], w_ref[...])

    body, extra = _ordered_after(body, 2, after)
    return pl.pallas_call(
        body, grid=(S // ts,), name=name,
        in_specs=[pl.BlockSpec((ts, k), lambda i: (i, 0)), _resident(w.shape)] + extra,
        out_specs=pl.BlockSpec((ts, n_out), lambda i: (i, 0)),
        out_shape=jax.ShapeDtypeStruct((S, n_out), F32),
        compiler_params=_params(40),
    )(a, w, *after)


def _matmul_normbwd(du, w, h, g, dres, name, after=(), ts=256):
    n_in = w.shape[0]
    row = pl.BlockSpec((ts, D), lambda i: (i, 0))
    vec = pl.BlockSpec((1, D), lambda i: (0, 0))

    def body(du_ref, w_ref, h_ref, g_ref, dres_ref, dh_ref, dg_ref, acc_ref):
        i = pl.program_id(0)
        acc_ref[...] = _dot(du_ref[...], w_ref[...])

        def norm_bwd(rows, dg):
            dx, dg_rows = _rms_bwd(acc_ref[rows, :], h_ref[rows, :], g_ref[...])
            dh_ref[rows, :] = dres_ref[rows, :] + dx
            return dg + dg_rows
        dg = _row_loop(ts, norm_bwd, jnp.zeros((1, D), F32))

        @pl.when(i == 0)
        def _():
            dg_ref[...] = dg

        @pl.when(i > 0)
        def _():
            dg_ref[...] += dg

    body, extra = _ordered_after(body, 5, after)
    return pl.pallas_call(
        body, grid=(S // ts,), name=name,
        in_specs=[pl.BlockSpec((ts, n_in), lambda i: (i, 0)), _resident(w.shape), row, vec, row] + extra,
        out_specs=[row, vec],
        out_shape=[jax.ShapeDtypeStruct((S, D), F32), jax.ShapeDtypeStruct((1, D), F32)],
        scratch_shapes=[pltpu.VMEM((ts, D), F32)],
        compiler_params=_params(52, ("arbitrary",)),
    )(du, w, h, g, dres, *after)


def _loss_head(h, g, tgt, ts=256):
    row = pl.BlockSpec((ts, D), lambda i: (i, 0))
    vec = pl.BlockSpec((1, D), lambda i: (0, 0))

    def body(h_ref, g_ref, t_ref, dh_ref, dg_ref, loss_ref):
        i = pl.program_id(0)

        def rows_fn(rows, carry):
            dg, part = carry
            x = h_ref[rows, :]
            gv = g_ref[...]
            err = _rms(x, gv) - t_ref[rows, :]
            part = part + 0.5 * jnp.sum(jnp.mean(err * err, axis=-1, keepdims=True), axis=0, keepdims=True)
            dx, dg_rows = _rms_bwd(err * (1.0 / D), x, gv)
            dh_ref[rows, :] = dx
            return dg + dg_rows, part
        dg, part = _row_loop(ts, rows_fn, (jnp.zeros((1, D), F32), jnp.zeros((1, 1), F32)))

        @pl.when(i == 0)
        def _():
            dg_ref[...] = dg
            loss_ref[...] = jnp.broadcast_to(part, loss_ref.shape)

        @pl.when(i > 0)
        def _():
            dg_ref[...] += dg
            loss_ref[...] += jnp.broadcast_to(part, loss_ref.shape)

    return pl.pallas_call(
        body, grid=(S // ts,), name="loss_head",
        in_specs=[row, vec, row], out_specs=[row, vec, pl.BlockSpec((1, 128), lambda i: (0, 0))],
        out_shape=[jax.ShapeDtypeStruct((S, D), F32), jax.ShapeDtypeStruct((1, D), F32),
                   jax.ShapeDtypeStruct((1, 128), F32)],
        compiler_params=_params(40, ("arbitrary",)),
    )(h, g, tgt)


def _pool_specs():
    blk = pl.BlockSpec((S, PG), lambda gi: (0, gi))
    wp = pl.BlockSpec((None, PG, PG), lambda gi: (gi, 0, 0))
    sc = pl.BlockSpec((1, PG), lambda gi: (0, gi))
    return blk, wp, sc


def _pool_fwd(u, wp_b, scale):
    blk, wp, sc = _pool_specs()

    def body(u_ref, wp_ref, sc_ref, y_ref, pooled_ref):
        win = 2 << pl.program_id(0)
        row = lax.broadcasted_iota(jnp.int32, (S, PG), 0)
        x = u_ref[...]
        s = x
        for k in (1, 2, 4, 8):
            s = s + jnp.where((row >= k) & (k < win), pltpu.roll(s, k, 0), 0.0)
        cnt = jnp.minimum(row + 1, win).astype(F32)
        pooled = (s / cnt - x).astype(BF16)
        pooled_ref[...] = pooled
        y_ref[...] = (_dot(pooled, wp_ref[...]) * sc_ref[...]).astype(BF16)

    return pl.pallas_call(
        body, grid=(len(POOL_WINDOWS),), name="pool_fwd", in_specs=[blk, wp, sc], out_specs=[blk, blk],
        out_shape=[jax.ShapeDtypeStruct((S, D_POOL), BF16), jax.ShapeDtypeStruct((S, D_POOL), BF16)],
        compiler_params=_params(40),
    )(u, wp_b, scale)


def _pool_bwd(dy, pooled, wp_b, scale):
    blk, wp, sc = _pool_specs()

    def body(dy_ref, p_ref, wp_ref, sc_ref, du_ref, dwp_ref, dsc_ref):
        win = 2 << pl.program_id(0)
        row = lax.broadcasted_iota(jnp.int32, (S, PG), 0)
        dyv = dy_ref[...]
        pooled = p_ref[...]
        w = wp_ref[...]
        dsc_ref[...] = jnp.sum(dyv * _dot(pooled, w), axis=0, keepdims=True)
        dz = (dyv * sc_ref[...]).astype(BF16)
        dwp_ref[...] = _dot_tn(pooled, dz)
        dpooled = _dot_nt(dz, w)
        cnt = jnp.minimum(row + 1, win).astype(F32)
        fs = dpooled / cnt
        for k in (1, 2, 4, 8):
            fs = fs + jnp.where((row < S - k) & (k < win), pltpu.roll(fs, S - k, 0), 0.0)
        du_ref[...] = (fs - dpooled).astype(BF16)

    return pl.pallas_call(
        body, grid=(len(POOL_WINDOWS),), name="pool_bwd", in_specs=[blk, blk, wp, sc], out_specs=[blk, wp, sc],
        out_shape=[jax.ShapeDtypeStruct((S, D_POOL), BF16), jax.ShapeDtypeStruct((len(POOL_WINDOWS), PG, PG), F32),
                   jax.ShapeDtypeStruct((1, D_POOL), F32)],
        compiler_params=_params(40),
    )(dy, pooled, wp_b, scale)


def _gla_in_specs(chunk_of):
    def at(width, col):
        return pl.BlockSpec((CH, width), lambda n: (chunk_of(n), col))
    return [at(H * DK, O_Q // (H * DK)), at(H * DK, O_K // (H * DK)), at(H * DV, O_V // (H * DV)),
            at(H * DV, O_G // (H * DV)), at(RANK_PAD, O_R // RANK_PAD)]


def _gla_decay_terms(lr_ref, wa_ref, ba_ref, q_ref, k_ref):
    row = lax.broadcasted_iota(jnp.int32, (CH, CH), 0)
    col = lax.broadcasted_iota(jnp.int32, (CH, CH), 1)
    tril = row >= col
    z = _dot(lr_ref[...].astype(BF16), wa_ref[...]) + ba_ref[...]
    la = _log_sigmoid(z) / GATE_NORM
    b = _tri_dot(jnp.where(tril, 1.0, 0.0).astype(BF16), la)
    bl = jnp.sum(la, axis=0, keepdims=True)
    e_b, e_nb, e_tb = jnp.exp(b), jnp.exp(-b), jnp.exp(bl - b)
    kk = k_ref[...]
    q_dec = (q_ref[...] * QK_SCALE) * e_b
    return tril, z, e_b, e_nb, e_tb, jnp.exp(bl), q_dec, kk * e_nb, kk * e_tb


def _gla_fwd(u, wa_b, ba, gn):
    wide = pl.BlockSpec((CH, H * DV), lambda n: (n, 0))

    def body(q_ref, k_ref, v_ref, g_ref, lr_ref, wa_ref, ba_ref, gn_ref, y_ref, o_ref, st_ref, state):
        @pl.when(pl.program_id(0) == 0)
        def _():
            state[...] = jnp.zeros_like(state)

        tril, _, _, _, _, dec, q_dec, k_inv, k_tail = _gla_decay_terms(lr_ref, wa_ref, ba_ref, q_ref, k_ref)
        for hd in range(H):
            ks, vs = slice(hd * DK, (hd + 1) * DK), slice(hd * DV, (hd + 1) * DV)
            qb, kib, ktb = q_dec[:, ks].astype(BF16), k_inv[:, ks].astype(BF16), k_tail[:, ks].astype(BF16)
            vb = v_ref[:, vs].astype(BF16)
            p = jnp.where(tril, _dot_nt(qb, kib), 0.0)
            st = state[hd]
            o = _dot(p.astype(BF16), vb) + _dot_nt(qb, st.astype(BF16))
            st_ref[hd] = st
            state[hd] = st * dec[:, ks] + _dot_tn(vb, ktb)
            o_ref[:, vs] = o
            on = _rms(o, gn_ref[...])
            gg = g_ref[:, vs]
            y_ref[:, vs] = (on * (gg * _sigmoid(gg))).astype(BF16)

    return pl.pallas_call(
        body, grid=(NCH,), name="gla_fwd",
        in_specs=_gla_in_specs(lambda n: n) + [pl.BlockSpec((RANK_PAD, H * DK), lambda n: (0, 0)),
                                               pl.BlockSpec((1, H * DK), lambda n: (0, 0)),
                                               pl.BlockSpec((1, DV), lambda n: (0, 0))],
        out_specs=[wide, wide, pl.BlockSpec((None, H, DV, DK), lambda n: (n, 0, 0, 0))],
        out_shape=[jax.ShapeDtypeStruct((S, H * DV), BF16), jax.ShapeDtypeStruct((S, H * DV), F32),
                   jax.ShapeDtypeStruct((NCH, H, DV, DK), F32)],
        scratch_shapes=[pltpu.VMEM((H, DV, DK), F32)],
        compiler_params=_params(32, ("arbitrary",)),
    )(u, u, u, u, u, wa_b, ba, gn)


GLA_DU = 2 * H * DK + 2 * H * DV + RANK_PAD


def _gla_bwd(u, o_arr, states, dy, wa_b, ba, gn, after=()):
    rev = lambda n: NCH - 1 - n
    wide = pl.BlockSpec((CH, H * DV), lambda n: (rev(n), 0))

    def body(q_ref, k_ref, v_ref, g_ref, lr_ref, o_ref, st_ref, dy_ref, wa_ref, ba_ref, gn_ref,
             du_ref, dwa_ref, dba_ref, dgn_ref, gstate, db_scr, dbl_scr):
        @pl.when(pl.program_id(0) == 0)
        def _():
            gstate[...] = jnp.zeros_like(gstate)
            dwa_ref[...] = jnp.zeros_like(dwa_ref)
            dba_ref[...] = jnp.zeros_like(dba_ref)
            dgn_ref[...] = jnp.zeros_like(dgn_ref)

        tril, z, e_b, e_nb, e_tb, dec, q_dec, k_inv, k_tail = _gla_decay_terms(lr_ref, wa_ref, ba_ref, q_ref, k_ref)
        gnv = gn_ref[...]
        dgn = jnp.zeros((1, DV), F32)
        for hd in range(H):
            ks, vs = slice(hd * DK, (hd + 1) * DK), slice(hd * DV, (hd + 1) * DV)
            qh, kih, kth = q_dec[:, ks], k_inv[:, ks], k_tail[:, ks]
            qb, kib, ktb = qh.astype(BF16), kih.astype(BF16), kth.astype(BF16)
            vb = v_ref[:, vs].astype(BF16)
            o = o_ref[:, vs]
            gg = g_ref[:, vs]
            dyh = dy_ref[:, vs]
            r = lax.rsqrt(jnp.mean(o * o, axis=-1, keepdims=True) + EPS)
            xh = o * r
            sg = _sigmoid(gg)
            dgate = dyh * (xh * gnv) * (sg * (1.0 + gg * (1.0 - sg)))
            don = dyh * (gg * sg)
            dgn = dgn + jnp.sum(don * xh, axis=0, keepdims=True)
            dxh = don * gnv
            d_o = (r * (dxh - xh * jnp.mean(dxh * xh, axis=-1, keepdims=True))).astype(BF16)
            pb = jnp.where(tril, _dot_nt(qb, kib), 0.0).astype(BF16)
            dpb = jnp.where(tril, _dot_nt(d_o, vb), 0.0).astype(BF16)
            gt = gstate[hd]
            gtb = gt.astype(BF16)
            st = st_ref[hd]
            dv = _dot_tn(pb, d_o) + _dot_nt(ktb, gtb)
            dq_dec = _dot(dpb, kib) + _dot(d_o, st.astype(BF16))
            dk_inv = _dot_tn(dpb, qb)
            dk_tail = _dot(vb, gtb)
            ddec = jnp.sum(gt * st, axis=0, keepdims=True)
            gstate[hd] = _dot_tn(d_o, qb) + dec[:, ks] * gt
            du_ref[:, ks] = (dq_dec * QK_SCALE * e_b[:, ks]).astype(BF16)
            du_ref[:, H * DK + hd * DK:H * DK + (hd + 1) * DK] = (dk_inv * e_nb[:, ks] + dk_tail * e_tb[:, ks]).astype(BF16)
            du_ref[:, 2 * H * DK + hd * DV:2 * H * DK + (hd + 1) * DV] = dv.astype(BF16)
            du_ref[:, 2 * H * DK + H * DV + hd * DV:2 * H * DK + H * DV + (hd + 1) * DV] = dgate.astype(BF16)
            db_scr[:, ks] = dq_dec * qh - dk_inv * kih - dk_tail * kth
            dbl_scr[:, ks] = jnp.sum(dk_tail * kth, axis=0, keepdims=True) + ddec * dec[:, ks]
        dgn_ref[...] += dgn
        row = lax.broadcasted_iota(jnp.int32, (CH, CH), 0)
        col = lax.broadcasted_iota(jnp.int32, (CH, CH), 1)
        dla = _tri_dot(jnp.where(row <= col, 1.0, 0.0).astype(BF16), db_scr[...]) + dbl_scr[...]
        dz = dla * (1.0 / GATE_NORM) * _sigmoid(-z)
        dzb = dz.astype(BF16)
        du_ref[:, GLA_DU - RANK_PAD:] = _dot_nt(dzb, wa_ref[...]).astype(BF16)
        dwa_ref[...] += _dot_tn(lr_ref[...].astype(BF16), dzb)
        dba_ref[...] += jnp.sum(dz, axis=0, keepdims=True)

    full = lambda shape: pl.BlockSpec(shape, lambda n: (0,) * len(shape))
    body, extra = _ordered_after(body, 11, after)
    return pl.pallas_call(
        body, grid=(NCH,), name="gla_bwd",
        in_specs=_gla_in_specs(rev) + [wide, pl.BlockSpec((None, H, DV, DK), lambda n: (rev(n), 0, 0, 0)),
                                       pl.BlockSpec((CH, H * DV), lambda n: (rev(n), 1)),
                                       full((RANK_PAD, H * DK)), full((1, H * DK)), full((1, DV))] + extra,
        out_specs=[pl.BlockSpec((CH, GLA_DU), lambda n: (rev(n), 0)), full((RANK_PAD, H * DK)), full((1, H * DK)),
                   full((1, DV))],
        out_shape=[jax.ShapeDtypeStruct((S, GLA_DU), BF16), jax.ShapeDtypeStruct((RANK_PAD, H * DK), F32),
                   jax.ShapeDtypeStruct((1, H * DK), F32), jax.ShapeDtypeStruct((1, DV), F32)],
        scratch_shapes=[pltpu.VMEM((H, DV, DK), F32), pltpu.VMEM((CH, H * DK), F32), pltpu.VMEM((1, H * DK), F32)],
        compiler_params=_params(32, ("arbitrary",)),
    )(u, u, u, u, u, o_arr, states, dy, wa_b, ba, gn, *after)


def _ffn_dw_in(n, dgu, tag, after=()):
    return _tn_matmul(n, dgu, pl.BlockSpec((S, 512), lambda s, m: (0, m)),
                      pl.BlockSpec((None, S, WIN_SHARD), lambda s, m: (s // (NDEV // 2), 0, s % (NDEV // 2))),
                      jax.ShapeDtypeStruct((D, 2 * FF), BF16), pl.BlockSpec((512, WIN_SHARD), lambda s, m: (m, s)),
                      (NDEV, D // 512), tag + "_dw_in", 32, after)


def _ffn_dw_out(act, df, tag, after=()):
    return _tn_matmul(act, df, pl.BlockSpec((S, 512), lambda m: (0, m)), pl.BlockSpec((S, D), lambda m: (0, 0)),
                      jax.ShapeDtypeStruct((FF, D), BF16), pl.BlockSpec((512, D), lambda m: (m, 0)),
                      (FF // 512,), tag + "_dw_out", 40, after)


def _fwd_ffn1(x, w, after=()):
    return _ffn_fwd(x, w["ffn1_norm"], w["ffn1_w_in"], w["ffn1_w_out"], "ffn1_fwd", after)


def _fwd_mixer(h1, w, after=()):
    u, n2 = _norm_matmul(h1, w["mix_norm"], w["w_in_mix"], "mix_in", after)
    y_pool, pooled = _pool_fwd(u, w["w_pool"], w["pool_scale"])
    y_gla, o_gla, states = _gla_fwd(u, w["w_alpha"], w["b_alpha"], w["gla_norm"])
    y = jnp.concatenate([y_pool, y_gla], axis=1)
    h2 = _matmul_residual(y, w["w_out_mix"], h1, "mix_out")
    return h2, dict(u=u, n2=n2, pooled=pooled, o_gla=o_gla, states=states, y=y)


def _fwd_ffn2_loss(h2, tgt, w, after=()):
    h3, n3, gu3 = _ffn_fwd(h2, w["ffn2_norm"], w["ffn2_w_in"], w["ffn2_w_out"], "ffn2_fwd", after)
    dh3, d_final, loss = _loss_head(h3, w["final_norm"], tgt)
    return dh3, d_final, loss, n3, gu3


def _backward(x, h1, h2, n1, gu1, sv, n3, gu3, dh3, w, at=lambda point, value: ()):
    g = {}
    dgu3, act3, df3, dh2, dh2b, g["ffn2_norm"] = _ffn_bwd_x(dh3, h2, w["ffn2_norm"], gu3, w["ffn2_w_in"], w["ffn2_w_out"], "ffn2_bwd")
    g["ffn2_w_out"] = _ffn_dw_out(act3, df3, "ffn2")
    g["ffn2_w_in"] = _ffn_dw_in(n3, dgu3, "ffn2", at("ffn2_w_out", g["ffn2_w_out"]))
    dy = _nt_matmul(dh2b, w["w_out_mix"], "mix_out_bwd", at("ffn2_w_in", g["ffn2_w_in"]))
    g["w_out_mix"] = _tn_matmul(sv["y"], dh2b, pl.BlockSpec((S, 512), lambda m: (0, m)), pl.BlockSpec((S, D), lambda m: (0, 0)),
                                jax.ShapeDtypeStruct((D, D), BF16), pl.BlockSpec((512, D), lambda m: (m, 0)), (D // 512,),
                                "mix_out_dw", 40)
    du_pool, g["w_pool"], g["pool_scale"] = _pool_bwd(dy, sv["pooled"], w["w_pool"], w["pool_scale"])
    du_gla, g["w_alpha"], g["b_alpha"], g["gla_norm"] = _gla_bwd(sv["u"], sv["o_gla"], sv["states"], dy, w["w_alpha"], w["b_alpha"],
                                                                 w["gla_norm"], at("pool_bwd", du_pool))
    du = jnp.concatenate([du_pool, du_gla], axis=1)
    g["w_in_mix"] = _tn_matmul(du, sv["n2"], pl.BlockSpec((S, 1408), lambda j, m: (0, j)), pl.BlockSpec((S, 512), lambda j, m: (0, m)),
                               jax.ShapeDtypeStruct((D_IN_PAD, D), BF16), pl.BlockSpec((1408, 512), lambda j, m: (j, m)),
                               (D_IN_PAD // 1408, D // 512), "mix_in_dw", 32)
    dh1, g["mix_norm"] = _matmul_normbwd(du, w["w_in_mix"], h1, w["mix_norm"], dh2, "mix_in_bwd", at("mixer_weights", g))
    dgu1, act1, df1 = _ffn_bwd_gu(dh1, gu1, w["ffn1_w_out"], "ffn1_bwd_gu", at("dh1", dh1))
    g["ffn1_w_out"] = _ffn_dw_out(act1, df1, "ffn1")
    half = S // FFN_TS // 2
    dx, dn_a = _ffn_bwd_dx(dgu1, dh1, x, w["ffn1_norm"], w["ffn1_w_in"], 0, half, "ffn1_bwd_dx_a",
                           at("ffn1_w_out", g["ffn1_w_out"]))
    g["ffn1_w_in"] = _ffn_dw_in(n1, dgu1, "ffn1", at("ffn1_dx_a", dx))
    dx, dn_b = _ffn_bwd_dx(dgu1, dh1, x, w["ffn1_norm"], w["ffn1_w_in"], half, half, "ffn1_bwd_dx_b",
                           at("ffn1_w_in", g["ffn1_w_in"]), into=dx)
    g["ffn1_norm"] = dn_a + dn_b
    return dx, g


def _local_step(x, tgt, w):
    h1, n1, gu1 = _fwd_ffn1(x, w)
    h2, sv = _fwd_mixer(h1, w)
    dh3, d_final, loss, n3, gu3 = _fwd_ffn2_loss(h2, tgt, w)
    dx, g = _backward(x, h1, h2, n1, gu1, sv, n3, gu3, dh3, w)
    return loss, dx, dict(final_norm=d_final, **g)


def _coords(p):
    return (p // 4, (p // 2) % 2, p % 2)


NCHIP = 4


def _place():
    return lax.axis_index("x"), lax.axis_index("y"), lax.axis_index("c")


def _rel_chip(x, y, rel):
    return ((1 - x) if rel & 1 else x, (1 - y) if rel & 2 else y)


def _dev_index(x, y, c):
    return 4 * x + 2 * y + c


def _cols(ref, p, width):
    return ref.at[:, pl.ds(pl.multiple_of(p * width, 128), width)]


def _sems(na, n):
    return [pltpu.SemaphoreType.DMA((na, n)), pltpu.SemaphoreType.DMA((na, n)), pltpu.SemaphoreType.DMA((na,))]


def _gather(items, name):
    arrays = [a for a, _ in items]
    kinds = [k for _, k in items]
    na = len(arrays)
    out_shape = [jax.ShapeDtypeStruct((NDEV,) + a.shape if k == "bcast" else (a.shape[0], NDEV * a.shape[1]), a.dtype)
                 for a, k in items]

    def body(*refs):
        ins, outs = refs[:na], refs[na:2 * na]
        send_sems, recv_sems, local_sems = refs[2 * na:]
        x, y, c = _place()
        sibling = (x, y, 1 - c)
        here, over_x, over_y, across = (x, y), (1 - x, y), (x, 1 - y), (1 - x, 1 - y)

        def half(ref, h):
            rows = ref.shape[0] // 2
            return ref.at[pl.ds(h * rows, rows), :]

        def slab(a, chip, core, h=None):
            ref = _slab(outs[a], kinds[a], _dev_index(*chip, core), ins[a].shape[1])
            return ref if h is None else half(ref, h)

        def copy(a, k, src, dst, to):
            return pltpu.make_async_remote_copy(src, dst, send_sems.at[a, k], recv_sems.at[a, k], device_id=to, device_id_type=MESH)

        sent = []

        def send(a, k, src, dst, to):
            sent.append(copy(a, k, src, dst, to))
            sent[-1].start()

        def arrived(a, k, chip, core, h=None):
            ref = slab(a, chip, core, h)
            copy(a, k, ref, ref, sibling).wait_recv()
            return ref

        local = [pltpu.make_async_copy(ins[a], slab(a, here, c), local_sems.at[a]) for a in range(na)]
        for cp in local:
            cp.start()
        for k, h, chip in ((1, 0, over_x), (4, 1, over_y), (2, 1, over_x), (5, 0, over_y)):
            for a in range(na):
                send(a, k, half(ins[a], h), slab(a, here, c, h), (*chip, c))
        for a in range(na):
            send(a, 0, ins[a], slab(a, here, c), sibling)
        for k, chip, h, onward, to, down in ((1, over_x, 0, 6, over_y, 7), (4, over_y, 1, 3, over_x, 10),
                                             (2, over_x, 1, None, None, 8), (5, over_y, 0, None, None, 9),
                                             (3, across, 1, None, None, 12), (6, across, 0, None, None, 11)):
            for a in range(na):
                ref = arrived(a, k, chip, c, h)
                if onward is not None:
                    send(a, onward, ref, ref, (*to, c))
                send(a, down, ref, ref, sibling)
        for a in range(na):
            arrived(a, 0, here, 1 - c)
        for k, chip, h in ((7, over_x, 0), (10, over_y, 1), (8, over_x, 1), (9, over_y, 0), (12, across, 1), (11, across, 0)):
            for a in range(na):
                arrived(a, k, chip, 1 - c, h)
        for cp in sent:
            cp.wait_send()
        for cp in local:
            cp.wait()

    return pl.pallas_call(body, name=name, in_specs=[ANY] * na, out_specs=[ANY] * na, out_shape=out_shape,
                          scratch_shapes=_sems(na, 13))(*arrays)


def _pair_add(own, kind, got, table, tr, name):
    _, rows, cols = got.shape
    if kind == "scatter":
        own_spec = pl.BlockSpec((None, tr, cols), lambda rel, i, t: (t[rel], i, 0))
    else:
        own_spec = pl.BlockSpec((tr, cols), lambda rel, i, t: (i, t[rel]))
    blk = pl.BlockSpec((None, tr, cols), lambda rel, i, t: (rel, i, 0))

    def body(t_ref, a_ref, b_ref, o_ref):
        o_ref[...] = (a_ref[...].astype(F32) + b_ref[...].astype(F32)).astype(o_ref.dtype)

    return pl.pallas_call(
        body, name=name, out_shape=jax.ShapeDtypeStruct(got.shape, got.dtype),
        grid_spec=pltpu.PrefetchScalarGridSpec(num_scalar_prefetch=1, grid=(NCHIP, rows // tr), in_specs=[own_spec, blk],
                                               out_specs=blk),
        compiler_params=_params(32),
    )(table, own, got)


HBM = pl.BlockSpec(memory_space=pltpu.HBM)
SEM = pl.BlockSpec(memory_space=pltpu.SEMAPHORE)
DATAFLOW = pltpu.SideEffectType.DATAFLOW_SIDE_EFFECTING


def _pair_copies(kinds):
    def describe(srcs, lands, send_sems, recv_sems):
        x, y, c = _place()
        na = len(srcs)
        for rel in range(NCHIP):
            p = _dev_index(*_rel_chip(x, y, rel), 1 - c)
            for a in range(na):
                src = srcs[a].at[p] if kinds[a] == "scatter" else _cols(srcs[a], p, srcs[a].shape[1] // NDEV)
                cp = pltpu.make_async_remote_copy(src, lands[a].at[rel], send_sems.at[rel * na + a], recv_sems.at[rel * na + a],
                                                  device_id=(x, y, 1 - c), device_id_type=MESH)
                yield cp, cp
    return describe


def _chip_copies(srcs, lands, send_sems, recv_sems):
    x, y, c = _place()
    na = len(srcs)
    for rel in range(1, NCHIP):
        for a in range(na):
            i = (rel - 1) * na + a
            cp = pltpu.make_async_remote_copy(srcs[a].at[rel], lands[a].at[rel], send_sems.at[i], recv_sems.at[i],
                                              device_id=(*_rel_chip(x, y, rel), c), device_id_type=MESH)
            yield cp, cp


def _slab(ref, kind, s, width):
    return _cols(ref, s, width) if kind == "bcast_cols" else ref.at[s]


def _all_copies(srcs, lands, send_sems, recv_sems):
    x, y, c = _place()
    na = len(srcs)
    me = _dev_index(x, y, c)
    for a in range(na):
        yield pltpu.make_async_copy(srcs[a], lands[a].at[me], send_sems.at[a]), None
    for k in range(1, NDEV):
        to, frm = (me + k) % NDEV, (me + NDEV - k) % NDEV
        for a in range(na):
            i = k * na + a
            send = pltpu.make_async_remote_copy(srcs[a], lands[a].at[me], send_sems.at[i], recv_sems.at[i],
                                                device_id=_coords(to), device_id_type=MESH)
            arrival = pltpu.make_async_remote_copy(srcs[a], lands[a].at[frm], send_sems.at[i], recv_sems.at[i],
                                                   device_id=_coords(to), device_id_type=MESH)
            yield send, arrival


def _gather_copies(kinds):
    def describe(srcs, lands, send_sems, recv_sems):
        x, y, c = _place()
        na = len(srcs)
        me = _dev_index(x, y, c)
        for a in range(na):
            yield pltpu.make_async_copy(srcs[a], _slab(lands[a], kinds[a], me, srcs[a].shape[-1]),
                                        send_sems.at[NCHIP * na + a]), None
        for rel in range(NCHIP):
            to = (x, y, 1 - c) if rel == 0 else (*_rel_chip(x, y, rel), c)
            for a in range(na):
                width = srcs[a].shape[-1]
                i = rel * na + a
                send = pltpu.make_async_remote_copy(srcs[a], _slab(lands[a], kinds[a], me, width), send_sems.at[i], recv_sems.at[i],
                                                    device_id=to, device_id_type=MESH)
                arrival = pltpu.make_async_remote_copy(srcs[a], _slab(lands[a], kinds[a], _dev_index(*to), width), send_sems.at[i],
                                                       recv_sems.at[i], device_id=to, device_id_type=MESH)
                yield send, arrival
    return describe


def _pass_to_sibling(arrays, kinds, widths, name):
    na = len(arrays)

    def body(*refs):
        bufs = refs[na:2 * na]
        send_sems, recv_sems = refs[2 * na:]
        x, y, c = _place()
        copies = []
        for rel in range(1, NCHIP):
            for a in range(na):
                mine = _slab(bufs[a], kinds[a], _dev_index(*_rel_chip(x, y, rel), c), widths[a])
                theirs = _slab(bufs[a], kinds[a], _dev_index(*_rel_chip(x, y, rel), 1 - c), widths[a])
                send = pltpu.make_async_remote_copy(mine, mine, send_sems.at[a, rel], recv_sems.at[a, rel],
                                                    device_id=(x, y, 1 - c), device_id_type=MESH)
                send.start()
                copies.append((send, pltpu.make_async_remote_copy(theirs, theirs, send_sems.at[a, rel], recv_sems.at[a, rel],
                                                                  device_id=(x, y, 1 - c), device_id_type=MESH)))
        for send, arrival in copies:
            send.wait_send()
            arrival.wait_recv()

    return pl.pallas_call(body, name=name, in_specs=[ANY] * na, out_specs=[ANY] * na,
                          out_shape=[jax.ShapeDtypeStruct(a.shape, a.dtype) for a in arrays],
                          input_output_aliases={i: i for i in range(na)}, scratch_shapes=_sems(na, NCHIP)[:2])(*arrays)


def _start_copies(name, srcs, lands, describe, ncopies, after=()):
    arrays = list(srcs) + list(lands)
    ns, n, nin = len(srcs), len(arrays), len(arrays) + len(after)

    def body(*refs):
        for send, _ in describe(refs[:ns], refs[ns:n], refs[nin], refs[nin + 1]):
            send.start()
        refs[-1][...] = jnp.zeros_like(refs[-1])

    out = pl.pallas_call(
        body, name=name,
        out_shape=(pltpu.SemaphoreType.DMA((ncopies,)), pltpu.SemaphoreType.DMA((ncopies,)),
                   *[pltpu.HBM(a.shape, a.dtype) for a in arrays], jax.ShapeDtypeStruct((8, 128), F32)),
        in_specs=[HBM] * n + [ANY] * len(after), out_specs=(SEM, SEM, *[HBM] * n, pl.BlockSpec(memory_space=pltpu.VMEM)),
        input_output_aliases={i: 2 + i for i in range(n)},
        compiler_params=pltpu.CompilerParams(has_side_effects=DATAFLOW),
    )(*[pltpu.with_memory_space_constraint(a, pltpu.HBM) for a in arrays], *after)
    return out[0], out[1], list(out[2:2 + n]), out[-1]


def _wait_copies(name, send_sems, recv_sems, thru, ns, describe, after):
    n = len(thru)

    def body(*refs):
        for send, arrival in describe(refs[:ns], refs[ns:n], refs[n], refs[n + 1]):
            if arrival is None:
                send.wait()
            else:
                send.wait_send()
                arrival.wait_recv()

    out = pl.pallas_call(
        body, name=name, out_shape=tuple(pltpu.HBM(a.shape, a.dtype) for a in thru),
        in_specs=[HBM] * n + [SEM, SEM] + [ANY] * len(after), out_specs=tuple([HBM] * n),
        input_output_aliases={i: i for i in range(n)},
        compiler_params=pltpu.CompilerParams(has_side_effects=DATAFLOW),
    )(*thru, send_sems, recv_sems, *after)
    return list(out[:ns]), list(out[ns:])


def _adamw(parts, w, m, v, tr, name, tc=None):
    rows, cols = w.shape
    tc = cols if tc is None else tc
    nparts = len(parts)
    blk = pl.BlockSpec((tr, tc), lambda i, j: (i, j))

    def slab_spec(s):
        return pl.BlockSpec((None, tr, tc), lambda i, j: (s, i, j))

    def body(*refs):
        p_refs = refs[:nparts]
        w_ref, m_ref, v_ref, g_ref, d_ref, nm_ref, nv_ref = refs[nparts:]
        g = p_refs[0][...].astype(F32)
        for p_ref in p_refs[1:]:
            g = g + p_ref[...].astype(F32)
        nm = ADAM_B1 * m_ref[...] + (1.0 - ADAM_B1) * g
        nv = ADAM_B2 * v_ref[...] + (1.0 - ADAM_B2) * (g * g)
        m_hat = nm / (1.0 - ADAM_B1 ** ADAM_STEP)
        v_hat = nv / (1.0 - ADAM_B2 ** ADAM_STEP)
        g_ref[...] = g
        d_ref[...] = -ADAM_LR * (m_hat / (jnp.sqrt(v_hat) + ADAM_EPS) + ADAM_WD * w_ref[...])
        nm_ref[...] = nm
        nv_ref[...] = nv

    return pl.pallas_call(
        body, grid=(rows // tr, cols // tc), name=name,
        in_specs=[slab_spec(s) for _, s in parts] + [blk, blk, blk], out_specs=[blk] * 4,
        out_shape=[jax.ShapeDtypeStruct((rows, cols), F32)] * 4,
        compiler_params=_params(40),
    )(*[a for a, _ in parts], w, m, v)


def _pack_small(vals, extra=None):
    flat = [vals[n].reshape(-1).astype(F32) for n, _ in SMALL]
    tail = jnp.zeros((SMALL_ROWS * 128 - LOSS_AT,), F32)
    if extra is not None:
        tail = tail.at[0].set(extra)
    return jnp.concatenate(flat + [tail]).reshape(SMALL_ROWS, 128)


def _unpack_small(packed, like):
    flat, out, at = packed.reshape(-1), {}, 0
    for n, size in SMALL:
        out[n] = flat[at:at + size].reshape(like[n].shape)
        at += size
    return out, flat[LOSS_AT]


def kernel(x, ffn1_norm, ffn1_w_in, ffn1_w_out, mix_norm, w_in_mix, w_pool, pool_scale, w_alpha, b_alpha, gla_norm, w_out_mix, ffn2_norm, ffn2_w_in, ffn2_w_out, final_norm, loss_target, m_ffn1_norm, m_ffn1_w_in, m_ffn1_w_out, m_mix_norm, m_w_in_mix, m_w_pool, m_pool_scale, m_w_alpha, m_b_alpha, m_gla_norm, m_w_out_mix, m_ffn2_norm, m_ffn2_w_in, m_ffn2_w_out, m_final_norm, v_ffn1_norm, v_ffn1_w_in, v_ffn1_w_out, v_mix_norm, v_w_in_mix, v_w_pool, v_pool_scale, v_w_alpha, v_b_alpha, v_gla_norm, v_w_out_mix, v_ffn2_norm, v_ffn2_w_in, v_ffn2_w_out, v_final_norm):
    names = ["ffn1_norm", "ffn1_w_in", "ffn1_w_out", "mix_norm", "w_in_mix", "w_pool", "pool_scale", "w_alpha", "b_alpha",
             "gla_norm", "w_out_mix", "ffn2_norm", "ffn2_w_in", "ffn2_w_out", "final_norm"]
    p = dict(zip(names, [ffn1_norm, ffn1_w_in, ffn1_w_out, mix_norm, w_in_mix, w_pool, pool_scale, w_alpha, b_alpha,
                         gla_norm, w_out_mix, ffn2_norm, ffn2_w_in, ffn2_w_out, final_norm]))
    m = dict(zip(names, [m_ffn1_norm, m_ffn1_w_in, m_ffn1_w_out, m_mix_norm, m_w_in_mix, m_w_pool, m_pool_scale, m_w_alpha,
                         m_b_alpha, m_gla_norm, m_w_out_mix, m_ffn2_norm, m_ffn2_w_in, m_ffn2_w_out, m_final_norm]))
    v = dict(zip(names, [v_ffn1_norm, v_ffn1_w_in, v_ffn1_w_out, v_mix_norm, v_w_in_mix, v_w_pool, v_pool_scale, v_w_alpha,
                         v_b_alpha, v_gla_norm, v_w_out_mix, v_ffn2_norm, v_ffn2_w_in, v_ffn2_w_out, v_final_norm]))

    mx, my, mc = _place()
    table = jnp.stack([_dev_index(*_rel_chip(mx, my, rel), mc) for rel in range(NCHIP)]).astype(jnp.int32)

    def landing(shard, kind):
        shape = (shard.shape[0], NDEV * shard.shape[1]) if kind == "bcast_cols" else (NDEV,) + shard.shape
        return lax.empty(shape, shard.dtype)

    def gather_begin(items, tag, after):
        kinds = [kind for _, kind in items]
        copies = _gather_copies(kinds)
        s, r, thru, tok = _start_copies(tag + "_start", [a for a, _ in items], [landing(a, kind) for a, kind in items], copies,
                                        (NCHIP + 1) * len(items), after)
        return (s, r, thru, copies, kinds, [a.shape[-1] for a, _ in items], tag), tok

    def gather_end(state, after):
        s, r, thru, copies, kinds, widths, tag = state
        _, lands = _wait_copies(tag + "_wait", s, r, thru, len(kinds), copies, after)
        return _pass_to_sibling(lands, kinds, widths, tag + "_pass")

    def shard16(n):
        return p[n][0].astype(BF16)

    g_w1in, g_w1out = _gather([(shard16("ffn1_w_in"), "bcast_cols"), (shard16("ffn1_w_out"), "bcast")], "gather_ffn1")
    mix_state, tok_m = gather_begin([(jnp.transpose(p["w_in_mix"][0]).astype(BF16), "bcast"), (shard16("w_out_mix"), "bcast"),
                                     (p["w_pool"][0].reshape(H * 32, PG), "bcast"), (p["w_alpha"][0], "bcast")], "gather_mix",
                                    (g_w1out,))
    ffn2_state, tok_f = gather_begin([(shard16("ffn2_w_in"), "bcast_cols"), (shard16("ffn2_w_out"), "bcast")], "gather_ffn2",
                                     (tok_m,))
    full = {"ffn1_w_in": g_w1in, "ffn1_w_out": g_w1out.reshape(FF, D), "final_norm": final_norm.reshape(1, D)}
    for n in ("ffn1_norm", "mix_norm", "ffn2_norm", "pool_scale", "b_alpha", "gla_norm"):
        full[n] = p[n]

    xs, tgt = x[0], loss_target[0]
    h1, n1, gu1 = _fwd_ffn1(xs, full, after=(tok_m, tok_f))
    g_wmix, g_wo, g_wpool, g_walpha = gather_end(mix_state, (h1,))
    walpha = jnp.transpose(g_walpha, (1, 0, 2)).reshape(RANK, H * DK)
    full.update({
        "w_in_mix": jnp.pad(g_wmix.reshape(D_IN, D), ((0, D_IN_PAD - D_IN), (0, 0))),
        "w_out_mix": g_wo.reshape(D, D),
        "w_pool": jnp.transpose(g_wpool.reshape(NDEV, H, 32, PG), (1, 0, 2, 3)).reshape(H, PG, PG).astype(BF16),
        "w_alpha": jnp.pad(walpha, ((0, RANK_PAD - RANK), (0, 0))).astype(BF16),
    })
    h2, sv = _fwd_mixer(h1, full)
    g_w2in, g_w2out = gather_end(ffn2_state, (h2,))
    full.update({"ffn2_w_in": g_w2in, "ffn2_w_out": g_w2out.reshape(FF, D)})
    dh3, d_final, loss_part, n3, gu3 = _fwd_ffn2_loss(h2, tgt, full)


    def slab_shape(a, kind):
        return (NCHIP,) + (a.shape[1:] if kind == "scatter" else (a.shape[0], a.shape[1] // NDEV))

    def pair_add_all(own, got, tag):
        return [_pair_add(a, kind, got_a, table, tr, "%s_pair_add_%d" % (tag, i))
                for i, ((a, kind, tr), got_a) in enumerate(zip(own, got))]

    def reduce_begin(own, tag, after=()):
        kinds = [kind for _, kind, _ in own]
        copies = _pair_copies(kinds)
        s, r, thru, tok = _start_copies(tag + "_pair_start", [a for a, _, _ in own],
                                        [lax.empty(slab_shape(a, kind), a.dtype) for a, kind, _ in own], copies,
                                        NCHIP * len(own), after)
        return dict(own=own, copies=copies, s=s, r=r, thru=thru, tag=tag), tok

    def reduce_middle(st, after):
        own, tag = st["own"], st["tag"]
        sent, got = _wait_copies(tag + "_pair_wait", st["s"], st["r"], st["thru"], len(own), st["copies"], after)
        pre = pair_add_all([(a, kind, tr) for a, (_, kind, tr) in zip(sent, own)], got, tag)
        st["s"], st["r"], st["thru"], tok = _start_copies(tag + "_chip_start", pre, [lax.empty(a.shape, a.dtype) for a in pre],
                                                          _chip_copies, (NCHIP - 1) * len(pre))
        return tok

    def reduce_end(st, after):
        n = len(st["own"])
        pre, land = _wait_copies(st["tag"] + "_chip_wait", st["s"], st["r"], st["thru"], n, _chip_copies, after)
        return [[(a, 0)] + [(b, rel) for rel in range(1, NCHIP)] for a, b in zip(pre, land)]

    def w_in_item(a):
        return (a, "scatter_cols", 512)

    def w_out_item(a):
        return (a.reshape(NDEV, WOUT_SHARD, D), "scatter", WOUT_SHARD)

    red, small = {}, {}

    def at(point, value):
        if point == "ffn2_w_out":
            red["w2out"], tok = reduce_begin([w_out_item(value)], "ffn2_w_out")
        elif point == "ffn2_w_in":
            tok_a = reduce_middle(red["w2out"], (value,))
            red["w2in"], tok = reduce_begin([w_in_item(value)], "ffn2_w_in", (tok_a,))
        elif point == "pool_bwd":
            tok = reduce_middle(red["w2in"], (value,))
        elif point == "mixer_weights":
            d_wmix8 = value["w_in_mix"][:D_IN].reshape(NDEV, MIX_SHARD, D)
            d_wpool8 = jnp.transpose(value["w_pool"].reshape(H, NDEV, 32, PG), (1, 0, 2, 3)).reshape(NDEV, H * 32, PG)
            d_walpha8 = jnp.transpose(value["w_alpha"][:RANK].reshape(RANK, NDEV, H * DK // NDEV), (1, 0, 2))
            red["mix"], tok = reduce_begin([(d_wmix8, "scatter", MIX_SHARD),
                                            (value["w_out_mix"].reshape(NDEV, D // NDEV, D), "scatter", D // NDEV),
                                            (d_wpool8, "scatter", H * 32), (d_walpha8, "scatter", RANK)], "mix")
        elif point == "dh1":
            tok = reduce_middle(red["mix"], (value,))
        elif point == "ffn1_w_out":
            red["w1out"], tok = reduce_begin([w_out_item(value)], "ffn1_w_out")
        elif point == "ffn1_dx_a":
            tok = reduce_middle(red["w1out"], (value,))
        elif point == "ffn1_w_in":
            red["w1in"], tok = reduce_begin([w_in_item(value)], "ffn1_w_in")
        return (tok,)

    dx, g = _backward(xs, h1, h2, n1, gu1, sv, n3, gu3, dh3, full, at)
    packed = _pack_small(dict(final_norm=d_final, **g), loss_part[0, 0])
    small_s, small_r, small_thru, tok_s = _start_copies("gather_small_start", [packed], [lax.empty((NDEV,) + packed.shape, F32)],
                                                        _all_copies, NDEV, (dx,))
    tok_c = reduce_middle(red["w1in"], (tok_s,))

    def upd(parts, n, shape2d, tr):
        res = _adamw(parts, p[n].reshape(shape2d), m[n].reshape(shape2d), v[n].reshape(shape2d), tr, "adamw_" + n)
        return [r.reshape(p[n].shape) for r in res]

    def transposed(a):
        return jnp.transpose(a[0])

    (p_w2out,) = reduce_end(red["w2out"], (tok_c,))
    (p_w2in,) = reduce_end(red["w2in"], (tok_c,))
    p_wmix, p_wo, p_wpool, p_walpha = reduce_end(red["mix"], (tok_c,))
    out = {
        "ffn2_w_in": upd(p_w2in, "ffn2_w_in", (D, WIN_SHARD), 128),
        "ffn2_w_out": upd(p_w2out, "ffn2_w_out", (WOUT_SHARD, D), 64),
        "w_out_mix": upd(p_wo, "w_out_mix", (D // NDEV, D), 64),
        "w_pool": upd(p_wpool, "w_pool", (H * 32, PG), H * 32),
        "w_alpha": upd(p_walpha, "w_alpha", (RANK, H * DK // NDEV), RANK),
    }
    out["w_in_mix"] = [jnp.transpose(r)[None] for r in
                       _adamw(p_wmix, transposed(p["w_in_mix"]), transposed(m["w_in_mix"]), transposed(v["w_in_mix"]),
                              MIX_SHARD, "adamw_w_in_mix", tc=512)]
    _, (r_small,) = _wait_copies("gather_small_wait", small_s, small_r, small_thru, 1, _all_copies,
                                 (out["w_in_mix"][3], out["ffn2_w_in"][3], out["ffn2_w_out"][3], out["w_out_mix"][3]))
    small_res = _adamw([(r_small, s) for s in range(NDEV)], _pack_small(p), _pack_small(m), _pack_small(v), SMALL_ROWS,
                       "adamw_small")
    (p_w1out,) = reduce_end(red["w1out"], (small_res[0],))
    out["ffn1_w_out"] = upd(p_w1out, "ffn1_w_out", (WOUT_SHARD, D), 64)
    (p_w1in,) = reduce_end(red["w1in"], (out["ffn1_w_out"][3],))
    out["ffn1_w_in"] = upd(p_w1in, "ffn1_w_in", (D, WIN_SHARD), 128)
    unpacked = [_unpack_small(r, p) for r in small_res]
    loss = unpacked[0][1]
    for n, _ in SMALL:
        out[n] = [u[0][n] for u in unpacked]

    return (loss, dx.reshape(1, S, D), *[out[n][0] for n in names], *[out[n][1] for n in names],
            *[out[n][2] for n in names], *[out[n][3] for n in names])
```

```python
import jax
import jax.numpy as jnp
from jax import lax
from jax.experimental import pallas as pl
from jax.experimental.pallas import tpu as pltpu

F32, BF16 = jnp.float32, jnp.bfloat16
MESH = pl.DeviceIdType.MESH
ANY = pl.BlockSpec(memory_space=pl.ANY)

NDEV = 8
S = 2048
D = 2048
FF = 5632
WIN_SHARD = 2 * FF // NDEV
WOUT_SHARD = FF // NDEV
D_POOL = 1024
PG = 256
POOL_WINDOWS = (2, 4, 8, 16)
H = 4
DK = 128
DV = 256
CH = 64
NCH = S // CH
RANK = 16
RANK_PAD = 128
D_IN = 4112
D_IN_PAD = 4224
MIX_SHARD = D_IN // NDEV
O_Q, O_K, O_V, O_G, O_R = 1024, 1536, 2048, 3072, 4096
GATE_NORM = 16.0
QK_SCALE = DK ** -0.5
EPS = 1e-6
ADAM_LR, ADAM_B1, ADAM_B2, ADAM_EPS, ADAM_WD, ADAM_STEP = 0.001, 0.9, 0.999, 1e-08, 0.01, 10
V7X_VMEM_BYTES = 64 << 20

SMALL = (("ffn1_norm", 2048), ("mix_norm", 2048), ("ffn2_norm", 2048), ("final_norm", 2048),
         ("pool_scale", 1024), ("b_alpha", 512), ("gla_norm", 256))
SMALL_ROWS = 80
LOSS_AT = sum(n for _, n in SMALL)


def _params(vmem_mb, sem=None):
    return pltpu.CompilerParams(dimension_semantics=sem, vmem_limit_bytes=min(vmem_mb << 20, V7X_VMEM_BYTES - (4 << 20)))


def _dot(a, b):
    return jnp.dot(a, b, preferred_element_type=F32)


def _dot_nt(a, b):
    return lax.dot_general(a, b, (((1,), (1,)), ((), ())), preferred_element_type=F32)


def _dot_tn(a, b):
    return lax.dot_general(a, b, (((0,), (0,)), ((), ())), preferred_element_type=F32)


def _sigmoid(x):
    return 0.5 * jnp.tanh(0.5 * x) + 0.5


def _log_sigmoid(x):
    return jnp.minimum(x, 0.0) - jnp.log(1.0 + jnp.exp(-jnp.abs(x)))


def _rms(x, g):
    r = lax.rsqrt(jnp.mean(x * x, axis=-1, keepdims=True) + EPS)
    return x * r * g


def _rms_bwd(dn, x, g):
    r = lax.rsqrt(jnp.mean(x * x, axis=-1, keepdims=True) + EPS)
    xh = x * r
    dxh = dn * g
    dx = r * (dxh - xh * jnp.mean(dxh * xh, axis=-1, keepdims=True))
    return dx, jnp.sum(dn * xh, axis=0, keepdims=True)


ROWS = 64


def _row_loop(total, fn, init=0):
    def step(t, carry):
        return fn(pl.ds(pl.multiple_of(t * ROWS, ROWS), ROWS), carry)
    return lax.fori_loop(0, total // ROWS, step, init)


def _split3(x):
    hi = x.astype(BF16)
    r1 = x - hi.astype(F32)
    mid = r1.astype(BF16)
    lo = (r1 - mid.astype(F32)).astype(BF16)
    return hi, mid, lo


def _tri_dot(tri_b, x):
    hi, mid, lo = _split3(x)
    return (_dot(tri_b, lo) + _dot(tri_b, mid)) + _dot(tri_b, hi)


FFN_TS, FFN_TF = 512, 512


def _ffn_specs():
    wg = pl.BlockSpec((D, FFN_TF), lambda i, j: (0, j))
    wu = pl.BlockSpec((D, FFN_TF), lambda i, j: (0, FF // FFN_TF + j))
    wo = pl.BlockSpec((FFN_TF, D), lambda i, j: (j, 0))
    row = pl.BlockSpec((FFN_TS, D), lambda i, j: (i, 0))
    vec = pl.BlockSpec((1, D), lambda i, j: (0, 0))
    gu = pl.BlockSpec((2, FFN_TS, FFN_TF), lambda i, j: (0, i, j))
    return wg, wu, wo, row, vec, gu


def _ordered_after(body, n_in, after):
    def wrapped(*refs):
        return body(*refs[:n_in], *refs[n_in + len(after):])
    return wrapped, [ANY] * len(after)


def _ffn_fwd(h, g, w_in8, w_out, name, after=()):
    nj = FF // FFN_TF
    wg, wu, wo, row, vec, gu = _ffn_specs()

    def body(h_ref, g_ref, wg_ref, wu_ref, wo_ref, ho_ref, n_ref, gu_ref, acc_ref):
        j = pl.program_id(1)

        @pl.when(j == 0)
        def _():
            def norm(rows, c):
                n_ref[rows, :] = _rms(h_ref[rows, :], g_ref[...]).astype(BF16)
                return c
            _row_loop(FFN_TS, norm)
            acc_ref[...] = jnp.zeros_like(acc_ref)

        n = n_ref[...]
        gate = _dot(n, wg_ref[...])
        up = _dot(n, wu_ref[...])
        gu_ref[0] = gate.astype(BF16)
        gu_ref[1] = up.astype(BF16)
        a = (gate * _sigmoid(gate)) * up
        acc_ref[...] += _dot(a.astype(BF16), wo_ref[...])

        @pl.when(j == nj - 1)
        def _():
            def residual(rows, c):
                ho_ref[rows, :] = h_ref[rows, :] + 0.5 * acc_ref[rows, :]
                return c
            _row_loop(FFN_TS, residual)

    body, extra = _ordered_after(body, 5, after)
    return pl.pallas_call(
        body, grid=(S // FFN_TS, nj), name=name,
        in_specs=[row, vec, wg, wu, wo] + extra, out_specs=[row, row, gu],
        out_shape=[jax.ShapeDtypeStruct((S, D), F32), jax.ShapeDtypeStruct((S, D), BF16),
                   jax.ShapeDtypeStruct((2, S, FF), BF16)],
        scratch_shapes=[pltpu.VMEM((FFN_TS, D), F32)],
        compiler_params=_params(56, ("arbitrary", "arbitrary")),
    )(h, g, w_in8, w_in8, w_out, *after)


def _ffn_bwd_x(dhp, h, g, gu_arr, w_in8, w_out, name, after=()):
    ni, nj = S // FFN_TS, FF // FFN_TF
    wg, wu, wo, row, vec, gu = _ffn_specs()
    act = pl.BlockSpec((FFN_TS, FFN_TF), lambda i, j: (i, j))

    def body(dhp_ref, h_ref, g_ref, gu_ref, wg_ref, wu_ref, wo_ref,
             dgu_ref, a_ref, df_ref, dh_ref, dhb_ref, dg_ref, acc_ref):
        i, j = pl.program_id(0), pl.program_id(1)

        @pl.when(j == 0)
        def _():
            def half(rows, c):
                df_ref[rows, :] = (0.5 * dhp_ref[rows, :]).astype(BF16)
                return c
            _row_loop(FFN_TS, half)
            acc_ref[...] = jnp.zeros_like(acc_ref)

        gate = gu_ref[0].astype(F32)
        up = gu_ref[1].astype(F32)
        da = _dot_nt(df_ref[...], wo_ref[...])
        sg = _sigmoid(gate)
        silu = gate * sg
        dgate = (da * up * (sg * (1.0 + gate * (1.0 - sg)))).astype(BF16)
        dup = (da * silu).astype(BF16)
        a_ref[...] = (silu * up).astype(BF16)
        dgu_ref[0] = dgate
        dgu_ref[1] = dup
        acc_ref[...] += _dot_nt(dgate, wg_ref[...]) + _dot_nt(dup, wu_ref[...])

        @pl.when(j == nj - 1)
        def _():
            def norm_bwd(rows, dg):
                dx, dg_rows = _rms_bwd(acc_ref[rows, :], h_ref[rows, :], g_ref[...])
                dh = dhp_ref[rows, :] + dx
                dh_ref[rows, :] = dh
                dhb_ref[rows, :] = dh.astype(BF16)
                return dg + dg_rows
            dg = _row_loop(FFN_TS, norm_bwd, jnp.zeros((1, D), F32))

            @pl.when(i == 0)
            def _():
                dg_ref[...] = dg

            @pl.when(i > 0)
            def _():
                dg_ref[...] += dg

    body, extra = _ordered_after(body, 7, after)
    return pl.pallas_call(
        body, grid=(ni, nj), name=name,
        in_specs=[row, row, vec, gu, wg, wu, wo] + extra,
        out_specs=[gu, act, row, row, row, vec],
        out_shape=[jax.ShapeDtypeStruct((2, S, FF), BF16), jax.ShapeDtypeStruct((S, FF), BF16),
                   jax.ShapeDtypeStruct((S, D), BF16), jax.ShapeDtypeStruct((S, D), F32),
                   jax.ShapeDtypeStruct((S, D), BF16), jax.ShapeDtypeStruct((1, D), F32)],
        scratch_shapes=[pltpu.VMEM((FFN_TS, D), F32)],
        compiler_params=_params(58, ("arbitrary", "arbitrary")),
    )(dhp, h, g, gu_arr, w_in8, w_in8, w_out, *after)


def _ffn_bwd_gu(dhp, gu_arr, w_out, name, after=()):
    _, _, wo, row, _, gu = _ffn_specs()
    act = pl.BlockSpec((FFN_TS, FFN_TF), lambda i, j: (i, j))

    def body(dhp_ref, gu_ref, wo_ref, dgu_ref, a_ref, df_ref):
        @pl.when(pl.program_id(1) == 0)
        def _():
            def half(rows, c):
                df_ref[rows, :] = (0.5 * dhp_ref[rows, :]).astype(BF16)
                return c
            _row_loop(FFN_TS, half)

        gate = gu_ref[0].astype(F32)
        up = gu_ref[1].astype(F32)
        da = _dot_nt(df_ref[...], wo_ref[...])
        sg = _sigmoid(gate)
        silu = gate * sg
        dgu_ref[0] = (da * up * (sg * (1.0 + gate * (1.0 - sg)))).astype(BF16)
        dgu_ref[1] = (da * silu).astype(BF16)
        a_ref[...] = (silu * up).astype(BF16)

    body, extra = _ordered_after(body, 3, after)
    return pl.pallas_call(
        body, grid=(S // FFN_TS, FF // FFN_TF), name=name,
        in_specs=[row, gu, wo] + extra, out_specs=[gu, act, row],
        out_shape=[jax.ShapeDtypeStruct((2, S, FF), BF16), jax.ShapeDtypeStruct((S, FF), BF16),
                   jax.ShapeDtypeStruct((S, D), BF16)],
        compiler_params=_params(40, ("arbitrary", "arbitrary")),
    )(dhp, gu_arr, w_out, *after)


def _ffn_bwd_dx(dgu, dhp, h, g, w_in, first_tile, ntiles, name, after=(), into=None):
    nj = FF // FFN_TF
    wg, wu, _, _, vec, _ = _ffn_specs()
    row_in = pl.BlockSpec((FFN_TS, D), lambda i, j: (first_tile + i, 0))
    dgu_spec = pl.BlockSpec((2, FFN_TS, FFN_TF), lambda i, j: (0, first_tile + i, j))
    after = tuple(after) + (() if into is None else (into,))

    def body(dgu_ref, dhp_ref, h_ref, g_ref, wg_ref, wu_ref, dh_ref, dg_ref, acc_ref):
        i, j = pl.program_id(0), pl.program_id(1)

        @pl.when(j == 0)
        def _():
            acc_ref[...] = jnp.zeros_like(acc_ref)

        acc_ref[...] += _dot_nt(dgu_ref[0], wg_ref[...]) + _dot_nt(dgu_ref[1], wu_ref[...])

        @pl.when(j == nj - 1)
        def _():
            def norm_bwd(rows, dg):
                dx, dg_rows = _rms_bwd(acc_ref[rows, :], h_ref[rows, :], g_ref[...])
                dh_ref[rows, :] = dhp_ref[rows, :] + dx
                return dg + dg_rows
            dg = _row_loop(FFN_TS, norm_bwd, jnp.zeros((1, D), F32))

            @pl.when(i == 0)
            def _():
                dg_ref[...] = dg

            @pl.when(i > 0)
            def _():
                dg_ref[...] += dg

    body, extra = _ordered_after(body, 6, after)
    return pl.pallas_call(
        body, grid=(ntiles, nj), name=name,
        in_specs=[dgu_spec, row_in, row_in, vec, wg, wu] + extra, out_specs=[row_in, vec],
        out_shape=[jax.ShapeDtypeStruct((S, D), F32), jax.ShapeDtypeStruct((1, D), F32)],
        input_output_aliases={} if into is None else {6 + len(after) - 1: 0},
        scratch_shapes=[pltpu.VMEM((FFN_TS, D), F32)],
        compiler_params=_params(48, ("arbitrary", "arbitrary")),
    )(dgu, dhp, h, g, w_in, w_in, *after)


def _tn_matmul(a, b, a_spec, b_spec, out_shape, out_spec, grid, name, vmem_mb, after=()):
    def body(a_ref, b_ref, o_ref):
        o_ref[...] = _dot_tn(a_ref[...], b_ref[...]).astype(o_ref.dtype)

    body, extra = _ordered_after(body, 2, after)
    return pl.pallas_call(body, grid=grid, name=name, in_specs=[a_spec, b_spec] + extra, out_specs=out_spec,
                          out_shape=out_shape, compiler_params=_params(vmem_mb))(a, b, *after)


def _resident(shape):
    return pl.BlockSpec(shape, lambda i: (0,) * len(shape), pipeline_mode=pl.Buffered(1))


def _norm_matmul(h, g, w, name, after=(), ts=256):
    n_out = w.shape[0]

    def body(h_ref, g_ref, w_ref, u_ref, n_ref):
        def norm(rows, c):
            n_ref[rows, :] = _rms(h_ref[rows, :], g_ref[...]).astype(BF16)
            return c
        _row_loop(ts, norm)
        u_ref[...] = _dot_nt(n_ref[...], w_ref[...])

    body, extra = _ordered_after(body, 3, after)
    return pl.pallas_call(
        body, grid=(S // ts,), name=name,
        in_specs=[pl.BlockSpec((ts, D), lambda i: (i, 0)), pl.BlockSpec((1, D), lambda i: (0, 0)), _resident(w.shape)] + extra,
        out_specs=[pl.BlockSpec((ts, n_out), lambda i: (i, 0)), pl.BlockSpec((ts, D), lambda i: (i, 0))],
        out_shape=[jax.ShapeDtypeStruct((S, n_out), F32), jax.ShapeDtypeStruct((S, D), BF16)],
        compiler_params=_params(48, ("arbitrary",)),
    )(h, g, w, *after)


def _matmul_residual(a, w, res, name, after=(), ts=512):
    k, n_out = w.shape

    def body(a_ref, w_ref, r_ref, o_ref):
        o_ref[...] = r_ref[...] + _dot(a_ref[...], w_ref[...])

    body, extra = _ordered_after(body, 3, after)
    return pl.pallas_call(
        body, grid=(S // ts,), name=name,
        in_specs=[pl.BlockSpec((ts, k), lambda i: (i, 0)), _resident(w.shape), pl.BlockSpec((ts, n_out), lambda i: (i, 0))] + extra,
        out_specs=pl.BlockSpec((ts, n_out), lambda i: (i, 0)),
        out_shape=jax.ShapeDtypeStruct((S, n_out), F32),
        compiler_params=_params(40),
    )(a, w, res, *after)


def _nt_matmul(a, w, name, after=(), ts=512):
    n_out, k = w.shape

    def body(a_ref, w_ref, o_ref):
        o_ref[...] = _dot_nt(a_ref[...], w_ref[...])

    body, extra = _ordered_after(body, 2, after)
    return pl.pallas_call(
        body, grid=(S // ts,), name=name,
        in_specs=[pl.BlockSpec((ts, k), lambda i: (i, 0)), _resident(w.shape)] + extra,
        out_specs=pl.BlockSpec((ts, n_out), lambda i: (i, 0)),
        out_shape=jax.ShapeDtypeStruct((S, n_out), F32),
        compiler_params=_params(40),
    )(a, w, *after)


def _matmul_normbwd(du, w, h, g, dres, name, after=(), ts=256):
    n_in = w.shape[0]
    row = pl.BlockSpec((ts, D), lambda i: (i, 0))
    vec = pl.BlockSpec((1, D), lambda i: (0, 0))

    def body(du_ref, w_ref, h_ref, g_ref, dres_ref, dh_ref, dg_ref, acc_ref):
        i = pl.program_id(0)
        acc_ref[...] = _dot(du_ref[...], w_ref[...])

        def norm_bwd(rows, dg):
            dx, dg_rows = _rms_bwd(acc_ref[rows, :], h_ref[rows, :], g_ref[...])
            dh_ref[rows, :] = dres_ref[rows, :] + dx
            return dg + dg_rows
        dg = _row_loop(ts, norm_bwd, jnp.zeros((1, D), F32))

        @pl.when(i == 0)
        def _():
            dg_ref[...] = dg

        @pl.when(i > 0)
        def _():
            dg_ref[...] += dg

    body, extra = _ordered_after(body, 5, after)
    return pl.pallas_call(
        body, grid=(S // ts,), name=name,
        in_specs=[pl.BlockSpec((ts, n_in), lambda i: (i, 0)), _resident(w.shape), row, vec, row] + extra,
        out_specs=[row, vec],
        out_shape=[jax.ShapeDtypeStruct((S, D), F32), jax.ShapeDtypeStruct((1, D), F32)],
        scratch_shapes=[pltpu.VMEM((ts, D), F32)],
        compiler_params=_params(52, ("arbitrary",)),
    )(du, w, h, g, dres, *after)


def _loss_head(h, g, tgt, ts=256):
    row = pl.BlockSpec((ts, D), lambda i: (i, 0))
    vec = pl.BlockSpec((1, D), lambda i: (0, 0))

    def body(h_ref, g_ref, t_ref, dh_ref, dg_ref, loss_ref):
        i = pl.program_id(0)

        def rows_fn(rows, carry):
            dg, part = carry
            x = h_ref[rows, :]
            gv = g_ref[...]
            err = _rms(x, gv) - t_ref[rows, :]
            part = part + 0.5 * jnp.sum(jnp.mean(err * err, axis=-1, keepdims=True), axis=0, keepdims=True)
            dx, dg_rows = _rms_bwd(err * (1.0 / D), x, gv)
            dh_ref[rows, :] = dx
            return dg + dg_rows, part
        dg, part = _row_loop(ts, rows_fn, (jnp.zeros((1, D), F32), jnp.zeros((1, 1), F32)))

        @pl.when(i == 0)
        def _():
            dg_ref[...] = dg
            loss_ref[...] = jnp.broadcast_to(part, loss_ref.shape)

        @pl.when(i > 0)
        def _():
            dg_ref[...] += dg
            loss_ref[...] += jnp.broadcast_to(part, loss_ref.shape)

    return pl.pallas_call(
        body, grid=(S // ts,), name="loss_head",
        in_specs=[row, vec, row], out_specs=[row, vec, pl.BlockSpec((1, 128), lambda i: (0, 0))],
        out_shape=[jax.ShapeDtypeStruct((S, D), F32), jax.ShapeDtypeStruct((1, D), F32),
                   jax.ShapeDtypeStruct((1, 128), F32)],
        compiler_params=_params(40, ("arbitrary",)),
    )(h, g, tgt)


def _pool_specs():
    blk = pl.BlockSpec((S, PG), lambda gi: (0, gi))
    wp = pl.BlockSpec((None, PG, PG), lambda gi: (gi, 0, 0))
    sc = pl.BlockSpec((1, PG), lambda gi: (0, gi))
    return blk, wp, sc


def _pool_fwd(u, wp_b, scale):
    blk, wp, sc = _pool_specs()

    def body(u_ref, wp_ref, sc_ref, y_ref, pooled_ref):
        win = 2 << pl.program_id(0)
        row = lax.broadcasted_iota(jnp.int32, (S, PG), 0)
        x = u_ref[...]
        s = x
        for k in (1, 2, 4, 8):
            s = s + jnp.where((row >= k) & (k < win), pltpu.roll(s, k, 0), 0.0)
        cnt = jnp.minimum(row + 1, win).astype(F32)
        pooled = (s / cnt - x).astype(BF16)
        pooled_ref[...] = pooled
        y_ref[...] = (_dot(pooled, wp_ref[...]) * sc_ref[...]).astype(BF16)

    return pl.pallas_call(
        body, grid=(len(POOL_WINDOWS),), name="pool_fwd", in_specs=[blk, wp, sc], out_specs=[blk, blk],
        out_shape=[jax.ShapeDtypeStruct((S, D_POOL), BF16), jax.ShapeDtypeStruct((S, D_POOL), BF16)],
        compiler_params=_params(40),
    )(u, wp_b, scale)


def _pool_bwd(dy, pooled, wp_b, scale):
    blk, wp, sc = _pool_specs()

    def body(dy_ref, p_ref, wp_ref, sc_ref, du_ref, dwp_ref, dsc_ref):
        win = 2 << pl.program_id(0)
        row = lax.broadcasted_iota(jnp.int32, (S, PG), 0)
        dyv = dy_ref[...]
        pooled = p_ref[...]
        w = wp_ref[...]
        dsc_ref[...] = jnp.sum(dyv * _dot(pooled, w), axis=0, keepdims=True)
        dz = (dyv * sc_ref[...]).astype(BF16)
        dwp_ref[...] = _dot_tn(pooled, dz)
        dpooled = _dot_nt(dz, w)
        cnt = jnp.minimum(row + 1, win).astype(F32)
        fs = dpooled / cnt
        for k in (1, 2, 4, 8):
            fs = fs + jnp.where((row < S - k) & (k < win), pltpu.roll(fs, S - k, 0), 0.0)
        du_ref[...] = (fs - dpooled).astype(BF16)

    return pl.pallas_call(
        body, grid=(len(POOL_WINDOWS),), name="pool_bwd", in_specs=[blk, blk, wp, sc], out_specs=[blk, wp, sc],
        out_shape=[jax.ShapeDtypeStruct((S, D_POOL), BF16), jax.ShapeDtypeStruct((len(POOL_WINDOWS), PG, PG), F32),
                   jax.ShapeDtypeStruct((1, D_POOL), F32)],
        compiler_params=_params(40),
    )(dy, pooled, wp_b, scale)


def _gla_in_specs(chunk_of):
    def at(width, col):
        return pl.BlockSpec((CH, width), lambda n: (chunk_of(n), col))
    return [at(H * DK, O_Q // (H * DK)), at(H * DK, O_K // (H * DK)), at(H * DV, O_V // (H * DV)),
            at(H * DV, O_G // (H * DV)), at(RANK_PAD, O_R // RANK_PAD)]


def _gla_decay_terms(lr_ref, wa_ref, ba_ref, q_ref, k_ref):
    row = lax.broadcasted_iota(jnp.int32, (CH, CH), 0)
    col = lax.broadcasted_iota(jnp.int32, (CH, CH), 1)
    tril = row >= col
    z = _dot(lr_ref[...].astype(BF16), wa_ref[...]) + ba_ref[...]
    la = _log_sigmoid(z) / GATE_NORM
    b = _tri_dot(jnp.where(tril, 1.0, 0.0).astype(BF16), la)
    bl = jnp.sum(la, axis=0, keepdims=True)
    e_b, e_nb, e_tb = jnp.exp(b), jnp.exp(-b), jnp.exp(bl - b)
    kk = k_ref[...]
    q_dec = (q_ref[...] * QK_SCALE) * e_b
    return tril, z, e_b, e_nb, e_tb, jnp.exp(bl), q_dec, kk * e_nb, kk * e_tb


def _gla_fwd(u, wa_b, ba, gn):
    wide = pl.BlockSpec((CH, H * DV), lambda n: (n, 0))

    def body(q_ref, k_ref, v_ref, g_ref, lr_ref, wa_ref, ba_ref, gn_ref, y_ref, o_ref, st_ref, state):
        @pl.when(pl.program_id(0) == 0)
        def _():
            state[...] = jnp.zeros_like(state)

        tril, _, _, _, _, dec, q_dec, k_inv, k_tail = _gla_decay_terms(lr_ref, wa_ref, ba_ref, q_ref, k_ref)
        for hd in range(H):
            ks, vs = slice(hd * DK, (hd + 1) * DK), slice(hd * DV, (hd + 1) * DV)
            qb, kib, ktb = q_dec[:, ks].astype(BF16), k_inv[:, ks].astype(BF16), k_tail[:, ks].astype(BF16)
            vb = v_ref[:, vs].astype(BF16)
            p = jnp.where(tril, _dot_nt(qb, kib), 0.0)
            st = state[hd]
            o = _dot(p.astype(BF16), vb) + _dot_nt(qb, st.astype(BF16))
            st_ref[hd] = st
            state[hd] = st * dec[:, ks] + _dot_tn(vb, ktb)
            o_ref[:, vs] = o
            on = _rms(o, gn_ref[...])
            gg = g_ref[:, vs]
            y_ref[:, vs] = (on * (gg * _sigmoid(gg))).astype(BF16)

    return pl.pallas_call(
        body, grid=(NCH,), name="gla_fwd",
        in_specs=_gla_in_specs(lambda n: n) + [pl.BlockSpec((RANK_PAD, H * DK), lambda n: (0, 0)),
                                               pl.BlockSpec((1, H * DK), lambda n: (0, 0)),
                                               pl.BlockSpec((1, DV), lambda n: (0, 0))],
        out_specs=[wide, wide, pl.BlockSpec((None, H, DV, DK), lambda n: (n, 0, 0, 0))],
        out_shape=[jax.ShapeDtypeStruct((S, H * DV), BF16), jax.ShapeDtypeStruct((S, H * DV), F32),
                   jax.ShapeDtypeStruct((NCH, H, DV, DK), F32)],
        scratch_shapes=[pltpu.VMEM((H, DV, DK), F32)],
        compiler_params=_params(32, ("arbitrary",)),
    )(u, u, u, u, u, wa_b, ba, gn)


GLA_DU = 2 * H * DK + 2 * H * DV + RANK_PAD


def _gla_bwd(u, o_arr, states, dy, wa_b, ba, gn, after=()):
    rev = lambda n: NCH - 1 - n
    wide = pl.BlockSpec((CH, H * DV), lambda n: (rev(n), 0))

    def body(q_ref, k_ref, v_ref, g_ref, lr_ref, o_ref, st_ref, dy_ref, wa_ref, ba_ref, gn_ref,
             du_ref, dwa_ref, dba_ref, dgn_ref, gstate, db_scr, dbl_scr):
        @pl.when(pl.program_id(0) == 0)
        def _():
            gstate[...] = jnp.zeros_like(gstate)
            dwa_ref[...] = jnp.zeros_like(dwa_ref)
            dba_ref[...] = jnp.zeros_like(dba_ref)
            dgn_ref[...] = jnp.zeros_like(dgn_ref)

        tril, z, e_b, e_nb, e_tb, dec, q_dec, k_inv, k_tail = _gla_decay_terms(lr_ref, wa_ref, ba_ref, q_ref, k_ref)
        gnv = gn_ref[...]
        dgn = jnp.zeros((1, DV), F32)
        for hd in range(H):
            ks, vs = slice(hd * DK, (hd + 1) * DK), slice(hd * DV, (hd + 1) * DV)
            qh, kih, kth = q_dec[:, ks], k_inv[:, ks], k_tail[:, ks]
            qb, kib, ktb = qh.astype(BF16), kih.astype(BF16), kth.astype(BF16)
            vb = v_ref[:, vs].astype(BF16)
            o = o_ref[:, vs]
            gg = g_ref[:, vs]
            dyh = dy_ref[:, vs]
            r = lax.rsqrt(jnp.mean(o * o, axis=-1, keepdims=True) + EPS)
            xh = o * r
            sg = _sigmoid(gg)
            dgate = dyh * (xh * gnv) * (sg * (1.0 + gg * (1.0 - sg)))
            don = dyh * (gg * sg)
            dgn = dgn + jnp.sum(don * xh, axis=0, keepdims=True)
            dxh = don * gnv
            d_o = (r * (dxh - xh * jnp.mean(dxh * xh, axis=-1, keepdims=True))).astype(BF16)
            pb = jnp.where(tril, _dot_nt(qb, kib), 0.0).astype(BF16)
            dpb = jnp.where(tril, _dot_nt(d_o, vb), 0.0).astype(BF16)
            gt = gstate[hd]
            gtb = gt.astype(BF16)
            st = st_ref[hd]
            dv = _dot_tn(pb, d_o) + _dot_nt(ktb, gtb)
            dq_dec = _dot(dpb, kib) + _dot(d_o, st.astype(BF16))
            dk_inv = _dot_tn(dpb, qb)
            dk_tail = _dot(vb, gtb)
            ddec = jnp.sum(gt * st, axis=0, keepdims=True)
            gstate[hd] = _dot_tn(d_o, qb) + dec[:, ks] * gt
            du_ref[:, ks] = (dq_dec * QK_SCALE * e_b[:, ks]).astype(BF16)
            du_ref[:, H * DK + hd * DK:H * DK + (hd + 1) * DK] = (dk_inv * e_nb[:, ks] + dk_tail * e_tb[:, ks]).astype(BF16)
            du_ref[:, 2 * H * DK + hd * DV:2 * H * DK + (hd + 1) * DV] = dv.astype(BF16)
            du_ref[:, 2 * H * DK + H * DV + hd * DV:2 * H * DK + H * DV + (hd + 1) * DV] = dgate.astype(BF16)
            db_scr[:, ks] = dq_dec * qh - dk_inv * kih - dk_tail * kth
            dbl_scr[:, ks] = jnp.sum(dk_tail * kth, axis=0, keepdims=True) + ddec * dec[:, ks]
        dgn_ref[...] += dgn
        row = lax.broadcasted_iota(jnp.int32, (CH, CH), 0)
        col = lax.broadcasted_iota(jnp.int32, (CH, CH), 1)
        dla = _tri_dot(jnp.where(row <= col, 1.0, 0.0).astype(BF16), db_scr[...]) + dbl_scr[...]
        dz = dla * (1.0 / GATE_NORM) * _sigmoid(-z)
        dzb = dz.astype(BF16)
        du_ref[:, GLA_DU - RANK_PAD:] = _dot_nt(dzb, wa_ref[...]).astype(BF16)
        dwa_ref[...] += _dot_tn(lr_ref[...].astype(BF16), dzb)
        dba_ref[...] += jnp.sum(dz, axis=0, keepdims=True)

    full = lambda shape: pl.BlockSpec(shape, lambda n: (0,) * len(shape))
    body, extra = _ordered_after(body, 11, after)
    return pl.pallas_call(
        body, grid=(NCH,), name="gla_bwd",
        in_specs=_gla_in_specs(rev) + [wide, pl.BlockSpec((None, H, DV, DK), lambda n: (rev(n), 0, 0, 0)),
                                       pl.BlockSpec((CH, H * DV), lambda n: (rev(n), 1)),
                                       full((RANK_PAD, H * DK)), full((1, H * DK)), full((1, DV))] + extra,
        out_specs=[pl.BlockSpec((CH, GLA_DU), lambda n: (rev(n), 0)), full((RANK_PAD, H * DK)), full((1, H * DK)),
                   full((1, DV))],
        out_shape=[jax.ShapeDtypeStruct((S, GLA_DU), BF16), jax.ShapeDtypeStruct((RANK_PAD, H * DK), F32),
                   jax.ShapeDtypeStruct((1, H * DK), F32), jax.ShapeDtypeStruct((1, DV), F32)],
        scratch_shapes=[pltpu.VMEM((H, DV, DK), F32), pltpu.VMEM((CH, H * DK), F32), pltpu.VMEM((1, H * DK), F32)],
        compiler_params=_params(32, ("arbitrary",)),
    )(u, u, u, u, u, o_arr, states, dy, wa_b, ba, gn, *after)


def _ffn_dw_in(n, dgu, tag, after=()):
    return _tn_matmul(n, dgu, pl.BlockSpec((S, 512), lambda s, m: (0, m)),
                      pl.BlockSpec((None, S, WIN_SHARD), lambda s, m: (s // (NDEV // 2), 0, s % (NDEV // 2))),
                      jax.ShapeDtypeStruct((D, 2 * FF), BF16), pl.BlockSpec((512, WIN_SHARD), lambda s, m: (m, s)),
                      (NDEV, D // 512), tag + "_dw_in", 32, after)


def _ffn_dw_out(act, df, tag, after=()):
    return _tn_matmul(act, df, pl.BlockSpec((S, 512), lambda m: (0, m)), pl.BlockSpec((S, D), lambda m: (0, 0)),
                      jax.ShapeDtypeStruct((FF, D), BF16), pl.BlockSpec((512, D), lambda m: (m, 0)),
                      (FF // 512,), tag + "_dw_out", 40, after)


def _fwd_ffn1(x, w, after=()):
    return _ffn_fwd(x, w["ffn1_norm"], w["ffn1_w_in"], w["ffn1_w_out"], "ffn1_fwd", after)


def _fwd_mixer(h1, w, after=(), at=lambda point, value: ()):
    u, n2 = _norm_matmul(h1, w["mix_norm"], w["w_in_mix"], "mix_in", after)
    y_pool, pooled = _pool_fwd(u, w["w_pool"], w["pool_scale"])
    y_gla, o_gla, states = _gla_fwd(u, w["w_alpha"], w["b_alpha"], w["gla_norm"])
    y = jnp.concatenate([y_pool, y_gla], axis=1)
    h2 = _matmul_residual(y, w["w_out_mix"], h1, "mix_out", at("gla_fwd", y))
    return h2, dict(u=u, n2=n2, pooled=pooled, o_gla=o_gla, states=states, y=y)


def _fwd_ffn2_loss(h2, tgt, w, after=()):
    h3, n3, gu3 = _ffn_fwd(h2, w["ffn2_norm"], w["ffn2_w_in"], w["ffn2_w_out"], "ffn2_fwd", after)
    dh3, d_final, loss = _loss_head(h3, w["final_norm"], tgt)
    return dh3, d_final, loss, n3, gu3


def _backward(x, h1, h2, n1, gu1, sv, n3, gu3, dh3, w, at=lambda point, value: ()):
    g = {}
    dgu3, act3, df3, dh2, dh2b, g["ffn2_norm"] = _ffn_bwd_x(dh3, h2, w["ffn2_norm"], gu3, w["ffn2_w_in"], w["ffn2_w_out"], "ffn2_bwd")
    g["ffn2_w_out"] = _ffn_dw_out(act3, df3, "ffn2")
    g["ffn2_w_in"] = _ffn_dw_in(n3, dgu3, "ffn2", at("ffn2_w_out", g["ffn2_w_out"]))
    dy = _nt_matmul(dh2b, w["w_out_mix"], "mix_out_bwd", at("ffn2_w_in", g["ffn2_w_in"]))
    g["w_out_mix"] = _tn_matmul(sv["y"], dh2b, pl.BlockSpec((S, 512), lambda m: (0, m)), pl.BlockSpec((S, D), lambda m: (0, 0)),
                                jax.ShapeDtypeStruct((D, D), BF16), pl.BlockSpec((512, D), lambda m: (m, 0)), (D // 512,),
                                "mix_out_dw", 40)
    du_pool, g["w_pool"], g["pool_scale"] = _pool_bwd(dy, sv["pooled"], w["w_pool"], w["pool_scale"])
    du_gla, g["w_alpha"], g["b_alpha"], g["gla_norm"] = _gla_bwd(sv["u"], sv["o_gla"], sv["states"], dy, w["w_alpha"], w["b_alpha"],
                                                                 w["gla_norm"], at("pool_bwd", du_pool))
    du = jnp.concatenate([du_pool, du_gla], axis=1)
    g["w_in_mix"] = _tn_matmul(du, sv["n2"], pl.BlockSpec((S, 1408), lambda j, m: (0, j)), pl.BlockSpec((S, 512), lambda j, m: (0, m)),
                               jax.ShapeDtypeStruct((D_IN_PAD, D), BF16), pl.BlockSpec((1408, 512), lambda j, m: (j, m)),
                               (D_IN_PAD // 1408, D // 512), "mix_in_dw", 32)
    dh1, g["mix_norm"] = _matmul_normbwd(du, w["w_in_mix"], h1, w["mix_norm"], dh2, "mix_in_bwd", at("mixer_weights", g))
    dgu1, act1, df1 = _ffn_bwd_gu(dh1, gu1, w["ffn1_w_out"], "ffn1_bwd_gu", at("dh1", dh1))
    g["ffn1_w_out"] = _ffn_dw_out(act1, df1, "ffn1")
    half = S // FFN_TS // 2
    dx, dn_a = _ffn_bwd_dx(dgu1, dh1, x, w["ffn1_norm"], w["ffn1_w_in"], 0, half, "ffn1_bwd_dx_a",
                           at("ffn1_w_out", g["ffn1_w_out"]))
    g["ffn1_w_in"] = _ffn_dw_in(n1, dgu1, "ffn1", at("ffn1_dx_a", dx))
    dx, dn_b = _ffn_bwd_dx(dgu1, dh1, x, w["ffn1_norm"], w["ffn1_w_in"], half, half, "ffn1_bwd_dx_b",
                           at("ffn1_w_in", g["ffn1_w_in"]), into=dx)
    g["ffn1_norm"] = dn_a + dn_b
    return dx, g


def _local_step(x, tgt, w):
    h1, n1, gu1 = _fwd_ffn1(x, w)
    h2, sv = _fwd_mixer(h1, w)
    dh3, d_final, loss, n3, gu3 = _fwd_ffn2_loss(h2, tgt, w)
    dx, g = _backward(x, h1, h2, n1, gu1, sv, n3, gu3, dh3, w)
    return loss, dx, dict(final_norm=d_final, **g)


def _coords(p):
    return (p // 4, (p // 2) % 2, p % 2)


NCHIP = 4


def _place():
    return lax.axis_index("x"), lax.axis_index("y"), lax.axis_index("c")


def _rel_chip(x, y, rel):
    return ((1 - x) if rel & 1 else x, (1 - y) if rel & 2 else y)


def _dev_index(x, y, c):
    return 4 * x + 2 * y + c


def _cols(ref, p, width):
    return ref.at[:, pl.ds(pl.multiple_of(p * width, 128), width)]


def _sems(na, n):
    return [pltpu.SemaphoreType.DMA((na, n)), pltpu.SemaphoreType.DMA((na, n)), pltpu.SemaphoreType.DMA((na,))]


def _gather(items, name):
    arrays = [a for a, _ in items]
    kinds = [k for _, k in items]
    na = len(arrays)
    out_shape = [jax.ShapeDtypeStruct((NDEV,) + a.shape if k == "bcast" else (a.shape[0], NDEV * a.shape[1]), a.dtype)
                 for a, k in items]

    def body(*refs):
        ins, outs = refs[:na], refs[na:2 * na]
        send_sems, recv_sems, local_sems = refs[2 * na:]
        x, y, c = _place()
        sibling = (x, y, 1 - c)
        here, over_x, over_y, across = (x, y), (1 - x, y), (x, 1 - y), (1 - x, 1 - y)

        def half(ref, h):
            rows = ref.shape[0] // 2
            return ref.at[pl.ds(h * rows, rows), :]

        def slab(a, chip, core, h=None):
            ref = _slab(outs[a], kinds[a], _dev_index(*chip, core), ins[a].shape[1])
            return ref if h is None else half(ref, h)

        def copy(a, k, src, dst, to):
            return pltpu.make_async_remote_copy(src, dst, send_sems.at[a, k], recv_sems.at[a, k], device_id=to, device_id_type=MESH)

        sent = []

        def send(a, k, src, dst, to):
            sent.append(copy(a, k, src, dst, to))
            sent[-1].start()

        def arrived(a, k, chip, core, h=None):
            ref = slab(a, chip, core, h)
            copy(a, k, ref, ref, sibling).wait_recv()
            return ref

        local = [pltpu.make_async_copy(ins[a], slab(a, here, c), local_sems.at[a]) for a in range(na)]
        for cp in local:
            cp.start()
        for k, h, chip in ((1, 0, over_x), (4, 1, over_y), (2, 1, over_x), (5, 0, over_y)):
            for a in range(na):
                send(a, k, half(ins[a], h), slab(a, here, c, h), (*chip, c))
        for a in range(na):
            send(a, 0, ins[a], slab(a, here, c), sibling)
        for k, chip, h, onward, to, down in ((1, over_x, 0, 6, over_y, 7), (4, over_y, 1, 3, over_x, 10),
                                             (2, over_x, 1, None, None, 8), (5, over_y, 0, None, None, 9),
                                             (3, across, 1, None, None, 12), (6, across, 0, None, None, 11)):
            for a in range(na):
                ref = arrived(a, k, chip, c, h)
                if onward is not None:
                    send(a, onward, ref, ref, (*to, c))
                send(a, down, ref, ref, sibling)
        for a in range(na):
            arrived(a, 0, here, 1 - c)
        for k, chip, h in ((7, over_x, 0), (10, over_y, 1), (8, over_x, 1), (9, over_y, 0), (12, across, 1), (11, across, 0)):
            for a in range(na):
                arrived(a, k, chip, 1 - c, h)
        for cp in sent:
            cp.wait_send()
        for cp in local:
            cp.wait()

    return pl.pallas_call(body, name=name, in_specs=[ANY] * na, out_specs=[ANY] * na, out_shape=out_shape,
                          scratch_shapes=_sems(na, 13))(*arrays)


def _pair_add(own, kind, got, table, tr, name):
    _, rows, cols = got.shape
    if kind == "scatter":
        own_spec = pl.BlockSpec((None, tr, cols), lambda rel, i, t: (t[rel], i, 0))
    else:
        own_spec = pl.BlockSpec((tr, cols), lambda rel, i, t: (i, t[rel]))
    blk = pl.BlockSpec((None, tr, cols), lambda rel, i, t: (rel, i, 0))

    def body(t_ref, a_ref, b_ref, o_ref):
        o_ref[...] = (a_ref[...].astype(F32) + b_ref[...].astype(F32)).astype(o_ref.dtype)

    return pl.pallas_call(
        body, name=name, out_shape=jax.ShapeDtypeStruct(got.shape, got.dtype),
        grid_spec=pltpu.PrefetchScalarGridSpec(num_scalar_prefetch=1, grid=(NCHIP, rows // tr), in_specs=[own_spec, blk],
                                               out_specs=blk),
        compiler_params=_params(32),
    )(table, own, got)


HBM = pl.BlockSpec(memory_space=pltpu.HBM)
SEM = pl.BlockSpec(memory_space=pltpu.SEMAPHORE)
DATAFLOW = pltpu.SideEffectType.DATAFLOW_SIDE_EFFECTING


def _pair_copies(kinds):
    def describe(srcs, lands, send_sems, recv_sems):
        x, y, c = _place()
        na = len(srcs)
        for rel in range(NCHIP):
            p = _dev_index(*_rel_chip(x, y, rel), 1 - c)
            for a in range(na):
                src = srcs[a].at[p] if kinds[a] == "scatter" else _cols(srcs[a], p, srcs[a].shape[1] // NDEV)
                cp = pltpu.make_async_remote_copy(src, lands[a].at[rel], send_sems.at[rel * na + a], recv_sems.at[rel * na + a],
                                                  device_id=(x, y, 1 - c), device_id_type=MESH)
                yield cp, cp
    return describe


def _chip_copies(srcs, lands, send_sems, recv_sems):
    x, y, c = _place()
    na = len(srcs)
    for rel in range(1, NCHIP):
        for a in range(na):
            i = (rel - 1) * na + a
            cp = pltpu.make_async_remote_copy(srcs[a].at[rel], lands[a].at[rel], send_sems.at[i], recv_sems.at[i],
                                              device_id=(*_rel_chip(x, y, rel), c), device_id_type=MESH)
            yield cp, cp


def _slab(ref, kind, s, width):
    return _cols(ref, s, width) if kind == "bcast_cols" else ref.at[s]


def _all_copies(srcs, lands, send_sems, recv_sems):
    x, y, c = _place()
    na = len(srcs)
    me = _dev_index(x, y, c)
    for a in range(na):
        yield pltpu.make_async_copy(srcs[a], lands[a].at[me], send_sems.at[a]), None
    for k in range(1, NDEV):
        to, frm = (me + k) % NDEV, (me + NDEV - k) % NDEV
        for a in range(na):
            i = k * na + a
            send = pltpu.make_async_remote_copy(srcs[a], lands[a].at[me], send_sems.at[i], recv_sems.at[i],
                                                device_id=_coords(to), device_id_type=MESH)
            arrival = pltpu.make_async_remote_copy(srcs[a], lands[a].at[frm], send_sems.at[i], recv_sems.at[i],
                                                   device_id=_coords(to), device_id_type=MESH)
            yield send, arrival


def _gather_copies(kinds):
    def describe(srcs, lands, send_sems, recv_sems):
        x, y, c = _place()
        na = len(srcs)
        me = _dev_index(x, y, c)
        for a in range(na):
            yield pltpu.make_async_copy(srcs[a], _slab(lands[a], kinds[a], me, srcs[a].shape[-1]),
                                        send_sems.at[NCHIP * na + a]), None
        for rel in range(NCHIP):
            to = (x, y, 1 - c) if rel == 0 else (*_rel_chip(x, y, rel), c)
            for a in range(na):
                width = srcs[a].shape[-1]
                i = rel * na + a
                send = pltpu.make_async_remote_copy(srcs[a], _slab(lands[a], kinds[a], me, width), send_sems.at[i], recv_sems.at[i],
                                                    device_id=to, device_id_type=MESH)
                arrival = pltpu.make_async_remote_copy(srcs[a], _slab(lands[a], kinds[a], _dev_index(*to), width), send_sems.at[i],
                                                       recv_sems.at[i], device_id=to, device_id_type=MESH)
                yield send, arrival
    return describe


def _relay_copies(kinds, widths, phase):
    def describe(srcs, lands, send_sems, recv_sems):
        x, y, c = _place()
        na = len(lands)
        here, over_x, over_y, across = (x, y), (1 - x, y), (x, 1 - y), (1 - x, 1 - y)
        sibling = (x, y, 1 - c)

        def half(ref, h):
            rows = ref.shape[0] // 2
            return ref if h is None else ref.at[pl.ds(h * rows, rows), :]

        def slab(a, chip, core, h=None):
            return half(_slab(lands[a], kinds[a], _dev_index(*chip, core), widths[a]), h)

        def pair(a, k, src, dst, to, arrival):
            i = k * na + a
            return (pltpu.make_async_remote_copy(src, dst, send_sems.at[i], recv_sems.at[i], device_id=to, device_id_type=MESH),
                    pltpu.make_async_remote_copy(arrival, arrival, send_sems.at[i], recv_sems.at[i], device_id=to,
                                                 device_id_type=MESH))

        if phase == 1:
            for a in range(na):
                yield pltpu.make_async_copy(srcs[a], slab(a, here, c), send_sems.at[a]), None
            for k, h, chip in ((1, 0, over_x), (2, 1, over_y), (3, 1, over_x), (4, 0, over_y)):
                for a in range(na):
                    yield pair(a, k, half(srcs[a], h), slab(a, here, c, h), (*chip, c), slab(a, chip, c, h))
            for a in range(na):
                yield pair(a, 5, srcs[a], slab(a, here, c), sibling, slab(a, here, 1 - c))
        elif phase == 2:
            for a in range(na):
                yield pair(a, 0, slab(a, over_x, c, 0), slab(a, over_x, c, 0), (*over_y, c), slab(a, across, c, 0))
                yield pair(a, 1, slab(a, over_y, c, 1), slab(a, over_y, c, 1), (*over_x, c), slab(a, across, c, 1))
            for k, (chip, h) in enumerate(((over_x, 0), (over_y, 1), (over_x, 1), (over_y, 0)), start=2):
                for a in range(na):
                    yield pair(a, k, slab(a, chip, c, h), slab(a, chip, c, h), sibling, slab(a, chip, 1 - c, h))
        else:
            for h in (0, 1):
                for a in range(na):
                    yield pair(a, h, slab(a, across, c, h), slab(a, across, c, h), sibling, slab(a, across, 1 - c, h))
    return describe


RELAY_COPIES = {1: 6, 2: 6, 3: 2}


def _relay_finish(arrays, kinds, widths, name):
    na = len(arrays)
    describe = _relay_copies(kinds, widths, 3)

    def body(*refs):
        copies = list(describe((), refs[na:2 * na], refs[2 * na], refs[2 * na + 1]))
        for send, _ in copies:
            send.start()
        for send, arrival in copies:
            send.wait_send()
            arrival.wait_recv()

    return pl.pallas_call(body, name=name, in_specs=[ANY] * na, out_specs=[ANY] * na,
                          out_shape=[jax.ShapeDtypeStruct(a.shape, a.dtype) for a in arrays],
                          input_output_aliases={i: i for i in range(na)},
                          scratch_shapes=[pltpu.SemaphoreType.DMA((RELAY_COPIES[3] * na,))] * 2)(*arrays)


def _pass_to_sibling(arrays, kinds, widths, name):
    na = len(arrays)

    def body(*refs):
        bufs = refs[na:2 * na]
        send_sems, recv_sems = refs[2 * na:]
        x, y, c = _place()
        copies = []
        for rel in range(1, NCHIP):
            for a in range(na):
                mine = _slab(bufs[a], kinds[a], _dev_index(*_rel_chip(x, y, rel), c), widths[a])
                theirs = _slab(bufs[a], kinds[a], _dev_index(*_rel_chip(x, y, rel), 1 - c), widths[a])
                send = pltpu.make_async_remote_copy(mine, mine, send_sems.at[a, rel], recv_sems.at[a, rel],
                                                    device_id=(x, y, 1 - c), device_id_type=MESH)
                send.start()
                copies.append((send, pltpu.make_async_remote_copy(theirs, theirs, send_sems.at[a, rel], recv_sems.at[a, rel],
                                                                  device_id=(x, y, 1 - c), device_id_type=MESH)))
        for send, arrival in copies:
            send.wait_send()
            arrival.wait_recv()

    return pl.pallas_call(body, name=name, in_specs=[ANY] * na, out_specs=[ANY] * na,
                          out_shape=[jax.ShapeDtypeStruct(a.shape, a.dtype) for a in arrays],
                          input_output_aliases={i: i for i in range(na)}, scratch_shapes=_sems(na, NCHIP)[:2])(*arrays)


def _start_copies(name, srcs, lands, describe, ncopies, after=()):
    arrays = list(srcs) + list(lands)
    ns, n, nin = len(srcs), len(arrays), len(arrays) + len(after)

    def body(*refs):
        for send, _ in describe(refs[:ns], refs[ns:n], refs[nin], refs[nin + 1]):
            send.start()
        refs[-1][...] = jnp.zeros_like(refs[-1])

    out = pl.pallas_call(
        body, name=name,
        out_shape=(pltpu.SemaphoreType.DMA((ncopies,)), pltpu.SemaphoreType.DMA((ncopies,)),
                   *[pltpu.HBM(a.shape, a.dtype) for a in arrays], jax.ShapeDtypeStruct((8, 128), F32)),
        in_specs=[HBM] * n + [ANY] * len(after), out_specs=(SEM, SEM, *[HBM] * n, pl.BlockSpec(memory_space=pltpu.VMEM)),
        input_output_aliases={i: 2 + i for i in range(n)},
        compiler_params=pltpu.CompilerParams(has_side_effects=DATAFLOW),
    )(*[pltpu.with_memory_space_constraint(a, pltpu.HBM) for a in arrays], *after)
    return out[0], out[1], list(out[2:2 + n]), out[-1]


def _wait_copies(name, send_sems, recv_sems, thru, ns, describe, after):
    n = len(thru)

    def body(*refs):
        for send, arrival in describe(refs[:ns], refs[ns:n], refs[n], refs[n + 1]):
            if arrival is None:
                send.wait()
            else:
                send.wait_send()
                arrival.wait_recv()

    out = pl.pallas_call(
        body, name=name, out_shape=tuple(pltpu.HBM(a.shape, a.dtype) for a in thru),
        in_specs=[HBM] * n + [SEM, SEM] + [ANY] * len(after), out_specs=tuple([HBM] * n),
        input_output_aliases={i: i for i in range(n)},
        compiler_params=pltpu.CompilerParams(has_side_effects=DATAFLOW),
    )(*thru, send_sems, recv_sems, *after)
    return list(out[:ns]), list(out[ns:])


def _adamw(parts, w, m, v, tr, name, tc=None):
    rows, cols = w.shape
    tc = cols if tc is None else tc
    nparts = len(parts)
    blk = pl.BlockSpec((tr, tc), lambda i, j: (i, j))

    def slab_spec(s):
        return pl.BlockSpec((None, tr, tc), lambda i, j: (s, i, j))

    def body(*refs):
        p_refs = refs[:nparts]
        w_ref, m_ref, v_ref, g_ref, d_ref, nm_ref, nv_ref = refs[nparts:]
        g = p_refs[0][...].astype(F32)
        for p_ref in p_refs[1:]:
            g = g + p_ref[...].astype(F32)
        nm = ADAM_B1 * m_ref[...] + (1.0 - ADAM_B1) * g
        nv = ADAM_B2 * v_ref[...] + (1.0 - ADAM_B2) * (g * g)
        m_hat = nm / (1.0 - ADAM_B1 ** ADAM_STEP)
        v_hat = nv / (1.0 - ADAM_B2 ** ADAM_STEP)
        g_ref[...] = g
        d_ref[...] = -ADAM_LR * (m_hat / (jnp.sqrt(v_hat) + ADAM_EPS) + ADAM_WD * w_ref[...])
        nm_ref[...] = nm
        nv_ref[...] = nv

    return pl.pallas_call(
        body, grid=(rows // tr, cols // tc), name=name,
        in_specs=[slab_spec(s) for _, s in parts] + [blk, blk, blk], out_specs=[blk] * 4,
        out_shape=[jax.ShapeDtypeStruct((rows, cols), F32)] * 4,
        compiler_params=_params(40),
    )(*[a for a, _ in parts], w, m, v)


def _pack_small(vals, extra=None):
    flat = [vals[n].reshape(-1).astype(F32) for n, _ in SMALL]
    tail = jnp.zeros((SMALL_ROWS * 128 - LOSS_AT,), F32)
    if extra is not None:
        tail = tail.at[0].set(extra)
    return jnp.concatenate(flat + [tail]).reshape(SMALL_ROWS, 128)


def _unpack_small(packed, like):
    flat, out, at = packed.reshape(-1), {}, 0
    for n, size in SMALL:
        out[n] = flat[at:at + size].reshape(like[n].shape)
        at += size
    return out, flat[LOSS_AT]


def kernel(x, ffn1_norm, ffn1_w_in, ffn1_w_out, mix_norm, w_in_mix, w_pool, pool_scale, w_alpha, b_alpha, gla_norm, w_out_mix, ffn2_norm, ffn2_w_in, ffn2_w_out, final_norm, loss_target, m_ffn1_norm, m_ffn1_w_in, m_ffn1_w_out, m_mix_norm, m_w_in_mix, m_w_pool, m_pool_scale, m_w_alpha, m_b_alpha, m_gla_norm, m_w_out_mix, m_ffn2_norm, m_ffn2_w_in, m_ffn2_w_out, m_final_norm, v_ffn1_norm, v_ffn1_w_in, v_ffn1_w_out, v_mix_norm, v_w_in_mix, v_w_pool, v_pool_scale, v_w_alpha, v_b_alpha, v_gla_norm, v_w_out_mix, v_ffn2_norm, v_ffn2_w_in, v_ffn2_w_out, v_final_norm):
    names = ["ffn1_norm", "ffn1_w_in", "ffn1_w_out", "mix_norm", "w_in_mix", "w_pool", "pool_scale", "w_alpha", "b_alpha",
             "gla_norm", "w_out_mix", "ffn2_norm", "ffn2_w_in", "ffn2_w_out", "final_norm"]
    p = dict(zip(names, [ffn1_norm, ffn1_w_in, ffn1_w_out, mix_norm, w_in_mix, w_pool, pool_scale, w_alpha, b_alpha,
                         gla_norm, w_out_mix, ffn2_norm, ffn2_w_in, ffn2_w_out, final_norm]))
    m = dict(zip(names, [m_ffn1_norm, m_ffn1_w_in, m_ffn1_w_out, m_mix_norm, m_w_in_mix, m_w_pool, m_pool_scale, m_w_alpha,
                         m_b_alpha, m_gla_norm, m_w_out_mix, m_ffn2_norm, m_ffn2_w_in, m_ffn2_w_out, m_final_norm]))
    v = dict(zip(names, [v_ffn1_norm, v_ffn1_w_in, v_ffn1_w_out, v_mix_norm, v_w_in_mix, v_w_pool, v_pool_scale, v_w_alpha,
                         v_b_alpha, v_gla_norm, v_w_out_mix, v_ffn2_norm, v_ffn2_w_in, v_ffn2_w_out, v_final_norm]))

    mx, my, mc = _place()
    table = jnp.stack([_dev_index(*_rel_chip(mx, my, rel), mc) for rel in range(NCHIP)]).astype(jnp.int32)

    def landing(shard, kind):
        shape = (shard.shape[0], NDEV * shard.shape[1]) if kind == "bcast_cols" else (NDEV,) + shard.shape
        return lax.empty(shape, shard.dtype)

    def gather_begin(items, tag, after):
        kinds = [kind for _, kind in items]
        copies = _gather_copies(kinds)
        s, r, thru, tok = _start_copies(tag + "_start", [a for a, _ in items], [landing(a, kind) for a, kind in items], copies,
                                        (NCHIP + 1) * len(items), after)
        return (s, r, thru, copies, kinds, [a.shape[-1] for a, _ in items], tag), tok

    def gather_end(state, after):
        s, r, thru, copies, kinds, widths, tag = state
        _, lands = _wait_copies(tag + "_wait", s, r, thru, len(kinds), copies, after)
        return _pass_to_sibling(lands, kinds, widths, tag + "_pass")

    def shard16(n):
        return p[n][0].astype(BF16)

    g_w1in, g_w1out = _gather([(shard16("ffn1_w_in"), "bcast_cols"), (shard16("ffn1_w_out"), "bcast")], "gather_ffn1")
    mix_state, tok_m = gather_begin([(jnp.transpose(p["w_in_mix"][0]).astype(BF16), "bcast"), (shard16("w_out_mix"), "bcast"),
                                     (p["w_pool"][0].reshape(H * 32, PG), "bcast"), (p["w_alpha"][0], "bcast")], "gather_mix",
                                    (g_w1out,))
    ffn2_items = [(shard16("ffn2_w_in"), "bcast_cols"), (shard16("ffn2_w_out"), "bcast")]
    ffn2_kinds = [kind for _, kind in ffn2_items]
    ffn2_widths = [a.shape[-1] for a, _ in ffn2_items]
    relay = {ph: _relay_copies(ffn2_kinds, ffn2_widths, ph) for ph in (1, 2)}
    f_s, f_r, f_thru, tok_f = _start_copies("gather_ffn2_start", [a for a, _ in ffn2_items],
                                            [landing(a, kind) for a, kind in ffn2_items], relay[1],
                                            RELAY_COPIES[1] * len(ffn2_items), (tok_m,))
    ffn2_lands = {}

    def relay_ffn2(point, value):
        _, lands = _wait_copies("gather_ffn2_wait", f_s, f_r, f_thru, len(ffn2_items), relay[1], (value,))
        ffn2_lands["s"], ffn2_lands["r"], ffn2_lands["thru"], tok = _start_copies(
            "gather_ffn2_relay_start", [], lands, relay[2], RELAY_COPIES[2] * len(ffn2_items))
        return (tok,)

    full ={"ffn1_w_in": g_w1in, "ffn1_w_out": g_w1out.reshape(FF, D), "final_norm": final_norm.reshape(1, D)}
    for n in ("ffn1_norm", "mix_norm", "ffn2_norm", "pool_scale", "b_alpha", "gla_norm"):
        full[n] = p[n]

    xs, tgt = x[0], loss_target[0]
    h1, n1, gu1 = _fwd_ffn1(xs, full, after=(tok_m, tok_f))
    g_wmix, g_wo, g_wpool, g_walpha = gather_end(mix_state, (h1,))
    walpha = jnp.transpose(g_walpha, (1, 0, 2)).reshape(RANK, H * DK)
    full.update({
        "w_in_mix": jnp.pad(g_wmix.reshape(D_IN, D), ((0, D_IN_PAD - D_IN), (0, 0))),
        "w_out_mix": g_wo.reshape(D, D),
        "w_pool": jnp.transpose(g_wpool.reshape(NDEV, H, 32, PG), (1, 0, 2, 3)).reshape(H, PG, PG).astype(BF16),
        "w_alpha": jnp.pad(walpha, ((0, RANK_PAD - RANK), (0, 0))).astype(BF16),
    })
    h2, sv = _fwd_mixer(h1, full, at=relay_ffn2)
    _, lands = _wait_copies("gather_ffn2_relay_wait", ffn2_lands["s"], ffn2_lands["r"], ffn2_lands["thru"], 0, relay[2], (h2,))
    g_w2in, g_w2out = _relay_finish(lands, ffn2_kinds, ffn2_widths, "gather_ffn2_finish")
    full.update({"ffn2_w_in": g_w2in, "ffn2_w_out": g_w2out.reshape(FF, D)})
    dh3, d_final, loss_part, n3, gu3 = _fwd_ffn2_loss(h2, tgt, full)


    def slab_shape(a, kind):
        return (NCHIP,) + (a.shape[1:] if kind == "scatter" else (a.shape[0], a.shape[1] // NDEV))

    def pair_add_all(own, got, tag):
        return [_pair_add(a, kind, got_a, table, tr, "%s_pair_add_%d" % (tag, i))
                for i, ((a, kind, tr), got_a) in enumerate(zip(own, got))]

    def reduce_begin(own, tag, after=()):
        kinds = [kind for _, kind, _ in own]
        copies = _pair_copies(kinds)
        s, r, thru, tok = _start_copies(tag + "_pair_start", [a for a, _, _ in own],
                                        [lax.empty(slab_shape(a, kind), a.dtype) for a, kind, _ in own], copies,
                                        NCHIP * len(own), after)
        return dict(own=own, copies=copies, s=s, r=r, thru=thru, tag=tag), tok

    def reduce_middle(st, after):
        own, tag = st["own"], st["tag"]
        sent, got = _wait_copies(tag + "_pair_wait", st["s"], st["r"], st["thru"], len(own), st["copies"], after)
        pre = pair_add_all([(a, kind, tr) for a, (_, kind, tr) in zip(sent, own)], got, tag)
        st["s"], st["r"], st["thru"], tok = _start_copies(tag + "_chip_start", pre, [lax.empty(a.shape, a.dtype) for a in pre],
                                                          _chip_copies, (NCHIP - 1) * len(pre))
        return tok

    def reduce_end(st, after):
        n = len(st["own"])
        pre, land = _wait_copies(st["tag"] + "_chip_wait", st["s"], st["r"], st["thru"], n, _chip_copies, after)
        return [[(a, 0)] + [(b, rel) for rel in range(1, NCHIP)] for a, b in zip(pre, land)]

    def w_in_item(a):
        return (a, "scatter_cols", 512)

    def w_out_item(a):
        return (a.reshape(NDEV, WOUT_SHARD, D), "scatter", WOUT_SHARD)

    red, small = {}, {}

    def at(point, value):
        if point == "ffn2_w_out":
            red["w2out"], tok = reduce_begin([w_out_item(value)], "ffn2_w_out")
        elif point == "ffn2_w_in":
            tok_a = reduce_middle(red["w2out"], (value,))
            red["w2in"], tok = reduce_begin([w_in_item(value)], "ffn2_w_in", (tok_a,))
        elif point == "pool_bwd":
            tok = reduce_middle(red["w2in"], (value,))
        elif point == "mixer_weights":
            d_wmix8 = value["w_in_mix"][:D_IN].reshape(NDEV, MIX_SHARD, D)
            d_wpool8 = jnp.transpose(value["w_pool"].reshape(H, NDEV, 32, PG), (1, 0, 2, 3)).reshape(NDEV, H * 32, PG)
            d_walpha8 = jnp.transpose(value["w_alpha"][:RANK].reshape(RANK, NDEV, H * DK // NDEV), (1, 0, 2))
            red["mix"], tok = reduce_begin([(d_wmix8, "scatter", MIX_SHARD),
                                            (value["w_out_mix"].reshape(NDEV, D // NDEV, D), "scatter", D // NDEV),
                                            (d_wpool8, "scatter", H * 32), (d_walpha8, "scatter", RANK)], "mix")
        elif point == "dh1":
            tok = reduce_middle(red["mix"], (value,))
        elif point == "ffn1_w_out":
            red["w1out"], tok = reduce_begin([w_out_item(value)], "ffn1_w_out")
        elif point == "ffn1_dx_a":
            tok = reduce_middle(red["w1out"], (value,))
        elif point == "ffn1_w_in":
            red["w1in"], tok = reduce_begin([w_in_item(value)], "ffn1_w_in")
        return (tok,)

    dx, g = _backward(xs, h1, h2, n1, gu1, sv, n3, gu3, dh3, full, at)
    packed = _pack_small(dict(final_norm=d_final, **g), loss_part[0, 0])
    small_s, small_r, small_thru, tok_s = _start_copies("gather_small_start", [packed], [lax.empty((NDEV,) + packed.shape, F32)],
                                                        _all_copies, NDEV, (dx,))
    tok_c = reduce_middle(red["w1in"], (tok_s,))

    def upd(parts, n, shape2d, tr):
        res = _adamw(parts, p[n].reshape(shape2d), m[n].reshape(shape2d), v[n].reshape(shape2d), tr, "adamw_" + n)
        return [r.reshape(p[n].shape) for r in res]

    def transposed(a):
        return jnp.transpose(a[0])

    (p_w2out,) = reduce_end(red["w2out"], (tok_c,))
    (p_w2in,) = reduce_end(red["w2in"], (tok_c,))
    p_wmix, p_wo, p_wpool, p_walpha = reduce_end(red["mix"], (tok_c,))
    out = {
        "ffn2_w_in": upd(p_w2in, "ffn2_w_in", (D, WIN_SHARD), 128),
        "ffn2_w_out": upd(p_w2out, "ffn2_w_out", (WOUT_SHARD, D), 64),
        "w_out_mix": upd(p_wo, "w_out_mix", (D // NDEV, D), 64),
        "w_pool": upd(p_wpool, "w_pool", (H * 32, PG), H * 32),
        "w_alpha": upd(p_walpha, "w_alpha", (RANK, H * DK // NDEV), RANK),
    }
    out["w_in_mix"] = [jnp.transpose(r)[None] for r in
                       _adamw(p_wmix, transposed(p["w_in_mix"]), transposed(m["w_in_mix"]), transposed(v["w_in_mix"]),
                              MIX_SHARD, "adamw_w_in_mix", tc=512)]
    _, (r_small,) = _wait_copies("gather_small_wait", small_s, small_r, small_thru, 1, _all_copies,
                                 (out["w_in_mix"][3], out["ffn2_w_in"][3], out["ffn2_w_out"][3], out["w_out_mix"][3]))
    small_res = _adamw([(r_small, s) for s in range(NDEV)], _pack_small(p), _pack_small(m), _pack_small(v), SMALL_ROWS,
                       "adamw_small")
    (p_w1out,) = reduce_end(red["w1out"], (small_res[0],))
    out["ffn1_w_out"] = upd(p_w1out, "ffn1_w_out", (WOUT_SHARD, D), 64)
    (p_w1in,) = reduce_end(red["w1in"], (out["ffn1_w_out"][3],))
    out["ffn1_w_in"] = upd(p_w1in, "ffn1_w_in", (D, WIN_SHARD), 128)
    unpacked = [_unpack_small(r, p) for r in small_res]
    loss = unpacked[0][1]
    for n, _ in SMALL:
        out[n] = [u[0][n] for u in unpacked]

    return (loss, dx.reshape(1, S, D), *[out[n][0] for n in names], *[out[n][1] for n in names],
            *[out[n][2] for n in names], *[out[n][3] for n in names])
```

```python
import jax
import jax.numpy as jnp
from jax import lax
from jax.experimental import pallas as pl
from jax.experimental.pallas import tpu as pltpu

F32, BF16 = jnp.float32, jnp.bfloat16
MESH = pl.DeviceIdType.MESH
ANY = pl.BlockSpec(memory_space=pl.ANY)

NDEV = 8
S = 2048
D = 2048
FF = 5632
WIN_SHARD = 2 * FF // NDEV
WOUT_SHARD = FF // NDEV
D_POOL = 1024
PG = 256
POOL_WINDOWS = (2, 4, 8, 16)
H = 4
DK = 128
DV = 256
CH = 64
NCH = S // CH
RANK = 16
RANK_PAD = 128
D_IN = 4112
D_IN_PAD = 4224
MIX_SHARD = D_IN // NDEV
O_Q, O_K, O_V, O_G, O_R = 1024, 1536, 2048, 3072, 4096
GATE_NORM = 16.0
QK_SCALE = DK ** -0.5
EPS = 1e-6
ADAM_LR, ADAM_B1, ADAM_B2, ADAM_EPS, ADAM_WD, ADAM_STEP = 0.001, 0.9, 0.999, 1e-08, 0.01, 10
V7X_VMEM_BYTES = 64 << 20

SMALL = (("ffn1_norm", 2048), ("mix_norm", 2048), ("ffn2_norm", 2048), ("final_norm", 2048),
         ("pool_scale", 1024), ("b_alpha", 512), ("gla_norm", 256))
SMALL_ROWS = 80
LOSS_AT = sum(n for _, n in SMALL)


def _params(vmem_mb, sem=None):
    return pltpu.CompilerParams(dimension_semantics=sem, vmem_limit_bytes=min(vmem_mb << 20, V7X_VMEM_BYTES - (4 << 20)))


def _dot(a, b):
    return jnp.dot(a, b, preferred_element_type=F32)


def _dot_nt(a, b):
    return lax.dot_general(a, b, (((1,), (1,)), ((), ())), preferred_element_type=F32)


def _dot_tn(a, b):
    return lax.dot_general(a, b, (((0,), (0,)), ((), ())), preferred_element_type=F32)


def _sigmoid(x):
    return 0.5 * jnp.tanh(0.5 * x) + 0.5


def _log_sigmoid(x):
    return jnp.minimum(x, 0.0) - jnp.log(1.0 + jnp.exp(-jnp.abs(x)))


def _rms(x, g):
    r = lax.rsqrt(jnp.mean(x * x, axis=-1, keepdims=True) + EPS)
    return x * r * g


def _rms_bwd(dn, x, g):
    r = lax.rsqrt(jnp.mean(x * x, axis=-1, keepdims=True) + EPS)
    xh = x * r
    dxh = dn * g
    dx = r * (dxh - xh * jnp.mean(dxh * xh, axis=-1, keepdims=True))
    return dx, jnp.sum(dn * xh, axis=0, keepdims=True)


ROWS = 64


def _row_loop(total, fn, init=0):
    def step(t, carry):
        return fn(pl.ds(pl.multiple_of(t * ROWS, ROWS), ROWS), carry)
    return lax.fori_loop(0, total // ROWS, step, init)


def _split3(x):
    hi = x.astype(BF16)
    r1 = x - hi.astype(F32)
    mid = r1.astype(BF16)
    lo = (r1 - mid.astype(F32)).astype(BF16)
    return hi, mid, lo


def _tri_dot(tri_b, x):
    hi, mid, lo = _split3(x)
    return (_dot(tri_b, lo) + _dot(tri_b, mid)) + _dot(tri_b, hi)


FFN_TS, FFN_TF = 512, 512


def _ffn_specs():
    wg = pl.BlockSpec((D, FFN_TF), lambda i, j: (0, j))
    wu = pl.BlockSpec((D, FFN_TF), lambda i, j: (0, FF // FFN_TF + j))
    wo = pl.BlockSpec((FFN_TF, D), lambda i, j: (j, 0))
    row = pl.BlockSpec((FFN_TS, D), lambda i, j: (i, 0))
    vec = pl.BlockSpec((1, D), lambda i, j: (0, 0))
    gu = pl.BlockSpec((2, FFN_TS, FFN_TF), lambda i, j: (0, i, j))
    return wg, wu, wo, row, vec, gu


def _ordered_after(body, n_in, after):
    def wrapped(*refs):
        return body(*refs[:n_in], *refs[n_in + len(after):])
    return wrapped, [ANY] * len(after)


def _ffn_fwd(h, g, w_in8, w_out, name, after=()):
    nj = FF // FFN_TF
    wg, wu, wo, row, vec, gu = _ffn_specs()

    def body(h_ref, g_ref, wg_ref, wu_ref, wo_ref, ho_ref, n_ref, gu_ref, acc_ref):
        j = pl.program_id(1)

        @pl.when(j == 0)
        def _():
            def norm(rows, c):
                n_ref[rows, :] = _rms(h_ref[rows, :], g_ref[...]).astype(BF16)
                return c
            _row_loop(FFN_TS, norm)
            acc_ref[...] = jnp.zeros_like(acc_ref)

        n = n_ref[...]
        gate = _dot(n, wg_ref[...])
        up = _dot(n, wu_ref[...])
        gu_ref[0] = gate.astype(BF16)
        gu_ref[1] = up.astype(BF16)
        a = (gate * _sigmoid(gate)) * up
        acc_ref[...] += _dot(a.astype(BF16), wo_ref[...])

        @pl.when(j == nj - 1)
        def _():
            def residual(rows, c):
                ho_ref[rows, :] = h_ref[rows, :] + 0.5 * acc_ref[rows, :]
                return c
            _row_loop(FFN_TS, residual)

    body, extra = _ordered_after(body, 5, after)
    return pl.pallas_call(
        body, grid=(S // FFN_TS, nj), name=name,
        in_specs=[row, vec, wg, wu, wo] + extra, out_specs=[row, row, gu],
        out_shape=[jax.ShapeDtypeStruct((S, D), F32), jax.ShapeDtypeStruct((S, D), BF16),
                   jax.ShapeDtypeStruct((2, S, FF), BF16)],
        scratch_shapes=[pltpu.VMEM((FFN_TS, D), F32)],
        compiler_params=_params(56, ("arbitrary", "arbitrary")),
    )(h, g, w_in8, w_in8, w_out, *after)


def _ffn_bwd_x(dhp, h, g, gu_arr, w_in8, w_out, name, after=()):
    ni, nj = S // FFN_TS, FF // FFN_TF
    wg, wu, wo, row, vec, gu = _ffn_specs()
    act = pl.BlockSpec((FFN_TS, FFN_TF), lambda i, j: (i, j))

    def body(dhp_ref, h_ref, g_ref, gu_ref, wg_ref, wu_ref, wo_ref,
             dgu_ref, a_ref, df_ref, dh_ref, dhb_ref, dg_ref, acc_ref):
        i, j = pl.program_id(0), pl.program_id(1)

        @pl.when(j == 0)
        def _():
            def half(rows, c):
                df_ref[rows, :] = (0.5 * dhp_ref[rows, :]).astype(BF16)
                return c
            _row_loop(FFN_TS, half)
            acc_ref[...] = jnp.zeros_like(acc_ref)

        gate = gu_ref[0].astype(F32)
        up = gu_ref[1].astype(F32)
        da = _dot_nt(df_ref[...], wo_ref[...])
        sg = _sigmoid(gate)
        silu = gate * sg
        dgate = (da * up * (sg * (1.0 + gate * (1.0 - sg)))).astype(BF16)
        dup = (da * silu).astype(BF16)
        a_ref[...] = (silu * up).astype(BF16)
        dgu_ref[0] = dgate
        dgu_ref[1] = dup
        acc_ref[...] += _dot_nt(dgate, wg_ref[...]) + _dot_nt(dup, wu_ref[...])

        @pl.when(j == nj - 1)
        def _():
            def norm_bwd(rows, dg):
                dx, dg_rows = _rms_bwd(acc_ref[rows, :], h_ref[rows, :], g_ref[...])
                dh = dhp_ref[rows, :] + dx
                dh_ref[rows, :] = dh
                dhb_ref[rows, :] = dh.astype(BF16)
                return dg + dg_rows
            dg = _row_loop(FFN_TS, norm_bwd, jnp.zeros((1, D), F32))

            @pl.when(i == 0)
            def _():
                dg_ref[...] = dg

            @pl.when(i > 0)
            def _():
                dg_ref[...] += dg

    body, extra = _ordered_after(body, 7, after)
    return pl.pallas_call(
        body, grid=(ni, nj), name=name,
        in_specs=[row, row, vec, gu, wg, wu, wo] + extra,
        out_specs=[gu, act, row, row, row, vec],
        out_shape=[jax.ShapeDtypeStruct((2, S, FF), BF16), jax.ShapeDtypeStruct((S, FF), BF16),
                   jax.ShapeDtypeStruct((S, D), BF16), jax.ShapeDtypeStruct((S, D), F32),
                   jax.ShapeDtypeStruct((S, D), BF16), jax.ShapeDtypeStruct((1, D), F32)],
        scratch_shapes=[pltpu.VMEM((FFN_TS, D), F32)],
        compiler_params=_params(58, ("arbitrary", "arbitrary")),
    )(dhp, h, g, gu_arr, w_in8, w_in8, w_out, *after)


def _ffn_bwd_gu(dhp, gu_arr, w_out, name, after=()):
    _, _, wo, row, _, gu = _ffn_specs()
    act = pl.BlockSpec((FFN_TS, FFN_TF), lambda i, j: (i, j))

    def body(dhp_ref, gu_ref, wo_ref, dgu_ref, a_ref, df_ref):
        @pl.when(pl.program_id(1) == 0)
        def _():
            def half(rows, c):
                df_ref[rows, :] = (0.5 * dhp_ref[rows, :]).astype(BF16)
                return c
            _row_loop(FFN_TS, half)

        gate = gu_ref[0].astype(F32)
        up = gu_ref[1].astype(F32)
        da = _dot_nt(df_ref[...], wo_ref[...])
        sg = _sigmoid(gate)
        silu = gate * sg
        dgu_ref[0] = (da * up * (sg * (1.0 + gate * (1.0 - sg)))).astype(BF16)
        dgu_ref[1] = (da * silu).astype(BF16)
        a_ref[...] = (silu * up).astype(BF16)

    body, extra = _ordered_after(body, 3, after)
    return pl.pallas_call(
        body, grid=(S // FFN_TS, FF // FFN_TF), name=name,
        in_specs=[row, gu, wo] + extra, out_specs=[gu, act, row],
        out_shape=[jax.ShapeDtypeStruct((2, S, FF), BF16), jax.ShapeDtypeStruct((S, FF), BF16),
                   jax.ShapeDtypeStruct((S, D), BF16)],
        compiler_params=_params(40, ("arbitrary", "arbitrary")),
    )(dhp, gu_arr, w_out, *after)


def _ffn_bwd_dx(dgu, dhp, h, g, w_in, first_tile, ntiles, name, after=(), into=None):
    nj = FF // FFN_TF
    wg, wu, _, _, vec, _ = _ffn_specs()
    row_in = pl.BlockSpec((FFN_TS, D), lambda i, j: (first_tile + i, 0))
    dgu_spec = pl.BlockSpec((2, FFN_TS, FFN_TF), lambda i, j: (0, first_tile + i, j))
    after = tuple(after) + (() if into is None else (into,))

    def body(dgu_ref, dhp_ref, h_ref, g_ref, wg_ref, wu_ref, dh_ref, dg_ref, acc_ref):
        i, j = pl.program_id(0), pl.program_id(1)

        @pl.when(j == 0)
        def _():
            acc_ref[...] = jnp.zeros_like(acc_ref)

        acc_ref[...] += _dot_nt(dgu_ref[0], wg_ref[...]) + _dot_nt(dgu_ref[1], wu_ref[...])

        @pl.when(j == nj - 1)
        def _():
            def norm_bwd(rows, dg):
                dx, dg_rows = _rms_bwd(acc_ref[rows, :], h_ref[rows, :], g_ref[...])
                dh_ref[rows, :] = dhp_ref[rows, :] + dx
                return dg + dg_rows
            dg = _row_loop(FFN_TS, norm_bwd, jnp.zeros((1, D), F32))

            @pl.when(i == 0)
            def _():
                dg_ref[...] = dg

            @pl.when(i > 0)
            def _():
                dg_ref[...] += dg

    body, extra = _ordered_after(body, 6, after)
    return pl.pallas_call(
        body, grid=(ntiles, nj), name=name,
        in_specs=[dgu_spec, row_in, row_in, vec, wg, wu] + extra, out_specs=[row_in, vec],
        out_shape=[jax.ShapeDtypeStruct((S, D), F32), jax.ShapeDtypeStruct((1, D), F32)],
        input_output_aliases={} if into is None else {6 + len(after) - 1: 0},
        scratch_shapes=[pltpu.VMEM((FFN_TS, D), F32)],
        compiler_params=_params(48, ("arbitrary", "arbitrary")),
    )(dgu, dhp, h, g, w_in, w_in, *after)


def _tn_matmul(a, b, a_spec, b_spec, out_shape, out_spec, grid, name, vmem_mb, after=()):
    def body(a_ref, b_ref, o_ref):
        o_ref[...] = _dot_tn(a_ref[...], b_ref[...]).astype(o_ref.dtype)

    body, extra = _ordered_after(body, 2, after)
    return pl.pallas_call(body, grid=grid, name=name, in_specs=[a_spec, b_spec] + extra, out_specs=out_spec,
                          out_shape=out_shape, compiler_params=_params(vmem_mb))(a, b, *after)


def _resident(shape):
    return pl.BlockSpec(shape, lambda i: (0,) * len(shape), pipeline_mode=pl.Buffered(1))


def _norm_matmul(h, g, w, name, after=(), ts=256):
    n_out = w.shape[0]

    def body(h_ref, g_ref, w_ref, u_ref, n_ref):
        def norm(rows, c):
            n_ref[rows, :] = _rms(h_ref[rows, :], g_ref[...]).astype(BF16)
            return c
        _row_loop(ts, norm)
        u_ref[...] = _dot_nt(n_ref[...], w_ref[...])

    body, extra = _ordered_after(body, 3, after)
    return pl.pallas_call(
        body, grid=(S // ts,), name=name,
        in_specs=[pl.BlockSpec((ts, D), lambda i: (i, 0)), pl.BlockSpec((1, D), lambda i: (0, 0)), _resident(w.shape)] + extra,
        out_specs=[pl.BlockSpec((ts, n_out), lambda i: (i, 0)), pl.BlockSpec((ts, D), lambda i: (i, 0))],
        out_shape=[jax.ShapeDtypeStruct((S, n_out), F32), jax.ShapeDtypeStruct((S, D), BF16)],
        compiler_params=_params(48, ("arbitrary",)),
    )(h, g, w, *after)


def _matmul_residual(a, w, res, name, after=(), ts=512):
    k, n_out = w.shape

    def body(a_ref, w_ref, r_ref, o_ref):
        o_ref[...] = r_ref[...] + _dot(a_ref[...], w_ref[...])

    body, extra = _ordered_after(body, 3, after)
    return pl.pallas_call(
        body, grid=(S // ts,), name=name,
        in_specs=[pl.BlockSpec((ts, k), lambda i: (i, 0)), _resident(w.shape), pl.BlockSpec((ts, n_out), lambda i: (i, 0))] + extra,
        out_specs=pl.BlockSpec((ts, n_out), lambda i: (i, 0)),
        out_shape=jax.ShapeDtypeStruct((S, n_out), F32),
        compiler_params=_params(40),
    )(a, w, res, *after)


def _nt_matmul(a, w, name, after=(), ts=512):
    n_out, k = w.shape

    def body(a_ref, w_ref, o_ref):
        o_ref[...] = _dot_nt(a_ref[...], w_ref[...])

    body, extra = _ordered_after(body, 2, after)
    return pl.pallas_call(
        body, grid=(S // ts,), name=name,
        in_specs=[pl.BlockSpec((ts, k), lambda i: (i, 0)), _resident(w.shape)] + extra,
        out_specs=pl.BlockSpec((ts, n_out), lambda i: (i, 0)),
        out_shape=jax.ShapeDtypeStruct((S, n_out), F32),
        compiler_params=_params(40),
    )(a, w, *after)


def _matmul_normbwd(du, w, h, g, dres, name, after=(), ts=256):
    n_in = w.shape[0]
    row = pl.BlockSpec((ts, D), lambda i: (i, 0))
    vec = pl.BlockSpec((1, D), lambda i: (0, 0))

    def body(du_ref, w_ref, h_ref, g_ref, dres_ref, dh_ref, dg_ref, acc_ref):
        i = pl.program_id(0)
        acc_ref[...] = _dot(du_ref[...], w_ref[...])

        def norm_bwd(rows, dg):
            dx, dg_rows = _rms_bwd(acc_ref[rows, :], h_ref[rows, :], g_ref[...])
            dh_ref[rows, :] = dres_ref[rows, :] + dx
            return dg + dg_rows
        dg = _row_loop(ts, norm_bwd, jnp.zeros((1, D), F32))

        @pl.when(i == 0)
        def _():
            dg_ref[...] = dg

        @pl.when(i > 0)
        def _():
            dg_ref[...] += dg

    body, extra = _ordered_after(body, 5, after)
    return pl.pallas_call(
        body, grid=(S // ts,), name=name,
        in_specs=[pl.BlockSpec((ts, n_in), lambda i: (i, 0)), _resident(w.shape), row, vec, row] + extra,
        out_specs=[row, vec],
        out_shape=[jax.ShapeDtypeStruct((S, D), F32), jax.ShapeDtypeStruct((1, D), F32)],
        scratch_shapes=[pltpu.VMEM((ts, D), F32)],
        compiler_params=_params(52, ("arbitrary",)),
    )(du, w, h, g, dres, *after)


def _loss_head(h, g, tgt, ts=256):
    row = pl.BlockSpec((ts, D), lambda i: (i, 0))
    vec = pl.BlockSpec((1, D), lambda i: (0, 0))

    def body(h_ref, g_ref, t_ref, dh_ref, dg_ref, loss_ref):
        i = pl.program_id(0)

        def rows_fn(rows, carry):
            dg, part = carry
            x = h_ref[rows, :]
            gv = g_ref[...]
            err = _rms(x, gv) - t_ref[rows, :]
            part = part + 0.5 * jnp.sum(jnp.mean(err * err, axis=-1, keepdims=True), axis=0, keepdims=True)
            dx, dg_rows = _rms_bwd(err * (1.0 / D), x, gv)
            dh_ref[rows, :] = dx
            return dg + dg_rows, part
        dg, part = _row_loop(ts, rows_fn, (jnp.zeros((1, D), F32), jnp.zeros((1, 1), F32)))

        @pl.when(i == 0)
        def _():
            dg_ref[...] = dg
            loss_ref[...] = jnp.broadcast_to(part, loss_ref.shape)

        @pl.when(i > 0)
        def _():
            dg_ref[...] += dg
            loss_ref[...] += jnp.broadcast_to(part, loss_ref.shape)

    return pl.pallas_call(
        body, grid=(S // ts,), name="loss_head",
        in_specs=[row, vec, row], out_specs=[row, vec, pl.BlockSpec((1, 128), lambda i: (0, 0))],
        out_shape=[jax.ShapeDtypeStruct((S, D), F32), jax.ShapeDtypeStruct((1, D), F32),
                   jax.ShapeDtypeStruct((1, 128), F32)],
        compiler_params=_params(40, ("arbitrary",)),
    )(h, g, tgt)


def _pool_specs():
    blk = pl.BlockSpec((S, PG), lambda gi: (0, gi))
    wp = pl.BlockSpec((None, PG, PG), lambda gi: (gi, 0, 0))
    sc = pl.BlockSpec((1, PG), lambda gi: (0, gi))
    return blk, wp, sc


def _pool_fwd(u, wp_b, scale):
    blk, wp, sc = _pool_specs()

    def body(u_ref, wp_ref, sc_ref, y_ref, pooled_ref):
        win = 2 << pl.program_id(0)
        row = lax.broadcasted_iota(jnp.int32, (S, PG), 0)
        x = u_ref[...]
        s = x
        for k in (1, 2, 4, 8):
            s = s + jnp.where((row >= k) & (k < win), pltpu.roll(s, k, 0), 0.0)
        cnt = jnp.minimum(row + 1, win).astype(F32)
        pooled = (s / cnt - x).astype(BF16)
        pooled_ref[...] = pooled
        y_ref[...] = (_dot(pooled, wp_ref[...]) * sc_ref[...]).astype(BF16)

    return pl.pallas_call(
        body, grid=(len(POOL_WINDOWS),), name="pool_fwd", in_specs=[blk, wp, sc], out_specs=[blk, blk],
        out_shape=[jax.ShapeDtypeStruct((S, D_POOL), BF16), jax.ShapeDtypeStruct((S, D_POOL), BF16)],
        compiler_params=_params(40),
    )(u, wp_b, scale)


def _pool_bwd(dy, pooled, wp_b, scale):
    blk, wp, sc = _pool_specs()

    def body(dy_ref, p_ref, wp_ref, sc_ref, du_ref, dwp_ref, dsc_ref):
        win = 2 << pl.program_id(0)
        row = lax.broadcasted_iota(jnp.int32, (S, PG), 0)
        dyv = dy_ref[...]
        pooled = p_ref[...]
        w = wp_ref[...]
        dsc_ref[...] = jnp.sum(dyv * _dot(pooled, w), axis=0, keepdims=True)
        dz = (dyv * sc_ref[...]).astype(BF16)
        dwp_ref[...] = _dot_tn(pooled, dz)
        dpooled = _dot_nt(dz, w)
        cnt = jnp.minimum(row + 1, win).astype(F32)
        fs = dpooled / cnt
        for k in (1, 2, 4, 8):
            fs = fs + jnp.where((row < S - k) & (k < win), pltpu.roll(fs, S - k, 0), 0.0)
        du_ref[...] = (fs - dpooled).astype(BF16)

    return pl.pallas_call(
        body, grid=(len(POOL_WINDOWS),), name="pool_bwd", in_specs=[blk, blk, wp, sc], out_specs=[blk, wp, sc],
        out_shape=[jax.ShapeDtypeStruct((S, D_POOL), BF16), jax.ShapeDtypeStruct((len(POOL_WINDOWS), PG, PG), F32),
                   jax.ShapeDtypeStruct((1, D_POOL), F32)],
        compiler_params=_params(40),
    )(dy, pooled, wp_b, scale)


def _gla_in_specs(chunk_of):
    def at(width, col):
        return pl.BlockSpec((CH, width), lambda n: (chunk_of(n), col))
    return [at(H * DK, O_Q // (H * DK)), at(H * DK, O_K // (H * DK)), at(H * DV, O_V // (H * DV)),
            at(H * DV, O_G // (H * DV)), at(RANK_PAD, O_R // RANK_PAD)]


def _gla_decay_terms(lr_ref, wa_ref, ba_ref, q_ref, k_ref):
    row = lax.broadcasted_iota(jnp.int32, (CH, CH), 0)
    col = lax.broadcasted_iota(jnp.int32, (CH, CH), 1)
    tril = row >= col
    z = _dot(lr_ref[...].astype(BF16), wa_ref[...]) + ba_ref[...]
    la = _log_sigmoid(z) / GATE_NORM
    b = _tri_dot(jnp.where(tril, 1.0, 0.0).astype(BF16), la)
    bl = jnp.sum(la, axis=0, keepdims=True)
    e_b, e_nb, e_tb = jnp.exp(b), jnp.exp(-b), jnp.exp(bl - b)
    kk = k_ref[...]
    q_dec = (q_ref[...] * QK_SCALE) * e_b
    return tril, z, e_b, e_nb, e_tb, jnp.exp(bl), q_dec, kk * e_nb, kk * e_tb


def _gla_fwd(u, wa_b, ba, gn):
    wide = pl.BlockSpec((CH, H * DV), lambda n: (n, 0))

    def body(q_ref, k_ref, v_ref, g_ref, lr_ref, wa_ref, ba_ref, gn_ref, y_ref, o_ref, st_ref, state):
        @pl.when(pl.program_id(0) == 0)
        def _():
            state[...] = jnp.zeros_like(state)

        tril, _, _, _, _, dec, q_dec, k_inv, k_tail = _gla_decay_terms(lr_ref, wa_ref, ba_ref, q_ref, k_ref)
        for hd in range(H):
            ks, vs = slice(hd * DK, (hd + 1) * DK), slice(hd * DV, (hd + 1) * DV)
            qb, kib, ktb = q_dec[:, ks].astype(BF16), k_inv[:, ks].astype(BF16), k_tail[:, ks].astype(BF16)
            vb = v_ref[:, vs].astype(BF16)
            p = jnp.where(tril, _dot_nt(qb, kib), 0.0)
            st = state[hd]
            o = _dot(p.astype(BF16), vb) + _dot_nt(qb, st.astype(BF16))
            st_ref[hd] = st
            state[hd] = st * dec[:, ks] + _dot_tn(vb, ktb)
            o_ref[:, vs] = o
            on = _rms(o, gn_ref[...])
            gg = g_ref[:, vs]
            y_ref[:, vs] = (on * (gg * _sigmoid(gg))).astype(BF16)

    return pl.pallas_call(
        body, grid=(NCH,), name="gla_fwd",
        in_specs=_gla_in_specs(lambda n: n) + [pl.BlockSpec((RANK_PAD, H * DK), lambda n: (0, 0)),
                                               pl.BlockSpec((1, H * DK), lambda n: (0, 0)),
                                               pl.BlockSpec((1, DV), lambda n: (0, 0))],
        out_specs=[wide, wide, pl.BlockSpec((None, H, DV, DK), lambda n: (n, 0, 0, 0))],
        out_shape=[jax.ShapeDtypeStruct((S, H * DV), BF16), jax.ShapeDtypeStruct((S, H * DV), F32),
                   jax.ShapeDtypeStruct((NCH, H, DV, DK), F32)],
        scratch_shapes=[pltpu.VMEM((H, DV, DK), F32)],
        compiler_params=_params(32, ("arbitrary",)),
    )(u, u, u, u, u, wa_b, ba, gn)


GLA_DU = 2 * H * DK + 2 * H * DV + RANK_PAD


def _gla_bwd(u, o_arr, states, dy, wa_b, ba, gn, after=()):
    rev = lambda n: NCH - 1 - n
    wide = pl.BlockSpec((CH, H * DV), lambda n: (rev(n), 0))

    def body(q_ref, k_ref, v_ref, g_ref, lr_ref, o_ref, st_ref, dy_ref, wa_ref, ba_ref, gn_ref,
             du_ref, dwa_ref, dba_ref, dgn_ref, gstate, db_scr, dbl_scr):
        @pl.when(pl.program_id(0) == 0)
        def _():
            gstate[...] = jnp.zeros_like(gstate)
            dwa_ref[...] = jnp.zeros_like(dwa_ref)
            dba_ref[...] = jnp.zeros_like(dba_ref)
            dgn_ref[...] = jnp.zeros_like(dgn_ref)

        tril, z, e_b, e_nb, e_tb, dec, q_dec, k_inv, k_tail = _gla_decay_terms(lr_ref, wa_ref, ba_ref, q_ref, k_ref)
        gnv = gn_ref[...]
        dgn = jnp.zeros((1, DV), F32)
        for hd in range(H):
            ks, vs = slice(hd * DK, (hd + 1) * DK), slice(hd * DV, (hd + 1) * DV)
            qh, kih, kth = q_dec[:, ks], k_inv[:, ks], k_tail[:, ks]
            qb, kib, ktb = qh.astype(BF16), kih.astype(BF16), kth.astype(BF16)
            vb = v_ref[:, vs].astype(BF16)
            o = o_ref[:, vs]
            gg = g_ref[:, vs]
            dyh = dy_ref[:, vs]
            r = lax.rsqrt(jnp.mean(o * o, axis=-1, keepdims=True) + EPS)
            xh = o * r
            sg = _sigmoid(gg)
            dgate = dyh * (xh * gnv) * (sg * (1.0 + gg * (1.0 - sg)))
            don = dyh * (gg * sg)
            dgn = dgn + jnp.sum(don * xh, axis=0, keepdims=True)
            dxh = don * gnv
            d_o = (r * (dxh - xh * jnp.mean(dxh * xh, axis=-1, keepdims=True))).astype(BF16)
            pb = jnp.where(tril, _dot_nt(qb, kib), 0.0).astype(BF16)
            dpb = jnp.where(tril, _dot_nt(d_o, vb), 0.0).astype(BF16)
            gt = gstate[hd]
            gtb = gt.astype(BF16)
            st = st_ref[hd]
            dv = _dot_tn(pb, d_o) + _dot_nt(ktb, gtb)
            dq_dec = _dot(dpb, kib) + _dot(d_o, st.astype(BF16))
            dk_inv = _dot_tn(dpb, qb)
            dk_tail = _dot(vb, gtb)
            ddec = jnp.sum(gt * st, axis=0, keepdims=True)
            gstate[hd] = _dot_tn(d_o, qb) + dec[:, ks] * gt
            du_ref[:, ks] = (dq_dec * QK_SCALE * e_b[:, ks]).astype(BF16)
            du_ref[:, H * DK + hd * DK:H * DK + (hd + 1) * DK] = (dk_inv * e_nb[:, ks] + dk_tail * e_tb[:, ks]).astype(BF16)
            du_ref[:, 2 * H * DK + hd * DV:2 * H * DK + (hd + 1) * DV] = dv.astype(BF16)
            du_ref[:, 2 * H * DK + H * DV + hd * DV:2 * H * DK + H * DV + (hd + 1) * DV] = dgate.astype(BF16)
            db_scr[:, ks] = dq_dec * qh - dk_inv * kih - dk_tail * kth
            dbl_scr[:, ks] = jnp.sum(dk_tail * kth, axis=0, keepdims=True) + ddec * dec[:, ks]
        dgn_ref[...] += dgn
        row = lax.broadcasted_iota(jnp.int32, (CH, CH), 0)
        col = lax.broadcasted_iota(jnp.int32, (CH, CH), 1)
        dla = _tri_dot(jnp.where(row <= col, 1.0, 0.0).astype(BF16), db_scr[...]) + dbl_scr[...]
        dz = dla * (1.0 / GATE_NORM) * _sigmoid(-z)
        dzb = dz.astype(BF16)
        du_ref[:, GLA_DU - RANK_PAD:] = _dot_nt(dzb, wa_ref[...]).astype(BF16)
        dwa_ref[...] += _dot_tn(lr_ref[...].astype(BF16), dzb)
        dba_ref[...] += jnp.sum(dz, axis=0, keepdims=True)

    full = lambda shape: pl.BlockSpec(shape, lambda n: (0,) * len(shape))
    body, extra = _ordered_after(body, 11, after)
    return pl.pallas_call(
        body, grid=(NCH,), name="gla_bwd",
        in_specs=_gla_in_specs(rev) + [wide, pl.BlockSpec((None, H, DV, DK), lambda n: (rev(n), 0, 0, 0)),
                                       pl.BlockSpec((CH, H * DV), lambda n: (rev(n), 1)),
                                       full((RANK_PAD, H * DK)), full((1, H * DK)), full((1, DV))] + extra,
        out_specs=[pl.BlockSpec((CH, GLA_DU), lambda n: (rev(n), 0)), full((RANK_PAD, H * DK)), full((1, H * DK)),
                   full((1, DV))],
        out_shape=[jax.ShapeDtypeStruct((S, GLA_DU), BF16), jax.ShapeDtypeStruct((RANK_PAD, H * DK), F32),
                   jax.ShapeDtypeStruct((1, H * DK), F32), jax.ShapeDtypeStruct((1, DV), F32)],
        scratch_shapes=[pltpu.VMEM((H, DV, DK), F32), pltpu.VMEM((CH, H * DK), F32), pltpu.VMEM((1, H * DK), F32)],
        compiler_params=_params(32, ("arbitrary",)),
    )(u, u, u, u, u, o_arr, states, dy, wa_b, ba, gn, *after)


def _ffn_dw_in(n, dgu, tag, after=(), first_tile=0, ntiles=D // 512):
    return _tn_matmul(n, dgu, pl.BlockSpec((S, 512), lambda s, m: (0, first_tile + m)),
                      pl.BlockSpec((None, S, WIN_SHARD), lambda s, m: (s // (NDEV // 2), 0, s % (NDEV // 2))),
                      jax.ShapeDtypeStruct((512 * ntiles, 2 * FF), BF16), pl.BlockSpec((512, WIN_SHARD), lambda s, m: (m, s)),
                      (NDEV, ntiles), tag + "_dw_in", 32, after)


def _ffn_dw_out(act, df, tag, after=()):
    return _tn_matmul(act, df, pl.BlockSpec((S, 512), lambda m: (0, m)), pl.BlockSpec((S, D), lambda m: (0, 0)),
                      jax.ShapeDtypeStruct((FF, D), BF16), pl.BlockSpec((512, D), lambda m: (m, 0)),
                      (FF // 512,), tag + "_dw_out", 40, after)


def _fwd_ffn1(x, w, after=()):
    return _ffn_fwd(x, w["ffn1_norm"], w["ffn1_w_in"], w["ffn1_w_out"], "ffn1_fwd", after)


def _fwd_mixer(h1, w, after=(), at=lambda point, value: ()):
    u, n2 = _norm_matmul(h1, w["mix_norm"], w["w_in_mix"], "mix_in", after)
    y_pool, pooled = _pool_fwd(u, w["w_pool"], w["pool_scale"])
    y_gla, o_gla, states = _gla_fwd(u, w["w_alpha"], w["b_alpha"], w["gla_norm"])
    y = jnp.concatenate([y_pool, y_gla], axis=1)
    h2 = _matmul_residual(y, w["w_out_mix"], h1, "mix_out", at("gla_fwd", y))
    return h2, dict(u=u, n2=n2, pooled=pooled, o_gla=o_gla, states=states, y=y)


def _fwd_ffn2_loss(h2, tgt, w, after=()):
    h3, n3, gu3 = _ffn_fwd(h2, w["ffn2_norm"], w["ffn2_w_in"], w["ffn2_w_out"], "ffn2_fwd", after)
    dh3, d_final, loss = _loss_head(h3, w["final_norm"], tgt)
    return dh3, d_final, loss, n3, gu3


def _backward(x, h1, h2, n1, gu1, sv, n3, gu3, dh3, w, at=lambda point, value: ()):
    g = {}
    dgu3, act3, df3, dh2, dh2b, g["ffn2_norm"] = _ffn_bwd_x(dh3, h2, w["ffn2_norm"], gu3, w["ffn2_w_in"], w["ffn2_w_out"], "ffn2_bwd")
    g["ffn2_w_out"] = _ffn_dw_out(act3, df3, "ffn2")
    g["ffn2_w_in"] = _ffn_dw_in(n3, dgu3, "ffn2", at("ffn2_w_out", g["ffn2_w_out"]))
    dy = _nt_matmul(dh2b, w["w_out_mix"], "mix_out_bwd", at("ffn2_w_in", g["ffn2_w_in"]))
    g["w_out_mix"] = _tn_matmul(sv["y"], dh2b, pl.BlockSpec((S, 512), lambda m: (0, m)), pl.BlockSpec((S, D), lambda m: (0, 0)),
                                jax.ShapeDtypeStruct((D, D), BF16), pl.BlockSpec((512, D), lambda m: (m, 0)), (D // 512,),
                                "mix_out_dw", 40)
    du_pool, g["w_pool"], g["pool_scale"] = _pool_bwd(dy, sv["pooled"], w["w_pool"], w["pool_scale"])
    du_gla, g["w_alpha"], g["b_alpha"], g["gla_norm"] = _gla_bwd(sv["u"], sv["o_gla"], sv["states"], dy, w["w_alpha"], w["b_alpha"],
                                                                 w["gla_norm"], at("pool_bwd", du_pool))
    du = jnp.concatenate([du_pool, du_gla], axis=1)
    g["w_in_mix"] = _tn_matmul(du, sv["n2"], pl.BlockSpec((S, 1408), lambda j, m: (0, j)), pl.BlockSpec((S, 512), lambda j, m: (0, m)),
                               jax.ShapeDtypeStruct((D_IN_PAD, D), BF16), pl.BlockSpec((1408, 512), lambda j, m: (j, m)),
                               (D_IN_PAD // 1408, D // 512), "mix_in_dw", 32)
    dh1, g["mix_norm"] = _matmul_normbwd(du, w["w_in_mix"], h1, w["mix_norm"], dh2, "mix_in_bwd", at("mixer_weights", g))
    dgu1, act1, df1 = _ffn_bwd_gu(dh1, gu1, w["ffn1_w_out"], "ffn1_bwd_gu", at("dh1", dh1))
    g["ffn1_w_out"] = _ffn_dw_out(act1, df1, "ffn1")
    half = S // FFN_TS // 2
    dx, dn_a = _ffn_bwd_dx(dgu1, dh1, x, w["ffn1_norm"], w["ffn1_w_in"], 0, half, "ffn1_bwd_dx_a",
                           at("ffn1_w_out", g["ffn1_w_out"]))
    g["ffn1_w_in_top"] = _ffn_dw_in(n1, dgu1, "ffn1_top", at("ffn1_dx_a", dx), 0, D // 1024)
    g["ffn1_w_in_bottom"] = _ffn_dw_in(n1, dgu1, "ffn1_bottom", at("ffn1_w_in_top", g["ffn1_w_in_top"]), D // 1024, D // 1024)
    dx, dn_b = _ffn_bwd_dx(dgu1, dh1, x, w["ffn1_norm"], w["ffn1_w_in"], half, half, "ffn1_bwd_dx_b",
                           at("ffn1_w_in_bottom", g["ffn1_w_in_bottom"]), into=dx)
    g["ffn1_norm"] = dn_a + dn_b
    return dx, g


def _local_step(x, tgt, w):
    h1, n1, gu1 = _fwd_ffn1(x, w)
    h2, sv = _fwd_mixer(h1, w)
    dh3, d_final, loss, n3, gu3 = _fwd_ffn2_loss(h2, tgt, w)
    dx, g = _backward(x, h1, h2, n1, gu1, sv, n3, gu3, dh3, w)
    return loss, dx, dict(final_norm=d_final, **g)


def _coords(p):
    return (p // 4, (p // 2) % 2, p % 2)


NCHIP = 4


def _place():
    return lax.axis_index("x"), lax.axis_index("y"), lax.axis_index("c")


def _rel_chip(x, y, rel):
    return ((1 - x) if rel & 1 else x, (1 - y) if rel & 2 else y)


def _dev_index(x, y, c):
    return 4 * x + 2 * y + c


def _cols(ref, p, width):
    return ref.at[:, pl.ds(pl.multiple_of(p * width, 128), width)]


def _sems(na, n):
    return [pltpu.SemaphoreType.DMA((na, n)), pltpu.SemaphoreType.DMA((na, n)), pltpu.SemaphoreType.DMA((na,))]


def _gather(items, name):
    arrays = [a for a, _ in items]
    kinds = [k for _, k in items]
    na = len(arrays)
    out_shape = [jax.ShapeDtypeStruct((NDEV,) + a.shape if k == "bcast" else (a.shape[0], NDEV * a.shape[1]), a.dtype)
                 for a, k in items]

    def body(*refs):
        ins, outs = refs[:na], refs[na:2 * na]
        send_sems, recv_sems, local_sems = refs[2 * na:]
        x, y, c = _place()
        sibling = (x, y, 1 - c)
        here, over_x, over_y, across = (x, y), (1 - x, y), (x, 1 - y), (1 - x, 1 - y)

        def half(ref, h):
            rows = ref.shape[0] // 2
            return ref.at[pl.ds(h * rows, rows), :]

        def slab(a, chip, core, h=None):
            ref = _slab(outs[a], kinds[a], _dev_index(*chip, core), ins[a].shape[1])
            return ref if h is None else half(ref, h)

        def copy(a, k, src, dst, to):
            return pltpu.make_async_remote_copy(src, dst, send_sems.at[a, k], recv_sems.at[a, k], device_id=to, device_id_type=MESH)

        sent = []

        def send(a, k, src, dst, to):
            sent.append(copy(a, k, src, dst, to))
            sent[-1].start()

        def arrived(a, k, chip, core, h=None):
            ref = slab(a, chip, core, h)
            copy(a, k, ref, ref, sibling).wait_recv()
            return ref

        local = [pltpu.make_async_copy(ins[a], slab(a, here, c), local_sems.at[a]) for a in range(na)]
        for cp in local:
            cp.start()
        for k, h, chip in ((1, 0, over_x), (4, 1, over_y), (2, 1, over_x), (5, 0, over_y)):
            for a in range(na):
                send(a, k, half(ins[a], h), slab(a, here, c, h), (*chip, c))
        for a in range(na):
            send(a, 0, ins[a], slab(a, here, c), sibling)
        for k, chip, h, onward, to, down in ((1, over_x, 0, 6, over_y, 7), (4, over_y, 1, 3, over_x, 10),
                                             (2, over_x, 1, None, None, 8), (5, over_y, 0, None, None, 9),
                                             (3, across, 1, None, None, 12), (6, across, 0, None, None, 11)):
            for a in range(na):
                ref = arrived(a, k, chip, c, h)
                if onward is not None:
                    send(a, onward, ref, ref, (*to, c))
                send(a, down, ref, ref, sibling)
        for a in range(na):
            arrived(a, 0, here, 1 - c)
        for k, chip, h in ((7, over_x, 0), (10, over_y, 1), (8, over_x, 1), (9, over_y, 0), (12, across, 1), (11, across, 0)):
            for a in range(na):
                arrived(a, k, chip, 1 - c, h)
        for cp in sent:
            cp.wait_send()
        for cp in local:
            cp.wait()

    return pl.pallas_call(body, name=name, in_specs=[ANY] * na, out_specs=[ANY] * na, out_shape=out_shape,
                          scratch_shapes=_sems(na, 13))(*arrays)


def _pair_add(own, kind, got, table, tr, name):
    _, rows, cols = got.shape
    if kind == "scatter":
        own_spec = pl.BlockSpec((None, tr, cols), lambda rel, i, t: (t[rel], i, 0))
    else:
        own_spec = pl.BlockSpec((tr, cols), lambda rel, i, t: (i, t[rel]))
    blk = pl.BlockSpec((None, tr, cols), lambda rel, i, t: (rel, i, 0))

    def body(t_ref, a_ref, b_ref, o_ref):
        o_ref[...] = (a_ref[...].astype(F32) + b_ref[...].astype(F32)).astype(o_ref.dtype)

    return pl.pallas_call(
        body, name=name, out_shape=jax.ShapeDtypeStruct(got.shape, got.dtype),
        grid_spec=pltpu.PrefetchScalarGridSpec(num_scalar_prefetch=1, grid=(NCHIP, rows // tr), in_specs=[own_spec, blk],
                                               out_specs=blk),
        compiler_params=_params(32),
    )(table, own, got)


HBM = pl.BlockSpec(memory_space=pltpu.HBM)
SEM = pl.BlockSpec(memory_space=pltpu.SEMAPHORE)
DATAFLOW = pltpu.SideEffectType.DATAFLOW_SIDE_EFFECTING


def _pair_copies(kinds):
    def describe(srcs, lands, send_sems, recv_sems):
        x, y, c = _place()
        na = len(srcs)
        for rel in range(NCHIP):
            p = _dev_index(*_rel_chip(x, y, rel), 1 - c)
            for a in range(na):
                src = srcs[a].at[p] if kinds[a] == "scatter" else _cols(srcs[a], p, srcs[a].shape[1] // NDEV)
                cp = pltpu.make_async_remote_copy(src, lands[a].at[rel], send_sems.at[rel * na + a], recv_sems.at[rel * na + a],
                                                  device_id=(x, y, 1 - c), device_id_type=MESH)
                yield cp, cp
    return describe


def _chip_copies(srcs, lands, send_sems, recv_sems):
    x, y, c = _place()
    na = len(srcs)
    for rel in range(1, NCHIP):
        for a in range(na):
            i = (rel - 1) * na + a
            cp = pltpu.make_async_remote_copy(srcs[a].at[rel], lands[a].at[rel], send_sems.at[i], recv_sems.at[i],
                                              device_id=(*_rel_chip(x, y, rel), c), device_id_type=MESH)
            yield cp, cp


def _slab(ref, kind, s, width):
    return _cols(ref, s, width) if kind == "bcast_cols" else ref.at[s]


def _all_copies(srcs, lands, send_sems, recv_sems):
    x, y, c = _place()
    na = len(srcs)
    me = _dev_index(x, y, c)
    for a in range(na):
        yield pltpu.make_async_copy(srcs[a], lands[a].at[me], send_sems.at[a]), None
    for k in range(1, NDEV):
        to, frm = (me + k) % NDEV, (me + NDEV - k) % NDEV
        for a in range(na):
            i = k * na + a
            send = pltpu.make_async_remote_copy(srcs[a], lands[a].at[me], send_sems.at[i], recv_sems.at[i],
                                                device_id=_coords(to), device_id_type=MESH)
            arrival = pltpu.make_async_remote_copy(srcs[a], lands[a].at[frm], send_sems.at[i], recv_sems.at[i],
                                                   device_id=_coords(to), device_id_type=MESH)
            yield send, arrival


def _gather_copies(kinds):
    def describe(srcs, lands, send_sems, recv_sems):
        x, y, c = _place()
        na = len(srcs)
        me = _dev_index(x, y, c)
        for a in range(na):
            yield pltpu.make_async_copy(srcs[a], _slab(lands[a], kinds[a], me, srcs[a].shape[-1]),
                                        send_sems.at[NCHIP * na + a]), None
        for rel in range(NCHIP):
            to = (x, y, 1 - c) if rel == 0 else (*_rel_chip(x, y, rel), c)
            for a in range(na):
                width = srcs[a].shape[-1]
                i = rel * na + a
                send = pltpu.make_async_remote_copy(srcs[a], _slab(lands[a], kinds[a], me, width), send_sems.at[i], recv_sems.at[i],
                                                    device_id=to, device_id_type=MESH)
                arrival = pltpu.make_async_remote_copy(srcs[a], _slab(lands[a], kinds[a], _dev_index(*to), width), send_sems.at[i],
                                                       recv_sems.at[i], device_id=to, device_id_type=MESH)
                yield send, arrival
    return describe


def _relay_copies(kinds, widths, phase):
    def describe(srcs, lands, send_sems, recv_sems):
        x, y, c = _place()
        na = len(lands)
        here, over_x, over_y, across = (x, y), (1 - x, y), (x, 1 - y), (1 - x, 1 - y)
        sibling = (x, y, 1 - c)

        def half(ref, h):
            rows = ref.shape[0] // 2
            return ref if h is None else ref.at[pl.ds(h * rows, rows), :]

        def slab(a, chip, core, h=None):
            return half(_slab(lands[a], kinds[a], _dev_index(*chip, core), widths[a]), h)

        def pair(a, k, src, dst, to, arrival):
            i = k * na + a
            return (pltpu.make_async_remote_copy(src, dst, send_sems.at[i], recv_sems.at[i], device_id=to, device_id_type=MESH),
                    pltpu.make_async_remote_copy(arrival, arrival, send_sems.at[i], recv_sems.at[i], device_id=to,
                                                 device_id_type=MESH))

        if phase == 1:
            for a in range(na):
                yield pltpu.make_async_copy(srcs[a], slab(a, here, c), send_sems.at[a]), None
            for k, h, chip in ((1, 0, over_x), (2, 1, over_y), (3, 1, over_x), (4, 0, over_y)):
                for a in range(na):
                    yield pair(a, k, half(srcs[a], h), slab(a, here, c, h), (*chip, c), slab(a, chip, c, h))
            for a in range(na):
                yield pair(a, 5, srcs[a], slab(a, here, c), sibling, slab(a, here, 1 - c))
        elif phase == 2:
            for a in range(na):
                yield pair(a, 0, slab(a, over_x, c, 0), slab(a, over_x, c, 0), (*over_y, c), slab(a, across, c, 0))
                yield pair(a, 1, slab(a, over_y, c, 1), slab(a, over_y, c, 1), (*over_x, c), slab(a, across, c, 1))
            for k, (chip, h) in enumerate(((over_x, 0), (over_y, 1), (over_x, 1), (over_y, 0)), start=2):
                for a in range(na):
                    yield pair(a, k, slab(a, chip, c, h), slab(a, chip, c, h), sibling, slab(a, chip, 1 - c, h))
        else:
            for h in (0, 1):
                for a in range(na):
                    yield pair(a, h, slab(a, across, c, h), slab(a, across, c, h), sibling, slab(a, across, 1 - c, h))
    return describe


RELAY_COPIES = {1: 6, 2: 6, 3: 2}


def _relay_finish(arrays, kinds, widths, name):
    na = len(arrays)
    describe = _relay_copies(kinds, widths, 3)

    def body(*refs):
        copies = list(describe((), refs[na:2 * na], refs[2 * na], refs[2 * na + 1]))
        for send, _ in copies:
            send.start()
        for send, arrival in copies:
            send.wait_send()
            arrival.wait_recv()

    return pl.pallas_call(body, name=name, in_specs=[ANY] * na, out_specs=[ANY] * na,
                          out_shape=[jax.ShapeDtypeStruct(a.shape, a.dtype) for a in arrays],
                          input_output_aliases={i: i for i in range(na)},
                          scratch_shapes=[pltpu.SemaphoreType.DMA((RELAY_COPIES[3] * na,))] * 2)(*arrays)


def _pass_to_sibling(arrays, kinds, widths, name):
    na = len(arrays)

    def body(*refs):
        bufs = refs[na:2 * na]
        send_sems, recv_sems = refs[2 * na:]
        x, y, c = _place()
        copies = []
        for rel in range(1, NCHIP):
            for a in range(na):
                mine = _slab(bufs[a], kinds[a], _dev_index(*_rel_chip(x, y, rel), c), widths[a])
                theirs = _slab(bufs[a], kinds[a], _dev_index(*_rel_chip(x, y, rel), 1 - c), widths[a])
                send = pltpu.make_async_remote_copy(mine, mine, send_sems.at[a, rel], recv_sems.at[a, rel],
                                                    device_id=(x, y, 1 - c), device_id_type=MESH)
                send.start()
                copies.append((send, pltpu.make_async_remote_copy(theirs, theirs, send_sems.at[a, rel], recv_sems.at[a, rel],
                                                                  device_id=(x, y, 1 - c), device_id_type=MESH)))
        for send, arrival in copies:
            send.wait_send()
            arrival.wait_recv()

    return pl.pallas_call(body, name=name, in_specs=[ANY] * na, out_specs=[ANY] * na,
                          out_shape=[jax.ShapeDtypeStruct(a.shape, a.dtype) for a in arrays],
                          input_output_aliases={i: i for i in range(na)}, scratch_shapes=_sems(na, NCHIP)[:2])(*arrays)


def _start_copies(name, srcs, lands, describe, ncopies, after=()):
    arrays = list(srcs) + list(lands)
    ns, n, nin = len(srcs), len(arrays), len(arrays) + len(after)

    def body(*refs):
        for send, _ in describe(refs[:ns], refs[ns:n], refs[nin], refs[nin + 1]):
            send.start()
        refs[-1][...] = jnp.zeros_like(refs[-1])

    out = pl.pallas_call(
        body, name=name,
        out_shape=(pltpu.SemaphoreType.DMA((ncopies,)), pltpu.SemaphoreType.DMA((ncopies,)),
                   *[pltpu.HBM(a.shape, a.dtype) for a in arrays], jax.ShapeDtypeStruct((8, 128), F32)),
        in_specs=[HBM] * n + [ANY] * len(after), out_specs=(SEM, SEM, *[HBM] * n, pl.BlockSpec(memory_space=pltpu.VMEM)),
        input_output_aliases={i: 2 + i for i in range(n)},
        compiler_params=pltpu.CompilerParams(has_side_effects=DATAFLOW),
    )(*[pltpu.with_memory_space_constraint(a, pltpu.HBM) for a in arrays], *after)
    return out[0], out[1], list(out[2:2 + n]), out[-1]


def _wait_copies(name, send_sems, recv_sems, thru, ns, describe, after):
    n = len(thru)

    def body(*refs):
        for send, arrival in describe(refs[:ns], refs[ns:n], refs[n], refs[n + 1]):
            if arrival is None:
                send.wait()
            else:
                send.wait_send()
                arrival.wait_recv()

    out = pl.pallas_call(
        body, name=name, out_shape=tuple(pltpu.HBM(a.shape, a.dtype) for a in thru),
        in_specs=[HBM] * n + [SEM, SEM] + [ANY] * len(after), out_specs=tuple([HBM] * n),
        input_output_aliases={i: i for i in range(n)},
        compiler_params=pltpu.CompilerParams(has_side_effects=DATAFLOW),
    )(*thru, send_sems, recv_sems, *after)
    return list(out[:ns]), list(out[ns:])


def _adamw(parts, w, m, v, tr, name, tc=None, first_row=0, into=None):
    rows, cols = w.shape
    part_rows = parts[0][0].shape[1]
    tc = cols if tc is None else tc
    nparts = len(parts)
    tile0 = first_row // tr
    blk = pl.BlockSpec((tr, tc), lambda i, j: (tile0 + i, j))
    carried = [] if into is None else list(into)

    def slab_spec(s):
        return pl.BlockSpec((None, tr, tc), lambda i, j: (s, i, j))

    def body(*refs):
        p_refs = refs[:nparts]
        w_ref, m_ref, v_ref = refs[nparts:nparts + 3]
        g_ref, d_ref, nm_ref, nv_ref = refs[nparts + 3 + len(carried):]
        g = p_refs[0][...].astype(F32)
        for p_ref in p_refs[1:]:
            g = g + p_ref[...].astype(F32)
        nm = ADAM_B1 * m_ref[...] + (1.0 - ADAM_B1) * g
        nv = ADAM_B2 * v_ref[...] + (1.0 - ADAM_B2) * (g * g)
        m_hat = nm / (1.0 - ADAM_B1 ** ADAM_STEP)
        v_hat = nv / (1.0 - ADAM_B2 ** ADAM_STEP)
        g_ref[...] = g
        d_ref[...] = -ADAM_LR * (m_hat / (jnp.sqrt(v_hat) + ADAM_EPS) + ADAM_WD * w_ref[...])
        nm_ref[...] = nm
        nv_ref[...] = nv

    return pl.pallas_call(
        body, grid=(part_rows // tr, cols // tc), name=name,
        in_specs=[slab_spec(s) for _, s in parts] + [blk, blk, blk] + [ANY] * len(carried), out_specs=[blk] * 4,
        out_shape=[jax.ShapeDtypeStruct((rows, cols), F32)] * 4,
        input_output_aliases={nparts + 3 + k: k for k in range(len(carried))},
        compiler_params=_params(40),
    )(*[a for a, _ in parts], w, m, v, *carried)


def _pack_small(vals, extra=None):
    flat = [vals[n].reshape(-1).astype(F32) for n, _ in SMALL]
    tail = jnp.zeros((SMALL_ROWS * 128 - LOSS_AT,), F32)
    if extra is not None:
        tail = tail.at[0].set(extra)
    return jnp.concatenate(flat + [tail]).reshape(SMALL_ROWS, 128)


def _unpack_small(packed, like):
    flat, out, at = packed.reshape(-1), {}, 0
    for n, size in SMALL:
        out[n] = flat[at:at + size].reshape(like[n].shape)
        at += size
    return out, flat[LOSS_AT]


def kernel(x, ffn1_norm, ffn1_w_in, ffn1_w_out, mix_norm, w_in_mix, w_pool, pool_scale, w_alpha, b_alpha, gla_norm, w_out_mix, ffn2_norm, ffn2_w_in, ffn2_w_out, final_norm, loss_target, m_ffn1_norm, m_ffn1_w_in, m_ffn1_w_out, m_mix_norm, m_w_in_mix, m_w_pool, m_pool_scale, m_w_alpha, m_b_alpha, m_gla_norm, m_w_out_mix, m_ffn2_norm, m_ffn2_w_in, m_ffn2_w_out, m_final_norm, v_ffn1_norm, v_ffn1_w_in, v_ffn1_w_out, v_mix_norm, v_w_in_mix, v_w_pool, v_pool_scale, v_w_alpha, v_b_alpha, v_gla_norm, v_w_out_mix, v_ffn2_norm, v_ffn2_w_in, v_ffn2_w_out, v_final_norm):
    names = ["ffn1_norm", "ffn1_w_in", "ffn1_w_out", "mix_norm", "w_in_mix", "w_pool", "pool_scale", "w_alpha", "b_alpha",
             "gla_norm", "w_out_mix", "ffn2_norm", "ffn2_w_in", "ffn2_w_out", "final_norm"]
    p = dict(zip(names, [ffn1_norm, ffn1_w_in, ffn1_w_out, mix_norm, w_in_mix, w_pool, pool_scale, w_alpha, b_alpha,
                         gla_norm, w_out_mix, ffn2_norm, ffn2_w_in, ffn2_w_out, final_norm]))
    m = dict(zip(names, [m_ffn1_norm, m_ffn1_w_in, m_ffn1_w_out, m_mix_norm, m_w_in_mix, m_w_pool, m_pool_scale, m_w_alpha,
                         m_b_alpha, m_gla_norm, m_w_out_mix, m_ffn2_norm, m_ffn2_w_in, m_ffn2_w_out, m_final_norm]))
    v = dict(zip(names, [v_ffn1_norm, v_ffn1_w_in, v_ffn1_w_out, v_mix_norm, v_w_in_mix, v_w_pool, v_pool_scale, v_w_alpha,
                         v_b_alpha, v_gla_norm, v_w_out_mix, v_ffn2_norm, v_ffn2_w_in, v_ffn2_w_out, v_final_norm]))

    mx, my, mc = _place()
    table = jnp.stack([_dev_index(*_rel_chip(mx, my, rel), mc) for rel in range(NCHIP)]).astype(jnp.int32)

    def landing(shard, kind):
        shape = (shard.shape[0], NDEV * shard.shape[1]) if kind == "bcast_cols" else (NDEV,) + shard.shape
        return lax.empty(shape, shard.dtype)

    def gather_begin(items, tag, after):
        kinds = [kind for _, kind in items]
        copies = _gather_copies(kinds)
        s, r, thru, tok = _start_copies(tag + "_start", [a for a, _ in items], [landing(a, kind) for a, kind in items], copies,
                                        (NCHIP + 1) * len(items), after)
        return (s, r, thru, copies, kinds, [a.shape[-1] for a, _ in items], tag), tok

    def gather_end(state, after):
        s, r, thru, copies, kinds, widths, tag = state
        _, lands = _wait_copies(tag + "_wait", s, r, thru, len(kinds), copies, after)
        return _pass_to_sibling(lands, kinds, widths, tag + "_pass")

    def shard16(n):
        return p[n][0].astype(BF16)

    g_w1in, g_w1out = _gather([(shard16("ffn1_w_in"), "bcast_cols"), (shard16("ffn1_w_out"), "bcast")], "gather_ffn1")
    mix_state, tok_m = gather_begin([(jnp.transpose(p["w_in_mix"][0]).astype(BF16), "bcast"), (shard16("w_out_mix"), "bcast"),
                                     (p["w_pool"][0].reshape(H * 32, PG), "bcast"), (p["w_alpha"][0], "bcast")], "gather_mix",
                                    (g_w1out,))
    ffn2_items = [(shard16("ffn2_w_in"), "bcast_cols"), (shard16("ffn2_w_out"), "bcast")]
    ffn2_kinds = [kind for _, kind in ffn2_items]
    ffn2_widths = [a.shape[-1] for a, _ in ffn2_items]
    relay = {ph: _relay_copies(ffn2_kinds, ffn2_widths, ph) for ph in (1, 2)}
    f_s, f_r, f_thru, tok_f = _start_copies("gather_ffn2_start", [a for a, _ in ffn2_items],
                                            [landing(a, kind) for a, kind in ffn2_items], relay[1],
                                            RELAY_COPIES[1] * len(ffn2_items), (tok_m,))
    ffn2_lands = {}

    def relay_ffn2(point, value):
        _, lands = _wait_copies("gather_ffn2_wait", f_s, f_r, f_thru, len(ffn2_items), relay[1], (value,))
        ffn2_lands["s"], ffn2_lands["r"], ffn2_lands["thru"], tok = _start_copies(
            "gather_ffn2_relay_start", [], lands, relay[2], RELAY_COPIES[2] * len(ffn2_items))
        return (tok,)

    full ={"ffn1_w_in": g_w1in, "ffn1_w_out": g_w1out.reshape(FF, D), "final_norm": final_norm.reshape(1, D)}
    for n in ("ffn1_norm", "mix_norm", "ffn2_norm", "pool_scale", "b_alpha", "gla_norm"):
        full[n] = p[n]

    xs, tgt = x[0], loss_target[0]
    h1, n1, gu1 = _fwd_ffn1(xs, full, after=(tok_m, tok_f))
    g_wmix, g_wo, g_wpool, g_walpha = gather_end(mix_state, (h1,))
    walpha = jnp.transpose(g_walpha, (1, 0, 2)).reshape(RANK, H * DK)
    full.update({
        "w_in_mix": jnp.pad(g_wmix.reshape(D_IN, D), ((0, D_IN_PAD - D_IN), (0, 0))),
        "w_out_mix": g_wo.reshape(D, D),
        "w_pool": jnp.transpose(g_wpool.reshape(NDEV, H, 32, PG), (1, 0, 2, 3)).reshape(H, PG, PG).astype(BF16),
        "w_alpha": jnp.pad(walpha, ((0, RANK_PAD - RANK), (0, 0))).astype(BF16),
    })
    h2, sv = _fwd_mixer(h1, full, at=relay_ffn2)
    _, lands = _wait_copies("gather_ffn2_relay_wait", ffn2_lands["s"], ffn2_lands["r"], ffn2_lands["thru"], 0, relay[2], (h2,))
    g_w2in, g_w2out = _relay_finish(lands, ffn2_kinds, ffn2_widths, "gather_ffn2_finish")
    full.update({"ffn2_w_in": g_w2in, "ffn2_w_out": g_w2out.reshape(FF, D)})
    dh3, d_final, loss_part, n3, gu3 = _fwd_ffn2_loss(h2, tgt, full)


    def slab_shape(a, kind):
        return (NCHIP,) + (a.shape[1:] if kind == "scatter" else (a.shape[0], a.shape[1] // NDEV))

    def pair_add_all(own, got, tag):
        return [_pair_add(a, kind, got_a, table, tr, "%s_pair_add_%d" % (tag, i))
                for i, ((a, kind, tr), got_a) in enumerate(zip(own, got))]

    def reduce_begin(own, tag, after=()):
        kinds = [kind for _, kind, _ in own]
        copies = _pair_copies(kinds)
        s, r, thru, tok = _start_copies(tag + "_pair_start", [a for a, _, _ in own],
                                        [lax.empty(slab_shape(a, kind), a.dtype) for a, kind, _ in own], copies,
                                        NCHIP * len(own), after)
        return dict(own=own, copies=copies, s=s, r=r, thru=thru, tag=tag), tok

    def reduce_middle(st, after):
        own, tag = st["own"], st["tag"]
        sent, got = _wait_copies(tag + "_pair_wait", st["s"], st["r"], st["thru"], len(own), st["copies"], after)
        pre = pair_add_all([(a, kind, tr) for a, (_, kind, tr) in zip(sent, own)], got, tag)
        st["s"], st["r"], st["thru"], tok = _start_copies(tag + "_chip_start", pre, [lax.empty(a.shape, a.dtype) for a in pre],
                                                          _chip_copies, (NCHIP - 1) * len(pre))
        return tok

    def reduce_end(st, after):
        n = len(st["own"])
        pre, land = _wait_copies(st["tag"] + "_chip_wait", st["s"], st["r"], st["thru"], n, _chip_copies, after)
        return [[(a, 0)] + [(b, rel) for rel in range(1, NCHIP)] for a, b in zip(pre, land)]

    def w_in_item(a):
        return (a, "scatter_cols", 512)

    def w_out_item(a):
        return (a.reshape(NDEV, WOUT_SHARD, D), "scatter", WOUT_SHARD)

    red, small = {}, {}

    def at(point, value):
        if point == "ffn2_w_out":
            red["w2out"], tok = reduce_begin([w_out_item(value)], "ffn2_w_out")
        elif point == "ffn2_w_in":
            tok_a = reduce_middle(red["w2out"], (value,))
            red["w2in"], tok = reduce_begin([w_in_item(value)], "ffn2_w_in", (tok_a,))
        elif point == "pool_bwd":
            tok = reduce_middle(red["w2in"], (value,))
        elif point == "mixer_weights":
            d_wmix8 = value["w_in_mix"][:D_IN].reshape(NDEV, MIX_SHARD, D)
            d_wpool8 = jnp.transpose(value["w_pool"].reshape(H, NDEV, 32, PG), (1, 0, 2, 3)).reshape(NDEV, H * 32, PG)
            d_walpha8 = jnp.transpose(value["w_alpha"][:RANK].reshape(RANK, NDEV, H * DK // NDEV), (1, 0, 2))
            red["mix"], tok = reduce_begin([(d_wmix8, "scatter", MIX_SHARD),
                                            (value["w_out_mix"].reshape(NDEV, D // NDEV, D), "scatter", D // NDEV),
                                            (d_wpool8, "scatter", H * 32), (d_walpha8, "scatter", RANK)], "mix")
        elif point == "dh1":
            tok = reduce_middle(red["mix"], (value,))
        elif point == "ffn1_w_out":
            red["w1out"], tok = reduce_begin([w_out_item(value)], "ffn1_w_out")
        elif point == "ffn1_dx_a":
            tok = reduce_middle(red["w1out"], (value,))
        elif point == "ffn1_w_in_top":
            red["w1in_top"], tok = reduce_begin([w_in_item(value)], "ffn1_w_in_top")
        elif point == "ffn1_w_in_bottom":
            tok_a = reduce_middle(red["w1in_top"], (value,))
            red["w1in_bottom"], tok = reduce_begin([w_in_item(value)], "ffn1_w_in_bottom", (tok_a,))
        return (tok,)

    dx, g = _backward(xs, h1, h2, n1, gu1, sv, n3, gu3, dh3, full, at)
    packed = _pack_small(dict(final_norm=d_final, **g), loss_part[0, 0])
    small_s, small_r, small_thru, tok_s = _start_copies("gather_small_start", [packed], [lax.empty((NDEV,) + packed.shape, F32)],
                                                        _all_copies, NDEV, (dx,))
    tok_c = reduce_middle(red["w1in_bottom"], (tok_s,))

    def upd(parts, n, shape2d, tr):
        res = _adamw(parts, p[n].reshape(shape2d), m[n].reshape(shape2d), v[n].reshape(shape2d), tr, "adamw_" + n)
        return [r.reshape(p[n].shape) for r in res]

    def transposed(a):
        return jnp.transpose(a[0])

    (p_w2out,) = reduce_end(red["w2out"], (tok_c,))
    (p_w2in,) = reduce_end(red["w2in"], (tok_c,))
    p_wmix, p_wo, p_wpool, p_walpha = reduce_end(red["mix"], (tok_c,))
    out = {
        "ffn2_w_in": upd(p_w2in, "ffn2_w_in", (D, WIN_SHARD), 128),
        "ffn2_w_out": upd(p_w2out, "ffn2_w_out", (WOUT_SHARD, D), 64),
        "w_out_mix": upd(p_wo, "w_out_mix", (D // NDEV, D), 64),
        "w_pool": upd(p_wpool, "w_pool", (H * 32, PG), H * 32),
        "w_alpha": upd(p_walpha, "w_alpha", (RANK, H * DK // NDEV), RANK),
    }
    out["w_in_mix"] = [jnp.transpose(r)[None] for r in
                       _adamw(p_wmix, transposed(p["w_in_mix"]), transposed(m["w_in_mix"]), transposed(v["w_in_mix"]),
                              MIX_SHARD, "adamw_w_in_mix", tc=512)]
    _, (r_small,) = _wait_copies("gather_small_wait", small_s, small_r, small_thru, 1, _all_copies,
                                 (out["w_in_mix"][3], out["ffn2_w_in"][3], out["ffn2_w_out"][3], out["w_out_mix"][3]))
    small_res = _adamw([(r_small, s) for s in range(NDEV)], _pack_small(p), _pack_small(m), _pack_small(v), SMALL_ROWS,
                       "adamw_small")
    (p_w1out,) = reduce_end(red["w1out"], (small_res[0],))
    out["ffn1_w_out"] = upd(p_w1out, "ffn1_w_out", (WOUT_SHARD, D), 64)
    w1in = [a.reshape(D, WIN_SHARD) for a in (p["ffn1_w_in"], m["ffn1_w_in"], v["ffn1_w_in"])]
    (p_top,) = reduce_end(red["w1in_top"], (out["ffn1_w_out"][3],))
    top = _adamw(p_top, *w1in, 128, "adamw_ffn1_w_in_top")
    (p_bottom,) = reduce_end(red["w1in_bottom"], (top[3],))
    out["ffn1_w_in"] = [r.reshape(p["ffn1_w_in"].shape) for r in
                        _adamw(p_bottom, *w1in, 128, "adamw_ffn1_w_in_bottom", first_row=D // 2, into=top)]
    unpacked = [_unpack_small(r, p) for r in small_res]
    loss = unpacked[0][1]
    for n, _ in SMALL:
        out[n] = [u[0][n] for u in unpacked]

    return (loss, dx.reshape(1, S, D), *[out[n][0] for n in names], *[out[n][1] for n in names],
            *[out[n][2] for n in names], *[out[n][3] for n in names])
```

```python
import jax
import jax.numpy as jnp
from jax import lax
from jax.experimental import pallas as pl
from jax.experimental.pallas import tpu as pltpu

F32, BF16 = jnp.float32, jnp.bfloat16
MESH = pl.DeviceIdType.MESH
ANY = pl.BlockSpec(memory_space=pl.ANY)

NDEV = 8
S = 2048
D = 2048
FF = 5632
WIN_SHARD = 2 * FF // NDEV
WOUT_SHARD = FF // NDEV
D_POOL = 1024
PG = 256
POOL_WINDOWS = (2, 4, 8, 16)
H = 4
DK = 128
DV = 256
CH = 64
NCH = S // CH
RANK = 16
RANK_PAD = 128
D_IN = 4112
D_IN_PAD = 4224
MIX_SHARD = D_IN // NDEV
O_Q, O_K, O_V, O_G, O_R = 1024, 1536, 2048, 3072, 4096
GATE_NORM = 16.0
QK_SCALE = DK ** -0.5
EPS = 1e-6
ADAM_LR, ADAM_B1, ADAM_B2, ADAM_EPS, ADAM_WD, ADAM_STEP = 0.001, 0.9, 0.999, 1e-08, 0.01, 10
V7X_VMEM_BYTES = 64 << 20

SMALL = (("ffn1_norm", 2048), ("mix_norm", 2048), ("ffn2_norm", 2048), ("final_norm", 2048),
         ("pool_scale", 1024), ("b_alpha", 512), ("gla_norm", 256))
SMALL_ROWS = 80
LOSS_AT = sum(n for _, n in SMALL)


def _params(vmem_mb, sem=None):
    return pltpu.CompilerParams(dimension_semantics=sem, vmem_limit_bytes=min(vmem_mb << 20, V7X_VMEM_BYTES - (4 << 20)))


def _dot(a, b):
    return jnp.dot(a, b, preferred_element_type=F32)


def _dot_nt(a, b):
    return lax.dot_general(a, b, (((1,), (1,)), ((), ())), preferred_element_type=F32)


def _dot_tn(a, b):
    return lax.dot_general(a, b, (((0,), (0,)), ((), ())), preferred_element_type=F32)


def _sigmoid(x):
    return 0.5 * jnp.tanh(0.5 * x) + 0.5


def _log_sigmoid(x):
    return jnp.minimum(x, 0.0) - jnp.log(1.0 + jnp.exp(-jnp.abs(x)))


def _rms(x, g):
    r = lax.rsqrt(jnp.mean(x * x, axis=-1, keepdims=True) + EPS)
    return x * r * g


def _rms_bwd(dn, x, g):
    r = lax.rsqrt(jnp.mean(x * x, axis=-1, keepdims=True) + EPS)
    xh = x * r
    dxh = dn * g
    dx = r * (dxh - xh * jnp.mean(dxh * xh, axis=-1, keepdims=True))
    return dx, jnp.sum(dn * xh, axis=0, keepdims=True)


ROWS = 64


def _row_loop(total, fn, init=0):
    def step(t, carry):
        return fn(pl.ds(pl.multiple_of(t * ROWS, ROWS), ROWS), carry)
    return lax.fori_loop(0, total // ROWS, step, init)


def _split3(x):
    hi = x.astype(BF16)
    r1 = x - hi.astype(F32)
    mid = r1.astype(BF16)
    lo = (r1 - mid.astype(F32)).astype(BF16)
    return hi, mid, lo


def _tri_dot(tri_b, x):
    hi, mid, lo = _split3(x)
    return (_dot(tri_b, lo) + _dot(tri_b, mid)) + _dot(tri_b, hi)


FFN_TS, FFN_TF = 512, 512


def _ffn_specs():
    wg = pl.BlockSpec((D, FFN_TF), lambda i, j: (0, j))
    wu = pl.BlockSpec((D, FFN_TF), lambda i, j: (0, FF // FFN_TF + j))
    wo = pl.BlockSpec((FFN_TF, D), lambda i, j: (j, 0))
    row = pl.BlockSpec((FFN_TS, D), lambda i, j: (i, 0))
    vec = pl.BlockSpec((1, D), lambda i, j: (0, 0))
    gu = pl.BlockSpec((2, FFN_TS, FFN_TF), lambda i, j: (0, i, j))
    return wg, wu, wo, row, vec, gu


def _ordered_after(body, n_in, after):
    def wrapped(*refs):
        return body(*refs[:n_in], *refs[n_in + len(after):])
    return wrapped, [ANY] * len(after)


def _ffn_fwd(h, g, w_in8, w_out, name, after=()):
    nj = FF // FFN_TF
    wg, wu, wo, row, vec, gu = _ffn_specs()

    def body(h_ref, g_ref, wg_ref, wu_ref, wo_ref, ho_ref, n_ref, gu_ref, acc_ref):
        j = pl.program_id(1)

        @pl.when(j == 0)
        def _():
            def norm(rows, c):
                n_ref[rows, :] = _rms(h_ref[rows, :], g_ref[...]).astype(BF16)
                return c
            _row_loop(FFN_TS, norm)
            acc_ref[...] = jnp.zeros_like(acc_ref)

        n = n_ref[...]
        gate = _dot(n, wg_ref[...])
        up = _dot(n, wu_ref[...])
        gu_ref[0] = gate.astype(BF16)
        gu_ref[1] = up.astype(BF16)
        a = (gate * _sigmoid(gate)) * up
        acc_ref[...] += _dot(a.astype(BF16), wo_ref[...])

        @pl.when(j == nj - 1)
        def _():
            def residual(rows, c):
                ho_ref[rows, :] = h_ref[rows, :] + 0.5 * acc_ref[rows, :]
                return c
            _row_loop(FFN_TS, residual)

    body, extra = _ordered_after(body, 5, after)
    return pl.pallas_call(
        body, grid=(S // FFN_TS, nj), name=name,
        in_specs=[row, vec, wg, wu, wo] + extra, out_specs=[row, row, gu],
        out_shape=[jax.ShapeDtypeStruct((S, D), F32), jax.ShapeDtypeStruct((S, D), BF16),
                   jax.ShapeDtypeStruct((2, S, FF), BF16)],
        scratch_shapes=[pltpu.VMEM((FFN_TS, D), F32)],
        compiler_params=_params(56, ("arbitrary", "arbitrary")),
    )(h, g, w_in8, w_in8, w_out, *after)


def _ffn_bwd_x(dhp, h, g, gu_arr, w_in8, w_out, name, after=()):
    ni, nj = S // FFN_TS, FF // FFN_TF
    wg, wu, wo, row, vec, gu = _ffn_specs()
    act = pl.BlockSpec((FFN_TS, FFN_TF), lambda i, j: (i, j))

    def body(dhp_ref, h_ref, g_ref, gu_ref, wg_ref, wu_ref, wo_ref,
             dgu_ref, a_ref, df_ref, dh_ref, dhb_ref, dg_ref, acc_ref):
        i, j = pl.program_id(0), pl.program_id(1)

        @pl.when(j == 0)
        def _():
            def half(rows, c):
                df_ref[rows, :] = (0.5 * dhp_ref[rows, :]).astype(BF16)
                return c
            _row_loop(FFN_TS, half)
            acc_ref[...] = jnp.zeros_like(acc_ref)

        gate = gu_ref[0].astype(F32)
        up = gu_ref[1].astype(F32)
        da = _dot_nt(df_ref[...], wo_ref[...])
        sg = _sigmoid(gate)
        silu = gate * sg
        dgate = (da * up * (sg * (1.0 + gate * (1.0 - sg)))).astype(BF16)
        dup = (da * silu).astype(BF16)
        a_ref[...] = (silu * up).astype(BF16)
        dgu_ref[0] = dgate
        dgu_ref[1] = dup
        acc_ref[...] += _dot_nt(dgate, wg_ref[...]) + _dot_nt(dup, wu_ref[...])

        @pl.when(j == nj - 1)
        def _():
            def norm_bwd(rows, dg):
                dx, dg_rows = _rms_bwd(acc_ref[rows, :], h_ref[rows, :], g_ref[...])
                dh = dhp_ref[rows, :] + dx
                dh_ref[rows, :] = dh
                dhb_ref[rows, :] = dh.astype(BF16)
                return dg + dg_rows
            dg = _row_loop(FFN_TS, norm_bwd, jnp.zeros((1, D), F32))

            @pl.when(i == 0)
            def _():
                dg_ref[...] = dg

            @pl.when(i > 0)
            def _():
                dg_ref[...] += dg

    body, extra = _ordered_after(body, 7, after)
    return pl.pallas_call(
        body, grid=(ni, nj), name=name,
        in_specs=[row, row, vec, gu, wg, wu, wo] + extra,
        out_specs=[gu, act, row, row, row, vec],
        out_shape=[jax.ShapeDtypeStruct((2, S, FF), BF16), jax.ShapeDtypeStruct((S, FF), BF16),
                   jax.ShapeDtypeStruct((S, D), BF16), jax.ShapeDtypeStruct((S, D), F32),
                   jax.ShapeDtypeStruct((S, D), BF16), jax.ShapeDtypeStruct((1, D), F32)],
        scratch_shapes=[pltpu.VMEM((FFN_TS, D), F32)],
        compiler_params=_params(58, ("arbitrary", "arbitrary")),
    )(dhp, h, g, gu_arr, w_in8, w_in8, w_out, *after)


def _ffn_bwd_gu(df, gu_arr, w_out, name, after=()):
    ni = S // FFN_TS
    gu = pl.BlockSpec((2, FFN_TS, FFN_TF), lambda j, i: (0, i, j))
    wo = pl.BlockSpec((FFN_TF, D), lambda j, i: (j, 0))
    df_all = pl.BlockSpec((S, D), lambda j, i: (0, 0), pipeline_mode=pl.Buffered(1))

    def body(df_ref, gu_ref, wo_ref, dgu_ref, dwo_ref, acc_ref):
        i = pl.program_id(1)
        df_rows = df_ref[pl.ds(pl.multiple_of(i * FFN_TS, FFN_TS), FFN_TS), :]
        gate = gu_ref[0].astype(F32)
        up = gu_ref[1].astype(F32)
        da = _dot_nt(df_rows, wo_ref[...])
        sg = _sigmoid(gate)
        silu = gate * sg
        dgu_ref[0] = (da * up * (sg * (1.0 + gate * (1.0 - sg)))).astype(BF16)
        dgu_ref[1] = (da * silu).astype(BF16)
        part = _dot_tn((silu * up).astype(BF16), df_rows)

        @pl.when(i == 0)
        def _():
            acc_ref[...] = part

        @pl.when(i > 0)
        def _():
            acc_ref[...] += part

        @pl.when(i == ni - 1)
        def _():
            dwo_ref[...] = acc_ref[...].astype(BF16)

    body, extra = _ordered_after(body, 3, after)
    return pl.pallas_call(
        body, grid=(FF // FFN_TF, ni), name=name,
        in_specs=[df_all, gu, wo] + extra, out_specs=[gu, wo],
        out_shape=[jax.ShapeDtypeStruct((2, S, FF), BF16), jax.ShapeDtypeStruct((FF, D), BF16)],
        scratch_shapes=[pltpu.VMEM((FFN_TF, D), F32)],
        compiler_params=_params(40, ("arbitrary", "arbitrary")),
    )(df, gu_arr, w_out, *after)


def _ffn_bwd_dx(dgu, dhp, h, g, w_in, first_tile, ntiles, name, after=(), into=None):
    nj = FF // FFN_TF
    wg, wu, _, _, vec, _ = _ffn_specs()
    row_in = pl.BlockSpec((FFN_TS, D), lambda i, j: (first_tile + i, 0))
    dgu_spec = pl.BlockSpec((2, FFN_TS, FFN_TF), lambda i, j: (0, first_tile + i, j))
    after = tuple(after) + (() if into is None else (into,))

    def body(dgu_ref, dhp_ref, h_ref, g_ref, wg_ref, wu_ref, dh_ref, dg_ref, acc_ref):
        i, j = pl.program_id(0), pl.program_id(1)

        @pl.when(j == 0)
        def _():
            acc_ref[...] = jnp.zeros_like(acc_ref)

        acc_ref[...] += _dot_nt(dgu_ref[0], wg_ref[...]) + _dot_nt(dgu_ref[1], wu_ref[...])

        @pl.when(j == nj - 1)
        def _():
            def norm_bwd(rows, dg):
                dx, dg_rows = _rms_bwd(acc_ref[rows, :], h_ref[rows, :], g_ref[...])
                dh_ref[rows, :] = dhp_ref[rows, :] + dx
                return dg + dg_rows
            dg = _row_loop(FFN_TS, norm_bwd, jnp.zeros((1, D), F32))

            @pl.when(i == 0)
            def _():
                dg_ref[...] = dg

            @pl.when(i > 0)
            def _():
                dg_ref[...] += dg

    body, extra = _ordered_after(body, 6, after)
    return pl.pallas_call(
        body, grid=(ntiles, nj), name=name,
        in_specs=[dgu_spec, row_in, row_in, vec, wg, wu] + extra, out_specs=[row_in, vec],
        out_shape=[jax.ShapeDtypeStruct((S, D), F32), jax.ShapeDtypeStruct((1, D), F32)],
        input_output_aliases={} if into is None else {6 + len(after) - 1: 0},
        scratch_shapes=[pltpu.VMEM((FFN_TS, D), F32)],
        compiler_params=_params(48, ("arbitrary", "arbitrary")),
    )(dgu, dhp, h, g, w_in, w_in, *after)


def _tn_matmul(a, b, a_spec, b_spec, out_shape, out_spec, grid, name, vmem_mb, after=()):
    def body(a_ref, b_ref, o_ref):
        o_ref[...] = _dot_tn(a_ref[...], b_ref[...]).astype(o_ref.dtype)

    body, extra = _ordered_after(body, 2, after)
    return pl.pallas_call(body, grid=grid, name=name, in_specs=[a_spec, b_spec] + extra, out_specs=out_spec,
                          out_shape=out_shape, compiler_params=_params(vmem_mb))(a, b, *after)


def _resident(shape):
    return pl.BlockSpec(shape, lambda i: (0,) * len(shape), pipeline_mode=pl.Buffered(1))


def _norm_matmul(h, g, w, w_tail, name, after=(), ts=256):
    main = w.shape[0] // 128 * 128
    n_out = main + w_tail.shape[0]

    def body(h_ref, g_ref, w_ref, wt_ref, u_ref, n_ref):
        def norm(rows, c):
            n_ref[rows, :] = _rms(h_ref[rows, :], g_ref[...]).astype(BF16)
            return c
        _row_loop(ts, norm)
        n = n_ref[...]
        u_ref[:, :main] = _dot_nt(n, w_ref[:main, :])
        u_ref[:, main:] = _dot_nt(n, wt_ref[...])

    body, extra = _ordered_after(body, 4, after)
    return pl.pallas_call(
        body, grid=(S // ts,), name=name,
        in_specs=[pl.BlockSpec((ts, D), lambda i: (i, 0)), pl.BlockSpec((1, D), lambda i: (0, 0)), _resident(w.shape),
                  _resident(w_tail.shape)] + extra,
        out_specs=[pl.BlockSpec((ts, n_out), lambda i: (i, 0)), pl.BlockSpec((ts, D), lambda i: (i, 0))],
        out_shape=[jax.ShapeDtypeStruct((S, n_out), F32), jax.ShapeDtypeStruct((S, D), BF16)],
        compiler_params=_params(48, ("arbitrary",)),
    )(h, g, w, w_tail, *after)


def _matmul_residual(a, w, res, name, after=(), ts=512):
    k, n_out = w.shape

    def body(a_ref, w_ref, r_ref, o_ref):
        o_ref[...] = r_ref[...] + _dot(a_ref[...], w_ref[...])

    body, extra = _ordered_after(body, 3, after)
    return pl.pallas_call(
        body, grid=(S // ts,), name=name,
        in_specs=[pl.BlockSpec((ts, k), lambda i: (i, 0)), _resident(w.shape), pl.BlockSpec((ts, n_out), lambda i: (i, 0))] + extra,
        out_specs=pl.BlockSpec((ts, n_out), lambda i: (i, 0)),
        out_shape=jax.ShapeDtypeStruct((S, n_out), F32),
        compiler_params=_params(40),
    )(a, w, res, *after)


def _nt_matmul(a, w, name, after=(), ts=512):
    n_out, k = w.shape

    def body(a_ref, w_ref, o_ref):
        o_ref[...] = _dot_nt(a_ref[...], w_ref[...])

    body, extra = _ordered_after(body, 2, after)
    return pl.pallas_call(
        body, grid=(S // ts,), name=name,
        in_specs=[pl.BlockSpec((ts, k), lambda i: (i, 0)), _resident(w.shape)] + extra,
        out_specs=pl.BlockSpec((ts, n_out), lambda i: (i, 0)),
        out_shape=jax.ShapeDtypeStruct((S, n_out), F32),
        compiler_params=_params(40),
    )(a, w, *after)


def _matmul_normbwd(du, w, w_tail, h, g, dres, name, after=(), ts=256):
    main = w.shape[0] // 128 * 128
    n_in = main + w_tail.shape[0]
    row = pl.BlockSpec((ts, D), lambda i: (i, 0))
    vec = pl.BlockSpec((1, D), lambda i: (0, 0))

    def body(du_ref, w_ref, wt_ref, h_ref, g_ref, dres_ref, dh_ref, dg_ref, half_ref, acc_ref):
        i = pl.program_id(0)
        acc_ref[...] = _dot(du_ref[:, :main], w_ref[:main, :]) + _dot(du_ref[:, main:], wt_ref[...])

        def norm_bwd(rows, dg):
            dx, dg_rows = _rms_bwd(acc_ref[rows, :], h_ref[rows, :], g_ref[...])
            dh = dres_ref[rows, :] + dx
            dh_ref[rows, :] = dh
            half_ref[rows, :] = (0.5 * dh).astype(BF16)
            return dg + dg_rows
        dg = _row_loop(ts, norm_bwd, jnp.zeros((1, D), F32))

        @pl.when(i == 0)
        def _():
            dg_ref[...] = dg

        @pl.when(i > 0)
        def _():
            dg_ref[...] += dg

    body, extra = _ordered_after(body, 6, after)
    return pl.pallas_call(
        body, grid=(S // ts,), name=name,
        in_specs=[pl.BlockSpec((ts, n_in), lambda i: (i, 0)), _resident(w.shape), _resident(w_tail.shape), row, vec, row] + extra,
        out_specs=[row, vec, row],
        out_shape=[jax.ShapeDtypeStruct((S, D), F32), jax.ShapeDtypeStruct((1, D), F32), jax.ShapeDtypeStruct((S, D), BF16)],
        scratch_shapes=[pltpu.VMEM((ts, D), F32)],
        compiler_params=_params(52, ("arbitrary",)),
    )(du, w, w_tail, h, g, dres, *after)


def _loss_head(h, g, tgt, ts=256):
    row = pl.BlockSpec((ts, D), lambda i: (i, 0))
    vec = pl.BlockSpec((1, D), lambda i: (0, 0))

    def body(h_ref, g_ref, t_ref, dh_ref, dg_ref, loss_ref):
        i = pl.program_id(0)

        def rows_fn(rows, carry):
            dg, part = carry
            x = h_ref[rows, :]
            gv = g_ref[...]
            err = _rms(x, gv) - t_ref[rows, :]
            part = part + 0.5 * jnp.sum(jnp.mean(err * err, axis=-1, keepdims=True), axis=0, keepdims=True)
            dx, dg_rows = _rms_bwd(err * (1.0 / D), x, gv)
            dh_ref[rows, :] = dx
            return dg + dg_rows, part
        dg, part = _row_loop(ts, rows_fn, (jnp.zeros((1, D), F32), jnp.zeros((1, 1), F32)))

        @pl.when(i == 0)
        def _():
            dg_ref[...] = dg
            loss_ref[...] = jnp.broadcast_to(part, loss_ref.shape)

        @pl.when(i > 0)
        def _():
            dg_ref[...] += dg
            loss_ref[...] += jnp.broadcast_to(part, loss_ref.shape)

    return pl.pallas_call(
        body, grid=(S // ts,), name="loss_head",
        in_specs=[row, vec, row], out_specs=[row, vec, pl.BlockSpec((1, 128), lambda i: (0, 0))],
        out_shape=[jax.ShapeDtypeStruct((S, D), F32), jax.ShapeDtypeStruct((1, D), F32),
                   jax.ShapeDtypeStruct((1, 128), F32)],
        compiler_params=_params(40, ("arbitrary",)),
    )(h, g, tgt)


def _pool_specs():
    blk = pl.BlockSpec((S, PG), lambda gi: (0, gi))
    wp = pl.BlockSpec((None, PG, PG), lambda gi: (gi, 0, 0))
    sc = pl.BlockSpec((1, PG), lambda gi: (0, gi))
    return blk, wp, sc


def _pool_fwd(u, wp_b, scale):
    blk, wp, sc = _pool_specs()

    def body(u_ref, wp_ref, sc_ref, y_ref, pooled_ref):
        win = 2 << pl.program_id(0)
        row = lax.broadcasted_iota(jnp.int32, (S, PG), 0)
        x = u_ref[...]
        s = x
        for k in (1, 2, 4, 8):
            s = s + jnp.where((row >= k) & (k < win), pltpu.roll(s, k, 0), 0.0)
        cnt = jnp.minimum(row + 1, win).astype(F32)
        pooled = (s / cnt - x).astype(BF16)
        pooled_ref[...] = pooled
        y_ref[...] = (_dot(pooled, wp_ref[...]) * sc_ref[...]).astype(BF16)

    return pl.pallas_call(
        body, grid=(len(POOL_WINDOWS),), name="pool_fwd", in_specs=[blk, wp, sc], out_specs=[blk, blk],
        out_shape=[jax.ShapeDtypeStruct((S, D_POOL), BF16), jax.ShapeDtypeStruct((S, D_POOL), BF16)],
        compiler_params=_params(40),
    )(u, wp_b, scale)


def _pool_bwd(dy, pooled, wp_b, scale):
    blk, wp, sc = _pool_specs()

    def body(dy_ref, p_ref, wp_ref, sc_ref, du_ref, dwp_ref, dsc_ref):
        win = 2 << pl.program_id(0)
        row = lax.broadcasted_iota(jnp.int32, (S, PG), 0)
        dyv = dy_ref[...]
        pooled = p_ref[...]
        w = wp_ref[...]
        dsc_ref[...] = jnp.sum(dyv * _dot(pooled, w), axis=0, keepdims=True)
        dz = (dyv * sc_ref[...]).astype(BF16)
        dwp_ref[...] = _dot_tn(pooled, dz)
        dpooled = _dot_nt(dz, w)
        cnt = jnp.minimum(row + 1, win).astype(F32)
        fs = dpooled / cnt
        for k in (1, 2, 4, 8):
            fs = fs + jnp.where((row < S - k) & (k < win), pltpu.roll(fs, S - k, 0), 0.0)
        du_ref[...] = (fs - dpooled).astype(BF16)

    return pl.pallas_call(
        body, grid=(len(POOL_WINDOWS),), name="pool_bwd", in_specs=[blk, blk, wp, sc], out_specs=[blk, wp, sc],
        out_shape=[jax.ShapeDtypeStruct((S, D_POOL), BF16), jax.ShapeDtypeStruct((len(POOL_WINDOWS), PG, PG), F32),
                   jax.ShapeDtypeStruct((1, D_POOL), F32)],
        compiler_params=_params(40),
    )(dy, pooled, wp_b, scale)


def _gla_in_specs(chunk_of):
    def at(width, col):
        return pl.BlockSpec((CH, width), lambda n: (chunk_of(n), col))
    return [at(H * DK, O_Q // (H * DK)), at(H * DK, O_K // (H * DK)), at(H * DV, O_V // (H * DV)),
            at(H * DV, O_G // (H * DV)), at(RANK_PAD, O_R // RANK_PAD)]


def _gla_decay_terms(lr_ref, wa_ref, ba_ref, q_ref, k_ref):
    row = lax.broadcasted_iota(jnp.int32, (CH, CH), 0)
    col = lax.broadcasted_iota(jnp.int32, (CH, CH), 1)
    tril = row >= col
    z = _dot(lr_ref[...].astype(BF16), wa_ref[...]) + ba_ref[...]
    la = _log_sigmoid(z) / GATE_NORM
    b = _tri_dot(jnp.where(tril, 1.0, 0.0).astype(BF16), la)
    bl = jnp.sum(la, axis=0, keepdims=True)
    e_b, e_nb, e_tb = jnp.exp(b), jnp.exp(-b), jnp.exp(bl - b)
    kk = k_ref[...]
    q_dec = (q_ref[...] * QK_SCALE) * e_b
    return tril, z, e_b, e_nb, e_tb, jnp.exp(bl), q_dec, kk * e_nb, kk * e_tb


def _gla_fwd(u, wa_b, ba, gn):
    wide = pl.BlockSpec((CH, H * DV), lambda n: (n, 0))

    def body(q_ref, k_ref, v_ref, g_ref, lr_ref, wa_ref, ba_ref, gn_ref, y_ref, o_ref, st_ref, state):
        @pl.when(pl.program_id(0) == 0)
        def _():
            state[...] = jnp.zeros_like(state)

        tril, _, _, _, _, dec, q_dec, k_inv, k_tail = _gla_decay_terms(lr_ref, wa_ref, ba_ref, q_ref, k_ref)
        for hd in range(H):
            ks, vs = slice(hd * DK, (hd + 1) * DK), slice(hd * DV, (hd + 1) * DV)
            qb, kib, ktb = q_dec[:, ks].astype(BF16), k_inv[:, ks].astype(BF16), k_tail[:, ks].astype(BF16)
            vb = v_ref[:, vs].astype(BF16)
            p = jnp.where(tril, _dot_nt(qb, kib), 0.0)
            st = state[hd]
            o = _dot(p.astype(BF16), vb) + _dot_nt(qb, st.astype(BF16))
            st_ref[hd] = st
            state[hd] = st * dec[:, ks] + _dot_tn(vb, ktb)
            o_ref[:, vs] = o
            on = _rms(o, gn_ref[...])
            gg = g_ref[:, vs]
            y_ref[:, vs] = (on * (gg * _sigmoid(gg))).astype(BF16)

    return pl.pallas_call(
        body, grid=(NCH,), name="gla_fwd",
        in_specs=_gla_in_specs(lambda n: n) + [pl.BlockSpec((RANK_PAD, H * DK), lambda n: (0, 0)),
                                               pl.BlockSpec((1, H * DK), lambda n: (0, 0)),
                                               pl.BlockSpec((1, DV), lambda n: (0, 0))],
        out_specs=[wide, wide, pl.BlockSpec((None, H, DV, DK), lambda n: (n, 0, 0, 0))],
        out_shape=[jax.ShapeDtypeStruct((S, H * DV), BF16), jax.ShapeDtypeStruct((S, H * DV), F32),
                   jax.ShapeDtypeStruct((NCH, H, DV, DK), F32)],
        scratch_shapes=[pltpu.VMEM((H, DV, DK), F32)],
        compiler_params=_params(32, ("arbitrary",)),
    )(u, u, u, u, u, wa_b, ba, gn)


GLA_DU = 2 * H * DK + 2 * H * DV + RANK_PAD


def _gla_bwd(u, o_arr, states, dy, wa_b, ba, gn, after=()):
    rev = lambda n: NCH - 1 - n
    wide = pl.BlockSpec((CH, H * DV), lambda n: (rev(n), 0))

    def body(q_ref, k_ref, v_ref, g_ref, lr_ref, o_ref, st_ref, dy_ref, wa_ref, ba_ref, gn_ref,
             du_ref, dwa_ref, dba_ref, dgn_ref, gstate, db_scr, dbl_scr):
        @pl.when(pl.program_id(0) == 0)
        def _():
            gstate[...] = jnp.zeros_like(gstate)
            dwa_ref[...] = jnp.zeros_like(dwa_ref)
            dba_ref[...] = jnp.zeros_like(dba_ref)
            dgn_ref[...] = jnp.zeros_like(dgn_ref)

        tril, z, e_b, e_nb, e_tb, dec, q_dec, k_inv, k_tail = _gla_decay_terms(lr_ref, wa_ref, ba_ref, q_ref, k_ref)
        gnv = gn_ref[...]
        dgn = jnp.zeros((1, DV), F32)
        for hd in range(H):
            ks, vs = slice(hd * DK, (hd + 1) * DK), slice(hd * DV, (hd + 1) * DV)
            qh, kih, kth = q_dec[:, ks], k_inv[:, ks], k_tail[:, ks]
            qb, kib, ktb = qh.astype(BF16), kih.astype(BF16), kth.astype(BF16)
            vb = v_ref[:, vs].astype(BF16)
            o = o_ref[:, vs]
            gg = g_ref[:, vs]
            dyh = dy_ref[:, vs]
            r = lax.rsqrt(jnp.mean(o * o, axis=-1, keepdims=True) + EPS)
            xh = o * r
            sg = _sigmoid(gg)
            dgate = dyh * (xh * gnv) * (sg * (1.0 + gg * (1.0 - sg)))
            don = dyh * (gg * sg)
            dgn = dgn + jnp.sum(don * xh, axis=0, keepdims=True)
            dxh = don * gnv
            d_o = (r * (dxh - xh * jnp.mean(dxh * xh, axis=-1, keepdims=True))).astype(BF16)
            pb = jnp.where(tril, _dot_nt(qb, kib), 0.0).astype(BF16)
            dpb = jnp.where(tril, _dot_nt(d_o, vb), 0.0).astype(BF16)
            gt = gstate[hd]
            gtb = gt.astype(BF16)
            st = st_ref[hd]
            dv = _dot_tn(pb, d_o) + _dot_nt(ktb, gtb)
            dq_dec = _dot(dpb, kib) + _dot(d_o, st.astype(BF16))
            dk_inv = _dot_tn(dpb, qb)
            dk_tail = _dot(vb, gtb)
            ddec = jnp.sum(gt * st, axis=0, keepdims=True)
            gstate[hd] = _dot_tn(d_o, qb) + dec[:, ks] * gt
            du_ref[:, ks] = (dq_dec * QK_SCALE * e_b[:, ks]).astype(BF16)
            du_ref[:, H * DK + hd * DK:H * DK + (hd + 1) * DK] = (dk_inv * e_nb[:, ks] + dk_tail * e_tb[:, ks]).astype(BF16)
            du_ref[:, 2 * H * DK + hd * DV:2 * H * DK + (hd + 1) * DV] = dv.astype(BF16)
            du_ref[:, 2 * H * DK + H * DV + hd * DV:2 * H * DK + H * DV + (hd + 1) * DV] = dgate.astype(BF16)
            db_scr[:, ks] = dq_dec * qh - dk_inv * kih - dk_tail * kth
            dbl_scr[:, ks] = jnp.sum(dk_tail * kth, axis=0, keepdims=True) + ddec * dec[:, ks]
        dgn_ref[...] += dgn
        row = lax.broadcasted_iota(jnp.int32, (CH, CH), 0)
        col = lax.broadcasted_iota(jnp.int32, (CH, CH), 1)
        dla = _tri_dot(jnp.where(row <= col, 1.0, 0.0).astype(BF16), db_scr[...]) + dbl_scr[...]
        dz = dla * (1.0 / GATE_NORM) * _sigmoid(-z)
        dzb = dz.astype(BF16)
        du_ref[:, GLA_DU - RANK_PAD:] = _dot_nt(dzb, wa_ref[...]).astype(BF16)
        dwa_ref[...] += _dot_tn(lr_ref[...].astype(BF16), dzb)
        dba_ref[...] += jnp.sum(dz, axis=0, keepdims=True)

    full = lambda shape: pl.BlockSpec(shape, lambda n: (0,) * len(shape))
    body, extra = _ordered_after(body, 11, after)
    return pl.pallas_call(
        body, grid=(NCH,), name="gla_bwd",
        in_specs=_gla_in_specs(rev) + [wide, pl.BlockSpec((None, H, DV, DK), lambda n: (rev(n), 0, 0, 0)),
                                       pl.BlockSpec((CH, H * DV), lambda n: (rev(n), 1)),
                                       full((RANK_PAD, H * DK)), full((1, H * DK)), full((1, DV))] + extra,
        out_specs=[pl.BlockSpec((CH, GLA_DU), lambda n: (rev(n), 0)), full((RANK_PAD, H * DK)), full((1, H * DK)),
                   full((1, DV))],
        out_shape=[jax.ShapeDtypeStruct((S, GLA_DU), BF16), jax.ShapeDtypeStruct((RANK_PAD, H * DK), F32),
                   jax.ShapeDtypeStruct((1, H * DK), F32), jax.ShapeDtypeStruct((1, DV), F32)],
        scratch_shapes=[pltpu.VMEM((H, DV, DK), F32), pltpu.VMEM((CH, H * DK), F32), pltpu.VMEM((1, H * DK), F32)],
        compiler_params=_params(32, ("arbitrary",)),
    )(u, u, u, u, u, o_arr, states, dy, wa_b, ba, gn, *after)


def _ffn_dw_in(n, dgu, tag, after=(), first_tile=0, ntiles=D // 512):
    return _tn_matmul(n, dgu, pl.BlockSpec((S, 512), lambda s, m: (0, first_tile + m)),
                      pl.BlockSpec((None, S, WIN_SHARD), lambda s, m: (s // (NDEV // 2), 0, s % (NDEV // 2))),
                      jax.ShapeDtypeStruct((512 * ntiles, 2 * FF), BF16), pl.BlockSpec((512, WIN_SHARD), lambda s, m: (m, s)),
                      (NDEV, ntiles), tag + "_dw_in", 32, after)


def _ffn_dw_out(act, df, tag, after=()):
    return _tn_matmul(act, df, pl.BlockSpec((S, 512), lambda m: (0, m)), pl.BlockSpec((S, D), lambda m: (0, 0)),
                      jax.ShapeDtypeStruct((FF, D), BF16), pl.BlockSpec((512, D), lambda m: (m, 0)),
                      (FF // 512,), tag + "_dw_out", 40, after)


def _fwd_ffn1(x, w, after=()):
    return _ffn_fwd(x, w["ffn1_norm"], w["ffn1_w_in"], w["ffn1_w_out"], "ffn1_fwd", after)


def _fwd_mixer(h1, w, after=(), at=lambda point, value: ()):
    u, n2 = _norm_matmul(h1, w["mix_norm"], w["w_in_mix"], w["w_in_mix_tail"], "mix_in", after)
    y_pool, pooled = _pool_fwd(u, w["w_pool"], w["pool_scale"])
    y_gla, o_gla, states = _gla_fwd(u, w["w_alpha"], w["b_alpha"], w["gla_norm"])
    y = jnp.concatenate([y_pool, y_gla], axis=1)
    h2 = _matmul_residual(y, w["w_out_mix"], h1, "mix_out", at("gla_fwd", y))
    return h2, dict(u=u, n2=n2, pooled=pooled, o_gla=o_gla, states=states, y=y)


def _fwd_ffn2_loss(h2, tgt, w, after=()):
    h3, n3, gu3 = _ffn_fwd(h2, w["ffn2_norm"], w["ffn2_w_in"], w["ffn2_w_out"], "ffn2_fwd", after)
    dh3, d_final, loss = _loss_head(h3, w["final_norm"], tgt)
    return dh3, d_final, loss, n3, gu3


def _backward(x, h1, h2, n1, gu1, sv, n3, gu3, dh3, w, at=lambda point, value: ()):
    g = {}
    dgu3, act3, df3, dh2, dh2b, g["ffn2_norm"] = _ffn_bwd_x(dh3, h2, w["ffn2_norm"], gu3, w["ffn2_w_in"], w["ffn2_w_out"], "ffn2_bwd")
    g["ffn2_w_out"] = _ffn_dw_out(act3, df3, "ffn2")
    g["ffn2_w_in"] = _ffn_dw_in(n3, dgu3, "ffn2", at("ffn2_w_out", g["ffn2_w_out"]))
    dy = _nt_matmul(dh2b, w["w_out_mix"], "mix_out_bwd", at("ffn2_w_in", g["ffn2_w_in"]))
    g["w_out_mix"] = _tn_matmul(sv["y"], dh2b, pl.BlockSpec((S, 512), lambda m: (0, m)), pl.BlockSpec((S, D), lambda m: (0, 0)),
                                jax.ShapeDtypeStruct((D, D), BF16), pl.BlockSpec((512, D), lambda m: (m, 0)), (D // 512,),
                                "mix_out_dw", 40)
    du_pool, g["w_pool"], g["pool_scale"] = _pool_bwd(dy, sv["pooled"], w["w_pool"], w["pool_scale"])
    du_gla, g["w_alpha"], g["b_alpha"], g["gla_norm"] = _gla_bwd(sv["u"], sv["o_gla"], sv["states"], dy, w["w_alpha"], w["b_alpha"],
                                                                 w["gla_norm"], at("pool_bwd", du_pool))
    du = jnp.concatenate([du_pool, du_gla], axis=1)
    g["w_in_mix"] = _tn_matmul(du, sv["n2"], pl.BlockSpec((S, 1408), lambda j, m: (0, j)), pl.BlockSpec((S, 512), lambda j, m: (0, m)),
                               jax.ShapeDtypeStruct((D_IN, D), BF16), pl.BlockSpec((1408, 512), lambda j, m: (j, m)),
                               (D_IN_PAD // 1408, D // 512), "mix_in_dw", 32)
    dh1, g["mix_norm"], df1 = _matmul_normbwd(du, w["w_in_mix"], w["w_in_mix_tail"], h1, w["mix_norm"], dh2, "mix_in_bwd",
                                              at("mixer_weights", g))
    dgu1, g["ffn1_w_out"] = _ffn_bwd_gu(df1, gu1, w["ffn1_w_out"], "ffn1_bwd_gu", at("dh1", dh1))
    half = S // FFN_TS // 2
    dx, dn_a = _ffn_bwd_dx(dgu1, dh1, x, w["ffn1_norm"], w["ffn1_w_in"], 0, half, "ffn1_bwd_dx_a",
                           at("ffn1_w_out", g["ffn1_w_out"]))
    g["ffn1_w_in_top"] = _ffn_dw_in(n1, dgu1, "ffn1_top", at("ffn1_dx_a", dx), 0, D // 1024)
    g["ffn1_w_in_bottom"] = _ffn_dw_in(n1, dgu1, "ffn1_bottom", at("ffn1_w_in_top", g["ffn1_w_in_top"]), D // 1024, D // 1024)
    dx, dn_b = _ffn_bwd_dx(dgu1, dh1, x, w["ffn1_norm"], w["ffn1_w_in"], half, half, "ffn1_bwd_dx_b",
                           at("ffn1_w_in_bottom", g["ffn1_w_in_bottom"]), into=dx)
    g["ffn1_norm"] = dn_a + dn_b
    return dx, g


def _local_step(x, tgt, w):
    h1, n1, gu1 = _fwd_ffn1(x, w)
    h2, sv = _fwd_mixer(h1, w)
    dh3, d_final, loss, n3, gu3 = _fwd_ffn2_loss(h2, tgt, w)
    dx, g = _backward(x, h1, h2, n1, gu1, sv, n3, gu3, dh3, w)
    return loss, dx, dict(final_norm=d_final, **g)


def _coords(p):
    return (p // 4, (p // 2) % 2, p % 2)


NCHIP = 4


def _place():
    return lax.axis_index("x"), lax.axis_index("y"), lax.axis_index("c")


def _rel_chip(x, y, rel):
    return ((1 - x) if rel & 1 else x, (1 - y) if rel & 2 else y)


def _dev_index(x, y, c):
    return 4 * x + 2 * y + c


def _cols(ref, p, width):
    return ref.at[:, pl.ds(pl.multiple_of(p * width, 128), width)]


def _sems(na, n):
    return [pltpu.SemaphoreType.DMA((na, n)), pltpu.SemaphoreType.DMA((na, n)), pltpu.SemaphoreType.DMA((na,))]


def _gather(items, name):
    arrays = [a for a, _ in items]
    kinds = [k for _, k in items]
    na = len(arrays)
    out_shape = [jax.ShapeDtypeStruct((NDEV,) + a.shape if k == "bcast" else (a.shape[0], NDEV * a.shape[1]), a.dtype)
                 for a, k in items]

    def body(*refs):
        ins, outs = refs[:na], refs[na:2 * na]
        send_sems, recv_sems, local_sems = refs[2 * na:]
        x, y, c = _place()
        sibling = (x, y, 1 - c)
        here, over_x, over_y, across = (x, y), (1 - x, y), (x, 1 - y), (1 - x, 1 - y)

        def half(ref, h):
            rows = ref.shape[0] // 2
            return ref.at[pl.ds(h * rows, rows), :]

        def slab(a, chip, core, h=None):
            ref = _slab(outs[a], kinds[a], _dev_index(*chip, core), ins[a].shape[1])
            return ref if h is None else half(ref, h)

        def copy(a, k, src, dst, to):
            return pltpu.make_async_remote_copy(src, dst, send_sems.at[a, k], recv_sems.at[a, k], device_id=to, device_id_type=MESH)

        sent = []

        def send(a, k, src, dst, to):
            sent.append(copy(a, k, src, dst, to))
            sent[-1].start()

        def arrived(a, k, chip, core, h=None):
            ref = slab(a, chip, core, h)
            copy(a, k, ref, ref, sibling).wait_recv()
            return ref

        local = [pltpu.make_async_copy(ins[a], slab(a, here, c), local_sems.at[a]) for a in range(na)]
        for cp in local:
            cp.start()
        for k, h, chip in ((1, 0, over_x), (4, 1, over_y), (2, 1, over_x), (5, 0, over_y)):
            for a in range(na):
                send(a, k, half(ins[a], h), slab(a, here, c, h), (*chip, c))
        for a in range(na):
            send(a, 0, ins[a], slab(a, here, c), sibling)
        for k, chip, h, onward, to, down in ((1, over_x, 0, 6, over_y, 7), (4, over_y, 1, 3, over_x, 10),
                                             (2, over_x, 1, None, None, 8), (5, over_y, 0, None, None, 9),
                                             (3, across, 1, None, None, 12), (6, across, 0, None, None, 11)):
            for a in range(na):
                ref = arrived(a, k, chip, c, h)
                if onward is not None:
                    send(a, onward, ref, ref, (*to, c))
                send(a, down, ref, ref, sibling)
        for a in range(na):
            arrived(a, 0, here, 1 - c)
        for k, chip, h in ((7, over_x, 0), (10, over_y, 1), (8, over_x, 1), (9, over_y, 0), (12, across, 1), (11, across, 0)):
            for a in range(na):
                arrived(a, k, chip, 1 - c, h)
        for cp in sent:
            cp.wait_send()
        for cp in local:
            cp.wait()

    return pl.pallas_call(body, name=name, in_specs=[ANY] * na, out_specs=[ANY] * na, out_shape=out_shape,
                          scratch_shapes=_sems(na, 13))(*arrays)


def _pair_add(own, kind, got, table, tr, name):
    _, rows, cols = got.shape
    if kind == "scatter":
        own_spec = pl.BlockSpec((None, tr, cols), lambda rel, i, t: (t[rel], i, 0))
    else:
        own_spec = pl.BlockSpec((tr, cols), lambda rel, i, t: (i, t[rel]))
    blk = pl.BlockSpec((None, tr, cols), lambda rel, i, t: (rel, i, 0))

    def body(t_ref, a_ref, b_ref, o_ref):
        o_ref[...] = (a_ref[...].astype(F32) + b_ref[...].astype(F32)).astype(o_ref.dtype)

    return pl.pallas_call(
        body, name=name, out_shape=jax.ShapeDtypeStruct(got.shape, got.dtype),
        grid_spec=pltpu.PrefetchScalarGridSpec(num_scalar_prefetch=1, grid=(NCHIP, rows // tr), in_specs=[own_spec, blk],
                                               out_specs=blk),
        compiler_params=_params(32),
    )(table, own, got)


HBM = pl.BlockSpec(memory_space=pltpu.HBM)
SEM = pl.BlockSpec(memory_space=pltpu.SEMAPHORE)
DATAFLOW = pltpu.SideEffectType.DATAFLOW_SIDE_EFFECTING


def _pair_copies(kinds):
    def describe(srcs, lands, send_sems, recv_sems):
        x, y, c = _place()
        na = len(srcs)
        for rel in range(NCHIP):
            p = _dev_index(*_rel_chip(x, y, rel), 1 - c)
            for a in range(na):
                src = srcs[a].at[p] if kinds[a] == "scatter" else _cols(srcs[a], p, srcs[a].shape[1] // NDEV)
                cp = pltpu.make_async_remote_copy(src, lands[a].at[rel], send_sems.at[rel * na + a], recv_sems.at[rel * na + a],
                                                  device_id=(x, y, 1 - c), device_id_type=MESH)
                yield cp, cp
    return describe


def _chip_copies(srcs, lands, send_sems, recv_sems):
    x, y, c = _place()
    na = len(srcs)
    for rel in range(1, NCHIP):
        for a in range(na):
            i = (rel - 1) * na + a
            cp = pltpu.make_async_remote_copy(srcs[a].at[rel], lands[a].at[rel], send_sems.at[i], recv_sems.at[i],
                                              device_id=(*_rel_chip(x, y, rel), c), device_id_type=MESH)
            yield cp, cp


def _slab(ref, kind, s, width):
    return _cols(ref, s, width) if kind == "bcast_cols" else ref.at[s]


def _all_copies(srcs, lands, send_sems, recv_sems):
    x, y, c = _place()
    na = len(srcs)
    me = _dev_index(x, y, c)
    for a in range(na):
        yield pltpu.make_async_copy(srcs[a], lands[a].at[me], send_sems.at[a]), None
    for k in range(1, NDEV):
        to, frm = (me + k) % NDEV, (me + NDEV - k) % NDEV
        for a in range(na):
            i = k * na + a
            send = pltpu.make_async_remote_copy(srcs[a], lands[a].at[me], send_sems.at[i], recv_sems.at[i],
                                                device_id=_coords(to), device_id_type=MESH)
            arrival = pltpu.make_async_remote_copy(srcs[a], lands[a].at[frm], send_sems.at[i], recv_sems.at[i],
                                                   device_id=_coords(to), device_id_type=MESH)
            yield send, arrival


def _gather_copies(kinds):
    def describe(srcs, lands, send_sems, recv_sems):
        x, y, c = _place()
        na = len(srcs)
        me = _dev_index(x, y, c)
        for a in range(na):
            yield pltpu.make_async_copy(srcs[a], _slab(lands[a], kinds[a], me, srcs[a].shape[-1]),
                                        send_sems.at[NCHIP * na + a]), None
        for rel in range(NCHIP):
            to = (x, y, 1 - c) if rel == 0 else (*_rel_chip(x, y, rel), c)
            for a in range(na):
                width = srcs[a].shape[-1]
                i = rel * na + a
                send = pltpu.make_async_remote_copy(srcs[a], _slab(lands[a], kinds[a], me, width), send_sems.at[i], recv_sems.at[i],
                                                    device_id=to, device_id_type=MESH)
                arrival = pltpu.make_async_remote_copy(srcs[a], _slab(lands[a], kinds[a], _dev_index(*to), width), send_sems.at[i],
                                                       recv_sems.at[i], device_id=to, device_id_type=MESH)
                yield send, arrival
    return describe


def _relay_copies(kinds, widths, phase):
    def describe(srcs, lands, send_sems, recv_sems):
        x, y, c = _place()
        na = len(lands)
        here, over_x, over_y, across = (x, y), (1 - x, y), (x, 1 - y), (1 - x, 1 - y)
        sibling = (x, y, 1 - c)

        def half(ref, h):
            rows = ref.shape[0] // 2
            return ref if h is None else ref.at[pl.ds(h * rows, rows), :]

        def slab(a, chip, core, h=None):
            return half(_slab(lands[a], kinds[a], _dev_index(*chip, core), widths[a]), h)

        def pair(a, k, src, dst, to, arrival):
            i = k * na + a
            return (pltpu.make_async_remote_copy(src, dst, send_sems.at[i], recv_sems.at[i], device_id=to, device_id_type=MESH),
                    pltpu.make_async_remote_copy(arrival, arrival, send_sems.at[i], recv_sems.at[i], device_id=to,
                                                 device_id_type=MESH))

        if phase == 1:
            for a in range(na):
                yield pltpu.make_async_copy(srcs[a], slab(a, here, c), send_sems.at[a]), None
            for k, h, chip in ((1, 0, over_x), (2, 1, over_y), (3, 1, over_x), (4, 0, over_y)):
                for a in range(na):
                    yield pair(a, k, half(srcs[a], h), slab(a, here, c, h), (*chip, c), slab(a, chip, c, h))
            for a in range(na):
                yield pair(a, 5, srcs[a], slab(a, here, c), sibling, slab(a, here, 1 - c))
        elif phase == 2:
            for a in range(na):
                yield pair(a, 0, slab(a, over_x, c, 0), slab(a, over_x, c, 0), (*over_y, c), slab(a, across, c, 0))
                yield pair(a, 1, slab(a, over_y, c, 1), slab(a, over_y, c, 1), (*over_x, c), slab(a, across, c, 1))
            for k, (chip, h) in enumerate(((over_x, 0), (over_y, 1), (over_x, 1), (over_y, 0)), start=2):
                for a in range(na):
                    yield pair(a, k, slab(a, chip, c, h), slab(a, chip, c, h), sibling, slab(a, chip, 1 - c, h))
        else:
            for h in (0, 1):
                for a in range(na):
                    yield pair(a, h, slab(a, across, c, h), slab(a, across, c, h), sibling, slab(a, across, 1 - c, h))
    return describe


RELAY_COPIES = {1: 6, 2: 6, 3: 2}


def _relay_finish(arrays, kinds, widths, name):
    na = len(arrays)
    describe = _relay_copies(kinds, widths, 3)

    def body(*refs):
        copies = list(describe((), refs[na:2 * na], refs[2 * na], refs[2 * na + 1]))
        for send, _ in copies:
            send.start()
        for send, arrival in copies:
            send.wait_send()
            arrival.wait_recv()

    return pl.pallas_call(body, name=name, in_specs=[ANY] * na, out_specs=[ANY] * na,
                          out_shape=[jax.ShapeDtypeStruct(a.shape, a.dtype) for a in arrays],
                          input_output_aliases={i: i for i in range(na)},
                          scratch_shapes=[pltpu.SemaphoreType.DMA((RELAY_COPIES[3] * na,))] * 2)(*arrays)


def _pass_to_sibling(arrays, kinds, widths, name):
    na = len(arrays)

    def body(*refs):
        bufs = refs[na:2 * na]
        send_sems, recv_sems = refs[2 * na:]
        x, y, c = _place()
        copies = []
        for rel in range(1, NCHIP):
            for a in range(na):
                mine = _slab(bufs[a], kinds[a], _dev_index(*_rel_chip(x, y, rel), c), widths[a])
                theirs = _slab(bufs[a], kinds[a], _dev_index(*_rel_chip(x, y, rel), 1 - c), widths[a])
                send = pltpu.make_async_remote_copy(mine, mine, send_sems.at[a, rel], recv_sems.at[a, rel],
                                                    device_id=(x, y, 1 - c), device_id_type=MESH)
                send.start()
                copies.append((send, pltpu.make_async_remote_copy(theirs, theirs, send_sems.at[a, rel], recv_sems.at[a, rel],
                                                                  device_id=(x, y, 1 - c), device_id_type=MESH)))
        for send, arrival in copies:
            send.wait_send()
            arrival.wait_recv()

    return pl.pallas_call(body, name=name, in_specs=[ANY] * na, out_specs=[ANY] * na,
                          out_shape=[jax.ShapeDtypeStruct(a.shape, a.dtype) for a in arrays],
                          input_output_aliases={i: i for i in range(na)}, scratch_shapes=_sems(na, NCHIP)[:2])(*arrays)


def _start_copies(name, srcs, lands, describe, ncopies, after=()):
    arrays = list(srcs) + list(lands)
    ns, n, nin = len(srcs), len(arrays), len(arrays) + len(after)

    def body(*refs):
        for send, _ in describe(refs[:ns], refs[ns:n], refs[nin], refs[nin + 1]):
            send.start()
        refs[-1][...] = jnp.zeros_like(refs[-1])

    out = pl.pallas_call(
        body, name=name,
        out_shape=(pltpu.SemaphoreType.DMA((ncopies,)), pltpu.SemaphoreType.DMA((ncopies,)),
                   *[pltpu.HBM(a.shape, a.dtype) for a in arrays], jax.ShapeDtypeStruct((8, 128), F32)),
        in_specs=[HBM] * n + [ANY] * len(after), out_specs=(SEM, SEM, *[HBM] * n, pl.BlockSpec(memory_space=pltpu.VMEM)),
        input_output_aliases={i: 2 + i for i in range(n)},
        compiler_params=pltpu.CompilerParams(has_side_effects=DATAFLOW),
    )(*[pltpu.with_memory_space_constraint(a, pltpu.HBM) for a in arrays], *after)
    return out[0], out[1], list(out[2:2 + n]), out[-1]


def _wait_copies(name, send_sems, recv_sems, thru, ns, describe, after):
    n = len(thru)

    def body(*refs):
        for send, arrival in describe(refs[:ns], refs[ns:n], refs[n], refs[n + 1]):
            if arrival is None:
                send.wait()
            else:
                send.wait_send()
                arrival.wait_recv()

    out = pl.pallas_call(
        body, name=name, out_shape=tuple(pltpu.HBM(a.shape, a.dtype) for a in thru),
        in_specs=[HBM] * n + [SEM, SEM] + [ANY] * len(after), out_specs=tuple([HBM] * n),
        input_output_aliases={i: i for i in range(n)},
        compiler_params=pltpu.CompilerParams(has_side_effects=DATAFLOW),
    )(*thru, send_sems, recv_sems, *after)
    return list(out[:ns]), list(out[ns:])


def _adamw(parts, w, m, v, tr, name, tc=None, first_row=0, into=None):
    rows, cols = w.shape
    part_rows = parts[0][0].shape[1]
    tc = cols if tc is None else tc
    nparts = len(parts)
    tile0 = first_row // tr
    blk = pl.BlockSpec((tr, tc), lambda i, j: (tile0 + i, j))
    carried = [] if into is None else list(into)

    def slab_spec(s):
        return pl.BlockSpec((None, tr, tc), lambda i, j: (s, i, j))

    def body(*refs):
        p_refs = refs[:nparts]
        w_ref, m_ref, v_ref = refs[nparts:nparts + 3]
        g_ref, d_ref, nm_ref, nv_ref = refs[nparts + 3 + len(carried):]
        g = p_refs[0][...].astype(F32)
        for p_ref in p_refs[1:]:
            g = g + p_ref[...].astype(F32)
        nm = ADAM_B1 * m_ref[...] + (1.0 - ADAM_B1) * g
        nv = ADAM_B2 * v_ref[...] + (1.0 - ADAM_B2) * (g * g)
        m_hat = nm / (1.0 - ADAM_B1 ** ADAM_STEP)
        v_hat = nv / (1.0 - ADAM_B2 ** ADAM_STEP)
        g_ref[...] = g
        d_ref[...] = -ADAM_LR * (m_hat / (jnp.sqrt(v_hat) + ADAM_EPS) + ADAM_WD * w_ref[...])
        nm_ref[...] = nm
        nv_ref[...] = nv

    return pl.pallas_call(
        body, grid=(part_rows // tr, cols // tc), name=name,
        in_specs=[slab_spec(s) for _, s in parts] + [blk, blk, blk] + [ANY] * len(carried), out_specs=[blk] * 4,
        out_shape=[jax.ShapeDtypeStruct((rows, cols), F32)] * 4,
        input_output_aliases={nparts + 3 + k: k for k in range(len(carried))},
        compiler_params=_params(40),
    )(*[a for a, _ in parts], w, m, v, *carried)


def _pack_small(vals, extra=None):
    flat = [vals[n].reshape(-1).astype(F32) for n, _ in SMALL]
    tail = jnp.zeros((SMALL_ROWS * 128 - LOSS_AT,), F32)
    if extra is not None:
        tail = tail.at[0].set(extra)
    return jnp.concatenate(flat + [tail]).reshape(SMALL_ROWS, 128)


def _unpack_small(packed, like):
    flat, out, at = packed.reshape(-1), {}, 0
    for n, size in SMALL:
        out[n] = flat[at:at + size].reshape(like[n].shape)
        at += size
    return out, flat[LOSS_AT]


def kernel(x, ffn1_norm, ffn1_w_in, ffn1_w_out, mix_norm, w_in_mix, w_pool, pool_scale, w_alpha, b_alpha, gla_norm, w_out_mix, ffn2_norm, ffn2_w_in, ffn2_w_out, final_norm, loss_target, m_ffn1_norm, m_ffn1_w_in, m_ffn1_w_out, m_mix_norm, m_w_in_mix, m_w_pool, m_pool_scale, m_w_alpha, m_b_alpha, m_gla_norm, m_w_out_mix, m_ffn2_norm, m_ffn2_w_in, m_ffn2_w_out, m_final_norm, v_ffn1_norm, v_ffn1_w_in, v_ffn1_w_out, v_mix_norm, v_w_in_mix, v_w_pool, v_pool_scale, v_w_alpha, v_b_alpha, v_gla_norm, v_w_out_mix, v_ffn2_norm, v_ffn2_w_in, v_ffn2_w_out, v_final_norm):
    names = ["ffn1_norm", "ffn1_w_in", "ffn1_w_out", "mix_norm", "w_in_mix", "w_pool", "pool_scale", "w_alpha", "b_alpha",
             "gla_norm", "w_out_mix", "ffn2_norm", "ffn2_w_in", "ffn2_w_out", "final_norm"]
    p = dict(zip(names, [ffn1_norm, ffn1_w_in, ffn1_w_out, mix_norm, w_in_mix, w_pool, pool_scale, w_alpha, b_alpha,
                         gla_norm, w_out_mix, ffn2_norm, ffn2_w_in, ffn2_w_out, final_norm]))
    m = dict(zip(names, [m_ffn1_norm, m_ffn1_w_in, m_ffn1_w_out, m_mix_norm, m_w_in_mix, m_w_pool, m_pool_scale, m_w_alpha,
                         m_b_alpha, m_gla_norm, m_w_out_mix, m_ffn2_norm, m_ffn2_w_in, m_ffn2_w_out, m_final_norm]))
    v = dict(zip(names, [v_ffn1_norm, v_ffn1_w_in, v_ffn1_w_out, v_mix_norm, v_w_in_mix, v_w_pool, v_pool_scale, v_w_alpha,
                         v_b_alpha, v_gla_norm, v_w_out_mix, v_ffn2_norm, v_ffn2_w_in, v_ffn2_w_out, v_final_norm]))

    mx, my, mc = _place()
    table = jnp.stack([_dev_index(*_rel_chip(mx, my, rel), mc) for rel in range(NCHIP)]).astype(jnp.int32)

    def landing(shard, kind):
        shape = (shard.shape[0], NDEV * shard.shape[1]) if kind == "bcast_cols" else (NDEV,) + shard.shape
        return lax.empty(shape, shard.dtype)

    def gather_begin(items, tag, after):
        kinds = [kind for _, kind in items]
        copies = _gather_copies(kinds)
        s, r, thru, tok = _start_copies(tag + "_start", [a for a, _ in items], [landing(a, kind) for a, kind in items], copies,
                                        (NCHIP + 1) * len(items), after)
        return (s, r, thru, copies, kinds, [a.shape[-1] for a, _ in items], tag), tok

    def gather_end(state, after):
        s, r, thru, copies, kinds, widths, tag = state
        _, lands = _wait_copies(tag + "_wait", s, r, thru, len(kinds), copies, after)
        return _pass_to_sibling(lands, kinds, widths, tag + "_pass")

    def shard16(n):
        return p[n][0].astype(BF16)

    g_w1in, g_w1out = _gather([(shard16("ffn1_w_in"), "bcast_cols"), (shard16("ffn1_w_out"), "bcast")], "gather_ffn1")
    mix_state, tok_m = gather_begin([(jnp.transpose(p["w_in_mix"][0]).astype(BF16), "bcast"), (shard16("w_out_mix"), "bcast"),
                                     (p["w_pool"][0].reshape(H * 32, PG), "bcast"), (p["w_alpha"][0], "bcast")], "gather_mix",
                                    (g_w1out,))
    ffn2_items = [(shard16("ffn2_w_in"), "bcast_cols"), (shard16("ffn2_w_out"), "bcast")]
    ffn2_kinds = [kind for _, kind in ffn2_items]
    ffn2_widths = [a.shape[-1] for a, _ in ffn2_items]
    relay = {ph: _relay_copies(ffn2_kinds, ffn2_widths, ph) for ph in (1, 2)}
    f_s, f_r, f_thru, tok_f = _start_copies("gather_ffn2_start", [a for a, _ in ffn2_items],
                                            [landing(a, kind) for a, kind in ffn2_items], relay[1],
                                            RELAY_COPIES[1] * len(ffn2_items), (tok_m,))
    ffn2_lands = {}

    def relay_ffn2(point, value):
        _, lands = _wait_copies("gather_ffn2_wait", f_s, f_r, f_thru, len(ffn2_items), relay[1], (value,))
        ffn2_lands["s"], ffn2_lands["r"], ffn2_lands["thru"], tok = _start_copies(
            "gather_ffn2_relay_start", [], lands, relay[2], RELAY_COPIES[2] * len(ffn2_items))
        return (tok,)

    full ={"ffn1_w_in": g_w1in, "ffn1_w_out": g_w1out.reshape(FF, D), "final_norm": final_norm.reshape(1, D)}
    for n in ("ffn1_norm", "mix_norm", "ffn2_norm", "pool_scale", "b_alpha", "gla_norm"):
        full[n] = p[n]

    xs, tgt = x[0], loss_target[0]
    h1, n1, gu1 = _fwd_ffn1(xs, full, after=(tok_m, tok_f))
    g_wmix, g_wo, g_wpool, g_walpha = gather_end(mix_state, (h1,))
    walpha = jnp.transpose(g_walpha, (1, 0, 2)).reshape(RANK, H * DK)
    full.update({
        "w_in_mix": g_wmix.reshape(D_IN, D),
        "w_in_mix_tail": jnp.pad(g_wmix[NDEV - 1, MIX_SHARD - RANK:], ((0, RANK_PAD - RANK), (0, 0))),
        "w_out_mix": g_wo.reshape(D, D),
        "w_pool": jnp.transpose(g_wpool.reshape(NDEV, H, 32, PG), (1, 0, 2, 3)).reshape(H, PG, PG).astype(BF16),
        "w_alpha": jnp.pad(walpha, ((0, RANK_PAD - RANK), (0, 0))).astype(BF16),
    })
    h2, sv = _fwd_mixer(h1, full, at=relay_ffn2)
    _, lands = _wait_copies("gather_ffn2_relay_wait", ffn2_lands["s"], ffn2_lands["r"], ffn2_lands["thru"], 0, relay[2], (h2,))
    g_w2in, g_w2out = _relay_finish(lands, ffn2_kinds, ffn2_widths, "gather_ffn2_finish")
    full.update({"ffn2_w_in": g_w2in, "ffn2_w_out": g_w2out.reshape(FF, D)})
    dh3, d_final, loss_part, n3, gu3 = _fwd_ffn2_loss(h2, tgt, full)


    def slab_shape(a, kind):
        return (NCHIP,) + (a.shape[1:] if kind == "scatter" else (a.shape[0], a.shape[1] // NDEV))

    def pair_add_all(own, got, tag):
        return [_pair_add(a, kind, got_a, table, tr, "%s_pair_add_%d" % (tag, i))
                for i, ((a, kind, tr), got_a) in enumerate(zip(own, got))]

    def reduce_begin(own, tag, after=()):
        kinds = [kind for _, kind, _ in own]
        copies = _pair_copies(kinds)
        s, r, thru, tok = _start_copies(tag + "_pair_start", [a for a, _, _ in own],
                                        [lax.empty(slab_shape(a, kind), a.dtype) for a, kind, _ in own], copies,
                                        NCHIP * len(own), after)
        return dict(own=own, copies=copies, s=s, r=r, thru=thru, tag=tag), tok

    def reduce_middle(st, after):
        own, tag = st["own"], st["tag"]
        sent, got = _wait_copies(tag + "_pair_wait", st["s"], st["r"], st["thru"], len(own), st["copies"], after)
        pre = pair_add_all([(a, kind, tr) for a, (_, kind, tr) in zip(sent, own)], got, tag)
        st["s"], st["r"], st["thru"], tok = _start_copies(tag + "_chip_start", pre, [lax.empty(a.shape, a.dtype) for a in pre],
                                                          _chip_copies, (NCHIP - 1) * len(pre))
        return tok

    def reduce_end(st, after):
        n = len(st["own"])
        pre, land = _wait_copies(st["tag"] + "_chip_wait", st["s"], st["r"], st["thru"], n, _chip_copies, after)
        return [[(a, 0)] + [(b, rel) for rel in range(1, NCHIP)] for a, b in zip(pre, land)]

    def w_in_item(a):
        return (a, "scatter_cols", 512)

    def w_out_item(a):
        return (a.reshape(NDEV, WOUT_SHARD, D), "scatter", WOUT_SHARD)

    red, small = {}, {}

    def at(point, value):
        if point == "ffn2_w_out":
            red["w2out"], tok = reduce_begin([w_out_item(value)], "ffn2_w_out")
        elif point == "ffn2_w_in":
            tok_a = reduce_middle(red["w2out"], (value,))
            red["w2in"], tok = reduce_begin([w_in_item(value)], "ffn2_w_in", (tok_a,))
        elif point == "pool_bwd":
            tok = reduce_middle(red["w2in"], (value,))
        elif point == "mixer_weights":
            d_wmix8 = value["w_in_mix"].reshape(NDEV, MIX_SHARD, D)
            d_wpool8 = jnp.transpose(value["w_pool"].reshape(H, NDEV, 32, PG), (1, 0, 2, 3)).reshape(NDEV, H * 32, PG)
            d_walpha8 = jnp.transpose(value["w_alpha"][:RANK].reshape(RANK, NDEV, H * DK // NDEV), (1, 0, 2))
            red["mix"], tok = reduce_begin([(d_wmix8, "scatter", MIX_SHARD),
                                            (value["w_out_mix"].reshape(NDEV, D // NDEV, D), "scatter", D // NDEV),
                                            (d_wpool8, "scatter", H * 32), (d_walpha8, "scatter", RANK)], "mix")
        elif point == "dh1":
            tok = reduce_middle(red["mix"], (value,))
        elif point == "ffn1_w_out":
            red["w1out"], tok = reduce_begin([w_out_item(value)], "ffn1_w_out")
        elif point == "ffn1_dx_a":
            tok = reduce_middle(red["w1out"], (value,))
        elif point == "ffn1_w_in_top":
            red["w1in_top"], tok = reduce_begin([w_in_item(value)], "ffn1_w_in_top")
        elif point == "ffn1_w_in_bottom":
            tok_a = reduce_middle(red["w1in_top"], (value,))
            red["w1in_bottom"], tok = reduce_begin([w_in_item(value)], "ffn1_w_in_bottom", (tok_a,))
        return (tok,)

    dx, g = _backward(xs, h1, h2, n1, gu1, sv, n3, gu3, dh3, full, at)
    packed = _pack_small(dict(final_norm=d_final, **g), loss_part[0, 0])
    small_s, small_r, small_thru, tok_s = _start_copies("gather_small_start", [packed], [lax.empty((NDEV,) + packed.shape, F32)],
                                                        _all_copies, NDEV, (dx,))
    tok_c = reduce_middle(red["w1in_bottom"], (tok_s,))

    def upd(parts, n, shape2d, tr):
        res = _adamw(parts, p[n].reshape(shape2d), m[n].reshape(shape2d), v[n].reshape(shape2d), tr, "adamw_" + n)
        return [r.reshape(p[n].shape) for r in res]

    def transposed(a):
        return jnp.transpose(a[0])

    (p_w2out,) = reduce_end(red["w2out"], (tok_c,))
    (p_w2in,) = reduce_end(red["w2in"], (tok_c,))
    p_wmix, p_wo, p_wpool, p_walpha = reduce_end(red["mix"], (tok_c,))
    out = {
        "ffn2_w_in": upd(p_w2in, "ffn2_w_in", (D, WIN_SHARD), 128),
        "ffn2_w_out": upd(p_w2out, "ffn2_w_out", (WOUT_SHARD, D), 64),
        "w_out_mix": upd(p_wo, "w_out_mix", (D // NDEV, D), 64),
        "w_pool": upd(p_wpool, "w_pool", (H * 32, PG), H * 32),
        "w_alpha": upd(p_walpha, "w_alpha", (RANK, H * DK // NDEV), RANK),
    }
    out["w_in_mix"] = [jnp.transpose(r)[None] for r in
                       _adamw(p_wmix, transposed(p["w_in_mix"]), transposed(m["w_in_mix"]), transposed(v["w_in_mix"]),
                              MIX_SHARD, "adamw_w_in_mix", tc=512)]
    _, (r_small,) = _wait_copies("gather_small_wait", small_s, small_r, small_thru, 1, _all_copies,
                                 (out["w_in_mix"][3], out["ffn2_w_in"][3], out["ffn2_w_out"][3], out["w_out_mix"][3]))
    small_res = _adamw([(r_small, s) for s in range(NDEV)], _pack_small(p), _pack_small(m), _pack_small(v), SMALL_ROWS,
                       "adamw_small")
    (p_w1out,) = reduce_end(red["w1out"], (small_res[0],))
    out["ffn1_w_out"] = upd(p_w1out, "ffn1_w_out", (WOUT_SHARD, D), 64)
    w1in = [a.reshape(D, WIN_SHARD) for a in (p["ffn1_w_in"], m["ffn1_w_in"], v["ffn1_w_in"])]
    (p_top,) = reduce_end(red["w1in_top"], (out["ffn1_w_out"][3],))
    top = _adamw(p_top, *w1in, 128, "adamw_ffn1_w_in_top")
    (p_bottom,) = reduce_end(red["w1in_bottom"], (top[3],))
    out["ffn1_w_in"] = [r.reshape(p["ffn1_w_in"].shape) for r in
                        _adamw(p_bottom, *w1in, 128, "adamw_ffn1_w_in_bottom", first_row=D // 2, into=top)]
    unpacked = [_unpack_small(r, p) for r in small_res]
    loss = unpacked[0][1]
    for n, _ in SMALL:
        out[n] = [u[0][n] for u in unpacked]

    return (loss, dx.reshape(1, S, D), *[out[n][0] for n in names], *[out[n][1] for n in names],
            *[out[n][2] for n in names], *[out[n][3] for n in names])
```

```python
import jax
import jax.numpy as jnp
from jax import lax
from jax.experimental import pallas as pl
from jax.experimental.pallas import tpu as pltpu

F32, BF16 = jnp.float32, jnp.bfloat16
MESH = pl.DeviceIdType.MESH
ANY = pl.BlockSpec(memory_space=pl.ANY)

NDEV = 8
S = 2048
D = 2048
FF = 5632
WIN_SHARD = 2 * FF // NDEV
WOUT_SHARD = FF // NDEV
D_POOL = 1024
PG = 256
POOL_WINDOWS = (2, 4, 8, 16)
H = 4
DK = 128
DV = 256
CH = 64
NCH = S // CH
RANK = 16
RANK_PAD = 128
D_IN = 4112
D_IN_PAD = 4224
MIX_SHARD = D_IN // NDEV
O_Q, O_K, O_V, O_G, O_R = 1024, 1536, 2048, 3072, 4096
GATE_NORM = 16.0
QK_SCALE = DK ** -0.5
EPS = 1e-6
ADAM_LR, ADAM_B1, ADAM_B2, ADAM_EPS, ADAM_WD, ADAM_STEP = 0.001, 0.9, 0.999, 1e-08, 0.01, 10
V7X_VMEM_BYTES = 64 << 20

SMALL = (("ffn1_norm", 2048), ("mix_norm", 2048), ("ffn2_norm", 2048), ("final_norm", 2048),
         ("pool_scale", 1024), ("b_alpha", 512), ("gla_norm", 256))
SMALL_ROWS = 80
LOSS_AT = sum(n for _, n in SMALL)


def _params(vmem_mb, sem=None):
    return pltpu.CompilerParams(dimension_semantics=sem, vmem_limit_bytes=min(vmem_mb << 20, V7X_VMEM_BYTES - (4 << 20)))


def _pallas(body, **kwargs):
    call = pl.pallas_call(body, **kwargs)

    def run(*operands):
        return call(*[pltpu.with_memory_space_constraint(a, pltpu.HBM) if a.size * a.dtype.itemsize >= 1 << 18 else a
                      for a in operands])
    return run


def _dot(a, b):
    return jnp.dot(a, b, preferred_element_type=F32)


def _dot_nt(a, b):
    return lax.dot_general(a, b, (((1,), (1,)), ((), ())), preferred_element_type=F32)


def _dot_tn(a, b):
    return lax.dot_general(a, b, (((0,), (0,)), ((), ())), preferred_element_type=F32)


def _sigmoid(x):
    return 0.5 * jnp.tanh(0.5 * x) + 0.5


def _log_sigmoid(x):
    return jnp.minimum(x, 0.0) - jnp.log(1.0 + jnp.exp(-jnp.abs(x)))


def _rms(x, g):
    r = lax.rsqrt(jnp.mean(x * x, axis=-1, keepdims=True) + EPS)
    return x * r * g


def _rms_bwd(dn, x, g):
    r = lax.rsqrt(jnp.mean(x * x, axis=-1, keepdims=True) + EPS)
    xh = x * r
    dxh = dn * g
    dx = r * (dxh - xh * jnp.mean(dxh * xh, axis=-1, keepdims=True))
    return dx, jnp.sum(dn * xh, axis=0, keepdims=True)


ROWS = 64


def _row_loop(total, fn, init=0):
    def step(t, carry):
        return fn(pl.ds(pl.multiple_of(t * ROWS, ROWS), ROWS), carry)
    return lax.fori_loop(0, total // ROWS, step, init)


def _split3(x):
    hi = x.astype(BF16)
    r1 = x - hi.astype(F32)
    mid = r1.astype(BF16)
    lo = (r1 - mid.astype(F32)).astype(BF16)
    return hi, mid, lo


def _tri_dot(tri_b, x):
    hi, mid, lo = _split3(x)
    return (_dot(tri_b, lo) + _dot(tri_b, mid)) + _dot(tri_b, hi)


FFN_TS, FFN_TF = 512, 512


def _ffn_specs():
    wg = pl.BlockSpec((D, FFN_TF), lambda i, j: (0, j))
    wu = pl.BlockSpec((D, FFN_TF), lambda i, j: (0, FF // FFN_TF + j))
    wo = pl.BlockSpec((FFN_TF, D), lambda i, j: (j, 0))
    row = pl.BlockSpec((FFN_TS, D), lambda i, j: (i, 0))
    vec = pl.BlockSpec((1, D), lambda i, j: (0, 0))
    gu = pl.BlockSpec((2, FFN_TS, FFN_TF), lambda i, j: (0, i, j))
    return wg, wu, wo, row, vec, gu


def _ordered_after(body, n_in, after):
    def wrapped(*refs):
        return body(*refs[:n_in], *refs[n_in + len(after):])
    return wrapped, [ANY] * len(after)


def _ffn_fwd(h, g, w_in8, w_out, name, after=()):
    nj = FF // FFN_TF
    wg, wu, wo, row, vec, gu = _ffn_specs()

    def body(h_ref, g_ref, wg_ref, wu_ref, wo_ref, ho_ref, n_ref, gu_ref, acc_ref):
        j = pl.program_id(1)

        @pl.when(j == 0)
        def _():
            def norm(rows, c):
                n_ref[rows, :] = _rms(h_ref[rows, :], g_ref[...]).astype(BF16)
                return c
            _row_loop(FFN_TS, norm)
            acc_ref[...] = jnp.zeros_like(acc_ref)

        n = n_ref[...]
        gate = _dot(n, wg_ref[...])
        up = _dot(n, wu_ref[...])
        gu_ref[0] = gate.astype(BF16)
        gu_ref[1] = up.astype(BF16)
        a = (gate * _sigmoid(gate)) * up
        acc_ref[...] += _dot(a.astype(BF16), wo_ref[...])

        @pl.when(j == nj - 1)
        def _():
            def residual(rows, c):
                ho_ref[rows, :] = h_ref[rows, :] + 0.5 * acc_ref[rows, :]
                return c
            _row_loop(FFN_TS, residual)

    body, extra = _ordered_after(body, 5, after)
    return _pallas(
        body, grid=(S // FFN_TS, nj), name=name,
        in_specs=[row, vec, wg, wu, wo] + extra, out_specs=[row, row, gu],
        out_shape=[jax.ShapeDtypeStruct((S, D), F32), jax.ShapeDtypeStruct((S, D), BF16),
                   jax.ShapeDtypeStruct((2, S, FF), BF16)],
        scratch_shapes=[pltpu.VMEM((FFN_TS, D), F32)],
        compiler_params=_params(56, ("arbitrary", "arbitrary")),
    )(h, g, w_in8, w_in8, w_out, *after)


def _ffn_bwd_x(dhp, h, g, gu_arr, w_in8, w_out, name, after=()):
    ni, nj = S // FFN_TS, FF // FFN_TF
    wg, wu, wo, row, vec, gu = _ffn_specs()
    act = pl.BlockSpec((FFN_TS, FFN_TF), lambda i, j: (i, j))

    def body(dhp_ref, h_ref, g_ref, gu_ref, wg_ref, wu_ref, wo_ref,
             dgu_ref, a_ref, df_ref, dh_ref, dhb_ref, dg_ref, acc_ref):
        i, j = pl.program_id(0), pl.program_id(1)

        @pl.when(j == 0)
        def _():
            def half(rows, c):
                df_ref[rows, :] = (0.5 * dhp_ref[rows, :]).astype(BF16)
                return c
            _row_loop(FFN_TS, half)
            acc_ref[...] = jnp.zeros_like(acc_ref)

        gate = gu_ref[0].astype(F32)
        up = gu_ref[1].astype(F32)
        da = _dot_nt(df_ref[...], wo_ref[...])
        sg = _sigmoid(gate)
        silu = gate * sg
        dgate = (da * up * (sg * (1.0 + gate * (1.0 - sg)))).astype(BF16)
        dup = (da * silu).astype(BF16)
        a_ref[...] = (silu * up).astype(BF16)
        dgu_ref[0] = dgate
        dgu_ref[1] = dup
        acc_ref[...] += _dot_nt(dgate, wg_ref[...]) + _dot_nt(dup, wu_ref[...])

        @pl.when(j == nj - 1)
        def _():
            def norm_bwd(rows, dg):
                dx, dg_rows = _rms_bwd(acc_ref[rows, :], h_ref[rows, :], g_ref[...])
                dh = dhp_ref[rows, :] + dx
                dh_ref[rows, :] = dh
                dhb_ref[rows, :] = dh.astype(BF16)
                return dg + dg_rows
            dg = _row_loop(FFN_TS, norm_bwd, jnp.zeros((1, D), F32))

            @pl.when(i == 0)
            def _():
                dg_ref[...] = dg

            @pl.when(i > 0)
            def _():
                dg_ref[...] += dg

    body, extra = _ordered_after(body, 7, after)
    return _pallas(
        body, grid=(ni, nj), name=name,
        in_specs=[row, row, vec, gu, wg, wu, wo] + extra,
        out_specs=[gu, act, row, row, row, vec],
        out_shape=[jax.ShapeDtypeStruct((2, S, FF), BF16), jax.ShapeDtypeStruct((S, FF), BF16),
                   jax.ShapeDtypeStruct((S, D), BF16), jax.ShapeDtypeStruct((S, D), F32),
                   jax.ShapeDtypeStruct((S, D), BF16), jax.ShapeDtypeStruct((1, D), F32)],
        scratch_shapes=[pltpu.VMEM((FFN_TS, D), F32)],
        compiler_params=_params(58, ("arbitrary", "arbitrary")),
    )(dhp, h, g, gu_arr, w_in8, w_in8, w_out, *after)


def _ffn_bwd_gu(df, gu_arr, w_out, name, after=()):
    ni = S // FFN_TS
    gu = pl.BlockSpec((2, FFN_TS, FFN_TF), lambda j, i: (0, i, j))
    wo = pl.BlockSpec((FFN_TF, D), lambda j, i: (j, 0))
    df_all = pl.BlockSpec((S, D), lambda j, i: (0, 0), pipeline_mode=pl.Buffered(1))

    def body(df_ref, gu_ref, wo_ref, dgu_ref, dwo_ref, acc_ref):
        i = pl.program_id(1)
        df_rows = df_ref[pl.ds(pl.multiple_of(i * FFN_TS, FFN_TS), FFN_TS), :]
        gate = gu_ref[0].astype(F32)
        up = gu_ref[1].astype(F32)
        da = _dot_nt(df_rows, wo_ref[...])
        sg = _sigmoid(gate)
        silu = gate * sg
        dgu_ref[0] = (da * up * (sg * (1.0 + gate * (1.0 - sg)))).astype(BF16)
        dgu_ref[1] = (da * silu).astype(BF16)
        part = _dot_tn((silu * up).astype(BF16), df_rows)

        @pl.when(i == 0)
        def _():
            acc_ref[...] = part

        @pl.when(i > 0)
        def _():
            acc_ref[...] += part

        @pl.when(i == ni - 1)
        def _():
            dwo_ref[...] = acc_ref[...].astype(BF16)

    body, extra = _ordered_after(body, 3, after)
    return _pallas(
        body, grid=(FF // FFN_TF, ni), name=name,
        in_specs=[df_all, gu, wo] + extra, out_specs=[gu, wo],
        out_shape=[jax.ShapeDtypeStruct((2, S, FF), BF16), jax.ShapeDtypeStruct((FF, D), BF16)],
        scratch_shapes=[pltpu.VMEM((FFN_TF, D), F32)],
        compiler_params=_params(40, ("arbitrary", "arbitrary")),
    )(df, gu_arr, w_out, *after)


def _ffn_bwd_dx(dgu, dhp, h, g, w_in, first_tile, ntiles, name, after=(), into=None):
    nj = FF // FFN_TF
    wg, wu, _, _, vec, _ = _ffn_specs()
    row_in = pl.BlockSpec((FFN_TS, D), lambda i, j: (first_tile + i, 0))
    dgu_spec = pl.BlockSpec((2, FFN_TS, FFN_TF), lambda i, j: (0, first_tile + i, j))
    after = tuple(after) + (() if into is None else (into,))

    def body(dgu_ref, dhp_ref, h_ref, g_ref, wg_ref, wu_ref, dh_ref, dg_ref, acc_ref):
        i, j = pl.program_id(0), pl.program_id(1)

        @pl.when(j == 0)
        def _():
            acc_ref[...] = jnp.zeros_like(acc_ref)

        acc_ref[...] += _dot_nt(dgu_ref[0], wg_ref[...]) + _dot_nt(dgu_ref[1], wu_ref[...])

        @pl.when(j == nj - 1)
        def _():
            def norm_bwd(rows, dg):
                dx, dg_rows = _rms_bwd(acc_ref[rows, :], h_ref[rows, :], g_ref[...])
                dh_ref[rows, :] = dhp_ref[rows, :] + dx
                return dg + dg_rows
            dg = _row_loop(FFN_TS, norm_bwd, jnp.zeros((1, D), F32))

            @pl.when(i == 0)
            def _():
                dg_ref[...] = dg

            @pl.when(i > 0)
            def _():
                dg_ref[...] += dg

    body, extra = _ordered_after(body, 6, after)
    return _pallas(
        body, grid=(ntiles, nj), name=name,
        in_specs=[dgu_spec, row_in, row_in, vec, wg, wu] + extra, out_specs=[row_in, vec],
        out_shape=[jax.ShapeDtypeStruct((S, D), F32), jax.ShapeDtypeStruct((1, D), F32)],
        input_output_aliases={} if into is None else {6 + len(after) - 1: 0},
        scratch_shapes=[pltpu.VMEM((FFN_TS, D), F32)],
        compiler_params=_params(48, ("arbitrary", "arbitrary")),
    )(dgu, dhp, h, g, w_in, w_in, *after)


def _tn_matmul(a, b, a_spec, b_spec, out_shape, out_spec, grid, name, vmem_mb, after=()):
    def body(a_ref, b_ref, o_ref):
        o_ref[...] = _dot_tn(a_ref[...], b_ref[...]).astype(o_ref.dtype)

    body, extra = _ordered_after(body, 2, after)
    return _pallas(body, grid=grid, name=name, in_specs=[a_spec, b_spec] + extra, out_specs=out_spec,
                          out_shape=out_shape, compiler_params=_params(vmem_mb))(a, b, *after)


def _resident(shape):
    return pl.BlockSpec(shape, lambda i: (0,) * len(shape), pipeline_mode=pl.Buffered(1))


def _norm_matmul(h, g, w, w_tail, name, after=(), ts=256):
    main = w.shape[0] // 128 * 128
    n_out = main + w_tail.shape[0]

    def body(h_ref, g_ref, w_ref, wt_ref, u_ref, n_ref):
        def norm(rows, c):
            n_ref[rows, :] = _rms(h_ref[rows, :], g_ref[...]).astype(BF16)
            return c
        _row_loop(ts, norm)
        n = n_ref[...]
        u_ref[:, :main] = _dot_nt(n, w_ref[:main, :])
        u_ref[:, main:] = _dot_nt(n, wt_ref[...])

    body, extra = _ordered_after(body, 4, after)
    return _pallas(
        body, grid=(S // ts,), name=name,
        in_specs=[pl.BlockSpec((ts, D), lambda i: (i, 0)), pl.BlockSpec((1, D), lambda i: (0, 0)), _resident(w.shape),
                  _resident(w_tail.shape)] + extra,
        out_specs=[pl.BlockSpec((ts, n_out), lambda i: (i, 0)), pl.BlockSpec((ts, D), lambda i: (i, 0))],
        out_shape=[jax.ShapeDtypeStruct((S, n_out), F32), jax.ShapeDtypeStruct((S, D), BF16)],
        compiler_params=_params(48, ("arbitrary",)),
    )(h, g, w, w_tail, *after)


def _matmul_residual(a, w, res, name, after=(), ts=512):
    k, n_out = w.shape

    def body(a_ref, w_ref, r_ref, o_ref):
        o_ref[...] = r_ref[...] + _dot(a_ref[...], w_ref[...])

    body, extra = _ordered_after(body, 3, after)
    return _pallas(
        body, grid=(S // ts,), name=name,
        in_specs=[pl.BlockSpec((ts, k), lambda i: (i, 0)), _resident(w.shape), pl.BlockSpec((ts, n_out), lambda i: (i, 0))] + extra,
        out_specs=pl.BlockSpec((ts, n_out), lambda i: (i, 0)),
        out_shape=jax.ShapeDtypeStruct((S, n_out), F32),
        compiler_params=_params(40),
    )(a, w, res, *after)


def _nt_matmul(a, w, name, after=(), ts=512):
    n_out, k = w.shape

    def body(a_ref, w_ref, o_ref):
        o_ref[...] = _dot_nt(a_ref[...], w_ref[...])

    body, extra = _ordered_after(body, 2, after)
    return _pallas(
        body, grid=(S // ts,), name=name,
        in_specs=[pl.BlockSpec((ts, k), lambda i: (i, 0)), _resident(w.shape)] + extra,
        out_specs=pl.BlockSpec((ts, n_out), lambda i: (i, 0)),
        out_shape=jax.ShapeDtypeStruct((S, n_out), F32),
        compiler_params=_params(40),
    )(a, w, *after)


def _matmul_normbwd(du, w, w_tail, h, g, dres, name, after=(), ts=256):
    main = w.shape[0] // 128 * 128
    n_in = main + w_tail.shape[0]
    row = pl.BlockSpec((ts, D), lambda i: (i, 0))
    vec = pl.BlockSpec((1, D), lambda i: (0, 0))

    def body(du_ref, w_ref, wt_ref, h_ref, g_ref, dres_ref, dh_ref, dg_ref, half_ref, acc_ref):
        i = pl.program_id(0)
        acc_ref[...] = _dot(du_ref[:, :main], w_ref[:main, :]) + _dot(du_ref[:, main:], wt_ref[...])

        def norm_bwd(rows, dg):
            dx, dg_rows = _rms_bwd(acc_ref[rows, :], h_ref[rows, :], g_ref[...])
            dh = dres_ref[rows, :] + dx
            dh_ref[rows, :] = dh
            half_ref[rows, :] = (0.5 * dh).astype(BF16)
            return dg + dg_rows
        dg = _row_loop(ts, norm_bwd, jnp.zeros((1, D), F32))

        @pl.when(i == 0)
        def _():
            dg_ref[...] = dg

        @pl.when(i > 0)
        def _():
            dg_ref[...] += dg

    body, extra = _ordered_after(body, 6, after)
    return _pallas(
        body, grid=(S // ts,), name=name,
        in_specs=[pl.BlockSpec((ts, n_in), lambda i: (i, 0)), _resident(w.shape), _resident(w_tail.shape), row, vec, row] + extra,
        out_specs=[row, vec, row],
        out_shape=[jax.ShapeDtypeStruct((S, D), F32), jax.ShapeDtypeStruct((1, D), F32), jax.ShapeDtypeStruct((S, D), BF16)],
        scratch_shapes=[pltpu.VMEM((ts, D), F32)],
        compiler_params=_params(52, ("arbitrary",)),
    )(du, w, w_tail, h, g, dres, *after)


def _loss_head(h, g, tgt, ts=256):
    row = pl.BlockSpec((ts, D), lambda i: (i, 0))
    vec = pl.BlockSpec((1, D), lambda i: (0, 0))

    def body(h_ref, g_ref, t_ref, dh_ref, dg_ref, loss_ref):
        i = pl.program_id(0)

        def rows_fn(rows, carry):
            dg, part = carry
            x = h_ref[rows, :]
            gv = g_ref[...]
            err = _rms(x, gv) - t_ref[rows, :]
            part = part + 0.5 * jnp.sum(jnp.mean(err * err, axis=-1, keepdims=True), axis=0, keepdims=True)
            dx, dg_rows = _rms_bwd(err * (1.0 / D), x, gv)
            dh_ref[rows, :] = dx
            return dg + dg_rows, part
        dg, part = _row_loop(ts, rows_fn, (jnp.zeros((1, D), F32), jnp.zeros((1, 1), F32)))

        @pl.when(i == 0)
        def _():
            dg_ref[...] = dg
            loss_ref[...] = jnp.broadcast_to(part, loss_ref.shape)

        @pl.when(i > 0)
        def _():
            dg_ref[...] += dg
            loss_ref[...] += jnp.broadcast_to(part, loss_ref.shape)

    return _pallas(
        body, grid=(S // ts,), name="loss_head",
        in_specs=[row, vec, row], out_specs=[row, vec, pl.BlockSpec((1, 128), lambda i: (0, 0))],
        out_shape=[jax.ShapeDtypeStruct((S, D), F32), jax.ShapeDtypeStruct((1, D), F32),
                   jax.ShapeDtypeStruct((1, 128), F32)],
        compiler_params=_params(40, ("arbitrary",)),
    )(h, g, tgt)


def _pool_specs():
    blk = pl.BlockSpec((S, PG), lambda gi: (0, gi))
    wp = pl.BlockSpec((None, PG, PG), lambda gi: (gi, 0, 0))
    sc = pl.BlockSpec((1, PG), lambda gi: (0, gi))
    return blk, wp, sc


def _pool_fwd(u, wp_b, scale):
    blk, wp, sc = _pool_specs()

    def body(u_ref, wp_ref, sc_ref, y_ref, pooled_ref):
        win = 2 << pl.program_id(0)
        row = lax.broadcasted_iota(jnp.int32, (S, PG), 0)
        x = u_ref[...]
        s = x
        for k in (1, 2, 4, 8):
            s = s + jnp.where((row >= k) & (k < win), pltpu.roll(s, k, 0), 0.0)
        cnt = jnp.minimum(row + 1, win).astype(F32)
        pooled = (s / cnt - x).astype(BF16)
        pooled_ref[...] = pooled
        y_ref[...] = (_dot(pooled, wp_ref[...]) * sc_ref[...]).astype(BF16)

    return _pallas(
        body, grid=(len(POOL_WINDOWS),), name="pool_fwd", in_specs=[blk, wp, sc], out_specs=[blk, blk],
        out_shape=[jax.ShapeDtypeStruct((S, D_POOL), BF16), jax.ShapeDtypeStruct((S, D_POOL), BF16)],
        compiler_params=_params(40),
    )(u, wp_b, scale)


def _pool_bwd(dy, pooled, wp_b, scale):
    blk, wp, sc = _pool_specs()

    def body(dy_ref, p_ref, wp_ref, sc_ref, du_ref, dwp_ref, dsc_ref):
        win = 2 << pl.program_id(0)
        row = lax.broadcasted_iota(jnp.int32, (S, PG), 0)
        dyv = dy_ref[...]
        pooled = p_ref[...]
        w = wp_ref[...]
        dsc_ref[...] = jnp.sum(dyv * _dot(pooled, w), axis=0, keepdims=True)
        dz = (dyv * sc_ref[...]).astype(BF16)
        dwp_ref[...] = _dot_tn(pooled, dz)
        dpooled = _dot_nt(dz, w)
        cnt = jnp.minimum(row + 1, win).astype(F32)
        fs = dpooled / cnt
        for k in (1, 2, 4, 8):
            fs = fs + jnp.where((row < S - k) & (k < win), pltpu.roll(fs, S - k, 0), 0.0)
        du_ref[...] = (fs - dpooled).astype(BF16)

    return _pallas(
        body, grid=(len(POOL_WINDOWS),), name="pool_bwd", in_specs=[blk, blk, wp, sc], out_specs=[blk, wp, sc],
        out_shape=[jax.ShapeDtypeStruct((S, D_POOL), BF16), jax.ShapeDtypeStruct((len(POOL_WINDOWS), PG, PG), F32),
                   jax.ShapeDtypeStruct((1, D_POOL), F32)],
        compiler_params=_params(40),
    )(dy, pooled, wp_b, scale)


def _gla_in_specs(chunk_of):
    def at(width, col):
        return pl.BlockSpec((CH, width), lambda n: (chunk_of(n), col))
    return [at(H * DK, O_Q // (H * DK)), at(H * DK, O_K // (H * DK)), at(H * DV, O_V // (H * DV)),
            at(H * DV, O_G // (H * DV)), at(RANK_PAD, O_R // RANK_PAD)]


def _gla_decay_terms(lr_ref, wa_ref, ba_ref, q_ref, k_ref):
    row = lax.broadcasted_iota(jnp.int32, (CH, CH), 0)
    col = lax.broadcasted_iota(jnp.int32, (CH, CH), 1)
    tril = row >= col
    z = _dot(lr_ref[...].astype(BF16), wa_ref[...]) + ba_ref[...]
    la = _log_sigmoid(z) / GATE_NORM
    b = _tri_dot(jnp.where(tril, 1.0, 0.0).astype(BF16), la)
    bl = jnp.sum(la, axis=0, keepdims=True)
    e_b, e_nb, e_tb = jnp.exp(b), jnp.exp(-b), jnp.exp(bl - b)
    kk = k_ref[...]
    q_dec = (q_ref[...] * QK_SCALE) * e_b
    return tril, z, e_b, e_nb, e_tb, jnp.exp(bl), q_dec, kk * e_nb, kk * e_tb


def _gla_fwd(u, wa_b, ba, gn):
    wide = pl.BlockSpec((CH, H * DV), lambda n: (n, 0))

    def body(q_ref, k_ref, v_ref, g_ref, lr_ref, wa_ref, ba_ref, gn_ref, y_ref, o_ref, st_ref, state):
        @pl.when(pl.program_id(0) == 0)
        def _():
            state[...] = jnp.zeros_like(state)

        tril, _, _, _, _, dec, q_dec, k_inv, k_tail = _gla_decay_terms(lr_ref, wa_ref, ba_ref, q_ref, k_ref)
        for hd in range(H):
            ks, vs = slice(hd * DK, (hd + 1) * DK), slice(hd * DV, (hd + 1) * DV)
            qb, kib, ktb = q_dec[:, ks].astype(BF16), k_inv[:, ks].astype(BF16), k_tail[:, ks].astype(BF16)
            vb = v_ref[:, vs].astype(BF16)
            p = jnp.where(tril, _dot_nt(qb, kib), 0.0)
            st = state[hd]
            o = _dot(p.astype(BF16), vb) + _dot_nt(qb, st.astype(BF16))
            st_ref[hd] = st
            state[hd] = st * dec[:, ks] + _dot_tn(vb, ktb)
            o_ref[:, vs] = o
            on = _rms(o, gn_ref[...])
            gg = g_ref[:, vs]
            y_ref[:, vs] = (on * (gg * _sigmoid(gg))).astype(BF16)

    return _pallas(
        body, grid=(NCH,), name="gla_fwd",
        in_specs=_gla_in_specs(lambda n: n) + [pl.BlockSpec((RANK_PAD, H * DK), lambda n: (0, 0)),
                                               pl.BlockSpec((1, H * DK), lambda n: (0, 0)),
                                               pl.BlockSpec((1, DV), lambda n: (0, 0))],
        out_specs=[wide, wide, pl.BlockSpec((None, H, DV, DK), lambda n: (n, 0, 0, 0))],
        out_shape=[jax.ShapeDtypeStruct((S, H * DV), BF16), jax.ShapeDtypeStruct((S, H * DV), F32),
                   jax.ShapeDtypeStruct((NCH, H, DV, DK), F32)],
        scratch_shapes=[pltpu.VMEM((H, DV, DK), F32)],
        compiler_params=_params(32, ("arbitrary",)),
    )(u, u, u, u, u, wa_b, ba, gn)


GLA_DU = 2 * H * DK + 2 * H * DV + RANK_PAD


def _gla_bwd(u, o_arr, states, dy, wa_b, ba, gn, after=()):
    rev = lambda n: NCH - 1 - n
    wide = pl.BlockSpec((CH, H * DV), lambda n: (rev(n), 0))

    def body(q_ref, k_ref, v_ref, g_ref, lr_ref, o_ref, st_ref, dy_ref, wa_ref, ba_ref, gn_ref,
             du_ref, dwa_ref, dba_ref, dgn_ref, gstate, db_scr, dbl_scr):
        @pl.when(pl.program_id(0) == 0)
        def _():
            gstate[...] = jnp.zeros_like(gstate)
            dwa_ref[...] = jnp.zeros_like(dwa_ref)
            dba_ref[...] = jnp.zeros_like(dba_ref)
            dgn_ref[...] = jnp.zeros_like(dgn_ref)

        tril, z, e_b, e_nb, e_tb, dec, q_dec, k_inv, k_tail = _gla_decay_terms(lr_ref, wa_ref, ba_ref, q_ref, k_ref)
        gnv = gn_ref[...]
        dgn = jnp.zeros((1, DV), F32)
        for hd in range(H):
            ks, vs = slice(hd * DK, (hd + 1) * DK), slice(hd * DV, (hd + 1) * DV)
            qh, kih, kth = q_dec[:, ks], k_inv[:, ks], k_tail[:, ks]
            qb, kib, ktb = qh.astype(BF16), kih.astype(BF16), kth.astype(BF16)
            vb = v_ref[:, vs].astype(BF16)
            o = o_ref[:, vs]
            gg = g_ref[:, vs]
            dyh = dy_ref[:, vs]
            r = lax.rsqrt(jnp.mean(o * o, axis=-1, keepdims=True) + EPS)
            xh = o * r
            sg = _sigmoid(gg)
            dgate = dyh * (xh * gnv) * (sg * (1.0 + gg * (1.0 - sg)))
            don = dyh * (gg * sg)
            dgn = dgn + jnp.sum(don * xh, axis=0, keepdims=True)
            dxh = don * gnv
            d_o = (r * (dxh - xh * jnp.mean(dxh * xh, axis=-1, keepdims=True))).astype(BF16)
            pb = jnp.where(tril, _dot_nt(qb, kib), 0.0).astype(BF16)
            dpb = jnp.where(tril, _dot_nt(d_o, vb), 0.0).astype(BF16)
            gt = gstate[hd]
            gtb = gt.astype(BF16)
            st = st_ref[hd]
            dv = _dot_tn(pb, d_o) + _dot_nt(ktb, gtb)
            dq_dec = _dot(dpb, kib) + _dot(d_o, st.astype(BF16))
            dk_inv = _dot_tn(dpb, qb)
            dk_tail = _dot(vb, gtb)
            ddec = jnp.sum(gt * st, axis=0, keepdims=True)
            gstate[hd] = _dot_tn(d_o, qb) + dec[:, ks] * gt
            du_ref[:, ks] = (dq_dec * QK_SCALE * e_b[:, ks]).astype(BF16)
            du_ref[:, H * DK + hd * DK:H * DK + (hd + 1) * DK] = (dk_inv * e_nb[:, ks] + dk_tail * e_tb[:, ks]).astype(BF16)
            du_ref[:, 2 * H * DK + hd * DV:2 * H * DK + (hd + 1) * DV] = dv.astype(BF16)
            du_ref[:, 2 * H * DK + H * DV + hd * DV:2 * H * DK + H * DV + (hd + 1) * DV] = dgate.astype(BF16)
            db_scr[:, ks] = dq_dec * qh - dk_inv * kih - dk_tail * kth
            dbl_scr[:, ks] = jnp.sum(dk_tail * kth, axis=0, keepdims=True) + ddec * dec[:, ks]
        dgn_ref[...] += dgn
        row = lax.broadcasted_iota(jnp.int32, (CH, CH), 0)
        col = lax.broadcasted_iota(jnp.int32, (CH, CH), 1)
        dla = _tri_dot(jnp.where(row <= col, 1.0, 0.0).astype(BF16), db_scr[...]) + dbl_scr[...]
        dz = dla * (1.0 / GATE_NORM) * _sigmoid(-z)
        dzb = dz.astype(BF16)
        du_ref[:, GLA_DU - RANK_PAD:] = _dot_nt(dzb, wa_ref[...]).astype(BF16)
        dwa_ref[...] += _dot_tn(lr_ref[...].astype(BF16), dzb)
        dba_ref[...] += jnp.sum(dz, axis=0, keepdims=True)

    full = lambda shape: pl.BlockSpec(shape, lambda n: (0,) * len(shape))
    body, extra = _ordered_after(body, 11, after)
    return _pallas(
        body, grid=(NCH,), name="gla_bwd",
        in_specs=_gla_in_specs(rev) + [wide, pl.BlockSpec((None, H, DV, DK), lambda n: (rev(n), 0, 0, 0)),
                                       pl.BlockSpec((CH, H * DV), lambda n: (rev(n), 1)),
                                       full((RANK_PAD, H * DK)), full((1, H * DK)), full((1, DV))] + extra,
        out_specs=[pl.BlockSpec((CH, GLA_DU), lambda n: (rev(n), 0)), full((RANK_PAD, H * DK)), full((1, H * DK)),
                   full((1, DV))],
        out_shape=[jax.ShapeDtypeStruct((S, GLA_DU), BF16), jax.ShapeDtypeStruct((RANK_PAD, H * DK), F32),
                   jax.ShapeDtypeStruct((1, H * DK), F32), jax.ShapeDtypeStruct((1, DV), F32)],
        scratch_shapes=[pltpu.VMEM((H, DV, DK), F32), pltpu.VMEM((CH, H * DK), F32), pltpu.VMEM((1, H * DK), F32)],
        compiler_params=_params(32, ("arbitrary",)),
    )(u, u, u, u, u, o_arr, states, dy, wa_b, ba, gn, *after)


def _ffn_dw_in(n, dgu, tag, after=(), first_tile=0, ntiles=D // 512):
    return _tn_matmul(n, dgu, pl.BlockSpec((S, 512), lambda s, m: (0, first_tile + m)),
                      pl.BlockSpec((None, S, WIN_SHARD), lambda s, m: (s // (NDEV // 2), 0, s % (NDEV // 2))),
                      jax.ShapeDtypeStruct((512 * ntiles, 2 * FF), BF16), pl.BlockSpec((512, WIN_SHARD), lambda s, m: (m, s)),
                      (NDEV, ntiles), tag + "_dw_in", 32, after)


def _ffn_dw_out(act, df, tag, after=()):
    return _tn_matmul(act, df, pl.BlockSpec((S, 512), lambda m: (0, m)), pl.BlockSpec((S, D), lambda m: (0, 0)),
                      jax.ShapeDtypeStruct((FF, D), BF16), pl.BlockSpec((512, D), lambda m: (m, 0)),
                      (FF // 512,), tag + "_dw_out", 40, after)


def _fwd_ffn1(x, w, after=()):
    return _ffn_fwd(x, w["ffn1_norm"], w["ffn1_w_in"], w["ffn1_w_out"], "ffn1_fwd", after)


def _fwd_mixer(h1, w, after=(), at=lambda point, value: ()):
    u, n2 = _norm_matmul(h1, w["mix_norm"], w["w_in_mix"], w["w_in_mix_tail"], "mix_in", after)
    y_pool, pooled = _pool_fwd(u, w["w_pool"], w["pool_scale"])
    y_gla, o_gla, states = _gla_fwd(u, w["w_alpha"], w["b_alpha"], w["gla_norm"])
    y = jnp.concatenate([y_pool, y_gla], axis=1)
    h2 = _matmul_residual(y, w["w_out_mix"], h1, "mix_out", at("gla_fwd", y))
    return h2, dict(u=u, n2=n2, pooled=pooled, o_gla=o_gla, states=states, y=y)


def _fwd_ffn2_loss(h2, tgt, w, after=()):
    h3, n3, gu3 = _ffn_fwd(h2, w["ffn2_norm"], w["ffn2_w_in"], w["ffn2_w_out"], "ffn2_fwd", after)
    dh3, d_final, loss = _loss_head(h3, w["final_norm"], tgt)
    return dh3, d_final, loss, n3, gu3


def _backward(x, h1, h2, n1, gu1, sv, n3, gu3, dh3, w, at=lambda point, value: ()):
    g = {}
    dgu3, act3, df3, dh2, dh2b, g["ffn2_norm"] = _ffn_bwd_x(dh3, h2, w["ffn2_norm"], gu3, w["ffn2_w_in"], w["ffn2_w_out"], "ffn2_bwd")
    g["ffn2_w_out"] = _ffn_dw_out(act3, df3, "ffn2")
    g["ffn2_w_in"] = _ffn_dw_in(n3, dgu3, "ffn2", at("ffn2_w_out", g["ffn2_w_out"]))
    dy = _nt_matmul(dh2b, w["w_out_mix"], "mix_out_bwd", at("ffn2_w_in", g["ffn2_w_in"]))
    g["w_out_mix"] = _tn_matmul(sv["y"], dh2b, pl.BlockSpec((S, 512), lambda m: (0, m)), pl.BlockSpec((S, D), lambda m: (0, 0)),
                                jax.ShapeDtypeStruct((D, D), BF16), pl.BlockSpec((512, D), lambda m: (m, 0)), (D // 512,),
                                "mix_out_dw", 40)
    du_pool, g["w_pool"], g["pool_scale"] = _pool_bwd(dy, sv["pooled"], w["w_pool"], w["pool_scale"])
    du_gla, g["w_alpha"], g["b_alpha"], g["gla_norm"] = _gla_bwd(sv["u"], sv["o_gla"], sv["states"], dy, w["w_alpha"], w["b_alpha"],
                                                                 w["gla_norm"], at("pool_bwd", du_pool))
    du = jnp.concatenate([du_pool, du_gla], axis=1)
    g["w_in_mix"] = _tn_matmul(du, sv["n2"], pl.BlockSpec((S, 1408), lambda j, m: (0, j)), pl.BlockSpec((S, 512), lambda j, m: (0, m)),
                               jax.ShapeDtypeStruct((D_IN, D), BF16), pl.BlockSpec((1408, 512), lambda j, m: (j, m)),
                               (D_IN_PAD // 1408, D // 512), "mix_in_dw", 32)
    dh1, g["mix_norm"], df1 = _matmul_normbwd(du, w["w_in_mix"], w["w_in_mix_tail"], h1, w["mix_norm"], dh2, "mix_in_bwd",
                                              at("mixer_weights", g))
    dgu1, g["ffn1_w_out"] = _ffn_bwd_gu(df1, gu1, w["ffn1_w_out"], "ffn1_bwd_gu", at("dh1", dh1))
    half = S // FFN_TS // 2
    dx, dn_a = _ffn_bwd_dx(dgu1, dh1, x, w["ffn1_norm"], w["ffn1_w_in"], 0, half, "ffn1_bwd_dx_a",
                           at("ffn1_w_out", g["ffn1_w_out"]))
    g["ffn1_w_in_top"] = _ffn_dw_in(n1, dgu1, "ffn1_top", at("ffn1_dx_a", dx), 0, D // 1024)
    g["ffn1_w_in_bottom"] = _ffn_dw_in(n1, dgu1, "ffn1_bottom", at("ffn1_w_in_top", g["ffn1_w_in_top"]), D // 1024, D // 1024)
    dx, dn_b = _ffn_bwd_dx(dgu1, dh1, x, w["ffn1_norm"], w["ffn1_w_in"], half, half, "ffn1_bwd_dx_b",
                           at("ffn1_w_in_bottom", g["ffn1_w_in_bottom"]), into=dx)
    g["ffn1_norm"] = dn_a + dn_b
    return dx, g


def _local_step(x, tgt, w):
    h1, n1, gu1 = _fwd_ffn1(x, w)
    h2, sv = _fwd_mixer(h1, w)
    dh3, d_final, loss, n3, gu3 = _fwd_ffn2_loss(h2, tgt, w)
    dx, g = _backward(x, h1, h2, n1, gu1, sv, n3, gu3, dh3, w)
    return loss, dx, dict(final_norm=d_final, **g)


def _coords(p):
    return (p // 4, (p // 2) % 2, p % 2)


NCHIP = 4


def _place():
    return lax.axis_index("x"), lax.axis_index("y"), lax.axis_index("c")


def _rel_chip(x, y, rel):
    return ((1 - x) if rel & 1 else x, (1 - y) if rel & 2 else y)


def _dev_index(x, y, c):
    return 4 * x + 2 * y + c


def _cols(ref, p, width):
    return ref.at[:, pl.ds(pl.multiple_of(p * width, 128), width)]


def _sems(na, n):
    return [pltpu.SemaphoreType.DMA((na, n)), pltpu.SemaphoreType.DMA((na, n)), pltpu.SemaphoreType.DMA((na,))]


def _gather(items, name):
    arrays = [a for a, _ in items]
    kinds = [k for _, k in items]
    na = len(arrays)
    out_shape = [jax.ShapeDtypeStruct((NDEV,) + a.shape if k == "bcast" else (a.shape[0], NDEV * a.shape[1]), a.dtype)
                 for a, k in items]

    def body(*refs):
        ins, outs = refs[:na], refs[na:2 * na]
        send_sems, recv_sems, local_sems = refs[2 * na:]
        x, y, c = _place()
        sibling = (x, y, 1 - c)
        here, over_x, over_y, across = (x, y), (1 - x, y), (x, 1 - y), (1 - x, 1 - y)

        def half(ref, h):
            rows = ref.shape[0] // 2
            return ref.at[pl.ds(h * rows, rows), :]

        def slab(a, chip, core, h=None):
            ref = _slab(outs[a], kinds[a], _dev_index(*chip, core), ins[a].shape[1])
            return ref if h is None else half(ref, h)

        def copy(a, k, src, dst, to):
            return pltpu.make_async_remote_copy(src, dst, send_sems.at[a, k], recv_sems.at[a, k], device_id=to, device_id_type=MESH)

        sent = []

        def send(a, k, src, dst, to):
            sent.append(copy(a, k, src, dst, to))
            sent[-1].start()

        def arrived(a, k, chip, core, h=None):
            ref = slab(a, chip, core, h)
            copy(a, k, ref, ref, sibling).wait_recv()
            return ref

        local = [pltpu.make_async_copy(ins[a], slab(a, here, c), local_sems.at[a]) for a in range(na)]
        for cp in local:
            cp.start()
        for k, h, chip in ((1, 0, over_x), (4, 1, over_y), (2, 1, over_x), (5, 0, over_y)):
            for a in range(na):
                send(a, k, half(ins[a], h), slab(a, here, c, h), (*chip, c))
        for a in range(na):
            send(a, 0, ins[a], slab(a, here, c), sibling)
        for k, chip, h, onward, to, down in ((1, over_x, 0, 6, over_y, 7), (4, over_y, 1, 3, over_x, 10),
                                             (2, over_x, 1, None, None, 8), (5, over_y, 0, None, None, 9),
                                             (3, across, 1, None, None, 12), (6, across, 0, None, None, 11)):
            for a in range(na):
                ref = arrived(a, k, chip, c, h)
                if onward is not None:
                    send(a, onward, ref, ref, (*to, c))
                send(a, down, ref, ref, sibling)
        for a in range(na):
            arrived(a, 0, here, 1 - c)
        for k, chip, h in ((7, over_x, 0), (10, over_y, 1), (8, over_x, 1), (9, over_y, 0), (12, across, 1), (11, across, 0)):
            for a in range(na):
                arrived(a, k, chip, 1 - c, h)
        for cp in sent:
            cp.wait_send()
        for cp in local:
            cp.wait()

    return pl.pallas_call(body, name=name, in_specs=[ANY] * na, out_specs=[ANY] * na, out_shape=out_shape,
                          scratch_shapes=_sems(na, 13))(*arrays)


def _pair_add(own, kind, got, table, tr, name):
    _, rows, cols = got.shape
    if kind == "scatter":
        own_spec = pl.BlockSpec((None, tr, cols), lambda rel, i, t: (t[rel], i, 0))
    else:
        own_spec = pl.BlockSpec((tr, cols), lambda rel, i, t: (i, t[rel]))
    blk = pl.BlockSpec((None, tr, cols), lambda rel, i, t: (rel, i, 0))

    def body(t_ref, a_ref, b_ref, o_ref):
        o_ref[...] = (a_ref[...].astype(F32) + b_ref[...].astype(F32)).astype(o_ref.dtype)

    return _pallas(
        body, name=name, out_shape=jax.ShapeDtypeStruct(got.shape, got.dtype),
        grid_spec=pltpu.PrefetchScalarGridSpec(num_scalar_prefetch=1, grid=(NCHIP, rows // tr), in_specs=[own_spec, blk],
                                               out_specs=blk),
        compiler_params=_params(32),
    )(table, own, got)


HBM = pl.BlockSpec(memory_space=pltpu.HBM)
SEM = pl.BlockSpec(memory_space=pltpu.SEMAPHORE)
DATAFLOW = pltpu.SideEffectType.DATAFLOW_SIDE_EFFECTING


def _pair_copies(kinds):
    def describe(srcs, lands, send_sems, recv_sems):
        x, y, c = _place()
        na = len(srcs)
        for rel in range(NCHIP):
            p = _dev_index(*_rel_chip(x, y, rel), 1 - c)
            for a in range(na):
                src = srcs[a].at[p] if kinds[a] == "scatter" else _cols(srcs[a], p, srcs[a].shape[1] // NDEV)
                cp = pltpu.make_async_remote_copy(src, lands[a].at[rel], send_sems.at[rel * na + a], recv_sems.at[rel * na + a],
                                                  device_id=(x, y, 1 - c), device_id_type=MESH)
                yield cp, cp
    return describe


def _chip_copies(srcs, lands, send_sems, recv_sems):
    x, y, c = _place()
    na = len(srcs)
    for rel in range(1, NCHIP):
        for a in range(na):
            i = (rel - 1) * na + a
            cp = pltpu.make_async_remote_copy(srcs[a].at[rel], lands[a].at[rel], send_sems.at[i], recv_sems.at[i],
                                              device_id=(*_rel_chip(x, y, rel), c), device_id_type=MESH)
            yield cp, cp


def _slab(ref, kind, s, width):
    return _cols(ref, s, width) if kind == "bcast_cols" else ref.at[s]


def _all_copies(srcs, lands, send_sems, recv_sems):
    x, y, c = _place()
    na = len(srcs)
    me = _dev_index(x, y, c)
    for a in range(na):
        yield pltpu.make_async_copy(srcs[a], lands[a].at[me], send_sems.at[a]), None
    for k in range(1, NDEV):
        to, frm = (me + k) % NDEV, (me + NDEV - k) % NDEV
        for a in range(na):
            i = k * na + a
            send = pltpu.make_async_remote_copy(srcs[a], lands[a].at[me], send_sems.at[i], recv_sems.at[i],
                                                device_id=_coords(to), device_id_type=MESH)
            arrival = pltpu.make_async_remote_copy(srcs[a], lands[a].at[frm], send_sems.at[i], recv_sems.at[i],
                                                   device_id=_coords(to), device_id_type=MESH)
            yield send, arrival


def _gather_copies(kinds):
    def describe(srcs, lands, send_sems, recv_sems):
        x, y, c = _place()
        na = len(srcs)
        me = _dev_index(x, y, c)
        for a in range(na):
            yield pltpu.make_async_copy(srcs[a], _slab(lands[a], kinds[a], me, srcs[a].shape[-1]),
                                        send_sems.at[NCHIP * na + a]), None
        for rel in range(NCHIP):
            to = (x, y, 1 - c) if rel == 0 else (*_rel_chip(x, y, rel), c)
            for a in range(na):
                width = srcs[a].shape[-1]
                i = rel * na + a
                send = pltpu.make_async_remote_copy(srcs[a], _slab(lands[a], kinds[a], me, width), send_sems.at[i], recv_sems.at[i],
                                                    device_id=to, device_id_type=MESH)
                arrival = pltpu.make_async_remote_copy(srcs[a], _slab(lands[a], kinds[a], _dev_index(*to), width), send_sems.at[i],
                                                       recv_sems.at[i], device_id=to, device_id_type=MESH)
                yield send, arrival
    return describe


def _relay_copies(kinds, widths, phase):
    def describe(srcs, lands, send_sems, recv_sems):
        x, y, c = _place()
        na = len(lands)
        here, over_x, over_y, across = (x, y), (1 - x, y), (x, 1 - y), (1 - x, 1 - y)
        sibling = (x, y, 1 - c)

        def half(ref, h):
            rows = ref.shape[0] // 2
            return ref if h is None else ref.at[pl.ds(h * rows, rows), :]

        def slab(a, chip, core, h=None):
            return half(_slab(lands[a], kinds[a], _dev_index(*chip, core), widths[a]), h)

        def pair(a, k, src, dst, to, arrival):
            i = k * na + a
            return (pltpu.make_async_remote_copy(src, dst, send_sems.at[i], recv_sems.at[i], device_id=to, device_id_type=MESH),
                    pltpu.make_async_remote_copy(arrival, arrival, send_sems.at[i], recv_sems.at[i], device_id=to,
                                                 device_id_type=MESH))

        if phase == 1:
            for a in range(na):
                yield pltpu.make_async_copy(srcs[a], slab(a, here, c), send_sems.at[a]), None
            for k, h, chip in ((1, 0, over_x), (2, 1, over_y), (3, 1, over_x), (4, 0, over_y)):
                for a in range(na):
                    yield pair(a, k, half(srcs[a], h), slab(a, here, c, h), (*chip, c), slab(a, chip, c, h))
            for a in range(na):
                yield pair(a, 5, srcs[a], slab(a, here, c), sibling, slab(a, here, 1 - c))
        elif phase == 2:
            for a in range(na):
                yield pair(a, 0, slab(a, over_x, c, 0), slab(a, over_x, c, 0), (*over_y, c), slab(a, across, c, 0))
                yield pair(a, 1, slab(a, over_y, c, 1), slab(a, over_y, c, 1), (*over_x, c), slab(a, across, c, 1))
            for k, (chip, h) in enumerate(((over_x, 0), (over_y, 1), (over_x, 1), (over_y, 0)), start=2):
                for a in range(na):
                    yield pair(a, k, slab(a, chip, c, h), slab(a, chip, c, h), sibling, slab(a, chip, 1 - c, h))
        else:
            for h in (0, 1):
                for a in range(na):
                    yield pair(a, h, slab(a, across, c, h), slab(a, across, c, h), sibling, slab(a, across, 1 - c, h))
    return describe


RELAY_COPIES = {1: 6, 2: 6, 3: 2}


def _relay_finish(arrays, kinds, widths, name):
    na = len(arrays)
    describe = _relay_copies(kinds, widths, 3)

    def body(*refs):
        copies = list(describe((), refs[na:2 * na], refs[2 * na], refs[2 * na + 1]))
        for send, _ in copies:
            send.start()
        for send, arrival in copies:
            send.wait_send()
            arrival.wait_recv()

    return pl.pallas_call(body, name=name, in_specs=[ANY] * na, out_specs=[ANY] * na,
                          out_shape=[jax.ShapeDtypeStruct(a.shape, a.dtype) for a in arrays],
                          input_output_aliases={i: i for i in range(na)},
                          scratch_shapes=[pltpu.SemaphoreType.DMA((RELAY_COPIES[3] * na,))] * 2)(*arrays)


def _pass_to_sibling(arrays, kinds, widths, name):
    na = len(arrays)

    def body(*refs):
        bufs = refs[na:2 * na]
        send_sems, recv_sems = refs[2 * na:]
        x, y, c = _place()
        copies = []
        for rel in range(1, NCHIP):
            for a in range(na):
                mine = _slab(bufs[a], kinds[a], _dev_index(*_rel_chip(x, y, rel), c), widths[a])
                theirs = _slab(bufs[a], kinds[a], _dev_index(*_rel_chip(x, y, rel), 1 - c), widths[a])
                send = pltpu.make_async_remote_copy(mine, mine, send_sems.at[a, rel], recv_sems.at[a, rel],
                                                    device_id=(x, y, 1 - c), device_id_type=MESH)
                send.start()
                copies.append((send, pltpu.make_async_remote_copy(theirs, theirs, send_sems.at[a, rel], recv_sems.at[a, rel],
                                                                  device_id=(x, y, 1 - c), device_id_type=MESH)))
        for send, arrival in copies:
            send.wait_send()
            arrival.wait_recv()

    return pl.pallas_call(body, name=name, in_specs=[ANY] * na, out_specs=[ANY] * na,
                          out_shape=[jax.ShapeDtypeStruct(a.shape, a.dtype) for a in arrays],
                          input_output_aliases={i: i for i in range(na)}, scratch_shapes=_sems(na, NCHIP)[:2])(*arrays)


def _start_copies(name, srcs, lands, describe, ncopies, after=()):
    arrays = list(srcs) + list(lands)
    ns, n, nin = len(srcs), len(arrays), len(arrays) + len(after)

    def body(*refs):
        for send, _ in describe(refs[:ns], refs[ns:n], refs[nin], refs[nin + 1]):
            send.start()
        refs[-1][...] = jnp.zeros_like(refs[-1])

    out = pl.pallas_call(
        body, name=name,
        out_shape=(pltpu.SemaphoreType.DMA((ncopies,)), pltpu.SemaphoreType.DMA((ncopies,)),
                   *[pltpu.HBM(a.shape, a.dtype) for a in arrays], jax.ShapeDtypeStruct((8, 128), F32)),
        in_specs=[HBM] * n + [ANY] * len(after), out_specs=(SEM, SEM, *[HBM] * n, pl.BlockSpec(memory_space=pltpu.VMEM)),
        input_output_aliases={i: 2 + i for i in range(n)},
        compiler_params=pltpu.CompilerParams(has_side_effects=DATAFLOW),
    )(*[pltpu.with_memory_space_constraint(a, pltpu.HBM) for a in arrays], *after)
    return out[0], out[1], list(out[2:2 + n]), out[-1]


def _wait_copies(name, send_sems, recv_sems, thru, ns, describe, after):
    n = len(thru)

    def body(*refs):
        for send, arrival in describe(refs[:ns], refs[ns:n], refs[n], refs[n + 1]):
            if arrival is None:
                send.wait()
            else:
                send.wait_send()
                arrival.wait_recv()

    out = pl.pallas_call(
        body, name=name, out_shape=tuple(pltpu.HBM(a.shape, a.dtype) for a in thru),
        in_specs=[HBM] * n + [SEM, SEM] + [ANY] * len(after), out_specs=tuple([HBM] * n),
        input_output_aliases={i: i for i in range(n)},
        compiler_params=pltpu.CompilerParams(has_side_effects=DATAFLOW),
    )(*thru, send_sems, recv_sems, *after)
    return list(out[:ns]), list(out[ns:])


def _adamw(parts, w, m, v, tr, name, tc=None, first_row=0, into=None):
    rows, cols = w.shape
    part_rows = parts[0][0].shape[1]
    tc = cols if tc is None else tc
    nparts = len(parts)
    tile0 = first_row // tr
    blk = pl.BlockSpec((tr, tc), lambda i, j: (tile0 + i, j))
    carried = [] if into is None else list(into)

    def slab_spec(s):
        return pl.BlockSpec((None, tr, tc), lambda i, j: (s, i, j))

    def body(*refs):
        p_refs = refs[:nparts]
        w_ref, m_ref, v_ref = refs[nparts:nparts + 3]
        g_ref, d_ref, nm_ref, nv_ref = refs[nparts + 3 + len(carried):]
        g = p_refs[0][...].astype(F32)
        for p_ref in p_refs[1:]:
            g = g + p_ref[...].astype(F32)
        nm = ADAM_B1 * m_ref[...] + (1.0 - ADAM_B1) * g
        nv = ADAM_B2 * v_ref[...] + (1.0 - ADAM_B2) * (g * g)
        m_hat = nm / (1.0 - ADAM_B1 ** ADAM_STEP)
        v_hat = nv / (1.0 - ADAM_B2 ** ADAM_STEP)
        g_ref[...] = g
        d_ref[...] = -ADAM_LR * (m_hat / (jnp.sqrt(v_hat) + ADAM_EPS) + ADAM_WD * w_ref[...])
        nm_ref[...] = nm
        nv_ref[...] = nv

    return _pallas(
        body, grid=(part_rows // tr, cols // tc), name=name,
        in_specs=[slab_spec(s) for _, s in parts] + [blk, blk, blk] + [ANY] * len(carried), out_specs=[blk] * 4,
        out_shape=[jax.ShapeDtypeStruct((rows, cols), F32)] * 4,
        input_output_aliases={nparts + 3 + k: k for k in range(len(carried))},
        compiler_params=_params(40),
    )(*[a for a, _ in parts], w, m, v, *carried)


def _pack_small(vals, extra=None):
    flat = [vals[n].reshape(-1).astype(F32) for n, _ in SMALL]
    tail = jnp.zeros((SMALL_ROWS * 128 - LOSS_AT,), F32)
    if extra is not None:
        tail = tail.at[0].set(extra)
    return jnp.concatenate(flat + [tail]).reshape(SMALL_ROWS, 128)


def _unpack_small(packed, like):
    flat, out, at = packed.reshape(-1), {}, 0
    for n, size in SMALL:
        out[n] = flat[at:at + size].reshape(like[n].shape)
        at += size
    return out, flat[LOSS_AT]


def kernel(x, ffn1_norm, ffn1_w_in, ffn1_w_out, mix_norm, w_in_mix, w_pool, pool_scale, w_alpha, b_alpha, gla_norm, w_out_mix, ffn2_norm, ffn2_w_in, ffn2_w_out, final_norm, loss_target, m_ffn1_norm, m_ffn1_w_in, m_ffn1_w_out, m_mix_norm, m_w_in_mix, m_w_pool, m_pool_scale, m_w_alpha, m_b_alpha, m_gla_norm, m_w_out_mix, m_ffn2_norm, m_ffn2_w_in, m_ffn2_w_out, m_final_norm, v_ffn1_norm, v_ffn1_w_in, v_ffn1_w_out, v_mix_norm, v_w_in_mix, v_w_pool, v_pool_scale, v_w_alpha, v_b_alpha, v_gla_norm, v_w_out_mix, v_ffn2_norm, v_ffn2_w_in, v_ffn2_w_out, v_final_norm):
    names = ["ffn1_norm", "ffn1_w_in", "ffn1_w_out", "mix_norm", "w_in_mix", "w_pool", "pool_scale", "w_alpha", "b_alpha",
             "gla_norm", "w_out_mix", "ffn2_norm", "ffn2_w_in", "ffn2_w_out", "final_norm"]
    p = dict(zip(names, [ffn1_norm, ffn1_w_in, ffn1_w_out, mix_norm, w_in_mix, w_pool, pool_scale, w_alpha, b_alpha,
                         gla_norm, w_out_mix, ffn2_norm, ffn2_w_in, ffn2_w_out, final_norm]))
    m = dict(zip(names, [m_ffn1_norm, m_ffn1_w_in, m_ffn1_w_out, m_mix_norm, m_w_in_mix, m_w_pool, m_pool_scale, m_w_alpha,
                         m_b_alpha, m_gla_norm, m_w_out_mix, m_ffn2_norm, m_ffn2_w_in, m_ffn2_w_out, m_final_norm]))
    v = dict(zip(names, [v_ffn1_norm, v_ffn1_w_in, v_ffn1_w_out, v_mix_norm, v_w_in_mix, v_w_pool, v_pool_scale, v_w_alpha,
                         v_b_alpha, v_gla_norm, v_w_out_mix, v_ffn2_norm, v_ffn2_w_in, v_ffn2_w_out, v_final_norm]))

    mx, my, mc = _place()
    table = jnp.stack([_dev_index(*_rel_chip(mx, my, rel), mc) for rel in range(NCHIP)]).astype(jnp.int32)

    def landing(shard, kind):
        shape = (shard.shape[0], NDEV * shard.shape[1]) if kind == "bcast_cols" else (NDEV,) + shard.shape
        return lax.empty(shape, shard.dtype)

    def gather_begin(items, tag, after):
        kinds = [kind for _, kind in items]
        copies = _gather_copies(kinds)
        s, r, thru, tok = _start_copies(tag + "_start", [a for a, _ in items], [landing(a, kind) for a, kind in items], copies,
                                        (NCHIP + 1) * len(items), after)
        return (s, r, thru, copies, kinds, [a.shape[-1] for a, _ in items], tag), tok

    def gather_end(state, after):
        s, r, thru, copies, kinds, widths, tag = state
        _, lands = _wait_copies(tag + "_wait", s, r, thru, len(kinds), copies, after)
        return _pass_to_sibling(lands, kinds, widths, tag + "_pass")

    def shard16(n):
        return p[n][0].astype(BF16)

    g_w1in, g_w1out = _gather([(shard16("ffn1_w_in"), "bcast_cols"), (shard16("ffn1_w_out"), "bcast")], "gather_ffn1")
    mix_state, tok_m = gather_begin([(jnp.transpose(p["w_in_mix"][0]).astype(BF16), "bcast"), (shard16("w_out_mix"), "bcast"),
                                     (p["w_pool"][0].reshape(H * 32, PG), "bcast"), (p["w_alpha"][0], "bcast")], "gather_mix",
                                    (g_w1out,))
    ffn2_items = [(shard16("ffn2_w_in"), "bcast_cols"), (shard16("ffn2_w_out"), "bcast")]
    ffn2_kinds = [kind for _, kind in ffn2_items]
    ffn2_widths = [a.shape[-1] for a, _ in ffn2_items]
    relay = {ph: _relay_copies(ffn2_kinds, ffn2_widths, ph) for ph in (1, 2)}
    f_s, f_r, f_thru, tok_f = _start_copies("gather_ffn2_start", [a for a, _ in ffn2_items],
                                            [landing(a, kind) for a, kind in ffn2_items], relay[1],
                                            RELAY_COPIES[1] * len(ffn2_items), (tok_m,))
    ffn2_lands = {}

    def relay_ffn2(point, value):
        _, lands = _wait_copies("gather_ffn2_wait", f_s, f_r, f_thru, len(ffn2_items), relay[1], (value,))
        ffn2_lands["s"], ffn2_lands["r"], ffn2_lands["thru"], tok = _start_copies(
            "gather_ffn2_relay_start", [], lands, relay[2], RELAY_COPIES[2] * len(ffn2_items))
        return (tok,)

    full ={"ffn1_w_in": g_w1in, "ffn1_w_out": g_w1out.reshape(FF, D), "final_norm": final_norm.reshape(1, D)}
    for n in ("ffn1_norm", "mix_norm", "ffn2_norm", "pool_scale", "b_alpha", "gla_norm"):
        full[n] = p[n]

    xs, tgt = x[0], loss_target[0]
    h1, n1, gu1 = _fwd_ffn1(xs, full, after=(tok_m, tok_f))
    g_wmix, g_wo, g_wpool, g_walpha = gather_end(mix_state, (h1,))
    walpha = jnp.transpose(g_walpha, (1, 0, 2)).reshape(RANK, H * DK)
    full.update({
        "w_in_mix": g_wmix.reshape(D_IN, D),
        "w_in_mix_tail": jnp.pad(g_wmix[NDEV - 1, MIX_SHARD - RANK:], ((0, RANK_PAD - RANK), (0, 0))),
        "w_out_mix": g_wo.reshape(D, D),
        "w_pool": jnp.transpose(g_wpool.reshape(NDEV, H, 32, PG), (1, 0, 2, 3)).reshape(H, PG, PG).astype(BF16),
        "w_alpha": jnp.pad(walpha, ((0, RANK_PAD - RANK), (0, 0))).astype(BF16),
    })
    h2, sv = _fwd_mixer(h1, full, at=relay_ffn2)
    _, lands = _wait_copies("gather_ffn2_relay_wait", ffn2_lands["s"], ffn2_lands["r"], ffn2_lands["thru"], 0, relay[2], (h2,))
    g_w2in, g_w2out = _relay_finish(lands, ffn2_kinds, ffn2_widths, "gather_ffn2_finish")
    full.update({"ffn2_w_in": g_w2in, "ffn2_w_out": g_w2out.reshape(FF, D)})
    dh3, d_final, loss_part, n3, gu3 = _fwd_ffn2_loss(h2, tgt, full)


    def slab_shape(a, kind):
        return (NCHIP,) + (a.shape[1:] if kind == "scatter" else (a.shape[0], a.shape[1] // NDEV))

    def pair_add_all(own, got, tag):
        return [_pair_add(a, kind, got_a, table, tr, "%s_pair_add_%d" % (tag, i))
                for i, ((a, kind, tr), got_a) in enumerate(zip(own, got))]

    def reduce_begin(own, tag, after=()):
        kinds = [kind for _, kind, _ in own]
        copies = _pair_copies(kinds)
        s, r, thru, tok = _start_copies(tag + "_pair_start", [a for a, _, _ in own],
                                        [lax.empty(slab_shape(a, kind), a.dtype) for a, kind, _ in own], copies,
                                        NCHIP * len(own), after)
        return dict(own=own, copies=copies, s=s, r=r, thru=thru, tag=tag), tok

    def reduce_middle(st, after):
        own, tag = st["own"], st["tag"]
        sent, got = _wait_copies(tag + "_pair_wait", st["s"], st["r"], st["thru"], len(own), st["copies"], after)
        pre = pair_add_all([(a, kind, tr) for a, (_, kind, tr) in zip(sent, own)], got, tag)
        st["s"], st["r"], st["thru"], tok = _start_copies(tag + "_chip_start", pre, [lax.empty(a.shape, a.dtype) for a in pre],
                                                          _chip_copies, (NCHIP - 1) * len(pre))
        return tok

    def reduce_end(st, after):
        n = len(st["own"])
        pre, land = _wait_copies(st["tag"] + "_chip_wait", st["s"], st["r"], st["thru"], n, _chip_copies, after)
        return [[(a, 0)] + [(b, rel) for rel in range(1, NCHIP)] for a, b in zip(pre, land)]

    def w_in_item(a):
        return (a, "scatter_cols", 512)

    def w_out_item(a):
        return (a.reshape(NDEV, WOUT_SHARD, D), "scatter", WOUT_SHARD)

    red, small = {}, {}

    def at(point, value):
        if point == "ffn2_w_out":
            red["w2out"], tok = reduce_begin([w_out_item(value)], "ffn2_w_out")
        elif point == "ffn2_w_in":
            tok_a = reduce_middle(red["w2out"], (value,))
            red["w2in"], tok = reduce_begin([w_in_item(value)], "ffn2_w_in", (tok_a,))
        elif point == "pool_bwd":
            tok = reduce_middle(red["w2in"], (value,))
        elif point == "mixer_weights":
            d_wmix8 = value["w_in_mix"].reshape(NDEV, MIX_SHARD, D)
            d_wpool8 = jnp.transpose(value["w_pool"].reshape(H, NDEV, 32, PG), (1, 0, 2, 3)).reshape(NDEV, H * 32, PG)
            d_walpha8 = jnp.transpose(value["w_alpha"][:RANK].reshape(RANK, NDEV, H * DK // NDEV), (1, 0, 2))
            red["mix"], tok = reduce_begin([(d_wmix8, "scatter", MIX_SHARD),
                                            (value["w_out_mix"].reshape(NDEV, D // NDEV, D), "scatter", D // NDEV),
                                            (d_wpool8, "scatter", H * 32), (d_walpha8, "scatter", RANK)], "mix")
        elif point == "dh1":
            tok = reduce_middle(red["mix"], (value,))
        elif point == "ffn1_w_out":
            red["w1out"], tok = reduce_begin([w_out_item(value)], "ffn1_w_out")
        elif point == "ffn1_dx_a":
            tok = reduce_middle(red["w1out"], (value,))
        elif point == "ffn1_w_in_top":
            red["w1in_top"], tok = reduce_begin([w_in_item(value)], "ffn1_w_in_top")
        elif point == "ffn1_w_in_bottom":
            tok_a = reduce_middle(red["w1in_top"], (value,))
            red["w1in_bottom"], tok = reduce_begin([w_in_item(value)], "ffn1_w_in_bottom", (tok_a,))
        return (tok,)

    dx, g = _backward(xs, h1, h2, n1, gu1, sv, n3, gu3, dh3, full, at)
    packed = _pack_small(dict(final_norm=d_final, **g), loss_part[0, 0])
    small_s, small_r, small_thru, tok_s = _start_copies("gather_small_start", [packed], [lax.empty((NDEV,) + packed.shape, F32)],
                                                        _all_copies, NDEV, (dx,))
    tok_c = reduce_middle(red["w1in_bottom"], (tok_s,))

    def upd(parts, n, shape2d, tr):
        res = _adamw(parts, p[n].reshape(shape2d), m[n].reshape(shape2d), v[n].reshape(shape2d), tr, "adamw_" + n)
        return [r.reshape(p[n].shape) for r in res]

    def transposed(a):
        return jnp.transpose(a[0])

    (p_w2out,) = reduce_end(red["w2out"], (tok_c,))
    (p_w2in,) = reduce_end(red["w2in"], (tok_c,))
    p_wmix, p_wo, p_wpool, p_walpha = reduce_end(red["mix"], (tok_c,))
    out = {
        "ffn2_w_in": upd(p_w2in, "ffn2_w_in", (D, WIN_SHARD), 128),
        "ffn2_w_out": upd(p_w2out, "ffn2_w_out", (WOUT_SHARD, D), 64),
        "w_out_mix": upd(p_wo, "w_out_mix", (D // NDEV, D), 64),
        "w_pool": upd(p_wpool, "w_pool", (H * 32, PG), H * 32),
        "w_alpha": upd(p_walpha, "w_alpha", (RANK, H * DK // NDEV), RANK),
    }
    out["w_in_mix"] = [jnp.transpose(r)[None] for r in
                       _adamw(p_wmix, transposed(p["w_in_mix"]), transposed(m["w_in_mix"]), transposed(v["w_in_mix"]),
                              MIX_SHARD, "adamw_w_in_mix", tc=512)]
    _, (r_small,) = _wait_copies("gather_small_wait", small_s, small_r, small_thru, 1, _all_copies,
                                 (out["w_in_mix"][3], out["ffn2_w_in"][3], out["ffn2_w_out"][3], out["w_out_mix"][3]))
    small_res = _adamw([(r_small, s) for s in range(NDEV)], _pack_small(p), _pack_small(m), _pack_small(v), SMALL_ROWS,
                       "adamw_small")
    (p_w1out,) = reduce_end(red["w1out"], (small_res[0],))
    out["ffn1_w_out"] = upd(p_w1out, "ffn1_w_out", (WOUT_SHARD, D), 64)
    w1in = [a.reshape(D, WIN_SHARD) for a in (p["ffn1_w_in"], m["ffn1_w_in"], v["ffn1_w_in"])]
    (p_top,) = reduce_end(red["w1in_top"], (out["ffn1_w_out"][3],))
    top = _adamw(p_top, *w1in, 128, "adamw_ffn1_w_in_top")
    (p_bottom,) = reduce_end(red["w1in_bottom"], (top[3],))
    out["ffn1_w_in"] = [r.reshape(p["ffn1_w_in"].shape) for r in
                        _adamw(p_bottom, *w1in, 128, "adamw_ffn1_w_in_bottom", first_row=D // 2, into=top)]
    unpacked = [_unpack_small(r, p) for r in small_res]
    loss = unpacked[0][1]
    for n, _ in SMALL:
        out[n] = [u[0][n] for u in unpacked]

    return (loss, dx.reshape(1, S, D), *[out[n][0] for n in names], *[out[n][1] for n in names],
            *[out[n][2] for n in names], *[out[n][3] for n in names])
```

```python
import jax
import jax.numpy as jnp
from jax import lax
from jax.experimental import pallas as pl
from jax.experimental.pallas import tpu as pltpu

F32, BF16 = jnp.float32, jnp.bfloat16
MESH = pl.DeviceIdType.MESH
ANY = pl.BlockSpec(memory_space=pl.ANY)

NDEV = 8
S = 2048
D = 2048
FF = 5632
WIN_SHARD = 2 * FF // NDEV
WOUT_SHARD = FF // NDEV
D_POOL = 1024
PG = 256
POOL_WINDOWS = (2, 4, 8, 16)
H = 4
DK = 128
DV = 256
CH = 64
NCH = S // CH
RANK = 16
RANK_PAD = 128
D_IN = 4112
D_IN_PAD = 4224
MIX_SHARD = D_IN // NDEV
O_Q, O_K, O_V, O_G, O_R = 1024, 1536, 2048, 3072, 4096
GATE_NORM = 16.0
QK_SCALE = DK ** -0.5
EPS = 1e-6
ADAM_LR, ADAM_B1, ADAM_B2, ADAM_EPS, ADAM_WD, ADAM_STEP = 0.001, 0.9, 0.999, 1e-08, 0.01, 10
V7X_VMEM_BYTES = 64 << 20

SMALL = (("ffn1_norm", 2048), ("mix_norm", 2048), ("ffn2_norm", 2048), ("final_norm", 2048),
         ("pool_scale", 1024), ("b_alpha", 512), ("gla_norm", 256))
SMALL_ROWS = 80
LOSS_AT = sum(n for _, n in SMALL)


def _params(vmem_mb, sem=None):
    return pltpu.CompilerParams(dimension_semantics=sem, vmem_limit_bytes=min(vmem_mb << 20, V7X_VMEM_BYTES - (4 << 20)))


def _pallas(body, **kwargs):
    call = pl.pallas_call(body, **kwargs)

    def run(*operands):
        return call(*[pltpu.with_memory_space_constraint(a, pltpu.HBM) if a.size * a.dtype.itemsize >= 1 << 18 else a
                      for a in operands])
    return run


def _dot(a, b):
    return jnp.dot(a, b, preferred_element_type=F32)


def _dot_nt(a, b):
    return lax.dot_general(a, b, (((1,), (1,)), ((), ())), preferred_element_type=F32)


def _dot_tn(a, b):
    return lax.dot_general(a, b, (((0,), (0,)), ((), ())), preferred_element_type=F32)


def _sigmoid(x):
    return 0.5 * jnp.tanh(0.5 * x) + 0.5


def _log_sigmoid(x):
    return jnp.minimum(x, 0.0) - jnp.log(1.0 + jnp.exp(-jnp.abs(x)))


def _rms(x, g):
    r = lax.rsqrt(jnp.mean(x * x, axis=-1, keepdims=True) + EPS)
    return x * r * g


def _rms_bwd(dn, x, g):
    r = lax.rsqrt(jnp.mean(x * x, axis=-1, keepdims=True) + EPS)
    xh = x * r
    dxh = dn * g
    dx = r * (dxh - xh * jnp.mean(dxh * xh, axis=-1, keepdims=True))
    return dx, jnp.sum(dn * xh, axis=0, keepdims=True)


ROWS = 64


def _row_loop(total, fn, init=0):
    def step(t, carry):
        return fn(pl.ds(pl.multiple_of(t * ROWS, ROWS), ROWS), carry)
    return lax.fori_loop(0, total // ROWS, step, init)


def _split3(x):
    hi = x.astype(BF16)
    r1 = x - hi.astype(F32)
    mid = r1.astype(BF16)
    lo = (r1 - mid.astype(F32)).astype(BF16)
    return hi, mid, lo


def _tri_dot(tri_b, x):
    hi, mid, lo = _split3(x)
    return (_dot(tri_b, lo) + _dot(tri_b, mid)) + _dot(tri_b, hi)


FFN_TS, FFN_TF = 512, 512


def _ffn_specs():
    wg = pl.BlockSpec((D, FFN_TF), lambda i, j: (0, j))
    wu = pl.BlockSpec((D, FFN_TF), lambda i, j: (0, FF // FFN_TF + j))
    wo = pl.BlockSpec((FFN_TF, D), lambda i, j: (j, 0))
    row = pl.BlockSpec((FFN_TS, D), lambda i, j: (i, 0))
    vec = pl.BlockSpec((1, D), lambda i, j: (0, 0))
    gu = pl.BlockSpec((2, FFN_TS, FFN_TF), lambda i, j: (0, i, j))
    return wg, wu, wo, row, vec, gu


def _ordered_after(body, n_in, after):
    def wrapped(*refs):
        return body(*refs[:n_in], *refs[n_in + len(after):])
    return wrapped, [ANY] * len(after)


def _ffn_fwd(h, g, w_in8, w_out, name, after=(), head=None):
    nj = FF // FFN_TF
    wg, wu, wo, row, vec, gu = _ffn_specs()
    n_in = 5 if head is None else 7

    def body(*refs):
        h_ref, g_ref, wg_ref, wu_ref, wo_ref = refs[:5]
        ho_ref, n_ref, gu_ref = refs[n_in:n_in + 3]
        acc_ref = refs[-1]
        i, j = pl.program_id(0), pl.program_id(1)

        @pl.when(j == 0)
        def _():
            def norm(rows, c):
                n_ref[rows, :] = _rms(h_ref[rows, :], g_ref[...]).astype(BF16)
                return c
            _row_loop(FFN_TS, norm)
            acc_ref[...] = jnp.zeros_like(acc_ref)

        n = n_ref[...]
        gate = _dot(n, wg_ref[...])
        up = _dot(n, wu_ref[...])
        gu_ref[0] = gate.astype(BF16)
        gu_ref[1] = up.astype(BF16)
        a = (gate * _sigmoid(gate)) * up
        acc_ref[...] += _dot(a.astype(BF16), wo_ref[...])

        if head is None:
            @pl.when(j == nj - 1)
            def _():
                def residual(rows, c):
                    ho_ref[rows, :] = h_ref[rows, :] + 0.5 * acc_ref[rows, :]
                    return c
                _row_loop(FFN_TS, residual)
        else:
            gf_ref, t_ref = refs[5:7]
            dgf_ref, loss_ref = refs[n_in + 3:n_in + 5]

            @pl.when(j == nj - 1)
            def _():
                def rows_fn(rows, carry):
                    dg, part = carry
                    x = h_ref[rows, :] + 0.5 * acc_ref[rows, :]
                    gv = gf_ref[...]
                    err = _rms(x, gv) - t_ref[rows, :]
                    part = part + 0.5 * jnp.sum(jnp.mean(err * err, axis=-1, keepdims=True), axis=0, keepdims=True)
                    dx, dg_rows = _rms_bwd(err * (1.0 / D), x, gv)
                    ho_ref[rows, :] = dx
                    return dg + dg_rows, part
                dg, part = _row_loop(FFN_TS, rows_fn, (jnp.zeros((1, D), F32), jnp.zeros((1, 1), F32)))

                @pl.when(i == 0)
                def _():
                    dgf_ref[...] = dg
                    loss_ref[...] = jnp.broadcast_to(part, loss_ref.shape)

                @pl.when(i > 0)
                def _():
                    dgf_ref[...] += dg
                    loss_ref[...] += jnp.broadcast_to(part, loss_ref.shape)

    body, extra = _ordered_after(body, n_in, after)
    head_in = [] if head is None else [vec, row]
    head_out = [] if head is None else [vec, pl.BlockSpec((1, 128), lambda i, j: (0, 0))]
    head_shape = [] if head is None else [jax.ShapeDtypeStruct((1, D), F32), jax.ShapeDtypeStruct((1, 128), F32)]
    return _pallas(
        body, grid=(S // FFN_TS, nj), name=name,
        in_specs=[row, vec, wg, wu, wo] + head_in + extra, out_specs=[row, row, gu] + head_out,
        out_shape=[jax.ShapeDtypeStruct((S, D), F32), jax.ShapeDtypeStruct((S, D), BF16),
                   jax.ShapeDtypeStruct((2, S, FF), BF16)] + head_shape,
        scratch_shapes=[pltpu.VMEM((FFN_TS, D), F32)],
        compiler_params=_params(58, ("arbitrary", "arbitrary")),
    )(h, g, w_in8, w_in8, w_out, *(head or ()), *after)


def _ffn_bwd_x(dhp, h, g, gu_arr, w_in8, w_out, name, after=()):
    ni, nj = S // FFN_TS, FF // FFN_TF
    wg, wu, wo, row, vec, gu = _ffn_specs()
    act = pl.BlockSpec((FFN_TS, FFN_TF), lambda i, j: (i, j))

    def body(dhp_ref, h_ref, g_ref, gu_ref, wg_ref, wu_ref, wo_ref,
             dgu_ref, a_ref, df_ref, dh_ref, dhb_ref, dg_ref, acc_ref):
        i, j = pl.program_id(0), pl.program_id(1)

        @pl.when(j == 0)
        def _():
            def half(rows, c):
                df_ref[rows, :] = (0.5 * dhp_ref[rows, :]).astype(BF16)
                return c
            _row_loop(FFN_TS, half)
            acc_ref[...] = jnp.zeros_like(acc_ref)

        gate = gu_ref[0].astype(F32)
        up = gu_ref[1].astype(F32)
        da = _dot_nt(df_ref[...], wo_ref[...])
        sg = _sigmoid(gate)
        silu = gate * sg
        dgate = (da * up * (sg * (1.0 + gate * (1.0 - sg)))).astype(BF16)
        dup = (da * silu).astype(BF16)
        a_ref[...] = (silu * up).astype(BF16)
        dgu_ref[0] = dgate
        dgu_ref[1] = dup
        acc_ref[...] += _dot_nt(dgate, wg_ref[...]) + _dot_nt(dup, wu_ref[...])

        @pl.when(j == nj - 1)
        def _():
            def norm_bwd(rows, dg):
                dx, dg_rows = _rms_bwd(acc_ref[rows, :], h_ref[rows, :], g_ref[...])
                dh = dhp_ref[rows, :] + dx
                dh_ref[rows, :] = dh
                dhb_ref[rows, :] = dh.astype(BF16)
                return dg + dg_rows
            dg = _row_loop(FFN_TS, norm_bwd, jnp.zeros((1, D), F32))

            @pl.when(i == 0)
            def _():
                dg_ref[...] = dg

            @pl.when(i > 0)
            def _():
                dg_ref[...] += dg

    body, extra = _ordered_after(body, 7, after)
    return _pallas(
        body, grid=(ni, nj), name=name,
        in_specs=[row, row, vec, gu, wg, wu, wo] + extra,
        out_specs=[gu, act, row, row, row, vec],
        out_shape=[jax.ShapeDtypeStruct((2, S, FF), BF16), jax.ShapeDtypeStruct((S, FF), BF16),
                   jax.ShapeDtypeStruct((S, D), BF16), jax.ShapeDtypeStruct((S, D), F32),
                   jax.ShapeDtypeStruct((S, D), BF16), jax.ShapeDtypeStruct((1, D), F32)],
        scratch_shapes=[pltpu.VMEM((FFN_TS, D), F32)],
        compiler_params=_params(58, ("arbitrary", "arbitrary")),
    )(dhp, h, g, gu_arr, w_in8, w_in8, w_out, *after)


def _ffn_bwd_gu(df, gu_arr, w_out, name, after=()):
    ni = S // FFN_TS
    gu = pl.BlockSpec((2, FFN_TS, FFN_TF), lambda j, i: (0, i, j))
    wo = pl.BlockSpec((FFN_TF, D), lambda j, i: (j, 0))
    df_all = pl.BlockSpec((S, D), lambda j, i: (0, 0), pipeline_mode=pl.Buffered(1))

    def body(df_ref, gu_ref, wo_ref, dgu_ref, dwo_ref, acc_ref):
        i = pl.program_id(1)
        df_rows = df_ref[pl.ds(pl.multiple_of(i * FFN_TS, FFN_TS), FFN_TS), :]
        gate = gu_ref[0].astype(F32)
        up = gu_ref[1].astype(F32)
        da = _dot_nt(df_rows, wo_ref[...])
        sg = _sigmoid(gate)
        silu = gate * sg
        dgu_ref[0] = (da * up * (sg * (1.0 + gate * (1.0 - sg)))).astype(BF16)
        dgu_ref[1] = (da * silu).astype(BF16)
        part = _dot_tn((silu * up).astype(BF16), df_rows)

        @pl.when(i == 0)
        def _():
            acc_ref[...] = part

        @pl.when(i > 0)
        def _():
            acc_ref[...] += part

        @pl.when(i == ni - 1)
        def _():
            dwo_ref[...] = acc_ref[...].astype(BF16)

    body, extra = _ordered_after(body, 3, after)
    return _pallas(
        body, grid=(FF // FFN_TF, ni), name=name,
        in_specs=[df_all, gu, wo] + extra, out_specs=[gu, wo],
        out_shape=[jax.ShapeDtypeStruct((2, S, FF), BF16), jax.ShapeDtypeStruct((FF, D), BF16)],
        scratch_shapes=[pltpu.VMEM((FFN_TF, D), F32)],
        compiler_params=_params(40, ("arbitrary", "arbitrary")),
    )(df, gu_arr, w_out, *after)


def _ffn_bwd_dx(dgu, dhp, h, g, w_in, first_tile, ntiles, name, after=(), into=None):
    nj = FF // FFN_TF
    wg, wu, _, _, vec, _ = _ffn_specs()
    row_in = pl.BlockSpec((FFN_TS, D), lambda i, j: (first_tile + i, 0))
    dgu_spec = pl.BlockSpec((2, FFN_TS, FFN_TF), lambda i, j: (0, first_tile + i, j))
    after = tuple(after) + (() if into is None else (into,))

    def body(dgu_ref, dhp_ref, h_ref, g_ref, wg_ref, wu_ref, dh_ref, dg_ref, acc_ref):
        i, j = pl.program_id(0), pl.program_id(1)

        @pl.when(j == 0)
        def _():
            acc_ref[...] = jnp.zeros_like(acc_ref)

        acc_ref[...] += _dot_nt(dgu_ref[0], wg_ref[...]) + _dot_nt(dgu_ref[1], wu_ref[...])

        @pl.when(j == nj - 1)
        def _():
            def norm_bwd(rows, dg):
                dx, dg_rows = _rms_bwd(acc_ref[rows, :], h_ref[rows, :], g_ref[...])
                dh_ref[rows, :] = dhp_ref[rows, :] + dx
                return dg + dg_rows
            dg = _row_loop(FFN_TS, norm_bwd, jnp.zeros((1, D), F32))

            @pl.when(i == 0)
            def _():
                dg_ref[...] = dg

            @pl.when(i > 0)
            def _():
                dg_ref[...] += dg

    body, extra = _ordered_after(body, 6, after)
    return _pallas(
        body, grid=(ntiles, nj), name=name,
        in_specs=[dgu_spec, row_in, row_in, vec, wg, wu] + extra, out_specs=[row_in, vec],
        out_shape=[jax.ShapeDtypeStruct((S, D), F32), jax.ShapeDtypeStruct((1, D), F32)],
        input_output_aliases={} if into is None else {6 + len(after) - 1: 0},
        scratch_shapes=[pltpu.VMEM((FFN_TS, D), F32)],
        compiler_params=_params(48, ("arbitrary", "arbitrary")),
    )(dgu, dhp, h, g, w_in, w_in, *after)


def _tn_matmul(a, b, a_spec, b_spec, out_shape, out_spec, grid, name, vmem_mb, after=()):
    def body(a_ref, b_ref, o_ref):
        o_ref[...] = _dot_tn(a_ref[...], b_ref[...]).astype(o_ref.dtype)

    body, extra = _ordered_after(body, 2, after)
    return _pallas(body, grid=grid, name=name, in_specs=[a_spec, b_spec] + extra, out_specs=out_spec,
                          out_shape=out_shape, compiler_params=_params(vmem_mb))(a, b, *after)


def _resident(shape):
    return pl.BlockSpec(shape, lambda i: (0,) * len(shape), pipeline_mode=pl.Buffered(1))


def _norm_matmul(h, g, w, w_tail, name, after=(), ts=256):
    main = w.shape[0] // 128 * 128
    n_out = main + w_tail.shape[0]

    def body(h_ref, g_ref, w_ref, wt_ref, u_ref, n_ref):
        def norm(rows, c):
            n_ref[rows, :] = _rms(h_ref[rows, :], g_ref[...]).astype(BF16)
            return c
        _row_loop(ts, norm)
        n = n_ref[...]
        u_ref[:, :main] = _dot_nt(n, w_ref[:main, :])
        u_ref[:, main:] = _dot_nt(n, wt_ref[...])

    body, extra = _ordered_after(body, 4, after)
    return _pallas(
        body, grid=(S // ts,), name=name,
        in_specs=[pl.BlockSpec((ts, D), lambda i: (i, 0)), pl.BlockSpec((1, D), lambda i: (0, 0)), _resident(w.shape),
                  _resident(w_tail.shape)] + extra,
        out_specs=[pl.BlockSpec((ts, n_out), lambda i: (i, 0)), pl.BlockSpec((ts, D), lambda i: (i, 0))],
        out_shape=[jax.ShapeDtypeStruct((S, n_out), F32), jax.ShapeDtypeStruct((S, D), BF16)],
        compiler_params=_params(48, ("arbitrary",)),
    )(h, g, w, w_tail, *after)


def _matmul_residual(a, w, res, name, after=(), ts=512):
    k, n_out = w.shape

    def body(a_ref, w_ref, r_ref, o_ref):
        o_ref[...] = r_ref[...] + _dot(a_ref[...], w_ref[...])

    body, extra = _ordered_after(body, 3, after)
    return _pallas(
        body, grid=(S // ts,), name=name,
        in_specs=[pl.BlockSpec((ts, k), lambda i: (i, 0)), _resident(w.shape), pl.BlockSpec((ts, n_out), lambda i: (i, 0))] + extra,
        out_specs=pl.BlockSpec((ts, n_out), lambda i: (i, 0)),
        out_shape=jax.ShapeDtypeStruct((S, n_out), F32),
        compiler_params=_params(40),
    )(a, w, res, *after)


def _nt_matmul(a, w, name, after=(), ts=512):
    n_out, k = w.shape

    def body(a_ref, w_ref, o_ref):
        o_ref[...] = _dot_nt(a_ref[...], w_ref[...])

    body, extra = _ordered_after(body, 2, after)
    return _pallas(
        body, grid=(S // ts,), name=name,
        in_specs=[pl.BlockSpec((ts, k), lambda i: (i, 0)), _resident(w.shape)] + extra,
        out_specs=pl.BlockSpec((ts, n_out), lambda i: (i, 0)),
        out_shape=jax.ShapeDtypeStruct((S, n_out), F32),
        compiler_params=_params(40),
    )(a, w, *after)


def _matmul_normbwd(du, w, w_tail, h, g, dres, name, after=(), ts=256):
    main = w.shape[0] // 128 * 128
    n_in = main + w_tail.shape[0]
    row = pl.BlockSpec((ts, D), lambda i: (i, 0))
    vec = pl.BlockSpec((1, D), lambda i: (0, 0))

    def body(du_ref, w_ref, wt_ref, h_ref, g_ref, dres_ref, dh_ref, dg_ref, half_ref, acc_ref):
        i = pl.program_id(0)
        acc_ref[...] = _dot(du_ref[:, :main], w_ref[:main, :]) + _dot(du_ref[:, main:], wt_ref[...])

        def norm_bwd(rows, dg):
            dx, dg_rows = _rms_bwd(acc_ref[rows, :], h_ref[rows, :], g_ref[...])
            dh = dres_ref[rows, :] + dx
            dh_ref[rows, :] = dh
            half_ref[rows, :] = (0.5 * dh).astype(BF16)
            return dg + dg_rows
        dg = _row_loop(ts, norm_bwd, jnp.zeros((1, D), F32))

        @pl.when(i == 0)
        def _():
            dg_ref[...] = dg

        @pl.when(i > 0)
        def _():
            dg_ref[...] += dg

    body, extra = _ordered_after(body, 6, after)
    return _pallas(
        body, grid=(S // ts,), name=name,
        in_specs=[pl.BlockSpec((ts, n_in), lambda i: (i, 0)), _resident(w.shape), _resident(w_tail.shape), row, vec, row] + extra,
        out_specs=[row, vec, row],
        out_shape=[jax.ShapeDtypeStruct((S, D), F32), jax.ShapeDtypeStruct((1, D), F32), jax.ShapeDtypeStruct((S, D), BF16)],
        scratch_shapes=[pltpu.VMEM((ts, D), F32)],
        compiler_params=_params(52, ("arbitrary",)),
    )(du, w, w_tail, h, g, dres, *after)


def _pool_specs():
    blk = pl.BlockSpec((S, PG), lambda gi: (0, gi))
    wp = pl.BlockSpec((None, PG, PG), lambda gi: (gi, 0, 0))
    sc = pl.BlockSpec((1, PG), lambda gi: (0, gi))
    return blk, wp, sc


def _pool_fwd(u, wp_b, scale):
    blk, wp, sc = _pool_specs()

    def body(u_ref, wp_ref, sc_ref, y_ref, pooled_ref):
        win = 2 << pl.program_id(0)
        row = lax.broadcasted_iota(jnp.int32, (S, PG), 0)
        x = u_ref[...]
        s = x
        for k in (1, 2, 4, 8):
            s = s + jnp.where((row >= k) & (k < win), pltpu.roll(s, k, 0), 0.0)
        cnt = jnp.minimum(row + 1, win).astype(F32)
        pooled = (s / cnt - x).astype(BF16)
        pooled_ref[...] = pooled
        y_ref[...] = (_dot(pooled, wp_ref[...]) * sc_ref[...]).astype(BF16)

    return _pallas(
        body, grid=(len(POOL_WINDOWS),), name="pool_fwd", in_specs=[blk, wp, sc], out_specs=[blk, blk],
        out_shape=[jax.ShapeDtypeStruct((S, D_POOL), BF16), jax.ShapeDtypeStruct((S, D_POOL), BF16)],
        compiler_params=_params(40),
    )(u, wp_b, scale)


def _pool_bwd(dy, pooled, wp_b, scale):
    blk, wp, sc = _pool_specs()

    def body(dy_ref, p_ref, wp_ref, sc_ref, du_ref, dwp_ref, dsc_ref):
        win = 2 << pl.program_id(0)
        row = lax.broadcasted_iota(jnp.int32, (S, PG), 0)
        dyv = dy_ref[...]
        pooled = p_ref[...]
        w = wp_ref[...]
        dsc_ref[...] = jnp.sum(dyv * _dot(pooled, w), axis=0, keepdims=True)
        dz = (dyv * sc_ref[...]).astype(BF16)
        dwp_ref[...] = _dot_tn(pooled, dz)
        dpooled = _dot_nt(dz, w)
        cnt = jnp.minimum(row + 1, win).astype(F32)
        fs = dpooled / cnt
        for k in (1, 2, 4, 8):
            fs = fs + jnp.where((row < S - k) & (k < win), pltpu.roll(fs, S - k, 0), 0.0)
        du_ref[...] = (fs - dpooled).astype(BF16)

    return _pallas(
        body, grid=(len(POOL_WINDOWS),), name="pool_bwd", in_specs=[blk, blk, wp, sc], out_specs=[blk, wp, sc],
        out_shape=[jax.ShapeDtypeStruct((S, D_POOL), BF16), jax.ShapeDtypeStruct((len(POOL_WINDOWS), PG, PG), F32),
                   jax.ShapeDtypeStruct((1, D_POOL), F32)],
        compiler_params=_params(40),
    )(dy, pooled, wp_b, scale)


GLA_CPS = 2
GLA_ROWS = GLA_CPS * CH
GLA_STEPS = NCH // GLA_CPS


def _gla_in_specs(step_of):
    def at(width, col):
        return pl.BlockSpec((GLA_ROWS, width), lambda n: (step_of(n), col))
    return [at(H * DK, O_Q // (H * DK)), at(H * DK, O_K // (H * DK)), at(H * DV, O_V // (H * DV)),
            at(H * DV, O_G // (H * DV)), at(RANK_PAD, O_R // RANK_PAD)]


def _gla_decay_terms(lr, wa_ref, ba_ref, q, k):
    row = lax.broadcasted_iota(jnp.int32, (CH, CH), 0)
    col = lax.broadcasted_iota(jnp.int32, (CH, CH), 1)
    tril = row >= col
    z = _dot(lr.astype(BF16), wa_ref[...]) + ba_ref[...]
    la = _log_sigmoid(z) / GATE_NORM
    b = _tri_dot(jnp.where(tril, 1.0, 0.0).astype(BF16), la)
    bl = jnp.sum(la, axis=0, keepdims=True)
    e_b, e_nb, e_tb = jnp.exp(b), jnp.exp(-b), jnp.exp(bl - b)
    q_dec = (q * QK_SCALE) * e_b
    return tril, z, e_b, e_nb, e_tb, jnp.exp(bl), q_dec, k * e_nb, k * e_tb


def _gla_fwd(u, wa_b, ba, gn):
    wide = pl.BlockSpec((GLA_ROWS, H * DV), lambda n: (n, 0))

    def body(q_ref, k_ref, v_ref, g_ref, lr_ref, wa_ref, ba_ref, gn_ref, y_ref, o_ref, st_ref, state):
        @pl.when(pl.program_id(0) == 0)
        def _():
            state[...] = jnp.zeros_like(state)

        for c in range(GLA_CPS):
            rows = slice(c * CH, (c + 1) * CH)
            tril, _, _, _, _, dec, q_dec, k_inv, k_tail = _gla_decay_terms(lr_ref[rows, :], wa_ref, ba_ref, q_ref[rows, :],
                                                                          k_ref[rows, :])
            for hd in range(H):
                ks, vs = slice(hd * DK, (hd + 1) * DK), slice(hd * DV, (hd + 1) * DV)
                qb, kib, ktb = q_dec[:, ks].astype(BF16), k_inv[:, ks].astype(BF16), k_tail[:, ks].astype(BF16)
                vb = v_ref[rows, vs].astype(BF16)
                p = jnp.where(tril, _dot_nt(qb, kib), 0.0)
                st = state[hd]
                o = _dot(p.astype(BF16), vb) + _dot_nt(qb, st.astype(BF16))
                st_ref[c, hd] = st
                state[hd] = st * dec[:, ks] + _dot_tn(vb, ktb)
                o_ref[rows, vs] = o
                on = _rms(o, gn_ref[...])
                gg = g_ref[rows, vs]
                y_ref[rows, vs] = (on * (gg * _sigmoid(gg))).astype(BF16)

    return _pallas(
        body, grid=(GLA_STEPS,), name="gla_fwd",
        in_specs=_gla_in_specs(lambda n: n) + [pl.BlockSpec((RANK_PAD, H * DK), lambda n: (0, 0)),
                                               pl.BlockSpec((1, H * DK), lambda n: (0, 0)),
                                               pl.BlockSpec((1, DV), lambda n: (0, 0))],
        out_specs=[wide, wide, pl.BlockSpec((GLA_CPS, H, DV, DK), lambda n: (n, 0, 0, 0))],
        out_shape=[jax.ShapeDtypeStruct((S, H * DV), BF16), jax.ShapeDtypeStruct((S, H * DV), F32),
                   jax.ShapeDtypeStruct((NCH, H, DV, DK), F32)],
        scratch_shapes=[pltpu.VMEM((H, DV, DK), F32)],
        compiler_params=_params(32, ("arbitrary",)),
    )(u, u, u, u, u, wa_b, ba, gn)


GLA_DU = 2 * H * DK + 2 * H * DV + RANK_PAD


def _gla_bwd(u, o_arr, states, dy, wa_b, ba, gn, after=()):
    rev = lambda n: GLA_STEPS - 1 - n
    wide = pl.BlockSpec((GLA_ROWS, H * DV), lambda n: (rev(n), 0))

    def body(q_ref, k_ref, v_ref, g_ref, lr_ref, o_ref, st_ref, dy_ref, wa_ref, ba_ref, gn_ref,
             du_ref, dwa_ref, dba_ref, dgn_ref, gstate, db_scr, dbl_scr):
        @pl.when(pl.program_id(0) == 0)
        def _():
            gstate[...] = jnp.zeros_like(gstate)
            dwa_ref[...] = jnp.zeros_like(dwa_ref)
            dba_ref[...] = jnp.zeros_like(dba_ref)
            dgn_ref[...] = jnp.zeros_like(dgn_ref)

        gnv = gn_ref[...]
        dgn = jnp.zeros((1, DV), F32)
        dwa = jnp.zeros((RANK_PAD, H * DK), F32)
        dba = jnp.zeros((1, H * DK), F32)
        srow = lax.broadcasted_iota(jnp.int32, (CH, CH), 0)
        scol = lax.broadcasted_iota(jnp.int32, (CH, CH), 1)
        for c in reversed(range(GLA_CPS)):
            rows = slice(c * CH, (c + 1) * CH)
            lr = lr_ref[rows, :]
            tril, z, e_b, e_nb, e_tb, dec, q_dec, k_inv, k_tail = _gla_decay_terms(lr, wa_ref, ba_ref, q_ref[rows, :],
                                                                                  k_ref[rows, :])
            for hd in range(H):
                ks, vs = slice(hd * DK, (hd + 1) * DK), slice(hd * DV, (hd + 1) * DV)
                qh, kih, kth = q_dec[:, ks], k_inv[:, ks], k_tail[:, ks]
                qb, kib, ktb = qh.astype(BF16), kih.astype(BF16), kth.astype(BF16)
                vb = v_ref[rows, vs].astype(BF16)
                o = o_ref[rows, vs]
                gg = g_ref[rows, vs]
                dyh = dy_ref[rows, vs]
                r = lax.rsqrt(jnp.mean(o * o, axis=-1, keepdims=True) + EPS)
                xh = o * r
                sg = _sigmoid(gg)
                dgate = dyh * (xh * gnv) * (sg * (1.0 + gg * (1.0 - sg)))
                don = dyh * (gg * sg)
                dgn = dgn + jnp.sum(don * xh, axis=0, keepdims=True)
                dxh = don * gnv
                d_o = (r * (dxh - xh * jnp.mean(dxh * xh, axis=-1, keepdims=True))).astype(BF16)
                pb = jnp.where(tril, _dot_nt(qb, kib), 0.0).astype(BF16)
                dpb = jnp.where(tril, _dot_nt(d_o, vb), 0.0).astype(BF16)
                gt = gstate[hd]
                gtb = gt.astype(BF16)
                st = st_ref[c, hd]
                dv = _dot_tn(pb, d_o) + _dot_nt(ktb, gtb)
                dq_dec = _dot(dpb, kib) + _dot(d_o, st.astype(BF16))
                dk_inv = _dot_tn(dpb, qb)
                dk_tail = _dot(vb, gtb)
                ddec = jnp.sum(gt * st, axis=0, keepdims=True)
                gstate[hd] = _dot_tn(d_o, qb) + dec[:, ks] * gt
                du_ref[rows, ks] = (dq_dec * QK_SCALE * e_b[:, ks]).astype(BF16)
                du_ref[rows, H * DK + hd * DK:H * DK + (hd + 1) * DK] = (dk_inv * e_nb[:, ks] + dk_tail * e_tb[:, ks]).astype(BF16)
                du_ref[rows, 2 * H * DK + hd * DV:2 * H * DK + (hd + 1) * DV] = dv.astype(BF16)
                du_ref[rows, 2 * H * DK + H * DV + hd * DV:2 * H * DK + H * DV + (hd + 1) * DV] = dgate.astype(BF16)
                db_scr[c, :, ks] = dq_dec * qh - dk_inv * kih - dk_tail * kth
                dbl_scr[c, :, ks] = jnp.sum(dk_tail * kth, axis=0, keepdims=True) + ddec * dec[:, ks]
            dla = _tri_dot(jnp.where(srow <= scol, 1.0, 0.0).astype(BF16), db_scr[c]) + dbl_scr[c]
            dz = dla * (1.0 / GATE_NORM) * _sigmoid(-z)
            dzb = dz.astype(BF16)
            du_ref[rows, GLA_DU - RANK_PAD:] = _dot_nt(dzb, wa_ref[...]).astype(BF16)
            dwa = dwa + _dot_tn(lr.astype(BF16), dzb)
            dba = dba + jnp.sum(dz, axis=0, keepdims=True)
        dgn_ref[...] += dgn
        dwa_ref[...] += dwa
        dba_ref[...] += dba

    full = lambda shape: pl.BlockSpec(shape, lambda n: (0,) * len(shape))
    body, extra = _ordered_after(body, 11, after)
    return _pallas(
        body, grid=(GLA_STEPS,), name="gla_bwd",
        in_specs=_gla_in_specs(rev) + [wide, pl.BlockSpec((GLA_CPS, H, DV, DK), lambda n: (rev(n), 0, 0, 0)),
                                       pl.BlockSpec((GLA_ROWS, H * DV), lambda n: (rev(n), 1)),
                                       full((RANK_PAD, H * DK)), full((1, H * DK)), full((1, DV))] + extra,
        out_specs=[pl.BlockSpec((GLA_ROWS, GLA_DU), lambda n: (rev(n), 0)), full((RANK_PAD, H * DK)), full((1, H * DK)),
                   full((1, DV))],
        out_shape=[jax.ShapeDtypeStruct((S, GLA_DU), BF16), jax.ShapeDtypeStruct((RANK_PAD, H * DK), F32),
                   jax.ShapeDtypeStruct((1, H * DK), F32), jax.ShapeDtypeStruct((1, DV), F32)],
        scratch_shapes=[pltpu.VMEM((H, DV, DK), F32), pltpu.VMEM((GLA_CPS, CH, H * DK), F32),
                        pltpu.VMEM((GLA_CPS, 1, H * DK), F32)],
        compiler_params=_params(32, ("arbitrary",)),
    )(u, u, u, u, u, o_arr, states, dy, wa_b, ba, gn, *after)


def _ffn_dw_in(n, dgu, tag, after=(), first_tile=0, ntiles=D // 512):
    return _tn_matmul(n, dgu, pl.BlockSpec((S, 512), lambda s, m: (0, first_tile + m)),
                      pl.BlockSpec((None, S, WIN_SHARD), lambda s, m: (s // (NDEV // 2), 0, s % (NDEV // 2))),
                      jax.ShapeDtypeStruct((512 * ntiles, 2 * FF), BF16), pl.BlockSpec((512, WIN_SHARD), lambda s, m: (m, s)),
                      (NDEV, ntiles), tag + "_dw_in", 32, after)


def _ffn_dw_out(act, df, tag, after=()):
    return _tn_matmul(act, df, pl.BlockSpec((S, 512), lambda m: (0, m)), pl.BlockSpec((S, D), lambda m: (0, 0)),
                      jax.ShapeDtypeStruct((FF, D), BF16), pl.BlockSpec((512, D), lambda m: (m, 0)),
                      (FF // 512,), tag + "_dw_out", 40, after)


def _fwd_ffn1(x, w, after=()):
    return _ffn_fwd(x, w["ffn1_norm"], w["ffn1_w_in"], w["ffn1_w_out"], "ffn1_fwd", after)


def _fwd_mixer(h1, w, after=(), at=lambda point, value: ()):
    u, n2 = _norm_matmul(h1, w["mix_norm"], w["w_in_mix"], w["w_in_mix_tail"], "mix_in", after)
    y_pool, pooled = _pool_fwd(u, w["w_pool"], w["pool_scale"])
    y_gla, o_gla, states = _gla_fwd(u, w["w_alpha"], w["b_alpha"], w["gla_norm"])
    y = jnp.concatenate([y_pool, y_gla], axis=1)
    h2 = _matmul_residual(y, w["w_out_mix"], h1, "mix_out", at("gla_fwd", y))
    return h2, dict(u=u, n2=n2, pooled=pooled, o_gla=o_gla, states=states, y=y)


def _fwd_ffn2_loss(h2, tgt, w, after=()):
    dh3, n3, gu3, d_final, loss = _ffn_fwd(h2, w["ffn2_norm"], w["ffn2_w_in"], w["ffn2_w_out"], "ffn2_fwd", after,
                                           head=(w["final_norm"], tgt))
    return dh3, d_final, loss, n3, gu3


def _backward(x, h1, h2, n1, gu1, sv, n3, gu3, dh3, w, at=lambda point, value: ()):
    g = {}
    dgu3, act3, df3, dh2, dh2b, g["ffn2_norm"] = _ffn_bwd_x(dh3, h2, w["ffn2_norm"], gu3, w["ffn2_w_in"], w["ffn2_w_out"], "ffn2_bwd")
    g["ffn2_w_out"] = _ffn_dw_out(act3, df3, "ffn2")
    g["ffn2_w_in"] = _ffn_dw_in(n3, dgu3, "ffn2", at("ffn2_w_out", g["ffn2_w_out"]))
    dy = _nt_matmul(dh2b, w["w_out_mix"], "mix_out_bwd", at("ffn2_w_in", g["ffn2_w_in"]))
    g["w_out_mix"] = _tn_matmul(sv["y"], dh2b, pl.BlockSpec((S, 512), lambda m: (0, m)), pl.BlockSpec((S, D), lambda m: (0, 0)),
                                jax.ShapeDtypeStruct((D, D), BF16), pl.BlockSpec((512, D), lambda m: (m, 0)), (D // 512,),
                                "mix_out_dw", 40)
    du_pool, g["w_pool"], g["pool_scale"] = _pool_bwd(dy, sv["pooled"], w["w_pool"], w["pool_scale"])
    du_gla, g["w_alpha"], g["b_alpha"], g["gla_norm"] = _gla_bwd(sv["u"], sv["o_gla"], sv["states"], dy, w["w_alpha"], w["b_alpha"],
                                                                 w["gla_norm"], at("pool_bwd", du_pool))
    du = jnp.concatenate([du_pool, du_gla], axis=1)
    g["w_in_mix"] = _tn_matmul(du, sv["n2"], pl.BlockSpec((S, 1408), lambda j, m: (0, j)), pl.BlockSpec((S, 512), lambda j, m: (0, m)),
                               jax.ShapeDtypeStruct((D_IN, D), BF16), pl.BlockSpec((1408, 512), lambda j, m: (j, m)),
                               (D_IN_PAD // 1408, D // 512), "mix_in_dw", 32)
    dh1, g["mix_norm"], df1 = _matmul_normbwd(du, w["w_in_mix"], w["w_in_mix_tail"], h1, w["mix_norm"], dh2, "mix_in_bwd",
                                              at("mixer_weights", g))
    dgu1, g["ffn1_w_out"] = _ffn_bwd_gu(df1, gu1, w["ffn1_w_out"], "ffn1_bwd_gu", at("dh1", dh1))
    half = S // FFN_TS // 2
    dx, dn_a = _ffn_bwd_dx(dgu1, dh1, x, w["ffn1_norm"], w["ffn1_w_in"], 0, half, "ffn1_bwd_dx_a",
                           at("ffn1_w_out", g["ffn1_w_out"]))
    g["ffn1_w_in_top"] = _ffn_dw_in(n1, dgu1, "ffn1_top", at("ffn1_dx_a", dx), 0, D // 1024)
    g["ffn1_w_in_bottom"] = _ffn_dw_in(n1, dgu1, "ffn1_bottom", at("ffn1_w_in_top", g["ffn1_w_in_top"]), D // 1024, D // 1024)
    dx, dn_b = _ffn_bwd_dx(dgu1, dh1, x, w["ffn1_norm"], w["ffn1_w_in"], half, half, "ffn1_bwd_dx_b",
                           at("ffn1_w_in_bottom", g["ffn1_w_in_bottom"]), into=dx)
    g["ffn1_norm"] = dn_a + dn_b
    return dx, g


def _local_step(x, tgt, w):
    h1, n1, gu1 = _fwd_ffn1(x, w)
    h2, sv = _fwd_mixer(h1, w)
    dh3, d_final, loss, n3, gu3 = _fwd_ffn2_loss(h2, tgt, w)
    dx, g = _backward(x, h1, h2, n1, gu1, sv, n3, gu3, dh3, w)
    return loss, dx, dict(final_norm=d_final, **g)


def _coords(p):
    return (p // 4, (p // 2) % 2, p % 2)


NCHIP = 4


def _place():
    return lax.axis_index("x"), lax.axis_index("y"), lax.axis_index("c")


def _rel_chip(x, y, rel):
    return ((1 - x) if rel & 1 else x, (1 - y) if rel & 2 else y)


def _dev_index(x, y, c):
    return 4 * x + 2 * y + c


def _cols(ref, p, width):
    return ref.at[:, pl.ds(pl.multiple_of(p * width, 128), width)]


def _sems(na, n):
    return [pltpu.SemaphoreType.DMA((na, n)), pltpu.SemaphoreType.DMA((na, n)), pltpu.SemaphoreType.DMA((na,))]


def _gather(items, name):
    arrays = [a for a, _ in items]
    kinds = [k for _, k in items]
    na = len(arrays)
    out_shape = [jax.ShapeDtypeStruct((NDEV,) + a.shape if k == "bcast" else (a.shape[0], NDEV * a.shape[1]), a.dtype)
                 for a, k in items]

    def body(*refs):
        ins, outs = refs[:na], refs[na:2 * na]
        send_sems, recv_sems, local_sems = refs[2 * na:]
        x, y, c = _place()
        sibling = (x, y, 1 - c)
        here, over_x, over_y, across = (x, y), (1 - x, y), (x, 1 - y), (1 - x, 1 - y)

        def half(ref, h):
            rows = ref.shape[0] // 2
            return ref.at[pl.ds(h * rows, rows), :]

        def slab(a, chip, core, h=None):
            ref = _slab(outs[a], kinds[a], _dev_index(*chip, core), ins[a].shape[1])
            return ref if h is None else half(ref, h)

        def copy(a, k, src, dst, to):
            return pltpu.make_async_remote_copy(src, dst, send_sems.at[a, k], recv_sems.at[a, k], device_id=to, device_id_type=MESH)

        sent = []

        def send(a, k, src, dst, to):
            sent.append(copy(a, k, src, dst, to))
            sent[-1].start()

        def arrived(a, k, chip, core, h=None):
            ref = slab(a, chip, core, h)
            copy(a, k, ref, ref, sibling).wait_recv()
            return ref

        local = [pltpu.make_async_copy(ins[a], slab(a, here, c), local_sems.at[a]) for a in range(na)]
        for cp in local:
            cp.start()
        for k, h, chip in ((1, 0, over_x), (4, 1, over_y), (2, 1, over_x), (5, 0, over_y)):
            for a in range(na):
                send(a, k, half(ins[a], h), slab(a, here, c, h), (*chip, c))
        for a in range(na):
            send(a, 0, ins[a], slab(a, here, c), sibling)
        for k, chip, h, onward, to, down in ((1, over_x, 0, 6, over_y, 7), (4, over_y, 1, 3, over_x, 10),
                                             (2, over_x, 1, None, None, 8), (5, over_y, 0, None, None, 9),
                                             (3, across, 1, None, None, 12), (6, across, 0, None, None, 11)):
            for a in range(na):
                ref = arrived(a, k, chip, c, h)
                if onward is not None:
                    send(a, onward, ref, ref, (*to, c))
                send(a, down, ref, ref, sibling)
        for a in range(na):
            arrived(a, 0, here, 1 - c)
        for k, chip, h in ((7, over_x, 0), (10, over_y, 1), (8, over_x, 1), (9, over_y, 0), (12, across, 1), (11, across, 0)):
            for a in range(na):
                arrived(a, k, chip, 1 - c, h)
        for cp in sent:
            cp.wait_send()
        for cp in local:
            cp.wait()

    return pl.pallas_call(body, name=name, in_specs=[ANY] * na, out_specs=[ANY] * na, out_shape=out_shape,
                          scratch_shapes=_sems(na, 13))(*arrays)


def _pair_add(own, kind, got, table, tr, name):
    _, rows, cols = got.shape
    if kind == "scatter":
        own_spec = pl.BlockSpec((None, tr, cols), lambda rel, i, t: (t[rel], i, 0))
    else:
        own_spec = pl.BlockSpec((tr, cols), lambda rel, i, t: (i, t[rel]))
    blk = pl.BlockSpec((None, tr, cols), lambda rel, i, t: (rel, i, 0))

    def body(t_ref, a_ref, b_ref, o_ref):
        o_ref[...] = (a_ref[...].astype(F32) + b_ref[...].astype(F32)).astype(o_ref.dtype)

    return _pallas(
        body, name=name, out_shape=jax.ShapeDtypeStruct(got.shape, got.dtype),
        grid_spec=pltpu.PrefetchScalarGridSpec(num_scalar_prefetch=1, grid=(NCHIP, rows // tr), in_specs=[own_spec, blk],
                                               out_specs=blk),
        compiler_params=_params(32),
    )(table, own, got)


HBM = pl.BlockSpec(memory_space=pltpu.HBM)
SEM = pl.BlockSpec(memory_space=pltpu.SEMAPHORE)
DATAFLOW = pltpu.SideEffectType.DATAFLOW_SIDE_EFFECTING


def _pair_copies(kinds):
    def describe(srcs, lands, send_sems, recv_sems):
        x, y, c = _place()
        na = len(srcs)
        for rel in range(NCHIP):
            p = _dev_index(*_rel_chip(x, y, rel), 1 - c)
            for a in range(na):
                src = srcs[a].at[p] if kinds[a] == "scatter" else _cols(srcs[a], p, srcs[a].shape[1] // NDEV)
                cp = pltpu.make_async_remote_copy(src, lands[a].at[rel], send_sems.at[rel * na + a], recv_sems.at[rel * na + a],
                                                  device_id=(x, y, 1 - c), device_id_type=MESH)
                yield cp, cp
    return describe


def _chip_copies(srcs, lands, send_sems, recv_sems):
    x, y, c = _place()
    na = len(srcs)
    for rel in range(1, NCHIP):
        for a in range(na):
            i = (rel - 1) * na + a
            cp = pltpu.make_async_remote_copy(srcs[a].at[rel], lands[a].at[rel], send_sems.at[i], recv_sems.at[i],
                                              device_id=(*_rel_chip(x, y, rel), c), device_id_type=MESH)
            yield cp, cp


def _slab(ref, kind, s, width):
    return _cols(ref, s, width) if kind == "bcast_cols" else ref.at[s]


def _all_copies(srcs, lands, send_sems, recv_sems):
    x, y, c = _place()
    na = len(srcs)
    me = _dev_index(x, y, c)
    for a in range(na):
        yield pltpu.make_async_copy(srcs[a], lands[a].at[me], send_sems.at[a]), None
    for k in range(1, NDEV):
        to, frm = (me + k) % NDEV, (me + NDEV - k) % NDEV
        for a in range(na):
            i = k * na + a
            send = pltpu.make_async_remote_copy(srcs[a], lands[a].at[me], send_sems.at[i], recv_sems.at[i],
                                                device_id=_coords(to), device_id_type=MESH)
            arrival = pltpu.make_async_remote_copy(srcs[a], lands[a].at[frm], send_sems.at[i], recv_sems.at[i],
                                                   device_id=_coords(to), device_id_type=MESH)
            yield send, arrival


def _gather_copies(kinds):
    def describe(srcs, lands, send_sems, recv_sems):
        x, y, c = _place()
        na = len(srcs)
        me = _dev_index(x, y, c)
        for a in range(na):
            yield pltpu.make_async_copy(srcs[a], _slab(lands[a], kinds[a], me, srcs[a].shape[-1]),
                                        send_sems.at[NCHIP * na + a]), None
        for rel in range(NCHIP):
            to = (x, y, 1 - c) if rel == 0 else (*_rel_chip(x, y, rel), c)
            for a in range(na):
                width = srcs[a].shape[-1]
                i = rel * na + a
                send = pltpu.make_async_remote_copy(srcs[a], _slab(lands[a], kinds[a], me, width), send_sems.at[i], recv_sems.at[i],
                                                    device_id=to, device_id_type=MESH)
                arrival = pltpu.make_async_remote_copy(srcs[a], _slab(lands[a], kinds[a], _dev_index(*to), width), send_sems.at[i],
                                                       recv_sems.at[i], device_id=to, device_id_type=MESH)
                yield send, arrival
    return describe


def _relay_copies(kinds, widths, phase):
    def describe(srcs, lands, send_sems, recv_sems):
        x, y, c = _place()
        na = len(lands)
        here, over_x, over_y, across = (x, y), (1 - x, y), (x, 1 - y), (1 - x, 1 - y)
        sibling = (x, y, 1 - c)

        def half(ref, h):
            rows = ref.shape[0] // 2
            return ref if h is None else ref.at[pl.ds(h * rows, rows), :]

        def slab(a, chip, core, h=None):
            return half(_slab(lands[a], kinds[a], _dev_index(*chip, core), widths[a]), h)

        def pair(a, k, src, dst, to, arrival):
            i = k * na + a
            return (pltpu.make_async_remote_copy(src, dst, send_sems.at[i], recv_sems.at[i], device_id=to, device_id_type=MESH),
                    pltpu.make_async_remote_copy(arrival, arrival, send_sems.at[i], recv_sems.at[i], device_id=to,
                                                 device_id_type=MESH))

        if phase == 1:
            for a in range(na):
                yield pltpu.make_async_copy(srcs[a], slab(a, here, c), send_sems.at[a]), None
            for k, h, chip in ((1, 0, over_x), (2, 1, over_y), (3, 1, over_x), (4, 0, over_y)):
                for a in range(na):
                    yield pair(a, k, half(srcs[a], h), slab(a, here, c, h), (*chip, c), slab(a, chip, c, h))
            for a in range(na):
                yield pair(a, 5, srcs[a], slab(a, here, c), sibling, slab(a, here, 1 - c))
        elif phase == 2:
            for a in range(na):
                yield pair(a, 0, slab(a, over_x, c, 0), slab(a, over_x, c, 0), (*over_y, c), slab(a, across, c, 0))
                yield pair(a, 1, slab(a, over_y, c, 1), slab(a, over_y, c, 1), (*over_x, c), slab(a, across, c, 1))
            for k, (chip, h) in enumerate(((over_x, 0), (over_y, 1), (over_x, 1), (over_y, 0)), start=2):
                for a in range(na):
                    yield pair(a, k, slab(a, chip, c, h), slab(a, chip, c, h), sibling, slab(a, chip, 1 - c, h))
        else:
            for h in (0, 1):
                for a in range(na):
                    yield pair(a, h, slab(a, across, c, h), slab(a, across, c, h), sibling, slab(a, across, 1 - c, h))
    return describe


RELAY_COPIES = {1: 6, 2: 6, 3: 2}


def _relay_finish(arrays, kinds, widths, name):
    na = len(arrays)
    describe = _relay_copies(kinds, widths, 3)

    def body(*refs):
        copies = list(describe((), refs[na:2 * na], refs[2 * na], refs[2 * na + 1]))
        for send, _ in copies:
            send.start()
        for send, arrival in copies:
            send.wait_send()
            arrival.wait_recv()

    return pl.pallas_call(body, name=name, in_specs=[ANY] * na, out_specs=[ANY] * na,
                          out_shape=[jax.ShapeDtypeStruct(a.shape, a.dtype) for a in arrays],
                          input_output_aliases={i: i for i in range(na)},
                          scratch_shapes=[pltpu.SemaphoreType.DMA((RELAY_COPIES[3] * na,))] * 2)(*arrays)


def _pass_to_sibling(arrays, kinds, widths, name):
    na = len(arrays)

    def body(*refs):
        bufs = refs[na:2 * na]
        send_sems, recv_sems = refs[2 * na:]
        x, y, c = _place()
        copies = []
        for rel in range(1, NCHIP):
            for a in range(na):
                mine = _slab(bufs[a], kinds[a], _dev_index(*_rel_chip(x, y, rel), c), widths[a])
                theirs = _slab(bufs[a], kinds[a], _dev_index(*_rel_chip(x, y, rel), 1 - c), widths[a])
                send = pltpu.make_async_remote_copy(mine, mine, send_sems.at[a, rel], recv_sems.at[a, rel],
                                                    device_id=(x, y, 1 - c), device_id_type=MESH)
                send.start()
                copies.append((send, pltpu.make_async_remote_copy(theirs, theirs, send_sems.at[a, rel], recv_sems.at[a, rel],
                                                                  device_id=(x, y, 1 - c), device_id_type=MESH)))
        for send, arrival in copies:
            send.wait_send()
            arrival.wait_recv()

    return pl.pallas_call(body, name=name, in_specs=[ANY] * na, out_specs=[ANY] * na,
                          out_shape=[jax.ShapeDtypeStruct(a.shape, a.dtype) for a in arrays],
                          input_output_aliases={i: i for i in range(na)}, scratch_shapes=_sems(na, NCHIP)[:2])(*arrays)


def _start_copies(name, srcs, lands, describe, ncopies, after=()):
    arrays = list(srcs) + list(lands)
    ns, n, nin = len(srcs), len(arrays), len(arrays) + len(after)

    def body(*refs):
        for send, _ in describe(refs[:ns], refs[ns:n], refs[nin], refs[nin + 1]):
            send.start()
        refs[-1][...] = jnp.zeros_like(refs[-1])

    out = pl.pallas_call(
        body, name=name,
        out_shape=(pltpu.SemaphoreType.DMA((ncopies,)), pltpu.SemaphoreType.DMA((ncopies,)),
                   *[pltpu.HBM(a.shape, a.dtype) for a in arrays], jax.ShapeDtypeStruct((8, 128), F32)),
        in_specs=[HBM] * n + [ANY] * len(after), out_specs=(SEM, SEM, *[HBM] * n, pl.BlockSpec(memory_space=pltpu.VMEM)),
        input_output_aliases={i: 2 + i for i in range(n)},
        compiler_params=pltpu.CompilerParams(has_side_effects=DATAFLOW),
    )(*[pltpu.with_memory_space_constraint(a, pltpu.HBM) for a in arrays], *after)
    return out[0], out[1], list(out[2:2 + n]), out[-1]


def _wait_copies(name, send_sems, recv_sems, thru, ns, describe, after):
    n = len(thru)

    def body(*refs):
        for send, arrival in describe(refs[:ns], refs[ns:n], refs[n], refs[n + 1]):
            if arrival is None:
                send.wait()
            else:
                send.wait_send()
                arrival.wait_recv()

    out = pl.pallas_call(
        body, name=name, out_shape=tuple(pltpu.HBM(a.shape, a.dtype) for a in thru),
        in_specs=[HBM] * n + [SEM, SEM] + [ANY] * len(after), out_specs=tuple([HBM] * n),
        input_output_aliases={i: i for i in range(n)},
        compiler_params=pltpu.CompilerParams(has_side_effects=DATAFLOW),
    )(*thru, send_sems, recv_sems, *after)
    return list(out[:ns]), list(out[ns:])


def _adamw(parts, w, m, v, tr, name, tc=None, first_row=0, into=None):
    rows, cols = w.shape
    part_rows = parts[0][0].shape[1]
    tc = cols if tc is None else tc
    nparts = len(parts)
    tile0 = first_row // tr
    blk = pl.BlockSpec((tr, tc), lambda i, j: (tile0 + i, j))
    carried = [] if into is None else list(into)

    def slab_spec(s):
        return pl.BlockSpec((None, tr, tc), lambda i, j: (s, i, j))

    def body(*refs):
        p_refs = refs[:nparts]
        w_ref, m_ref, v_ref = refs[nparts:nparts + 3]
        g_ref, d_ref, nm_ref, nv_ref = refs[nparts + 3 + len(carried):]
        g = p_refs[0][...].astype(F32)
        for p_ref in p_refs[1:]:
            g = g + p_ref[...].astype(F32)
        nm = ADAM_B1 * m_ref[...] + (1.0 - ADAM_B1) * g
        nv = ADAM_B2 * v_ref[...] + (1.0 - ADAM_B2) * (g * g)
        m_hat = nm / (1.0 - ADAM_B1 ** ADAM_STEP)
        v_hat = nv / (1.0 - ADAM_B2 ** ADAM_STEP)
        g_ref[...] = g
        d_ref[...] = -ADAM_LR * (m_hat / (jnp.sqrt(v_hat) + ADAM_EPS) + ADAM_WD * w_ref[...])
        nm_ref[...] = nm
        nv_ref[...] = nv

    return _pallas(
        body, grid=(part_rows // tr, cols // tc), name=name,
        in_specs=[slab_spec(s) for _, s in parts] + [blk, blk, blk] + [ANY] * len(carried), out_specs=[blk] * 4,
        out_shape=[jax.ShapeDtypeStruct((rows, cols), F32)] * 4,
        input_output_aliases={nparts + 3 + k: k for k in range(len(carried))},
        compiler_params=_params(40),
    )(*[a for a, _ in parts], w, m, v, *carried)


def _pack_small(vals, extra=None):
    flat = [vals[n].reshape(-1).astype(F32) for n, _ in SMALL]
    tail = jnp.zeros((SMALL_ROWS * 128 - LOSS_AT,), F32)
    if extra is not None:
        tail = tail.at[0].set(extra)
    return jnp.concatenate(flat + [tail]).reshape(SMALL_ROWS, 128)


def _unpack_small(packed, like):
    flat, out, at = packed.reshape(-1), {}, 0
    for n, size in SMALL:
        out[n] = flat[at:at + size].reshape(like[n].shape)
        at += size
    return out, flat[LOSS_AT]


def kernel(x, ffn1_norm, ffn1_w_in, ffn1_w_out, mix_norm, w_in_mix, w_pool, pool_scale, w_alpha, b_alpha, gla_norm, w_out_mix, ffn2_norm, ffn2_w_in, ffn2_w_out, final_norm, loss_target, m_ffn1_norm, m_ffn1_w_in, m_ffn1_w_out, m_mix_norm, m_w_in_mix, m_w_pool, m_pool_scale, m_w_alpha, m_b_alpha, m_gla_norm, m_w_out_mix, m_ffn2_norm, m_ffn2_w_in, m_ffn2_w_out, m_final_norm, v_ffn1_norm, v_ffn1_w_in, v_ffn1_w_out, v_mix_norm, v_w_in_mix, v_w_pool, v_pool_scale, v_w_alpha, v_b_alpha, v_gla_norm, v_w_out_mix, v_ffn2_norm, v_ffn2_w_in, v_ffn2_w_out, v_final_norm):
    names = ["ffn1_norm", "ffn1_w_in", "ffn1_w_out", "mix_norm", "w_in_mix", "w_pool", "pool_scale", "w_alpha", "b_alpha",
             "gla_norm", "w_out_mix", "ffn2_norm", "ffn2_w_in", "ffn2_w_out", "final_norm"]
    p = dict(zip(names, [ffn1_norm, ffn1_w_in, ffn1_w_out, mix_norm, w_in_mix, w_pool, pool_scale, w_alpha, b_alpha,
                         gla_norm, w_out_mix, ffn2_norm, ffn2_w_in, ffn2_w_out, final_norm]))
    m = dict(zip(names, [m_ffn1_norm, m_ffn1_w_in, m_ffn1_w_out, m_mix_norm, m_w_in_mix, m_w_pool, m_pool_scale, m_w_alpha,
                         m_b_alpha, m_gla_norm, m_w_out_mix, m_ffn2_norm, m_ffn2_w_in, m_ffn2_w_out, m_final_norm]))
    v = dict(zip(names, [v_ffn1_norm, v_ffn1_w_in, v_ffn1_w_out, v_mix_norm, v_w_in_mix, v_w_pool, v_pool_scale, v_w_alpha,
                         v_b_alpha, v_gla_norm, v_w_out_mix, v_ffn2_norm, v_ffn2_w_in, v_ffn2_w_out, v_final_norm]))

    mx, my, mc = _place()
    table = jnp.stack([_dev_index(*_rel_chip(mx, my, rel), mc) for rel in range(NCHIP)]).astype(jnp.int32)

    def landing(shard, kind):
        shape = (shard.shape[0], NDEV * shard.shape[1]) if kind == "bcast_cols" else (NDEV,) + shard.shape
        return lax.empty(shape, shard.dtype)

    def gather_begin(items, tag, after):
        kinds = [kind for _, kind in items]
        copies = _gather_copies(kinds)
        s, r, thru, tok = _start_copies(tag + "_start", [a for a, _ in items], [landing(a, kind) for a, kind in items], copies,
                                        (NCHIP + 1) * len(items), after)
        return (s, r, thru, copies, kinds, [a.shape[-1] for a, _ in items], tag), tok

    def gather_end(state, after):
        s, r, thru, copies, kinds, widths, tag = state
        _, lands = _wait_copies(tag + "_wait", s, r, thru, len(kinds), copies, after)
        return _pass_to_sibling(lands, kinds, widths, tag + "_pass")

    def shard16(n):
        return p[n][0].astype(BF16)

    g_w1in, g_w1out = _gather([(shard16("ffn1_w_in"), "bcast_cols"), (shard16("ffn1_w_out"), "bcast")], "gather_ffn1")
    mix_state, tok_m = gather_begin([(jnp.transpose(p["w_in_mix"][0]).astype(BF16), "bcast"), (shard16("w_out_mix"), "bcast"),
                                     (p["w_pool"][0].reshape(H * 32, PG), "bcast"), (p["w_alpha"][0], "bcast")], "gather_mix",
                                    (g_w1out,))
    ffn2_items = [(shard16("ffn2_w_in"), "bcast_cols"), (shard16("ffn2_w_out"), "bcast")]
    ffn2_kinds = [kind for _, kind in ffn2_items]
    ffn2_widths = [a.shape[-1] for a, _ in ffn2_items]
    relay = {ph: _relay_copies(ffn2_kinds, ffn2_widths, ph) for ph in (1, 2)}
    f_s, f_r, f_thru, tok_f = _start_copies("gather_ffn2_start", [a for a, _ in ffn2_items],
                                            [landing(a, kind) for a, kind in ffn2_items], relay[1],
                                            RELAY_COPIES[1] * len(ffn2_items), (tok_m,))
    ffn2_lands = {}

    def relay_ffn2(point, value):
        _, lands = _wait_copies("gather_ffn2_wait", f_s, f_r, f_thru, len(ffn2_items), relay[1], (value,))
        ffn2_lands["s"], ffn2_lands["r"], ffn2_lands["thru"], tok = _start_copies(
            "gather_ffn2_relay_start", [], lands, relay[2], RELAY_COPIES[2] * len(ffn2_items))
        return (tok,)

    full ={"ffn1_w_in": g_w1in, "ffn1_w_out": g_w1out.reshape(FF, D), "final_norm": final_norm.reshape(1, D)}
    for n in ("ffn1_norm", "mix_norm", "ffn2_norm", "pool_scale", "b_alpha", "gla_norm"):
        full[n] = p[n]

    xs, tgt = x[0], loss_target[0]
    h1, n1, gu1 = _fwd_ffn1(xs, full, after=(tok_m, tok_f))
    g_wmix, g_wo, g_wpool, g_walpha = gather_end(mix_state, (h1,))
    walpha = jnp.transpose(g_walpha, (1, 0, 2)).reshape(RANK, H * DK)
    full.update({
        "w_in_mix": g_wmix.reshape(D_IN, D),
        "w_in_mix_tail": jnp.pad(g_wmix[NDEV - 1, MIX_SHARD - RANK:], ((0, RANK_PAD - RANK), (0, 0))),
        "w_out_mix": g_wo.reshape(D, D),
        "w_pool": jnp.transpose(g_wpool.reshape(NDEV, H, 32, PG), (1, 0, 2, 3)).reshape(H, PG, PG).astype(BF16),
        "w_alpha": jnp.pad(walpha, ((0, RANK_PAD - RANK), (0, 0))).astype(BF16),
    })
    h2, sv = _fwd_mixer(h1, full, at=relay_ffn2)
    _, lands = _wait_copies("gather_ffn2_relay_wait", ffn2_lands["s"], ffn2_lands["r"], ffn2_lands["thru"], 0, relay[2], (h2,))
    g_w2in, g_w2out = _relay_finish(lands, ffn2_kinds, ffn2_widths, "gather_ffn2_finish")
    full.update({"ffn2_w_in": g_w2in, "ffn2_w_out": g_w2out.reshape(FF, D)})
    dh3, d_final, loss_part, n3, gu3 = _fwd_ffn2_loss(h2, tgt, full)


    def slab_shape(a, kind):
        return (NCHIP,) + (a.shape[1:] if kind == "scatter" else (a.shape[0], a.shape[1] // NDEV))

    def pair_add_all(own, got, tag):
        return [_pair_add(a, kind, got_a, table, tr, "%s_pair_add_%d" % (tag, i))
                for i, ((a, kind, tr), got_a) in enumerate(zip(own, got))]

    def reduce_begin(own, tag, after=()):
        kinds = [kind for _, kind, _ in own]
        copies = _pair_copies(kinds)
        s, r, thru, tok = _start_copies(tag + "_pair_start", [a for a, _, _ in own],
                                        [lax.empty(slab_shape(a, kind), a.dtype) for a, kind, _ in own], copies,
                                        NCHIP * len(own), after)
        return dict(own=own, copies=copies, s=s, r=r, thru=thru, tag=tag), tok

    def reduce_middle(st, after):
        own, tag = st["own"], st["tag"]
        sent, got = _wait_copies(tag + "_pair_wait", st["s"], st["r"], st["thru"], len(own), st["copies"], after)
        pre = pair_add_all([(a, kind, tr) for a, (_, kind, tr) in zip(sent, own)], got, tag)
        st["s"], st["r"], st["thru"], tok = _start_copies(tag + "_chip_start", pre, [lax.empty(a.shape, a.dtype) for a in pre],
                                                          _chip_copies, (NCHIP - 1) * len(pre))
        return tok

    def reduce_end(st, after):
        n = len(st["own"])
        pre, land = _wait_copies(st["tag"] + "_chip_wait", st["s"], st["r"], st["thru"], n, _chip_copies, after)
        return [[(a, 0)] + [(b, rel) for rel in range(1, NCHIP)] for a, b in zip(pre, land)]

    def w_in_item(a):
        return (a, "scatter_cols", 512)

    def w_out_item(a):
        return (a.reshape(NDEV, WOUT_SHARD, D), "scatter", WOUT_SHARD)

    red, small = {}, {}

    def at(point, value):
        if point == "ffn2_w_out":
            red["w2out"], tok = reduce_begin([w_out_item(value)], "ffn2_w_out")
        elif point == "ffn2_w_in":
            tok_a = reduce_middle(red["w2out"], (value,))
            red["w2in"], tok = reduce_begin([w_in_item(value)], "ffn2_w_in", (tok_a,))
        elif point == "pool_bwd":
            tok = reduce_middle(red["w2in"], (value,))
        elif point == "mixer_weights":
            d_wmix8 = value["w_in_mix"].reshape(NDEV, MIX_SHARD, D)
            d_wpool8 = jnp.transpose(value["w_pool"].reshape(H, NDEV, 32, PG), (1, 0, 2, 3)).reshape(NDEV, H * 32, PG)
            d_walpha8 = jnp.transpose(value["w_alpha"][:RANK].reshape(RANK, NDEV, H * DK // NDEV), (1, 0, 2))
            red["mix"], tok = reduce_begin([(d_wmix8, "scatter", MIX_SHARD),
                                            (value["w_out_mix"].reshape(NDEV, D // NDEV, D), "scatter", D // NDEV),
                                            (d_wpool8, "scatter", H * 32), (d_walpha8, "scatter", RANK)], "mix")
        elif point == "dh1":
            tok = reduce_middle(red["mix"], (value,))
        elif point == "ffn1_w_out":
            red["w1out"], tok = reduce_begin([w_out_item(value)], "ffn1_w_out")
        elif point == "ffn1_dx_a":
            tok = reduce_middle(red["w1out"], (value,))
        elif point == "ffn1_w_in_top":
            red["w1in_top"], tok = reduce_begin([w_in_item(value)], "ffn1_w_in_top")
        elif point == "ffn1_w_in_bottom":
            tok_a = reduce_middle(red["w1in_top"], (value,))
            red["w1in_bottom"], tok = reduce_begin([w_in_item(value)], "ffn1_w_in_bottom", (tok_a,))
        return (tok,)

    dx, g = _backward(xs, h1, h2, n1, gu1, sv, n3, gu3, dh3, full, at)
    packed = _pack_small(dict(final_norm=d_final, **g), loss_part[0, 0])
    small_s, small_r, small_thru, tok_s = _start_copies("gather_small_start", [packed], [lax.empty((NDEV,) + packed.shape, F32)],
                                                        _all_copies, NDEV, (dx,))
    tok_c = reduce_middle(red["w1in_bottom"], (tok_s,))

    def upd(parts, n, shape2d, tr):
        res = _adamw(parts, p[n].reshape(shape2d), m[n].reshape(shape2d), v[n].reshape(shape2d), tr, "adamw_" + n)
        return [r.reshape(p[n].shape) for r in res]

    def transposed(a):
        return jnp.transpose(a[0])

    (p_w2out,) = reduce_end(red["w2out"], (tok_c,))
    (p_w2in,) = reduce_end(red["w2in"], (tok_c,))
    p_wmix, p_wo, p_wpool, p_walpha = reduce_end(red["mix"], (tok_c,))
    out = {
        "ffn2_w_in": upd(p_w2in, "ffn2_w_in", (D, WIN_SHARD), 128),
        "ffn2_w_out": upd(p_w2out, "ffn2_w_out", (WOUT_SHARD, D), 64),
        "w_out_mix": upd(p_wo, "w_out_mix", (D // NDEV, D), 64),
        "w_pool": upd(p_wpool, "w_pool", (H * 32, PG), H * 32),
        "w_alpha": upd(p_walpha, "w_alpha", (RANK, H * DK // NDEV), RANK),
    }
    out["w_in_mix"] = [jnp.transpose(r)[None] for r in
                       _adamw(p_wmix, transposed(p["w_in_mix"]), transposed(m["w_in_mix"]), transposed(v["w_in_mix"]),
                              MIX_SHARD, "adamw_w_in_mix", tc=512)]
    _, (r_small,) = _wait_copies("gather_small_wait", small_s, small_r, small_thru, 1, _all_copies,
                                 (out["w_in_mix"][3], out["ffn2_w_in"][3], out["ffn2_w_out"][3], out["w_out_mix"][3]))
    small_res = _adamw([(r_small, s) for s in range(NDEV)], _pack_small(p), _pack_small(m), _pack_small(v), SMALL_ROWS,
                       "adamw_small")
    (p_w1out,) = reduce_end(red["w1out"], (small_res[0],))
    out["ffn1_w_out"] = upd(p_w1out, "ffn1_w_out", (WOUT_SHARD, D), 64)
    w1in = [a.reshape(D, WIN_SHARD) for a in (p["ffn1_w_in"], m["ffn1_w_in"], v["ffn1_w_in"])]
    (p_top,) = reduce_end(red["w1in_top"], (out["ffn1_w_out"][3],))
    top = _adamw(p_top, *w1in, 128, "adamw_ffn1_w_in_top")
    (p_bottom,) = reduce_end(red["w1in_bottom"], (top[3],))
    out["ffn1_w_in"] = [r.reshape(p["ffn1_w_in"].shape) for r in
                        _adamw(p_bottom, *w1in, 128, "adamw_ffn1_w_in_bottom", first_row=D // 2, into=top)]
    unpacked = [_unpack_small(r, p) for r in small_res]
    loss = unpacked[0][1]
    for n, _ in SMALL:
        out[n] = [u[0][n] for u in unpacked]

    return (loss, dx.reshape(1, S, D), *[out[n][0] for n in names], *[out[n][1] for n in names],
            *[out[n][2] for n in names], *[out[n][3] for n in names])
```

```python
import jax
import jax.numpy as jnp
from jax import lax
from jax.experimental import pallas as pl
from jax.experimental.pallas import tpu as pltpu

F32, BF16 = jnp.float32, jnp.bfloat16
MESH = pl.DeviceIdType.MESH
ANY = pl.BlockSpec(memory_space=pl.ANY)

NDEV = 8
S = 2048
D = 2048
FF = 5632
WIN_SHARD = 2 * FF // NDEV
WOUT_SHARD = FF // NDEV
D_POOL = 1024
PG = 256
POOL_WINDOWS = (2, 4, 8, 16)
H = 4
DK = 128
DV = 256
CH = 64
NCH = S // CH
RANK = 16
RANK_PAD = 128
D_IN = 4112
D_IN_PAD = 4224
MIX_SHARD = D_IN // NDEV
O_Q, O_K, O_V, O_G, O_R = 1024, 1536, 2048, 3072, 4096
GATE_NORM = 16.0
QK_SCALE = DK ** -0.5
EPS = 1e-6
ADAM_LR, ADAM_B1, ADAM_B2, ADAM_EPS, ADAM_WD, ADAM_STEP = 0.001, 0.9, 0.999, 1e-08, 0.01, 10
V7X_VMEM_BYTES = 64 << 20

SMALL = (("ffn1_norm", 2048), ("mix_norm", 2048), ("ffn2_norm", 2048), ("final_norm", 2048),
         ("pool_scale", 1024), ("b_alpha", 512), ("gla_norm", 256))
SMALL_ROWS = 80
LOSS_AT = sum(n for _, n in SMALL)


def _params(vmem_mb, sem=None):
    return pltpu.CompilerParams(dimension_semantics=sem, vmem_limit_bytes=min(vmem_mb << 20, V7X_VMEM_BYTES - (4 << 20)))


def _pallas(body, **kwargs):
    call = pl.pallas_call(body, **kwargs)

    def run(*operands):
        return call(*[pltpu.with_memory_space_constraint(a, pltpu.HBM) if a.size * a.dtype.itemsize >= 1 << 18 else a
                      for a in operands])
    return run


def _dot(a, b):
    return jnp.dot(a, b, preferred_element_type=F32)


def _dot_nt(a, b):
    return lax.dot_general(a, b, (((1,), (1,)), ((), ())), preferred_element_type=F32)


def _dot_tn(a, b):
    return lax.dot_general(a, b, (((0,), (0,)), ((), ())), preferred_element_type=F32)


def _sigmoid(x):
    return 0.5 * jnp.tanh(0.5 * x) + 0.5


def _log_sigmoid(x):
    return jnp.minimum(x, 0.0) - jnp.log(1.0 + jnp.exp(-jnp.abs(x)))


def _rms(x, g):
    r = lax.rsqrt(jnp.mean(x * x, axis=-1, keepdims=True) + EPS)
    return x * r * g


def _rms_bwd(dn, x, g):
    r = lax.rsqrt(jnp.mean(x * x, axis=-1, keepdims=True) + EPS)
    xh = x * r
    dxh = dn * g
    dx = r * (dxh - xh * jnp.mean(dxh * xh, axis=-1, keepdims=True))
    return dx, jnp.sum(dn * xh, axis=0, keepdims=True)


ROWS = 64


def _row_loop(total, fn, init=0):
    def step(t, carry):
        return fn(pl.ds(pl.multiple_of(t * ROWS, ROWS), ROWS), carry)
    return lax.fori_loop(0, total // ROWS, step, init)


def _split3(x):
    hi = x.astype(BF16)
    r1 = x - hi.astype(F32)
    mid = r1.astype(BF16)
    lo = (r1 - mid.astype(F32)).astype(BF16)
    return hi, mid, lo


def _tri_dot(tri_b, x):
    hi, mid, lo = _split3(x)
    return (_dot(tri_b, lo) + _dot(tri_b, mid)) + _dot(tri_b, hi)


FFN_TS, FFN_TF = 512, 512


def _ffn_specs():
    wg = pl.BlockSpec((D, FFN_TF), lambda i, j: (0, j))
    wu = pl.BlockSpec((D, FFN_TF), lambda i, j: (0, FF // FFN_TF + j))
    wo = pl.BlockSpec((FFN_TF, D), lambda i, j: (j, 0))
    row = pl.BlockSpec((FFN_TS, D), lambda i, j: (i, 0))
    vec = pl.BlockSpec((1, D), lambda i, j: (0, 0))
    gu = pl.BlockSpec((2, FFN_TS, FFN_TF), lambda i, j: (0, i, j))
    return wg, wu, wo, row, vec, gu


def _ordered_after(body, n_in, after):
    def wrapped(*refs):
        return body(*refs[:n_in], *refs[n_in + len(after):])
    return wrapped, [ANY] * len(after)


def _ffn_fwd(h, g, w_in8, w_out, name, after=(), head=None):
    nj = FF // FFN_TF
    wg, wu, wo, row, vec, gu = _ffn_specs()
    n_in = 5 if head is None else 7

    def body(*refs):
        h_ref, g_ref, wg_ref, wu_ref, wo_ref = refs[:5]
        ho_ref, n_ref, gu_ref = refs[n_in:n_in + 3]
        acc_ref = refs[-1]
        i, j = pl.program_id(0), pl.program_id(1)

        @pl.when(j == 0)
        def _():
            def norm(rows, c):
                n_ref[rows, :] = _rms(h_ref[rows, :], g_ref[...]).astype(BF16)
                return c
            _row_loop(FFN_TS, norm)
            acc_ref[...] = jnp.zeros_like(acc_ref)

        n = n_ref[...]
        gate = _dot(n, wg_ref[...])
        up = _dot(n, wu_ref[...])
        gu_ref[0] = gate.astype(BF16)
        gu_ref[1] = up.astype(BF16)
        a = (gate * _sigmoid(gate)) * up
        acc_ref[...] += _dot(a.astype(BF16), wo_ref[...])

        if head is None:
            @pl.when(j == nj - 1)
            def _():
                def residual(rows, c):
                    ho_ref[rows, :] = h_ref[rows, :] + 0.5 * acc_ref[rows, :]
                    return c
                _row_loop(FFN_TS, residual)
        else:
            gf_ref, t_ref = refs[5:7]
            dgf_ref, loss_ref = refs[n_in + 3:n_in + 5]

            @pl.when(j == nj - 1)
            def _():
                def rows_fn(rows, carry):
                    dg, part = carry
                    x = h_ref[rows, :] + 0.5 * acc_ref[rows, :]
                    gv = gf_ref[...]
                    err = _rms(x, gv) - t_ref[rows, :]
                    part = part + 0.5 * jnp.sum(jnp.mean(err * err, axis=-1, keepdims=True), axis=0, keepdims=True)
                    dx, dg_rows = _rms_bwd(err * (1.0 / D), x, gv)
                    ho_ref[rows, :] = dx
                    return dg + dg_rows, part
                dg, part = _row_loop(FFN_TS, rows_fn, (jnp.zeros((1, D), F32), jnp.zeros((1, 1), F32)))

                @pl.when(i == 0)
                def _():
                    dgf_ref[...] = dg
                    loss_ref[...] = jnp.broadcast_to(part, loss_ref.shape)

                @pl.when(i > 0)
                def _():
                    dgf_ref[...] += dg
                    loss_ref[...] += jnp.broadcast_to(part, loss_ref.shape)

    body, extra = _ordered_after(body, n_in, after)
    head_in = [] if head is None else [vec, row]
    head_out = [] if head is None else [vec, pl.BlockSpec((1, 128), lambda i, j: (0, 0))]
    head_shape = [] if head is None else [jax.ShapeDtypeStruct((1, D), F32), jax.ShapeDtypeStruct((1, 128), F32)]
    return _pallas(
        body, grid=(S // FFN_TS, nj), name=name,
        in_specs=[row, vec, wg, wu, wo] + head_in + extra, out_specs=[row, row, gu] + head_out,
        out_shape=[jax.ShapeDtypeStruct((S, D), F32), jax.ShapeDtypeStruct((S, D), BF16),
                   jax.ShapeDtypeStruct((2, S, FF), BF16)] + head_shape,
        scratch_shapes=[pltpu.VMEM((FFN_TS, D), F32)],
        compiler_params=_params(58, ("arbitrary", "arbitrary")),
    )(h, g, w_in8, w_in8, w_out, *(head or ()), *after)


def _ffn_bwd_x(dhp, h, g, gu_arr, w_in8, w_out, name, after=()):
    ni, nj = S // FFN_TS, FF // FFN_TF
    wg, wu, wo, row, vec, gu = _ffn_specs()
    act = pl.BlockSpec((FFN_TS, FFN_TF), lambda i, j: (i, j))

    def body(dhp_ref, h_ref, g_ref, gu_ref, wg_ref, wu_ref, wo_ref,
             dgu_ref, a_ref, df_ref, dh_ref, dhb_ref, dg_ref, acc_ref):
        i, j = pl.program_id(0), pl.program_id(1)

        @pl.when(j == 0)
        def _():
            def half(rows, c):
                df_ref[rows, :] = (0.5 * dhp_ref[rows, :]).astype(BF16)
                return c
            _row_loop(FFN_TS, half)
            acc_ref[...] = jnp.zeros_like(acc_ref)

        gate = gu_ref[0].astype(F32)
        up = gu_ref[1].astype(F32)
        da = _dot_nt(df_ref[...], wo_ref[...])
        sg = _sigmoid(gate)
        silu = gate * sg
        dgate = (da * up * (sg * (1.0 + gate * (1.0 - sg)))).astype(BF16)
        dup = (da * silu).astype(BF16)
        a_ref[...] = (silu * up).astype(BF16)
        dgu_ref[0] = dgate
        dgu_ref[1] = dup
        acc_ref[...] += _dot_nt(dgate, wg_ref[...]) + _dot_nt(dup, wu_ref[...])

        @pl.when(j == nj - 1)
        def _():
            def norm_bwd(rows, dg):
                dx, dg_rows = _rms_bwd(acc_ref[rows, :], h_ref[rows, :], g_ref[...])
                dh = dhp_ref[rows, :] + dx
                dh_ref[rows, :] = dh
                dhb_ref[rows, :] = dh.astype(BF16)
                return dg + dg_rows
            dg = _row_loop(FFN_TS, norm_bwd, jnp.zeros((1, D), F32))

            @pl.when(i == 0)
            def _():
                dg_ref[...] = dg

            @pl.when(i > 0)
            def _():
                dg_ref[...] += dg

    body, extra = _ordered_after(body, 7, after)
    return _pallas(
        body, grid=(ni, nj), name=name,
        in_specs=[row, row, vec, gu, wg, wu, wo] + extra,
        out_specs=[gu, act, row, row, row, vec],
        out_shape=[jax.ShapeDtypeStruct((2, S, FF), BF16), jax.ShapeDtypeStruct((S, FF), BF16),
                   jax.ShapeDtypeStruct((S, D), BF16), jax.ShapeDtypeStruct((S, D), F32),
                   jax.ShapeDtypeStruct((S, D), BF16), jax.ShapeDtypeStruct((1, D), F32)],
        scratch_shapes=[pltpu.VMEM((FFN_TS, D), F32)],
        compiler_params=_params(58, ("arbitrary", "arbitrary")),
    )(dhp, h, g, gu_arr, w_in8, w_in8, w_out, *after)


def _ffn_bwd_gu(df, gu_arr, w_out, name, after=()):
    ni = S // FFN_TS
    gu = pl.BlockSpec((2, FFN_TS, FFN_TF), lambda j, i: (0, i, j))
    wo = pl.BlockSpec((FFN_TF, D), lambda j, i: (j, 0))
    df_all = pl.BlockSpec((S, D), lambda j, i: (0, 0), pipeline_mode=pl.Buffered(1))

    def body(df_ref, gu_ref, wo_ref, dgu_ref, dwo_ref, acc_ref):
        i = pl.program_id(1)
        df_rows = df_ref[pl.ds(pl.multiple_of(i * FFN_TS, FFN_TS), FFN_TS), :]
        gate = gu_ref[0].astype(F32)
        up = gu_ref[1].astype(F32)
        da = _dot_nt(df_rows, wo_ref[...])
        sg = _sigmoid(gate)
        silu = gate * sg
        dgu_ref[0] = (da * up * (sg * (1.0 + gate * (1.0 - sg)))).astype(BF16)
        dgu_ref[1] = (da * silu).astype(BF16)
        part = _dot_tn((silu * up).astype(BF16), df_rows)

        @pl.when(i == 0)
        def _():
            acc_ref[...] = part

        @pl.when(i > 0)
        def _():
            acc_ref[...] += part

        @pl.when(i == ni - 1)
        def _():
            dwo_ref[...] = acc_ref[...].astype(BF16)

    body, extra = _ordered_after(body, 3, after)
    return _pallas(
        body, grid=(FF // FFN_TF, ni), name=name,
        in_specs=[df_all, gu, wo] + extra, out_specs=[gu, wo],
        out_shape=[jax.ShapeDtypeStruct((2, S, FF), BF16), jax.ShapeDtypeStruct((FF, D), BF16)],
        scratch_shapes=[pltpu.VMEM((FFN_TF, D), F32)],
        compiler_params=_params(40, ("arbitrary", "arbitrary")),
    )(df, gu_arr, w_out, *after)


def _ffn_bwd_dx(dgu, dhp, h, g, w_in, first_tile, ntiles, name, after=(), into=None):
    nj = FF // FFN_TF
    wg, wu, _, _, vec, _ = _ffn_specs()
    row_in = pl.BlockSpec((FFN_TS, D), lambda i, j: (first_tile + i, 0))
    dgu_spec = pl.BlockSpec((2, FFN_TS, FFN_TF), lambda i, j: (0, first_tile + i, j))
    after = tuple(after) + (() if into is None else (into,))

    def body(dgu_ref, dhp_ref, h_ref, g_ref, wg_ref, wu_ref, dh_ref, dg_ref, acc_ref):
        i, j = pl.program_id(0), pl.program_id(1)

        @pl.when(j == 0)
        def _():
            acc_ref[...] = jnp.zeros_like(acc_ref)

        acc_ref[...] += _dot_nt(dgu_ref[0], wg_ref[...]) + _dot_nt(dgu_ref[1], wu_ref[...])

        @pl.when(j == nj - 1)
        def _():
            def norm_bwd(rows, dg):
                dx, dg_rows = _rms_bwd(acc_ref[rows, :], h_ref[rows, :], g_ref[...])
                dh_ref[rows, :] = dhp_ref[rows, :] + dx
                return dg + dg_rows
            dg = _row_loop(FFN_TS, norm_bwd, jnp.zeros((1, D), F32))

            @pl.when(i == 0)
            def _():
                dg_ref[...] = dg

            @pl.when(i > 0)
            def _():
                dg_ref[...] += dg

    body, extra = _ordered_after(body, 6, after)
    return _pallas(
        body, grid=(ntiles, nj), name=name,
        in_specs=[dgu_spec, row_in, row_in, vec, wg, wu] + extra, out_specs=[row_in, vec],
        out_shape=[jax.ShapeDtypeStruct((S, D), F32), jax.ShapeDtypeStruct((1, D), F32)],
        input_output_aliases={} if into is None else {6 + len(after) - 1: 0},
        scratch_shapes=[pltpu.VMEM((FFN_TS, D), F32)],
        compiler_params=_params(48, ("arbitrary", "arbitrary")),
    )(dgu, dhp, h, g, w_in, w_in, *after)


def _tn_matmul(a, b, a_spec, b_spec, out_shape, out_spec, grid, name, vmem_mb, after=()):
    def body(a_ref, b_ref, o_ref):
        o_ref[...] = _dot_tn(a_ref[...], b_ref[...]).astype(o_ref.dtype)

    body, extra = _ordered_after(body, 2, after)
    return _pallas(body, grid=grid, name=name, in_specs=[a_spec, b_spec] + extra, out_specs=out_spec,
                          out_shape=out_shape, compiler_params=_params(vmem_mb))(a, b, *after)


def _resident(shape):
    return pl.BlockSpec(shape, lambda i: (0,) * len(shape), pipeline_mode=pl.Buffered(1))


def _norm_matmul(h, g, w, w_tail, name, after=(), ts=256):
    main = w.shape[0] // 128 * 128
    n_out = main + w_tail.shape[0]

    def body(h_ref, g_ref, w_ref, wt_ref, u_ref, n_ref):
        def norm(rows, c):
            n_ref[rows, :] = _rms(h_ref[rows, :], g_ref[...]).astype(BF16)
            return c
        _row_loop(ts, norm)
        n = n_ref[...]
        u_ref[:, :main] = _dot_nt(n, w_ref[:main, :])
        u_ref[:, main:] = _dot_nt(n, wt_ref[...])

    body, extra = _ordered_after(body, 4, after)
    return _pallas(
        body, grid=(S // ts,), name=name,
        in_specs=[pl.BlockSpec((ts, D), lambda i: (i, 0)), pl.BlockSpec((1, D), lambda i: (0, 0)), _resident(w.shape),
                  _resident(w_tail.shape)] + extra,
        out_specs=[pl.BlockSpec((ts, n_out), lambda i: (i, 0)), pl.BlockSpec((ts, D), lambda i: (i, 0))],
        out_shape=[jax.ShapeDtypeStruct((S, n_out), F32), jax.ShapeDtypeStruct((S, D), BF16)],
        compiler_params=_params(48, ("arbitrary",)),
    )(h, g, w, w_tail, *after)


def _matmul_residual(a, w, res, name, after=(), ts=512):
    k, n_out = w.shape

    def body(a_ref, w_ref, r_ref, o_ref):
        o_ref[...] = r_ref[...] + _dot(a_ref[...], w_ref[...])

    body, extra = _ordered_after(body, 3, after)
    return _pallas(
        body, grid=(S // ts,), name=name,
        in_specs=[pl.BlockSpec((ts, k), lambda i: (i, 0)), _resident(w.shape), pl.BlockSpec((ts, n_out), lambda i: (i, 0))] + extra,
        out_specs=pl.BlockSpec((ts, n_out), lambda i: (i, 0)),
        out_shape=jax.ShapeDtypeStruct((S, n_out), F32),
        compiler_params=_params(40),
    )(a, w, res, *after)


def _nt_matmul(a, w, name, after=(), ts=512):
    n_out, k = w.shape

    def body(a_ref, w_ref, o_ref):
        o_ref[...] = _dot_nt(a_ref[...], w_ref[...])

    body, extra = _ordered_after(body, 2, after)
    return _pallas(
        body, grid=(S // ts,), name=name,
        in_specs=[pl.BlockSpec((ts, k), lambda i: (i, 0)), _resident(w.shape)] + extra,
        out_specs=pl.BlockSpec((ts, n_out), lambda i: (i, 0)),
        out_shape=jax.ShapeDtypeStruct((S, n_out), F32),
        compiler_params=_params(40),
    )(a, w, *after)


def _matmul_normbwd(du, w, w_tail, h, g, dres, name, after=(), ts=256):
    main = w.shape[0] // 128 * 128
    n_in = main + w_tail.shape[0]
    row = pl.BlockSpec((ts, D), lambda i: (i, 0))
    vec = pl.BlockSpec((1, D), lambda i: (0, 0))

    def body(du_ref, w_ref, wt_ref, h_ref, g_ref, dres_ref, dh_ref, dg_ref, half_ref, acc_ref):
        i = pl.program_id(0)
        acc_ref[...] = _dot(du_ref[:, :main], w_ref[:main, :]) + _dot(du_ref[:, main:], wt_ref[...])

        def norm_bwd(rows, dg):
            dx, dg_rows = _rms_bwd(acc_ref[rows, :], h_ref[rows, :], g_ref[...])
            dh = dres_ref[rows, :] + dx
            dh_ref[rows, :] = dh
            half_ref[rows, :] = (0.5 * dh).astype(BF16)
            return dg + dg_rows
        dg = _row_loop(ts, norm_bwd, jnp.zeros((1, D), F32))

        @pl.when(i == 0)
        def _():
            dg_ref[...] = dg

        @pl.when(i > 0)
        def _():
            dg_ref[...] += dg

    body, extra = _ordered_after(body, 6, after)
    return _pallas(
        body, grid=(S // ts,), name=name,
        in_specs=[pl.BlockSpec((ts, n_in), lambda i: (i, 0)), _resident(w.shape), _resident(w_tail.shape), row, vec, row] + extra,
        out_specs=[row, vec, row],
        out_shape=[jax.ShapeDtypeStruct((S, D), F32), jax.ShapeDtypeStruct((1, D), F32), jax.ShapeDtypeStruct((S, D), BF16)],
        scratch_shapes=[pltpu.VMEM((ts, D), F32)],
        compiler_params=_params(52, ("arbitrary",)),
    )(du, w, w_tail, h, g, dres, *after)


def _pool_specs():
    blk = pl.BlockSpec((S, PG), lambda gi: (0, gi))
    wp = pl.BlockSpec((None, PG, PG), lambda gi: (gi, 0, 0))
    sc = pl.BlockSpec((1, PG), lambda gi: (0, gi))
    return blk, wp, sc


def _pool_fwd(u, wp_b, scale):
    blk, wp, sc = _pool_specs()

    def body(u_ref, wp_ref, sc_ref, y_ref, pooled_ref):
        win = 2 << pl.program_id(0)
        row = lax.broadcasted_iota(jnp.int32, (S, PG), 0)
        x = u_ref[...]
        s = x
        for k in (1, 2, 4, 8):
            s = s + jnp.where((row >= k) & (k < win), pltpu.roll(s, k, 0), 0.0)
        cnt = jnp.minimum(row + 1, win).astype(F32)
        pooled = (s / cnt - x).astype(BF16)
        pooled_ref[...] = pooled
        y_ref[...] = (_dot(pooled, wp_ref[...]) * sc_ref[...]).astype(BF16)

    return _pallas(
        body, grid=(len(POOL_WINDOWS),), name="pool_fwd", in_specs=[blk, wp, sc], out_specs=[blk, blk],
        out_shape=[jax.ShapeDtypeStruct((S, D_POOL), BF16), jax.ShapeDtypeStruct((S, D_POOL), BF16)],
        compiler_params=_params(40),
    )(u, wp_b, scale)


def _pool_bwd(dy, pooled, wp_b, scale):
    blk, wp, sc = _pool_specs()

    def body(dy_ref, p_ref, wp_ref, sc_ref, du_ref, dwp_ref, dsc_ref):
        win = 2 << pl.program_id(0)
        row = lax.broadcasted_iota(jnp.int32, (S, PG), 0)
        dyv = dy_ref[...]
        pooled = p_ref[...]
        w = wp_ref[...]
        dsc_ref[...] = jnp.sum(dyv * _dot(pooled, w), axis=0, keepdims=True)
        dz = (dyv * sc_ref[...]).astype(BF16)
        dwp_ref[...] = _dot_tn(pooled, dz)
        dpooled = _dot_nt(dz, w)
        cnt = jnp.minimum(row + 1, win).astype(F32)
        fs = dpooled / cnt
        for k in (1, 2, 4, 8):
            fs = fs + jnp.where((row < S - k) & (k < win), pltpu.roll(fs, S - k, 0), 0.0)
        du_ref[...] = (fs - dpooled).astype(BF16)

    return _pallas(
        body, grid=(len(POOL_WINDOWS),), name="pool_bwd", in_specs=[blk, blk, wp, sc], out_specs=[blk, wp, sc],
        out_shape=[jax.ShapeDtypeStruct((S, D_POOL), BF16), jax.ShapeDtypeStruct((len(POOL_WINDOWS), PG, PG), F32),
                   jax.ShapeDtypeStruct((1, D_POOL), F32)],
        compiler_params=_params(40),
    )(dy, pooled, wp_b, scale)


GLA_CPS = 2
GLA_ROWS = GLA_CPS * CH
GLA_STEPS = NCH // GLA_CPS


def _gla_in_specs(step_of):
    def at(width, col):
        return pl.BlockSpec((GLA_ROWS, width), lambda n: (step_of(n), col))
    return [at(H * DK, O_Q // (H * DK)), at(H * DK, O_K // (H * DK)), at(H * DV, O_V // (H * DV)),
            at(H * DV, O_G // (H * DV)), at(RANK_PAD, O_R // RANK_PAD)]


def _gla_decay_terms(lr, wa_ref, ba_ref, q, k):
    row = lax.broadcasted_iota(jnp.int32, (CH, CH), 0)
    col = lax.broadcasted_iota(jnp.int32, (CH, CH), 1)
    tril = row >= col
    z = _dot(lr.astype(BF16), wa_ref[...]) + ba_ref[...]
    la = _log_sigmoid(z) / GATE_NORM
    b = _tri_dot(jnp.where(tril, 1.0, 0.0).astype(BF16), la)
    bl = jnp.sum(la, axis=0, keepdims=True)
    e_b, e_nb, e_tb = jnp.exp(b), jnp.exp(-b), jnp.exp(bl - b)
    q_dec = (q * QK_SCALE) * e_b
    return tril, z, e_b, e_nb, e_tb, jnp.exp(bl), q_dec, k * e_nb, k * e_tb


def _gla_fwd(u, wa_b, ba, gn):
    wide = pl.BlockSpec((GLA_ROWS, H * DV), lambda n: (n, 0))

    def body(q_ref, k_ref, v_ref, g_ref, lr_ref, wa_ref, ba_ref, gn_ref, y_ref, o_ref, st_ref, state):
        @pl.when(pl.program_id(0) == 0)
        def _():
            state[...] = jnp.zeros_like(state)

        for c in range(GLA_CPS):
            rows = slice(c * CH, (c + 1) * CH)
            tril, _, _, _, _, dec, q_dec, k_inv, k_tail = _gla_decay_terms(lr_ref[rows, :], wa_ref, ba_ref, q_ref[rows, :],
                                                                          k_ref[rows, :])
            for hd in range(H):
                ks, vs = slice(hd * DK, (hd + 1) * DK), slice(hd * DV, (hd + 1) * DV)
                qb, kib, ktb = q_dec[:, ks].astype(BF16), k_inv[:, ks].astype(BF16), k_tail[:, ks].astype(BF16)
                vb = v_ref[rows, vs].astype(BF16)
                p = jnp.where(tril, _dot_nt(qb, kib), 0.0)
                st = state[hd]
                o = _dot(p.astype(BF16), vb) + _dot_nt(qb, st.astype(BF16))
                st_ref[c, hd] = st
                state[hd] = st * dec[:, ks] + _dot_tn(vb, ktb)
                o_ref[rows, vs] = o
                on = _rms(o, gn_ref[...])
                gg = g_ref[rows, vs]
                y_ref[rows, vs] = (on * (gg * _sigmoid(gg))).astype(BF16)

    return _pallas(
        body, grid=(GLA_STEPS,), name="gla_fwd",
        in_specs=_gla_in_specs(lambda n: n) + [pl.BlockSpec((RANK_PAD, H * DK), lambda n: (0, 0)),
                                               pl.BlockSpec((1, H * DK), lambda n: (0, 0)),
                                               pl.BlockSpec((1, DV), lambda n: (0, 0))],
        out_specs=[wide, wide, pl.BlockSpec((GLA_CPS, H, DV, DK), lambda n: (n, 0, 0, 0))],
        out_shape=[jax.ShapeDtypeStruct((S, H * DV), BF16), jax.ShapeDtypeStruct((S, H * DV), F32),
                   jax.ShapeDtypeStruct((NCH, H, DV, DK), F32)],
        scratch_shapes=[pltpu.VMEM((H, DV, DK), F32)],
        compiler_params=_params(32, ("arbitrary",)),
    )(u, u, u, u, u, wa_b, ba, gn)


GLA_DU = 2 * H * DK + 2 * H * DV + RANK_PAD


def _gla_bwd(u, o_arr, states, dy, wa_b, ba, gn, after=()):
    rev = lambda n: GLA_STEPS - 1 - n
    wide = pl.BlockSpec((GLA_ROWS, H * DV), lambda n: (rev(n), 0))

    def body(q_ref, k_ref, v_ref, g_ref, lr_ref, o_ref, st_ref, dy_ref, wa_ref, ba_ref, gn_ref,
             du_ref, dwa_ref, dba_ref, dgn_ref, gstate, db_scr, dbl_scr):
        @pl.when(pl.program_id(0) == 0)
        def _():
            gstate[...] = jnp.zeros_like(gstate)
            dwa_ref[...] = jnp.zeros_like(dwa_ref)
            dba_ref[...] = jnp.zeros_like(dba_ref)
            dgn_ref[...] = jnp.zeros_like(dgn_ref)

        gnv = gn_ref[...]
        dgn = jnp.zeros((1, DV), F32)
        dwa = jnp.zeros((RANK_PAD, H * DK), F32)
        dba = jnp.zeros((1, H * DK), F32)
        srow = lax.broadcasted_iota(jnp.int32, (CH, CH), 0)
        scol = lax.broadcasted_iota(jnp.int32, (CH, CH), 1)
        for c in reversed(range(GLA_CPS)):
            rows = slice(c * CH, (c + 1) * CH)
            lr = lr_ref[rows, :]
            tril, z, e_b, e_nb, e_tb, dec, q_dec, k_inv, k_tail = _gla_decay_terms(lr, wa_ref, ba_ref, q_ref[rows, :],
                                                                                  k_ref[rows, :])
            for hd in range(H):
                ks, vs = slice(hd * DK, (hd + 1) * DK), slice(hd * DV, (hd + 1) * DV)
                qh, kih, kth = q_dec[:, ks], k_inv[:, ks], k_tail[:, ks]
                qb, kib, ktb = qh.astype(BF16), kih.astype(BF16), kth.astype(BF16)
                vb = v_ref[rows, vs].astype(BF16)
                o = o_ref[rows, vs]
                gg = g_ref[rows, vs]
                dyh = dy_ref[rows, vs]
                r = lax.rsqrt(jnp.mean(o * o, axis=-1, keepdims=True) + EPS)
                xh = o * r
                sg = _sigmoid(gg)
                dgate = dyh * (xh * gnv) * (sg * (1.0 + gg * (1.0 - sg)))
                don = dyh * (gg * sg)
                dgn = dgn + jnp.sum(don * xh, axis=0, keepdims=True)
                dxh = don * gnv
                d_o = (r * (dxh - xh * jnp.mean(dxh * xh, axis=-1, keepdims=True))).astype(BF16)
                pb = jnp.where(tril, _dot_nt(qb, kib), 0.0).astype(BF16)
                dpb = jnp.where(tril, _dot_nt(d_o, vb), 0.0).astype(BF16)
                gt = gstate[hd]
                gtb = gt.astype(BF16)
                st = st_ref[c, hd]
                dv = _dot_tn(pb, d_o) + _dot_nt(ktb, gtb)
                dq_dec = _dot(dpb, kib) + _dot(d_o, st.astype(BF16))
                dk_inv = _dot_tn(dpb, qb)
                dk_tail = _dot(vb, gtb)
                ddec = jnp.sum(gt * st, axis=0, keepdims=True)
                gstate[hd] = _dot_tn(d_o, qb) + dec[:, ks] * gt
                du_ref[rows, ks] = (dq_dec * QK_SCALE * e_b[:, ks]).astype(BF16)
                du_ref[rows, H * DK + hd * DK:H * DK + (hd + 1) * DK] = (dk_inv * e_nb[:, ks] + dk_tail * e_tb[:, ks]).astype(BF16)
                du_ref[rows, 2 * H * DK + hd * DV:2 * H * DK + (hd + 1) * DV] = dv.astype(BF16)
                du_ref[rows, 2 * H * DK + H * DV + hd * DV:2 * H * DK + H * DV + (hd + 1) * DV] = dgate.astype(BF16)
                db_scr[c, :, ks] = dq_dec * qh - dk_inv * kih - dk_tail * kth
                dbl_scr[c, :, ks] = jnp.sum(dk_tail * kth, axis=0, keepdims=True) + ddec * dec[:, ks]
            dla = _tri_dot(jnp.where(srow <= scol, 1.0, 0.0).astype(BF16), db_scr[c]) + dbl_scr[c]
            dz = dla * (1.0 / GATE_NORM) * _sigmoid(-z)
            dzb = dz.astype(BF16)
            du_ref[rows, GLA_DU - RANK_PAD:] = _dot_nt(dzb, wa_ref[...]).astype(BF16)
            dwa = dwa + _dot_tn(lr.astype(BF16), dzb)
            dba = dba + jnp.sum(dz, axis=0, keepdims=True)
        dgn_ref[...] += dgn
        dwa_ref[...] += dwa
        dba_ref[...] += dba

    full = lambda shape: pl.BlockSpec(shape, lambda n: (0,) * len(shape))
    body, extra = _ordered_after(body, 11, after)
    return _pallas(
        body, grid=(GLA_STEPS,), name="gla_bwd",
        in_specs=_gla_in_specs(rev) + [wide, pl.BlockSpec((GLA_CPS, H, DV, DK), lambda n: (rev(n), 0, 0, 0)),
                                       pl.BlockSpec((GLA_ROWS, H * DV), lambda n: (rev(n), 1)),
                                       full((RANK_PAD, H * DK)), full((1, H * DK)), full((1, DV))] + extra,
        out_specs=[pl.BlockSpec((GLA_ROWS, GLA_DU), lambda n: (rev(n), 0)), full((RANK_PAD, H * DK)), full((1, H * DK)),
                   full((1, DV))],
        out_shape=[jax.ShapeDtypeStruct((S, GLA_DU), BF16), jax.ShapeDtypeStruct((RANK_PAD, H * DK), F32),
                   jax.ShapeDtypeStruct((1, H * DK), F32), jax.ShapeDtypeStruct((1, DV), F32)],
        scratch_shapes=[pltpu.VMEM((H, DV, DK), F32), pltpu.VMEM((GLA_CPS, CH, H * DK), F32),
                        pltpu.VMEM((GLA_CPS, 1, H * DK), F32)],
        compiler_params=_params(32, ("arbitrary",)),
    )(u, u, u, u, u, o_arr, states, dy, wa_b, ba, gn, *after)


def _ffn_dw_in(n, dgu, tag, after=(), first_tile=0, ntiles=D // 512):
    return _tn_matmul(n, dgu, pl.BlockSpec((S, 512), lambda s, m: (0, first_tile + m)),
                      pl.BlockSpec((None, S, WIN_SHARD), lambda s, m: (s // (NDEV // 2), 0, s % (NDEV // 2))),
                      jax.ShapeDtypeStruct((512 * ntiles, 2 * FF), BF16), pl.BlockSpec((512, WIN_SHARD), lambda s, m: (m, s)),
                      (NDEV, ntiles), tag + "_dw_in", 32, after)


def _ffn_dw_out(act, df, tag, after=()):
    return _tn_matmul(act, df, pl.BlockSpec((S, 512), lambda m: (0, m)), pl.BlockSpec((S, D), lambda m: (0, 0)),
                      jax.ShapeDtypeStruct((FF, D), BF16), pl.BlockSpec((512, D), lambda m: (m, 0)),
                      (FF // 512,), tag + "_dw_out", 40, after)


def _fwd_ffn1(x, w, after=()):
    return _ffn_fwd(x, w["ffn1_norm"], w["ffn1_w_in"], w["ffn1_w_out"], "ffn1_fwd", after)


def _fwd_mixer(h1, w, after=(), at=lambda point, value: ()):
    u, n2 = _norm_matmul(h1, w["mix_norm"], w["w_in_mix"], w["w_in_mix_tail"], "mix_in", after)
    y_pool, pooled = _pool_fwd(u, w["w_pool"], w["pool_scale"])
    y_gla, o_gla, states = _gla_fwd(u, w["w_alpha"], w["b_alpha"], w["gla_norm"])
    y = jnp.concatenate([y_pool, y_gla], axis=1)
    h2 = _matmul_residual(y, w["w_out_mix"], h1, "mix_out", at("gla_fwd", y))
    return h2, dict(u=u, n2=n2, pooled=pooled, o_gla=o_gla, states=states, y=y)


def _fwd_ffn2_loss(h2, tgt, w, after=()):
    dh3, n3, gu3, d_final, loss = _ffn_fwd(h2, w["ffn2_norm"], w["ffn2_w_in"], w["ffn2_w_out"], "ffn2_fwd", after,
                                           head=(w["final_norm"], tgt))
    return dh3, d_final, loss, n3, gu3


def _backward(x, h1, h2, n1, gu1, sv, n3, gu3, dh3, w, at=lambda point, value: ()):
    g = {}
    dgu3, act3, df3, dh2, dh2b, g["ffn2_norm"] = _ffn_bwd_x(dh3, h2, w["ffn2_norm"], gu3, w["ffn2_w_in"], w["ffn2_w_out"], "ffn2_bwd")
    g["ffn2_w_out"] = _ffn_dw_out(act3, df3, "ffn2")
    g["ffn2_w_in"] = _ffn_dw_in(n3, dgu3, "ffn2", at("ffn2_w_out", g["ffn2_w_out"]))
    dy = _nt_matmul(dh2b, w["w_out_mix"], "mix_out_bwd", at("ffn2_w_in", g["ffn2_w_in"]))
    g["w_out_mix"] = _tn_matmul(sv["y"], dh2b, pl.BlockSpec((S, 512), lambda m: (0, m)), pl.BlockSpec((S, D), lambda m: (0, 0)),
                                jax.ShapeDtypeStruct((D, D), BF16), pl.BlockSpec((512, D), lambda m: (m, 0)), (D // 512,),
                                "mix_out_dw", 40)
    du_pool, g["w_pool"], g["pool_scale"] = _pool_bwd(dy, sv["pooled"], w["w_pool"], w["pool_scale"])
    du_gla, g["w_alpha"], g["b_alpha"], g["gla_norm"] = _gla_bwd(sv["u"], sv["o_gla"], sv["states"], dy, w["w_alpha"], w["b_alpha"],
                                                                 w["gla_norm"], at("pool_bwd", du_pool))
    du = jnp.concatenate([du_pool, du_gla], axis=1)
    g["w_in_mix"] = _tn_matmul(du, sv["n2"], pl.BlockSpec((S, 1408), lambda j, m: (0, j)), pl.BlockSpec((S, 512), lambda j, m: (0, m)),
                               jax.ShapeDtypeStruct((D_IN, D), BF16), pl.BlockSpec((1408, 512), lambda j, m: (j, m)),
                               (D_IN_PAD // 1408, D // 512), "mix_in_dw", 32)
    dh1, g["mix_norm"], df1 = _matmul_normbwd(du, w["w_in_mix"], w["w_in_mix_tail"], h1, w["mix_norm"], dh2, "mix_in_bwd",
                                              at("mixer_weights", g))
    dgu1, g["ffn1_w_out"] = _ffn_bwd_gu(df1, gu1, w["ffn1_w_out"], "ffn1_bwd_gu", at("dh1", dh1))
    half = S // FFN_TS // 2
    dx, dn_a = _ffn_bwd_dx(dgu1, dh1, x, w["ffn1_norm"], w["ffn1_w_in"], 0, half, "ffn1_bwd_dx_a",
                           at("ffn1_w_out", g["ffn1_w_out"]))
    g["ffn1_w_in_top"] = _ffn_dw_in(n1, dgu1, "ffn1_top", at("ffn1_dx_a", dx), 0, D // 1024)
    g["ffn1_w_in_bottom"] = _ffn_dw_in(n1, dgu1, "ffn1_bottom", at("ffn1_w_in_top", g["ffn1_w_in_top"]), D // 1024, D // 1024)
    dx, dn_b = _ffn_bwd_dx(dgu1, dh1, x, w["ffn1_norm"], w["ffn1_w_in"], half, half, "ffn1_bwd_dx_b",
                           at("ffn1_w_in_bottom", g["ffn1_w_in_bottom"]), into=dx)
    g["ffn1_norm"] = dn_a + dn_b
    return dx, g


def _local_step(x, tgt, w):
    h1, n1, gu1 = _fwd_ffn1(x, w)
    h2, sv = _fwd_mixer(h1, w)
    dh3, d_final, loss, n3, gu3 = _fwd_ffn2_loss(h2, tgt, w)
    dx, g = _backward(x, h1, h2, n1, gu1, sv, n3, gu3, dh3, w)
    return loss, dx, dict(final_norm=d_final, **g)


def _coords(p):
    return (p // 4, (p // 2) % 2, p % 2)


NCHIP = 4


def _place():
    return lax.axis_index("x"), lax.axis_index("y"), lax.axis_index("c")


def _rel_chip(x, y, rel):
    return ((1 - x) if rel & 1 else x, (1 - y) if rel & 2 else y)


def _dev_index(x, y, c):
    return 4 * x + 2 * y + c


def _cols(ref, p, width):
    return ref.at[:, pl.ds(pl.multiple_of(p * width, 128), width)]


def _sems(na, n):
    return [pltpu.SemaphoreType.DMA((na, n)), pltpu.SemaphoreType.DMA((na, n)), pltpu.SemaphoreType.DMA((na,))]


def _gather(items, name):
    arrays = [a for a, _ in items]
    kinds = [k for _, k in items]
    na = len(arrays)
    out_shape = [jax.ShapeDtypeStruct((NDEV,) + a.shape if k == "bcast" else (a.shape[0], NDEV * a.shape[1]), a.dtype)
                 for a, k in items]

    def body(*refs):
        ins, outs = refs[:na], refs[na:2 * na]
        send_sems, recv_sems, local_sems = refs[2 * na:]
        x, y, c = _place()
        sibling = (x, y, 1 - c)
        here, over_x, over_y, across = (x, y), (1 - x, y), (x, 1 - y), (1 - x, 1 - y)

        def half(ref, h):
            rows = ref.shape[0] // 2
            return ref.at[pl.ds(h * rows, rows), :]

        def slab(a, chip, core, h=None):
            ref = _slab(outs[a], kinds[a], _dev_index(*chip, core), ins[a].shape[1])
            return ref if h is None else half(ref, h)

        def copy(a, k, src, dst, to):
            return pltpu.make_async_remote_copy(src, dst, send_sems.at[a, k], recv_sems.at[a, k], device_id=to, device_id_type=MESH)

        sent = []

        def send(a, k, src, dst, to):
            sent.append(copy(a, k, src, dst, to))
            sent[-1].start()

        def arrived(a, k, chip, core, h=None):
            ref = slab(a, chip, core, h)
            copy(a, k, ref, ref, sibling).wait_recv()
            return ref

        local = [pltpu.make_async_copy(ins[a], slab(a, here, c), local_sems.at[a]) for a in range(na)]
        for cp in local:
            cp.start()
        for k, h, chip in ((1, 0, over_x), (4, 1, over_y), (2, 1, over_x), (5, 0, over_y)):
            for a in range(na):
                send(a, k, half(ins[a], h), slab(a, here, c, h), (*chip, c))
        for a in range(na):
            send(a, 0, ins[a], slab(a, here, c), sibling)
        for k, chip, h, onward, to, down in ((1, over_x, 0, 6, over_y, 7), (4, over_y, 1, 3, over_x, 10),
                                             (2, over_x, 1, None, None, 8), (5, over_y, 0, None, None, 9),
                                             (3, across, 1, None, None, 12), (6, across, 0, None, None, 11)):
            for a in range(na):
                ref = arrived(a, k, chip, c, h)
                if onward is not None:
                    send(a, onward, ref, ref, (*to, c))
                send(a, down, ref, ref, sibling)
        for a in range(na):
            arrived(a, 0, here, 1 - c)
        for k, chip, h in ((7, over_x, 0), (10, over_y, 1), (8, over_x, 1), (9, over_y, 0), (12, across, 1), (11, across, 0)):
            for a in range(na):
                arrived(a, k, chip, 1 - c, h)
        for cp in sent:
            cp.wait_send()
        for cp in local:
            cp.wait()

    return pl.pallas_call(body, name=name, in_specs=[ANY] * na, out_specs=[ANY] * na, out_shape=out_shape,
                          scratch_shapes=_sems(na, 13))(*arrays)


def _pair_add(own, kind, got, table, tr, name):
    _, rows, cols = got.shape
    if kind == "scatter":
        own_spec = pl.BlockSpec((None, tr, cols), lambda rel, i, t: (t[rel], i, 0))
    else:
        own_spec = pl.BlockSpec((tr, cols), lambda rel, i, t: (i, t[rel]))
    blk = pl.BlockSpec((None, tr, cols), lambda rel, i, t: (rel, i, 0))

    def body(t_ref, a_ref, b_ref, o_ref):
        o_ref[...] = (a_ref[...].astype(F32) + b_ref[...].astype(F32)).astype(o_ref.dtype)

    return _pallas(
        body, name=name, out_shape=jax.ShapeDtypeStruct(got.shape, got.dtype),
        grid_spec=pltpu.PrefetchScalarGridSpec(num_scalar_prefetch=1, grid=(NCHIP, rows // tr), in_specs=[own_spec, blk],
                                               out_specs=blk),
        compiler_params=_params(32),
    )(table, own, got)


HBM = pl.BlockSpec(memory_space=pltpu.HBM)
SEM = pl.BlockSpec(memory_space=pltpu.SEMAPHORE)
DATAFLOW = pltpu.SideEffectType.DATAFLOW_SIDE_EFFECTING


def _pair_copies(kinds):
    def describe(srcs, lands, send_sems, recv_sems):
        x, y, c = _place()
        na = len(srcs)
        for rel in range(NCHIP):
            p = _dev_index(*_rel_chip(x, y, rel), 1 - c)
            for a in range(na):
                src = srcs[a].at[p] if kinds[a] == "scatter" else _cols(srcs[a], p, srcs[a].shape[1] // NDEV)
                cp = pltpu.make_async_remote_copy(src, lands[a].at[rel], send_sems.at[rel * na + a], recv_sems.at[rel * na + a],
                                                  device_id=(x, y, 1 - c), device_id_type=MESH)
                yield cp, cp
    return describe


def _chip_copies(srcs, lands, send_sems, recv_sems):
    x, y, c = _place()
    na = len(srcs)
    for rel in range(1, NCHIP):
        for a in range(na):
            i = (rel - 1) * na + a
            cp = pltpu.make_async_remote_copy(srcs[a].at[rel], lands[a].at[rel], send_sems.at[i], recv_sems.at[i],
                                              device_id=(*_rel_chip(x, y, rel), c), device_id_type=MESH)
            yield cp, cp


def _slab(ref, kind, s, width):
    return _cols(ref, s, width) if kind == "bcast_cols" else ref.at[s]


def _all_copies(srcs, lands, send_sems, recv_sems):
    x, y, c = _place()
    na = len(srcs)
    me = _dev_index(x, y, c)
    for a in range(na):
        yield pltpu.make_async_copy(srcs[a], lands[a].at[me], send_sems.at[a]), None
    for k in range(1, NDEV):
        to, frm = (me + k) % NDEV, (me + NDEV - k) % NDEV
        for a in range(na):
            i = k * na + a
            send = pltpu.make_async_remote_copy(srcs[a], lands[a].at[me], send_sems.at[i], recv_sems.at[i],
                                                device_id=_coords(to), device_id_type=MESH)
            arrival = pltpu.make_async_remote_copy(srcs[a], lands[a].at[frm], send_sems.at[i], recv_sems.at[i],
                                                   device_id=_coords(to), device_id_type=MESH)
            yield send, arrival


def _gather_copies(kinds):
    def describe(srcs, lands, send_sems, recv_sems):
        x, y, c = _place()
        na = len(srcs)
        me = _dev_index(x, y, c)
        for a in range(na):
            yield pltpu.make_async_copy(srcs[a], _slab(lands[a], kinds[a], me, srcs[a].shape[-1]),
                                        send_sems.at[NCHIP * na + a]), None
        for rel in range(NCHIP):
            to = (x, y, 1 - c) if rel == 0 else (*_rel_chip(x, y, rel), c)
            for a in range(na):
                width = srcs[a].shape[-1]
                i = rel * na + a
                send = pltpu.make_async_remote_copy(srcs[a], _slab(lands[a], kinds[a], me, width), send_sems.at[i], recv_sems.at[i],
                                                    device_id=to, device_id_type=MESH)
                arrival = pltpu.make_async_remote_copy(srcs[a], _slab(lands[a], kinds[a], _dev_index(*to), width), send_sems.at[i],
                                                       recv_sems.at[i], device_id=to, device_id_type=MESH)
                yield send, arrival
    return describe


def _relay_copies(kinds, widths, phase):
    def describe(srcs, lands, send_sems, recv_sems):
        x, y, c = _place()
        na = len(lands)
        here, over_x, over_y, across = (x, y), (1 - x, y), (x, 1 - y), (1 - x, 1 - y)
        sibling = (x, y, 1 - c)

        def half(ref, h):
            rows = ref.shape[0] // 2
            return ref if h is None else ref.at[pl.ds(h * rows, rows), :]

        def slab(a, chip, core, h=None):
            return half(_slab(lands[a], kinds[a], _dev_index(*chip, core), widths[a]), h)

        def pair(a, k, src, dst, to, arrival):
            i = k * na + a
            return (pltpu.make_async_remote_copy(src, dst, send_sems.at[i], recv_sems.at[i], device_id=to, device_id_type=MESH),
                    pltpu.make_async_remote_copy(arrival, arrival, send_sems.at[i], recv_sems.at[i], device_id=to,
                                                 device_id_type=MESH))

        if phase == 1:
            for a in range(na):
                yield pltpu.make_async_copy(srcs[a], slab(a, here, c), send_sems.at[a]), None
            for k, h, chip in ((1, 0, over_x), (2, 1, over_y), (3, 1, over_x), (4, 0, over_y)):
                for a in range(na):
                    yield pair(a, k, half(srcs[a], h), slab(a, here, c, h), (*chip, c), slab(a, chip, c, h))
            for a in range(na):
                yield pair(a, 5, srcs[a], slab(a, here, c), sibling, slab(a, here, 1 - c))
        elif phase == 2:
            for a in range(na):
                yield pair(a, 0, slab(a, over_x, c, 0), slab(a, over_x, c, 0), (*over_y, c), slab(a, across, c, 0))
                yield pair(a, 1, slab(a, over_y, c, 1), slab(a, over_y, c, 1), (*over_x, c), slab(a, across, c, 1))
            for k, (chip, h) in enumerate(((over_x, 0), (over_y, 1), (over_x, 1), (over_y, 0)), start=2):
                for a in range(na):
                    yield pair(a, k, slab(a, chip, c, h), slab(a, chip, c, h), sibling, slab(a, chip, 1 - c, h))
        else:
            for h in (0, 1):
                for a in range(na):
                    yield pair(a, h, slab(a, across, c, h), slab(a, across, c, h), sibling, slab(a, across, 1 - c, h))
    return describe


RELAY_COPIES = {1: 6, 2: 6, 3: 2}


def _relay_finish(arrays, kinds, widths, name):
    na = len(arrays)
    describe = _relay_copies(kinds, widths, 3)

    def body(*refs):
        copies = list(describe((), refs[na:2 * na], refs[2 * na], refs[2 * na + 1]))
        for send, _ in copies:
            send.start()
        for send, arrival in copies:
            send.wait_send()
            arrival.wait_recv()

    return pl.pallas_call(body, name=name, in_specs=[ANY] * na, out_specs=[ANY] * na,
                          out_shape=[jax.ShapeDtypeStruct(a.shape, a.dtype) for a in arrays],
                          input_output_aliases={i: i for i in range(na)},
                          scratch_shapes=[pltpu.SemaphoreType.DMA((RELAY_COPIES[3] * na,))] * 2)(*arrays)


def _pass_to_sibling(arrays, kinds, widths, name):
    na = len(arrays)

    def body(*refs):
        bufs = refs[na:2 * na]
        send_sems, recv_sems = refs[2 * na:]
        x, y, c = _place()
        copies = []
        for rel in range(1, NCHIP):
            for a in range(na):
                mine = _slab(bufs[a], kinds[a], _dev_index(*_rel_chip(x, y, rel), c), widths[a])
                theirs = _slab(bufs[a], kinds[a], _dev_index(*_rel_chip(x, y, rel), 1 - c), widths[a])
                send = pltpu.make_async_remote_copy(mine, mine, send_sems.at[a, rel], recv_sems.at[a, rel],
                                                    device_id=(x, y, 1 - c), device_id_type=MESH)
                send.start()
                copies.append((send, pltpu.make_async_remote_copy(theirs, theirs, send_sems.at[a, rel], recv_sems.at[a, rel],
                                                                  device_id=(x, y, 1 - c), device_id_type=MESH)))
        for send, arrival in copies:
            send.wait_send()
            arrival.wait_recv()

    return pl.pallas_call(body, name=name, in_specs=[ANY] * na, out_specs=[ANY] * na,
                          out_shape=[jax.ShapeDtypeStruct(a.shape, a.dtype) for a in arrays],
                          input_output_aliases={i: i for i in range(na)}, scratch_shapes=_sems(na, NCHIP)[:2])(*arrays)


def _start_copies(name, srcs, lands, describe, ncopies, after=()):
    arrays = list(srcs) + list(lands)
    ns, n, nin = len(srcs), len(arrays), len(arrays) + len(after)

    def body(*refs):
        for send, _ in describe(refs[:ns], refs[ns:n], refs[nin], refs[nin + 1]):
            send.start()
        refs[-1][...] = jnp.zeros_like(refs[-1])

    out = pl.pallas_call(
        body, name=name,
        out_shape=(pltpu.SemaphoreType.DMA((ncopies,)), pltpu.SemaphoreType.DMA((ncopies,)),
                   *[pltpu.HBM(a.shape, a.dtype) for a in arrays], jax.ShapeDtypeStruct((8, 128), F32)),
        in_specs=[HBM] * n + [ANY] * len(after), out_specs=(SEM, SEM, *[HBM] * n, pl.BlockSpec(memory_space=pltpu.VMEM)),
        input_output_aliases={i: 2 + i for i in range(n)},
        compiler_params=pltpu.CompilerParams(has_side_effects=DATAFLOW),
    )(*[pltpu.with_memory_space_constraint(a, pltpu.HBM) for a in arrays], *after)
    return out[0], out[1], list(out[2:2 + n]), out[-1]


def _wait_copies(name, send_sems, recv_sems, thru, ns, describe, after):
    n = len(thru)

    def body(*refs):
        for send, arrival in describe(refs[:ns], refs[ns:n], refs[n], refs[n + 1]):
            if arrival is None:
                send.wait()
            else:
                send.wait_send()
                arrival.wait_recv()

    out = pl.pallas_call(
        body, name=name, out_shape=tuple(pltpu.HBM(a.shape, a.dtype) for a in thru),
        in_specs=[HBM] * n + [SEM, SEM] + [ANY] * len(after), out_specs=tuple([HBM] * n),
        input_output_aliases={i: i for i in range(n)},
        compiler_params=pltpu.CompilerParams(has_side_effects=DATAFLOW),
    )(*thru, send_sems, recv_sems, *after)
    return list(out[:ns]), list(out[ns:])


def _adamw(parts, w, m, v, tr, name, tc=None, first_row=0, into=None):
    rows, cols = w.shape
    part_rows = parts[0][0].shape[1]
    tc = cols if tc is None else tc
    nparts = len(parts)
    tile0 = first_row // tr
    blk = pl.BlockSpec((tr, tc), lambda i, j: (tile0 + i, j))
    carried = [] if into is None else list(into)

    def slab_spec(s):
        return pl.BlockSpec((None, tr, tc), lambda i, j: (s, i, j))

    def body(*refs):
        p_refs = refs[:nparts]
        w_ref, m_ref, v_ref = refs[nparts:nparts + 3]
        g_ref, d_ref, nm_ref, nv_ref = refs[nparts + 3 + len(carried):]
        g = p_refs[0][...].astype(F32)
        for p_ref in p_refs[1:]:
            g = g + p_ref[...].astype(F32)
        nm = ADAM_B1 * m_ref[...] + (1.0 - ADAM_B1) * g
        nv = ADAM_B2 * v_ref[...] + (1.0 - ADAM_B2) * (g * g)
        m_hat = nm / (1.0 - ADAM_B1 ** ADAM_STEP)
        v_hat = nv / (1.0 - ADAM_B2 ** ADAM_STEP)
        g_ref[...] = g
        d_ref[...] = -ADAM_LR * (m_hat / (jnp.sqrt(v_hat) + ADAM_EPS) + ADAM_WD * w_ref[...])
        nm_ref[...] = nm
        nv_ref[...] = nv

    return _pallas(
        body, grid=(part_rows // tr, cols // tc), name=name,
        in_specs=[slab_spec(s) for _, s in parts] + [blk, blk, blk] + [ANY] * len(carried), out_specs=[blk] * 4,
        out_shape=[jax.ShapeDtypeStruct((rows, cols), F32)] * 4,
        input_output_aliases={nparts + 3 + k: k for k in range(len(carried))},
        compiler_params=_params(40),
    )(*[a for a, _ in parts], w, m, v, *carried)


def _pack_small(vals, extra=None):
    flat = [vals[n].reshape(-1).astype(F32) for n, _ in SMALL]
    tail = jnp.zeros((SMALL_ROWS * 128 - LOSS_AT,), F32)
    if extra is not None:
        tail = tail.at[0].set(extra)
    return jnp.concatenate(flat + [tail]).reshape(SMALL_ROWS, 128)


def _unpack_small(packed, like):
    flat, out, at = packed.reshape(-1), {}, 0
    for n, size in SMALL:
        out[n] = flat[at:at + size].reshape(like[n].shape)
        at += size
    return out, flat[LOSS_AT]


def kernel(x, ffn1_norm, ffn1_w_in, ffn1_w_out, mix_norm, w_in_mix, w_pool, pool_scale, w_alpha, b_alpha, gla_norm, w_out_mix, ffn2_norm, ffn2_w_in, ffn2_w_out, final_norm, loss_target, m_ffn1_norm, m_ffn1_w_in, m_ffn1_w_out, m_mix_norm, m_w_in_mix, m_w_pool, m_pool_scale, m_w_alpha, m_b_alpha, m_gla_norm, m_w_out_mix, m_ffn2_norm, m_ffn2_w_in, m_ffn2_w_out, m_final_norm, v_ffn1_norm, v_ffn1_w_in, v_ffn1_w_out, v_mix_norm, v_w_in_mix, v_w_pool, v_pool_scale, v_w_alpha, v_b_alpha, v_gla_norm, v_w_out_mix, v_ffn2_norm, v_ffn2_w_in, v_ffn2_w_out, v_final_norm):
    names = ["ffn1_norm", "ffn1_w_in", "ffn1_w_out", "mix_norm", "w_in_mix", "w_pool", "pool_scale", "w_alpha", "b_alpha",
             "gla_norm", "w_out_mix", "ffn2_norm", "ffn2_w_in", "ffn2_w_out", "final_norm"]
    p = dict(zip(names, [ffn1_norm, ffn1_w_in, ffn1_w_out, mix_norm, w_in_mix, w_pool, pool_scale, w_alpha, b_alpha,
                         gla_norm, w_out_mix, ffn2_norm, ffn2_w_in, ffn2_w_out, final_norm]))
    m = dict(zip(names, [m_ffn1_norm, m_ffn1_w_in, m_ffn1_w_out, m_mix_norm, m_w_in_mix, m_w_pool, m_pool_scale, m_w_alpha,
                         m_b_alpha, m_gla_norm, m_w_out_mix, m_ffn2_norm, m_ffn2_w_in, m_ffn2_w_out, m_final_norm]))
    v = dict(zip(names, [v_ffn1_norm, v_ffn1_w_in, v_ffn1_w_out, v_mix_norm, v_w_in_mix, v_w_pool, v_pool_scale, v_w_alpha,
                         v_b_alpha, v_gla_norm, v_w_out_mix, v_ffn2_norm, v_ffn2_w_in, v_ffn2_w_out, v_final_norm]))

    mx, my, mc = _place()
    table = jnp.stack([_dev_index(*_rel_chip(mx, my, rel), mc) for rel in range(NCHIP)]).astype(jnp.int32)

    def landing(shard, kind):
        shape = (shard.shape[0], NDEV * shard.shape[1]) if kind == "bcast_cols" else (NDEV,) + shard.shape
        return lax.empty(shape, shard.dtype)

    def gather_begin(items, tag, after):
        kinds = [kind for _, kind in items]
        copies = _gather_copies(kinds)
        s, r, thru, tok = _start_copies(tag + "_start", [a for a, _ in items], [landing(a, kind) for a, kind in items], copies,
                                        (NCHIP + 1) * len(items), after)
        return (s, r, thru, copies, kinds, [a.shape[-1] for a, _ in items], tag), tok

    def gather_end(state, after):
        s, r, thru, copies, kinds, widths, tag = state
        _, lands = _wait_copies(tag + "_wait", s, r, thru, len(kinds), copies, after)
        return _pass_to_sibling(lands, kinds, widths, tag + "_pass")

    def shard16(n):
        return p[n][0].astype(BF16)

    g_w1in, g_w1out = _gather([(shard16("ffn1_w_in"), "bcast_cols"), (shard16("ffn1_w_out"), "bcast")], "gather_ffn1")
    mix_state, tok_m = gather_begin([(jnp.transpose(p["w_in_mix"][0]).astype(BF16), "bcast"), (shard16("w_out_mix"), "bcast"),
                                     (p["w_pool"][0].reshape(H * 32, PG), "bcast"), (p["w_alpha"][0], "bcast")], "gather_mix",
                                    (g_w1out,))
    ffn2_items = [(shard16("ffn2_w_in"), "bcast_cols"), (shard16("ffn2_w_out"), "bcast")]
    ffn2_kinds = [kind for _, kind in ffn2_items]
    ffn2_widths = [a.shape[-1] for a, _ in ffn2_items]
    relay = {ph: _relay_copies(ffn2_kinds, ffn2_widths, ph) for ph in (1, 2)}
    f_s, f_r, f_thru, tok_f = _start_copies("gather_ffn2_start", [a for a, _ in ffn2_items],
                                            [landing(a, kind) for a, kind in ffn2_items], relay[1],
                                            RELAY_COPIES[1] * len(ffn2_items), (tok_m,))
    ffn2_lands = {}

    def relay_ffn2(point, value):
        _, lands = _wait_copies("gather_ffn2_wait", f_s, f_r, f_thru, len(ffn2_items), relay[1], (value,))
        ffn2_lands["s"], ffn2_lands["r"], ffn2_lands["thru"], tok = _start_copies(
            "gather_ffn2_relay_start", [], lands, relay[2], RELAY_COPIES[2] * len(ffn2_items))
        return (tok,)

    full ={"ffn1_w_in": g_w1in, "ffn1_w_out": g_w1out.reshape(FF, D), "final_norm": final_norm.reshape(1, D)}
    for n in ("ffn1_norm", "mix_norm", "ffn2_norm", "pool_scale", "b_alpha", "gla_norm"):
        full[n] = p[n]

    xs, tgt = x[0], loss_target[0]
    h1, n1, gu1 = _fwd_ffn1(xs, full, after=(tok_m, tok_f))
    g_wmix, g_wo, g_wpool, g_walpha = gather_end(mix_state, (h1,))
    walpha = jnp.transpose(g_walpha, (1, 0, 2)).reshape(RANK, H * DK)
    full.update({
        "w_in_mix": g_wmix.reshape(D_IN, D),
        "w_in_mix_tail": jnp.pad(g_wmix[NDEV - 1, MIX_SHARD - RANK:], ((0, RANK_PAD - RANK), (0, 0))),
        "w_out_mix": g_wo.reshape(D, D),
        "w_pool": jnp.transpose(g_wpool.reshape(NDEV, H, 32, PG), (1, 0, 2, 3)).reshape(H, PG, PG).astype(BF16),
        "w_alpha": jnp.pad(walpha, ((0, RANK_PAD - RANK), (0, 0))).astype(BF16),
    })
    h2, sv = _fwd_mixer(h1, full, at=relay_ffn2)
    _, lands = _wait_copies("gather_ffn2_relay_wait", ffn2_lands["s"], ffn2_lands["r"], ffn2_lands["thru"], 0, relay[2], (h2,))
    g_w2in, g_w2out = _relay_finish(lands, ffn2_kinds, ffn2_widths, "gather_ffn2_finish")
    full.update({"ffn2_w_in": g_w2in, "ffn2_w_out": g_w2out.reshape(FF, D)})
    dh3, d_final, loss_part, n3, gu3 = _fwd_ffn2_loss(h2, tgt, full)


    def slab_shape(a, kind):
        return (NCHIP,) + (a.shape[1:] if kind == "scatter" else (a.shape[0], a.shape[1] // NDEV))

    def pair_add_all(own, got, tag):
        return [_pair_add(a, kind, got_a, table, tr, "%s_pair_add_%d" % (tag, i))
                for i, ((a, kind, tr), got_a) in enumerate(zip(own, got))]

    def reduce_begin(own, tag, after=()):
        kinds = [kind for _, kind, _ in own]
        copies = _pair_copies(kinds)
        s, r, thru, tok = _start_copies(tag + "_pair_start", [a for a, _, _ in own],
                                        [lax.empty(slab_shape(a, kind), a.dtype) for a, kind, _ in own], copies,
                                        NCHIP * len(own), after)
        return dict(own=own, copies=copies, s=s, r=r, thru=thru, tag=tag), tok

    def reduce_middle(st, after):
        own, tag = st["own"], st["tag"]
        sent, got = _wait_copies(tag + "_pair_wait", st["s"], st["r"], st["thru"], len(own), st["copies"], after)
        pre = pair_add_all([(a, kind, tr) for a, (_, kind, tr) in zip(sent, own)], got, tag)
        st["s"], st["r"], st["thru"], tok = _start_copies(tag + "_chip_start", pre, [lax.empty(a.shape, a.dtype) for a in pre],
                                                          _chip_copies, (NCHIP - 1) * len(pre))
        return tok

    def reduce_end(st, after):
        n = len(st["own"])
        pre, land = _wait_copies(st["tag"] + "_chip_wait", st["s"], st["r"], st["thru"], n, _chip_copies, after)
        return [[(a, 0)] + [(b, rel) for rel in range(1, NCHIP)] for a, b in zip(pre, land)]

    def w_in_item(a):
        return (a, "scatter_cols", 512)

    def w_out_item(a):
        return (a.reshape(NDEV, WOUT_SHARD, D), "scatter", WOUT_SHARD)

    red, small = {}, {}

    def at(point, value):
        if point == "ffn2_w_out":
            red["w2out"], tok = reduce_begin([w_out_item(value)], "ffn2_w_out")
        elif point == "ffn2_w_in":
            tok_a = reduce_middle(red["w2out"], (value,))
            red["w2in"], tok = reduce_begin([w_in_item(value)], "ffn2_w_in", (tok_a,))
        elif point == "pool_bwd":
            tok = reduce_middle(red["w2in"], (value,))
        elif point == "mixer_weights":
            d_wmix8 = value["w_in_mix"].reshape(NDEV, MIX_SHARD, D)
            d_wpool8 = jnp.transpose(value["w_pool"].reshape(H, NDEV, 32, PG), (1, 0, 2, 3)).reshape(NDEV, H * 32, PG)
            d_walpha8 = jnp.transpose(value["w_alpha"][:RANK].reshape(RANK, NDEV, H * DK // NDEV), (1, 0, 2))
            red["mix"], tok = reduce_begin([(d_wmix8, "scatter", MIX_SHARD),
                                            (value["w_out_mix"].reshape(NDEV, D // NDEV, D), "scatter", D // NDEV),
                                            (d_wpool8, "scatter", H * 32), (d_walpha8, "scatter", RANK)], "mix")
        elif point == "dh1":
            tok = reduce_middle(red["mix"], (value,))
        elif point == "ffn1_w_out":
            red["w1out"], tok = reduce_begin([w_out_item(value)], "ffn1_w_out")
        elif point == "ffn1_dx_a":
            tok = reduce_middle(red["w1out"], (value,))
        elif point == "ffn1_w_in_top":
            red["w1in_top"], tok = reduce_begin([w_in_item(value)], "ffn1_w_in_top")
        elif point == "ffn1_w_in_bottom":
            tok_a = reduce_middle(red["w1in_top"], (value,))
            red["w1in_bottom"], tok = reduce_begin([w_in_item(value)], "ffn1_w_in_bottom", (tok_a,))
        return (tok,)

    dx, g = _backward(xs, h1, h2, n1, gu1, sv, n3, gu3, dh3, full, at)
    packed = _pack_small(dict(final_norm=d_final, **g), loss_part[0, 0])
    small_s, small_r, small_thru, tok_s = _start_copies("gather_small_start", [packed], [lax.empty((NDEV,) + packed.shape, F32)],
                                                        _all_copies, NDEV, (dx,))
    tok_c = reduce_middle(red["w1in_bottom"], (tok_s,))

    def upd(parts, n, shape2d, tr):
        res = _adamw(parts, p[n].reshape(shape2d), m[n].reshape(shape2d), v[n].reshape(shape2d), tr, "adamw_" + n)
        return [r.reshape(p[n].shape) for r in res]

    def transposed(a):
        return jnp.transpose(a[0])

    (p_w2out,) = reduce_end(red["w2out"], (tok_c,))
    (p_w2in,) = reduce_end(red["w2in"], (tok_c,))
    p_wmix, p_wo, p_wpool, p_walpha = reduce_end(red["mix"], (tok_c,))
    out = {
        "ffn2_w_in": upd(p_w2in, "ffn2_w_in", (D, WIN_SHARD), 256),
        "ffn2_w_out": upd(p_w2out, "ffn2_w_out", (WOUT_SHARD, D), WOUT_SHARD // 4),
        "w_out_mix": upd(p_wo, "w_out_mix", (D // NDEV, D), 64),
        "w_pool": upd(p_wpool, "w_pool", (H * 32, PG), H * 32),
        "w_alpha": upd(p_walpha, "w_alpha", (RANK, H * DK // NDEV), RANK),
    }
    out["w_in_mix"] = [jnp.transpose(r)[None] for r in
                       _adamw(p_wmix, transposed(p["w_in_mix"]), transposed(m["w_in_mix"]), transposed(v["w_in_mix"]),
                              MIX_SHARD, "adamw_w_in_mix", tc=512)]
    _, (r_small,) = _wait_copies("gather_small_wait", small_s, small_r, small_thru, 1, _all_copies,
                                 (out["w_in_mix"][3], out["ffn2_w_in"][3], out["ffn2_w_out"][3], out["w_out_mix"][3]))
    small_res = _adamw([(r_small, s) for s in range(NDEV)], _pack_small(p), _pack_small(m), _pack_small(v), SMALL_ROWS,
                       "adamw_small")
    (p_w1out,) = reduce_end(red["w1out"], (small_res[0],))
    out["ffn1_w_out"] = upd(p_w1out, "ffn1_w_out", (WOUT_SHARD, D), WOUT_SHARD // 4)
    w1in = [a.reshape(D, WIN_SHARD) for a in (p["ffn1_w_in"], m["ffn1_w_in"], v["ffn1_w_in"])]
    (p_top,) = reduce_end(red["w1in_top"], (out["ffn1_w_out"][3],))
    top = _adamw(p_top, *w1in, 256, "adamw_ffn1_w_in_top")
    (p_bottom,) = reduce_end(red["w1in_bottom"], (top[3],))
    out["ffn1_w_in"] = [r.reshape(p["ffn1_w_in"].shape) for r in
                        _adamw(p_bottom, *w1in, 256, "adamw_ffn1_w_in_bottom", first_row=D // 2, into=top)]
    unpacked = [_unpack_small(r, p) for r in small_res]
    loss = unpacked[0][1]
    for n, _ in SMALL:
        out[n] = [u[0][n] for u in unpacked]

    return (loss, dx.reshape(1, S, D), *[out[n][0] for n in names], *[out[n][1] for n in names],
            *[out[n][2] for n in names], *[out[n][3] for n in names])
```

```python
import jax
import jax.numpy as jnp
from jax import lax
from jax.experimental import pallas as pl
from jax.experimental.pallas import tpu as pltpu

F32, BF16 = jnp.float32, jnp.bfloat16
MESH = pl.DeviceIdType.MESH
ANY = pl.BlockSpec(memory_space=pl.ANY)

NDEV = 8
S = 2048
D = 2048
FF = 5632
WIN_SHARD = 2 * FF // NDEV
WOUT_SHARD = FF // NDEV
D_POOL = 1024
PG = 256
POOL_WINDOWS = (2, 4, 8, 16)
H = 4
DK = 128
DV = 256
CH = 64
NCH = S // CH
RANK = 16
RANK_PAD = 128
D_IN = 4112
D_IN_PAD = 4224
MIX_SHARD = D_IN // NDEV
O_Q, O_K, O_V, O_G, O_R = 1024, 1536, 2048, 3072, 4096
GATE_NORM = 16.0
QK_SCALE = DK ** -0.5
EPS = 1e-6
ADAM_LR, ADAM_B1, ADAM_B2, ADAM_EPS, ADAM_WD, ADAM_STEP = 0.001, 0.9, 0.999, 1e-08, 0.01, 10
V7X_VMEM_BYTES = 64 << 20

SMALL = (("ffn1_norm", 2048), ("mix_norm", 2048), ("ffn2_norm", 2048), ("final_norm", 2048),
         ("pool_scale", 1024), ("b_alpha", 512), ("gla_norm", 256))
SMALL_ROWS = 80
LOSS_AT = sum(n for _, n in SMALL)


def _params(vmem_mb, sem=None):
    return pltpu.CompilerParams(dimension_semantics=sem, vmem_limit_bytes=min(vmem_mb << 20, V7X_VMEM_BYTES - (4 << 20)))


def _pallas(body, **kwargs):
    call = pl.pallas_call(body, **kwargs)

    def run(*operands):
        return call(*[pltpu.with_memory_space_constraint(a, pltpu.HBM) if a.size * a.dtype.itemsize >= 1 << 18 else a
                      for a in operands])
    return run


def _dot(a, b):
    return jnp.dot(a, b, preferred_element_type=F32)


def _dot_nt(a, b):
    return lax.dot_general(a, b, (((1,), (1,)), ((), ())), preferred_element_type=F32)


def _dot_tn(a, b):
    return lax.dot_general(a, b, (((0,), (0,)), ((), ())), preferred_element_type=F32)


def _sigmoid(x):
    return 0.5 * jnp.tanh(0.5 * x) + 0.5


def _log_sigmoid(x):
    return jnp.minimum(x, 0.0) - jnp.log(1.0 + jnp.exp(-jnp.abs(x)))


def _rms(x, g):
    r = lax.rsqrt(jnp.mean(x * x, axis=-1, keepdims=True) + EPS)
    return x * r * g


def _rms_bwd(dn, x, g):
    r = lax.rsqrt(jnp.mean(x * x, axis=-1, keepdims=True) + EPS)
    xh = x * r
    dxh = dn * g
    dx = r * (dxh - xh * jnp.mean(dxh * xh, axis=-1, keepdims=True))
    return dx, jnp.sum(dn * xh, axis=0, keepdims=True)


ROWS = 64


def _row_loop(total, fn, init=0):
    def step(t, carry):
        return fn(pl.ds(pl.multiple_of(t * ROWS, ROWS), ROWS), carry)
    return lax.fori_loop(0, total // ROWS, step, init)


def _split3(x):
    hi = x.astype(BF16)
    r1 = x - hi.astype(F32)
    mid = r1.astype(BF16)
    lo = (r1 - mid.astype(F32)).astype(BF16)
    return hi, mid, lo


def _tri_dot(tri_b, x):
    hi, mid, lo = _split3(x)
    return (_dot(tri_b, lo) + _dot(tri_b, mid)) + _dot(tri_b, hi)


FFN_TS, FFN_TF = 512, 512


def _ffn_specs():
    wg = pl.BlockSpec((D, FFN_TF), lambda i, j: (0, j))
    wu = pl.BlockSpec((D, FFN_TF), lambda i, j: (0, FF // FFN_TF + j))
    wo = pl.BlockSpec((FFN_TF, D), lambda i, j: (j, 0))
    row = pl.BlockSpec((FFN_TS, D), lambda i, j: (i, 0))
    vec = pl.BlockSpec((1, D), lambda i, j: (0, 0))
    gu = pl.BlockSpec((2, FFN_TS, FFN_TF), lambda i, j: (0, i, j))
    return wg, wu, wo, row, vec, gu


def _ordered_after(body, n_in, after):
    def wrapped(*refs):
        return body(*refs[:n_in], *refs[n_in + len(after):])
    return wrapped, [ANY] * len(after)


def _ffn_fwd(h, g, w_in8, w_out, name, after=(), head=None):
    nj = FF // FFN_TF
    wg, wu, wo, row, vec, gu = _ffn_specs()
    n_in = 5 if head is None else 7

    def body(*refs):
        h_ref, g_ref, wg_ref, wu_ref, wo_ref = refs[:5]
        ho_ref, n_ref, gu_ref = refs[n_in:n_in + 3]
        acc_ref = refs[-1]
        i, j = pl.program_id(0), pl.program_id(1)

        @pl.when(j == 0)
        def _():
            def norm(rows, c):
                n_ref[rows, :] = _rms(h_ref[rows, :], g_ref[...]).astype(BF16)
                return c
            _row_loop(FFN_TS, norm)
            acc_ref[...] = jnp.zeros_like(acc_ref)

        n = n_ref[...]
        gate = _dot(n, wg_ref[...])
        up = _dot(n, wu_ref[...])
        gu_ref[0] = gate.astype(BF16)
        gu_ref[1] = up.astype(BF16)
        a = (gate * _sigmoid(gate)) * up
        acc_ref[...] += _dot(a.astype(BF16), wo_ref[...])

        if head is None:
            @pl.when(j == nj - 1)
            def _():
                def residual(rows, c):
                    ho_ref[rows, :] = h_ref[rows, :] + 0.5 * acc_ref[rows, :]
                    return c
                _row_loop(FFN_TS, residual)
        else:
            gf_ref, t_ref = refs[5:7]
            dgf_ref, loss_ref = refs[n_in + 3:n_in + 5]

            @pl.when(j == nj - 1)
            def _():
                def rows_fn(rows, carry):
                    dg, part = carry
                    x = h_ref[rows, :] + 0.5 * acc_ref[rows, :]
                    gv = gf_ref[...]
                    err = _rms(x, gv) - t_ref[rows, :]
                    part = part + 0.5 * jnp.sum(jnp.mean(err * err, axis=-1, keepdims=True), axis=0, keepdims=True)
                    dx, dg_rows = _rms_bwd(err * (1.0 / D), x, gv)
                    ho_ref[rows, :] = dx
                    return dg + dg_rows, part
                dg, part = _row_loop(FFN_TS, rows_fn, (jnp.zeros((1, D), F32), jnp.zeros((1, 1), F32)))

                @pl.when(i == 0)
                def _():
                    dgf_ref[...] = dg
                    loss_ref[...] = jnp.broadcast_to(part, loss_ref.shape)

                @pl.when(i > 0)
                def _():
                    dgf_ref[...] += dg
                    loss_ref[...] += jnp.broadcast_to(part, loss_ref.shape)

    body, extra = _ordered_after(body, n_in, after)
    head_in = [] if head is None else [vec, row]
    head_out = [] if head is None else [vec, pl.BlockSpec((1, 128), lambda i, j: (0, 0))]
    head_shape = [] if head is None else [jax.ShapeDtypeStruct((1, D), F32), jax.ShapeDtypeStruct((1, 128), F32)]
    return _pallas(
        body, grid=(S // FFN_TS, nj), name=name,
        in_specs=[row, vec, wg, wu, wo] + head_in + extra, out_specs=[row, row, gu] + head_out,
        out_shape=[jax.ShapeDtypeStruct((S, D), F32), jax.ShapeDtypeStruct((S, D), BF16),
                   jax.ShapeDtypeStruct((2, S, FF), BF16)] + head_shape,
        scratch_shapes=[pltpu.VMEM((FFN_TS, D), F32)],
        compiler_params=_params(58, ("arbitrary", "arbitrary")),
    )(h, g, w_in8, w_in8, w_out, *(head or ()), *after)


def _ffn_bwd_x(dhp, h, g, gu_arr, w_in8, w_out, name, after=()):
    ni, nj = S // FFN_TS, FF // FFN_TF
    wg, wu, wo, row, vec, gu = _ffn_specs()
    act = pl.BlockSpec((FFN_TS, FFN_TF), lambda i, j: (i, j))

    def body(dhp_ref, h_ref, g_ref, gu_ref, wg_ref, wu_ref, wo_ref,
             dgu_ref, a_ref, df_ref, dh_ref, dhb_ref, dg_ref, acc_ref):
        i, j = pl.program_id(0), pl.program_id(1)

        @pl.when(j == 0)
        def _():
            def half(rows, c):
                df_ref[rows, :] = (0.5 * dhp_ref[rows, :]).astype(BF16)
                return c
            _row_loop(FFN_TS, half)
            acc_ref[...] = jnp.zeros_like(acc_ref)

        gate = gu_ref[0].astype(F32)
        up = gu_ref[1].astype(F32)
        da = _dot_nt(df_ref[...], wo_ref[...])
        sg = _sigmoid(gate)
        silu = gate * sg
        dgate = (da * up * (sg * (1.0 + gate * (1.0 - sg)))).astype(BF16)
        dup = (da * silu).astype(BF16)
        a_ref[...] = (silu * up).astype(BF16)
        dgu_ref[0] = dgate
        dgu_ref[1] = dup
        acc_ref[...] += _dot_nt(dgate, wg_ref[...]) + _dot_nt(dup, wu_ref[...])

        @pl.when(j == nj - 1)
        def _():
            def norm_bwd(rows, dg):
                dx, dg_rows = _rms_bwd(acc_ref[rows, :], h_ref[rows, :], g_ref[...])
                dh = dhp_ref[rows, :] + dx
                dh_ref[rows, :] = dh
                dhb_ref[rows, :] = dh.astype(BF16)
                return dg + dg_rows
            dg = _row_loop(FFN_TS, norm_bwd, jnp.zeros((1, D), F32))

            @pl.when(i == 0)
            def _():
                dg_ref[...] = dg

            @pl.when(i > 0)
            def _():
                dg_ref[...] += dg

    body, extra = _ordered_after(body, 7, after)
    return _pallas(
        body, grid=(ni, nj), name=name,
        in_specs=[row, row, vec, gu, wg, wu, wo] + extra,
        out_specs=[gu, act, row, row, row, vec],
        out_shape=[jax.ShapeDtypeStruct((2, S, FF), BF16), jax.ShapeDtypeStruct((S, FF), BF16),
                   jax.ShapeDtypeStruct((S, D), BF16), jax.ShapeDtypeStruct((S, D), F32),
                   jax.ShapeDtypeStruct((S, D), BF16), jax.ShapeDtypeStruct((1, D), F32)],
        scratch_shapes=[pltpu.VMEM((FFN_TS, D), F32)],
        compiler_params=_params(58, ("arbitrary", "arbitrary")),
    )(dhp, h, g, gu_arr, w_in8, w_in8, w_out, *after)


def _ffn_bwd_gu(df, gu_arr, w_out, name, after=()):
    ni = S // FFN_TS
    gu = pl.BlockSpec((2, FFN_TS, FFN_TF), lambda j, i: (0, i, j))
    wo = pl.BlockSpec((FFN_TF, D), lambda j, i: (j, 0))
    df_all = pl.BlockSpec((S, D), lambda j, i: (0, 0), pipeline_mode=pl.Buffered(1))

    def body(df_ref, gu_ref, wo_ref, dgu_ref, dwo_ref, acc_ref):
        i = pl.program_id(1)
        df_rows = df_ref[pl.ds(pl.multiple_of(i * FFN_TS, FFN_TS), FFN_TS), :]
        gate = gu_ref[0].astype(F32)
        up = gu_ref[1].astype(F32)
        da = _dot_nt(df_rows, wo_ref[...])
        sg = _sigmoid(gate)
        silu = gate * sg
        dgu_ref[0] = (da * up * (sg * (1.0 + gate * (1.0 - sg)))).astype(BF16)
        dgu_ref[1] = (da * silu).astype(BF16)
        part = _dot_tn((silu * up).astype(BF16), df_rows)

        @pl.when(i == 0)
        def _():
            acc_ref[...] = part

        @pl.when(i > 0)
        def _():
            acc_ref[...] += part

        @pl.when(i == ni - 1)
        def _():
            dwo_ref[...] = acc_ref[...].astype(BF16)

    body, extra = _ordered_after(body, 3, after)
    return _pallas(
        body, grid=(FF // FFN_TF, ni), name=name,
        in_specs=[df_all, gu, wo] + extra, out_specs=[gu, wo],
        out_shape=[jax.ShapeDtypeStruct((2, S, FF), BF16), jax.ShapeDtypeStruct((FF, D), BF16)],
        scratch_shapes=[pltpu.VMEM((FFN_TF, D), F32)],
        compiler_params=_params(40, ("arbitrary", "arbitrary")),
    )(df, gu_arr, w_out, *after)


def _ffn_bwd_dx(dgu, dhp, h, g, w_in, first_tile, ntiles, name, after=(), into=None):
    nj = FF // FFN_TF
    wg, wu, _, _, vec, _ = _ffn_specs()
    row_in = pl.BlockSpec((FFN_TS, D), lambda i, j: (first_tile + i, 0))
    dgu_spec = pl.BlockSpec((2, FFN_TS, FFN_TF), lambda i, j: (0, first_tile + i, j))
    after = tuple(after) + (() if into is None else (into,))

    def body(dgu_ref, dhp_ref, h_ref, g_ref, wg_ref, wu_ref, dh_ref, dg_ref, acc_ref):
        i, j = pl.program_id(0), pl.program_id(1)

        @pl.when(j == 0)
        def _():
            acc_ref[...] = jnp.zeros_like(acc_ref)

        acc_ref[...] += _dot_nt(dgu_ref[0], wg_ref[...]) + _dot_nt(dgu_ref[1], wu_ref[...])

        @pl.when(j == nj - 1)
        def _():
            def norm_bwd(rows, dg):
                dx, dg_rows = _rms_bwd(acc_ref[rows, :], h_ref[rows, :], g_ref[...])
                dh_ref[rows, :] = dhp_ref[rows, :] + dx
                return dg + dg_rows
            dg = _row_loop(FFN_TS, norm_bwd, jnp.zeros((1, D), F32))

            @pl.when(i == 0)
            def _():
                dg_ref[...] = dg

            @pl.when(i > 0)
            def _():
                dg_ref[...] += dg

    body, extra = _ordered_after(body, 6, after)
    return _pallas(
        body, grid=(ntiles, nj), name=name,
        in_specs=[dgu_spec, row_in, row_in, vec, wg, wu] + extra, out_specs=[row_in, vec],
        out_shape=[jax.ShapeDtypeStruct((S, D), F32), jax.ShapeDtypeStruct((1, D), F32)],
        input_output_aliases={} if into is None else {6 + len(after) - 1: 0},
        scratch_shapes=[pltpu.VMEM((FFN_TS, D), F32)],
        compiler_params=_params(48, ("arbitrary", "arbitrary")),
    )(dgu, dhp, h, g, w_in, w_in, *after)


def _tn_matmul(a, b, a_spec, b_spec, out_shape, out_spec, grid, name, vmem_mb, after=()):
    def body(a_ref, b_ref, o_ref):
        o_ref[...] = _dot_tn(a_ref[...], b_ref[...]).astype(o_ref.dtype)

    body, extra = _ordered_after(body, 2, after)
    return _pallas(body, grid=grid, name=name, in_specs=[a_spec, b_spec] + extra, out_specs=out_spec,
                          out_shape=out_shape, compiler_params=_params(vmem_mb))(a, b, *after)


def _resident(shape):
    return pl.BlockSpec(shape, lambda i: (0,) * len(shape), pipeline_mode=pl.Buffered(1))


def _norm_matmul(h, g, w, w_tail, name, after=(), ts=256):
    main = w.shape[0] // 128 * 128
    n_out = main + w_tail.shape[0]

    def body(h_ref, g_ref, w_ref, wt_ref, u_ref, n_ref):
        def norm(rows, c):
            n_ref[rows, :] = _rms(h_ref[rows, :], g_ref[...]).astype(BF16)
            return c
        _row_loop(ts, norm)
        n = n_ref[...]
        u_ref[:, :main] = _dot_nt(n, w_ref[:main, :])
        u_ref[:, main:] = _dot_nt(n, wt_ref[...])

    body, extra = _ordered_after(body, 4, after)
    return _pallas(
        body, grid=(S // ts,), name=name,
        in_specs=[pl.BlockSpec((ts, D), lambda i: (i, 0)), pl.BlockSpec((1, D), lambda i: (0, 0)), _resident(w.shape),
                  _resident(w_tail.shape)] + extra,
        out_specs=[pl.BlockSpec((ts, n_out), lambda i: (i, 0)), pl.BlockSpec((ts, D), lambda i: (i, 0))],
        out_shape=[jax.ShapeDtypeStruct((S, n_out), F32), jax.ShapeDtypeStruct((S, D), BF16)],
        compiler_params=_params(48, ("arbitrary",)),
    )(h, g, w, w_tail, *after)


def _matmul_residual(a, w, res, name, after=(), ts=512):
    k, n_out = w.shape

    def body(a_ref, w_ref, r_ref, o_ref):
        o_ref[...] = r_ref[...] + _dot(a_ref[...], w_ref[...])

    body, extra = _ordered_after(body, 3, after)
    return _pallas(
        body, grid=(S // ts,), name=name,
        in_specs=[pl.BlockSpec((ts, k), lambda i: (i, 0)), _resident(w.shape), pl.BlockSpec((ts, n_out), lambda i: (i, 0))] + extra,
        out_specs=pl.BlockSpec((ts, n_out), lambda i: (i, 0)),
        out_shape=jax.ShapeDtypeStruct((S, n_out), F32),
        compiler_params=_params(40),
    )(a, w, res, *after)


def _nt_matmul(a, w, name, after=(), ts=512):
    n_out, k = w.shape

    def body(a_ref, w_ref, o_ref):
        o_ref[...] = _dot_nt(a_ref[...], w_ref[...])

    body, extra = _ordered_after(body, 2, after)
    return _pallas(
        body, grid=(S // ts,), name=name,
        in_specs=[pl.BlockSpec((ts, k), lambda i: (i, 0)), _resident(w.shape)] + extra,
        out_specs=pl.BlockSpec((ts, n_out), lambda i: (i, 0)),
        out_shape=jax.ShapeDtypeStruct((S, n_out), F32),
        compiler_params=_params(40),
    )(a, w, *after)


def _matmul_normbwd(du, w, w_tail, h, g, dres, name, after=(), ts=256):
    main = w.shape[0] // 128 * 128
    n_in = main + w_tail.shape[0]
    row = pl.BlockSpec((ts, D), lambda i: (i, 0))
    vec = pl.BlockSpec((1, D), lambda i: (0, 0))

    def body(du_ref, w_ref, wt_ref, h_ref, g_ref, dres_ref, dh_ref, dg_ref, half_ref, acc_ref):
        i = pl.program_id(0)
        acc_ref[...] = _dot(du_ref[:, :main], w_ref[:main, :]) + _dot(du_ref[:, main:], wt_ref[...])

        def norm_bwd(rows, dg):
            dx, dg_rows = _rms_bwd(acc_ref[rows, :], h_ref[rows, :], g_ref[...])
            dh = dres_ref[rows, :] + dx
            dh_ref[rows, :] = dh
            half_ref[rows, :] = (0.5 * dh).astype(BF16)
            return dg + dg_rows
        dg = _row_loop(ts, norm_bwd, jnp.zeros((1, D), F32))

        @pl.when(i == 0)
        def _():
            dg_ref[...] = dg

        @pl.when(i > 0)
        def _():
            dg_ref[...] += dg

    body, extra = _ordered_after(body, 6, after)
    return _pallas(
        body, grid=(S // ts,), name=name,
        in_specs=[pl.BlockSpec((ts, n_in), lambda i: (i, 0)), _resident(w.shape), _resident(w_tail.shape), row, vec, row] + extra,
        out_specs=[row, vec, row],
        out_shape=[jax.ShapeDtypeStruct((S, D), F32), jax.ShapeDtypeStruct((1, D), F32), jax.ShapeDtypeStruct((S, D), BF16)],
        scratch_shapes=[pltpu.VMEM((ts, D), F32)],
        compiler_params=_params(52, ("arbitrary",)),
    )(du, w, w_tail, h, g, dres, *after)


def _pool_specs():
    blk = pl.BlockSpec((S, PG), lambda gi: (0, gi))
    wp = pl.BlockSpec((None, PG, PG), lambda gi: (gi, 0, 0))
    sc = pl.BlockSpec((1, PG), lambda gi: (0, gi))
    return blk, wp, sc


def _pool_fwd(u, wp_b, scale):
    blk, wp, sc = _pool_specs()

    def body(u_ref, wp_ref, sc_ref, y_ref, pooled_ref):
        win = 2 << pl.program_id(0)
        row = lax.broadcasted_iota(jnp.int32, (S, PG), 0)
        x = u_ref[...]
        s = x
        for k in (1, 2, 4, 8):
            s = s + jnp.where((row >= k) & (k < win), pltpu.roll(s, k, 0), 0.0)
        cnt = jnp.minimum(row + 1, win).astype(F32)
        pooled = (s / cnt - x).astype(BF16)
        pooled_ref[...] = pooled
        y_ref[...] = (_dot(pooled, wp_ref[...]) * sc_ref[...]).astype(BF16)

    return _pallas(
        body, grid=(len(POOL_WINDOWS),), name="pool_fwd", in_specs=[blk, wp, sc], out_specs=[blk, blk],
        out_shape=[jax.ShapeDtypeStruct((S, D_POOL), BF16), jax.ShapeDtypeStruct((S, D_POOL), BF16)],
        compiler_params=_params(40),
    )(u, wp_b, scale)


def _pool_bwd(dy, pooled, wp_b, scale):
    blk, wp, sc = _pool_specs()

    def body(dy_ref, p_ref, wp_ref, sc_ref, du_ref, dwp_ref, dsc_ref):
        win = 2 << pl.program_id(0)
        row = lax.broadcasted_iota(jnp.int32, (S, PG), 0)
        dyv = dy_ref[...]
        pooled = p_ref[...]
        w = wp_ref[...]
        dsc_ref[...] = jnp.sum(dyv * _dot(pooled, w), axis=0, keepdims=True)
        dz = (dyv * sc_ref[...]).astype(BF16)
        dwp_ref[...] = _dot_tn(pooled, dz)
        dpooled = _dot_nt(dz, w)
        cnt = jnp.minimum(row + 1, win).astype(F32)
        fs = dpooled / cnt
        for k in (1, 2, 4, 8):
            fs = fs + jnp.where((row < S - k) & (k < win), pltpu.roll(fs, S - k, 0), 0.0)
        du_ref[...] = (fs - dpooled).astype(BF16)

    return _pallas(
        body, grid=(len(POOL_WINDOWS),), name="pool_bwd", in_specs=[blk, blk, wp, sc], out_specs=[blk, wp, sc],
        out_shape=[jax.ShapeDtypeStruct((S, D_POOL), BF16), jax.ShapeDtypeStruct((len(POOL_WINDOWS), PG, PG), F32),
                   jax.ShapeDtypeStruct((1, D_POOL), F32)],
        compiler_params=_params(40),
    )(dy, pooled, wp_b, scale)


GLA_CPS = 2
GLA_ROWS = GLA_CPS * CH
GLA_STEPS = NCH // GLA_CPS


def _gla_in_specs(step_of):
    def at(width, col):
        return pl.BlockSpec((GLA_ROWS, width), lambda n: (step_of(n), col))
    return [at(H * DK, O_Q // (H * DK)), at(H * DK, O_K // (H * DK)), at(H * DV, O_V // (H * DV)),
            at(H * DV, O_G // (H * DV)), at(RANK_PAD, O_R // RANK_PAD)]


def _gla_decay_terms(lr, wa_ref, ba_ref, q, k):
    row = lax.broadcasted_iota(jnp.int32, (CH, CH), 0)
    col = lax.broadcasted_iota(jnp.int32, (CH, CH), 1)
    tril = row >= col
    z = _dot(lr.astype(BF16), wa_ref[...]) + ba_ref[...]
    la = _log_sigmoid(z) / GATE_NORM
    b = _tri_dot(jnp.where(tril, 1.0, 0.0).astype(BF16), la)
    bl = jnp.sum(la, axis=0, keepdims=True)
    e_b, e_nb, e_tb = jnp.exp(b), jnp.exp(-b), jnp.exp(bl - b)
    q_dec = (q * QK_SCALE) * e_b
    return tril, z, e_b, e_nb, e_tb, jnp.exp(bl), q_dec, k * e_nb, k * e_tb


def _gla_fwd(u, wa_b, ba, gn):
    wide = pl.BlockSpec((GLA_ROWS, H * DV), lambda n: (n, 0))

    def body(q_ref, k_ref, v_ref, g_ref, lr_ref, wa_ref, ba_ref, gn_ref, y_ref, o_ref, st_ref, state):
        @pl.when(pl.program_id(0) == 0)
        def _():
            state[...] = jnp.zeros_like(state)

        for c in range(GLA_CPS):
            rows = slice(c * CH, (c + 1) * CH)
            tril, _, _, _, _, dec, q_dec, k_inv, k_tail = _gla_decay_terms(lr_ref[rows, :], wa_ref, ba_ref, q_ref[rows, :],
                                                                          k_ref[rows, :])
            for hd in range(H):
                ks, vs = slice(hd * DK, (hd + 1) * DK), slice(hd * DV, (hd + 1) * DV)
                qb, kib, ktb = q_dec[:, ks].astype(BF16), k_inv[:, ks].astype(BF16), k_tail[:, ks].astype(BF16)
                vb = v_ref[rows, vs].astype(BF16)
                p = jnp.where(tril, _dot_nt(qb, kib), 0.0)
                st = state[hd]
                o = _dot(p.astype(BF16), vb) + _dot_nt(qb, st.astype(BF16))
                st_ref[c, hd] = st
                state[hd] = st * dec[:, ks] + _dot_tn(vb, ktb)
                o_ref[rows, vs] = o
                on = _rms(o, gn_ref[...])
                gg = g_ref[rows, vs]
                y_ref[rows, vs] = (on * (gg * _sigmoid(gg))).astype(BF16)

    return _pallas(
        body, grid=(GLA_STEPS,), name="gla_fwd",
        in_specs=_gla_in_specs(lambda n: n) + [pl.BlockSpec((RANK_PAD, H * DK), lambda n: (0, 0)),
                                               pl.BlockSpec((1, H * DK), lambda n: (0, 0)),
                                               pl.BlockSpec((1, DV), lambda n: (0, 0))],
        out_specs=[wide, wide, pl.BlockSpec((GLA_CPS, H, DV, DK), lambda n: (n, 0, 0, 0))],
        out_shape=[jax.ShapeDtypeStruct((S, H * DV), BF16), jax.ShapeDtypeStruct((S, H * DV), F32),
                   jax.ShapeDtypeStruct((NCH, H, DV, DK), F32)],
        scratch_shapes=[pltpu.VMEM((H, DV, DK), F32)],
        compiler_params=_params(32, ("arbitrary",)),
    )(u, u, u, u, u, wa_b, ba, gn)


GLA_DU = 2 * H * DK + 2 * H * DV + RANK_PAD


def _gla_bwd(u, o_arr, states, dy, wa_b, ba, gn, after=()):
    rev = lambda n: GLA_STEPS - 1 - n
    wide = pl.BlockSpec((GLA_ROWS, H * DV), lambda n: (rev(n), 0))

    def body(q_ref, k_ref, v_ref, g_ref, lr_ref, o_ref, st_ref, dy_ref, wa_ref, ba_ref, gn_ref,
             du_ref, dwa_ref, dba_ref, dgn_ref, gstate, db_scr, dbl_scr):
        @pl.when(pl.program_id(0) == 0)
        def _():
            gstate[...] = jnp.zeros_like(gstate)
            dwa_ref[...] = jnp.zeros_like(dwa_ref)
            dba_ref[...] = jnp.zeros_like(dba_ref)
            dgn_ref[...] = jnp.zeros_like(dgn_ref)

        gnv = gn_ref[...]
        dgn = jnp.zeros((1, DV), F32)
        dwa = jnp.zeros((RANK_PAD, H * DK), F32)
        dba = jnp.zeros((1, H * DK), F32)
        srow = lax.broadcasted_iota(jnp.int32, (CH, CH), 0)
        scol = lax.broadcasted_iota(jnp.int32, (CH, CH), 1)
        for c in reversed(range(GLA_CPS)):
            rows = slice(c * CH, (c + 1) * CH)
            lr = lr_ref[rows, :]
            tril, z, e_b, e_nb, e_tb, dec, q_dec, k_inv, k_tail = _gla_decay_terms(lr, wa_ref, ba_ref, q_ref[rows, :],
                                                                                  k_ref[rows, :])
            for hd in range(H):
                ks, vs = slice(hd * DK, (hd + 1) * DK), slice(hd * DV, (hd + 1) * DV)
                qh, kih, kth = q_dec[:, ks], k_inv[:, ks], k_tail[:, ks]
                qb, kib, ktb = qh.astype(BF16), kih.astype(BF16), kth.astype(BF16)
                vb = v_ref[rows, vs].astype(BF16)
                o = o_ref[rows, vs]
                gg = g_ref[rows, vs]
                dyh = dy_ref[rows, vs]
                r = lax.rsqrt(jnp.mean(o * o, axis=-1, keepdims=True) + EPS)
                xh = o * r
                sg = _sigmoid(gg)
                dgate = dyh * (xh * gnv) * (sg * (1.0 + gg * (1.0 - sg)))
                don = dyh * (gg * sg)
                dgn = dgn + jnp.sum(don * xh, axis=0, keepdims=True)
                dxh = don * gnv
                d_o = (r * (dxh - xh * jnp.mean(dxh * xh, axis=-1, keepdims=True))).astype(BF16)
                pb = jnp.where(tril, _dot_nt(qb, kib), 0.0).astype(BF16)
                dpb = jnp.where(tril, _dot_nt(d_o, vb), 0.0).astype(BF16)
                gt = gstate[hd]
                gtb = gt.astype(BF16)
                st = st_ref[c, hd]
                dv = _dot_tn(pb, d_o) + _dot_nt(ktb, gtb)
                dq_dec = _dot(dpb, kib) + _dot(d_o, st.astype(BF16))
                dk_inv = _dot_tn(dpb, qb)
                dk_tail = _dot(vb, gtb)
                ddec = jnp.sum(gt * st, axis=0, keepdims=True)
                gstate[hd] = _dot_tn(d_o, qb) + dec[:, ks] * gt
                du_ref[rows, ks] = (dq_dec * QK_SCALE * e_b[:, ks]).astype(BF16)
                du_ref[rows, H * DK + hd * DK:H * DK + (hd + 1) * DK] = (dk_inv * e_nb[:, ks] + dk_tail * e_tb[:, ks]).astype(BF16)
                du_ref[rows, 2 * H * DK + hd * DV:2 * H * DK + (hd + 1) * DV] = dv.astype(BF16)
                du_ref[rows, 2 * H * DK + H * DV + hd * DV:2 * H * DK + H * DV + (hd + 1) * DV] = dgate.astype(BF16)
                db_scr[c, :, ks] = dq_dec * qh - dk_inv * kih - dk_tail * kth
                dbl_scr[c, :, ks] = jnp.sum(dk_tail * kth, axis=0, keepdims=True) + ddec * dec[:, ks]
            dla = _tri_dot(jnp.where(srow <= scol, 1.0, 0.0).astype(BF16), db_scr[c]) + dbl_scr[c]
            dz = dla * (1.0 / GATE_NORM) * _sigmoid(-z)
            dzb = dz.astype(BF16)
            du_ref[rows, GLA_DU - RANK_PAD:] = _dot_nt(dzb, wa_ref[...]).astype(BF16)
            dwa = dwa + _dot_tn(lr.astype(BF16), dzb)
            dba = dba + jnp.sum(dz, axis=0, keepdims=True)
        dgn_ref[...] += dgn
        dwa_ref[...] += dwa
        dba_ref[...] += dba

    full = lambda shape: pl.BlockSpec(shape, lambda n: (0,) * len(shape))
    body, extra = _ordered_after(body, 11, after)
    return _pallas(
        body, grid=(GLA_STEPS,), name="gla_bwd",
        in_specs=_gla_in_specs(rev) + [wide, pl.BlockSpec((GLA_CPS, H, DV, DK), lambda n: (rev(n), 0, 0, 0)),
                                       pl.BlockSpec((GLA_ROWS, H * DV), lambda n: (rev(n), 1)),
                                       full((RANK_PAD, H * DK)), full((1, H * DK)), full((1, DV))] + extra,
        out_specs=[pl.BlockSpec((GLA_ROWS, GLA_DU), lambda n: (rev(n), 0)), full((RANK_PAD, H * DK)), full((1, H * DK)),
                   full((1, DV))],
        out_shape=[jax.ShapeDtypeStruct((S, GLA_DU), BF16), jax.ShapeDtypeStruct((RANK_PAD, H * DK), F32),
                   jax.ShapeDtypeStruct((1, H * DK), F32), jax.ShapeDtypeStruct((1, DV), F32)],
        scratch_shapes=[pltpu.VMEM((H, DV, DK), F32), pltpu.VMEM((GLA_CPS, CH, H * DK), F32),
                        pltpu.VMEM((GLA_CPS, 1, H * DK), F32)],
        compiler_params=_params(32, ("arbitrary",)),
    )(u, u, u, u, u, o_arr, states, dy, wa_b, ba, gn, *after)


def _ffn_dw_in(n, dgu, tag, after=(), first_tile=0, ntiles=D // 512):
    return _tn_matmul(n, dgu, pl.BlockSpec((S, 512), lambda s, m: (0, first_tile + m)),
                      pl.BlockSpec((None, S, WIN_SHARD), lambda s, m: (s // (NDEV // 2), 0, s % (NDEV // 2))),
                      jax.ShapeDtypeStruct((512 * ntiles, 2 * FF), BF16), pl.BlockSpec((512, WIN_SHARD), lambda s, m: (m, s)),
                      (NDEV, ntiles), tag + "_dw_in", 32, after)


def _ffn_dw_out(act, df, tag, after=()):
    return _tn_matmul(act, df, pl.BlockSpec((S, 512), lambda m: (0, m)), pl.BlockSpec((S, D), lambda m: (0, 0)),
                      jax.ShapeDtypeStruct((FF, D), BF16), pl.BlockSpec((512, D), lambda m: (m, 0)),
                      (FF // 512,), tag + "_dw_out", 40, after)


def _fwd_ffn1(x, w, after=()):
    return _ffn_fwd(x, w["ffn1_norm"], w["ffn1_w_in"], w["ffn1_w_out"], "ffn1_fwd", after)


def _fwd_mixer(h1, w, after=(), at=lambda point, value: ()):
    u, n2 = _norm_matmul(h1, w["mix_norm"], w["w_in_mix"], w["w_in_mix_tail"], "mix_in", after)
    y_pool, pooled = _pool_fwd(u, w["w_pool"], w["pool_scale"])
    y_gla, o_gla, states = _gla_fwd(u, w["w_alpha"], w["b_alpha"], w["gla_norm"])
    y = jnp.concatenate([y_pool, y_gla], axis=1)
    h2 = _matmul_residual(y, w["w_out_mix"], h1, "mix_out", at("gla_fwd", y))
    return h2, dict(u=u, n2=n2, pooled=pooled, o_gla=o_gla, states=states, y=y)


def _fwd_ffn2_loss(h2, tgt, w, after=()):
    dh3, n3, gu3, d_final, loss = _ffn_fwd(h2, w["ffn2_norm"], w["ffn2_w_in"], w["ffn2_w_out"], "ffn2_fwd", after,
                                           head=(w["final_norm"], tgt))
    return dh3, d_final, loss, n3, gu3


def _backward(x, h1, h2, n1, gu1, sv, n3, gu3, dh3, w, at=lambda point, value: ()):
    g = {}
    dgu3, act3, df3, dh2, dh2b, g["ffn2_norm"] = _ffn_bwd_x(dh3, h2, w["ffn2_norm"], gu3, w["ffn2_w_in"], w["ffn2_w_out"], "ffn2_bwd")
    g["ffn2_w_out"] = _ffn_dw_out(act3, df3, "ffn2")
    g["ffn2_w_in"] = _ffn_dw_in(n3, dgu3, "ffn2", at("ffn2_w_out", g["ffn2_w_out"]))
    dy = _nt_matmul(dh2b, w["w_out_mix"], "mix_out_bwd", at("ffn2_w_in", g["ffn2_w_in"]))
    g["w_out_mix"] = _tn_matmul(sv["y"], dh2b, pl.BlockSpec((S, 512), lambda m: (0, m)), pl.BlockSpec((S, D), lambda m: (0, 0)),
                                jax.ShapeDtypeStruct((D, D), BF16), pl.BlockSpec((512, D), lambda m: (m, 0)), (D // 512,),
                                "mix_out_dw", 40)
    du_pool, g["w_pool"], g["pool_scale"] = _pool_bwd(dy, sv["pooled"], w["w_pool"], w["pool_scale"])
    du_gla, g["w_alpha"], g["b_alpha"], g["gla_norm"] = _gla_bwd(sv["u"], sv["o_gla"], sv["states"], dy, w["w_alpha"], w["b_alpha"],
                                                                 w["gla_norm"], at("pool_bwd", du_pool))
    du = jnp.concatenate([du_pool, du_gla], axis=1)
    g["w_in_mix"] = _tn_matmul(du, sv["n2"], pl.BlockSpec((S, 1408), lambda j, m: (0, j)), pl.BlockSpec((S, 512), lambda j, m: (0, m)),
                               jax.ShapeDtypeStruct((D_IN, D), BF16), pl.BlockSpec((1408, 512), lambda j, m: (j, m)),
                               (D_IN_PAD // 1408, D // 512), "mix_in_dw", 32)
    dh1, g["mix_norm"], df1 = _matmul_normbwd(du, w["w_in_mix"], w["w_in_mix_tail"], h1, w["mix_norm"], dh2, "mix_in_bwd",
                                              at("mixer_weights", g))
    dgu1, g["ffn1_w_out"] = _ffn_bwd_gu(df1, gu1, w["ffn1_w_out"], "ffn1_bwd_gu", at("dh1", dh1))
    half = S // FFN_TS // 2
    dx, dn_a = _ffn_bwd_dx(dgu1, dh1, x, w["ffn1_norm"], w["ffn1_w_in"], 0, half, "ffn1_bwd_dx_a",
                           at("ffn1_w_out", g["ffn1_w_out"]))
    g["ffn1_w_in_top"] = _ffn_dw_in(n1, dgu1, "ffn1_top", at("ffn1_dx_a", dx), 0, D // 1024)
    g["ffn1_w_in_bottom"] = _ffn_dw_in(n1, dgu1, "ffn1_bottom", at("ffn1_w_in_top", g["ffn1_w_in_top"]), D // 1024, D // 1024)
    dx, dn_b = _ffn_bwd_dx(dgu1, dh1, x, w["ffn1_norm"], w["ffn1_w_in"], half, half, "ffn1_bwd_dx_b",
                           at("ffn1_w_in_bottom", g["ffn1_w_in_bottom"]), into=dx)
    g["ffn1_norm"] = dn_a + dn_b
    return dx, g


def _local_step(x, tgt, w):
    h1, n1, gu1 = _fwd_ffn1(x, w)
    h2, sv = _fwd_mixer(h1, w)
    dh3, d_final, loss, n3, gu3 = _fwd_ffn2_loss(h2, tgt, w)
    dx, g = _backward(x, h1, h2, n1, gu1, sv, n3, gu3, dh3, w)
    return loss, dx, dict(final_norm=d_final, **g)


def _coords(p):
    return (p // 4, (p // 2) % 2, p % 2)


NCHIP = 4


def _place():
    return lax.axis_index("x"), lax.axis_index("y"), lax.axis_index("c")


def _rel_chip(x, y, rel):
    return ((1 - x) if rel & 1 else x, (1 - y) if rel & 2 else y)


def _dev_index(x, y, c):
    return 4 * x + 2 * y + c


def _cols(ref, p, width):
    return ref.at[:, pl.ds(pl.multiple_of(p * width, 128), width)]


def _sems(na, n):
    return [pltpu.SemaphoreType.DMA((na, n)), pltpu.SemaphoreType.DMA((na, n)), pltpu.SemaphoreType.DMA((na,))]


def _gather(items, name):
    arrays = [a for a, _ in items]
    kinds = [k for _, k in items]
    na = len(arrays)
    out_shape = [jax.ShapeDtypeStruct((NDEV,) + a.shape if k == "bcast" else (a.shape[0], NDEV * a.shape[1]), a.dtype)
                 for a, k in items]

    def body(*refs):
        ins, outs = refs[:na], refs[na:2 * na]
        send_sems, recv_sems, local_sems = refs[2 * na:]
        x, y, c = _place()
        sibling = (x, y, 1 - c)
        here, over_x, over_y, across = (x, y), (1 - x, y), (x, 1 - y), (1 - x, 1 - y)

        def half(ref, h):
            rows = ref.shape[0] // 2
            return ref.at[pl.ds(h * rows, rows), :]

        def slab(a, chip, core, h=None):
            ref = _slab(outs[a], kinds[a], _dev_index(*chip, core), ins[a].shape[1])
            return ref if h is None else half(ref, h)

        def copy(a, k, src, dst, to):
            return pltpu.make_async_remote_copy(src, dst, send_sems.at[a, k], recv_sems.at[a, k], device_id=to, device_id_type=MESH)

        sent = []

        def send(a, k, src, dst, to):
            sent.append(copy(a, k, src, dst, to))
            sent[-1].start()

        def arrived(a, k, chip, core, h=None):
            ref = slab(a, chip, core, h)
            copy(a, k, ref, ref, sibling).wait_recv()
            return ref

        local = [pltpu.make_async_copy(ins[a], slab(a, here, c), local_sems.at[a]) for a in range(na)]
        for cp in local:
            cp.start()
        for k, h, chip in ((1, 0, over_x), (4, 1, over_y), (2, 1, over_x), (5, 0, over_y)):
            for a in range(na):
                send(a, k, half(ins[a], h), slab(a, here, c, h), (*chip, c))
        for a in range(na):
            send(a, 0, ins[a], slab(a, here, c), sibling)
        for k, chip, h, onward, to, down in ((1, over_x, 0, 6, over_y, 7), (4, over_y, 1, 3, over_x, 10),
                                             (2, over_x, 1, None, None, 8), (5, over_y, 0, None, None, 9),
                                             (3, across, 1, None, None, 12), (6, across, 0, None, None, 11)):
            for a in range(na):
                ref = arrived(a, k, chip, c, h)
                if onward is not None:
                    send(a, onward, ref, ref, (*to, c))
                send(a, down, ref, ref, sibling)
        for a in range(na):
            arrived(a, 0, here, 1 - c)
        for k, chip, h in ((7, over_x, 0), (10, over_y, 1), (8, over_x, 1), (9, over_y, 0), (12, across, 1), (11, across, 0)):
            for a in range(na):
                arrived(a, k, chip, 1 - c, h)
        for cp in sent:
            cp.wait_send()
        for cp in local:
            cp.wait()

    return pl.pallas_call(body, name=name, in_specs=[ANY] * na, out_specs=[ANY] * na, out_shape=out_shape,
                          scratch_shapes=_sems(na, 13))(*arrays)


def _pair_add(own, kind, got, table, tr, name):
    _, rows, cols = got.shape
    if kind == "scatter":
        own_spec = pl.BlockSpec((None, tr, cols), lambda rel, i, t: (t[rel], i, 0))
    else:
        own_spec = pl.BlockSpec((tr, cols), lambda rel, i, t: (i, t[rel]))
    blk = pl.BlockSpec((None, tr, cols), lambda rel, i, t: (rel, i, 0))

    def body(t_ref, a_ref, b_ref, o_ref):
        o_ref[...] = (a_ref[...].astype(F32) + b_ref[...].astype(F32)).astype(o_ref.dtype)

    return _pallas(
        body, name=name, out_shape=jax.ShapeDtypeStruct(got.shape, got.dtype),
        grid_spec=pltpu.PrefetchScalarGridSpec(num_scalar_prefetch=1, grid=(NCHIP, rows // tr), in_specs=[own_spec, blk],
                                               out_specs=blk),
        compiler_params=_params(32),
    )(table, own, got)


HBM = pl.BlockSpec(memory_space=pltpu.HBM)
SEM = pl.BlockSpec(memory_space=pltpu.SEMAPHORE)
DATAFLOW = pltpu.SideEffectType.DATAFLOW_SIDE_EFFECTING


def _pair_copies(kinds):
    def describe(srcs, lands, send_sems, recv_sems):
        x, y, c = _place()
        na = len(srcs)
        for rel in range(NCHIP):
            p = _dev_index(*_rel_chip(x, y, rel), 1 - c)
            for a in range(na):
                src = srcs[a].at[p] if kinds[a] == "scatter" else _cols(srcs[a], p, srcs[a].shape[1] // NDEV)
                cp = pltpu.make_async_remote_copy(src, lands[a].at[rel], send_sems.at[rel * na + a], recv_sems.at[rel * na + a],
                                                  device_id=(x, y, 1 - c), device_id_type=MESH)
                yield cp, cp
    return describe


def _chip_copies(srcs, lands, send_sems, recv_sems):
    x, y, c = _place()
    na = len(srcs)
    for rel in range(1, NCHIP):
        for a in range(na):
            i = (rel - 1) * na + a
            cp = pltpu.make_async_remote_copy(srcs[a].at[rel], lands[a].at[rel], send_sems.at[i], recv_sems.at[i],
                                              device_id=(*_rel_chip(x, y, rel), c), device_id_type=MESH)
            yield cp, cp


def _slab(ref, kind, s, width):
    return _cols(ref, s, width) if kind == "bcast_cols" else ref.at[s]


def _all_copies(srcs, lands, send_sems, recv_sems):
    x, y, c = _place()
    na = len(srcs)
    me = _dev_index(x, y, c)
    for a in range(na):
        yield pltpu.make_async_copy(srcs[a], lands[a].at[me], send_sems.at[a]), None
    for k in range(1, NDEV):
        to, frm = (me + k) % NDEV, (me + NDEV - k) % NDEV
        for a in range(na):
            i = k * na + a
            send = pltpu.make_async_remote_copy(srcs[a], lands[a].at[me], send_sems.at[i], recv_sems.at[i],
                                                device_id=_coords(to), device_id_type=MESH)
            arrival = pltpu.make_async_remote_copy(srcs[a], lands[a].at[frm], send_sems.at[i], recv_sems.at[i],
                                                   device_id=_coords(to), device_id_type=MESH)
            yield send, arrival


def _gather_copies(kinds):
    def describe(srcs, lands, send_sems, recv_sems):
        x, y, c = _place()
        na = len(srcs)
        me = _dev_index(x, y, c)
        for a in range(na):
            yield pltpu.make_async_copy(srcs[a], _slab(lands[a], kinds[a], me, srcs[a].shape[-1]),
                                        send_sems.at[NCHIP * na + a]), None
        for rel in range(NCHIP):
            to = (x, y, 1 - c) if rel == 0 else (*_rel_chip(x, y, rel), c)
            for a in range(na):
                width = srcs[a].shape[-1]
                i = rel * na + a
                send = pltpu.make_async_remote_copy(srcs[a], _slab(lands[a], kinds[a], me, width), send_sems.at[i], recv_sems.at[i],
                                                    device_id=to, device_id_type=MESH)
                arrival = pltpu.make_async_remote_copy(srcs[a], _slab(lands[a], kinds[a], _dev_index(*to), width), send_sems.at[i],
                                                       recv_sems.at[i], device_id=to, device_id_type=MESH)
                yield send, arrival
    return describe


def _relay_copies(kinds, widths, phase):
    def describe(srcs, lands, send_sems, recv_sems):
        x, y, c = _place()
        na = len(lands)
        here, over_x, over_y, across = (x, y), (1 - x, y), (x, 1 - y), (1 - x, 1 - y)
        sibling = (x, y, 1 - c)

        def half(ref, h):
            rows = ref.shape[0] // 2
            return ref if h is None else ref.at[pl.ds(h * rows, rows), :]

        def slab(a, chip, core, h=None):
            return half(_slab(lands[a], kinds[a], _dev_index(*chip, core), widths[a]), h)

        def pair(a, k, src, dst, to, arrival):
            i = k * na + a
            return (pltpu.make_async_remote_copy(src, dst, send_sems.at[i], recv_sems.at[i], device_id=to, device_id_type=MESH),
                    pltpu.make_async_remote_copy(arrival, arrival, send_sems.at[i], recv_sems.at[i], device_id=to,
                                                 device_id_type=MESH))

        if phase == 1:
            for a in range(na):
                yield pltpu.make_async_copy(srcs[a], slab(a, here, c), send_sems.at[a]), None
            for k, h, chip in ((1, 0, over_x), (2, 1, over_y), (3, 1, over_x), (4, 0, over_y)):
                for a in range(na):
                    yield pair(a, k, half(srcs[a], h), slab(a, here, c, h), (*chip, c), slab(a, chip, c, h))
            for a in range(na):
                yield pair(a, 5, srcs[a], slab(a, here, c), sibling, slab(a, here, 1 - c))
        elif phase == 2:
            for a in range(na):
                yield pair(a, 0, slab(a, over_x, c, 0), slab(a, over_x, c, 0), (*over_y, c), slab(a, across, c, 0))
                yield pair(a, 1, slab(a, over_y, c, 1), slab(a, over_y, c, 1), (*over_x, c), slab(a, across, c, 1))
            for k, (chip, h) in enumerate(((over_x, 0), (over_y, 1), (over_x, 1), (over_y, 0)), start=2):
                for a in range(na):
                    yield pair(a, k, slab(a, chip, c, h), slab(a, chip, c, h), sibling, slab(a, chip, 1 - c, h))
        else:
            for h in (0, 1):
                for a in range(na):
                    yield pair(a, h, slab(a, across, c, h), slab(a, across, c, h), sibling, slab(a, across, 1 - c, h))
    return describe


RELAY_COPIES = {1: 6, 2: 6, 3: 2}


def _relay_finish(arrays, kinds, widths, name):
    na = len(arrays)
    describe = _relay_copies(kinds, widths, 3)

    def body(*refs):
        copies = list(describe((), refs[na:2 * na], refs[2 * na], refs[2 * na + 1]))
        for send, _ in copies:
            send.start()
        for send, arrival in copies:
            send.wait_send()
            arrival.wait_recv()

    return pl.pallas_call(body, name=name, in_specs=[ANY] * na, out_specs=[ANY] * na,
                          out_shape=[jax.ShapeDtypeStruct(a.shape, a.dtype) for a in arrays],
                          input_output_aliases={i: i for i in range(na)},
                          scratch_shapes=[pltpu.SemaphoreType.DMA((RELAY_COPIES[3] * na,))] * 2)(*arrays)


def _pass_to_sibling(arrays, kinds, widths, name):
    na = len(arrays)

    def body(*refs):
        bufs = refs[na:2 * na]
        send_sems, recv_sems = refs[2 * na:]
        x, y, c = _place()
        copies = []
        for rel in range(1, NCHIP):
            for a in range(na):
                mine = _slab(bufs[a], kinds[a], _dev_index(*_rel_chip(x, y, rel), c), widths[a])
                theirs = _slab(bufs[a], kinds[a], _dev_index(*_rel_chip(x, y, rel), 1 - c), widths[a])
                send = pltpu.make_async_remote_copy(mine, mine, send_sems.at[a, rel], recv_sems.at[a, rel],
                                                    device_id=(x, y, 1 - c), device_id_type=MESH)
                send.start()
                copies.append((send, pltpu.make_async_remote_copy(theirs, theirs, send_sems.at[a, rel], recv_sems.at[a, rel],
                                                                  device_id=(x, y, 1 - c), device_id_type=MESH)))
        for send, arrival in copies:
            send.wait_send()
            arrival.wait_recv()

    return pl.pallas_call(body, name=name, in_specs=[ANY] * na, out_specs=[ANY] * na,
                          out_shape=[jax.ShapeDtypeStruct(a.shape, a.dtype) for a in arrays],
                          input_output_aliases={i: i for i in range(na)}, scratch_shapes=_sems(na, NCHIP)[:2])(*arrays)


def _start_copies(name, srcs, lands, describe, ncopies, after=()):
    arrays = list(srcs) + list(lands)
    ns, n, nin = len(srcs), len(arrays), len(arrays) + len(after)

    def body(*refs):
        for send, _ in describe(refs[:ns], refs[ns:n], refs[nin], refs[nin + 1]):
            send.start()
        refs[-1][...] = jnp.zeros_like(refs[-1])

    out = pl.pallas_call(
        body, name=name,
        out_shape=(pltpu.SemaphoreType.DMA((ncopies,)), pltpu.SemaphoreType.DMA((ncopies,)),
                   *[pltpu.HBM(a.shape, a.dtype) for a in arrays], jax.ShapeDtypeStruct((8, 128), F32)),
        in_specs=[HBM] * n + [ANY] * len(after), out_specs=(SEM, SEM, *[HBM] * n, pl.BlockSpec(memory_space=pltpu.VMEM)),
        input_output_aliases={i: 2 + i for i in range(n)},
        compiler_params=pltpu.CompilerParams(has_side_effects=DATAFLOW),
    )(*[pltpu.with_memory_space_constraint(a, pltpu.HBM) for a in arrays], *after)
    return out[0], out[1], list(out[2:2 + n]), out[-1]


def _wait_copies(name, send_sems, recv_sems, thru, ns, describe, after):
    n = len(thru)

    def body(*refs):
        for send, arrival in describe(refs[:ns], refs[ns:n], refs[n], refs[n + 1]):
            if arrival is None:
                send.wait()
            else:
                send.wait_send()
                arrival.wait_recv()

    out = pl.pallas_call(
        body, name=name, out_shape=tuple(pltpu.HBM(a.shape, a.dtype) for a in thru),
        in_specs=[HBM] * n + [SEM, SEM] + [ANY] * len(after), out_specs=tuple([HBM] * n),
        input_output_aliases={i: i for i in range(n)},
        compiler_params=pltpu.CompilerParams(has_side_effects=DATAFLOW),
    )(*thru, send_sems, recv_sems, *after)
    return list(out[:ns]), list(out[ns:])


def _adamw(parts, w, m, v, tr, name, tc=None, first_row=0, into=None):
    rows, cols = w.shape
    part_rows = parts[0][0].shape[1]
    tc = cols if tc is None else tc
    nparts = len(parts)
    tile0 = first_row // tr
    blk = pl.BlockSpec((tr, tc), lambda i, j: (tile0 + i, j))
    carried = [] if into is None else list(into)

    def slab_spec(s):
        return pl.BlockSpec((None, tr, tc), lambda i, j: (s, i, j))

    def body(*refs):
        p_refs = refs[:nparts]
        w_ref, m_ref, v_ref = refs[nparts:nparts + 3]
        g_ref, d_ref, nm_ref, nv_ref = refs[nparts + 3 + len(carried):]
        g = p_refs[0][...].astype(F32)
        for p_ref in p_refs[1:]:
            g = g + p_ref[...].astype(F32)
        nm = ADAM_B1 * m_ref[...] + (1.0 - ADAM_B1) * g
        nv = ADAM_B2 * v_ref[...] + (1.0 - ADAM_B2) * (g * g)
        m_hat = nm / (1.0 - ADAM_B1 ** ADAM_STEP)
        v_hat = nv / (1.0 - ADAM_B2 ** ADAM_STEP)
        g_ref[...] = g
        d_ref[...] = -ADAM_LR * (m_hat / (jnp.sqrt(v_hat) + ADAM_EPS) + ADAM_WD * w_ref[...])
        nm_ref[...] = nm
        nv_ref[...] = nv

    return _pallas(
        body, grid=(part_rows // tr, cols // tc), name=name,
        in_specs=[slab_spec(s) for _, s in parts] + [blk, blk, blk] + [ANY] * len(carried), out_specs=[blk] * 4,
        out_shape=[jax.ShapeDtypeStruct((rows, cols), F32)] * 4,
        input_output_aliases={nparts + 3 + k: k for k in range(len(carried))},
        compiler_params=_params(40),
    )(*[a for a, _ in parts], w, m, v, *carried)


def _pack_small(vals, extra=None):
    flat = [vals[n].reshape(-1).astype(F32) for n, _ in SMALL]
    tail = jnp.zeros((SMALL_ROWS * 128 - LOSS_AT,), F32)
    if extra is not None:
        tail = tail.at[0].set(extra)
    return jnp.concatenate(flat + [tail]).reshape(SMALL_ROWS, 128)


def _unpack_small(packed, like):
    flat, out, at = packed.reshape(-1), {}, 0
    for n, size in SMALL:
        out[n] = flat[at:at + size].reshape(like[n].shape)
        at += size
    return out, flat[LOSS_AT]


def kernel(x, ffn1_norm, ffn1_w_in, ffn1_w_out, mix_norm, w_in_mix, w_pool, pool_scale, w_alpha, b_alpha, gla_norm, w_out_mix, ffn2_norm, ffn2_w_in, ffn2_w_out, final_norm, loss_target, m_ffn1_norm, m_ffn1_w_in, m_ffn1_w_out, m_mix_norm, m_w_in_mix, m_w_pool, m_pool_scale, m_w_alpha, m_b_alpha, m_gla_norm, m_w_out_mix, m_ffn2_norm, m_ffn2_w_in, m_ffn2_w_out, m_final_norm, v_ffn1_norm, v_ffn1_w_in, v_ffn1_w_out, v_mix_norm, v_w_in_mix, v_w_pool, v_pool_scale, v_w_alpha, v_b_alpha, v_gla_norm, v_w_out_mix, v_ffn2_norm, v_ffn2_w_in, v_ffn2_w_out, v_final_norm):
    names = ["ffn1_norm", "ffn1_w_in", "ffn1_w_out", "mix_norm", "w_in_mix", "w_pool", "pool_scale", "w_alpha", "b_alpha",
             "gla_norm", "w_out_mix", "ffn2_norm", "ffn2_w_in", "ffn2_w_out", "final_norm"]
    p = dict(zip(names, [ffn1_norm, ffn1_w_in, ffn1_w_out, mix_norm, w_in_mix, w_pool, pool_scale, w_alpha, b_alpha,
                         gla_norm, w_out_mix, ffn2_norm, ffn2_w_in, ffn2_w_out, final_norm]))
    m = dict(zip(names, [m_ffn1_norm, m_ffn1_w_in, m_ffn1_w_out, m_mix_norm, m_w_in_mix, m_w_pool, m_pool_scale, m_w_alpha,
                         m_b_alpha, m_gla_norm, m_w_out_mix, m_ffn2_norm, m_ffn2_w_in, m_ffn2_w_out, m_final_norm]))
    v = dict(zip(names, [v_ffn1_norm, v_ffn1_w_in, v_ffn1_w_out, v_mix_norm, v_w_in_mix, v_w_pool, v_pool_scale, v_w_alpha,
                         v_b_alpha, v_gla_norm, v_w_out_mix, v_ffn2_norm, v_ffn2_w_in, v_ffn2_w_out, v_final_norm]))

    mx, my, mc = _place()
    table = jnp.stack([_dev_index(*_rel_chip(mx, my, rel), mc) for rel in range(NCHIP)]).astype(jnp.int32)

    def landing(shard, kind):
        shape = (shard.shape[0], NDEV * shard.shape[1]) if kind == "bcast_cols" else (NDEV,) + shard.shape
        return lax.empty(shape, shard.dtype)

    def gather_begin(items, tag, after):
        kinds = [kind for _, kind in items]
        copies = _gather_copies(kinds)
        s, r, thru, tok = _start_copies(tag + "_start", [a for a, _ in items], [landing(a, kind) for a, kind in items], copies,
                                        (NCHIP + 1) * len(items), after)
        return (s, r, thru, copies, kinds, [a.shape[-1] for a, _ in items], tag), tok

    def gather_end(state, after):
        s, r, thru, copies, kinds, widths, tag = state
        _, lands = _wait_copies(tag + "_wait", s, r, thru, len(kinds), copies, after)
        return _pass_to_sibling(lands, kinds, widths, tag + "_pass")

    def shard16(n):
        return p[n][0].astype(BF16)

    g_w1in, g_w1out = _gather([(shard16("ffn1_w_in"), "bcast_cols"), (shard16("ffn1_w_out"), "bcast")], "gather_ffn1")
    mix_state, tok_m = gather_begin([(jnp.transpose(p["w_in_mix"][0]).astype(BF16), "bcast"), (shard16("w_out_mix"), "bcast"),
                                     (p["w_pool"][0].reshape(H * 32, PG), "bcast"), (p["w_alpha"][0], "bcast")], "gather_mix",
                                    (g_w1out,))
    ffn2_items = [(shard16("ffn2_w_in"), "bcast_cols"), (shard16("ffn2_w_out"), "bcast")]
    ffn2_kinds = [kind for _, kind in ffn2_items]
    ffn2_widths = [a.shape[-1] for a, _ in ffn2_items]
    relay = {ph: _relay_copies(ffn2_kinds, ffn2_widths, ph) for ph in (1, 2)}
    f_s, f_r, f_thru, tok_f = _start_copies("gather_ffn2_start", [a for a, _ in ffn2_items],
                                            [landing(a, kind) for a, kind in ffn2_items], relay[1],
                                            RELAY_COPIES[1] * len(ffn2_items), (tok_m,))
    ffn2_lands = {}

    def relay_ffn2(point, value):
        _, lands = _wait_copies("gather_ffn2_wait", f_s, f_r, f_thru, len(ffn2_items), relay[1], (value,))
        ffn2_lands["s"], ffn2_lands["r"], ffn2_lands["thru"], tok = _start_copies(
            "gather_ffn2_relay_start", [], lands, relay[2], RELAY_COPIES[2] * len(ffn2_items))
        return (tok,)

    full ={"ffn1_w_in": g_w1in, "ffn1_w_out": g_w1out.reshape(FF, D), "final_norm": final_norm.reshape(1, D)}
    for n in ("ffn1_norm", "mix_norm", "ffn2_norm", "pool_scale", "b_alpha", "gla_norm"):
        full[n] = p[n]

    xs, tgt = x[0], loss_target[0]
    h1, n1, gu1 = _fwd_ffn1(xs, full, after=(tok_m, tok_f))
    g_wmix, g_wo, g_wpool, g_walpha = gather_end(mix_state, (h1,))
    walpha = jnp.transpose(g_walpha, (1, 0, 2)).reshape(RANK, H * DK)
    full.update({
        "w_in_mix": g_wmix.reshape(D_IN, D),
        "w_in_mix_tail": jnp.pad(g_wmix[NDEV - 1, MIX_SHARD - RANK:], ((0, RANK_PAD - RANK), (0, 0))),
        "w_out_mix": g_wo.reshape(D, D),
        "w_pool": jnp.transpose(g_wpool.reshape(NDEV, H, 32, PG), (1, 0, 2, 3)).reshape(H, PG, PG).astype(BF16),
        "w_alpha": jnp.pad(walpha, ((0, RANK_PAD - RANK), (0, 0))).astype(BF16),
    })
    h2, sv = _fwd_mixer(h1, full, at=relay_ffn2)
    _, lands = _wait_copies("gather_ffn2_relay_wait", ffn2_lands["s"], ffn2_lands["r"], ffn2_lands["thru"], 0, relay[2], (h2,))
    g_w2in, g_w2out = _relay_finish(lands, ffn2_kinds, ffn2_widths, "gather_ffn2_finish")
    full.update({"ffn2_w_in": g_w2in, "ffn2_w_out": g_w2out.reshape(FF, D)})
    dh3, d_final, loss_part, n3, gu3 = _fwd_ffn2_loss(h2, tgt, full)


    def slab_shape(a, kind):
        return (NCHIP,) + (a.shape[1:] if kind == "scatter" else (a.shape[0], a.shape[1] // NDEV))

    def pair_add_all(own, got, tag):
        return [_pair_add(a, kind, got_a, table, tr, "%s_pair_add_%d" % (tag, i))
                for i, ((a, kind, tr), got_a) in enumerate(zip(own, got))]

    def reduce_begin(own, tag, after=()):
        kinds = [kind for _, kind, _ in own]
        copies = _pair_copies(kinds)
        s, r, thru, tok = _start_copies(tag + "_pair_start", [a for a, _, _ in own],
                                        [lax.empty(slab_shape(a, kind), a.dtype) for a, kind, _ in own], copies,
                                        NCHIP * len(own), after)
        return dict(own=own, copies=copies, s=s, r=r, thru=thru, tag=tag), tok

    def reduce_middle(st, after):
        own, tag = st["own"], st["tag"]
        sent, got = _wait_copies(tag + "_pair_wait", st["s"], st["r"], st["thru"], len(own), st["copies"], after)
        pre = pair_add_all([(a, kind, tr) for a, (_, kind, tr) in zip(sent, own)], got, tag)
        st["s"], st["r"], st["thru"], tok = _start_copies(tag + "_chip_start", pre, [lax.empty(a.shape, a.dtype) for a in pre],
                                                          _chip_copies, (NCHIP - 1) * len(pre))
        return tok

    def reduce_end(st, after):
        n = len(st["own"])
        pre, land = _wait_copies(st["tag"] + "_chip_wait", st["s"], st["r"], st["thru"], n, _chip_copies, after)
        return [[(a, 0)] + [(b, rel) for rel in range(1, NCHIP)] for a, b in zip(pre, land)]

    def w_in_item(a):
        return (a, "scatter_cols", 1024)

    def w_out_item(a):
        return (a.reshape(NDEV, WOUT_SHARD, D), "scatter", WOUT_SHARD)

    red, small = {}, {}

    def at(point, value):
        if point == "ffn2_w_out":
            red["w2out"], tok = reduce_begin([w_out_item(value)], "ffn2_w_out")
        elif point == "ffn2_w_in":
            tok_a = reduce_middle(red["w2out"], (value,))
            red["w2in"], tok = reduce_begin([w_in_item(value)], "ffn2_w_in", (tok_a,))
        elif point == "pool_bwd":
            tok = reduce_middle(red["w2in"], (value,))
        elif point == "mixer_weights":
            d_wmix8 = value["w_in_mix"].reshape(NDEV, MIX_SHARD, D)
            d_wpool8 = jnp.transpose(value["w_pool"].reshape(H, NDEV, 32, PG), (1, 0, 2, 3)).reshape(NDEV, H * 32, PG)
            d_walpha8 = jnp.transpose(value["w_alpha"][:RANK].reshape(RANK, NDEV, H * DK // NDEV), (1, 0, 2))
            red["mix"], tok = reduce_begin([(d_wmix8, "scatter", MIX_SHARD),
                                            (value["w_out_mix"].reshape(NDEV, D // NDEV, D), "scatter", D // NDEV),
                                            (d_wpool8, "scatter", H * 32), (d_walpha8, "scatter", RANK)], "mix")
        elif point == "dh1":
            tok = reduce_middle(red["mix"], (value,))
        elif point == "ffn1_w_out":
            red["w1out"], tok = reduce_begin([w_out_item(value)], "ffn1_w_out")
        elif point == "ffn1_dx_a":
            tok = reduce_middle(red["w1out"], (value,))
        elif point == "ffn1_w_in_top":
            red["w1in_top"], tok = reduce_begin([w_in_item(value)], "ffn1_w_in_top")
        elif point == "ffn1_w_in_bottom":
            tok_a = reduce_middle(red["w1in_top"], (value,))
            red["w1in_bottom"], tok = reduce_begin([w_in_item(value)], "ffn1_w_in_bottom", (tok_a,))
        return (tok,)

    dx, g = _backward(xs, h1, h2, n1, gu1, sv, n3, gu3, dh3, full, at)
    packed = _pack_small(dict(final_norm=d_final, **g), loss_part[0, 0])
    small_s, small_r, small_thru, tok_s = _start_copies("gather_small_start", [packed], [lax.empty((NDEV,) + packed.shape, F32)],
                                                        _all_copies, NDEV, (dx,))
    tok_c = reduce_middle(red["w1in_bottom"], (tok_s,))

    def upd(parts, n, shape2d, tr):
        res = _adamw(parts, p[n].reshape(shape2d), m[n].reshape(shape2d), v[n].reshape(shape2d), tr, "adamw_" + n)
        return [r.reshape(p[n].shape) for r in res]

    def transposed(a):
        return jnp.transpose(a[0])

    (p_w2out,) = reduce_end(red["w2out"], (tok_c,))
    (p_w2in,) = reduce_end(red["w2in"], (tok_c,))
    p_wmix, p_wo, p_wpool, p_walpha = reduce_end(red["mix"], (tok_c,))
    out = {
        "ffn2_w_in": upd(p_w2in, "ffn2_w_in", (D, WIN_SHARD), 256),
        "ffn2_w_out": upd(p_w2out, "ffn2_w_out", (WOUT_SHARD, D), WOUT_SHARD // 4),
        "w_out_mix": upd(p_wo, "w_out_mix", (D // NDEV, D), 64),
        "w_pool": upd(p_wpool, "w_pool", (H * 32, PG), H * 32),
        "w_alpha": upd(p_walpha, "w_alpha", (RANK, H * DK // NDEV), RANK),
    }
    out["w_in_mix"] = [jnp.transpose(r)[None] for r in
                       _adamw(p_wmix, transposed(p["w_in_mix"]), transposed(m["w_in_mix"]), transposed(v["w_in_mix"]),
                              MIX_SHARD, "adamw_w_in_mix", tc=512)]
    _, (r_small,) = _wait_copies("gather_small_wait", small_s, small_r, small_thru, 1, _all_copies,
                                 (out["w_in_mix"][3], out["ffn2_w_in"][3], out["ffn2_w_out"][3], out["w_out_mix"][3]))
    small_res = _adamw([(r_small, s) for s in range(NDEV)], _pack_small(p), _pack_small(m), _pack_small(v), SMALL_ROWS,
                       "adamw_small")
    (p_w1out,) = reduce_end(red["w1out"], (small_res[0],))
    out["ffn1_w_out"] = upd(p_w1out, "ffn1_w_out", (WOUT_SHARD, D), WOUT_SHARD // 4)
    w1in = [a.reshape(D, WIN_SHARD) for a in (p["ffn1_w_in"], m["ffn1_w_in"], v["ffn1_w_in"])]
    (p_top,) = reduce_end(red["w1in_top"], (out["ffn1_w_out"][3],))
    top = _adamw(p_top, *w1in, 256, "adamw_ffn1_w_in_top")
    (p_bottom,) = reduce_end(red["w1in_bottom"], (top[3],))
    out["ffn1_w_in"] = [r.reshape(p["ffn1_w_in"].shape) for r in
                        _adamw(p_bottom, *w1in, 256, "adamw_ffn1_w_in_bottom", first_row=D // 2, into=top)]
    unpacked = [_unpack_small(r, p) for r in small_res]
    loss = unpacked[0][1]
    for n, _ in SMALL:
        out[n] = [u[0][n] for u in unpacked]

    return (loss, dx.reshape(1, S, D), *[out[n][0] for n in names], *[out[n][1] for n in names],
            *[out[n][2] for n in names], *[out[n][3] for n in names])
```

```python
import jax
import jax.numpy as jnp
from jax import lax
from jax.experimental import pallas as pl
from jax.experimental.pallas import tpu as pltpu

F32, BF16 = jnp.float32, jnp.bfloat16
MESH = pl.DeviceIdType.MESH
ANY = pl.BlockSpec(memory_space=pl.ANY)

NDEV = 8
S = 2048
D = 2048
FF = 5632
WIN_SHARD = 2 * FF // NDEV
WOUT_SHARD = FF // NDEV
D_POOL = 1024
PG = 256
POOL_WINDOWS = (2, 4, 8, 16)
H = 4
DK = 128
DV = 256
CH = 64
NCH = S // CH
RANK = 16
RANK_PAD = 128
D_IN = 4112
D_IN_PAD = 4224
MIX_SHARD = D_IN // NDEV
O_Q, O_K, O_V, O_G, O_R = 1024, 1536, 2048, 3072, 4096
GATE_NORM = 16.0
QK_SCALE = DK ** -0.5
EPS = 1e-6
ADAM_LR, ADAM_B1, ADAM_B2, ADAM_EPS, ADAM_WD, ADAM_STEP = 0.001, 0.9, 0.999, 1e-08, 0.01, 10
V7X_VMEM_BYTES = 64 << 20

SMALL = (("ffn1_norm", 2048), ("mix_norm", 2048), ("ffn2_norm", 2048), ("final_norm", 2048),
         ("pool_scale", 1024), ("b_alpha", 512), ("gla_norm", 256))
SMALL_ROWS = 80
LOSS_AT = sum(n for _, n in SMALL)


def _params(vmem_mb, sem=None):
    return pltpu.CompilerParams(dimension_semantics=sem, vmem_limit_bytes=min(vmem_mb << 20, V7X_VMEM_BYTES - (4 << 20)))


def _pallas(body, **kwargs):
    call = pl.pallas_call(body, **kwargs)

    def run(*operands):
        return call(*[pltpu.with_memory_space_constraint(a, pltpu.HBM) if a.size * a.dtype.itemsize >= 1 << 18 else a
                      for a in operands])
    return run


def _dot(a, b):
    return jnp.dot(a, b, preferred_element_type=F32)


def _dot_nt(a, b):
    return lax.dot_general(a, b, (((1,), (1,)), ((), ())), preferred_element_type=F32)


def _dot_tn(a, b):
    return lax.dot_general(a, b, (((0,), (0,)), ((), ())), preferred_element_type=F32)


def _sigmoid(x):
    return 0.5 * jnp.tanh(0.5 * x) + 0.5


def _log_sigmoid(x):
    return jnp.minimum(x, 0.0) - jnp.log(1.0 + jnp.exp(-jnp.abs(x)))


def _rms(x, g):
    r = lax.rsqrt(jnp.mean(x * x, axis=-1, keepdims=True) + EPS)
    return x * r * g


def _rms_bwd(dn, x, g):
    r = lax.rsqrt(jnp.mean(x * x, axis=-1, keepdims=True) + EPS)
    xh = x * r
    dxh = dn * g
    dx = r * (dxh - xh * jnp.mean(dxh * xh, axis=-1, keepdims=True))
    return dx, jnp.sum(dn * xh, axis=0, keepdims=True)


ROWS = 64


def _row_loop(total, fn, init=0):
    def step(t, carry):
        return fn(pl.ds(pl.multiple_of(t * ROWS, ROWS), ROWS), carry)
    return lax.fori_loop(0, total // ROWS, step, init)


def _split3(x):
    hi = x.astype(BF16)
    r1 = x - hi.astype(F32)
    mid = r1.astype(BF16)
    lo = (r1 - mid.astype(F32)).astype(BF16)
    return hi, mid, lo


def _tri_dot(tri_b, x):
    hi, mid, lo = _split3(x)
    return (_dot(tri_b, lo) + _dot(tri_b, mid)) + _dot(tri_b, hi)


FFN_TS, FFN_TF = 512, 512


def _ffn_specs():
    wg = pl.BlockSpec((D, FFN_TF), lambda i, j: (0, j))
    wu = pl.BlockSpec((D, FFN_TF), lambda i, j: (0, FF // FFN_TF + j))
    wo = pl.BlockSpec((FFN_TF, D), lambda i, j: (j, 0))
    row = pl.BlockSpec((FFN_TS, D), lambda i, j: (i, 0))
    vec = pl.BlockSpec((1, D), lambda i, j: (0, 0))
    gu = pl.BlockSpec((2, FFN_TS, FFN_TF), lambda i, j: (0, i, j))
    return wg, wu, wo, row, vec, gu


def _ordered_after(body, n_in, after):
    def wrapped(*refs):
        return body(*refs[:n_in], *refs[n_in + len(after):])
    return wrapped, [ANY] * len(after)


def _ffn_fwd(h, g, w_in8, w_out, name, after=(), head=None):
    nj = FF // FFN_TF
    wg, wu, wo, row, vec, gu = _ffn_specs()
    n_in = 5 if head is None else 7

    def body(*refs):
        h_ref, g_ref, wg_ref, wu_ref, wo_ref = refs[:5]
        ho_ref, n_ref, gu_ref = refs[n_in:n_in + 3]
        acc_ref = refs[-1]
        i, j = pl.program_id(0), pl.program_id(1)

        @pl.when(j == 0)
        def _():
            def norm(rows, c):
                n_ref[rows, :] = _rms(h_ref[rows, :], g_ref[...]).astype(BF16)
                return c
            _row_loop(FFN_TS, norm)
            acc_ref[...] = jnp.zeros_like(acc_ref)

        n = n_ref[...]
        gate = _dot(n, wg_ref[...])
        up = _dot(n, wu_ref[...])
        gu_ref[0] = gate.astype(BF16)
        gu_ref[1] = up.astype(BF16)
        a = (gate * _sigmoid(gate)) * up
        acc_ref[...] += _dot(a.astype(BF16), wo_ref[...])

        if head is None:
            @pl.when(j == nj - 1)
            def _():
                def residual(rows, c):
                    ho_ref[rows, :] = h_ref[rows, :] + 0.5 * acc_ref[rows, :]
                    return c
                _row_loop(FFN_TS, residual)
        else:
            gf_ref, t_ref = refs[5:7]
            dgf_ref, loss_ref = refs[n_in + 3:n_in + 5]

            @pl.when(j == nj - 1)
            def _():
                def rows_fn(rows, carry):
                    dg, part = carry
                    x = h_ref[rows, :] + 0.5 * acc_ref[rows, :]
                    gv = gf_ref[...]
                    err = _rms(x, gv) - t_ref[rows, :]
                    part = part + 0.5 * jnp.sum(jnp.mean(err * err, axis=-1, keepdims=True), axis=0, keepdims=True)
                    dx, dg_rows = _rms_bwd(err * (1.0 / D), x, gv)
                    ho_ref[rows, :] = dx
                    return dg + dg_rows, part
                dg, part = _row_loop(FFN_TS, rows_fn, (jnp.zeros((1, D), F32), jnp.zeros((1, 1), F32)))

                @pl.when(i == 0)
                def _():
                    dgf_ref[...] = dg
                    loss_ref[...] = jnp.broadcast_to(part, loss_ref.shape)

                @pl.when(i > 0)
                def _():
                    dgf_ref[...] += dg
                    loss_ref[...] += jnp.broadcast_to(part, loss_ref.shape)

    body, extra = _ordered_after(body, n_in, after)
    head_in = [] if head is None else [vec, row]
    head_out = [] if head is None else [vec, pl.BlockSpec((1, 128), lambda i, j: (0, 0))]
    head_shape = [] if head is None else [jax.ShapeDtypeStruct((1, D), F32), jax.ShapeDtypeStruct((1, 128), F32)]
    return _pallas(
        body, grid=(S // FFN_TS, nj), name=name,
        in_specs=[row, vec, wg, wu, wo] + head_in + extra, out_specs=[row, row, gu] + head_out,
        out_shape=[jax.ShapeDtypeStruct((S, D), F32), jax.ShapeDtypeStruct((S, D), BF16),
                   jax.ShapeDtypeStruct((2, S, FF), BF16)] + head_shape,
        scratch_shapes=[pltpu.VMEM((FFN_TS, D), F32)],
        compiler_params=_params(58, ("arbitrary", "arbitrary")),
    )(h, g, w_in8, w_in8, w_out, *(head or ()), *after)


def _ffn_bwd_x(dhp, h, g, gu_arr, w_in8, w_out, name, after=()):
    ni, nj = S // FFN_TS, FF // FFN_TF
    wg, wu, wo, row, vec, gu = _ffn_specs()
    act = pl.BlockSpec((FFN_TS, FFN_TF), lambda i, j: (i, j))

    def body(dhp_ref, h_ref, g_ref, gu_ref, wg_ref, wu_ref, wo_ref,
             dgu_ref, a_ref, df_ref, dh_ref, dhb_ref, dg_ref, acc_ref):
        i, j = pl.program_id(0), pl.program_id(1)

        @pl.when(j == 0)
        def _():
            def half(rows, c):
                df_ref[rows, :] = (0.5 * dhp_ref[rows, :]).astype(BF16)
                return c
            _row_loop(FFN_TS, half)
            acc_ref[...] = jnp.zeros_like(acc_ref)

        gate = gu_ref[0].astype(F32)
        up = gu_ref[1].astype(F32)
        da = _dot_nt(df_ref[...], wo_ref[...])
        sg = _sigmoid(gate)
        silu = gate * sg
        dgate = (da * up * (sg * (1.0 + gate * (1.0 - sg)))).astype(BF16)
        dup = (da * silu).astype(BF16)
        a_ref[...] = (silu * up).astype(BF16)
        dgu_ref[0] = dgate
        dgu_ref[1] = dup
        acc_ref[...] += _dot_nt(dgate, wg_ref[...]) + _dot_nt(dup, wu_ref[...])

        @pl.when(j == nj - 1)
        def _():
            def norm_bwd(rows, dg):
                dx, dg_rows = _rms_bwd(acc_ref[rows, :], h_ref[rows, :], g_ref[...])
                dh = dhp_ref[rows, :] + dx
                dh_ref[rows, :] = dh
                dhb_ref[rows, :] = dh.astype(BF16)
                return dg + dg_rows
            dg = _row_loop(FFN_TS, norm_bwd, jnp.zeros((1, D), F32))

            @pl.when(i == 0)
            def _():
                dg_ref[...] = dg

            @pl.when(i > 0)
            def _():
                dg_ref[...] += dg

    body, extra = _ordered_after(body, 7, after)
    return _pallas(
        body, grid=(ni, nj), name=name,
        in_specs=[row, row, vec, gu, wg, wu, wo] + extra,
        out_specs=[gu, act, row, row, row, vec],
        out_shape=[jax.ShapeDtypeStruct((2, S, FF), BF16), jax.ShapeDtypeStruct((S, FF), BF16),
                   jax.ShapeDtypeStruct((S, D), BF16), jax.ShapeDtypeStruct((S, D), F32),
                   jax.ShapeDtypeStruct((S, D), BF16), jax.ShapeDtypeStruct((1, D), F32)],
        scratch_shapes=[pltpu.VMEM((FFN_TS, D), F32)],
        compiler_params=_params(58, ("arbitrary", "arbitrary")),
    )(dhp, h, g, gu_arr, w_in8, w_in8, w_out, *after)


def _ffn_bwd_gu(df, gu_arr, w_out, name, after=()):
    ni = S // FFN_TS
    gu = pl.BlockSpec((2, FFN_TS, FFN_TF), lambda j, i: (0, i, j))
    wo = pl.BlockSpec((FFN_TF, D), lambda j, i: (j, 0))
    df_all = pl.BlockSpec((S, D), lambda j, i: (0, 0), pipeline_mode=pl.Buffered(1))

    def body(df_ref, gu_ref, wo_ref, dgu_ref, dwo_ref, acc_ref):
        i = pl.program_id(1)
        df_rows = df_ref[pl.ds(pl.multiple_of(i * FFN_TS, FFN_TS), FFN_TS), :]
        gate = gu_ref[0].astype(F32)
        up = gu_ref[1].astype(F32)
        da = _dot_nt(df_rows, wo_ref[...])
        sg = _sigmoid(gate)
        silu = gate * sg
        dgu_ref[0] = (da * up * (sg * (1.0 + gate * (1.0 - sg)))).astype(BF16)
        dgu_ref[1] = (da * silu).astype(BF16)
        part = _dot_tn((silu * up).astype(BF16), df_rows)

        @pl.when(i == 0)
        def _():
            acc_ref[...] = part

        @pl.when(i > 0)
        def _():
            acc_ref[...] += part

        @pl.when(i == ni - 1)
        def _():
            dwo_ref[...] = acc_ref[...].astype(BF16)

    body, extra = _ordered_after(body, 3, after)
    return _pallas(
        body, grid=(FF // FFN_TF, ni), name=name,
        in_specs=[df_all, gu, wo] + extra, out_specs=[gu, wo],
        out_shape=[jax.ShapeDtypeStruct((2, S, FF), BF16), jax.ShapeDtypeStruct((FF, D), BF16)],
        scratch_shapes=[pltpu.VMEM((FFN_TF, D), F32)],
        compiler_params=_params(40, ("arbitrary", "arbitrary")),
    )(df, gu_arr, w_out, *after)


def _ffn_bwd_dx(dgu, dhp, h, g, w_in, first_tile, ntiles, name, after=(), into=None):
    nj = FF // FFN_TF
    wg, wu, _, _, vec, _ = _ffn_specs()
    row_in = pl.BlockSpec((FFN_TS, D), lambda i, j: (first_tile + i, 0))
    dgu_spec = pl.BlockSpec((2, FFN_TS, FFN_TF), lambda i, j: (0, first_tile + i, j))
    after = tuple(after) + (() if into is None else (into,))

    def body(dgu_ref, dhp_ref, h_ref, g_ref, wg_ref, wu_ref, dh_ref, dg_ref, acc_ref):
        i, j = pl.program_id(0), pl.program_id(1)

        @pl.when(j == 0)
        def _():
            acc_ref[...] = jnp.zeros_like(acc_ref)

        acc_ref[...] += _dot_nt(dgu_ref[0], wg_ref[...]) + _dot_nt(dgu_ref[1], wu_ref[...])

        @pl.when(j == nj - 1)
        def _():
            def norm_bwd(rows, dg):
                dx, dg_rows = _rms_bwd(acc_ref[rows, :], h_ref[rows, :], g_ref[...])
                dh_ref[rows, :] = dhp_ref[rows, :] + dx
                return dg + dg_rows
            dg = _row_loop(FFN_TS, norm_bwd, jnp.zeros((1, D), F32))

            @pl.when(i == 0)
            def _():
                dg_ref[...] = dg

            @pl.when(i > 0)
            def _():
                dg_ref[...] += dg

    body, extra = _ordered_after(body, 6, after)
    return _pallas(
        body, grid=(ntiles, nj), name=name,
        in_specs=[dgu_spec, row_in, row_in, vec, wg, wu] + extra, out_specs=[row_in, vec],
        out_shape=[jax.ShapeDtypeStruct((S, D), F32), jax.ShapeDtypeStruct((1, D), F32)],
        input_output_aliases={} if into is None else {6 + len(after) - 1: 0},
        scratch_shapes=[pltpu.VMEM((FFN_TS, D), F32)],
        compiler_params=_params(48, ("arbitrary", "arbitrary")),
    )(dgu, dhp, h, g, w_in, w_in, *after)


def _tn_matmul(a, b, a_spec, b_spec, out_shape, out_spec, grid, name, vmem_mb, after=()):
    def body(a_ref, b_ref, o_ref):
        o_ref[...] = _dot_tn(a_ref[...], b_ref[...]).astype(o_ref.dtype)

    body, extra = _ordered_after(body, 2, after)
    return _pallas(body, grid=grid, name=name, in_specs=[a_spec, b_spec] + extra, out_specs=out_spec,
                          out_shape=out_shape, compiler_params=_params(vmem_mb))(a, b, *after)


def _resident(shape):
    return pl.BlockSpec(shape, lambda i: (0,) * len(shape), pipeline_mode=pl.Buffered(1))


def _norm_matmul(h, g, w, w_tail, name, after=(), ts=256):
    main = w.shape[0] // 128 * 128
    n_out = main + w_tail.shape[0]

    def body(h_ref, g_ref, w_ref, wt_ref, u_ref, n_ref):
        def norm(rows, c):
            n_ref[rows, :] = _rms(h_ref[rows, :], g_ref[...]).astype(BF16)
            return c
        _row_loop(ts, norm)
        n = n_ref[...]
        u_ref[:, :main] = _dot_nt(n, w_ref[:main, :])
        u_ref[:, main:] = _dot_nt(n, wt_ref[...])

    body, extra = _ordered_after(body, 4, after)
    return _pallas(
        body, grid=(S // ts,), name=name,
        in_specs=[pl.BlockSpec((ts, D), lambda i: (i, 0)), pl.BlockSpec((1, D), lambda i: (0, 0)), _resident(w.shape),
                  _resident(w_tail.shape)] + extra,
        out_specs=[pl.BlockSpec((ts, n_out), lambda i: (i, 0)), pl.BlockSpec((ts, D), lambda i: (i, 0))],
        out_shape=[jax.ShapeDtypeStruct((S, n_out), F32), jax.ShapeDtypeStruct((S, D), BF16)],
        compiler_params=_params(48, ("arbitrary",)),
    )(h, g, w, w_tail, *after)


def _matmul_residual(a, w, res, name, after=(), ts=512):
    k, n_out = w.shape

    def body(a_ref, w_ref, r_ref, o_ref):
        o_ref[...] = r_ref[...] + _dot(a_ref[...], w_ref[...])

    body, extra = _ordered_after(body, 3, after)
    return _pallas(
        body, grid=(S // ts,), name=name,
        in_specs=[pl.BlockSpec((ts, k), lambda i: (i, 0)), _resident(w.shape), pl.BlockSpec((ts, n_out), lambda i: (i, 0))] + extra,
        out_specs=pl.BlockSpec((ts, n_out), lambda i: (i, 0)),
        out_shape=jax.ShapeDtypeStruct((S, n_out), F32),
        compiler_params=_params(40),
    )(a, w, res, *after)


def _nt_matmul(a, w, name, after=(), ts=512):
    n_out, k = w.shape

    def body(a_ref, w_ref, o_ref):
        o_ref[...] = _dot_nt(a_ref[...], w_ref[...])

    body, extra = _ordered_after(body, 2, after)
    return _pallas(
        body, grid=(S // ts,), name=name,
        in_specs=[pl.BlockSpec((ts, k), lambda i: (i, 0)), _resident(w.shape)] + extra,
        out_specs=pl.BlockSpec((ts, n_out), lambda i: (i, 0)),
        out_shape=jax.ShapeDtypeStruct((S, n_out), F32),
        compiler_params=_params(40),
    )(a, w, *after)


def _matmul_normbwd(du, w, w_tail, h, g, dres, name, after=(), ts=256):
    main = w.shape[0] // 128 * 128
    n_in = main + w_tail.shape[0]
    row = pl.BlockSpec((ts, D), lambda i: (i, 0))
    vec = pl.BlockSpec((1, D), lambda i: (0, 0))

    def body(du_ref, w_ref, wt_ref, h_ref, g_ref, dres_ref, dh_ref, dg_ref, half_ref, acc_ref):
        i = pl.program_id(0)
        acc_ref[...] = _dot(du_ref[:, :main], w_ref[:main, :]) + _dot(du_ref[:, main:], wt_ref[...])

        def norm_bwd(rows, dg):
            dx, dg_rows = _rms_bwd(acc_ref[rows, :], h_ref[rows, :], g_ref[...])
            dh = dres_ref[rows, :] + dx
            dh_ref[rows, :] = dh
            half_ref[rows, :] = (0.5 * dh).astype(BF16)
            return dg + dg_rows
        dg = _row_loop(ts, norm_bwd, jnp.zeros((1, D), F32))

        @pl.when(i == 0)
        def _():
            dg_ref[...] = dg

        @pl.when(i > 0)
        def _():
            dg_ref[...] += dg

    body, extra = _ordered_after(body, 6, after)
    return _pallas(
        body, grid=(S // ts,), name=name,
        in_specs=[pl.BlockSpec((ts, n_in), lambda i: (i, 0)), _resident(w.shape), _resident(w_tail.shape), row, vec, row] + extra,
        out_specs=[row, vec, row],
        out_shape=[jax.ShapeDtypeStruct((S, D), F32), jax.ShapeDtypeStruct((1, D), F32), jax.ShapeDtypeStruct((S, D), BF16)],
        scratch_shapes=[pltpu.VMEM((ts, D), F32)],
        compiler_params=_params(52, ("arbitrary",)),
    )(du, w, w_tail, h, g, dres, *after)


def _pool_specs():
    blk = pl.BlockSpec((S, PG), lambda gi: (0, gi))
    wp = pl.BlockSpec((None, PG, PG), lambda gi: (gi, 0, 0))
    sc = pl.BlockSpec((1, PG), lambda gi: (0, gi))
    return blk, wp, sc


def _pool_fwd(u, wp_b, scale):
    blk, wp, sc = _pool_specs()

    def body(u_ref, wp_ref, sc_ref, y_ref, pooled_ref):
        win = 2 << pl.program_id(0)
        row = lax.broadcasted_iota(jnp.int32, (S, PG), 0)
        x = u_ref[...]
        s = x
        for k in (1, 2, 4, 8):
            s = s + jnp.where((row >= k) & (k < win), pltpu.roll(s, k, 0), 0.0)
        cnt = jnp.minimum(row + 1, win).astype(F32)
        pooled = (s / cnt - x).astype(BF16)
        pooled_ref[...] = pooled
        y_ref[...] = (_dot(pooled, wp_ref[...]) * sc_ref[...]).astype(BF16)

    return _pallas(
        body, grid=(len(POOL_WINDOWS),), name="pool_fwd", in_specs=[blk, wp, sc], out_specs=[blk, blk],
        out_shape=[jax.ShapeDtypeStruct((S, D_POOL), BF16), jax.ShapeDtypeStruct((S, D_POOL), BF16)],
        compiler_params=_params(40),
    )(u, wp_b, scale)


def _pool_bwd(dy, pooled, wp_b, scale):
    blk, wp, sc = _pool_specs()

    def body(dy_ref, p_ref, wp_ref, sc_ref, du_ref, dwp_ref, dsc_ref):
        win = 2 << pl.program_id(0)
        row = lax.broadcasted_iota(jnp.int32, (S, PG), 0)
        dyv = dy_ref[...]
        pooled = p_ref[...]
        w = wp_ref[...]
        dsc_ref[...] = jnp.sum(dyv * _dot(pooled, w), axis=0, keepdims=True)
        dz = (dyv * sc_ref[...]).astype(BF16)
        dwp_ref[...] = _dot_tn(pooled, dz)
        dpooled = _dot_nt(dz, w)
        cnt = jnp.minimum(row + 1, win).astype(F32)
        fs = dpooled / cnt
        for k in (1, 2, 4, 8):
            fs = fs + jnp.where((row < S - k) & (k < win), pltpu.roll(fs, S - k, 0), 0.0)
        du_ref[...] = (fs - dpooled).astype(BF16)

    return _pallas(
        body, grid=(len(POOL_WINDOWS),), name="pool_bwd", in_specs=[blk, blk, wp, sc], out_specs=[blk, wp, sc],
        out_shape=[jax.ShapeDtypeStruct((S, D_POOL), BF16), jax.ShapeDtypeStruct((len(POOL_WINDOWS), PG, PG), F32),
                   jax.ShapeDtypeStruct((1, D_POOL), F32)],
        compiler_params=_params(40),
    )(dy, pooled, wp_b, scale)


GLA_CPS = 2
GLA_ROWS = GLA_CPS * CH
GLA_STEPS = NCH // GLA_CPS


def _gla_in_specs(step_of):
    def at(width, col):
        return pl.BlockSpec((GLA_ROWS, width), lambda n: (step_of(n), col))
    return [at(H * DK, O_Q // (H * DK)), at(H * DK, O_K // (H * DK)), at(H * DV, O_V // (H * DV)),
            at(H * DV, O_G // (H * DV)), at(RANK_PAD, O_R // RANK_PAD)]


def _gla_decay_terms(lr, wa_ref, ba_ref, q, k):
    row = lax.broadcasted_iota(jnp.int32, (CH, CH), 0)
    col = lax.broadcasted_iota(jnp.int32, (CH, CH), 1)
    tril = row >= col
    z = _dot(lr.astype(BF16), wa_ref[...]) + ba_ref[...]
    la = _log_sigmoid(z) / GATE_NORM
    b = _tri_dot(jnp.where(tril, 1.0, 0.0).astype(BF16), la)
    bl = jnp.sum(la, axis=0, keepdims=True)
    e_b, e_nb, e_tb = jnp.exp(b), jnp.exp(-b), jnp.exp(bl - b)
    q_dec = (q * QK_SCALE) * e_b
    return tril, z, e_b, e_nb, e_tb, jnp.exp(bl), q_dec, k * e_nb, k * e_tb


def _gla_fwd(u, wa_b, ba, gn):
    wide = pl.BlockSpec((GLA_ROWS, H * DV), lambda n: (n, 0))

    def body(q_ref, k_ref, v_ref, g_ref, lr_ref, wa_ref, ba_ref, gn_ref, y_ref, o_ref, st_ref, state):
        @pl.when(pl.program_id(0) == 0)
        def _():
            state[...] = jnp.zeros_like(state)

        for c in range(GLA_CPS):
            rows = slice(c * CH, (c + 1) * CH)
            tril, _, _, _, _, dec, q_dec, k_inv, k_tail = _gla_decay_terms(lr_ref[rows, :], wa_ref, ba_ref, q_ref[rows, :],
                                                                          k_ref[rows, :])
            for hd in range(H):
                ks, vs = slice(hd * DK, (hd + 1) * DK), slice(hd * DV, (hd + 1) * DV)
                qb, kib, ktb = q_dec[:, ks].astype(BF16), k_inv[:, ks].astype(BF16), k_tail[:, ks].astype(BF16)
                vb = v_ref[rows, vs].astype(BF16)
                p = jnp.where(tril, _dot_nt(qb, kib), 0.0)
                st = state[hd]
                o = _dot(p.astype(BF16), vb) + _dot_nt(qb, st.astype(BF16))
                st_ref[c, hd] = st
                state[hd] = st * dec[:, ks] + _dot_tn(vb, ktb)
                o_ref[rows, vs] = o
                on = _rms(o, gn_ref[...])
                gg = g_ref[rows, vs]
                y_ref[rows, vs] = (on * (gg * _sigmoid(gg))).astype(BF16)

    return _pallas(
        body, grid=(GLA_STEPS,), name="gla_fwd",
        in_specs=_gla_in_specs(lambda n: n) + [pl.BlockSpec((RANK_PAD, H * DK), lambda n: (0, 0)),
                                               pl.BlockSpec((1, H * DK), lambda n: (0, 0)),
                                               pl.BlockSpec((1, DV), lambda n: (0, 0))],
        out_specs=[wide, wide, pl.BlockSpec((GLA_CPS, H, DV, DK), lambda n: (n, 0, 0, 0))],
        out_shape=[jax.ShapeDtypeStruct((S, H * DV), BF16), jax.ShapeDtypeStruct((S, H * DV), F32),
                   jax.ShapeDtypeStruct((NCH, H, DV, DK), F32)],
        scratch_shapes=[pltpu.VMEM((H, DV, DK), F32)],
        compiler_params=_params(32, ("arbitrary",)),
    )(u, u, u, u, u, wa_b, ba, gn)


GLA_DU = 2 * H * DK + 2 * H * DV + RANK_PAD


def _gla_bwd(u, o_arr, states, dy, wa_b, ba, gn, after=()):
    rev = lambda n: GLA_STEPS - 1 - n
    wide = pl.BlockSpec((GLA_ROWS, H * DV), lambda n: (rev(n), 0))

    def body(q_ref, k_ref, v_ref, g_ref, lr_ref, o_ref, st_ref, dy_ref, wa_ref, ba_ref, gn_ref,
             du_ref, dwa_ref, dba_ref, dgn_ref, gstate, db_scr, dbl_scr):
        @pl.when(pl.program_id(0) == 0)
        def _():
            gstate[...] = jnp.zeros_like(gstate)
            dwa_ref[...] = jnp.zeros_like(dwa_ref)
            dba_ref[...] = jnp.zeros_like(dba_ref)
            dgn_ref[...] = jnp.zeros_like(dgn_ref)

        gnv = gn_ref[...]
        dgn = jnp.zeros((1, DV), F32)
        dwa = jnp.zeros((RANK_PAD, H * DK), F32)
        dba = jnp.zeros((1, H * DK), F32)
        srow = lax.broadcasted_iota(jnp.int32, (CH, CH), 0)
        scol = lax.broadcasted_iota(jnp.int32, (CH, CH), 1)
        for c in reversed(range(GLA_CPS)):
            rows = slice(c * CH, (c + 1) * CH)
            lr = lr_ref[rows, :]
            tril, z, e_b, e_nb, e_tb, dec, q_dec, k_inv, k_tail = _gla_decay_terms(lr, wa_ref, ba_ref, q_ref[rows, :],
                                                                                  k_ref[rows, :])
            for hd in range(H):
                ks, vs = slice(hd * DK, (hd + 1) * DK), slice(hd * DV, (hd + 1) * DV)
                qh, kih, kth = q_dec[:, ks], k_inv[:, ks], k_tail[:, ks]
                qb, kib, ktb = qh.astype(BF16), kih.astype(BF16), kth.astype(BF16)
                vb = v_ref[rows, vs].astype(BF16)
                o = o_ref[rows, vs]
                gg = g_ref[rows, vs]
                dyh = dy_ref[rows, vs]
                r = lax.rsqrt(jnp.mean(o * o, axis=-1, keepdims=True) + EPS)
                xh = o * r
                sg = _sigmoid(gg)
                dgate = dyh * (xh * gnv) * (sg * (1.0 + gg * (1.0 - sg)))
                don = dyh * (gg * sg)
                dgn = dgn + jnp.sum(don * xh, axis=0, keepdims=True)
                dxh = don * gnv
                d_o = (r * (dxh - xh * jnp.mean(dxh * xh, axis=-1, keepdims=True))).astype(BF16)
                pb = jnp.where(tril, _dot_nt(qb, kib), 0.0).astype(BF16)
                dpb = jnp.where(tril, _dot_nt(d_o, vb), 0.0).astype(BF16)
                gt = gstate[hd]
                gtb = gt.astype(BF16)
                st = st_ref[c, hd]
                dv = _dot_tn(pb, d_o) + _dot_nt(ktb, gtb)
                dq_dec = _dot(dpb, kib) + _dot(d_o, st.astype(BF16))
                dk_inv = _dot_tn(dpb, qb)
                dk_tail = _dot(vb, gtb)
                ddec = jnp.sum(gt * st, axis=0, keepdims=True)
                gstate[hd] = _dot_tn(d_o, qb) + dec[:, ks] * gt
                du_ref[rows, ks] = (dq_dec * QK_SCALE * e_b[:, ks]).astype(BF16)
                du_ref[rows, H * DK + hd * DK:H * DK + (hd + 1) * DK] = (dk_inv * e_nb[:, ks] + dk_tail * e_tb[:, ks]).astype(BF16)
                du_ref[rows, 2 * H * DK + hd * DV:2 * H * DK + (hd + 1) * DV] = dv.astype(BF16)
                du_ref[rows, 2 * H * DK + H * DV + hd * DV:2 * H * DK + H * DV + (hd + 1) * DV] = dgate.astype(BF16)
                db_scr[c, :, ks] = dq_dec * qh - dk_inv * kih - dk_tail * kth
                dbl_scr[c, :, ks] = jnp.sum(dk_tail * kth, axis=0, keepdims=True) + ddec * dec[:, ks]
            dla = _tri_dot(jnp.where(srow <= scol, 1.0, 0.0).astype(BF16), db_scr[c]) + dbl_scr[c]
            dz = dla * (1.0 / GATE_NORM) * _sigmoid(-z)
            dzb = dz.astype(BF16)
            du_ref[rows, GLA_DU - RANK_PAD:] = _dot_nt(dzb, wa_ref[...]).astype(BF16)
            dwa = dwa + _dot_tn(lr.astype(BF16), dzb)
            dba = dba + jnp.sum(dz, axis=0, keepdims=True)
        dgn_ref[...] += dgn
        dwa_ref[...] += dwa
        dba_ref[...] += dba

    full = lambda shape: pl.BlockSpec(shape, lambda n: (0,) * len(shape))
    body, extra = _ordered_after(body, 11, after)
    return _pallas(
        body, grid=(GLA_STEPS,), name="gla_bwd",
        in_specs=_gla_in_specs(rev) + [wide, pl.BlockSpec((GLA_CPS, H, DV, DK), lambda n: (rev(n), 0, 0, 0)),
                                       pl.BlockSpec((GLA_ROWS, H * DV), lambda n: (rev(n), 1)),
                                       full((RANK_PAD, H * DK)), full((1, H * DK)), full((1, DV))] + extra,
        out_specs=[pl.BlockSpec((GLA_ROWS, GLA_DU), lambda n: (rev(n), 0)), full((RANK_PAD, H * DK)), full((1, H * DK)),
                   full((1, DV))],
        out_shape=[jax.ShapeDtypeStruct((S, GLA_DU), BF16), jax.ShapeDtypeStruct((RANK_PAD, H * DK), F32),
                   jax.ShapeDtypeStruct((1, H * DK), F32), jax.ShapeDtypeStruct((1, DV), F32)],
        scratch_shapes=[pltpu.VMEM((H, DV, DK), F32), pltpu.VMEM((GLA_CPS, CH, H * DK), F32),
                        pltpu.VMEM((GLA_CPS, 1, H * DK), F32)],
        compiler_params=_params(32, ("arbitrary",)),
    )(u, u, u, u, u, o_arr, states, dy, wa_b, ba, gn, *after)


def _ffn_dw_in(n, dgu, tag, after=(), first_tile=0, ntiles=D // 512):
    return _tn_matmul(n, dgu, pl.BlockSpec((S, 512), lambda s, m: (0, first_tile + m)),
                      pl.BlockSpec((None, S, WIN_SHARD), lambda s, m: (s // (NDEV // 2), 0, s % (NDEV // 2))),
                      jax.ShapeDtypeStruct((512 * ntiles, 2 * FF), BF16), pl.BlockSpec((512, WIN_SHARD), lambda s, m: (m, s)),
                      (NDEV, ntiles), tag + "_dw_in", 32, after)


def _ffn_dw_out(act, df, tag, after=()):
    return _tn_matmul(act, df, pl.BlockSpec((S, 512), lambda m: (0, m)), pl.BlockSpec((S, D), lambda m: (0, 0)),
                      jax.ShapeDtypeStruct((FF, D), BF16), pl.BlockSpec((512, D), lambda m: (m, 0)),
                      (FF // 512,), tag + "_dw_out", 40, after)


def _fwd_ffn1(x, w, after=()):
    return _ffn_fwd(x, w["ffn1_norm"], w["ffn1_w_in"], w["ffn1_w_out"], "ffn1_fwd", after)


def _fwd_mixer(h1, w, after=(), at=lambda point, value: ()):
    u, n2 = _norm_matmul(h1, w["mix_norm"], w["w_in_mix"], w["w_in_mix_tail"], "mix_in", after)
    y_pool, pooled = _pool_fwd(u, w["w_pool"], w["pool_scale"])
    y_gla, o_gla, states = _gla_fwd(u, w["w_alpha"], w["b_alpha"], w["gla_norm"])
    y = jnp.concatenate([y_pool, y_gla], axis=1)
    h2 = _matmul_residual(y, w["w_out_mix"], h1, "mix_out", at("gla_fwd", y))
    return h2, dict(u=u, n2=n2, pooled=pooled, o_gla=o_gla, states=states, y=y)


def _fwd_ffn2_loss(h2, tgt, w, after=()):
    dh3, n3, gu3, d_final, loss = _ffn_fwd(h2, w["ffn2_norm"], w["ffn2_w_in"], w["ffn2_w_out"], "ffn2_fwd", after,
                                           head=(w["final_norm"], tgt))
    return dh3, d_final, loss, n3, gu3


def _backward(x, h1, h2, n1, gu1, sv, n3, gu3, dh3, w, at=lambda point, value: ()):
    g = {}
    dgu3, act3, df3, dh2, dh2b, g["ffn2_norm"] = _ffn_bwd_x(dh3, h2, w["ffn2_norm"], gu3, w["ffn2_w_in"], w["ffn2_w_out"], "ffn2_bwd")
    g["ffn2_w_out"] = _ffn_dw_out(act3, df3, "ffn2")
    g["ffn2_w_in"] = _ffn_dw_in(n3, dgu3, "ffn2", at("ffn2_w_out", g["ffn2_w_out"]))
    dy = _nt_matmul(dh2b, w["w_out_mix"], "mix_out_bwd", at("ffn2_w_in", g["ffn2_w_in"]))
    g["w_out_mix"] = _tn_matmul(sv["y"], dh2b, pl.BlockSpec((S, 512), lambda m: (0, m)), pl.BlockSpec((S, D), lambda m: (0, 0)),
                                jax.ShapeDtypeStruct((D, D), BF16), pl.BlockSpec((512, D), lambda m: (m, 0)), (D // 512,),
                                "mix_out_dw", 40)
    du_pool, g["w_pool"], g["pool_scale"] = _pool_bwd(dy, sv["pooled"], w["w_pool"], w["pool_scale"])
    du_gla, g["w_alpha"], g["b_alpha"], g["gla_norm"] = _gla_bwd(sv["u"], sv["o_gla"], sv["states"], dy, w["w_alpha"], w["b_alpha"],
                                                                 w["gla_norm"], at("pool_bwd", du_pool))
    du = jnp.concatenate([du_pool, du_gla], axis=1)
    g["w_in_mix"] = _tn_matmul(du, sv["n2"], pl.BlockSpec((S, 1408), lambda j, m: (0, j)), pl.BlockSpec((S, 512), lambda j, m: (0, m)),
                               jax.ShapeDtypeStruct((D_IN, D), BF16), pl.BlockSpec((1408, 512), lambda j, m: (j, m)),
                               (D_IN_PAD // 1408, D // 512), "mix_in_dw", 32)
    dh1, g["mix_norm"], df1 = _matmul_normbwd(du, w["w_in_mix"], w["w_in_mix_tail"], h1, w["mix_norm"], dh2, "mix_in_bwd",
                                              at("mixer_weights", g))
    dgu1, g["ffn1_w_out"] = _ffn_bwd_gu(df1, gu1, w["ffn1_w_out"], "ffn1_bwd_gu", at("dh1", dh1))
    half = S // FFN_TS // 2
    dx, dn_a = _ffn_bwd_dx(dgu1, dh1, x, w["ffn1_norm"], w["ffn1_w_in"], 0, half, "ffn1_bwd_dx_a",
                           at("ffn1_w_out", g["ffn1_w_out"]))
    g["ffn1_w_in_top"] = _ffn_dw_in(n1, dgu1, "ffn1_top", at("ffn1_dx_a", dx), 0, D // 1024)
    g["ffn1_w_in_bottom"] = _ffn_dw_in(n1, dgu1, "ffn1_bottom", at("ffn1_w_in_top", g["ffn1_w_in_top"]), D // 1024, D // 1024)
    dx, dn_b = _ffn_bwd_dx(dgu1, dh1, x, w["ffn1_norm"], w["ffn1_w_in"], half, half, "ffn1_bwd_dx_b",
                           at("ffn1_w_in_bottom", g["ffn1_w_in_bottom"]), into=dx)
    g["ffn1_norm"] = dn_a + dn_b
    return dx, g


def _local_step(x, tgt, w):
    h1, n1, gu1 = _fwd_ffn1(x, w)
    h2, sv = _fwd_mixer(h1, w)
    dh3, d_final, loss, n3, gu3 = _fwd_ffn2_loss(h2, tgt, w)
    dx, g = _backward(x, h1, h2, n1, gu1, sv, n3, gu3, dh3, w)
    return loss, dx, dict(final_norm=d_final, **g)


def _coords(p):
    return (p // 4, (p // 2) % 2, p % 2)


NCHIP = 4


def _place():
    return lax.axis_index("x"), lax.axis_index("y"), lax.axis_index("c")


def _rel_chip(x, y, rel):
    return ((1 - x) if rel & 1 else x, (1 - y) if rel & 2 else y)


def _dev_index(x, y, c):
    return 4 * x + 2 * y + c


def _cols(ref, p, width):
    return ref.at[:, pl.ds(pl.multiple_of(p * width, 128), width)]


def _sems(na, n):
    return [pltpu.SemaphoreType.DMA((na, n)), pltpu.SemaphoreType.DMA((na, n)), pltpu.SemaphoreType.DMA((na,))]


def _gather(items, name):
    arrays = [a for a, _ in items]
    kinds = [k for _, k in items]
    na = len(arrays)
    out_shape = [jax.ShapeDtypeStruct((NDEV,) + a.shape if k == "bcast" else (a.shape[0], NDEV * a.shape[1]), a.dtype)
                 for a, k in items]

    def body(*refs):
        ins, outs = refs[:na], refs[na:2 * na]
        send_sems, recv_sems, local_sems = refs[2 * na:]
        x, y, c = _place()
        sibling = (x, y, 1 - c)
        here, over_x, over_y, across = (x, y), (1 - x, y), (x, 1 - y), (1 - x, 1 - y)

        def half(ref, h):
            rows = ref.shape[0] // 2
            return ref.at[pl.ds(h * rows, rows), :]

        def slab(a, chip, core, h=None):
            ref = _slab(outs[a], kinds[a], _dev_index(*chip, core), ins[a].shape[1])
            return ref if h is None else half(ref, h)

        def copy(a, k, src, dst, to):
            return pltpu.make_async_remote_copy(src, dst, send_sems.at[a, k], recv_sems.at[a, k], device_id=to, device_id_type=MESH)

        sent = []

        def send(a, k, src, dst, to):
            sent.append(copy(a, k, src, dst, to))
            sent[-1].start()

        def arrived(a, k, chip, core, h=None):
            ref = slab(a, chip, core, h)
            copy(a, k, ref, ref, sibling).wait_recv()
            return ref

        local = [pltpu.make_async_copy(ins[a], slab(a, here, c), local_sems.at[a]) for a in range(na)]
        for cp in local:
            cp.start()
        for k, h, chip in ((1, 0, over_x), (4, 1, over_y), (2, 1, over_x), (5, 0, over_y)):
            for a in range(na):
                send(a, k, half(ins[a], h), slab(a, here, c, h), (*chip, c))
        for a in range(na):
            send(a, 0, ins[a], slab(a, here, c), sibling)
        for k, chip, h, onward, to, down in ((1, over_x, 0, 6, over_y, 7), (4, over_y, 1, 3, over_x, 10),
                                             (2, over_x, 1, None, None, 8), (5, over_y, 0, None, None, 9),
                                             (3, across, 1, None, None, 12), (6, across, 0, None, None, 11)):
            for a in range(na):
                ref = arrived(a, k, chip, c, h)
                if onward is not None:
                    send(a, onward, ref, ref, (*to, c))
                send(a, down, ref, ref, sibling)
        for a in range(na):
            arrived(a, 0, here, 1 - c)
        for k, chip, h in ((7, over_x, 0), (10, over_y, 1), (8, over_x, 1), (9, over_y, 0), (12, across, 1), (11, across, 0)):
            for a in range(na):
                arrived(a, k, chip, 1 - c, h)
        for cp in sent:
            cp.wait_send()
        for cp in local:
            cp.wait()

    return pl.pallas_call(body, name=name, in_specs=[ANY] * na, out_specs=[ANY] * na, out_shape=out_shape,
                          scratch_shapes=_sems(na, 13))(*arrays)


def _pair_add(own, kind, got, table, tr, name):
    _, rows, cols = got.shape
    if kind == "scatter":
        own_spec = pl.BlockSpec((None, tr, cols), lambda rel, i, t: (t[rel], i, 0))
    else:
        own_spec = pl.BlockSpec((tr, cols), lambda rel, i, t: (i, t[rel]))
    blk = pl.BlockSpec((None, tr, cols), lambda rel, i, t: (rel, i, 0))

    def body(t_ref, a_ref, b_ref, o_ref):
        o_ref[...] = (a_ref[...].astype(F32) + b_ref[...].astype(F32)).astype(o_ref.dtype)

    return _pallas(
        body, name=name, out_shape=jax.ShapeDtypeStruct(got.shape, got.dtype),
        grid_spec=pltpu.PrefetchScalarGridSpec(num_scalar_prefetch=1, grid=(NCHIP, rows // tr), in_specs=[own_spec, blk],
                                               out_specs=blk),
        compiler_params=_params(32),
    )(table, own, got)


HBM = pl.BlockSpec(memory_space=pltpu.HBM)
SEM = pl.BlockSpec(memory_space=pltpu.SEMAPHORE)
DATAFLOW = pltpu.SideEffectType.DATAFLOW_SIDE_EFFECTING


def _pair_copies(kinds):
    def describe(srcs, lands, send_sems, recv_sems):
        x, y, c = _place()
        na = len(srcs)
        for rel in range(NCHIP):
            p = _dev_index(*_rel_chip(x, y, rel), 1 - c)
            for a in range(na):
                src = srcs[a].at[p] if kinds[a] == "scatter" else _cols(srcs[a], p, srcs[a].shape[1] // NDEV)
                cp = pltpu.make_async_remote_copy(src, lands[a].at[rel], send_sems.at[rel * na + a], recv_sems.at[rel * na + a],
                                                  device_id=(x, y, 1 - c), device_id_type=MESH)
                yield cp, cp
    return describe


def _chip_copies(srcs, lands, send_sems, recv_sems):
    x, y, c = _place()
    na = len(srcs)
    for rel in range(1, NCHIP):
        for a in range(na):
            i = (rel - 1) * na + a
            cp = pltpu.make_async_remote_copy(srcs[a].at[rel], lands[a].at[rel], send_sems.at[i], recv_sems.at[i],
                                              device_id=(*_rel_chip(x, y, rel), c), device_id_type=MESH)
            yield cp, cp


def _slab(ref, kind, s, width):
    return _cols(ref, s, width) if kind == "bcast_cols" else ref.at[s]


def _all_copies(srcs, lands, send_sems, recv_sems):
    x, y, c = _place()
    na = len(srcs)
    me = _dev_index(x, y, c)
    for a in range(na):
        yield pltpu.make_async_copy(srcs[a], lands[a].at[me], send_sems.at[a]), None
    for k in range(1, NDEV):
        to, frm = (me + k) % NDEV, (me + NDEV - k) % NDEV
        for a in range(na):
            i = k * na + a
            send = pltpu.make_async_remote_copy(srcs[a], lands[a].at[me], send_sems.at[i], recv_sems.at[i],
                                                device_id=_coords(to), device_id_type=MESH)
            arrival = pltpu.make_async_remote_copy(srcs[a], lands[a].at[frm], send_sems.at[i], recv_sems.at[i],
                                                   device_id=_coords(to), device_id_type=MESH)
            yield send, arrival


def _gather_copies(kinds):
    def describe(srcs, lands, send_sems, recv_sems):
        x, y, c = _place()
        na = len(srcs)
        me = _dev_index(x, y, c)
        for a in range(na):
            yield pltpu.make_async_copy(srcs[a], _slab(lands[a], kinds[a], me, srcs[a].shape[-1]),
                                        send_sems.at[NCHIP * na + a]), None
        for rel in range(NCHIP):
            to = (x, y, 1 - c) if rel == 0 else (*_rel_chip(x, y, rel), c)
            for a in range(na):
                width = srcs[a].shape[-1]
                i = rel * na + a
                send = pltpu.make_async_remote_copy(srcs[a], _slab(lands[a], kinds[a], me, width), send_sems.at[i], recv_sems.at[i],
                                                    device_id=to, device_id_type=MESH)
                arrival = pltpu.make_async_remote_copy(srcs[a], _slab(lands[a], kinds[a], _dev_index(*to), width), send_sems.at[i],
                                                       recv_sems.at[i], device_id=to, device_id_type=MESH)
                yield send, arrival
    return describe


def _relay_copies(kinds, widths, phase):
    def describe(srcs, lands, send_sems, recv_sems):
        x, y, c = _place()
        na = len(lands)
        here, over_x, over_y, across = (x, y), (1 - x, y), (x, 1 - y), (1 - x, 1 - y)
        sibling = (x, y, 1 - c)

        def half(ref, h):
            cut = ref.shape[0] // 2 // 16 * 16
            if h is None or (h == 0 and cut == 0):
                return ref
            if cut == 0:
                return None
            return ref.at[pl.ds(0, cut), :] if h == 0 else ref.at[pl.ds(cut, ref.shape[0] - cut), :]

        def slab(a, chip, core, h=None):
            return half(_slab(lands[a], kinds[a], _dev_index(*chip, core), widths[a]), h)

        def pair(a, k, src, dst, to, arrival):
            i = k * na + a
            return (pltpu.make_async_remote_copy(src, dst, send_sems.at[i], recv_sems.at[i], device_id=to, device_id_type=MESH),
                    pltpu.make_async_remote_copy(arrival, arrival, send_sems.at[i], recv_sems.at[i], device_id=to,
                                                 device_id_type=MESH))

        def pairs(a, k, src, dst, to, arrival):
            return [] if src is None else [pair(a, k, src, dst, to, arrival)]

        if phase == 1:
            for a in range(na):
                yield pltpu.make_async_copy(srcs[a], slab(a, here, c), send_sems.at[a]), None
            for k, h, chip in ((1, 0, over_x), (2, 1, over_y), (3, 1, over_x), (4, 0, over_y)):
                for a in range(na):
                    yield from pairs(a, k, half(srcs[a], h), slab(a, here, c, h), (*chip, c), slab(a, chip, c, h))
            for a in range(na):
                yield pair(a, 5, srcs[a], slab(a, here, c), sibling, slab(a, here, 1 - c))
        elif phase == 2:
            for a in range(na):
                yield from pairs(a, 0, slab(a, over_x, c, 0), slab(a, over_x, c, 0), (*over_y, c), slab(a, across, c, 0))
                yield from pairs(a, 1, slab(a, over_y, c, 1), slab(a, over_y, c, 1), (*over_x, c), slab(a, across, c, 1))
            for k, (chip, h) in enumerate(((over_x, 0), (over_y, 1), (over_x, 1), (over_y, 0)), start=2):
                for a in range(na):
                    yield from pairs(a, k, slab(a, chip, c, h), slab(a, chip, c, h), sibling, slab(a, chip, 1 - c, h))
        else:
            for h in (0, 1):
                for a in range(na):
                    yield from pairs(a, h, slab(a, across, c, h), slab(a, across, c, h), sibling, slab(a, across, 1 - c, h))
    return describe


RELAY_COPIES = {1: 6, 2: 6, 3: 2}


def _relay_finish(arrays, kinds, widths, name):
    na = len(arrays)
    describe = _relay_copies(kinds, widths, 3)

    def body(*refs):
        copies = list(describe((), refs[na:2 * na], refs[2 * na], refs[2 * na + 1]))
        for send, _ in copies:
            send.start()
        for send, arrival in copies:
            send.wait_send()
            arrival.wait_recv()

    return pl.pallas_call(body, name=name, in_specs=[ANY] * na, out_specs=[ANY] * na,
                          out_shape=[jax.ShapeDtypeStruct(a.shape, a.dtype) for a in arrays],
                          input_output_aliases={i: i for i in range(na)},
                          scratch_shapes=[pltpu.SemaphoreType.DMA((RELAY_COPIES[3] * na,))] * 2)(*arrays)


def _pass_to_sibling(arrays, kinds, widths, name):
    na = len(arrays)

    def body(*refs):
        bufs = refs[na:2 * na]
        send_sems, recv_sems = refs[2 * na:]
        x, y, c = _place()
        copies = []
        for rel in range(1, NCHIP):
            for a in range(na):
                mine = _slab(bufs[a], kinds[a], _dev_index(*_rel_chip(x, y, rel), c), widths[a])
                theirs = _slab(bufs[a], kinds[a], _dev_index(*_rel_chip(x, y, rel), 1 - c), widths[a])
                send = pltpu.make_async_remote_copy(mine, mine, send_sems.at[a, rel], recv_sems.at[a, rel],
                                                    device_id=(x, y, 1 - c), device_id_type=MESH)
                send.start()
                copies.append((send, pltpu.make_async_remote_copy(theirs, theirs, send_sems.at[a, rel], recv_sems.at[a, rel],
                                                                  device_id=(x, y, 1 - c), device_id_type=MESH)))
        for send, arrival in copies:
            send.wait_send()
            arrival.wait_recv()

    return pl.pallas_call(body, name=name, in_specs=[ANY] * na, out_specs=[ANY] * na,
                          out_shape=[jax.ShapeDtypeStruct(a.shape, a.dtype) for a in arrays],
                          input_output_aliases={i: i for i in range(na)}, scratch_shapes=_sems(na, NCHIP)[:2])(*arrays)


def _start_copies(name, srcs, lands, describe, ncopies, after=()):
    arrays = list(srcs) + list(lands)
    ns, n, nin = len(srcs), len(arrays), len(arrays) + len(after)

    def body(*refs):
        for send, _ in describe(refs[:ns], refs[ns:n], refs[nin], refs[nin + 1]):
            send.start()
        refs[-1][...] = jnp.zeros_like(refs[-1])

    out = pl.pallas_call(
        body, name=name,
        out_shape=(pltpu.SemaphoreType.DMA((ncopies,)), pltpu.SemaphoreType.DMA((ncopies,)),
                   *[pltpu.HBM(a.shape, a.dtype) for a in arrays], jax.ShapeDtypeStruct((8, 128), F32)),
        in_specs=[HBM] * n + [ANY] * len(after), out_specs=(SEM, SEM, *[HBM] * n, pl.BlockSpec(memory_space=pltpu.VMEM)),
        input_output_aliases={i: 2 + i for i in range(n)},
        compiler_params=pltpu.CompilerParams(has_side_effects=DATAFLOW),
    )(*[pltpu.with_memory_space_constraint(a, pltpu.HBM) for a in arrays], *after)
    return out[0], out[1], list(out[2:2 + n]), out[-1]


def _wait_copies(name, send_sems, recv_sems, thru, ns, describe, after):
    n = len(thru)

    def body(*refs):
        for send, arrival in describe(refs[:ns], refs[ns:n], refs[n], refs[n + 1]):
            if arrival is None:
                send.wait()
            else:
                send.wait_send()
                arrival.wait_recv()

    out = pl.pallas_call(
        body, name=name, out_shape=tuple(pltpu.HBM(a.shape, a.dtype) for a in thru),
        in_specs=[HBM] * n + [SEM, SEM] + [ANY] * len(after), out_specs=tuple([HBM] * n),
        input_output_aliases={i: i for i in range(n)},
        compiler_params=pltpu.CompilerParams(has_side_effects=DATAFLOW),
    )(*thru, send_sems, recv_sems, *after)
    return list(out[:ns]), list(out[ns:])


def _adamw(parts, w, m, v, tr, name, tc=None, first_row=0, into=None):
    rows, cols = w.shape
    part_rows = parts[0][0].shape[1]
    tc = cols if tc is None else tc
    nparts = len(parts)
    tile0 = first_row // tr
    blk = pl.BlockSpec((tr, tc), lambda i, j: (tile0 + i, j))
    carried = [] if into is None else list(into)

    def slab_spec(s):
        return pl.BlockSpec((None, tr, tc), lambda i, j: (s, i, j))

    def body(*refs):
        p_refs = refs[:nparts]
        w_ref, m_ref, v_ref = refs[nparts:nparts + 3]
        g_ref, d_ref, nm_ref, nv_ref = refs[nparts + 3 + len(carried):]
        g = p_refs[0][...].astype(F32)
        for p_ref in p_refs[1:]:
            g = g + p_ref[...].astype(F32)
        nm = ADAM_B1 * m_ref[...] + (1.0 - ADAM_B1) * g
        nv = ADAM_B2 * v_ref[...] + (1.0 - ADAM_B2) * (g * g)
        m_hat = nm / (1.0 - ADAM_B1 ** ADAM_STEP)
        v_hat = nv / (1.0 - ADAM_B2 ** ADAM_STEP)
        g_ref[...] = g
        d_ref[...] = -ADAM_LR * (m_hat / (jnp.sqrt(v_hat) + ADAM_EPS) + ADAM_WD * w_ref[...])
        nm_ref[...] = nm
        nv_ref[...] = nv

    return _pallas(
        body, grid=(part_rows // tr, cols // tc), name=name,
        in_specs=[slab_spec(s) for _, s in parts] + [blk, blk, blk] + [ANY] * len(carried), out_specs=[blk] * 4,
        out_shape=[jax.ShapeDtypeStruct((rows, cols), F32)] * 4,
        input_output_aliases={nparts + 3 + k: k for k in range(len(carried))},
        compiler_params=_params(40),
    )(*[a for a, _ in parts], w, m, v, *carried)


def _pack_small(vals, extra=None):
    flat = [vals[n].reshape(-1).astype(F32) for n, _ in SMALL]
    tail = jnp.zeros((SMALL_ROWS * 128 - LOSS_AT,), F32)
    if extra is not None:
        tail = tail.at[0].set(extra)
    return jnp.concatenate(flat + [tail]).reshape(SMALL_ROWS, 128)


def _unpack_small(packed, like):
    flat, out, at = packed.reshape(-1), {}, 0
    for n, size in SMALL:
        out[n] = flat[at:at + size].reshape(like[n].shape)
        at += size
    return out, flat[LOSS_AT]


def kernel(x, ffn1_norm, ffn1_w_in, ffn1_w_out, mix_norm, w_in_mix, w_pool, pool_scale, w_alpha, b_alpha, gla_norm, w_out_mix, ffn2_norm, ffn2_w_in, ffn2_w_out, final_norm, loss_target, m_ffn1_norm, m_ffn1_w_in, m_ffn1_w_out, m_mix_norm, m_w_in_mix, m_w_pool, m_pool_scale, m_w_alpha, m_b_alpha, m_gla_norm, m_w_out_mix, m_ffn2_norm, m_ffn2_w_in, m_ffn2_w_out, m_final_norm, v_ffn1_norm, v_ffn1_w_in, v_ffn1_w_out, v_mix_norm, v_w_in_mix, v_w_pool, v_pool_scale, v_w_alpha, v_b_alpha, v_gla_norm, v_w_out_mix, v_ffn2_norm, v_ffn2_w_in, v_ffn2_w_out, v_final_norm):
    names = ["ffn1_norm", "ffn1_w_in", "ffn1_w_out", "mix_norm", "w_in_mix", "w_pool", "pool_scale", "w_alpha", "b_alpha",
             "gla_norm", "w_out_mix", "ffn2_norm", "ffn2_w_in", "ffn2_w_out", "final_norm"]
    p = dict(zip(names, [ffn1_norm, ffn1_w_in, ffn1_w_out, mix_norm, w_in_mix, w_pool, pool_scale, w_alpha, b_alpha,
                         gla_norm, w_out_mix, ffn2_norm, ffn2_w_in, ffn2_w_out, final_norm]))
    m = dict(zip(names, [m_ffn1_norm, m_ffn1_w_in, m_ffn1_w_out, m_mix_norm, m_w_in_mix, m_w_pool, m_pool_scale, m_w_alpha,
                         m_b_alpha, m_gla_norm, m_w_out_mix, m_ffn2_norm, m_ffn2_w_in, m_ffn2_w_out, m_final_norm]))
    v = dict(zip(names, [v_ffn1_norm, v_ffn1_w_in, v_ffn1_w_out, v_mix_norm, v_w_in_mix, v_w_pool, v_pool_scale, v_w_alpha,
                         v_b_alpha, v_gla_norm, v_w_out_mix, v_ffn2_norm, v_ffn2_w_in, v_ffn2_w_out, v_final_norm]))

    mx, my, mc = _place()
    table = jnp.stack([_dev_index(*_rel_chip(mx, my, rel), mc) for rel in range(NCHIP)]).astype(jnp.int32)

    def landing(shard, kind):
        shape = (shard.shape[0], NDEV * shard.shape[1]) if kind == "bcast_cols" else (NDEV,) + shard.shape
        return lax.empty(shape, shard.dtype)

    def gather_begin(items, tag, after):
        kinds = [kind for _, kind in items]
        copies = _gather_copies(kinds)
        s, r, thru, tok = _start_copies(tag + "_start", [a for a, _ in items], [landing(a, kind) for a, kind in items], copies,
                                        (NCHIP + 1) * len(items), after)
        return (s, r, thru, copies, kinds, [a.shape[-1] for a, _ in items], tag), tok

    def gather_end(state, after):
        s, r, thru, copies, kinds, widths, tag = state
        _, lands = _wait_copies(tag + "_wait", s, r, thru, len(kinds), copies, after)
        return _pass_to_sibling(lands, kinds, widths, tag + "_pass")

    def shard16(n):
        return p[n][0].astype(BF16)

    mix_items = [(jnp.transpose(p["w_in_mix"][0]).astype(BF16), "bcast"), (shard16("w_out_mix"), "bcast"),
                 (p["w_pool"][0].reshape(H * 32, PG), "bcast"), (p["w_alpha"][0], "bcast")]
    mix_kinds = [kind for _, kind in mix_items]
    mix_widths = [a.shape[-1] for a, _ in mix_items]
    mix_relay = {ph: _relay_copies(mix_kinds, mix_widths, ph) for ph in (1, 2)}
    m_s, m_r, m_thru, tok_m1 = _start_copies("gather_mix_start", [a for a, _ in mix_items],
                                             [landing(a, kind) for a, kind in mix_items], mix_relay[1],
                                             RELAY_COPIES[1] * len(mix_items))
    started = tok_m1[0, 0].astype(BF16)
    g_w1in, g_w1out = _gather([(shard16("ffn1_w_in") + started, "bcast_cols"), (shard16("ffn1_w_out"), "bcast")], "gather_ffn1")
    _, mix_lands = _wait_copies("gather_mix_wait", m_s, m_r, m_thru, len(mix_items), mix_relay[1], (g_w1out,))
    m_s, m_r, m_thru, tok_m = _start_copies("gather_mix_relay_start", [], mix_lands, mix_relay[2],
                                            RELAY_COPIES[2] * len(mix_items))
    ffn2_items = [(shard16("ffn2_w_in"), "bcast_cols"), (shard16("ffn2_w_out"), "bcast")]
    ffn2_kinds = [kind for _, kind in ffn2_items]
    ffn2_widths = [a.shape[-1] for a, _ in ffn2_items]
    relay = {ph: _relay_copies(ffn2_kinds, ffn2_widths, ph) for ph in (1, 2)}
    f_s, f_r, f_thru, tok_f = _start_copies("gather_ffn2_start", [a for a, _ in ffn2_items],
                                            [landing(a, kind) for a, kind in ffn2_items], relay[1],
                                            RELAY_COPIES[1] * len(ffn2_items), (tok_m,))
    ffn2_lands = {}

    def relay_ffn2(point, value):
        _, lands = _wait_copies("gather_ffn2_wait", f_s, f_r, f_thru, len(ffn2_items), relay[1], (value,))
        ffn2_lands["s"], ffn2_lands["r"], ffn2_lands["thru"], tok = _start_copies(
            "gather_ffn2_relay_start", [], lands, relay[2], RELAY_COPIES[2] * len(ffn2_items))
        return (tok,)

    full ={"ffn1_w_in": g_w1in, "ffn1_w_out": g_w1out.reshape(FF, D), "final_norm": final_norm.reshape(1, D)}
    for n in ("ffn1_norm", "mix_norm", "ffn2_norm", "pool_scale", "b_alpha", "gla_norm"):
        full[n] = p[n]

    xs, tgt = x[0], loss_target[0]
    h1, n1, gu1 = _fwd_ffn1(xs, full, after=(tok_m, tok_f))
    _, mix_lands = _wait_copies("gather_mix_relay_wait", m_s, m_r, m_thru, 0, mix_relay[2], (h1,))
    g_wmix, g_wo, g_wpool, g_walpha = _relay_finish(mix_lands, mix_kinds, mix_widths, "gather_mix_finish")
    walpha = jnp.transpose(g_walpha, (1, 0, 2)).reshape(RANK, H * DK)
    full.update({
        "w_in_mix": g_wmix.reshape(D_IN, D),
        "w_in_mix_tail": jnp.pad(g_wmix[NDEV - 1, MIX_SHARD - RANK:], ((0, RANK_PAD - RANK), (0, 0))),
        "w_out_mix": g_wo.reshape(D, D),
        "w_pool": jnp.transpose(g_wpool.reshape(NDEV, H, 32, PG), (1, 0, 2, 3)).reshape(H, PG, PG).astype(BF16),
        "w_alpha": jnp.pad(walpha, ((0, RANK_PAD - RANK), (0, 0))).astype(BF16),
    })
    h2, sv = _fwd_mixer(h1, full, at=relay_ffn2)
    _, lands = _wait_copies("gather_ffn2_relay_wait", ffn2_lands["s"], ffn2_lands["r"], ffn2_lands["thru"], 0, relay[2], (h2,))
    g_w2in, g_w2out = _relay_finish(lands, ffn2_kinds, ffn2_widths, "gather_ffn2_finish")
    full.update({"ffn2_w_in": g_w2in, "ffn2_w_out": g_w2out.reshape(FF, D)})
    dh3, d_final, loss_part, n3, gu3 = _fwd_ffn2_loss(h2, tgt, full)


    def slab_shape(a, kind):
        return (NCHIP,) + (a.shape[1:] if kind == "scatter" else (a.shape[0], a.shape[1] // NDEV))

    def pair_add_all(own, got, tag):
        return [_pair_add(a, kind, got_a, table, tr, "%s_pair_add_%d" % (tag, i))
                for i, ((a, kind, tr), got_a) in enumerate(zip(own, got))]

    def reduce_begin(own, tag, after=()):
        kinds = [kind for _, kind, _ in own]
        copies = _pair_copies(kinds)
        s, r, thru, tok = _start_copies(tag + "_pair_start", [a for a, _, _ in own],
                                        [lax.empty(slab_shape(a, kind), a.dtype) for a, kind, _ in own], copies,
                                        NCHIP * len(own), after)
        return dict(own=own, copies=copies, s=s, r=r, thru=thru, tag=tag), tok

    def reduce_middle(st, after):
        own, tag = st["own"], st["tag"]
        sent, got = _wait_copies(tag + "_pair_wait", st["s"], st["r"], st["thru"], len(own), st["copies"], after)
        pre = pair_add_all([(a, kind, tr) for a, (_, kind, tr) in zip(sent, own)], got, tag)
        st["s"], st["r"], st["thru"], tok = _start_copies(tag + "_chip_start", pre, [lax.empty(a.shape, a.dtype) for a in pre],
                                                          _chip_copies, (NCHIP - 1) * len(pre))
        return tok

    def reduce_end(st, after):
        n = len(st["own"])
        pre, land = _wait_copies(st["tag"] + "_chip_wait", st["s"], st["r"], st["thru"], n, _chip_copies, after)
        return [[(a, 0)] + [(b, rel) for rel in range(1, NCHIP)] for a, b in zip(pre, land)]

    def w_in_item(a):
        return (a, "scatter_cols", 512)

    def w_out_item(a):
        return (a.reshape(NDEV, WOUT_SHARD, D), "scatter", WOUT_SHARD)

    red, small = {}, {}

    def at(point, value):
        if point == "ffn2_w_out":
            red["w2out"], tok = reduce_begin([w_out_item(value)], "ffn2_w_out")
        elif point == "ffn2_w_in":
            tok_a = reduce_middle(red["w2out"], (value,))
            red["w2in"], tok = reduce_begin([w_in_item(value)], "ffn2_w_in", (tok_a,))
        elif point == "pool_bwd":
            tok = reduce_middle(red["w2in"], (value,))
        elif point == "mixer_weights":
            d_wmix8 = value["w_in_mix"].reshape(NDEV, MIX_SHARD, D)
            d_wpool8 = jnp.transpose(value["w_pool"].reshape(H, NDEV, 32, PG), (1, 0, 2, 3)).reshape(NDEV, H * 32, PG)
            d_walpha8 = jnp.transpose(value["w_alpha"][:RANK].reshape(RANK, NDEV, H * DK // NDEV), (1, 0, 2))
            red["mix"], tok = reduce_begin([(d_wmix8, "scatter", MIX_SHARD),
                                            (value["w_out_mix"].reshape(NDEV, D // NDEV, D), "scatter", D // NDEV),
                                            (d_wpool8, "scatter", H * 32), (d_walpha8, "scatter", RANK)], "mix")
        elif point == "dh1":
            tok = reduce_middle(red["mix"], (value,))
        elif point == "ffn1_w_out":
            red["w1out"], tok = reduce_begin([w_out_item(value)], "ffn1_w_out")
        elif point == "ffn1_dx_a":
            tok = reduce_middle(red["w1out"], (value,))
        elif point == "ffn1_w_in_top":
            red["w1in_top"], tok = reduce_begin([w_in_item(value)], "ffn1_w_in_top")
        elif point == "ffn1_w_in_bottom":
            tok_a = reduce_middle(red["w1in_top"], (value,))
            red["w1in_bottom"], tok = reduce_begin([w_in_item(value)], "ffn1_w_in_bottom", (tok_a,))
        return (tok,)

    dx, g = _backward(xs, h1, h2, n1, gu1, sv, n3, gu3, dh3, full, at)
    packed = _pack_small(dict(final_norm=d_final, **g), loss_part[0, 0])
    small_s, small_r, small_thru, tok_s = _start_copies("gather_small_start", [packed], [lax.empty((NDEV,) + packed.shape, F32)],
                                                        _all_copies, NDEV, (dx,))
    tok_c = reduce_middle(red["w1in_bottom"], (tok_s,))

    def upd(parts, n, shape2d, tr):
        res = _adamw(parts, p[n].reshape(shape2d), m[n].reshape(shape2d), v[n].reshape(shape2d), tr, "adamw_" + n)
        return [r.reshape(p[n].shape) for r in res]

    def transposed(a):
        return jnp.transpose(a[0])

    (p_w2out,) = reduce_end(red["w2out"], (tok_c,))
    (p_w2in,) = reduce_end(red["w2in"], (tok_c,))
    p_wmix, p_wo, p_wpool, p_walpha = reduce_end(red["mix"], (tok_c,))
    out = {
        "ffn2_w_in": upd(p_w2in, "ffn2_w_in", (D, WIN_SHARD), 256),
        "ffn2_w_out": upd(p_w2out, "ffn2_w_out", (WOUT_SHARD, D), WOUT_SHARD // 4),
        "w_out_mix": upd(p_wo, "w_out_mix", (D // NDEV, D), 64),
        "w_pool": upd(p_wpool, "w_pool", (H * 32, PG), H * 32),
        "w_alpha": upd(p_walpha, "w_alpha", (RANK, H * DK // NDEV), RANK),
    }
    out["w_in_mix"] = [jnp.transpose(r)[None] for r in
                       _adamw(p_wmix, transposed(p["w_in_mix"]), transposed(m["w_in_mix"]), transposed(v["w_in_mix"]),
                              MIX_SHARD, "adamw_w_in_mix", tc=512)]
    _, (r_small,) = _wait_copies("gather_small_wait", small_s, small_r, small_thru, 1, _all_copies,
                                 (out["w_in_mix"][3], out["ffn2_w_in"][3], out["ffn2_w_out"][3], out["w_out_mix"][3]))
    small_res = _adamw([(r_small, s) for s in range(NDEV)], _pack_small(p), _pack_small(m), _pack_small(v), SMALL_ROWS,
                       "adamw_small")
    (p_w1out,) = reduce_end(red["w1out"], (small_res[0],))
    out["ffn1_w_out"] = upd(p_w1out, "ffn1_w_out", (WOUT_SHARD, D), WOUT_SHARD // 4)
    w1in = [a.reshape(D, WIN_SHARD) for a in (p["ffn1_w_in"], m["ffn1_w_in"], v["ffn1_w_in"])]
    (p_top,) = reduce_end(red["w1in_top"], (out["ffn1_w_out"][3],))
    top = _adamw(p_top, *w1in, 256, "adamw_ffn1_w_in_top")
    (p_bottom,) = reduce_end(red["w1in_bottom"], (top[3],))
    out["ffn1_w_in"] = [r.reshape(p["ffn1_w_in"].shape) for r in
                        _adamw(p_bottom, *w1in, 256, "adamw_ffn1_w_in_bottom", first_row=D // 2, into=top)]
    unpacked = [_unpack_small(r, p) for r in small_res]
    loss = unpacked[0][1]
    for n, _ in SMALL:
        out[n] = [u[0][n] for u in unpacked]

    return (loss, dx.reshape(1, S, D), *[out[n][0] for n in names], *[out[n][1] for n in names],
            *[out[n][2] for n in names], *[out[n][3] for n in names])
```
